```python
import jax, jax.numpy as jnp
from jax import lax
import numpy as np

D_MODEL = 1024
BATCH = 32
SEQ = 2048
DEPTH = 2

MEM_LEN = 256
HEAD_DIM = 64
N_SB_HEADS = 12
N_MEM_HEADS = 4
DIL_GROUPS = ((128, 1), (512, 4), (2048, 16))
HEADS_PER_GROUP = 4
N_DIL_HEADS = HEADS_PER_GROUP * len(DIL_GROUPS)
SB_WIDTH = N_SB_HEADS * HEAD_DIM
MEM_WIDTH = N_MEM_HEADS * HEAD_DIM
DIL_WIDTH = N_DIL_HEADS * HEAD_DIM
D_FF = 2816
CONV_WIDTH = 3
Q_BLOCK = 128
N_A_LAYERS = DEPTH // 2
N_B_LAYERS = DEPTH - N_A_LAYERS
EPS = 1e-6
ALIBI_MAX_BIAS = 8.0

kernel_name = "yoco_stickbreaking_dilated_hybrid"


def rmsnorm(x, g):
    xf = x.astype(jnp.float32)
    y = xf * lax.rsqrt(jnp.mean(xf * xf, axis=-1, keepdims=True) + EPS)
    return (y * g.astype(jnp.float32)).astype(x.dtype)


def alibi_slopes(n):
    return 2.0 ** (-ALIBI_MAX_BIAS * jnp.arange(1, n + 1, dtype=jnp.float32) / n)


def _heads(t, n_heads):
    return t.reshape(t.shape[0], t.shape[1], n_heads, HEAD_DIM)


def stick_breaking_attention(q, k, v):
    s_len = q.shape[1]
    scale = HEAD_DIM ** -0.5
    outs = []
    for blk in range(s_len // Q_BLOCK):
        q0 = blk * Q_BLOCK
        k_end = q0 + Q_BLOCK
        z = jnp.einsum('bqhd,bkhd->bhqk', q[:, q0:k_end], k[:, :k_end]).astype(jnp.float32) * scale
        t_pos = q0 + jnp.arange(Q_BLOCK)[:, None]
        s_pos = jnp.arange(k_end)[None, :]
        causal = s_pos < t_pos
        log_stay = jnp.where(causal, -jax.nn.softplus(z), 0.0)
        later = lax.cumsum(log_stay, axis=3, reverse=True) - log_stay
        w = jnp.where(causal, jnp.exp(jax.nn.log_sigmoid(z) + later), 0.0)
        outs.append(jnp.einsum('bhqk,bkhd->bqhd', w.astype(v.dtype), v[:, :k_end]))
    return jnp.concatenate(outs, axis=1)


def dilated_window_attention(q, k, v, slopes, window, dilation):
    b, s_len, n_h, dh = q.shape
    w_sub = window // dilation
    blk = w_sub
    L = s_len // dilation
    nb = -(-L // blk)
    Lp = nb * blk

    def by_residue(t):
        return t.reshape(b, L, dilation, n_h, dh).transpose(0, 2, 1, 3, 4)

    qs, ks, vs = by_residue(q), by_residue(k), by_residue(v)
    qb = jnp.pad(qs, ((0, 0), (0, 0), (0, Lp - L), (0, 0), (0, 0))).reshape(b, dilation, nb, blk, n_h, dh)

    def key_blocks(t):
        tp = jnp.pad(t, ((0, 0), (0, 0), (blk, Lp - L), (0, 0), (0, 0)))
        prev = tp[:, :, :Lp].reshape(b, dilation, nb, blk, n_h, dh)
        cur = tp[:, :, blk:].reshape(b, dilation, nb, blk, n_h, dh)
        return jnp.concatenate([prev, cur], axis=3)

    kb, vb = key_blocks(ks), key_blocks(vs)
    sc = jnp.einsum('brnqhd,brnkhd->brnhqk', qb, kb).astype(jnp.float32) * dh ** -0.5
    n_idx = jnp.arange(nb)[:, None, None]
    i_idx = jnp.arange(blk)[None, :, None]
    j_idx = jnp.arange(2 * blk)[None, None, :]
    delta = i_idx + blk - j_idx
    valid = (delta >= 0) & (delta <= w_sub) & (n_idx * blk - blk + j_idx >= 0)
    bias = -slopes[None, :, None, None] * (delta * dilation).astype(jnp.float32)[:, None]
    sc = jnp.where(valid[:, None], sc + bias, -jnp.inf)
    m = jnp.max(sc, axis=-1, keepdims=True)
    p = jnp.exp(sc - m)
    denom = jnp.sum(p, axis=-1)
    o = jnp.einsum('brnhqk,brnkhd->brnqhd', p.astype(v.dtype), vb).astype(jnp.float32)
    o = o / jnp.moveaxis(denom, 3, 4)[..., None]
    lse = jnp.moveaxis(m[..., 0] + jnp.log(denom), 3, 4)

    def back(t):
        t = t.reshape((b, dilation, Lp) + t.shape[4:])[:, :, :L]
        t = jnp.swapaxes(t, 1, 2)
        return t.reshape((b, s_len) + t.shape[3:])

    return back(o).astype(q.dtype), back(lse)


def memory_branch(q_mem, mem, norm_mem, w_mem_kv):
    k_m, v_m = jnp.split(rmsnorm(mem, norm_mem) @ w_mem_kv, 2, axis=-1)
    q, k, v = _heads(q_mem, N_MEM_HEADS), _heads(k_m, N_MEM_HEADS), _heads(v_m, N_MEM_HEADS)
    sc = jnp.einsum('bqhd,bkhd->bhqk', q, k).astype(jnp.float32) * HEAD_DIM ** -0.5
    p = jax.nn.softmax(sc, axis=-1)
    o = jnp.einsum('bhqk,bkhd->bqhd', p.astype(v.dtype), v)
    return o.reshape(q_mem.shape)


def conv_ffn(x, w_up, w_conv, w_down):
    s_len = x.shape[1]
    u = x @ w_up
    up = jnp.pad(u, ((0, 0), (CONV_WIDTH - 1, 0), (0, 0)))
    c = w_conv[0] * up[:, 0:s_len]
    for j in range(1, CONV_WIDTH):
        c = c + w_conv[j] * up[:, j:j + s_len]
    a, g = jnp.split(c, 2, axis=-1)
    return (jax.nn.silu(g) * a) @ w_down


def self_decoder_layer(x, mem, norm_attn, w_in, w_out, norm_mem, w_mem_kv, norm_ffn, ffn_up, ffn_conv, ffn_down):
    b, s_len, _ = x.shape
    proj = rmsnorm(x, norm_attn) @ w_in
    q_sb, k_sb, v_sb, q_mem = jnp.split(proj, [SB_WIDTH, 2 * SB_WIDTH, 3 * SB_WIDTH], axis=-1)
    o_sb = stick_breaking_attention(_heads(q_sb, N_SB_HEADS), _heads(k_sb, N_SB_HEADS), _heads(v_sb, N_SB_HEADS))
    o_mem = memory_branch(q_mem, mem, norm_mem, w_mem_kv)
    x = x + jnp.concatenate([o_sb.reshape(b, s_len, SB_WIDTH), o_mem], axis=-1) @ w_out
    return x + conv_ffn(rmsnorm(x, norm_ffn), ffn_up, ffn_conv, ffn_down)


def cross_decoder_layer(x, k_sh, v_sh, mem, norm_attn, w_in, w_out, norm_mem, w_mem_kv, norm_ffn, ffn_up, ffn_conv, ffn_down):
    b, s_len, _ = x.shape
    proj = rmsnorm(x, norm_attn) @ w_in
    q_dil, q_mem = jnp.split(proj, [DIL_WIDTH], axis=-1)
    q_dil = _heads(q_dil, N_DIL_HEADS)
    slopes = alibi_slopes(N_DIL_HEADS)
    outs, lses = [], []
    for g, (window, dilation) in enumerate(DIL_GROUPS):
        hs = slice(g * HEADS_PER_GROUP, (g + 1) * HEADS_PER_GROUP)
        o_g, lse_g = dilated_window_attention(q_dil[:, :, hs], k_sh[:, :, hs], v_sh[:, :, hs], slopes[hs], window, dilation)
        outs.append(o_g)
        lses.append(lse_g)
    alpha = jax.nn.softmax(jnp.stack(lses, axis=0), axis=0)
    o_dil = jnp.concatenate([o * alpha[g][..., None].astype(o.dtype) for g, o in enumerate(outs)], axis=2)
    o_mem = memory_branch(q_mem, mem, norm_mem, w_mem_kv)
    x = x + jnp.concatenate([o_dil.reshape(b, s_len, DIL_WIDTH), o_mem], axis=-1) @ w_out
    return x + conv_ffn(rmsnorm(x, norm_ffn), ffn_up, ffn_conv, ffn_down)


def _fwd_setup_inputs(seed: int = 0) -> dict:
    key = jax.random.key(seed)
    ks = jax.random.split(key, 24)

    def w(k, shape, fan_in):
        return jax.random.normal(k, shape, jnp.float32) * fan_in ** -0.5

    def gain(k, shape):
        return 1.0 + 0.02 * jax.random.normal(k, shape, jnp.float32)

    na, nb = N_A_LAYERS, N_B_LAYERS
    return {
        'x': jax.random.normal(ks[0], (BATCH, SEQ, D_MODEL), jnp.float32),
        'mem': jax.random.normal(ks[1], (BATCH, MEM_LEN, D_MODEL), jnp.float32),
        'a_norm_attn': gain(ks[2], (na, D_MODEL)),
        'a_w_in': w(ks[3], (na, D_MODEL, 3 * SB_WIDTH + MEM_WIDTH), D_MODEL),
        'a_w_out': w(ks[4], (na, SB_WIDTH + MEM_WIDTH, D_MODEL), SB_WIDTH + MEM_WIDTH),
        'a_norm_mem': gain(ks[5], (na, D_MODEL)),
        'a_w_mem_kv': w(ks[6], (na, D_MODEL, 2 * MEM_WIDTH), D_MODEL),
        'a_norm_ffn': gain(ks[7], (na, D_MODEL)),
        'a_ffn_up': w(ks[8], (na, D_MODEL, 2 * D_FF), D_MODEL),
        'a_ffn_conv': w(ks[9], (na, CONV_WIDTH, 2 * D_FF), CONV_WIDTH),
        'a_ffn_down': w(ks[10], (na, D_FF, D_MODEL), D_FF),
        'kv_norm': gain(ks[11], (D_MODEL,)),
        'w_kv_shared': w(ks[12], (D_MODEL, 2 * DIL_WIDTH), D_MODEL),
        'b_norm_attn': gain(ks[13], (nb, D_MODEL)),
        'b_w_in': w(ks[14], (nb, D_MODEL, DIL_WIDTH + MEM_WIDTH), D_MODEL),
        'b_w_out': w(ks[15], (nb, DIL_WIDTH + MEM_WIDTH, D_MODEL), DIL_WIDTH + MEM_WIDTH),
        'b_norm_mem': gain(ks[16], (nb, D_MODEL)),
        'b_w_mem_kv': w(ks[17], (nb, D_MODEL, 2 * MEM_WIDTH), D_MODEL),
        'b_norm_ffn': gain(ks[18], (nb, D_MODEL)),
        'b_ffn_up': w(ks[19], (nb, D_MODEL, 2 * D_FF), D_MODEL),
        'b_ffn_conv': w(ks[20], (nb, CONV_WIDTH, 2 * D_FF), CONV_WIDTH),
        'b_ffn_down': w(ks[21], (nb, D_FF, D_MODEL), D_FF),
        'final_norm': gain(ks[22], (D_MODEL,)),
    }


def _fwd_reference(x, mem, a_norm_attn, a_w_in, a_w_out, a_norm_mem, a_w_mem_kv, a_norm_ffn, a_ffn_up, a_ffn_conv, a_ffn_down,
              kv_norm, w_kv_shared, b_norm_attn, b_w_in, b_w_out, b_norm_mem, b_w_mem_kv, b_norm_ffn, b_ffn_up, b_ffn_conv,
              b_ffn_down, final_norm):
    b, s_len, _ = x.shape
    h = x
    k_sh = None
    v_sh = None
    for layer in range(DEPTH):
        if layer < N_A_LAYERS:
            i = layer
            h = self_decoder_layer(h, mem, a_norm_attn[i], a_w_in[i], a_w_out[i], a_norm_mem[i], a_w_mem_kv[i],
                                   a_norm_ffn[i], a_ffn_up[i], a_ffn_conv[i], a_ffn_down[i])
            if layer == N_A_LAYERS - 1:
                k_flat, v_flat = jnp.split(rmsnorm(h, kv_norm) @ w_kv_shared, 2, axis=-1)
                k_sh = _heads(k_flat, N_DIL_HEADS)
                v_sh = _heads(v_flat, N_DIL_HEADS)
        else:
            j = layer - N_A_LAYERS
            h = cross_decoder_layer(h, k_sh, v_sh, mem, b_norm_attn[j], b_w_in[j], b_w_out[j], b_norm_mem[j],
                                    b_w_mem_kv[j], b_norm_ffn[j], b_ffn_up[j], b_ffn_conv[j], b_ffn_down[j])
    return rmsnorm(h, final_norm)


import jax as _jax
import jax.numpy as _jnp

TWIN_FORMAT = 'train_step'
FWD_PARAMS = ['x', 'mem', 'a_norm_attn', 'a_w_in', 'a_w_out', 'a_norm_mem', 'a_w_mem_kv', 'a_norm_ffn', 'a_ffn_up', 'a_ffn_conv', 'a_ffn_down', 'kv_norm', 'w_kv_shared', 'b_norm_attn', 'b_w_in', 'b_w_out', 'b_norm_mem', 'b_w_mem_kv', 'b_norm_ffn', 'b_ffn_up', 'b_ffn_conv', 'b_ffn_down', 'final_norm']
TWIN_WEIGHTS = ['a_norm_attn', 'a_w_in', 'a_w_out', 'a_norm_mem', 'a_w_mem_kv', 'a_norm_ffn', 'a_ffn_up', 'a_ffn_conv', 'a_ffn_down', 'kv_norm', 'w_kv_shared', 'b_norm_attn', 'b_w_in', 'b_w_out', 'b_norm_mem', 'b_w_mem_kv', 'b_norm_ffn', 'b_ffn_up', 'b_ffn_conv', 'b_ffn_down', 'final_norm']
TWIN_DIFF_INPUT = 'x'
TWIN_INPUTS = ['x', 'mem', 'a_norm_attn', 'a_w_in', 'a_w_out', 'a_norm_mem', 'a_w_mem_kv', 'a_norm_ffn', 'a_ffn_up', 'a_ffn_conv', 'a_ffn_down', 'kv_norm', 'w_kv_shared', 'b_norm_attn', 'b_w_in', 'b_w_out', 'b_norm_mem', 'b_w_mem_kv', 'b_norm_ffn', 'b_ffn_up', 'b_ffn_conv', 'b_ffn_down', 'final_norm', 'loss_target', 'm_a_norm_attn', 'm_a_w_in', 'm_a_w_out', 'm_a_norm_mem', 'm_a_w_mem_kv', 'm_a_norm_ffn', 'm_a_ffn_up', 'm_a_ffn_conv', 'm_a_ffn_down', 'm_kv_norm', 'm_w_kv_shared', 'm_b_norm_attn', 'm_b_w_in', 'm_b_w_out', 'm_b_norm_mem', 'm_b_w_mem_kv', 'm_b_norm_ffn', 'm_b_ffn_up', 'm_b_ffn_conv', 'm_b_ffn_down', 'm_final_norm', 'v_a_norm_attn', 'v_a_w_in', 'v_a_w_out', 'v_a_norm_mem', 'v_a_w_mem_kv', 'v_a_norm_ffn', 'v_a_ffn_up', 'v_a_ffn_conv', 'v_a_ffn_down', 'v_kv_norm', 'v_w_kv_shared', 'v_b_norm_attn', 'v_b_w_in', 'v_b_w_out', 'v_b_norm_mem', 'v_b_w_mem_kv', 'v_b_norm_ffn', 'v_b_ffn_up', 'v_b_ffn_conv', 'v_b_ffn_down', 'v_final_norm']
TWIN_OUTPUTS = ['loss', 'grad_x', 'grad_a_norm_attn', 'grad_a_w_in', 'grad_a_w_out', 'grad_a_norm_mem', 'grad_a_w_mem_kv', 'grad_a_norm_ffn', 'grad_a_ffn_up', 'grad_a_ffn_conv', 'grad_a_ffn_down', 'grad_kv_norm', 'grad_w_kv_shared', 'grad_b_norm_attn', 'grad_b_w_in', 'grad_b_w_out', 'grad_b_norm_mem', 'grad_b_w_mem_kv', 'grad_b_norm_ffn', 'grad_b_ffn_up', 'grad_b_ffn_conv', 'grad_b_ffn_down', 'grad_final_norm', 'delta_a_norm_attn', 'delta_a_w_in', 'delta_a_w_out', 'delta_a_norm_mem', 'delta_a_w_mem_kv', 'delta_a_norm_ffn', 'delta_a_ffn_up', 'delta_a_ffn_conv', 'delta_a_ffn_down', 'delta_kv_norm', 'delta_w_kv_shared', 'delta_b_norm_attn', 'delta_b_w_in', 'delta_b_w_out', 'delta_b_norm_mem', 'delta_b_w_mem_kv', 'delta_b_norm_ffn', 'delta_b_ffn_up', 'delta_b_ffn_conv', 'delta_b_ffn_down', 'delta_final_norm', 'new_m_a_norm_attn', 'new_m_a_w_in', 'new_m_a_w_out', 'new_m_a_norm_mem', 'new_m_a_w_mem_kv', 'new_m_a_norm_ffn', 'new_m_a_ffn_up', 'new_m_a_ffn_conv', 'new_m_a_ffn_down', 'new_m_kv_norm', 'new_m_w_kv_shared', 'new_m_b_norm_attn', 'new_m_b_w_in', 'new_m_b_w_out', 'new_m_b_norm_mem', 'new_m_b_w_mem_kv', 'new_m_b_norm_ffn', 'new_m_b_ffn_up', 'new_m_b_ffn_conv', 'new_m_b_ffn_down', 'new_m_final_norm', 'new_v_a_norm_attn', 'new_v_a_w_in', 'new_v_a_w_out', 'new_v_a_norm_mem', 'new_v_a_w_mem_kv', 'new_v_a_norm_ffn', 'new_v_a_ffn_up', 'new_v_a_ffn_conv', 'new_v_a_ffn_down', 'new_v_kv_norm', 'new_v_w_kv_shared', 'new_v_b_norm_attn', 'new_v_b_w_in', 'new_v_b_w_out', 'new_v_b_norm_mem', 'new_v_b_w_mem_kv', 'new_v_b_norm_ffn', 'new_v_b_ffn_up', 'new_v_b_ffn_conv', 'new_v_b_ffn_down', 'new_v_final_norm']
TWIN_LEAF_KINDS = {'loss': 'loss', 'grad_x': 'grad_x', 'grad_a_norm_attn': 'grad_w', 'grad_a_w_in': 'grad_w', 'grad_a_w_out': 'grad_w', 'grad_a_norm_mem': 'grad_w', 'grad_a_w_mem_kv': 'grad_w', 'grad_a_norm_ffn': 'grad_w', 'grad_a_ffn_up': 'grad_w', 'grad_a_ffn_conv': 'grad_w', 'grad_a_ffn_down': 'grad_w', 'grad_kv_norm': 'grad_w', 'grad_w_kv_shared': 'grad_w', 'grad_b_norm_attn': 'grad_w', 'grad_b_w_in': 'grad_w', 'grad_b_w_out': 'grad_w', 'grad_b_norm_mem': 'grad_w', 'grad_b_w_mem_kv': 'grad_w', 'grad_b_norm_ffn': 'grad_w', 'grad_b_ffn_up': 'grad_w', 'grad_b_ffn_conv': 'grad_w', 'grad_b_ffn_down': 'grad_w', 'grad_final_norm': 'grad_w', 'delta_a_norm_attn': 'delta_w', 'delta_a_w_in': 'delta_w', 'delta_a_w_out': 'delta_w', 'delta_a_norm_mem': 'delta_w', 'delta_a_w_mem_kv': 'delta_w', 'delta_a_norm_ffn': 'delta_w', 'delta_a_ffn_up': 'delta_w', 'delta_a_ffn_conv': 'delta_w', 'delta_a_ffn_down': 'delta_w', 'delta_kv_norm': 'delta_w', 'delta_w_kv_shared': 'delta_w', 'delta_b_norm_attn': 'delta_w', 'delta_b_w_in': 'delta_w', 'delta_b_w_out': 'delta_w', 'delta_b_norm_mem': 'delta_w', 'delta_b_w_mem_kv': 'delta_w', 'delta_b_norm_ffn': 'delta_w', 'delta_b_ffn_up': 'delta_w', 'delta_b_ffn_conv': 'delta_w', 'delta_b_ffn_down': 'delta_w', 'delta_final_norm': 'delta_w', 'new_m_a_norm_attn': 'new_m', 'new_m_a_w_in': 'new_m', 'new_m_a_w_out': 'new_m', 'new_m_a_norm_mem': 'new_m', 'new_m_a_w_mem_kv': 'new_m', 'new_m_a_norm_ffn': 'new_m', 'new_m_a_ffn_up': 'new_m', 'new_m_a_ffn_conv': 'new_m', 'new_m_a_ffn_down': 'new_m', 'new_m_kv_norm': 'new_m', 'new_m_w_kv_shared': 'new_m', 'new_m_b_norm_attn': 'new_m', 'new_m_b_w_in': 'new_m', 'new_m_b_w_out': 'new_m', 'new_m_b_norm_mem': 'new_m', 'new_m_b_w_mem_kv': 'new_m', 'new_m_b_norm_ffn': 'new_m', 'new_m_b_ffn_up': 'new_m', 'new_m_b_ffn_conv': 'new_m', 'new_m_b_ffn_down': 'new_m', 'new_m_final_norm': 'new_m', 'new_v_a_norm_attn': 'new_v', 'new_v_a_w_in': 'new_v', 'new_v_a_w_out': 'new_v', 'new_v_a_norm_mem': 'new_v', 'new_v_a_w_mem_kv': 'new_v', 'new_v_a_norm_ffn': 'new_v', 'new_v_a_ffn_up': 'new_v', 'new_v_a_ffn_conv': 'new_v', 'new_v_a_ffn_down': 'new_v', 'new_v_kv_norm': 'new_v', 'new_v_w_kv_shared': 'new_v', 'new_v_b_norm_attn': 'new_v', 'new_v_b_w_in': 'new_v', 'new_v_b_w_out': 'new_v', 'new_v_b_norm_mem': 'new_v', 'new_v_b_w_mem_kv': 'new_v', 'new_v_b_norm_ffn': 'new_v', 'new_v_b_ffn_up': 'new_v', 'new_v_b_ffn_conv': 'new_v', 'new_v_b_ffn_down': 'new_v', 'new_v_final_norm': 'new_v'}


def _forward(args):
    return _fwd_reference(*[args[k] for k in FWD_PARAMS])


def _output_shape():
    out = _jax.eval_shape(lambda: _forward(_fwd_setup_inputs(0)))
    return out.shape, out.dtype

N_MICROBATCH = 1
ADAM_LR = 0.001
ADAM_B1 = 0.9
ADAM_B2 = 0.999
ADAM_EPS = 1e-08
ADAM_WD = 0.01
ADAM_STEP = 10
PER_EXAMPLE_BATCH_AXIS = {'x': 0, 'mem': 0, 'loss_target': 0}
SHARED_INPUTS = []
_WEIGHT_DTYPES = {'a_norm_attn': _jnp.float32, 'a_w_in': _jnp.float32, 'a_w_out': _jnp.float32, 'a_norm_mem': _jnp.float32, 'a_w_mem_kv': _jnp.float32, 'a_norm_ffn': _jnp.float32, 'a_ffn_up': _jnp.float32, 'a_ffn_conv': _jnp.float32, 'a_ffn_down': _jnp.float32, 'kv_norm': _jnp.float32, 'w_kv_shared': _jnp.float32, 'b_norm_attn': _jnp.float32, 'b_w_in': _jnp.float32, 'b_w_out': _jnp.float32, 'b_norm_mem': _jnp.float32, 'b_w_mem_kv': _jnp.float32, 'b_norm_ffn': _jnp.float32, 'b_ffn_up': _jnp.float32, 'b_ffn_conv': _jnp.float32, 'b_ffn_down': _jnp.float32, 'final_norm': _jnp.float32}
MOMENT_SCALE = {'a_norm_attn': 1.928501e-01, 'a_w_in': 1.131350e-01, 'a_w_out': 1.498486e-01, 'a_norm_mem': 2.115805e-02, 'a_w_mem_kv': 2.833549e-02, 'a_norm_ffn': 1.907648e-01, 'a_ffn_up': 7.916228e-02, 'a_ffn_conv': 8.348345e-02, 'a_ffn_down': 1.292847e-01, 'kv_norm': 4.664560e-02, 'w_kv_shared': 3.677912e-02, 'b_norm_attn': 3.031442e-02, 'b_w_in': 3.060545e-02, 'b_w_out': 3.668341e-02, 'b_norm_mem': 1.672364e-02, 'b_w_mem_kv': 2.229250e-02, 'b_norm_ffn': 1.517375e-01, 'b_ffn_up': 6.482535e-02, 'b_ffn_conv': 6.441490e-02, 'b_ffn_down': 1.065162e-01, 'final_norm': 6.385089e+01}


def _to_microbatches(a, axis):
    t = _jnp.moveaxis(a, axis, 0)
    t = t.reshape((N_MICROBATCH, t.shape[0] // N_MICROBATCH) + t.shape[1:])
    return _jnp.moveaxis(t, 1, axis + 1)


def setup_inputs(seed: int = 0) -> dict:
    inp = _fwd_setup_inputs(seed)
    key = _jax.random.fold_in(_jax.random.key(seed), 7919)
    shape, _ = _output_shape()
    out = dict(inp)
    out["loss_target"] = _jax.random.normal(_jax.random.fold_in(key, 0), shape, _jnp.float32)
    for i, name in enumerate(TWIN_WEIGHTS):
        w = inp[name].astype(_jnp.float32)
        if MOMENT_SCALE is None:
            s = _jnp.sqrt(_jnp.mean(_jnp.square(w)) + 1e-30)
        else:
            s = MOMENT_SCALE[name]
        km, kv = _jax.random.split(_jax.random.fold_in(key, i + 1))
        out[name] = w
        out["m_" + name] = s * _jax.random.normal(km, w.shape, _jnp.float32)
        out["v_" + name] = (s * s) * _jax.random.uniform(kv, w.shape, _jnp.float32, 0.5, 1.5)
    if N_MICROBATCH > 1:
        for name, axis in PER_EXAMPLE_BATCH_AXIS.items():
            out[name] = _to_microbatches(out[name], axis)
    return {'x': out['x'], 'mem': out['mem'], 'a_norm_attn': out['a_norm_attn'], 'a_w_in': out['a_w_in'], 'a_w_out': out['a_w_out'], 'a_norm_mem': out['a_norm_mem'], 'a_w_mem_kv': out['a_w_mem_kv'], 'a_norm_ffn': out['a_norm_ffn'], 'a_ffn_up': out['a_ffn_up'], 'a_ffn_conv': out['a_ffn_conv'], 'a_ffn_down': out['a_ffn_down'], 'kv_norm': out['kv_norm'], 'w_kv_shared': out['w_kv_shared'], 'b_norm_attn': out['b_norm_attn'], 'b_w_in': out['b_w_in'], 'b_w_out': out['b_w_out'], 'b_norm_mem': out['b_norm_mem'], 'b_w_mem_kv': out['b_w_mem_kv'], 'b_norm_ffn': out['b_norm_ffn'], 'b_ffn_up': out['b_ffn_up'], 'b_ffn_conv': out['b_ffn_conv'], 'b_ffn_down': out['b_ffn_down'], 'final_norm': out['final_norm'], 'loss_target': out['loss_target'], 'm_a_norm_attn': out['m_a_norm_attn'], 'm_a_w_in': out['m_a_w_in'], 'm_a_w_out': out['m_a_w_out'], 'm_a_norm_mem': out['m_a_norm_mem'], 'm_a_w_mem_kv': out['m_a_w_mem_kv'], 'm_a_norm_ffn': out['m_a_norm_ffn'], 'm_a_ffn_up': out['m_a_ffn_up'], 'm_a_ffn_conv': out['m_a_ffn_conv'], 'm_a_ffn_down': out['m_a_ffn_down'], 'm_kv_norm': out['m_kv_norm'], 'm_w_kv_shared': out['m_w_kv_shared'], 'm_b_norm_attn': out['m_b_norm_attn'], 'm_b_w_in': out['m_b_w_in'], 'm_b_w_out': out['m_b_w_out'], 'm_b_norm_mem': out['m_b_norm_mem'], 'm_b_w_mem_kv': out['m_b_w_mem_kv'], 'm_b_norm_ffn': out['m_b_norm_ffn'], 'm_b_ffn_up': out['m_b_ffn_up'], 'm_b_ffn_conv': out['m_b_ffn_conv'], 'm_b_ffn_down': out['m_b_ffn_down'], 'm_final_norm': out['m_final_norm'], 'v_a_norm_attn': out['v_a_norm_attn'], 'v_a_w_in': out['v_a_w_in'], 'v_a_w_out': out['v_a_w_out'], 'v_a_norm_mem': out['v_a_norm_mem'], 'v_a_w_mem_kv': out['v_a_w_mem_kv'], 'v_a_norm_ffn': out['v_a_norm_ffn'], 'v_a_ffn_up': out['v_a_ffn_up'], 'v_a_ffn_conv': out['v_a_ffn_conv'], 'v_a_ffn_down': out['v_a_ffn_down'], 'v_kv_norm': out['v_kv_norm'], 'v_w_kv_shared': out['v_w_kv_shared'], 'v_b_norm_attn': out['v_b_norm_attn'], 'v_b_w_in': out['v_b_w_in'], 'v_b_w_out': out['v_b_w_out'], 'v_b_norm_mem': out['v_b_norm_mem'], 'v_b_w_mem_kv': out['v_b_w_mem_kv'], 'v_b_norm_ffn': out['v_b_norm_ffn'], 'v_b_ffn_up': out['v_b_ffn_up'], 'v_b_ffn_conv': out['v_b_ffn_conv'], 'v_b_ffn_down': out['v_b_ffn_down'], 'v_final_norm': out['v_final_norm']}


def _loss(weights, diff, rest, loss_target):
    with _jax.named_scope("forward"):
        args = {**rest, TWIN_DIFF_INPUT: diff, **{k: w.astype(_WEIGHT_DTYPES[k]) for k, w in weights.items()}}
        y = _forward(args)
    with _jax.named_scope("loss_head"):
        err = _jnp.square(y.astype(_jnp.float32) - loss_target)
        return 0.5 * _jnp.sum(_jnp.mean(err, axis=-1)) if err.ndim else 0.5 * err


def _adamw(w, g, m, v):
    m = ADAM_B1 * m + (1.0 - ADAM_B1) * g
    v = ADAM_B2 * v + (1.0 - ADAM_B2) * _jnp.square(g)
    m_hat = m / (1.0 - ADAM_B1 ** ADAM_STEP)
    v_hat = v / (1.0 - ADAM_B2 ** ADAM_STEP)
    delta = -ADAM_LR * (m_hat / (_jnp.sqrt(v_hat) + ADAM_EPS) + ADAM_WD * w)
    return delta, m, v


def reference(x, mem, a_norm_attn, a_w_in, a_w_out, a_norm_mem, a_w_mem_kv, a_norm_ffn, a_ffn_up, a_ffn_conv, a_ffn_down, kv_norm, w_kv_shared, b_norm_attn, b_w_in, b_w_out, b_norm_mem, b_w_mem_kv, b_norm_ffn, b_ffn_up, b_ffn_conv, b_ffn_down, final_norm, loss_target, m_a_norm_attn, m_a_w_in, m_a_w_out, m_a_norm_mem, m_a_w_mem_kv, m_a_norm_ffn, m_a_ffn_up, m_a_ffn_conv, m_a_ffn_down, m_kv_norm, m_w_kv_shared, m_b_norm_attn, m_b_w_in, m_b_w_out, m_b_norm_mem, m_b_w_mem_kv, m_b_norm_ffn, m_b_ffn_up, m_b_ffn_conv, m_b_ffn_down, m_final_norm, v_a_norm_attn, v_a_w_in, v_a_w_out, v_a_norm_mem, v_a_w_mem_kv, v_a_norm_ffn, v_a_ffn_up, v_a_ffn_conv, v_a_ffn_down, v_kv_norm, v_w_kv_shared, v_b_norm_attn, v_b_w_in, v_b_w_out, v_b_norm_mem, v_b_w_mem_kv, v_b_norm_ffn, v_b_ffn_up, v_b_ffn_conv, v_b_ffn_down, v_final_norm):
    given = dict(x=x, mem=mem, a_norm_attn=a_norm_attn, a_w_in=a_w_in, a_w_out=a_w_out, a_norm_mem=a_norm_mem, a_w_mem_kv=a_w_mem_kv, a_norm_ffn=a_norm_ffn, a_ffn_up=a_ffn_up, a_ffn_conv=a_ffn_conv, a_ffn_down=a_ffn_down, kv_norm=kv_norm, w_kv_shared=w_kv_shared, b_norm_attn=b_norm_attn, b_w_in=b_w_in, b_w_out=b_w_out, b_norm_mem=b_norm_mem, b_w_mem_kv=b_w_mem_kv, b_norm_ffn=b_norm_ffn, b_ffn_up=b_ffn_up, b_ffn_conv=b_ffn_conv, b_ffn_down=b_ffn_down, final_norm=final_norm, loss_target=loss_target, m_a_norm_attn=m_a_norm_attn, m_a_w_in=m_a_w_in, m_a_w_out=m_a_w_out, m_a_norm_mem=m_a_norm_mem, m_a_w_mem_kv=m_a_w_mem_kv, m_a_norm_ffn=m_a_norm_ffn, m_a_ffn_up=m_a_ffn_up, m_a_ffn_conv=m_a_ffn_conv, m_a_ffn_down=m_a_ffn_down, m_kv_norm=m_kv_norm, m_w_kv_shared=m_w_kv_shared, m_b_norm_attn=m_b_norm_attn, m_b_w_in=m_b_w_in, m_b_w_out=m_b_w_out, m_b_norm_mem=m_b_norm_mem, m_b_w_mem_kv=m_b_w_mem_kv, m_b_norm_ffn=m_b_norm_ffn, m_b_ffn_up=m_b_ffn_up, m_b_ffn_conv=m_b_ffn_conv, m_b_ffn_down=m_b_ffn_down, m_final_norm=m_final_norm, v_a_norm_attn=v_a_norm_attn, v_a_w_in=v_a_w_in, v_a_w_out=v_a_w_out, v_a_norm_mem=v_a_norm_mem, v_a_w_mem_kv=v_a_w_mem_kv, v_a_norm_ffn=v_a_norm_ffn, v_a_ffn_up=v_a_ffn_up, v_a_ffn_conv=v_a_ffn_conv, v_a_ffn_down=v_a_ffn_down, v_kv_norm=v_kv_norm, v_w_kv_shared=v_w_kv_shared, v_b_norm_attn=v_b_norm_attn, v_b_w_in=v_b_w_in, v_b_w_out=v_b_w_out, v_b_norm_mem=v_b_norm_mem, v_b_w_mem_kv=v_b_w_mem_kv, v_b_norm_ffn=v_b_norm_ffn, v_b_ffn_up=v_b_ffn_up, v_b_ffn_conv=v_b_ffn_conv, v_b_ffn_down=v_b_ffn_down, v_final_norm=v_final_norm)
    weights = {n: given[n] for n in TWIN_WEIGHTS}
    shared = {n: given[n] for n in SHARED_INPUTS}
    per_example = {n: given[n] for n in ['x', 'mem']}
    grad_fn = _jax.value_and_grad(_loss, argnums=(0, 1))

    def one_microbatch(ex, loss_target):
        ex = dict(ex)
        diff = ex.pop(TWIN_DIFF_INPUT)
        return grad_fn(weights, diff, {**shared, **ex}, loss_target)

    if N_MICROBATCH == 1:
        loss, (grad_w, grad_x) = one_microbatch(per_example, given["loss_target"])
    else:
        def body(carry, xs):
            loss_sum, grad_sum = carry
            l_k, (gw_k, gx_k) = one_microbatch(xs[0], xs[1])
            with _jax.named_scope("update"):
                return (loss_sum + l_k, _jax.tree.map(_jnp.add, grad_sum, gw_k)), gx_k

        init = (_jnp.zeros((), _jnp.float32), _jax.tree.map(_jnp.zeros_like, weights))
        (loss, grad_w), grad_x = _jax.lax.scan(body, init, (per_example, given["loss_target"]))
    with _jax.named_scope("update"):
        delta_w, new_m, new_v = {}, {}, {}
        for n in TWIN_WEIGHTS:
            delta_w[n], new_m[n], new_v[n] = _adamw(weights[n], grad_w[n], given["m_" + n], given["v_" + n])
    return (loss, grad_x, *[grad_w[n] for n in TWIN_WEIGHTS], *[delta_w[n] for n in TWIN_WEIGHTS],
            *[new_m[n] for n in TWIN_WEIGHTS], *[new_v[n] for n in TWIN_WEIGHTS])
```

```python
import functools
import math

import jax
import jax.numpy as jnp
from jax import lax
from jax.experimental import pallas as pl
from jax.experimental.pallas import tpu as pltpu

F32 = jnp.float32
BF16 = jnp.bfloat16

N_DEV = 8
HEAD_DIM = 64
N_SB_HEADS = 12
N_DIL_HEADS = 12
DIL_GROUPS = ((128, 1), (512, 4), (2048, 16))
SB_WIDTH = N_SB_HEADS * HEAD_DIM
MEM_WIDTH = 256
DIL_WIDTH = N_DIL_HEADS * HEAD_DIM
ATT_SCALE = HEAD_DIM ** -0.5
EPS = 1e-6
ALIBI_MAX_BIAS = 8.0
NEG_BIG = -1e30

ADAM_LR = 0.001
ADAM_B1 = 0.9
ADAM_B2 = 0.999
ADAM_EPS = 1e-08
ADAM_WD = 0.01
ADAM_STEP = 10

LANE = 128
QBLK = 128
VMEM_LIMIT_BYTES = 48 * 1024 * 1024
PACK_COLS = 1024
MESH_ID = pl.DeviceIdType.MESH


def _cp(*sem):
    return pltpu.CompilerParams(dimension_semantics=sem, vmem_limit_bytes=VMEM_LIMIT_BYTES)


def _pick(n, cands):
    for c in cands:
        if n % c == 0:
            return c
    raise ValueError(f"no tile for {n} in {cands}")


def _dot(a, b):
    return jnp.dot(a, b, preferred_element_type=F32)


def _dot_nt(a, b):
    return lax.dot_general(a, b, (((1,), (1,)), ((), ())), preferred_element_type=F32)


def _dot_tn(a, b):
    return lax.dot_general(a, b, (((0,), (0,)), ((), ())), preferred_element_type=F32)


def _dot_split(x, u):
    hi = x.astype(BF16)
    lo = (x - hi.astype(F32)).astype(BF16)
    return _dot(hi, u) + _dot(lo, u)


def _mm(a, b, *, name, out_dtype, res=None, trans_a=False):
    if trans_a:
        kdim, m = a.shape
    else:
        m, kdim = a.shape
    kb, n = b.shape
    assert kb == kdim, (a.shape, b.shape)
    tm = _pick(m, (1024, 512, 256, 128))
    tn = _pick(n, (512, 256, 128))
    if trans_a:
        tk = _pick(kdim, (1024, 512, 256))
    else:
        tk = kdim if kdim <= 2048 else _pick(kdim, (2048, 1536, 1408, 1280, 1024, 512))
    nk = kdim // tk
    has_res = res is not None

    def body(*refs):
        if has_res:
            a_ref, b_ref, r_ref, o_ref = refs[:4]
            scr = refs[4:]
        else:
            a_ref, b_ref, o_ref = refs[:3]
            r_ref = None
            scr = refs[3:]
        av = a_ref[...].astype(BF16)
        bv = b_ref[...].astype(BF16)
        p = _dot_tn(av, bv) if trans_a else _dot(av, bv)

        def finish(acc):
            if has_res:
                acc = acc + r_ref[...]
            o_ref[...] = acc.astype(o_ref.dtype)

        if nk == 1:
            finish(p)
        else:
            acc_ref = scr[0]
            k = pl.program_id(2)

            @pl.when(k == 0)
            def _():
                acc_ref[...] = p

            @pl.when(k > 0)
            def _():
                acc_ref[...] += p

            @pl.when(k == nk - 1)
            def _():
                finish(acc_ref[...])

    if trans_a:
        a_spec = pl.BlockSpec((tk, tm), lambda i, j, k: (k, i))
    else:
        a_spec = pl.BlockSpec((tm, tk), lambda i, j, k: (i, k))
    in_specs = [a_spec, pl.BlockSpec((tk, tn), lambda i, j, k: (k, j))]
    args = [a, b]
    if has_res:
        in_specs.append(pl.BlockSpec((tm, tn), lambda i, j, k: (i, j)))
        args.append(res)
    return pl.pallas_call(
        body, name=name,
        grid=(m // tm, n // tn, nk),
        in_specs=in_specs,
        out_specs=pl.BlockSpec((tm, tn), lambda i, j, k: (i, j)),
        out_shape=jax.ShapeDtypeStruct((m, n), out_dtype),
        scratch_shapes=[pltpu.VMEM((tm, tn), F32)] if nk > 1 else [],
        compiler_params=_cp("parallel", "parallel", "arbitrary"),
    )(*args)


def _rms_fwd(x, gains, *, name):
    t, d = x.shape
    tr = _pick(t, (512, 256, 128, 8))
    ng = len(gains)

    def body(x_ref, *rest):
        g_refs, n_refs, r_ref = rest[:ng], rest[ng:2 * ng], rest[2 * ng]
        xv = x_ref[...]
        r = lax.rsqrt(jnp.mean(xv * xv, axis=-1, keepdims=True) + EPS)
        xh = xv * r
        for g_ref, n_ref in zip(g_refs, n_refs):
            n_ref[...] = (xh * g_ref[...]).astype(BF16)
        r_ref[...] = r

    row = pl.BlockSpec((tr, d), lambda i: (i, 0))
    gsp = pl.BlockSpec((1, d), lambda i: (0, 0))
    outs = pl.pallas_call(
        body, name=name, grid=(t // tr,),
        in_specs=[row] + [gsp] * ng,
        out_specs=[row] * ng + [pl.BlockSpec((tr, 1), lambda i: (i, 0))],
        out_shape=[jax.ShapeDtypeStruct((t, d), BF16)] * ng + [jax.ShapeDtypeStruct((t, 1), F32)],
        compiler_params=_cp("parallel"),
    )(x, *gains)
    return list(outs[:ng]), outs[ng]


def _rms_bwd(x, r, pairs, dres, *, name, need_dx=True):
    t, d = x.shape
    tr = _pick(t, (512, 256, 128, 8))
    npair = len(pairs)
    has_res = dres is not None

    def body(*refs):
        x_ref, r_ref = refs[:2]
        pr = refs[2:2 + 2 * npair]
        pos = 2 + 2 * npair
        res_ref = None
        if has_res:
            res_ref = refs[pos]
            pos += 1
        dx_ref = None
        if need_dx:
            dx_ref = refs[pos]
            pos += 1
        dg_refs = refs[pos:pos + npair]
        i = pl.program_id(0)
        rv = r_ref[...]
        xh = x_ref[...] * rv
        dx = res_ref[...] if has_res else None
        for k in range(npair):
            dn = pr[2 * k][...].astype(F32)
            g = pr[2 * k + 1][...]
            part = jnp.sum(dn * xh, axis=0, keepdims=True)

            @pl.when(i == 0)
            def _():
                dg_refs[k][...] = part

            @pl.when(i > 0)
            def _():
                dg_refs[k][...] += part

            if need_dx:
                dxh = dn * g
                c = jnp.mean(dxh * xh, axis=-1, keepdims=True)
                term = rv * (dxh - xh * c)
                dx = term if dx is None else dx + term
        if need_dx:
            dx_ref[...] = dx

    row = pl.BlockSpec((tr, d), lambda i: (i, 0))
    gsp = pl.BlockSpec((1, d), lambda i: (0, 0))
    in_specs = [row, pl.BlockSpec((tr, 1), lambda i: (i, 0))]
    args = [x, r]
    for dn, g in pairs:
        in_specs += [row, gsp]
        args += [dn, g]
    if has_res:
        in_specs.append(row)
        args.append(dres)
    out_specs, out_shape = [], []
    if need_dx:
        out_specs.append(row)
        out_shape.append(jax.ShapeDtypeStruct((t, d), F32))
    out_specs += [gsp] * npair
    out_shape += [jax.ShapeDtypeStruct((1, d), F32)] * npair
    outs = pl.pallas_call(
        body, name=name, grid=(t // tr,), in_specs=in_specs, out_specs=out_specs, out_shape=out_shape,
        compiler_params=_cp("arbitrary"),
    )(*args)
    if need_dx:
        return outs[0], list(outs[1:])
    return None, list(outs)


def _loss_head(h, g, tgt, *, name):
    t, d = h.shape
    tr = _pick(t, (512, 256, 128, 8))

    def body(h_ref, g_ref, t_ref, dh_ref, dg_ref, l_ref):
        i = pl.program_id(0)
        xv = h_ref[...]
        gv = g_ref[...]
        r = lax.rsqrt(jnp.mean(xv * xv, axis=-1, keepdims=True) + EPS)
        xh = xv * r
        e = xh * gv - t_ref[...]
        dy = e * (1.0 / d)
        lpart = jnp.sum(e * e, axis=0, keepdims=True)
        gpart = jnp.sum(dy * xh, axis=0, keepdims=True)

        @pl.when(i == 0)
        def _():
            l_ref[...] = lpart
            dg_ref[...] = gpart

        @pl.when(i > 0)
        def _():
            l_ref[...] += lpart
            dg_ref[...] += gpart

        dxh = dy * gv
        c = jnp.mean(dxh * xh, axis=-1, keepdims=True)
        dh_ref[...] = r * (dxh - xh * c)

    row = pl.BlockSpec((tr, d), lambda i: (i, 0))
    gsp = pl.BlockSpec((1, d), lambda i: (0, 0))
    return pl.pallas_call(
        body, name=name, grid=(t // tr,), in_specs=[row, gsp, row], out_specs=[row, gsp, gsp],
        out_shape=[jax.ShapeDtypeStruct((t, d), F32), jax.ShapeDtypeStruct((1, d), F32),
                   jax.ShapeDtypeStruct((1, d), F32)],
        compiler_params=_cp("arbitrary"),
    )(h, g, tgt)


def _head_masks2(shape):
    lane = lax.broadcasted_iota(jnp.int32, shape, 1)
    return [lane < HEAD_DIM, lane >= HEAD_DIM]


def _sb_fwd(proj, *, name):
    b, s, _ = proj.shape
    nq = s // QBLK
    npair = SB_WIDTH // LANE

    def body(q_ref, k_ref, v_ref, o_ref, r_ref, acc_ref, car_ref):
        i = pl.program_id(2)
        masks = _head_masks2((QBLK, LANE))
        row = lax.broadcasted_iota(jnp.int32, (QBLK, QBLK), 0)
        col = lax.broadcasted_iota(jnp.int32, (QBLK, QBLK), 1)
        later_mat = (row > col).astype(BF16)
        q = q_ref[0]
        qh = [jnp.where(mk, q, jnp.zeros_like(q)) for mk in masks]
        acc_ref[...] = jnp.zeros_like(acc_ref)
        car_ref[...] = jnp.zeros_like(car_ref)

        def step(jj, carry):
            j = i - jj
            off = pl.multiple_of(j * QBLK, QBLK)
            kj = k_ref[0, pl.ds(off, QBLK), :]
            vj = v_ref[0, pl.ds(off, QBLK), :]
            causal = (j * QBLK + col) < (i * QBLK + row)
            for h in range(2):
                z = _dot_nt(qh[h], kj) * ATT_SCALE
                sp = jnp.maximum(z, 0.0) + jnp.log1p(jnp.exp(-jnp.abs(z)))
                ls = jnp.where(causal, -sp, 0.0)
                later = _dot_split(ls, later_mat)
                car = car_ref[h]
                w = jnp.where(causal, jnp.exp((z - sp) + later + car), 0.0)
                acc_ref[h] += _dot(w.astype(BF16), vj)
                car_ref[h] = car + jnp.sum(ls, axis=1, keepdims=True)
            return carry

        lax.fori_loop(0, i + 1, step, 0)
        o_ref[0] = jnp.where(masks[0], acc_ref[0], acc_ref[1]).astype(o_ref.dtype)
        r_ref[0] = jnp.where(masks[0], car_ref[0], car_ref[1])

    kv_off = SB_WIDTH // LANE
    return pl.pallas_call(
        body, name=name, grid=(b, npair, nq),
        in_specs=[pl.BlockSpec((1, QBLK, LANE), lambda bb, p, i: (bb, i, p)),
                  pl.BlockSpec((1, s, LANE), lambda bb, p, i: (bb, 0, kv_off + p)),
                  pl.BlockSpec((1, s, LANE), lambda bb, p, i: (bb, 0, 2 * kv_off + p))],
        out_specs=[pl.BlockSpec((1, QBLK, LANE), lambda bb, p, i: (bb, i, p)),
                   pl.BlockSpec((1, QBLK, LANE), lambda bb, p, i: (bb, i, p))],
        out_shape=[jax.ShapeDtypeStruct((b, s, SB_WIDTH), BF16), jax.ShapeDtypeStruct((b, s, SB_WIDTH), F32)],
        scratch_shapes=[pltpu.VMEM((2, QBLK, LANE), F32), pltpu.VMEM((2, QBLK, LANE), F32)],
        compiler_params=_cp("parallel", "parallel", "arbitrary"),
    )(proj, proj, proj)


def _sb_bwd(proj, dcat, rsum, *, name):
    b, s, _ = proj.shape
    nq = s // QBLK
    npair = SB_WIDTH // LANE

    def body(q_ref, k_ref, v_ref, do_ref, r_ref, dq_ref, dk_ref, dv_ref, dq_acc, cp_ref, cg_ref):
        i = pl.program_id(2)

        @pl.when(i == 0)
        def _():
            dk_ref[...] = jnp.zeros_like(dk_ref)
            dv_ref[...] = jnp.zeros_like(dv_ref)

        masks = _head_masks2((QBLK, LANE))
        row = lax.broadcasted_iota(jnp.int32, (QBLK, QBLK), 0)
        col = lax.broadcasted_iota(jnp.int32, (QBLK, QBLK), 1)
        incl_mat = (row <= col).astype(BF16)
        excl_mat = (row < col).astype(BF16)
        q = q_ref[0]
        do = do_ref[0]
        qh = [jnp.where(mk, q, jnp.zeros_like(q)) for mk in masks]
        doh = [jnp.where(mk, do, jnp.zeros_like(do)) for mk in masks]
        rv = r_ref[0]
        rtot = [rv[:, 0:1], rv[:, HEAD_DIM:HEAD_DIM + 1]]
        dq_acc[...] = jnp.zeros_like(dq_acc)
        cp_ref[...] = jnp.zeros_like(cp_ref)
        cg_ref[...] = jnp.zeros_like(cg_ref)

        def step(j, carry):
            off = pl.multiple_of(j * QBLK, QBLK)
            kj = k_ref[0, pl.ds(off, QBLK), :]
            vj = v_ref[0, pl.ds(off, QBLK), :]
            causal = (j * QBLK + col) < (i * QBLK + row)
            dk_part = None
            dv_part = None
            for h in range(2):
                z = _dot_nt(qh[h], kj) * ATT_SCALE
                sp = jnp.maximum(z, 0.0) + jnp.log1p(jnp.exp(-jnp.abs(z)))
                ls = jnp.where(causal, -sp, 0.0)
                lsig = z - sp
                cpv = cp_ref[h]
                later = (rtot[h] - cpv) - _dot_split(ls, incl_mat)
                w = jnp.where(causal, jnp.exp(lsig + later), 0.0)
                dw = _dot_nt(doh[h], vj)
                g = dw * w
                cgv = cg_ref[h]
                gpre = cgv + _dot_split(g, excl_mat)
                sig = jnp.exp(lsig)
                dz = jnp.where(causal, g * (1.0 - sig) - gpre * sig, 0.0) * ATT_SCALE
                dzb = dz.astype(BF16)
                wb = w.astype(BF16)
                dq_acc[h] += _dot(dzb, kj)
                dkp = _dot_tn(dzb, qh[h])
                dvp = _dot_tn(wb, doh[h])
                dk_part = dkp if dk_part is None else dk_part + dkp
                dv_part = dvp if dv_part is None else dv_part + dvp
                cp_ref[h] = cpv + jnp.sum(ls, axis=1, keepdims=True)
                cg_ref[h] = cgv + jnp.sum(g, axis=1, keepdims=True)
            dk_ref[0, pl.ds(off, QBLK), :] += dk_part
            dv_ref[0, pl.ds(off, QBLK), :] += dv_part
            return carry

        lax.fori_loop(0, i + 1, step, 0)
        dq_ref[0] = jnp.where(masks[0], dq_acc[0], dq_acc[1]).astype(dq_ref.dtype)

    kv_off = SB_WIDTH // LANE
    blk = pl.BlockSpec((1, QBLK, LANE), lambda bb, p, i: (bb, i, p))
    seq = pl.BlockSpec((1, s, LANE), lambda bb, p, i: (bb, 0, p))
    return pl.pallas_call(
        body, name=name, grid=(b, npair, nq),
        in_specs=[blk,
                  pl.BlockSpec((1, s, LANE), lambda bb, p, i: (bb, 0, kv_off + p)),
                  pl.BlockSpec((1, s, LANE), lambda bb, p, i: (bb, 0, 2 * kv_off + p)),
                  blk, blk],
        out_specs=[blk, seq, seq],
        out_shape=[jax.ShapeDtypeStruct((b, s, SB_WIDTH), BF16), jax.ShapeDtypeStruct((b, s, SB_WIDTH), F32),
                   jax.ShapeDtypeStruct((b, s, SB_WIDTH), F32)],
        scratch_shapes=[pltpu.VMEM((2, QBLK, LANE), F32), pltpu.VMEM((2, QBLK, LANE), F32),
                        pltpu.VMEM((2, QBLK, LANE), F32)],
        compiler_params=_cp("parallel", "parallel", "arbitrary"),
    )(proj, proj, proj, dcat, rsum)


GRP = 4 * HEAD_DIM


def _head_masks4(shape):
    lane = lax.broadcasted_iota(jnp.int32, shape, 1)
    return [(lane >= HEAD_DIM * h) & (lane < HEAD_DIM * (h + 1)) for h in range(4)]


def _band_terms(i, slopes_scaled):
    a = lax.broadcasted_iota(jnp.int32, (QBLK, 2 * QBLK), 0)
    bcol = lax.broadcasted_iota(jnp.int32, (QBLK, 2 * QBLK), 1)
    delta = a + QBLK - bcol
    valid = (delta >= 0) & (delta <= QBLK) & ((i > 0) | (bcol >= QBLK))
    dist = delta.astype(F32)
    return valid, [(-sl) * dist for sl in slopes_scaled]


def _attn_specs(banded, q_lane_blk, k_lane_blk, v_lane_blk):
    qs = pl.BlockSpec((1, QBLK, GRP), lambda n, i: (n, i, q_lane_blk))
    if banded:
        ks = [pl.BlockSpec((1, QBLK, GRP), lambda n, i: (n, jnp.maximum(i - 1, 0), k_lane_blk)),
              pl.BlockSpec((1, QBLK, GRP), lambda n, i: (n, i, k_lane_blk))]
        vs = [pl.BlockSpec((1, QBLK, GRP), lambda n, i: (n, jnp.maximum(i - 1, 0), v_lane_blk)),
              pl.BlockSpec((1, QBLK, GRP), lambda n, i: (n, i, v_lane_blk))]
    else:
        ks = [pl.BlockSpec((1, 2 * QBLK, GRP), lambda n, i: (n, 0, k_lane_blk))]
        vs = [pl.BlockSpec((1, 2 * QBLK, GRP), lambda n, i: (n, 0, v_lane_blk))]
    return qs, ks, vs


def _attn_fwd(q, k, v, *, name, banded, slopes_scaled=None, q_lane_blk=0, k_lane_blk=0, v_lane_blk=0,
              out_dtype=F32):
    n, l, _ = q.shape
    nb = l // QBLK
    nkv = 2 if banded else 1

    def body(*refs):
        q_ref = refs[0]
        k_refs = refs[1:1 + nkv]
        v_refs = refs[1 + nkv:1 + 2 * nkv]
        o_ref, lse_ref = refs[1 + 2 * nkv:]
        i = pl.program_id(1)
        masks = _head_masks4((QBLK, GRP))
        qv = q_ref[0]
        if banded:
            k2 = jnp.concatenate([k_refs[0][0], k_refs[1][0]], axis=0)
            v2 = jnp.concatenate([v_refs[0][0], v_refs[1][0]], axis=0)
            valid, bias = _band_terms(i, slopes_scaled)
        else:
            k2 = k_refs[0][0]
            v2 = v_refs[0][0]
        o = jnp.zeros((QBLK, GRP), F32)
        lse = jnp.zeros((QBLK, GRP), F32)
        for h in range(4):
            qm = jnp.where(masks[h], qv, jnp.zeros_like(qv))
            sc = _dot_nt(qm, k2) * ATT_SCALE
            if banded:
                sc = jnp.where(valid, sc + bias[h], NEG_BIG)
            m = jnp.max(sc, axis=-1, keepdims=True)
            p = jnp.exp(sc - m)
            den = jnp.sum(p, axis=-1, keepdims=True)
            oh = _dot(p.astype(BF16), v2) / den
            o = jnp.where(masks[h], oh, o)
            lse = jnp.where(masks[h], m + jnp.log(den), lse)
        o_ref[0] = o.astype(o_ref.dtype)
        lse_ref[0] = lse

    qs, ks, vs = _attn_specs(banded, q_lane_blk, k_lane_blk, v_lane_blk)
    ob = pl.BlockSpec((1, QBLK, GRP), lambda nn, i: (nn, i, 0))
    return pl.pallas_call(
        body, name=name, grid=(n, nb),
        in_specs=[qs] + ks + vs, out_specs=[ob, ob],
        out_shape=[jax.ShapeDtypeStruct((n, l, GRP), out_dtype), jax.ShapeDtypeStruct((n, l, GRP), F32)],
        compiler_params=_cp("parallel", "arbitrary"),
    )(q, *([k] * nkv), *([v] * nkv))


def _attn_bwd(q, k, v, do, lse, delta, *, name, banded, slopes_scaled=None, q_lane_blk=0, k_lane_blk=0,
              v_lane_blk=0, do_lane_blk=0):
    n, l, _ = q.shape
    nb = l // QBLK
    nkv = 2 if banded else 1
    lk = l if banded else 2 * QBLK

    def body(*refs):
        q_ref = refs[0]
        k_refs = refs[1:1 + nkv]
        v_refs = refs[1 + nkv:1 + 2 * nkv]
        do_ref, lse_ref, dl_ref, dq_ref, dk_ref, dv_ref = refs[1 + 2 * nkv:]
        i = pl.program_id(1)

        @pl.when(i == 0)
        def _():
            dk_ref[...] = jnp.zeros_like(dk_ref)
            dv_ref[...] = jnp.zeros_like(dv_ref)

        masks = _head_masks4((QBLK, GRP))
        qv = q_ref[0]
        dov = do_ref[0]
        lsev = lse_ref[0]
        dlv = dl_ref[0]
        if banded:
            k2 = jnp.concatenate([k_refs[0][0], k_refs[1][0]], axis=0)
            v2 = jnp.concatenate([v_refs[0][0], v_refs[1][0]], axis=0)
            valid, bias = _band_terms(i, slopes_scaled)
        else:
            k2 = k_refs[0][0]
            v2 = v_refs[0][0]
        dq = jnp.zeros((QBLK, GRP), F32)
        dk2 = jnp.zeros((2 * QBLK, GRP), F32)
        dv2 = jnp.zeros((2 * QBLK, GRP), F32)
        for h in range(4):
            c0 = HEAD_DIM * h
            qm = jnp.where(masks[h], qv, jnp.zeros_like(qv))
            dom = jnp.where(masks[h], dov, jnp.zeros_like(dov))
            sc = _dot_nt(qm, k2) * ATT_SCALE
            if banded:
                sc = jnp.where(valid, sc + bias[h], NEG_BIG)
            p = jnp.exp(sc - lsev[:, c0:c0 + 1])
            dp = _dot_nt(dom, v2)
            ds = (p * (dp - dlv[:, c0:c0 + 1]) * ATT_SCALE).astype(BF16)
            dq = jnp.where(masks[h], _dot(ds, k2), dq)
            dk2 = dk2 + _dot_tn(ds, qm)
            dv2 = dv2 + _dot_tn(p.astype(BF16), dom)
        dq_ref[0] = dq.astype(dq_ref.dtype)
        if banded:
            cur = pl.multiple_of(i * QBLK, QBLK)
            dk_ref[0, pl.ds(cur, QBLK), :] += dk2[QBLK:]
            dv_ref[0, pl.ds(cur, QBLK), :] += dv2[QBLK:]

            @pl.when(i > 0)
            def _():
                prev = pl.multiple_of((i - 1) * QBLK, QBLK)
                dk_ref[0, pl.ds(prev, QBLK), :] += dk2[:QBLK]
                dv_ref[0, pl.ds(prev, QBLK), :] += dv2[:QBLK]
        else:
            dk_ref[0] += dk2
            dv_ref[0] += dv2

    qs, ks, vs = _attn_specs(banded, q_lane_blk, k_lane_blk, v_lane_blk)
    ob = pl.BlockSpec((1, QBLK, GRP), lambda nn, i: (nn, i, 0))
    dos = pl.BlockSpec((1, QBLK, GRP), lambda nn, i: (nn, i, do_lane_blk))
    kvb = pl.BlockSpec((1, lk, GRP), lambda nn, i: (nn, 0, 0))
    return pl.pallas_call(
        body, name=name, grid=(n, nb),
        in_specs=[qs] + ks + vs + [dos, ob, ob], out_specs=[ob, kvb, kvb],
        out_shape=[jax.ShapeDtypeStruct((n, l, GRP), BF16), jax.ShapeDtypeStruct((n, lk, GRP), F32),
                   jax.ShapeDtypeStruct((n, lk, GRP), F32)],
        compiler_params=_cp("parallel", "arbitrary"),
    )(q, *([k] * nkv), *([v] * nkv), do, lse, delta)


def _attn_delta(do, o, *, name, lane_blks):
    t, _ = do.shape
    tr = _pick(t, (512, 256, 128, 8))
    ng = len(lane_blks)

    def body(*refs):
        do_refs, o_refs, d_ref = refs[:ng], refs[ng:2 * ng], refs[2 * ng]
        ra = lax.broadcasted_iota(jnp.int32, (GRP, GRP), 0) // HEAD_DIM
        rb = lax.broadcasted_iota(jnp.int32, (GRP, GRP), 1) // HEAD_DIM
        same_head = (ra == rb).astype(BF16)
        prod = None
        for a_ref, b_ref in zip(do_refs, o_refs):
            term = a_ref[...].astype(F32) * b_ref[...].astype(F32)
            prod = term if prod is None else prod + term
        d_ref[...] = _dot_split(prod, same_head)

    specs = [pl.BlockSpec((tr, GRP), functools.partial(lambda i, lb: (i, lb), lb=lb)) for lb in lane_blks]
    return pl.pallas_call(
        body, name=name, grid=(t // tr,), in_specs=specs + specs,
        out_specs=pl.BlockSpec((tr, GRP), lambda i: (i, 0)),
        out_shape=jax.ShapeDtypeStruct((t, GRP), F32),
        compiler_params=_cp("parallel"),
    )(*([do] * ng), *([o] * ng))


def _dil_combine(os, lses, *, name):
    t, _ = os[0].shape
    tr = _pick(t, (512, 256, 128, 8))
    ng = len(os)

    def body(*refs):
        o_refs, l_refs = refs[:ng], refs[ng:2 * ng]
        out_ref, lse_ref = refs[2 * ng:]
        ls = [r[...] for r in l_refs]
        m = functools.reduce(jnp.maximum, ls)
        tot = None
        for lv in ls:
            e = jnp.exp(lv - m)
            tot = e if tot is None else tot + e
        lse = m + jnp.log(tot)
        for g in range(ng):
            out_ref[:, GRP * g:GRP * (g + 1)] = (o_refs[g][...] * jnp.exp(ls[g] - lse)).astype(out_ref.dtype)
        lse_ref[...] = lse

    sp = pl.BlockSpec((tr, GRP), lambda i: (i, 0))
    return pl.pallas_call(
        body, name=name, grid=(t // tr,), in_specs=[sp] * (2 * ng),
        out_specs=[pl.BlockSpec((tr, GRP * ng), lambda i: (i, 0)), sp],
        out_shape=[jax.ShapeDtypeStruct((t, GRP * ng), BF16), jax.ShapeDtypeStruct((t, GRP), F32)],
        compiler_params=_cp("parallel"),
    )(*os, *lses)


FFN_LB = 256
FFN_ROWS = 64
HALO = 16


def _conv_chunk(u_ref, w, ci):
    r0 = pl.multiple_of(ci * FFN_ROWS, FFN_ROWS)
    cur = u_ref[0, pl.ds(r0, FFN_ROWS), :].astype(F32)
    p0 = pl.multiple_of(jnp.maximum(r0 - HALO, 0), HALO)
    prev = u_ref[0, pl.ds(p0, HALO), :].astype(F32)
    prev = jnp.where(ci > 0, prev, 0.0)
    rowi = lax.broadcasted_iota(jnp.int32, cur.shape, 0)
    s1 = jnp.where(rowi == 0, prev[HALO - 1:HALO], pltpu.roll(cur, 1, 0))
    s2 = jnp.where(rowi == 0, prev[HALO - 2:HALO - 1],
                   jnp.where(rowi == 1, prev[HALO - 1:HALO], pltpu.roll(cur, 2, 0)))
    c = w[0:1] * s2
    c = c + w[1:2] * s1
    c = c + w[2:3] * cur
    return c, cur, s1, s2


def _ffn_mid_fwd(u, wconv, *, name):
    b, s, f2 = u.shape
    f = f2 // 2
    nlb = f // FFN_LB

    def body(ua_ref, ug_ref, wa_ref, wg_ref, h_ref):
        wa = wa_ref[...]
        wg = wg_ref[...]

        def step(ci, carry):
            ca = _conv_chunk(ua_ref, wa, ci)[0]
            cg = _conv_chunk(ug_ref, wg, ci)[0]
            r0 = pl.multiple_of(ci * FFN_ROWS, FFN_ROWS)
            h_ref[0, pl.ds(r0, FFN_ROWS), :] = (cg * jax.nn.sigmoid(cg) * ca).astype(h_ref.dtype)
            return carry

        lax.fori_loop(0, s // FFN_ROWS, step, 0)

    return pl.pallas_call(
        body, name=name, grid=(nlb, b),
        in_specs=[pl.BlockSpec((1, s, FFN_LB), lambda l, bb: (bb, 0, l)),
                  pl.BlockSpec((1, s, FFN_LB), lambda l, bb: (bb, 0, nlb + l)),
                  pl.BlockSpec((3, FFN_LB), lambda l, bb: (0, l)),
                  pl.BlockSpec((3, FFN_LB), lambda l, bb: (0, nlb + l))],
        out_specs=pl.BlockSpec((1, s, FFN_LB), lambda l, bb: (bb, 0, l)),
        out_shape=jax.ShapeDtypeStruct((b, s, f), BF16),
        compiler_params=_cp("parallel", "parallel"),
    )(u, u, wconv, wconv)


def _ffn_mid_bwd(u, wconv, dh, *, name):
    b, s, f2 = u.shape
    f = f2 // 2
    nlb = f // FFN_LB
    nchunk = s // FFN_ROWS

    def body(ua_ref, ug_ref, wa_ref, wg_ref, dh_ref, dua_ref, dug_ref, dwa_ref, dwg_ref, dca_ref, dcg_ref):
        bb = pl.program_id(1)
        wa = wa_ref[...]
        wg = wg_ref[...]
        dca_ref[pl.ds(s, 8), :] = jnp.zeros((8, FFN_LB), F32)
        dcg_ref[pl.ds(s, 8), :] = jnp.zeros((8, FFN_LB), F32)

        def first(ci, carry):
            r0 = pl.multiple_of(ci * FFN_ROWS, FFN_ROWS)
            ca, cura, s1a, s2a = _conv_chunk(ua_ref, wa, ci)
            cg, curg, s1g, s2g = _conv_chunk(ug_ref, wg, ci)
            dhv = dh_ref[0, pl.ds(r0, FFN_ROWS), :].astype(F32)
            sg = jax.nn.sigmoid(cg)
            da = dhv * (cg * sg)
            dg = dhv * ca * (sg * (1.0 + cg * (1.0 - sg)))
            dca_ref[pl.ds(r0, FFN_ROWS), :] = da
            dcg_ref[pl.ds(r0, FFN_ROWS), :] = dg
            red = lambda x: jnp.sum(x, axis=0, keepdims=True)
            parts = (red(da * s2a), red(da * s1a), red(da * cura), red(dg * s2g), red(dg * s1g), red(dg * curg))
            return tuple(c + p for c, p in zip(carry, parts))

        zero = jnp.zeros((1, FFN_LB), F32)
        taps = lax.fori_loop(0, nchunk, first, (zero,) * 6)

        @pl.when(bb == 0)
        def _():
            for k in range(3):
                dwa_ref[k:k + 1, :] = taps[k]
                dwg_ref[k:k + 1, :] = taps[3 + k]

        @pl.when(bb > 0)
        def _():
            for k in range(3):
                dwa_ref[k:k + 1, :] += taps[k]
                dwg_ref[k:k + 1, :] += taps[3 + k]

        def second(ci, carry):
            r0 = pl.multiple_of(ci * FFN_ROWS, FFN_ROWS)
            rowi = lax.broadcasted_iota(jnp.int32, (FFN_ROWS, FFN_LB), 0)
            for dc_ref, w, out_ref in ((dca_ref, wa, dua_ref), (dcg_ref, wg, dug_ref)):
                cur = dc_ref[pl.ds(r0, FFN_ROWS), :]
                nxt = dc_ref[pl.ds(pl.multiple_of(r0 + FFN_ROWS, 8), 8), :]
                n1 = jnp.where(rowi == FFN_ROWS - 1, nxt[0:1], pltpu.roll(cur, FFN_ROWS - 1, 0))
                n2 = jnp.where(rowi == FFN_ROWS - 2, nxt[0:1],
                               jnp.where(rowi == FFN_ROWS - 1, nxt[1:2], pltpu.roll(cur, FFN_ROWS - 2, 0)))
                du = w[2:3] * cur + w[1:2] * n1 + w[0:1] * n2
                out_ref[0, pl.ds(r0, FFN_ROWS), :] = du.astype(out_ref.dtype)
            return carry

        lax.fori_loop(0, nchunk, second, 0)

    seq_a = pl.BlockSpec((1, s, FFN_LB), lambda l, bb: (bb, 0, l))
    seq_g = pl.BlockSpec((1, s, FFN_LB), lambda l, bb: (bb, 0, nlb + l))
    wsp = pl.BlockSpec((3, FFN_LB), lambda l, bb: (0, l))
    return pl.pallas_call(
        body, name=name, grid=(nlb, b),
        in_specs=[seq_a, seq_g, wsp, pl.BlockSpec((3, FFN_LB), lambda l, bb: (0, nlb + l)), seq_a],
        out_specs=[seq_a, seq_a, wsp, wsp],
        out_shape=[jax.ShapeDtypeStruct((b, s, f), BF16), jax.ShapeDtypeStruct((b, s, f), BF16),
                   jax.ShapeDtypeStruct((3, f), F32), jax.ShapeDtypeStruct((3, f), F32)],
        scratch_shapes=[pltpu.VMEM((s + 8, FFN_LB), F32), pltpu.VMEM((s + 8, FFN_LB), F32)],
        compiler_params=_cp("parallel", "arbitrary"),
    )(u, u, wconv, wconv, dh)


def _adam_math(w, g, m, v):
    m2 = ADAM_B1 * m + (1.0 - ADAM_B1) * g
    v2 = ADAM_B2 * v + (1.0 - ADAM_B2) * (g * g)
    m_hat = m2 / (1.0 - ADAM_B1 ** ADAM_STEP)
    v_hat = v2 / (1.0 - ADAM_B2 ** ADAM_STEP)
    delta = -ADAM_LR * (m_hat / (jnp.sqrt(v_hat) + ADAM_EPS) + ADAM_WD * w)
    return delta, m2, v2


def _adam(w, g, m, v, *, name):
    r, c = w.shape
    tr = _pick(r, (256, 128, 88, 64, 32, 16, 8))

    def body(w_ref, g_ref, m_ref, v_ref, d_ref, m2_ref, v2_ref):
        d, m2, v2 = _adam_math(w_ref[...], g_ref[...], m_ref[...], v_ref[...])
        d_ref[...] = d
        m2_ref[...] = m2
        v2_ref[...] = v2

    sp = pl.BlockSpec((tr, c), lambda i: (i, 0))
    return pl.pallas_call(
        body, name=name, grid=(r // tr,), in_specs=[sp] * 4, out_specs=[sp] * 3,
        out_shape=[jax.ShapeDtypeStruct((r, c), F32)] * 3,
        compiler_params=_cp("parallel"),
    )(w, g, m, v)


def _adam_small(quads, *, name):
    nq = len(quads)

    def body(*refs):
        ins, outs = refs[:4 * nq], refs[4 * nq:]
        for k in range(nq):
            w_ref, g_ref, m_ref, v_ref = ins[4 * k:4 * k + 4]
            d, m2, v2 = _adam_math(w_ref[...], g_ref[...], m_ref[...], v_ref[...])
            outs[3 * k][...] = d
            outs[3 * k + 1][...] = m2
            outs[3 * k + 2][...] = v2

    flat = [a for q in quads for a in q]
    out_shape = [jax.ShapeDtypeStruct(q[0].shape, F32) for q in quads for _ in range(3)]
    vm = pl.BlockSpec(memory_space=pltpu.VMEM)
    outs = pl.pallas_call(
        body, name=name, in_specs=[vm] * len(flat), out_specs=[vm] * len(out_shape), out_shape=out_shape,
        compiler_params=pltpu.CompilerParams(vmem_limit_bytes=VMEM_LIMIT_BYTES),
    )(*flat)
    return [tuple(outs[3 * k:3 * k + 3]) for k in range(nq)]


def _mesh_pos():
    return lax.axis_index("x"), lax.axis_index("y"), lax.axis_index("c")


def _flip(v, bit):
    return 1 - v if bit else v


def _all_gather_hbm(xl, *, name):
    r, c = xl.shape

    def body(x_ref, out_ref, send_sems, recv_sems, local_sem):
        x, y, cc = _mesh_pos()
        me, sibling = (x, y, cc), (x, y, 1 - cc)
        chips = [(1 - x, y), (x, 1 - y), (1 - x, 1 - y)]

        def rows(px, py, pc):
            return out_ref.at[pl.ds((4 * px + 2 * py + pc) * r, r), :]

        def copy(k, block, to, src=None):
            return pltpu.make_async_remote_copy(
                src_ref=rows(*block) if src is None else src, dst_ref=rows(*block),
                send_sem=send_sems.at[k], recv_sem=recv_sems.at[k], device_id=to, device_id_type=MESH_ID)

        mine = pltpu.make_async_copy(x_ref, rows(*me), local_sem)
        mine.start()
        first = [copy(0, me, sibling, src=x_ref)]
        first += [copy(1 + j, me, (*chip, cc), src=x_ref) for j, chip in enumerate(chips)]
        for cp in first:
            cp.start()
        passed = [copy(4 + j, (*chip, cc), sibling) for j, chip in enumerate(chips)]
        for j, chip in enumerate(chips):
            copy(1 + j, (*chip, cc), me).wait_recv()
            passed[j].start()
        copy(0, sibling, me).wait_recv()
        for j, chip in enumerate(chips):
            copy(4 + j, (*chip, 1 - cc), me).wait_recv()
        for cp in first + passed:
            cp.wait_send()
        mine.wait()

    hbm = pl.BlockSpec(memory_space=pltpu.HBM)
    return pl.pallas_call(
        body, name=name, in_specs=[hbm], out_specs=hbm,
        out_shape=jax.ShapeDtypeStruct((N_DEV * r, c), xl.dtype),
        scratch_shapes=[pltpu.SemaphoreType.DMA((7,)), pltpu.SemaphoreType.DMA((7,)), pltpu.SemaphoreType.DMA],
    )(xl)


def _all_reduce_small(xl, *, name):
    r, c = xl.shape

    def body(x_ref, sum_ref, all_ref, send_sems, recv_sems, local_sem):
        x, y, cc = _mesh_pos()
        me, sibling = (x, y, cc), (x, y, 1 - cc)
        chips = [(1 - x, y), (x, 1 - y), (1 - x, 1 - y)]

        def rows(px, py, pc):
            return all_ref.at[pl.ds((4 * px + 2 * py + pc) * r, r), :]

        def copy(k, block, to, src=None):
            return pltpu.make_async_remote_copy(
                src_ref=rows(*block) if src is None else src, dst_ref=rows(*block),
                send_sem=send_sems.at[k], recv_sem=recv_sems.at[k], device_id=to, device_id_type=MESH_ID)

        mine = pltpu.make_async_copy(x_ref, rows(*me), local_sem)
        mine.start()
        first = [copy(0, me, sibling, src=x_ref)]
        first += [copy(1 + j, me, (*chip, cc), src=x_ref) for j, chip in enumerate(chips)]
        for cp in first:
            cp.start()
        passed = [copy(4 + j, (*chip, cc), sibling) for j, chip in enumerate(chips)]
        for j, chip in enumerate(chips):
            copy(1 + j, (*chip, cc), me).wait_recv()
            passed[j].start()
        copy(0, sibling, me).wait_recv()
        for j, chip in enumerate(chips):
            copy(4 + j, (*chip, 1 - cc), me).wait_recv()
        for cp in first + passed:
            cp.wait_send()
        mine.wait()
        tot = all_ref[pl.ds(0, r), :]
        for dd in range(1, N_DEV):
            tot = tot + all_ref[pl.ds(dd * r, r), :]
        sum_ref[...] = tot

    vm = pl.BlockSpec(memory_space=pltpu.VMEM)
    return pl.pallas_call(
        body, name=name, in_specs=[vm], out_specs=[vm, vm],
        out_shape=[jax.ShapeDtypeStruct((r, c), F32), jax.ShapeDtypeStruct((N_DEV * r, c), F32)],
        scratch_shapes=[pltpu.SemaphoreType.DMA((7,)), pltpu.SemaphoreType.DMA((7,)), pltpu.SemaphoreType.DMA],
    )(xl)[0]


def _grad_exchange(gp, *, name):
    _, r, c = gp.shape

    def body(g_ref, recv_ref, send_sems, recv_sems, local_sem):
        x, y, cc = _mesh_pos()
        me = 4 * x + 2 * y + cc
        mine = pltpu.make_async_copy(g_ref.at[me], recv_ref.at[me], local_sem)
        mine.start()
        sends, lands = [], []
        for rel in range(1, N_DEV):
            px, py, pc = _flip(x, rel & 4), _flip(y, rel & 2), _flip(cc, rel & 1)
            peer = 4 * px + 2 * py + pc
            sends.append(pltpu.make_async_remote_copy(
                src_ref=g_ref.at[peer], dst_ref=recv_ref.at[me], send_sem=send_sems.at[rel - 1],
                recv_sem=recv_sems.at[rel - 1], device_id=(px, py, pc), device_id_type=MESH_ID))
            lands.append(pltpu.make_async_remote_copy(
                src_ref=g_ref.at[me], dst_ref=recv_ref.at[peer], send_sem=send_sems.at[rel - 1],
                recv_sem=recv_sems.at[rel - 1], device_id=(px, py, pc), device_id_type=MESH_ID))
        for cp in sends:
            cp.start()
        for cp in lands:
            cp.wait_recv()
        for cp in sends:
            cp.wait_send()
        mine.wait()

    hbm = pl.BlockSpec(memory_space=pltpu.HBM)
    return pl.pallas_call(
        body, name=name, in_specs=[hbm], out_specs=hbm,
        out_shape=jax.ShapeDtypeStruct(gp.shape, gp.dtype),
        scratch_shapes=[pltpu.SemaphoreType.DMA((7,)), pltpu.SemaphoreType.DMA((7,)), pltpu.SemaphoreType.DMA],
    )(gp)


def _sum_blocks(recv, *, name):
    nd, r, c = recv.shape
    tr = _pick(r, (448, 256, 128, 64, 32, 16))

    def body(x_ref, o_ref):
        tot = x_ref[0].astype(F32)
        for dd in range(1, nd):
            tot = tot + x_ref[dd].astype(F32)
        o_ref[...] = tot

    return pl.pallas_call(
        body, name=name, grid=(r // tr,),
        in_specs=[pl.BlockSpec((nd, tr, c), lambda i: (0, i, 0))],
        out_specs=pl.BlockSpec((tr, c), lambda i: (i, 0)),
        out_shape=jax.ShapeDtypeStruct((r, c), F32),
        compiler_params=_cp("parallel"),
    )(recv)


BIG_WEIGHTS = (("a_w_in", "col"), ("a_w_out", "row"), ("a_w_mem_kv", "row"), ("a_ffn_up", "col"),
               ("a_ffn_down", "row"), ("w_kv_shared", "col"), ("b_w_in", "row"), ("b_w_out", "row"),
               ("b_w_mem_kv", "row"), ("b_ffn_up", "col"), ("b_ffn_down", "row"))


def _as2d(a):
    return a.reshape(a.shape[-2], a.shape[-1]) if a.ndim >= 2 else a.reshape(1, a.shape[0])


def _pack_local(shards):
    return jnp.concatenate([_as2d(s).astype(BF16).reshape(-1, PACK_COLS) for s in shards], axis=0)


def _unpack_full(gathered, local_shapes):
    out = {}
    r0 = 0
    for (name, kind), (rows, cols) in zip(BIG_WEIGHTS, local_shapes):
        nr = rows * cols // PACK_COLS
        blk = gathered[:, r0:r0 + nr, :].reshape(N_DEV, rows, cols)
        if kind == "row":
            out[name] = blk.reshape(N_DEV * rows, cols)
        else:
            out[name] = blk.transpose(1, 0, 2).reshape(rows, N_DEV * cols)
        r0 += nr
    return out


def _pack_grads(grads, local_shapes):
    parts = []
    for (name, kind), (rows, cols) in zip(BIG_WEIGHTS, local_shapes):
        g = grads[name]
        if kind == "row":
            blk = g.reshape(N_DEV, rows, cols)
        else:
            blk = g.reshape(rows, N_DEV, cols).transpose(1, 0, 2)
        parts.append(blk.astype(BF16).reshape(N_DEV, rows * cols // PACK_COLS, PACK_COLS))
    return jnp.concatenate(parts, axis=1)


def _by_residue(t, d):
    if d == 1:
        return t
    b, s, c = t.shape
    return t.reshape(b, s // d, d, c).transpose(0, 2, 1, 3).reshape(b * d, s // d, c)


def _from_residue(t, d, b):
    if d == 1:
        return t
    n, l, c = t.shape
    return t.reshape(b, d, l, c).transpose(0, 2, 1, 3).reshape(b, l * d, c)


def _alibi_slopes():
    return [2.0 ** (-ALIBI_MAX_BIAS * (i + 1) / N_DIL_HEADS) for i in range(N_DIL_HEADS)]


def _conv_ffn_fwd(xin, gain, w_up, wconv, w_down, tag, b, s):
    (n,), r = _rms_fwd(xin, [gain], name=f"{tag}_rms_ffn")
    u = _mm(n, w_up, name=f"{tag}_up", out_dtype=BF16).reshape(b, s, -1)
    hmid = _ffn_mid_fwd(u, wconv, name=f"{tag}_ffn_mid").reshape(b * s, -1)
    xout = _mm(hmid, w_down, name=f"{tag}_down", out_dtype=F32, res=xin)
    return xout, (n, r, u, hmid)


def _conv_ffn_bwd(dxout, xin, gain, saved, w_up, w_up_t, wconv, w_down_t, tag, b, s):
    n, r, u, hmid = saved
    f = hmid.shape[1]
    dhmid = _mm(dxout, w_down_t, name=f"{tag}_d_hmid", out_dtype=BF16)
    g_down = _mm(hmid, dxout, name=f"{tag}_g_down", out_dtype=F32, trans_a=True)
    du_a, du_g, gc_a, gc_g = _ffn_mid_bwd(u, wconv, dhmid.reshape(b, s, f), name=f"{tag}_ffn_mid_bwd")
    du_a = du_a.reshape(b * s, f)
    du_g = du_g.reshape(b * s, f)
    dn = _mm(du_a, w_up_t[:f], name=f"{tag}_d_n_a", out_dtype=F32)
    dn = _mm(du_g, w_up_t[f:], name=f"{tag}_d_n_g", out_dtype=F32, res=dn)
    g_up = jnp.concatenate([_mm(n, du_a, name=f"{tag}_g_up_a", out_dtype=F32, trans_a=True),
                            _mm(n, du_g, name=f"{tag}_g_up_g", out_dtype=F32, trans_a=True)], axis=1)
    dxin, (g_gain,) = _rms_bwd(xin, r, [(dn, gain)], dxout, name=f"{tag}_rms_ffn_bwd")
    return dxin, g_up, g_down, jnp.concatenate([gc_a, gc_g], axis=1), g_gain


def _mem_kv_fwd(mem2d, gain, w_mem_kv, tag, b):
    (nm,), rm = _rms_fwd(mem2d, [gain], name=f"{tag}_rms_mem")
    kvm = _mm(nm, w_mem_kv, name=f"{tag}_mem_kv", out_dtype=BF16)
    return kvm.reshape(b, -1, 2 * MEM_WIDTH), (nm, rm)


def _mem_kv_bwd(dk, dv, mem2d, gain, saved, w_mem_kv_t, tag):
    nm, rm = saved
    dkvm = jnp.concatenate([dk, dv], axis=-1).reshape(-1, 2 * MEM_WIDTH)
    dnm = _mm(dkvm, w_mem_kv_t, name=f"{tag}_d_nm", out_dtype=F32)
    g_w = _mm(nm, dkvm, name=f"{tag}_g_mem_kv", out_dtype=F32, trans_a=True)
    _, (g_gain,) = _rms_bwd(mem2d, rm, [(dnm, gain)], None, name=f"{tag}_rms_mem_bwd", need_dx=False)
    return g_w, g_gain


def kernel(x, mem, a_norm_attn, a_w_in, a_w_out, a_norm_mem, a_w_mem_kv, a_norm_ffn, a_ffn_up, a_ffn_conv, a_ffn_down, kv_norm, w_kv_shared, b_norm_attn, b_w_in, b_w_out, b_norm_mem, b_w_mem_kv, b_norm_ffn, b_ffn_up, b_ffn_conv, b_ffn_down, final_norm, loss_target, m_a_norm_attn, m_a_w_in, m_a_w_out, m_a_norm_mem, m_a_w_mem_kv, m_a_norm_ffn, m_a_ffn_up, m_a_ffn_conv, m_a_ffn_down, m_kv_norm, m_w_kv_shared, m_b_norm_attn, m_b_w_in, m_b_w_out, m_b_norm_mem, m_b_w_mem_kv, m_b_norm_ffn, m_b_ffn_up, m_b_ffn_conv, m_b_ffn_down, m_final_norm, v_a_norm_attn, v_a_w_in, v_a_w_out, v_a_norm_mem, v_a_w_mem_kv, v_a_norm_ffn, v_a_ffn_up, v_a_ffn_conv, v_a_ffn_down, v_kv_norm, v_w_kv_shared, v_b_norm_attn, v_b_w_in, v_b_w_out, v_b_norm_mem, v_b_w_mem_kv, v_b_norm_ffn, v_b_ffn_up, v_b_ffn_conv, v_b_ffn_down, v_final_norm):
    names = ["a_norm_attn", "a_w_in", "a_w_out", "a_norm_mem", "a_w_mem_kv", "a_norm_ffn", "a_ffn_up",
             "a_ffn_conv", "a_ffn_down", "kv_norm", "w_kv_shared", "b_norm_attn", "b_w_in", "b_w_out",
             "b_norm_mem", "b_w_mem_kv", "b_norm_ffn", "b_ffn_up", "b_ffn_conv", "b_ffn_down", "final_norm"]
    wl = dict(zip(names, [a_norm_attn, a_w_in, a_w_out, a_norm_mem, a_w_mem_kv, a_norm_ffn, a_ffn_up,
                          a_ffn_conv, a_ffn_down, kv_norm, w_kv_shared, b_norm_attn, b_w_in, b_w_out,
                          b_norm_mem, b_w_mem_kv, b_norm_ffn, b_ffn_up, b_ffn_conv, b_ffn_down, final_norm]))
    ml = dict(zip(names, [m_a_norm_attn, m_a_w_in, m_a_w_out, m_a_norm_mem, m_a_w_mem_kv, m_a_norm_ffn,
                          m_a_ffn_up, m_a_ffn_conv, m_a_ffn_down, m_kv_norm, m_w_kv_shared, m_b_norm_attn,
                          m_b_w_in, m_b_w_out, m_b_norm_mem, m_b_w_mem_kv, m_b_norm_ffn, m_b_ffn_up,
                          m_b_ffn_conv, m_b_ffn_down, m_final_norm]))
    vl = dict(zip(names, [v_a_norm_attn, v_a_w_in, v_a_w_out, v_a_norm_mem, v_a_w_mem_kv, v_a_norm_ffn,
                          v_a_ffn_up, v_a_ffn_conv, v_a_ffn_down, v_kv_norm, v_w_kv_shared, v_b_norm_attn,
                          v_b_w_in, v_b_w_out, v_b_norm_mem, v_b_w_mem_kv, v_b_norm_ffn, v_b_ffn_up,
                          v_b_ffn_conv, v_b_ffn_down, v_final_norm]))
    b, s, d = x.shape
    t = b * s
    my_x, my_y, my_c = _mesh_pos()
    me = 4 * my_x + 2 * my_y + my_c

    local_shapes = [_as2d(wl[nm]).shape for nm, _ in BIG_WEIGHTS]
    packed = _pack_local([wl[nm] for nm, _ in BIG_WEIGHTS])
    nrows = packed.shape[0]
    gathered = _all_gather_hbm(packed, name="gather_weights").reshape(N_DEV, nrows, PACK_COLS)
    wf = _unpack_full(gathered, local_shapes)
    wt = {nm: wf[nm].T for nm in wf}

    sharded_small = ["a_norm_attn", "a_norm_mem", "a_norm_ffn", "a_ffn_conv", "b_ffn_conv"]
    small_flat = jnp.concatenate([wl[nm].reshape(-1) for nm in sharded_small])
    n_small = small_flat.shape[0]
    small_rows = -(-n_small // (8 * LANE)) * 8
    small_local = jnp.pad(small_flat, (0, small_rows * LANE - n_small)).reshape(small_rows, LANE)
    small_all = _all_gather_hbm(small_local, name="gather_small").reshape(N_DEV, small_rows * LANE)
    sfull = {}
    r0 = 0
    for nm in sharded_small:
        rows, cols = _as2d(wl[nm]).shape
        blk = small_all[:, r0:r0 + rows * cols].reshape(N_DEV, rows, cols)
        sfull[nm] = blk.transpose(1, 0, 2).reshape(rows, N_DEV * cols)
        r0 += rows * cols
    gain = {nm: sfull[nm] for nm in ("a_norm_attn", "a_norm_mem", "a_norm_ffn")}
    for nm in ("kv_norm", "b_norm_attn", "b_norm_mem", "b_norm_ffn", "final_norm"):
        gain[nm] = _as2d(wl[nm])
    conv_a, conv_b = sfull["a_ffn_conv"], sfull["b_ffn_conv"]

    x2d = x.reshape(t, d)
    mem2d = mem.reshape(-1, d)
    tgt2d = loss_target.reshape(t, d)
    qmem_blk_a = 3 * SB_WIDTH // GRP
    qmem_blk_b = DIL_WIDTH // GRP

    (n1,), r1 = _rms_fwd(x2d, [gain["a_norm_attn"]], name="a_rms_attn")
    proj_a = _mm(n1, wf["a_w_in"], name="a_in", out_dtype=BF16).reshape(b, s, -1)
    kvm_a, mem_saved_a = _mem_kv_fwd(mem2d, gain["a_norm_mem"], wf["a_w_mem_kv"], "a", b)
    o_sb, rsum = _sb_fwd(proj_a, name="a_sb_fwd")
    o_mem_a, lse_mem_a = _attn_fwd(proj_a, kvm_a, kvm_a, name="a_mem_fwd", banded=False,
                                   q_lane_blk=qmem_blk_a, k_lane_blk=0, v_lane_blk=1, out_dtype=BF16)
    cat_a = jnp.concatenate([o_sb, o_mem_a], axis=-1).reshape(t, d)
    x1 = _mm(cat_a, wf["a_w_out"], name="a_out", out_dtype=F32, res=x2d)
    xa, ffn_saved_a = _conv_ffn_fwd(x1, gain["a_norm_ffn"], wf["a_ffn_up"], conv_a, wf["a_ffn_down"], "a", b, s)

    (nk, n3), r3 = _rms_fwd(xa, [gain["kv_norm"], gain["b_norm_attn"]], name="b_rms_attn")
    kvsh = _mm(nk, wf["w_kv_shared"], name="kv_shared", out_dtype=BF16).reshape(b, s, -1)
    proj_b = _mm(n3, wf["b_w_in"], name="b_in", out_dtype=BF16).reshape(b, s, -1)
    kvm_b, mem_saved_b = _mem_kv_fwd(mem2d, gain["b_norm_mem"], wf["b_w_mem_kv"], "b", b)
    slopes = _alibi_slopes()
    dil_q, dil_k, dil_v, dil_o, dil_lse, dil_slopes = [], [], [], [], [], []
    for g, (_, dil) in enumerate(DIL_GROUPS):
        qg = _by_residue(proj_b[:, :, GRP * g:GRP * (g + 1)], dil)
        kg = _by_residue(kvsh[:, :, GRP * g:GRP * (g + 1)], dil)
        vg = _by_residue(kvsh[:, :, DIL_WIDTH + GRP * g:DIL_WIDTH + GRP * (g + 1)], dil)
        sl = [slopes[4 * g + h] * dil for h in range(4)]
        og, lg = _attn_fwd(qg, kg, vg, name=f"b_dil{g}_fwd", banded=True, slopes_scaled=sl)
        dil_q.append(qg)
        dil_k.append(kg)
        dil_v.append(vg)
        dil_slopes.append(sl)
        dil_o.append(_from_residue(og, dil, b).reshape(t, GRP))
        dil_lse.append(_from_residue(lg, dil, b).reshape(t, GRP))
    o_dil, lse_joint = _dil_combine(dil_o, dil_lse, name="b_dil_combine")
    o_mem_b, lse_mem_b = _attn_fwd(proj_b, kvm_b, kvm_b, name="b_mem_fwd", banded=False,
                                   q_lane_blk=qmem_blk_b, k_lane_blk=0, v_lane_blk=1, out_dtype=BF16)
    cat_b = jnp.concatenate([o_dil, o_mem_b.reshape(t, MEM_WIDTH)], axis=-1)
    x3 = _mm(cat_b, wf["b_w_out"], name="b_out", out_dtype=F32, res=xa)
    xb, ffn_saved_b = _conv_ffn_fwd(x3, gain["b_norm_ffn"], wf["b_ffn_up"], conv_b, wf["b_ffn_down"], "b", b, s)

    dxb, g_final, loss_vec = _loss_head(xb, gain["final_norm"], tgt2d, name="loss_head")

    grads = {}
    sgrads = {"final_norm": g_final}
    dx3, grads["b_ffn_up"], grads["b_ffn_down"], sgrads["b_ffn_conv"], sgrads["b_norm_ffn"] = _conv_ffn_bwd(
        dxb, x3, gain["b_norm_ffn"], ffn_saved_b, wf["b_ffn_up"], wt["b_ffn_up"], conv_b, wt["b_ffn_down"],
        "b", b, s)
    dcat_b = _mm(dx3, wt["b_w_out"], name="b_d_cat", out_dtype=BF16)
    grads["b_w_out"] = _mm(cat_b, dx3, name="b_g_out", out_dtype=F32, trans_a=True)
    dcat_b3 = dcat_b.reshape(b, s, d)
    delta_mem_b = _attn_delta(dcat_b, cat_b, name="b_mem_delta", lane_blks=[qmem_blk_b]).reshape(b, s, GRP)
    dq_mem_b, dkm_b, dvm_b = _attn_bwd(proj_b, kvm_b, kvm_b, dcat_b3, lse_mem_b, delta_mem_b, name="b_mem_bwd",
                                       banded=False, q_lane_blk=qmem_blk_b, k_lane_blk=0, v_lane_blk=1,
                                       do_lane_blk=qmem_blk_b)
    delta_dil = _attn_delta(dcat_b, cat_b, name="b_dil_delta", lane_blks=[0, 1, 2]).reshape(b, s, GRP)
    lse_joint3 = lse_joint.reshape(b, s, GRP)
    dq_parts, dk_parts, dv_parts = [], [], []
    for g, (_, dil) in enumerate(DIL_GROUPS):
        dog = _by_residue(dcat_b3[:, :, GRP * g:GRP * (g + 1)], dil)
        lg = _by_residue(lse_joint3, dil)
        dg = _by_residue(delta_dil, dil)
        dqg, dkg, dvg = _attn_bwd(dil_q[g], dil_k[g], dil_v[g], dog, lg, dg, name=f"b_dil{g}_bwd", banded=True,
                                  slopes_scaled=dil_slopes[g])
        dq_parts.append(_from_residue(dqg, dil, b))
        dk_parts.append(_from_residue(dkg, dil, b))
        dv_parts.append(_from_residue(dvg, dil, b))
    dproj_b = jnp.concatenate(dq_parts + [dq_mem_b], axis=-1).reshape(t, d)
    dn3 = _mm(dproj_b, wt["b_w_in"], name="b_d_n", out_dtype=F32)
    grads["b_w_in"] = _mm(n3, dproj_b, name="b_g_in", out_dtype=F32, trans_a=True)
    grads["b_w_mem_kv"], sgrads["b_norm_mem"] = _mem_kv_bwd(dkm_b, dvm_b, mem2d, gain["b_norm_mem"], mem_saved_b,
                                                           wt["b_w_mem_kv"], "b")
    dkvsh = jnp.concatenate(dk_parts + dv_parts, axis=-1).reshape(t, 2 * DIL_WIDTH).astype(BF16)
    dnk = _mm(dkvsh, wt["w_kv_shared"], name="kv_d_n", out_dtype=F32)
    grads["w_kv_shared"] = _mm(nk, dkvsh, name="kv_g", out_dtype=F32, trans_a=True)
    dxa, (sgrads["kv_norm"], sgrads["b_norm_attn"]) = _rms_bwd(
        xa, r3, [(dnk, gain["kv_norm"]), (dn3, gain["b_norm_attn"])], dx3, name="b_rms_attn_bwd")

    dx1, grads["a_ffn_up"], grads["a_ffn_down"], sgrads["a_ffn_conv"], sgrads["a_norm_ffn"] = _conv_ffn_bwd(
        dxa, x1, gain["a_norm_ffn"], ffn_saved_a, wf["a_ffn_up"], wt["a_ffn_up"], conv_a, wt["a_ffn_down"],
        "a", b, s)
    dcat_a = _mm(dx1, wt["a_w_out"], name="a_d_cat", out_dtype=BF16)
    grads["a_w_out"] = _mm(cat_a, dx1, name="a_g_out", out_dtype=F32, trans_a=True)
    dcat_a3 = dcat_a.reshape(b, s, d)
    delta_mem_a = _attn_delta(dcat_a, cat_a, name="a_mem_delta", lane_blks=[qmem_blk_b]).reshape(b, s, GRP)
    dq_mem_a, dkm_a, dvm_a = _attn_bwd(proj_a, kvm_a, kvm_a, dcat_a3, lse_mem_a, delta_mem_a, name="a_mem_bwd",
                                       banded=False, q_lane_blk=qmem_blk_a, k_lane_blk=0, v_lane_blk=1,
                                       do_lane_blk=qmem_blk_b)
    dq_sb, dk_sb, dv_sb = _sb_bwd(proj_a, dcat_a3, rsum, name="a_sb_bwd")
    dproj_a = jnp.concatenate([dq_sb, dk_sb.astype(BF16), dv_sb.astype(BF16), dq_mem_a], axis=-1).reshape(t, -1)
    dn1 = _mm(dproj_a, wt["a_w_in"], name="a_d_n", out_dtype=F32)
    grads["a_w_in"] = _mm(n1, dproj_a, name="a_g_in", out_dtype=F32, trans_a=True)
    grads["a_w_mem_kv"], sgrads["a_norm_mem"] = _mem_kv_bwd(dkm_a, dvm_a, mem2d, gain["a_norm_mem"], mem_saved_a,
                                                           wt["a_w_mem_kv"], "a")
    dx0, (sgrads["a_norm_attn"],) = _rms_bwd(x2d, r1, [(dn1, gain["a_norm_attn"])], dx1, name="a_rms_attn_bwd")
    grad_x = dx0.reshape(b, s, d)

    recv = _grad_exchange(_pack_grads(grads, local_shapes), name="exchange_grads")
    gsum = _sum_blocks(recv, name="sum_grads")
    gl = {}
    r0 = 0
    for (nm, _), (rows, cols) in zip(BIG_WEIGHTS, local_shapes):
        nr = rows * cols // PACK_COLS
        gl[nm] = gsum[r0:r0 + nr].reshape(rows, cols)
        r0 += nr

    small_names = ["a_norm_attn", "a_norm_mem", "a_norm_ffn", "kv_norm", "b_norm_attn", "b_norm_mem",
                   "b_norm_ffn", "final_norm", "a_ffn_conv", "b_ffn_conv"]
    small_flat = jnp.concatenate([sgrads[nm].reshape(-1) for nm in small_names] + [loss_vec.reshape(-1)])
    n_flat = small_flat.shape[0]
    red_rows = -(-n_flat // (8 * PACK_COLS)) * 8
    small_pack = jnp.pad(small_flat, (0, red_rows * PACK_COLS - n_flat)).reshape(red_rows, PACK_COLS)
    small_sum = _all_reduce_small(small_pack, name="reduce_small").reshape(-1)
    r0 = 0
    for nm in small_names:
        rows, cols = sgrads[nm].shape
        full = small_sum[r0:r0 + rows * cols].reshape(rows, cols)
        r0 += rows * cols
        if nm in sharded_small:
            lc = cols // N_DEV
            gl[nm] = lax.dynamic_slice(full, (0, me * lc), (rows, lc))
        else:
            gl[nm] = full
    loss = (0.5 / d) * jnp.sum(small_sum[r0:r0 + d])

    upd = {}
    for nm, _ in BIG_WEIGHTS:
        upd[nm] = _adam(_as2d(wl[nm]), gl[nm], _as2d(ml[nm]), _as2d(vl[nm]), name=f"adam_{nm}")
    res_small = _adam_small([(_as2d(wl[nm]), gl[nm], _as2d(ml[nm]), _as2d(vl[nm])) for nm in small_names],
                            name="adam_small")
    for nm, r in zip(small_names, res_small):
        upd[nm] = r

    g_out = [gl[nm].reshape(wl[nm].shape) for nm in names]
    d_out = [upd[nm][0].reshape(wl[nm].shape) for nm in names]
    m_out = [upd[nm][1].reshape(wl[nm].shape) for nm in names]
    v_out = [upd[nm][2].reshape(wl[nm].shape) for nm in names]
    return (loss, grad_x, *g_out, *d_out, *m_out, *v_out)
```

```python
import functools
import math

import jax
import jax.numpy as jnp
from jax import lax
from jax.experimental import pallas as pl
from jax.experimental.pallas import tpu as pltpu

F32 = jnp.float32
BF16 = jnp.bfloat16

N_DEV = 8
HEAD_DIM = 64
N_SB_HEADS = 12
N_DIL_HEADS = 12
DIL_GROUPS = ((128, 1), (512, 4), (2048, 16))
SB_WIDTH = N_SB_HEADS * HEAD_DIM
MEM_WIDTH = 256
DIL_WIDTH = N_DIL_HEADS * HEAD_DIM
ATT_SCALE = HEAD_DIM ** -0.5
EPS = 1e-6
ALIBI_MAX_BIAS = 8.0
NEG_BIG = -1e30

ADAM_LR = 0.001
ADAM_B1 = 0.9
ADAM_B2 = 0.999
ADAM_EPS = 1e-08
ADAM_WD = 0.01
ADAM_STEP = 10

LANE = 128
QBLK = 128
VMEM_LIMIT_BYTES = 48 * 1024 * 1024
PACK_COLS = 1024
MESH_ID = pl.DeviceIdType.MESH


def _cp(*sem):
    return pltpu.CompilerParams(dimension_semantics=sem, vmem_limit_bytes=VMEM_LIMIT_BYTES)


def _pick(n, cands):
    for c in cands:
        if n % c == 0:
            return c
    raise ValueError(f"no tile for {n} in {cands}")


def _dot(a, b):
    return jnp.dot(a, b, preferred_element_type=F32)


def _dot_nt(a, b):
    return lax.dot_general(a, b, (((1,), (1,)), ((), ())), preferred_element_type=F32)


def _dot_tn(a, b):
    return lax.dot_general(a, b, (((0,), (0,)), ((), ())), preferred_element_type=F32)


def _dot_split(x, u):
    hi = x.astype(BF16)
    lo = (x - hi.astype(F32)).astype(BF16)
    return _dot(hi, u) + _dot(lo, u)


def _mm(a, b, *, name, out_dtype, res=None, trans_a=False, trans_b=False):
    assert not (trans_a and trans_b)
    if trans_a:
        kdim, m = a.shape
    else:
        m, kdim = a.shape
    if trans_b:
        n, kb = b.shape
    else:
        kb, n = b.shape
    assert kb == kdim, (a.shape, b.shape)
    if trans_a:
        tm = _pick(m, (1408, 1024, 512, 256, 128))
        tn = _pick(n, (1024, 1280, 1408, 768, 512, 256, 128))
        tk = _pick(kdim, (512, 256))
    else:
        tm = _pick(m, (1024, 512, 256, 128))
        tn = _pick(n, (512, 1408, 256, 128))
        tk = kdim if kdim <= 2048 else _pick(kdim, (2048, 1536, 1408, 1280, 1024, 512))
    nk = kdim // tk
    has_res = res is not None

    def body(*refs):
        if has_res:
            a_ref, b_ref, r_ref, o_ref = refs[:4]
            scr = refs[4:]
        else:
            a_ref, b_ref, o_ref = refs[:3]
            r_ref = None
            scr = refs[3:]
        av = a_ref[...].astype(BF16)
        bv = b_ref[...].astype(BF16)
        if trans_a:
            p = _dot_tn(av, bv)
        elif trans_b:
            p = _dot_nt(av, bv)
        else:
            p = _dot(av, bv)

        def finish(acc):
            if has_res:
                acc = acc + r_ref[...]
            o_ref[...] = acc.astype(o_ref.dtype)

        if nk == 1:
            finish(p)
        else:
            acc_ref = scr[0]
            k = pl.program_id(2)

            @pl.when(k == 0)
            def _():
                acc_ref[...] = p

            @pl.when(k > 0)
            def _():
                acc_ref[...] += p

            @pl.when(k == nk - 1)
            def _():
                finish(acc_ref[...])

    if trans_a:
        a_spec = pl.BlockSpec((tk, tm), lambda i, j, k: (k, i))
    else:
        a_spec = pl.BlockSpec((tm, tk), lambda i, j, k: (i, k))
    if trans_b:
        b_spec = pl.BlockSpec((tn, tk), lambda i, j, k: (j, k))
    else:
        b_spec = pl.BlockSpec((tk, tn), lambda i, j, k: (k, j))
    in_specs = [a_spec, b_spec]
    args = [a, b]
    if has_res:
        in_specs.append(pl.BlockSpec((tm, tn), lambda i, j, k: (i, j)))
        args.append(res)
    return pl.pallas_call(
        body, name=name,
        grid=(m // tm, n // tn, nk),
        in_specs=in_specs,
        out_specs=pl.BlockSpec((tm, tn), lambda i, j, k: (i, j)),
        out_shape=jax.ShapeDtypeStruct((m, n), out_dtype),
        scratch_shapes=[pltpu.VMEM((tm, tn), F32)] if nk > 1 else [],
        compiler_params=_cp("parallel", "parallel", "arbitrary"),
    )(*args)


def _rms_fwd(x, gains, *, name):
    t, d = x.shape
    tr = _pick(t, (512, 256, 128, 8))
    ng = len(gains)

    def body(x_ref, *rest):
        g_refs, n_refs, r_ref = rest[:ng], rest[ng:2 * ng], rest[2 * ng]
        xv = x_ref[...]
        r = lax.rsqrt(jnp.mean(xv * xv, axis=-1, keepdims=True) + EPS)
        xh = xv * r
        for g_ref, n_ref in zip(g_refs, n_refs):
            n_ref[...] = (xh * g_ref[...]).astype(BF16)
        r_ref[...] = r

    row = pl.BlockSpec((tr, d), lambda i: (i, 0))
    gsp = pl.BlockSpec((1, d), lambda i: (0, 0))
    outs = pl.pallas_call(
        body, name=name, grid=(t // tr,),
        in_specs=[row] + [gsp] * ng,
        out_specs=[row] * ng + [pl.BlockSpec((tr, 1), lambda i: (i, 0))],
        out_shape=[jax.ShapeDtypeStruct((t, d), BF16)] * ng + [jax.ShapeDtypeStruct((t, 1), F32)],
        compiler_params=_cp("parallel"),
    )(x, *gains)
    return list(outs[:ng]), outs[ng]


def _rms_bwd(x, r, pairs, dres, *, name, need_dx=True):
    t, d = x.shape
    tr = _pick(t, (512, 256, 128, 8))
    npair = len(pairs)
    has_res = dres is not None

    def body(*refs):
        x_ref, r_ref = refs[:2]
        pr = refs[2:2 + 2 * npair]
        pos = 2 + 2 * npair
        res_ref = None
        if has_res:
            res_ref = refs[pos]
            pos += 1
        dx_ref = None
        if need_dx:
            dx_ref = refs[pos]
            pos += 1
        dg_refs = refs[pos:pos + npair]
        i = pl.program_id(0)
        rv = r_ref[...]
        xh = x_ref[...] * rv
        dx = res_ref[...] if has_res else None
        for k in range(npair):
            dn = pr[2 * k][...].astype(F32)
            g = pr[2 * k + 1][...]
            part = jnp.sum(dn * xh, axis=0, keepdims=True)

            @pl.when(i == 0)
            def _():
                dg_refs[k][...] = part

            @pl.when(i > 0)
            def _():
                dg_refs[k][...] += part

            if need_dx:
                dxh = dn * g
                c = jnp.mean(dxh * xh, axis=-1, keepdims=True)
                term = rv * (dxh - xh * c)
                dx = term if dx is None else dx + term
        if need_dx:
            dx_ref[...] = dx

    row = pl.BlockSpec((tr, d), lambda i: (i, 0))
    gsp = pl.BlockSpec((1, d), lambda i: (0, 0))
    in_specs = [row, pl.BlockSpec((tr, 1), lambda i: (i, 0))]
    args = [x, r]
    for dn, g in pairs:
        in_specs += [row, gsp]
        args += [dn, g]
    if has_res:
        in_specs.append(row)
        args.append(dres)
    out_specs, out_shape = [], []
    if need_dx:
        out_specs.append(row)
        out_shape.append(jax.ShapeDtypeStruct((t, d), F32))
    out_specs += [gsp] * npair
    out_shape += [jax.ShapeDtypeStruct((1, d), F32)] * npair
    outs = pl.pallas_call(
        body, name=name, grid=(t // tr,), in_specs=in_specs, out_specs=out_specs, out_shape=out_shape,
        compiler_params=_cp("arbitrary"),
    )(*args)
    if need_dx:
        return outs[0], list(outs[1:])
    return None, list(outs)


def _loss_head(h, g, tgt, *, name):
    t, d = h.shape
    tr = _pick(t, (512, 256, 128, 8))

    def body(h_ref, g_ref, t_ref, dh_ref, dg_ref, l_ref):
        i = pl.program_id(0)
        xv = h_ref[...]
        gv = g_ref[...]
        r = lax.rsqrt(jnp.mean(xv * xv, axis=-1, keepdims=True) + EPS)
        xh = xv * r
        e = xh * gv - t_ref[...]
        dy = e * (1.0 / d)
        lpart = jnp.sum(e * e, axis=0, keepdims=True)
        gpart = jnp.sum(dy * xh, axis=0, keepdims=True)

        @pl.when(i == 0)
        def _():
            l_ref[...] = lpart
            dg_ref[...] = gpart

        @pl.when(i > 0)
        def _():
            l_ref[...] += lpart
            dg_ref[...] += gpart

        dxh = dy * gv
        c = jnp.mean(dxh * xh, axis=-1, keepdims=True)
        dh_ref[...] = r * (dxh - xh * c)

    row = pl.BlockSpec((tr, d), lambda i: (i, 0))
    gsp = pl.BlockSpec((1, d), lambda i: (0, 0))
    return pl.pallas_call(
        body, name=name, grid=(t // tr,), in_specs=[row, gsp, row], out_specs=[row, gsp, gsp],
        out_shape=[jax.ShapeDtypeStruct((t, d), F32), jax.ShapeDtypeStruct((1, d), F32),
                   jax.ShapeDtypeStruct((1, d), F32)],
        compiler_params=_cp("arbitrary"),
    )(h, g, tgt)


GRP = 4 * HEAD_DIM
SB_KB = 2 * QBLK


def _head_masks4(shape):
    lane = lax.broadcasted_iota(jnp.int32, shape, 1)
    return [(lane >= HEAD_DIM * h) & (lane < HEAD_DIM * (h + 1)) for h in range(4)]


def _neg_softplus(z):
    return jnp.minimum(-z, 0.0) - jnp.log(1.0 + jnp.exp(-jnp.abs(z)))


def _sb_fwd(proj, *, name):
    b, s, _ = proj.shape
    nq = s // QBLK
    ngrp = SB_WIDTH // GRP

    def body(q_ref, k_ref, v_ref, o_ref, r_ref, acc_ref, car_ref):
        i = pl.program_id(2)
        masks = _head_masks4((QBLK, GRP))
        row = lax.broadcasted_iota(jnp.int32, (SB_KB, SB_KB), 0)
        col = lax.broadcasted_iota(jnp.int32, (SB_KB, SB_KB), 1)
        later_mat = (row > col).astype(BF16)
        col_minus_row = (lax.broadcasted_iota(jnp.int32, (QBLK, SB_KB), 1)
                         - lax.broadcasted_iota(jnp.int32, (QBLK, SB_KB), 0))
        qs = q_ref[0] * jnp.asarray(ATT_SCALE, BF16)
        qh = [jnp.where(mk, qs, jnp.zeros_like(qs)) for mk in masks]
        acc_ref[...] = jnp.zeros_like(acc_ref)
        car_ref[...] = jnp.zeros_like(car_ref)

        def process(jb, masked):
            off = pl.multiple_of(jb * SB_KB, SB_KB)
            k2 = k_ref[0, pl.ds(off, SB_KB), :]
            v2 = v_ref[0, pl.ds(off, SB_KB), :]
            zs = [_dot_nt(qh[h], k2) for h in range(4)]
            if masked:
                causal = col_minus_row < (i * QBLK - jb * SB_KB)
            lss = []
            for h in range(4):
                ls = _neg_softplus(zs[h])
                lss.append(jnp.where(causal, ls, 0.0) if masked else ls)
            laters = [_dot(lss[h].astype(BF16), later_mat) for h in range(4)]
            ws = []
            for h in range(4):
                car = car_ref[h]
                w = jnp.exp((zs[h] + lss[h]) + laters[h] + car)
                if masked:
                    w = jnp.where(causal, w, 0.0)
                ws.append(w.astype(BF16))
                car_ref[h] = car + jnp.sum(lss[h], axis=1, keepdims=True)
            for h in range(4):
                acc_ref[h] += _dot(ws[h], v2)

        top = i // 2
        process(top, True)

        def step(jj, carry):
            process(top - 1 - jj, False)
            return carry

        lax.fori_loop(0, top, step, 0)
        o = acc_ref[0]
        r = car_ref[0]
        for h in range(1, 4):
            o = jnp.where(masks[h], acc_ref[h], o)
            r = jnp.where(masks[h], car_ref[h], r)
        o_ref[0] = o.astype(o_ref.dtype)
        r_ref[0] = r

    blk = pl.BlockSpec((1, QBLK, GRP), lambda bb, p, i: (bb, i, p))
    return pl.pallas_call(
        body, name=name, grid=(b, ngrp, nq),
        in_specs=[blk,
                  pl.BlockSpec((1, s, GRP), lambda bb, p, i: (bb, 0, ngrp + p)),
                  pl.BlockSpec((1, s, GRP), lambda bb, p, i: (bb, 0, 2 * ngrp + p))],
        out_specs=[blk, blk],
        out_shape=[jax.ShapeDtypeStruct((b, s, SB_WIDTH), BF16), jax.ShapeDtypeStruct((b, s, SB_WIDTH), F32)],
        scratch_shapes=[pltpu.VMEM((4, QBLK, GRP), F32), pltpu.VMEM((4, QBLK, SB_KB), F32)],
        compiler_params=_cp("parallel", "parallel", "arbitrary"),
    )(proj, proj, proj)


def _sb_bwd(proj, dcat, rsum, *, name):
    b, s, _ = proj.shape
    nq = s // QBLK
    ngrp = SB_WIDTH // GRP

    def body(q_ref, k_ref, v_ref, do_ref, r_ref, dq_ref, dk_ref, dv_ref, dq_acc, cp_ref, cg_ref):
        i = pl.program_id(2)

        @pl.when(i == 0)
        def _():
            dk_ref[...] = jnp.zeros_like(dk_ref)
            dv_ref[...] = jnp.zeros_like(dv_ref)

        masks = _head_masks4((QBLK, GRP))
        row = lax.broadcasted_iota(jnp.int32, (SB_KB, SB_KB), 0)
        col = lax.broadcasted_iota(jnp.int32, (SB_KB, SB_KB), 1)
        later_mat = (row > col).astype(BF16)
        excl_mat = (row < col).astype(BF16)
        col_minus_row = (lax.broadcasted_iota(jnp.int32, (QBLK, SB_KB), 1)
                         - lax.broadcasted_iota(jnp.int32, (QBLK, SB_KB), 0))
        qs = q_ref[0] * jnp.asarray(ATT_SCALE, BF16)
        do = do_ref[0]
        qh = [jnp.where(mk, qs, jnp.zeros_like(qs)) for mk in masks]
        doh = [jnp.where(mk, do, jnp.zeros_like(do)) for mk in masks]
        q_stack = jnp.concatenate(qh, axis=0)
        do_stack = jnp.concatenate(doh, axis=0)
        rv = r_ref[0]
        rtot = [rv[:, HEAD_DIM * h:HEAD_DIM * h + 1] for h in range(4)]
        dq_acc[...] = jnp.zeros_like(dq_acc)
        cp_ref[...] = jnp.zeros_like(cp_ref)
        cg_ref[...] = jnp.zeros_like(cg_ref)

        def process(jb, masked):
            off = pl.multiple_of(jb * SB_KB, SB_KB)
            k2 = k_ref[0, pl.ds(off, SB_KB), :]
            v2 = v_ref[0, pl.ds(off, SB_KB), :]
            zs = [_dot_nt(qh[h], k2) for h in range(4)]
            dws = [_dot_nt(doh[h], v2) for h in range(4)]
            if masked:
                causal = col_minus_row < (i * QBLK - jb * SB_KB)
            lss, lsigs = [], []
            for h in range(4):
                ls = _neg_softplus(zs[h])
                lsigs.append(zs[h] + ls)
                lss.append(jnp.where(causal, ls, 0.0) if masked else ls)
            laters = [_dot(lss[h].astype(BF16), later_mat) for h in range(4)]
            ws, gs = [], []
            for h in range(4):
                cpv = cp_ref[h] + jnp.sum(lss[h], axis=1, keepdims=True)
                w = jnp.exp(lsigs[h] + ((rtot[h] - cpv) + laters[h]))
                if masked:
                    w = jnp.where(causal, w, 0.0)
                ws.append(w)
                gs.append(dws[h] * w)
                cp_ref[h] = cpv
            gpres = [_dot(gs[h].astype(BF16), excl_mat) for h in range(4)]
            dzs = []
            for h in range(4):
                cgv = cg_ref[h]
                sig = jnp.exp(lsigs[h])
                dz = gs[h] - sig * (gs[h] + (gpres[h] + cgv))
                if masked:
                    dz = jnp.where(causal, dz, 0.0)
                dzs.append(dz.astype(BF16))
                cg_ref[h] = cgv + jnp.sum(gs[h], axis=1, keepdims=True)
            for h in range(4):
                dq_acc[h] += _dot(dzs[h], k2)
            dk_ref[0, pl.ds(off, SB_KB), :] += _dot_tn(jnp.concatenate(dzs, axis=0), q_stack)
            dv_ref[0, pl.ds(off, SB_KB), :] += _dot_tn(
                jnp.concatenate([w.astype(BF16) for w in ws], axis=0), do_stack)

        top = i // 2

        def step(jb, carry):
            process(jb, False)
            return carry

        lax.fori_loop(0, top, step, 0)
        process(top, True)
        dq = dq_acc[0]
        for h in range(1, 4):
            dq = jnp.where(masks[h], dq_acc[h], dq)
        dq_ref[0] = (dq * ATT_SCALE).astype(dq_ref.dtype)

    blk = pl.BlockSpec((1, QBLK, GRP), lambda bb, p, i: (bb, i, p))
    seq = pl.BlockSpec((1, s, GRP), lambda bb, p, i: (bb, 0, p))
    return pl.pallas_call(
        body, name=name, grid=(b, ngrp, nq),
        in_specs=[blk,
                  pl.BlockSpec((1, s, GRP), lambda bb, p, i: (bb, 0, ngrp + p)),
                  pl.BlockSpec((1, s, GRP), lambda bb, p, i: (bb, 0, 2 * ngrp + p)),
                  blk, blk],
        out_specs=[blk, seq, seq],
        out_shape=[jax.ShapeDtypeStruct((b, s, SB_WIDTH), BF16), jax.ShapeDtypeStruct((b, s, SB_WIDTH), F32),
                   jax.ShapeDtypeStruct((b, s, SB_WIDTH), F32)],
        scratch_shapes=[pltpu.VMEM((4, QBLK, GRP), F32), pltpu.VMEM((4, QBLK, SB_KB), F32),
                        pltpu.VMEM((4, QBLK, SB_KB), F32)],
        compiler_params=_cp("parallel", "parallel", "arbitrary"),
    )(proj, proj, proj, dcat, rsum)


def _band_terms(i, slopes_scaled):
    a = lax.broadcasted_iota(jnp.int32, (QBLK, 2 * QBLK), 0)
    bcol = lax.broadcasted_iota(jnp.int32, (QBLK, 2 * QBLK), 1)
    delta = a + QBLK - bcol
    valid = (delta >= 0) & (delta <= QBLK) & ((i > 0) | (bcol >= QBLK))
    dist = delta.astype(F32)
    return valid, [(-sl) * dist for sl in slopes_scaled]


def _attn_specs(banded, q_lane_blk, k_lane_blk, v_lane_blk):
    qs = pl.BlockSpec((1, QBLK, GRP), lambda n, i: (n, i, q_lane_blk))
    if banded:
        ks = [pl.BlockSpec((1, QBLK, GRP), lambda n, i: (n, jnp.maximum(i - 1, 0), k_lane_blk)),
              pl.BlockSpec((1, QBLK, GRP), lambda n, i: (n, i, k_lane_blk))]
        vs = [pl.BlockSpec((1, QBLK, GRP), lambda n, i: (n, jnp.maximum(i - 1, 0), v_lane_blk)),
              pl.BlockSpec((1, QBLK, GRP), lambda n, i: (n, i, v_lane_blk))]
    else:
        ks = [pl.BlockSpec((1, 2 * QBLK, GRP), lambda n, i: (n, 0, k_lane_blk))]
        vs = [pl.BlockSpec((1, 2 * QBLK, GRP), lambda n, i: (n, 0, v_lane_blk))]
    return qs, ks, vs


def _attn_fwd(q, k, v, *, name, banded, slopes_scaled=None, q_lane_blk=0, k_lane_blk=0, v_lane_blk=0,
              out_dtype=F32):
    n, l, _ = q.shape
    nb = l // QBLK
    nkv = 2 if banded else 1

    def body(*refs):
        q_ref = refs[0]
        k_refs = refs[1:1 + nkv]
        v_refs = refs[1 + nkv:1 + 2 * nkv]
        o_ref, lse_ref = refs[1 + 2 * nkv:]
        i = pl.program_id(1)
        masks = _head_masks4((QBLK, GRP))
        qv = q_ref[0]
        if banded:
            k2 = jnp.concatenate([k_refs[0][0], k_refs[1][0]], axis=0)
            v2 = jnp.concatenate([v_refs[0][0], v_refs[1][0]], axis=0)
            valid, bias = _band_terms(i, slopes_scaled)
        else:
            k2 = k_refs[0][0]
            v2 = v_refs[0][0]
        qs = qv * jnp.asarray(ATT_SCALE, BF16)
        scs = [_dot_nt(jnp.where(masks[h], qs, jnp.zeros_like(qs)), k2) for h in range(4)]
        ps, dens, lses = [], [], []
        for h in range(4):
            sc = scs[h]
            if banded:
                sc = jnp.where(valid, sc + bias[h], NEG_BIG)
            m = jnp.max(sc, axis=-1, keepdims=True)
            p = jnp.exp(sc - m)
            den = jnp.sum(p, axis=-1, keepdims=True)
            ps.append(p.astype(BF16))
            dens.append(den)
            lses.append(m + jnp.log(den))
        ohs = [_dot(ps[h], v2) for h in range(4)]
        o = ohs[0] / dens[0]
        lse = jnp.broadcast_to(lses[0], (QBLK, GRP))
        for h in range(1, 4):
            o = jnp.where(masks[h], ohs[h] / dens[h], o)
            lse = jnp.where(masks[h], lses[h], lse)
        o_ref[0] = o.astype(o_ref.dtype)
        lse_ref[0] = lse

    qs, ks, vs = _attn_specs(banded, q_lane_blk, k_lane_blk, v_lane_blk)
    ob = pl.BlockSpec((1, QBLK, GRP), lambda nn, i: (nn, i, 0))
    return pl.pallas_call(
        body, name=name, grid=(n, nb),
        in_specs=[qs] + ks + vs, out_specs=[ob, ob],
        out_shape=[jax.ShapeDtypeStruct((n, l, GRP), out_dtype), jax.ShapeDtypeStruct((n, l, GRP), F32)],
        compiler_params=_cp("parallel", "arbitrary"),
    )(q, *([k] * nkv), *([v] * nkv))


def _attn_bwd(q, k, v, do, lse, delta, *, name, banded, slopes_scaled=None, q_lane_blk=0, k_lane_blk=0,
              v_lane_blk=0, do_lane_blk=0):
    n, l, _ = q.shape
    nb = l // QBLK
    nkv = 2 if banded else 1
    lk = l if banded else 2 * QBLK

    def body(*refs):
        q_ref = refs[0]
        k_refs = refs[1:1 + nkv]
        v_refs = refs[1 + nkv:1 + 2 * nkv]
        do_ref, lse_ref, dl_ref, dq_ref, dk_ref, dv_ref = refs[1 + 2 * nkv:]
        i = pl.program_id(1)

        @pl.when(i == 0)
        def _():
            dk_ref[...] = jnp.zeros_like(dk_ref)
            dv_ref[...] = jnp.zeros_like(dv_ref)

        masks = _head_masks4((QBLK, GRP))
        qv = q_ref[0]
        dov = do_ref[0]
        lsev = lse_ref[0]
        dlv = dl_ref[0]
        if banded:
            k2 = jnp.concatenate([k_refs[0][0], k_refs[1][0]], axis=0)
            v2 = jnp.concatenate([v_refs[0][0], v_refs[1][0]], axis=0)
            valid, bias = _band_terms(i, slopes_scaled)
        else:
            k2 = k_refs[0][0]
            v2 = v_refs[0][0]
        qs = qv * jnp.asarray(ATT_SCALE, BF16)
        qms = [jnp.where(masks[h], qs, jnp.zeros_like(qs)) for h in range(4)]
        doms = [jnp.where(masks[h], dov, jnp.zeros_like(dov)) for h in range(4)]
        scs = [_dot_nt(qms[h], k2) for h in range(4)]
        dps = [_dot_nt(doms[h], v2) for h in range(4)]
        pbs, dss = [], []
        for h in range(4):
            c0 = HEAD_DIM * h
            sc = scs[h]
            if banded:
                sc = jnp.where(valid, sc + bias[h], NEG_BIG)
            p = jnp.exp(sc - lsev[:, c0:c0 + 1])
            pbs.append(p.astype(BF16))
            dss.append((p * (dps[h] - dlv[:, c0:c0 + 1])).astype(BF16))
        dqs = [_dot(dss[h], k2) for h in range(4)]
        dk2 = _dot_tn(jnp.concatenate(dss, axis=0), jnp.concatenate(qms, axis=0))
        dv2 = _dot_tn(jnp.concatenate(pbs, axis=0), jnp.concatenate(doms, axis=0))
        dq = dqs[0]
        for h in range(1, 4):
            dq = jnp.where(masks[h], dqs[h], dq)
        dq_ref[0] = (dq * ATT_SCALE).astype(dq_ref.dtype)
        if banded:
            cur = pl.multiple_of(i * QBLK, QBLK)
            dk_ref[0, pl.ds(cur, QBLK), :] += dk2[QBLK:]
            dv_ref[0, pl.ds(cur, QBLK), :] += dv2[QBLK:]

            @pl.when(i > 0)
            def _():
                prev = pl.multiple_of((i - 1) * QBLK, QBLK)
                dk_ref[0, pl.ds(prev, QBLK), :] += dk2[:QBLK]
                dv_ref[0, pl.ds(prev, QBLK), :] += dv2[:QBLK]
        else:
            dk_ref[0] += dk2
            dv_ref[0] += dv2

    qs, ks, vs = _attn_specs(banded, q_lane_blk, k_lane_blk, v_lane_blk)
    ob = pl.BlockSpec((1, QBLK, GRP), lambda nn, i: (nn, i, 0))
    dos = pl.BlockSpec((1, QBLK, GRP), lambda nn, i: (nn, i, do_lane_blk))
    kvb = pl.BlockSpec((1, lk, GRP), lambda nn, i: (nn, 0, 0))
    return pl.pallas_call(
        body, name=name, grid=(n, nb),
        in_specs=[qs] + ks + vs + [dos, ob, ob], out_specs=[ob, kvb, kvb],
        out_shape=[jax.ShapeDtypeStruct((n, l, GRP), BF16), jax.ShapeDtypeStruct((n, lk, GRP), F32),
                   jax.ShapeDtypeStruct((n, lk, GRP), F32)],
        compiler_params=_cp("parallel", "arbitrary"),
    )(q, *([k] * nkv), *([v] * nkv), do, lse, delta)


def _attn_delta(do, o, *, name, lane_blks):
    t, _ = do.shape
    tr = _pick(t, (512, 256, 128, 8))
    ng = len(lane_blks)

    def body(*refs):
        do_refs, o_refs, d_ref = refs[:ng], refs[ng:2 * ng], refs[2 * ng]
        ra = lax.broadcasted_iota(jnp.int32, (GRP, GRP), 0) // HEAD_DIM
        rb = lax.broadcasted_iota(jnp.int32, (GRP, GRP), 1) // HEAD_DIM
        same_head = (ra == rb).astype(BF16)
        prod = None
        for a_ref, b_ref in zip(do_refs, o_refs):
            term = a_ref[...].astype(F32) * b_ref[...].astype(F32)
            prod = term if prod is None else prod + term
        d_ref[...] = _dot_split(prod, same_head)

    specs = [pl.BlockSpec((tr, GRP), functools.partial(lambda i, lb: (i, lb), lb=lb)) for lb in lane_blks]
    return pl.pallas_call(
        body, name=name, grid=(t // tr,), in_specs=specs + specs,
        out_specs=pl.BlockSpec((tr, GRP), lambda i: (i, 0)),
        out_shape=jax.ShapeDtypeStruct((t, GRP), F32),
        compiler_params=_cp("parallel"),
    )(*([do] * ng), *([o] * ng))


def _dil_combine(os, lses, *, name):
    t, _ = os[0].shape
    tr = _pick(t, (512, 256, 128, 8))
    ng = len(os)

    def body(*refs):
        o_refs, l_refs = refs[:ng], refs[ng:2 * ng]
        out_ref, lse_ref = refs[2 * ng:]
        ls = [r[...] for r in l_refs]
        m = functools.reduce(jnp.maximum, ls)
        tot = None
        for lv in ls:
            e = jnp.exp(lv - m)
            tot = e if tot is None else tot + e
        lse = m + jnp.log(tot)
        for g in range(ng):
            out_ref[:, GRP * g:GRP * (g + 1)] = (o_refs[g][...] * jnp.exp(ls[g] - lse)).astype(out_ref.dtype)
        lse_ref[...] = lse

    sp = pl.BlockSpec((tr, GRP), lambda i: (i, 0))
    return pl.pallas_call(
        body, name=name, grid=(t // tr,), in_specs=[sp] * (2 * ng),
        out_specs=[pl.BlockSpec((tr, GRP * ng), lambda i: (i, 0)), sp],
        out_shape=[jax.ShapeDtypeStruct((t, GRP * ng), BF16), jax.ShapeDtypeStruct((t, GRP), F32)],
        compiler_params=_cp("parallel"),
    )(*os, *lses)


FFN_LB = 256
FFN_ROWS = 64
HALO = 16


def _conv_chunk(u_ref, w, ci):
    r0 = pl.multiple_of(ci * FFN_ROWS, FFN_ROWS)
    cur = u_ref[0, pl.ds(r0, FFN_ROWS), :].astype(F32)
    p0 = pl.multiple_of(jnp.maximum(r0 - HALO, 0), HALO)
    prev = u_ref[0, pl.ds(p0, HALO), :].astype(F32)
    prev = jnp.where(ci > 0, prev, 0.0)
    rowi = lax.broadcasted_iota(jnp.int32, cur.shape, 0)
    s1 = jnp.where(rowi == 0, prev[HALO - 1:HALO], pltpu.roll(cur, 1, 0))
    s2 = jnp.where(rowi == 0, prev[HALO - 2:HALO - 1],
                   jnp.where(rowi == 1, prev[HALO - 1:HALO], pltpu.roll(cur, 2, 0)))
    c = w[0:1] * s2
    c = c + w[1:2] * s1
    c = c + w[2:3] * cur
    return c, cur, s1, s2


def _ffn_mid_fwd(u, wconv, *, name):
    b, s, f2 = u.shape
    f = f2 // 2
    nlb = f // FFN_LB

    def body(ua_ref, ug_ref, wa_ref, wg_ref, h_ref):
        wa = wa_ref[...]
        wg = wg_ref[...]

        def step(ci, carry):
            ca = _conv_chunk(ua_ref, wa, ci)[0]
            cg = _conv_chunk(ug_ref, wg, ci)[0]
            r0 = pl.multiple_of(ci * FFN_ROWS, FFN_ROWS)
            h_ref[0, pl.ds(r0, FFN_ROWS), :] = (cg * jax.nn.sigmoid(cg) * ca).astype(h_ref.dtype)
            return carry

        lax.fori_loop(0, s // FFN_ROWS, step, 0)

    return pl.pallas_call(
        body, name=name, grid=(nlb, b),
        in_specs=[pl.BlockSpec((1, s, FFN_LB), lambda l, bb: (bb, 0, l)),
                  pl.BlockSpec((1, s, FFN_LB), lambda l, bb: (bb, 0, nlb + l)),
                  pl.BlockSpec((3, FFN_LB), lambda l, bb: (0, l)),
                  pl.BlockSpec((3, FFN_LB), lambda l, bb: (0, nlb + l))],
        out_specs=pl.BlockSpec((1, s, FFN_LB), lambda l, bb: (bb, 0, l)),
        out_shape=jax.ShapeDtypeStruct((b, s, f), BF16),
        compiler_params=_cp("parallel", "parallel"),
    )(u, u, wconv, wconv)


def _ffn_mid_bwd(u, wconv, dh, *, name):
    b, s, f2 = u.shape
    f = f2 // 2
    nlb = f // FFN_LB
    nchunk = s // FFN_ROWS

    def body(ua_ref, ug_ref, wa_ref, wg_ref, dh_ref, dua_ref, dug_ref, dwa_ref, dwg_ref, dca_ref, dcg_ref):
        bb = pl.program_id(1)
        wa = wa_ref[...]
        wg = wg_ref[...]
        dca_ref[pl.ds(s, 8), :] = jnp.zeros((8, FFN_LB), F32)
        dcg_ref[pl.ds(s, 8), :] = jnp.zeros((8, FFN_LB), F32)

        def first(ci, carry):
            r0 = pl.multiple_of(ci * FFN_ROWS, FFN_ROWS)
            ca, cura, s1a, s2a = _conv_chunk(ua_ref, wa, ci)
            cg, curg, s1g, s2g = _conv_chunk(ug_ref, wg, ci)
            dhv = dh_ref[0, pl.ds(r0, FFN_ROWS), :].astype(F32)
            sg = jax.nn.sigmoid(cg)
            da = dhv * (cg * sg)
            dg = dhv * ca * (sg * (1.0 + cg * (1.0 - sg)))
            dca_ref[pl.ds(r0, FFN_ROWS), :] = da
            dcg_ref[pl.ds(r0, FFN_ROWS), :] = dg
            red = lambda x: jnp.sum(x, axis=0, keepdims=True)
            parts = (red(da * s2a), red(da * s1a), red(da * cura), red(dg * s2g), red(dg * s1g), red(dg * curg))
            return tuple(c + p for c, p in zip(carry, parts))

        zero = jnp.zeros((1, FFN_LB), F32)
        taps = lax.fori_loop(0, nchunk, first, (zero,) * 6)

        @pl.when(bb == 0)
        def _():
            for k in range(3):
                dwa_ref[k:k + 1, :] = taps[k]
                dwg_ref[k:k + 1, :] = taps[3 + k]

        @pl.when(bb > 0)
        def _():
            for k in range(3):
                dwa_ref[k:k + 1, :] += taps[k]
                dwg_ref[k:k + 1, :] += taps[3 + k]

        def second(ci, carry):
            r0 = pl.multiple_of(ci * FFN_ROWS, FFN_ROWS)
            rowi = lax.broadcasted_iota(jnp.int32, (FFN_ROWS, FFN_LB), 0)
            for dc_ref, w, out_ref in ((dca_ref, wa, dua_ref), (dcg_ref, wg, dug_ref)):
                cur = dc_ref[pl.ds(r0, FFN_ROWS), :]
                nxt = dc_ref[pl.ds(pl.multiple_of(r0 + FFN_ROWS, 8), 8), :]
                n1 = jnp.where(rowi == FFN_ROWS - 1, nxt[0:1], pltpu.roll(cur, FFN_ROWS - 1, 0))
                n2 = jnp.where(rowi == FFN_ROWS - 2, nxt[0:1],
                               jnp.where(rowi == FFN_ROWS - 1, nxt[1:2], pltpu.roll(cur, FFN_ROWS - 2, 0)))
                du = w[2:3] * cur + w[1:2] * n1 + w[0:1] * n2
                out_ref[0, pl.ds(r0, FFN_ROWS), :] = du.astype(out_ref.dtype)
            return carry

        lax.fori_loop(0, nchunk, second, 0)

    seq_a = pl.BlockSpec((1, s, FFN_LB), lambda l, bb: (bb, 0, l))
    seq_g = pl.BlockSpec((1, s, FFN_LB), lambda l, bb: (bb, 0, nlb + l))
    wsp = pl.BlockSpec((3, FFN_LB), lambda l, bb: (0, l))
    return pl.pallas_call(
        body, name=name, grid=(nlb, b),
        in_specs=[seq_a, seq_g, wsp, pl.BlockSpec((3, FFN_LB), lambda l, bb: (0, nlb + l)), seq_a],
        out_specs=[seq_a, seq_a, wsp, wsp],
        out_shape=[jax.ShapeDtypeStruct((b, s, f), BF16), jax.ShapeDtypeStruct((b, s, f), BF16),
                   jax.ShapeDtypeStruct((3, f), F32), jax.ShapeDtypeStruct((3, f), F32)],
        scratch_shapes=[pltpu.VMEM((s + 8, FFN_LB), F32), pltpu.VMEM((s + 8, FFN_LB), F32)],
        compiler_params=_cp("parallel", "arbitrary"),
    )(u, u, wconv, wconv, dh)


def _adam_math(w, g, m, v):
    m2 = ADAM_B1 * m + (1.0 - ADAM_B1) * g
    v2 = ADAM_B2 * v + (1.0 - ADAM_B2) * (g * g)
    m_hat = m2 / (1.0 - ADAM_B1 ** ADAM_STEP)
    v_hat = v2 / (1.0 - ADAM_B2 ** ADAM_STEP)
    delta = -ADAM_LR * (m_hat / (jnp.sqrt(v_hat) + ADAM_EPS) + ADAM_WD * w)
    return delta, m2, v2


def _adam(w, g, m, v, *, name):
    r, c = w.shape
    tr = _pick(r, (256, 128, 88, 64, 32, 16, 8))

    def body(w_ref, g_ref, m_ref, v_ref, d_ref, m2_ref, v2_ref):
        d, m2, v2 = _adam_math(w_ref[...], g_ref[...], m_ref[...], v_ref[...])
        d_ref[...] = d
        m2_ref[...] = m2
        v2_ref[...] = v2

    sp = pl.BlockSpec((tr, c), lambda i: (i, 0))
    return pl.pallas_call(
        body, name=name, grid=(r // tr,), in_specs=[sp] * 4, out_specs=[sp] * 3,
        out_shape=[jax.ShapeDtypeStruct((r, c), F32)] * 3,
        compiler_params=_cp("parallel"),
    )(w, g, m, v)


def _adam_small(quads, *, name):
    nq = len(quads)

    def body(*refs):
        ins, outs = refs[:4 * nq], refs[4 * nq:]
        for k in range(nq):
            w_ref, g_ref, m_ref, v_ref = ins[4 * k:4 * k + 4]
            d, m2, v2 = _adam_math(w_ref[...], g_ref[...], m_ref[...], v_ref[...])
            outs[3 * k][...] = d
            outs[3 * k + 1][...] = m2
            outs[3 * k + 2][...] = v2

    flat = [a for q in quads for a in q]
    out_shape = [jax.ShapeDtypeStruct(q[0].shape, F32) for q in quads for _ in range(3)]
    vm = pl.BlockSpec(memory_space=pltpu.VMEM)
    outs = pl.pallas_call(
        body, name=name, in_specs=[vm] * len(flat), out_specs=[vm] * len(out_shape), out_shape=out_shape,
        compiler_params=pltpu.CompilerParams(vmem_limit_bytes=VMEM_LIMIT_BYTES),
    )(*flat)
    return [tuple(outs[3 * k:3 * k + 3]) for k in range(nq)]


def _mesh_pos():
    return lax.axis_index("x"), lax.axis_index("y"), lax.axis_index("c")


def _flip(v, bit):
    return 1 - v if bit else v


def _all_gather_hbm(xl, *, name):
    r, c = xl.shape

    def body(x_ref, out_ref, send_sems, recv_sems, local_sem):
        x, y, cc = _mesh_pos()
        me, sibling = (x, y, cc), (x, y, 1 - cc)
        chips = [(1 - x, y), (x, 1 - y), (1 - x, 1 - y)]

        def rows(px, py, pc):
            return out_ref.at[pl.ds((4 * px + 2 * py + pc) * r, r), :]

        def copy(k, block, to, src=None):
            return pltpu.make_async_remote_copy(
                src_ref=rows(*block) if src is None else src, dst_ref=rows(*block),
                send_sem=send_sems.at[k], recv_sem=recv_sems.at[k], device_id=to, device_id_type=MESH_ID)

        mine = pltpu.make_async_copy(x_ref, rows(*me), local_sem)
        mine.start()
        first = [copy(0, me, sibling, src=x_ref)]
        first += [copy(1 + j, me, (*chip, cc), src=x_ref) for j, chip in enumerate(chips)]
        for cp in first:
            cp.start()
        passed = [copy(4 + j, (*chip, cc), sibling) for j, chip in enumerate(chips)]
        for j, chip in enumerate(chips):
            copy(1 + j, (*chip, cc), me).wait_recv()
            passed[j].start()
        copy(0, sibling, me).wait_recv()
        for j, chip in enumerate(chips):
            copy(4 + j, (*chip, 1 - cc), me).wait_recv()
        for cp in first + passed:
            cp.wait_send()
        mine.wait()

    hbm = pl.BlockSpec(memory_space=pltpu.HBM)
    return pl.pallas_call(
        body, name=name, in_specs=[hbm], out_specs=hbm,
        out_shape=jax.ShapeDtypeStruct((N_DEV * r, c), xl.dtype),
        scratch_shapes=[pltpu.SemaphoreType.DMA((7,)), pltpu.SemaphoreType.DMA((7,)), pltpu.SemaphoreType.DMA],
    )(xl)


def _all_reduce_small(xl, *, name):
    r, c = xl.shape

    def body(x_ref, sum_ref, all_ref, send_sems, recv_sems, local_sem):
        x, y, cc = _mesh_pos()
        me, sibling = (x, y, cc), (x, y, 1 - cc)
        chips = [(1 - x, y), (x, 1 - y), (1 - x, 1 - y)]

        def rows(px, py, pc):
            return all_ref.at[pl.ds((4 * px + 2 * py + pc) * r, r), :]

        def copy(k, block, to, src=None):
            return pltpu.make_async_remote_copy(
                src_ref=rows(*block) if src is None else src, dst_ref=rows(*block),
                send_sem=send_sems.at[k], recv_sem=recv_sems.at[k], device_id=to, device_id_type=MESH_ID)

        mine = pltpu.make_async_copy(x_ref, rows(*me), local_sem)
        mine.start()
        first = [copy(0, me, sibling, src=x_ref)]
        first += [copy(1 + j, me, (*chip, cc), src=x_ref) for j, chip in enumerate(chips)]
        for cp in first:
            cp.start()
        passed = [copy(4 + j, (*chip, cc), sibling) for j, chip in enumerate(chips)]
        for j, chip in enumerate(chips):
            copy(1 + j, (*chip, cc), me).wait_recv()
            passed[j].start()
        copy(0, sibling, me).wait_recv()
        for j, chip in enumerate(chips):
            copy(4 + j, (*chip, 1 - cc), me).wait_recv()
        for cp in first + passed:
            cp.wait_send()
        mine.wait()
        tot = all_ref[pl.ds(0, r), :]
        for dd in range(1, N_DEV):
            tot = tot + all_ref[pl.ds(dd * r, r), :]
        sum_ref[...] = tot

    vm = pl.BlockSpec(memory_space=pltpu.VMEM)
    return pl.pallas_call(
        body, name=name, in_specs=[vm], out_specs=[vm, vm],
        out_shape=[jax.ShapeDtypeStruct((r, c), F32), jax.ShapeDtypeStruct((N_DEV * r, c), F32)],
        scratch_shapes=[pltpu.SemaphoreType.DMA((7,)), pltpu.SemaphoreType.DMA((7,)), pltpu.SemaphoreType.DMA],
    )(xl)[0]


def _grad_exchange(gp, *, name):
    _, r, c = gp.shape

    def body(g_ref, recv_ref, send_sems, recv_sems, local_sem):
        x, y, cc = _mesh_pos()
        me = 4 * x + 2 * y + cc
        mine = pltpu.make_async_copy(g_ref.at[me], recv_ref.at[me], local_sem)
        mine.start()
        sends, lands = [], []
        for rel in range(1, N_DEV):
            px, py, pc = _flip(x, rel & 4), _flip(y, rel & 2), _flip(cc, rel & 1)
            peer = 4 * px + 2 * py + pc
            sends.append(pltpu.make_async_remote_copy(
                src_ref=g_ref.at[peer], dst_ref=recv_ref.at[me], send_sem=send_sems.at[rel - 1],
                recv_sem=recv_sems.at[rel - 1], device_id=(px, py, pc), device_id_type=MESH_ID))
            lands.append(pltpu.make_async_remote_copy(
                src_ref=g_ref.at[me], dst_ref=recv_ref.at[peer], send_sem=send_sems.at[rel - 1],
                recv_sem=recv_sems.at[rel - 1], device_id=(px, py, pc), device_id_type=MESH_ID))
        for cp in sends:
            cp.start()
        for cp in lands:
            cp.wait_recv()
        for cp in sends:
            cp.wait_send()
        mine.wait()

    hbm = pl.BlockSpec(memory_space=pltpu.HBM)
    return pl.pallas_call(
        body, name=name, in_specs=[hbm], out_specs=hbm,
        out_shape=jax.ShapeDtypeStruct(gp.shape, gp.dtype),
        scratch_shapes=[pltpu.SemaphoreType.DMA((7,)), pltpu.SemaphoreType.DMA((7,)), pltpu.SemaphoreType.DMA],
    )(gp)


def _sum_blocks(recv, *, name):
    nd, r, c = recv.shape
    tr = _pick(r, (448, 256, 128, 64, 32, 16))

    def body(x_ref, o_ref):
        tot = x_ref[0].astype(F32)
        for dd in range(1, nd):
            tot = tot + x_ref[dd].astype(F32)
        o_ref[...] = tot

    return pl.pallas_call(
        body, name=name, grid=(r // tr,),
        in_specs=[pl.BlockSpec((nd, tr, c), lambda i: (0, i, 0))],
        out_specs=pl.BlockSpec((tr, c), lambda i: (i, 0)),
        out_shape=jax.ShapeDtypeStruct((r, c), F32),
        compiler_params=_cp("parallel"),
    )(recv)


BIG_WEIGHTS = (("a_w_in", "col"), ("a_w_out", "row"), ("a_w_mem_kv", "row"), ("a_ffn_up", "col"),
               ("a_ffn_down", "row"), ("w_kv_shared", "col"), ("b_w_in", "row"), ("b_w_out", "row"),
               ("b_w_mem_kv", "row"), ("b_ffn_up", "col"), ("b_ffn_down", "row"))


def _as2d(a):
    return a.reshape(a.shape[-2], a.shape[-1]) if a.ndim >= 2 else a.reshape(1, a.shape[0])


def _pack_local(shards):
    return jnp.concatenate([_as2d(s).astype(BF16).reshape(-1, PACK_COLS) for s in shards], axis=0)


def _unpack_full(gathered, local_shapes):
    out = {}
    r0 = 0
    for (name, kind), (rows, cols) in zip(BIG_WEIGHTS, local_shapes):
        nr = rows * cols // PACK_COLS
        blk = gathered[:, r0:r0 + nr, :].reshape(N_DEV, rows, cols)
        if kind == "row":
            out[name] = blk.reshape(N_DEV * rows, cols)
        else:
            out[name] = blk.transpose(1, 0, 2).reshape(rows, N_DEV * cols)
        r0 += nr
    return out


def _pack_grads(grads, local_shapes):
    parts = []
    for (name, kind), (rows, cols) in zip(BIG_WEIGHTS, local_shapes):
        g = grads[name]
        if kind == "row":
            blk = g.reshape(N_DEV, rows, cols)
        else:
            blk = g.reshape(rows, N_DEV, cols).transpose(1, 0, 2)
        parts.append(blk.astype(BF16).reshape(N_DEV, rows * cols // PACK_COLS, PACK_COLS))
    return jnp.concatenate(parts, axis=1)


def _by_residue(t, d):
    if d == 1:
        return t
    b, s, c = t.shape
    return t.reshape(b, s // d, d, c).transpose(0, 2, 1, 3).reshape(b * d, s // d, c)


def _from_residue(t, d, b):
    if d == 1:
        return t
    n, l, c = t.shape
    return t.reshape(b, d, l, c).transpose(0, 2, 1, 3).reshape(b, l * d, c)


def _alibi_slopes():
    return [2.0 ** (-ALIBI_MAX_BIAS * (i + 1) / N_DIL_HEADS) for i in range(N_DIL_HEADS)]


def _conv_ffn_fwd(xin, gain, w_up, wconv, w_down, tag, b, s):
    (n,), r = _rms_fwd(xin, [gain], name=f"{tag}_rms_ffn")
    u = _mm(n, w_up, name=f"{tag}_up", out_dtype=BF16).reshape(b, s, -1)
    hmid = _ffn_mid_fwd(u, wconv, name=f"{tag}_ffn_mid").reshape(b * s, -1)
    xout = _mm(hmid, w_down, name=f"{tag}_down", out_dtype=F32, res=xin)
    return xout, (n, r, u, hmid)


def _conv_ffn_bwd(dxout, xin, gain, saved, w_up, wconv, w_down, tag, b, s):
    n, r, u, hmid = saved
    f = hmid.shape[1]
    dhmid = _mm(dxout, w_down, name=f"{tag}_d_hmid", out_dtype=BF16, trans_b=True)
    g_down = _mm(hmid, dxout, name=f"{tag}_g_down", out_dtype=BF16, trans_a=True)
    du_a, du_g, gc_a, gc_g = _ffn_mid_bwd(u, wconv, dhmid.reshape(b, s, f), name=f"{tag}_ffn_mid_bwd")
    du_a = du_a.reshape(b * s, f)
    du_g = du_g.reshape(b * s, f)
    dn = _mm(du_a, w_up[:, :f], name=f"{tag}_d_n_a", out_dtype=F32, trans_b=True)
    dn = _mm(du_g, w_up[:, f:], name=f"{tag}_d_n_g", out_dtype=F32, res=dn, trans_b=True)
    g_up = jnp.concatenate([_mm(n, du_a, name=f"{tag}_g_up_a", out_dtype=BF16, trans_a=True),
                            _mm(n, du_g, name=f"{tag}_g_up_g", out_dtype=BF16, trans_a=True)], axis=1)
    dxin, (g_gain,) = _rms_bwd(xin, r, [(dn, gain)], dxout, name=f"{tag}_rms_ffn_bwd")
    return dxin, g_up, g_down, jnp.concatenate([gc_a, gc_g], axis=1), g_gain


def _mem_kv_fwd(mem2d, gain, w_mem_kv, tag, b):
    (nm,), rm = _rms_fwd(mem2d, [gain], name=f"{tag}_rms_mem")
    kvm = _mm(nm, w_mem_kv, name=f"{tag}_mem_kv", out_dtype=BF16)
    return kvm.reshape(b, -1, 2 * MEM_WIDTH), (nm, rm)


def _mem_kv_bwd(dk, dv, mem2d, gain, saved, w_mem_kv, tag):
    nm, rm = saved
    dkvm = jnp.concatenate([dk, dv], axis=-1).reshape(-1, 2 * MEM_WIDTH)
    dnm = _mm(dkvm, w_mem_kv, name=f"{tag}_d_nm", out_dtype=F32, trans_b=True)
    g_w = _mm(nm, dkvm, name=f"{tag}_g_mem_kv", out_dtype=BF16, trans_a=True)
    _, (g_gain,) = _rms_bwd(mem2d, rm, [(dnm, gain)], None, name=f"{tag}_rms_mem_bwd", need_dx=False)
    return g_w, g_gain


def kernel(x, mem, a_norm_attn, a_w_in, a_w_out, a_norm_mem, a_w_mem_kv, a_norm_ffn, a_ffn_up, a_ffn_conv, a_ffn_down, kv_norm, w_kv_shared, b_norm_attn, b_w_in, b_w_out, b_norm_mem, b_w_mem_kv, b_norm_ffn, b_ffn_up, b_ffn_conv, b_ffn_down, final_norm, loss_target, m_a_norm_attn, m_a_w_in, m_a_w_out, m_a_norm_mem, m_a_w_mem_kv, m_a_norm_ffn, m_a_ffn_up, m_a_ffn_conv, m_a_ffn_down, m_kv_norm, m_w_kv_shared, m_b_norm_attn, m_b_w_in, m_b_w_out, m_b_norm_mem, m_b_w_mem_kv, m_b_norm_ffn, m_b_ffn_up, m_b_ffn_conv, m_b_ffn_down, m_final_norm, v_a_norm_attn, v_a_w_in, v_a_w_out, v_a_norm_mem, v_a_w_mem_kv, v_a_norm_ffn, v_a_ffn_up, v_a_ffn_conv, v_a_ffn_down, v_kv_norm, v_w_kv_shared, v_b_norm_attn, v_b_w_in, v_b_w_out, v_b_norm_mem, v_b_w_mem_kv, v_b_norm_ffn, v_b_ffn_up, v_b_ffn_conv, v_b_ffn_down, v_final_norm):
    names = ["a_norm_attn", "a_w_in", "a_w_out", "a_norm_mem", "a_w_mem_kv", "a_norm_ffn", "a_ffn_up",
             "a_ffn_conv", "a_ffn_down", "kv_norm", "w_kv_shared", "b_norm_attn", "b_w_in", "b_w_out",
             "b_norm_mem", "b_w_mem_kv", "b_norm_ffn", "b_ffn_up", "b_ffn_conv", "b_ffn_down", "final_norm"]
    wl = dict(zip(names, [a_norm_attn, a_w_in, a_w_out, a_norm_mem, a_w_mem_kv, a_norm_ffn, a_ffn_up,
                          a_ffn_conv, a_ffn_down, kv_norm, w_kv_shared, b_norm_attn, b_w_in, b_w_out,
                          b_norm_mem, b_w_mem_kv, b_norm_ffn, b_ffn_up, b_ffn_conv, b_ffn_down, final_norm]))
    ml = dict(zip(names, [m_a_norm_attn, m_a_w_in, m_a_w_out, m_a_norm_mem, m_a_w_mem_kv, m_a_norm_ffn,
                          m_a_ffn_up, m_a_ffn_conv, m_a_ffn_down, m_kv_norm, m_w_kv_shared, m_b_norm_attn,
                          m_b_w_in, m_b_w_out, m_b_norm_mem, m_b_w_mem_kv, m_b_norm_ffn, m_b_ffn_up,
                          m_b_ffn_conv, m_b_ffn_down, m_final_norm]))
    vl = dict(zip(names, [v_a_norm_attn, v_a_w_in, v_a_w_out, v_a_norm_mem, v_a_w_mem_kv, v_a_norm_ffn,
                          v_a_ffn_up, v_a_ffn_conv, v_a_ffn_down, v_kv_norm, v_w_kv_shared, v_b_norm_attn,
                          v_b_w_in, v_b_w_out, v_b_norm_mem, v_b_w_mem_kv, v_b_norm_ffn, v_b_ffn_up,
                          v_b_ffn_conv, v_b_ffn_down, v_final_norm]))
    b, s, d = x.shape
    t = b * s
    my_x, my_y, my_c = _mesh_pos()
    me = 4 * my_x + 2 * my_y + my_c

    local_shapes = [_as2d(wl[nm]).shape for nm, _ in BIG_WEIGHTS]
    packed = _pack_local([wl[nm] for nm, _ in BIG_WEIGHTS])
    nrows = packed.shape[0]
    gathered = _all_gather_hbm(packed, name="gather_weights").reshape(N_DEV, nrows, PACK_COLS)
    wf = _unpack_full(gathered, local_shapes)

    sharded_small = ["a_norm_attn", "a_norm_mem", "a_norm_ffn", "a_ffn_conv", "b_ffn_conv"]
    small_flat = jnp.concatenate([wl[nm].reshape(-1) for nm in sharded_small])
    n_small = small_flat.shape[0]
    small_rows = -(-n_small // (8 * LANE)) * 8
    small_local = jnp.pad(small_flat, (0, small_rows * LANE - n_small)).reshape(small_rows, LANE)
    small_all = _all_gather_hbm(small_local, name="gather_small").reshape(N_DEV, small_rows * LANE)
    sfull = {}
    r0 = 0
    for nm in sharded_small:
        rows, cols = _as2d(wl[nm]).shape
        blk = small_all[:, r0:r0 + rows * cols].reshape(N_DEV, rows, cols)
        sfull[nm] = blk.transpose(1, 0, 2).reshape(rows, N_DEV * cols)
        r0 += rows * cols
    gain = {nm: sfull[nm] for nm in ("a_norm_attn", "a_norm_mem", "a_norm_ffn")}
    for nm in ("kv_norm", "b_norm_attn", "b_norm_mem", "b_norm_ffn", "final_norm"):
        gain[nm] = _as2d(wl[nm])
    conv_a, conv_b = sfull["a_ffn_conv"], sfull["b_ffn_conv"]

    x2d = x.reshape(t, d)
    mem2d = mem.reshape(-1, d)
    tgt2d = loss_target.reshape(t, d)
    qmem_blk_a = 3 * SB_WIDTH // GRP
    qmem_blk_b = DIL_WIDTH // GRP

    (n1,), r1 = _rms_fwd(x2d, [gain["a_norm_attn"]], name="a_rms_attn")
    proj_a = _mm(n1, wf["a_w_in"], name="a_in", out_dtype=BF16).reshape(b, s, -1)
    kvm_a, mem_saved_a = _mem_kv_fwd(mem2d, gain["a_norm_mem"], wf["a_w_mem_kv"], "a", b)
    o_sb, rsum = _sb_fwd(proj_a, name="a_sb_fwd")
    o_mem_a, lse_mem_a = _attn_fwd(proj_a, kvm_a, kvm_a, name="a_mem_fwd", banded=False,
                                   q_lane_blk=qmem_blk_a, k_lane_blk=0, v_lane_blk=1, out_dtype=BF16)
    cat_a = jnp.concatenate([o_sb, o_mem_a], axis=-1).reshape(t, d)
    x1 = _mm(cat_a, wf["a_w_out"], name="a_out", out_dtype=F32, res=x2d)
    xa, ffn_saved_a = _conv_ffn_fwd(x1, gain["a_norm_ffn"], wf["a_ffn_up"], conv_a, wf["a_ffn_down"], "a", b, s)

    (nk, n3), r3 = _rms_fwd(xa, [gain["kv_norm"], gain["b_norm_attn"]], name="b_rms_attn")
    kvsh = _mm(nk, wf["w_kv_shared"], name="kv_shared", out_dtype=BF16).reshape(b, s, -1)
    proj_b = _mm(n3, wf["b_w_in"], name="b_in", out_dtype=BF16).reshape(b, s, -1)
    kvm_b, mem_saved_b = _mem_kv_fwd(mem2d, gain["b_norm_mem"], wf["b_w_mem_kv"], "b", b)
    slopes = _alibi_slopes()
    dil_q, dil_k, dil_v, dil_o, dil_lse, dil_slopes = [], [], [], [], [], []
    for g, (_, dil) in enumerate(DIL_GROUPS):
        qg = _by_residue(proj_b[:, :, GRP * g:GRP * (g + 1)], dil)
        kg = _by_residue(kvsh[:, :, GRP * g:GRP * (g + 1)], dil)
        vg = _by_residue(kvsh[:, :, DIL_WIDTH + GRP * g:DIL_WIDTH + GRP * (g + 1)], dil)
        sl = [slopes[4 * g + h] * dil for h in range(4)]
        og, lg = _attn_fwd(qg, kg, vg, name=f"b_dil{g}_fwd", banded=True, slopes_scaled=sl)
        dil_q.append(qg)
        dil_k.append(kg)
        dil_v.append(vg)
        dil_slopes.append(sl)
        dil_o.append(_from_residue(og, dil, b).reshape(t, GRP))
        dil_lse.append(_from_residue(lg, dil, b).reshape(t, GRP))
    o_dil, lse_joint = _dil_combine(dil_o, dil_lse, name="b_dil_combine")
    o_mem_b, lse_mem_b = _attn_fwd(proj_b, kvm_b, kvm_b, name="b_mem_fwd", banded=False,
                                   q_lane_blk=qmem_blk_b, k_lane_blk=0, v_lane_blk=1, out_dtype=BF16)
    cat_b = jnp.concatenate([o_dil, o_mem_b.reshape(t, MEM_WIDTH)], axis=-1)
    x3 = _mm(cat_b, wf["b_w_out"], name="b_out", out_dtype=F32, res=xa)
    xb, ffn_saved_b = _conv_ffn_fwd(x3, gain["b_norm_ffn"], wf["b_ffn_up"], conv_b, wf["b_ffn_down"], "b", b, s)

    dxb, g_final, loss_vec = _loss_head(xb, gain["final_norm"], tgt2d, name="loss_head")

    grads = {}
    sgrads = {"final_norm": g_final}
    dx3, grads["b_ffn_up"], grads["b_ffn_down"], sgrads["b_ffn_conv"], sgrads["b_norm_ffn"] = _conv_ffn_bwd(
        dxb, x3, gain["b_norm_ffn"], ffn_saved_b, wf["b_ffn_up"], conv_b, wf["b_ffn_down"], "b", b, s)
    dcat_b = _mm(dx3, wf["b_w_out"], name="b_d_cat", out_dtype=BF16, trans_b=True)
    grads["b_w_out"] = _mm(cat_b, dx3, name="b_g_out", out_dtype=BF16, trans_a=True)
    dcat_b3 = dcat_b.reshape(b, s, d)
    delta_mem_b = _attn_delta(dcat_b, cat_b, name="b_mem_delta", lane_blks=[qmem_blk_b]).reshape(b, s, GRP)
    dq_mem_b, dkm_b, dvm_b = _attn_bwd(proj_b, kvm_b, kvm_b, dcat_b3, lse_mem_b, delta_mem_b, name="b_mem_bwd",
                                       banded=False, q_lane_blk=qmem_blk_b, k_lane_blk=0, v_lane_blk=1,
                                       do_lane_blk=qmem_blk_b)
    delta_dil = _attn_delta(dcat_b, cat_b, name="b_dil_delta", lane_blks=[0, 1, 2]).reshape(b, s, GRP)
    lse_joint3 = lse_joint.reshape(b, s, GRP)
    dq_parts, dk_parts, dv_parts = [], [], []
    for g, (_, dil) in enumerate(DIL_GROUPS):
        dog = _by_residue(dcat_b3[:, :, GRP * g:GRP * (g + 1)], dil)
        lg = _by_residue(lse_joint3, dil)
        dg = _by_residue(delta_dil, dil)
        dqg, dkg, dvg = _attn_bwd(dil_q[g], dil_k[g], dil_v[g], dog, lg, dg, name=f"b_dil{g}_bwd", banded=True,
                                  slopes_scaled=dil_slopes[g])
        dq_parts.append(_from_residue(dqg, dil, b))
        dk_parts.append(_from_residue(dkg, dil, b))
        dv_parts.append(_from_residue(dvg, dil, b))
    dproj_b = jnp.concatenate(dq_parts + [dq_mem_b], axis=-1).reshape(t, d)
    dn3 = _mm(dproj_b, wf["b_w_in"], name="b_d_n", out_dtype=F32, trans_b=True)
    grads["b_w_in"] = _mm(n3, dproj_b, name="b_g_in", out_dtype=BF16, trans_a=True)
    grads["b_w_mem_kv"], sgrads["b_norm_mem"] = _mem_kv_bwd(dkm_b, dvm_b, mem2d, gain["b_norm_mem"], mem_saved_b,
                                                           wf["b_w_mem_kv"], "b")
    dkvsh = jnp.concatenate(dk_parts + dv_parts, axis=-1).reshape(t, 2 * DIL_WIDTH).astype(BF16)
    dnk = _mm(dkvsh, wf["w_kv_shared"], name="kv_d_n", out_dtype=F32, trans_b=True)
    grads["w_kv_shared"] = _mm(nk, dkvsh, name="kv_g", out_dtype=BF16, trans_a=True)
    dxa, (sgrads["kv_norm"], sgrads["b_norm_attn"]) = _rms_bwd(
        xa, r3, [(dnk, gain["kv_norm"]), (dn3, gain["b_norm_attn"])], dx3, name="b_rms_attn_bwd")

    dx1, grads["a_ffn_up"], grads["a_ffn_down"], sgrads["a_ffn_conv"], sgrads["a_norm_ffn"] = _conv_ffn_bwd(
        dxa, x1, gain["a_norm_ffn"], ffn_saved_a, wf["a_ffn_up"], conv_a, wf["a_ffn_down"], "a", b, s)
    dcat_a = _mm(dx1, wf["a_w_out"], name="a_d_cat", out_dtype=BF16, trans_b=True)
    grads["a_w_out"] = _mm(cat_a, dx1, name="a_g_out", out_dtype=BF16, trans_a=True)
    dcat_a3 = dcat_a.reshape(b, s, d)
    delta_mem_a = _attn_delta(dcat_a, cat_a, name="a_mem_delta", lane_blks=[qmem_blk_b]).reshape(b, s, GRP)
    dq_mem_a, dkm_a, dvm_a = _attn_bwd(proj_a, kvm_a, kvm_a, dcat_a3, lse_mem_a, delta_mem_a, name="a_mem_bwd",
                                       banded=False, q_lane_blk=qmem_blk_a, k_lane_blk=0, v_lane_blk=1,
                                       do_lane_blk=qmem_blk_b)
    dq_sb, dk_sb, dv_sb = _sb_bwd(proj_a, dcat_a3, rsum, name="a_sb_bwd")
    dproj_a = jnp.concatenate([dq_sb, dk_sb.astype(BF16), dv_sb.astype(BF16), dq_mem_a], axis=-1).reshape(t, -1)
    dn1 = _mm(dproj_a, wf["a_w_in"], name="a_d_n", out_dtype=F32, trans_b=True)
    grads["a_w_in"] = _mm(n1, dproj_a, name="a_g_in", out_dtype=BF16, trans_a=True)
    grads["a_w_mem_kv"], sgrads["a_norm_mem"] = _mem_kv_bwd(dkm_a, dvm_a, mem2d, gain["a_norm_mem"], mem_saved_a,
                                                           wf["a_w_mem_kv"], "a")
    dx0, (sgrads["a_norm_attn"],) = _rms_bwd(x2d, r1, [(dn1, gain["a_norm_attn"])], dx1, name="a_rms_attn_bwd")
    grad_x = dx0.reshape(b, s, d)

    recv = _grad_exchange(_pack_grads(grads, local_shapes), name="exchange_grads")
    gsum = _sum_blocks(recv, name="sum_grads")
    gl = {}
    r0 = 0
    for (nm, _), (rows, cols) in zip(BIG_WEIGHTS, local_shapes):
        nr = rows * cols // PACK_COLS
        gl[nm] = gsum[r0:r0 + nr].reshape(rows, cols)
        r0 += nr

    small_names = ["a_norm_attn", "a_norm_mem", "a_norm_ffn", "kv_norm", "b_norm_attn", "b_norm_mem",
                   "b_norm_ffn", "final_norm", "a_ffn_conv", "b_ffn_conv"]
    small_flat = jnp.concatenate([sgrads[nm].reshape(-1) for nm in small_names] + [loss_vec.reshape(-1)])
    n_flat = small_flat.shape[0]
    red_rows = -(-n_flat // (8 * PACK_COLS)) * 8
    small_pack = jnp.pad(small_flat, (0, red_rows * PACK_COLS - n_flat)).reshape(red_rows, PACK_COLS)
    small_sum = _all_reduce_small(small_pack, name="reduce_small").reshape(-1)
    r0 = 0
    for nm in small_names:
        rows, cols = sgrads[nm].shape
        full = small_sum[r0:r0 + rows * cols].reshape(rows, cols)
        r0 += rows * cols
        if nm in sharded_small:
            lc = cols // N_DEV
            gl[nm] = lax.dynamic_slice(full, (0, me * lc), (rows, lc))
        else:
            gl[nm] = full
    loss = (0.5 / d) * jnp.sum(small_sum[r0:r0 + d])

    upd = {}
    for nm, _ in BIG_WEIGHTS:
        upd[nm] = _adam(_as2d(wl[nm]), gl[nm], _as2d(ml[nm]), _as2d(vl[nm]), name=f"adam_{nm}")
    res_small = _adam_small([(_as2d(wl[nm]), gl[nm], _as2d(ml[nm]), _as2d(vl[nm])) for nm in small_names],
                            name="adam_small")
    for nm, r in zip(small_names, res_small):
        upd[nm] = r

    g_out = [gl[nm].reshape(wl[nm].shape) for nm in names]
    d_out = [upd[nm][0].reshape(wl[nm].shape) for nm in names]
    m_out = [upd[nm][1].reshape(wl[nm].shape) for nm in names]
    v_out = [upd[nm][2].reshape(wl[nm].shape) for nm in names]
    return (loss, grad_x, *g_out, *d_out, *m_out, *v_out)
```

```python
import functools
import math

import jax
import jax.numpy as jnp
from jax import lax
from jax.experimental import pallas as pl
from jax.experimental.pallas import tpu as pltpu

F32 = jnp.float32
BF16 = jnp.bfloat16

N_DEV = 8
HEAD_DIM = 64
N_SB_HEADS = 12
N_DIL_HEADS = 12
DIL_GROUPS = ((128, 1), (512, 4), (2048, 16))
SB_WIDTH = N_SB_HEADS * HEAD_DIM
MEM_WIDTH = 256
DIL_WIDTH = N_DIL_HEADS * HEAD_DIM
ATT_SCALE = HEAD_DIM ** -0.5
EPS = 1e-6
ALIBI_MAX_BIAS = 8.0
NEG_BIG = -1e30

ADAM_LR = 0.001
ADAM_B1 = 0.9
ADAM_B2 = 0.999
ADAM_EPS = 1e-08
ADAM_WD = 0.01
ADAM_STEP = 10

LANE = 128
QBLK = 128
VMEM_LIMIT_BYTES = 48 * 1024 * 1024
PACK_COLS = 1024
MESH_ID = pl.DeviceIdType.MESH


def _cp(*sem):
    return pltpu.CompilerParams(dimension_semantics=sem, vmem_limit_bytes=VMEM_LIMIT_BYTES)


def _pick(n, cands):
    for c in cands:
        if n % c == 0:
            return c
    raise ValueError(f"no tile for {n} in {cands}")


def _dot(a, b):
    return jnp.dot(a, b, preferred_element_type=F32)


def _dot_nt(a, b):
    return lax.dot_general(a, b, (((1,), (1,)), ((), ())), preferred_element_type=F32)


def _dot_tn(a, b):
    return lax.dot_general(a, b, (((0,), (0,)), ((), ())), preferred_element_type=F32)


def _dot_split(x, u):
    hi = x.astype(BF16)
    lo = (x - hi.astype(F32)).astype(BF16)
    return _dot(hi, u) + _dot(lo, u)


def _mm(a, b, *, name, out_dtype, res=None, trans_a=False, trans_b=False):
    assert not (trans_a and trans_b)
    if trans_a:
        kdim, m = a.shape
    else:
        m, kdim = a.shape
    if trans_b:
        n, kb = b.shape
    else:
        kb, n = b.shape
    assert kb == kdim, (a.shape, b.shape)
    if trans_a:
        tm = _pick(m, (1408, 1024, 512, 256, 128))
        tn = _pick(n, (1024, 1280, 1408, 768, 512, 256, 128))
        tk = _pick(kdim, (512, 256))
    else:
        tm = _pick(m, (1024, 512, 256, 128))
        tn = _pick(n, (512, 1408, 256, 128))
        tk = kdim if kdim <= 2048 else _pick(kdim, (2048, 1536, 1408, 1280, 1024, 512))
    nk = kdim // tk
    has_res = res is not None

    def body(*refs):
        if has_res:
            a_ref, b_ref, r_ref, o_ref = refs[:4]
            scr = refs[4:]
        else:
            a_ref, b_ref, o_ref = refs[:3]
            r_ref = None
            scr = refs[3:]
        av = a_ref[...].astype(BF16)
        bv = b_ref[...].astype(BF16)
        if trans_a:
            p = _dot_tn(av, bv)
        elif trans_b:
            p = _dot_nt(av, bv)
        else:
            p = _dot(av, bv)

        def finish(acc):
            if has_res:
                acc = acc + r_ref[...]
            o_ref[...] = acc.astype(o_ref.dtype)

        if nk == 1:
            finish(p)
        else:
            acc_ref = scr[0]
            k = pl.program_id(2)

            @pl.when(k == 0)
            def _():
                acc_ref[...] = p

            @pl.when(k > 0)
            def _():
                acc_ref[...] += p

            @pl.when(k == nk - 1)
            def _():
                finish(acc_ref[...])

    if trans_a:
        a_spec = pl.BlockSpec((tk, tm), lambda i, j, k: (k, i))
    else:
        a_spec = pl.BlockSpec((tm, tk), lambda i, j, k: (i, k))
    if trans_b:
        b_spec = pl.BlockSpec((tn, tk), lambda i, j, k: (j, k))
    else:
        b_spec = pl.BlockSpec((tk, tn), lambda i, j, k: (k, j))
    in_specs = [a_spec, b_spec]
    args = [a, b]
    if has_res:
        in_specs.append(pl.BlockSpec((tm, tn), lambda i, j, k: (i, j)))
        args.append(res)
    return pl.pallas_call(
        body, name=name,
        grid=(m // tm, n // tn, nk),
        in_specs=in_specs,
        out_specs=pl.BlockSpec((tm, tn), lambda i, j, k: (i, j)),
        out_shape=jax.ShapeDtypeStruct((m, n), out_dtype),
        scratch_shapes=[pltpu.VMEM((tm, tn), F32)] if nk > 1 else [],
        compiler_params=_cp("parallel", "parallel", "arbitrary"),
    )(*args)


def _rms_fwd(x, gains, *, name):
    t, d = x.shape
    tr = _pick(t, (512, 256, 128, 8))
    ng = len(gains)

    def body(x_ref, *rest):
        g_refs, n_refs, r_ref = rest[:ng], rest[ng:2 * ng], rest[2 * ng]
        xv = x_ref[...]
        r = lax.rsqrt(jnp.mean(xv * xv, axis=-1, keepdims=True) + EPS)
        xh = xv * r
        for g_ref, n_ref in zip(g_refs, n_refs):
            n_ref[...] = (xh * g_ref[...]).astype(BF16)
        r_ref[...] = r

    row = pl.BlockSpec((tr, d), lambda i: (i, 0))
    gsp = pl.BlockSpec((1, d), lambda i: (0, 0))
    outs = pl.pallas_call(
        body, name=name, grid=(t // tr,),
        in_specs=[row] + [gsp] * ng,
        out_specs=[row] * ng + [pl.BlockSpec((tr, 1), lambda i: (i, 0))],
        out_shape=[jax.ShapeDtypeStruct((t, d), BF16)] * ng + [jax.ShapeDtypeStruct((t, 1), F32)],
        compiler_params=_cp("parallel"),
    )(x, *gains)
    return list(outs[:ng]), outs[ng]


def _rms_bwd(x, r, pairs, dres, *, name, need_dx=True):
    t, d = x.shape
    tr = _pick(t, (512, 256, 128, 8))
    npair = len(pairs)
    has_res = dres is not None

    def body(*refs):
        x_ref, r_ref = refs[:2]
        pr = refs[2:2 + 2 * npair]
        pos = 2 + 2 * npair
        res_ref = None
        if has_res:
            res_ref = refs[pos]
            pos += 1
        dx_ref = None
        if need_dx:
            dx_ref = refs[pos]
            pos += 1
        dg_refs = refs[pos:pos + npair]
        i = pl.program_id(0)
        rv = r_ref[...]
        xh = x_ref[...] * rv
        dx = res_ref[...] if has_res else None
        for k in range(npair):
            dn = pr[2 * k][...].astype(F32)
            g = pr[2 * k + 1][...]
            part = jnp.sum(dn * xh, axis=0, keepdims=True)

            @pl.when(i == 0)
            def _():
                dg_refs[k][...] = part

            @pl.when(i > 0)
            def _():
                dg_refs[k][...] += part

            if need_dx:
                dxh = dn * g
                c = jnp.mean(dxh * xh, axis=-1, keepdims=True)
                term = rv * (dxh - xh * c)
                dx = term if dx is None else dx + term
        if need_dx:
            dx_ref[...] = dx

    row = pl.BlockSpec((tr, d), lambda i: (i, 0))
    gsp = pl.BlockSpec((1, d), lambda i: (0, 0))
    in_specs = [row, pl.BlockSpec((tr, 1), lambda i: (i, 0))]
    args = [x, r]
    for dn, g in pairs:
        in_specs += [row, gsp]
        args += [dn, g]
    if has_res:
        in_specs.append(row)
        args.append(dres)
    out_specs, out_shape = [], []
    if need_dx:
        out_specs.append(row)
        out_shape.append(jax.ShapeDtypeStruct((t, d), F32))
    out_specs += [gsp] * npair
    out_shape += [jax.ShapeDtypeStruct((1, d), F32)] * npair
    outs = pl.pallas_call(
        body, name=name, grid=(t // tr,), in_specs=in_specs, out_specs=out_specs, out_shape=out_shape,
        compiler_params=_cp("arbitrary"),
    )(*args)
    if need_dx:
        return outs[0], list(outs[1:])
    return None, list(outs)


def _loss_head(h, g, tgt, *, name):
    t, d = h.shape
    tr = _pick(t, (512, 256, 128, 8))

    def body(h_ref, g_ref, t_ref, dh_ref, dg_ref, l_ref):
        i = pl.program_id(0)
        xv = h_ref[...]
        gv = g_ref[...]
        r = lax.rsqrt(jnp.mean(xv * xv, axis=-1, keepdims=True) + EPS)
        xh = xv * r
        e = xh * gv - t_ref[...]
        dy = e * (1.0 / d)
        lpart = jnp.sum(e * e, axis=0, keepdims=True)
        gpart = jnp.sum(dy * xh, axis=0, keepdims=True)

        @pl.when(i == 0)
        def _():
            l_ref[...] = lpart
            dg_ref[...] = gpart

        @pl.when(i > 0)
        def _():
            l_ref[...] += lpart
            dg_ref[...] += gpart

        dxh = dy * gv
        c = jnp.mean(dxh * xh, axis=-1, keepdims=True)
        dh_ref[...] = r * (dxh - xh * c)

    row = pl.BlockSpec((tr, d), lambda i: (i, 0))
    gsp = pl.BlockSpec((1, d), lambda i: (0, 0))
    return pl.pallas_call(
        body, name=name, grid=(t // tr,), in_specs=[row, gsp, row], out_specs=[row, gsp, gsp],
        out_shape=[jax.ShapeDtypeStruct((t, d), F32), jax.ShapeDtypeStruct((1, d), F32),
                   jax.ShapeDtypeStruct((1, d), F32)],
        compiler_params=_cp("arbitrary"),
    )(h, g, tgt)


GRP = 4 * HEAD_DIM
SB_KB = 2 * QBLK
SB_QB = 2 * QBLK


def _head_masks4(shape):
    lane = lax.broadcasted_iota(jnp.int32, shape, 1)
    return [(lane >= HEAD_DIM * h) & (lane < HEAD_DIM * (h + 1)) for h in range(4)]


def _neg_softplus(z):
    return jnp.minimum(-z, 0.0) - jnp.log(1.0 + jnp.exp(-jnp.abs(z)))


def _stacked_col_minus_row():
    rowi = lax.broadcasted_iota(jnp.int32, (4 * SB_QB, SB_KB), 0)
    coli = lax.broadcasted_iota(jnp.int32, (4 * SB_QB, SB_KB), 1)
    return coli - (rowi & (SB_QB - 1))


def _sb_fwd(proj, after, *, name):
    b, s, _ = proj.shape
    nq = s // SB_QB
    ngrp = SB_WIDTH // GRP

    def body(q_ref, k_ref, v_ref, after_ref, o_ref, r_ref, acc_ref, car_ref):
        i = pl.program_id(2)
        masks = _head_masks4((SB_QB, GRP))
        row = lax.broadcasted_iota(jnp.int32, (SB_KB, SB_KB), 0)
        col = lax.broadcasted_iota(jnp.int32, (SB_KB, SB_KB), 1)
        later_mat = (row > col).astype(BF16)
        col_minus_row = _stacked_col_minus_row()
        qs = q_ref[0] * jnp.asarray(ATT_SCALE, BF16)
        q_stack = jnp.concatenate([jnp.where(mk, qs, jnp.zeros_like(qs)) for mk in masks], axis=0)
        acc_ref[...] = jnp.zeros_like(acc_ref)
        car_ref[...] = jnp.zeros_like(car_ref)

        def process(jb, masked):
            off = pl.multiple_of(jb * SB_KB, SB_KB)
            k2 = k_ref[0, pl.ds(off, SB_KB), :]
            v2 = v_ref[0, pl.ds(off, SB_KB), :]
            z = _dot_nt(q_stack, k2)
            ls = _neg_softplus(z)
            if masked:
                causal = col_minus_row < (i * SB_QB - jb * SB_KB)
                ls = jnp.where(causal, ls, 0.0)
            later = _dot(ls.astype(BF16), later_mat)
            car = car_ref[...]
            w = jnp.exp((z + ls) + later + car)
            if masked:
                w = jnp.where(causal, w, 0.0)
            car_ref[...] = car + jnp.sum(ls, axis=1, keepdims=True)
            acc_ref[...] += _dot(w.astype(BF16), v2)

        top = (i * SB_QB) // SB_KB
        process(top, True)

        def step(jj, carry):
            process(top - 1 - jj, False)
            return carry

        lax.fori_loop(0, top, step, 0)
        o = acc_ref[pl.ds(0, SB_QB), :]
        r = car_ref[pl.ds(0, SB_QB), :]
        for h in range(1, 4):
            o = jnp.where(masks[h], acc_ref[pl.ds(h * SB_QB, SB_QB), :], o)
            r = jnp.where(masks[h], car_ref[pl.ds(h * SB_QB, SB_QB), :], r)
        o_ref[0] = o.astype(o_ref.dtype)
        r_ref[0] = r

    blk = pl.BlockSpec((1, SB_QB, GRP), lambda bb, p, i: (bb, i, p))
    return pl.pallas_call(
        body, name=name, grid=(b, ngrp, nq),
        in_specs=[blk,
                  pl.BlockSpec((1, s, GRP), lambda bb, p, i: (bb, 0, ngrp + p)),
                  pl.BlockSpec((1, s, GRP), lambda bb, p, i: (bb, 0, 2 * ngrp + p)),
                  pl.BlockSpec(memory_space=pl.ANY)],
        out_specs=[blk, blk],
        out_shape=[jax.ShapeDtypeStruct((b, s, SB_WIDTH), BF16), jax.ShapeDtypeStruct((b, s, SB_WIDTH), F32)],
        scratch_shapes=[pltpu.VMEM((4 * SB_QB, GRP), F32), pltpu.VMEM((4 * SB_QB, SB_KB), F32)],
        compiler_params=_cp("parallel", "parallel", "arbitrary"),
    )(proj, proj, proj, after)


def _sb_bwd(proj, dcat, rsum, after, *, name):
    b, s, _ = proj.shape
    nq = s // SB_QB
    ngrp = SB_WIDTH // GRP

    def body(q_ref, k_ref, v_ref, do_ref, r_ref, after_ref, dq_ref, dk_ref, dv_ref, dq_acc, cp_ref, cg_ref):
        i = pl.program_id(2)

        @pl.when(i == 0)
        def _():
            dk_ref[...] = jnp.zeros_like(dk_ref)
            dv_ref[...] = jnp.zeros_like(dv_ref)

        masks = _head_masks4((SB_QB, GRP))
        row = lax.broadcasted_iota(jnp.int32, (SB_KB, SB_KB), 0)
        col = lax.broadcasted_iota(jnp.int32, (SB_KB, SB_KB), 1)
        later_mat = (row > col).astype(BF16)
        excl_mat = (row < col).astype(BF16)
        col_minus_row = _stacked_col_minus_row()
        qs = q_ref[0] * jnp.asarray(ATT_SCALE, BF16)
        do = do_ref[0]
        q_stack = jnp.concatenate([jnp.where(mk, qs, jnp.zeros_like(qs)) for mk in masks], axis=0)
        do_stack = jnp.concatenate([jnp.where(mk, do, jnp.zeros_like(do)) for mk in masks], axis=0)
        rv = r_ref[0]
        r_stack = jnp.concatenate([rv[:, HEAD_DIM * h:HEAD_DIM * h + 1] for h in range(4)], axis=0)
        dq_acc[...] = jnp.zeros_like(dq_acc)
        cp_ref[...] = jnp.zeros_like(cp_ref)
        cg_ref[...] = jnp.zeros_like(cg_ref)

        def process(jb, masked):
            off = pl.multiple_of(jb * SB_KB, SB_KB)
            k2 = k_ref[0, pl.ds(off, SB_KB), :]
            v2 = v_ref[0, pl.ds(off, SB_KB), :]
            z = _dot_nt(q_stack, k2)
            dw = _dot_nt(do_stack, v2)
            ls = _neg_softplus(z)
            lsig = z + ls
            if masked:
                causal = col_minus_row < (i * SB_QB - jb * SB_KB)
                ls = jnp.where(causal, ls, 0.0)
            later = _dot(ls.astype(BF16), later_mat)
            cpv = cp_ref[...] + jnp.sum(ls, axis=1, keepdims=True)
            cp_ref[...] = cpv
            w = jnp.exp(lsig + ((r_stack - cpv) + later))
            if masked:
                w = jnp.where(causal, w, 0.0)
            g = dw * w
            gpre = _dot(g.astype(BF16), excl_mat)
            cgv = cg_ref[...]
            cg_ref[...] = cgv + jnp.sum(g, axis=1, keepdims=True)
            dz = g - jnp.exp(lsig) * (g + (gpre + cgv))
            if masked:
                dz = jnp.where(causal, dz, 0.0)
            dzb = dz.astype(BF16)
            dq_acc[...] += _dot(dzb, k2)
            dk_ref[0, pl.ds(off, SB_KB), :] += _dot_tn(dzb, q_stack)
            dv_ref[0, pl.ds(off, SB_KB), :] += _dot_tn(w.astype(BF16), do_stack)

        top = (i * SB_QB) // SB_KB

        def step(jb, carry):
            process(jb, False)
            return carry

        lax.fori_loop(0, top, step, 0)
        process(top, True)
        dq = dq_acc[pl.ds(0, SB_QB), :]
        for h in range(1, 4):
            dq = jnp.where(masks[h], dq_acc[pl.ds(h * SB_QB, SB_QB), :], dq)
        dq_ref[0] = (dq * ATT_SCALE).astype(dq_ref.dtype)

    blk = pl.BlockSpec((1, SB_QB, GRP), lambda bb, p, i: (bb, i, p))
    seq = pl.BlockSpec((1, s, GRP), lambda bb, p, i: (bb, 0, p))
    return pl.pallas_call(
        body, name=name, grid=(b, ngrp, nq),
        in_specs=[blk,
                  pl.BlockSpec((1, s, GRP), lambda bb, p, i: (bb, 0, ngrp + p)),
                  pl.BlockSpec((1, s, GRP), lambda bb, p, i: (bb, 0, 2 * ngrp + p)),
                  blk, blk, pl.BlockSpec(memory_space=pl.ANY)],
        out_specs=[blk, seq, seq],
        out_shape=[jax.ShapeDtypeStruct((b, s, SB_WIDTH), BF16), jax.ShapeDtypeStruct((b, s, SB_WIDTH), F32),
                   jax.ShapeDtypeStruct((b, s, SB_WIDTH), F32)],
        scratch_shapes=[pltpu.VMEM((4 * SB_QB, GRP), F32), pltpu.VMEM((4 * SB_QB, SB_KB), F32),
                        pltpu.VMEM((4 * SB_QB, SB_KB), F32)],
        compiler_params=_cp("parallel", "parallel", "arbitrary"),
    )(proj, proj, proj, dcat, rsum, after)


def _band_terms(i, slopes_scaled):
    a = lax.broadcasted_iota(jnp.int32, (QBLK, 2 * QBLK), 0)
    bcol = lax.broadcasted_iota(jnp.int32, (QBLK, 2 * QBLK), 1)
    delta = a + QBLK - bcol
    valid = (delta >= 0) & (delta <= QBLK) & ((i > 0) | (bcol >= QBLK))
    dist = delta.astype(F32)
    return valid, [(-sl) * dist for sl in slopes_scaled]


def _attn_specs(banded, q_lane_blk, k_lane_blk, v_lane_blk):
    qs = pl.BlockSpec((1, QBLK, GRP), lambda n, i: (n, i, q_lane_blk))
    if banded:
        ks = [pl.BlockSpec((1, QBLK, GRP), lambda n, i: (n, jnp.maximum(i - 1, 0), k_lane_blk)),
              pl.BlockSpec((1, QBLK, GRP), lambda n, i: (n, i, k_lane_blk))]
        vs = [pl.BlockSpec((1, QBLK, GRP), lambda n, i: (n, jnp.maximum(i - 1, 0), v_lane_blk)),
              pl.BlockSpec((1, QBLK, GRP), lambda n, i: (n, i, v_lane_blk))]
    else:
        ks = [pl.BlockSpec((1, 2 * QBLK, GRP), lambda n, i: (n, 0, k_lane_blk))]
        vs = [pl.BlockSpec((1, 2 * QBLK, GRP), lambda n, i: (n, 0, v_lane_blk))]
    return qs, ks, vs


def _attn_fwd(q, k, v, *, name, banded, slopes_scaled=None, q_lane_blk=0, k_lane_blk=0, v_lane_blk=0,
              out_dtype=F32):
    n, l, _ = q.shape
    nb = l // QBLK
    nkv = 2 if banded else 1

    def body(*refs):
        q_ref = refs[0]
        k_refs = refs[1:1 + nkv]
        v_refs = refs[1 + nkv:1 + 2 * nkv]
        o_ref, lse_ref = refs[1 + 2 * nkv:]
        i = pl.program_id(1)
        masks = _head_masks4((QBLK, GRP))
        qv = q_ref[0]
        if banded:
            k2 = jnp.concatenate([k_refs[0][0], k_refs[1][0]], axis=0)
            v2 = jnp.concatenate([v_refs[0][0], v_refs[1][0]], axis=0)
            valid, bias = _band_terms(i, slopes_scaled)
        else:
            k2 = k_refs[0][0]
            v2 = v_refs[0][0]
        qs = qv * jnp.asarray(ATT_SCALE, BF16)
        scs = [_dot_nt(jnp.where(masks[h], qs, jnp.zeros_like(qs)), k2) for h in range(4)]
        ps, dens, lses = [], [], []
        for h in range(4):
            sc = scs[h]
            if banded:
                sc = jnp.where(valid, sc + bias[h], NEG_BIG)
            m = jnp.max(sc, axis=-1, keepdims=True)
            p = jnp.exp(sc - m)
            den = jnp.sum(p, axis=-1, keepdims=True)
            ps.append(p.astype(BF16))
            dens.append(den)
            lses.append(m + jnp.log(den))
        ohs = [_dot(ps[h], v2) for h in range(4)]
        o = ohs[0] / dens[0]
        lse = jnp.broadcast_to(lses[0], (QBLK, GRP))
        for h in range(1, 4):
            o = jnp.where(masks[h], ohs[h] / dens[h], o)
            lse = jnp.where(masks[h], lses[h], lse)
        o_ref[0] = o.astype(o_ref.dtype)
        lse_ref[0] = lse

    qs, ks, vs = _attn_specs(banded, q_lane_blk, k_lane_blk, v_lane_blk)
    ob = pl.BlockSpec((1, QBLK, GRP), lambda nn, i: (nn, i, 0))
    return pl.pallas_call(
        body, name=name, grid=(n, nb),
        in_specs=[qs] + ks + vs, out_specs=[ob, ob],
        out_shape=[jax.ShapeDtypeStruct((n, l, GRP), out_dtype), jax.ShapeDtypeStruct((n, l, GRP), F32)],
        compiler_params=_cp("parallel", "arbitrary"),
    )(q, *([k] * nkv), *([v] * nkv))


def _attn_bwd(q, k, v, do, lse, delta, *, name, banded, slopes_scaled=None, q_lane_blk=0, k_lane_blk=0,
              v_lane_blk=0, do_lane_blk=0):
    n, l, _ = q.shape
    nb = l // QBLK
    nkv = 2 if banded else 1
    lk = l if banded else 2 * QBLK

    def body(*refs):
        q_ref = refs[0]
        k_refs = refs[1:1 + nkv]
        v_refs = refs[1 + nkv:1 + 2 * nkv]
        do_ref, lse_ref, dl_ref, dq_ref, dk_ref, dv_ref = refs[1 + 2 * nkv:]
        i = pl.program_id(1)

        @pl.when(i == 0)
        def _():
            dk_ref[...] = jnp.zeros_like(dk_ref)
            dv_ref[...] = jnp.zeros_like(dv_ref)

        masks = _head_masks4((QBLK, GRP))
        qv = q_ref[0]
        dov = do_ref[0]
        lsev = lse_ref[0]
        dlv = dl_ref[0]
        if banded:
            k2 = jnp.concatenate([k_refs[0][0], k_refs[1][0]], axis=0)
            v2 = jnp.concatenate([v_refs[0][0], v_refs[1][0]], axis=0)
            valid, bias = _band_terms(i, slopes_scaled)
        else:
            k2 = k_refs[0][0]
            v2 = v_refs[0][0]
        qs = qv * jnp.asarray(ATT_SCALE, BF16)
        qms = [jnp.where(masks[h], qs, jnp.zeros_like(qs)) for h in range(4)]
        doms = [jnp.where(masks[h], dov, jnp.zeros_like(dov)) for h in range(4)]
        scs = [_dot_nt(qms[h], k2) for h in range(4)]
        dps = [_dot_nt(doms[h], v2) for h in range(4)]
        pbs, dss = [], []
        for h in range(4):
            c0 = HEAD_DIM * h
            sc = scs[h]
            if banded:
                sc = jnp.where(valid, sc + bias[h], NEG_BIG)
            p = jnp.exp(sc - lsev[:, c0:c0 + 1])
            pbs.append(p.astype(BF16))
            dss.append((p * (dps[h] - dlv[:, c0:c0 + 1])).astype(BF16))
        dqs = [_dot(dss[h], k2) for h in range(4)]
        dk2 = _dot_tn(jnp.concatenate(dss, axis=0), jnp.concatenate(qms, axis=0))
        dv2 = _dot_tn(jnp.concatenate(pbs, axis=0), jnp.concatenate(doms, axis=0))
        dq = dqs[0]
        for h in range(1, 4):
            dq = jnp.where(masks[h], dqs[h], dq)
        dq_ref[0] = (dq * ATT_SCALE).astype(dq_ref.dtype)
        if banded:
            cur = pl.multiple_of(i * QBLK, QBLK)
            dk_ref[0, pl.ds(cur, QBLK), :] += dk2[QBLK:]
            dv_ref[0, pl.ds(cur, QBLK), :] += dv2[QBLK:]

            @pl.when(i > 0)
            def _():
                prev = pl.multiple_of((i - 1) * QBLK, QBLK)
                dk_ref[0, pl.ds(prev, QBLK), :] += dk2[:QBLK]
                dv_ref[0, pl.ds(prev, QBLK), :] += dv2[:QBLK]
        else:
            dk_ref[0] += dk2
            dv_ref[0] += dv2

    qs, ks, vs = _attn_specs(banded, q_lane_blk, k_lane_blk, v_lane_blk)
    ob = pl.BlockSpec((1, QBLK, GRP), lambda nn, i: (nn, i, 0))
    dos = pl.BlockSpec((1, QBLK, GRP), lambda nn, i: (nn, i, do_lane_blk))
    kvb = pl.BlockSpec((1, lk, GRP), lambda nn, i: (nn, 0, 0))
    return pl.pallas_call(
        body, name=name, grid=(n, nb),
        in_specs=[qs] + ks + vs + [dos, ob, ob], out_specs=[ob, kvb, kvb],
        out_shape=[jax.ShapeDtypeStruct((n, l, GRP), BF16), jax.ShapeDtypeStruct((n, lk, GRP), F32),
                   jax.ShapeDtypeStruct((n, lk, GRP), F32)],
        compiler_params=_cp("parallel", "arbitrary"),
    )(q, *([k] * nkv), *([v] * nkv), do, lse, delta)


def _attn_delta(do, o, *, name, lane_blks):
    t, _ = do.shape
    tr = _pick(t, (512, 256, 128, 8))
    ng = len(lane_blks)

    def body(*refs):
        do_refs, o_refs, d_ref = refs[:ng], refs[ng:2 * ng], refs[2 * ng]
        ra = lax.broadcasted_iota(jnp.int32, (GRP, GRP), 0) // HEAD_DIM
        rb = lax.broadcasted_iota(jnp.int32, (GRP, GRP), 1) // HEAD_DIM
        same_head = (ra == rb).astype(BF16)
        prod = None
        for a_ref, b_ref in zip(do_refs, o_refs):
            term = a_ref[...].astype(F32) * b_ref[...].astype(F32)
            prod = term if prod is None else prod + term
        d_ref[...] = _dot_split(prod, same_head)

    specs = [pl.BlockSpec((tr, GRP), functools.partial(lambda i, lb: (i, lb), lb=lb)) for lb in lane_blks]
    return pl.pallas_call(
        body, name=name, grid=(t // tr,), in_specs=specs + specs,
        out_specs=pl.BlockSpec((tr, GRP), lambda i: (i, 0)),
        out_shape=jax.ShapeDtypeStruct((t, GRP), F32),
        compiler_params=_cp("parallel"),
    )(*([do] * ng), *([o] * ng))


def _dil_combine(os, lses, *, name):
    t, _ = os[0].shape
    tr = _pick(t, (512, 256, 128, 8))
    ng = len(os)

    def body(*refs):
        o_refs, l_refs = refs[:ng], refs[ng:2 * ng]
        out_ref, lse_ref = refs[2 * ng:]
        ls = [r[...] for r in l_refs]
        m = functools.reduce(jnp.maximum, ls)
        tot = None
        for lv in ls:
            e = jnp.exp(lv - m)
            tot = e if tot is None else tot + e
        lse = m + jnp.log(tot)
        for g in range(ng):
            out_ref[:, GRP * g:GRP * (g + 1)] = (o_refs[g][...] * jnp.exp(ls[g] - lse)).astype(out_ref.dtype)
        lse_ref[...] = lse

    sp = pl.BlockSpec((tr, GRP), lambda i: (i, 0))
    return pl.pallas_call(
        body, name=name, grid=(t // tr,), in_specs=[sp] * (2 * ng),
        out_specs=[pl.BlockSpec((tr, GRP * ng), lambda i: (i, 0)), sp],
        out_shape=[jax.ShapeDtypeStruct((t, GRP * ng), BF16), jax.ShapeDtypeStruct((t, GRP), F32)],
        compiler_params=_cp("parallel"),
    )(*os, *lses)


FFN_LB = 256
FFN_ROWS = 64
HALO = 16


def _conv_chunk(u_ref, w, ci):
    r0 = pl.multiple_of(ci * FFN_ROWS, FFN_ROWS)
    cur = u_ref[0, pl.ds(r0, FFN_ROWS), :].astype(F32)
    p0 = pl.multiple_of(jnp.maximum(r0 - HALO, 0), HALO)
    prev = u_ref[0, pl.ds(p0, HALO), :].astype(F32)
    prev = jnp.where(ci > 0, prev, 0.0)
    rowi = lax.broadcasted_iota(jnp.int32, (8, cur.shape[1]), 0)
    r1 = pltpu.roll(cur, 1, 0)
    r2 = pltpu.roll(cur, 2, 0)
    s1 = jnp.concatenate([jnp.where(rowi == 0, prev[HALO - 1:HALO], r1[0:8]), r1[8:]], axis=0)
    s2 = jnp.concatenate([jnp.where(rowi == 0, prev[HALO - 2:HALO - 1],
                                    jnp.where(rowi == 1, prev[HALO - 1:HALO], r2[0:8])), r2[8:]], axis=0)
    c = w[0:1] * s2
    c = c + w[1:2] * s1
    c = c + w[2:3] * cur
    return c, cur, s1, s2


def _ffn_mid_fwd(u, wconv, *, name):
    b, s, f2 = u.shape
    f = f2 // 2
    nlb = f // FFN_LB

    def body(ua_ref, ug_ref, wa_ref, wg_ref, h_ref):
        wa = wa_ref[...]
        wg = wg_ref[...]

        def step(ci, carry):
            ca = _conv_chunk(ua_ref, wa, ci)[0]
            cg = _conv_chunk(ug_ref, wg, ci)[0]
            r0 = pl.multiple_of(ci * FFN_ROWS, FFN_ROWS)
            h_ref[0, pl.ds(r0, FFN_ROWS), :] = (cg * jax.nn.sigmoid(cg) * ca).astype(h_ref.dtype)
            return carry

        lax.fori_loop(0, s // FFN_ROWS, step, 0)

    return pl.pallas_call(
        body, name=name, grid=(nlb, b),
        in_specs=[pl.BlockSpec((1, s, FFN_LB), lambda l, bb: (bb, 0, l)),
                  pl.BlockSpec((1, s, FFN_LB), lambda l, bb: (bb, 0, nlb + l)),
                  pl.BlockSpec((3, FFN_LB), lambda l, bb: (0, l)),
                  pl.BlockSpec((3, FFN_LB), lambda l, bb: (0, nlb + l))],
        out_specs=pl.BlockSpec((1, s, FFN_LB), lambda l, bb: (bb, 0, l)),
        out_shape=jax.ShapeDtypeStruct((b, s, f), BF16),
        compiler_params=_cp("parallel", "parallel"),
    )(u, u, wconv, wconv)


def _ffn_mid_bwd(u, wconv, dh, *, name):
    b, s, f2 = u.shape
    f = f2 // 2
    nlb = f // FFN_LB
    nchunk = s // FFN_ROWS

    def body(ua_ref, ug_ref, wa_ref, wg_ref, dh_ref, dua_ref, dug_ref, dwa_ref, dwg_ref):
        bb = pl.program_id(1)
        wa = wa_ref[...]
        wg = wg_ref[...]
        rowi = lax.broadcasted_iota(jnp.int32, (8, FFN_LB), 0)
        last = FFN_ROWS - 8

        def conv_transpose(dc, nxt, w):
            r1 = pltpu.roll(dc, FFN_ROWS - 1, 0)
            r2 = pltpu.roll(dc, FFN_ROWS - 2, 0)
            n1 = jnp.concatenate([r1[:last], jnp.where(rowi == 7, nxt[0:1], r1[last:])], axis=0)
            n2 = jnp.concatenate([r2[:last], jnp.where(rowi == 6, nxt[0:1],
                                                       jnp.where(rowi == 7, nxt[1:2], r2[last:]))], axis=0)
            return w[2:3] * dc + w[1:2] * n1 + w[0:1] * n2

        def step(t, carry):
            ci = nchunk - 1 - t
            nxt_a, nxt_g = carry[0], carry[1]
            r0 = pl.multiple_of(ci * FFN_ROWS, FFN_ROWS)
            ca, cura, s1a, s2a = _conv_chunk(ua_ref, wa, ci)
            cg, curg, s1g, s2g = _conv_chunk(ug_ref, wg, ci)
            dhv = dh_ref[0, pl.ds(r0, FFN_ROWS), :].astype(F32)
            sg = jax.nn.sigmoid(cg)
            da = dhv * (cg * sg)
            dg = dhv * ca * (sg * (1.0 + cg * (1.0 - sg)))
            dua_ref[0, pl.ds(r0, FFN_ROWS), :] = conv_transpose(da, nxt_a, wa).astype(dua_ref.dtype)
            dug_ref[0, pl.ds(r0, FFN_ROWS), :] = conv_transpose(dg, nxt_g, wg).astype(dug_ref.dtype)
            red = lambda x: jnp.sum(x, axis=0, keepdims=True)
            parts = (red(da * s2a), red(da * s1a), red(da * cura), red(dg * s2g), red(dg * s1g), red(dg * curg))
            return (da[0:8], dg[0:8]) + tuple(c + p for c, p in zip(carry[2:], parts))

        zero = jnp.zeros((1, FFN_LB), F32)
        zero8 = jnp.zeros((8, FFN_LB), F32)
        taps = lax.fori_loop(0, nchunk, step, (zero8, zero8) + (zero,) * 6)[2:]

        @pl.when(bb == 0)
        def _():
            for k in range(3):
                dwa_ref[k:k + 1, :] = taps[k]
                dwg_ref[k:k + 1, :] = taps[3 + k]

        @pl.when(bb > 0)
        def _():
            for k in range(3):
                dwa_ref[k:k + 1, :] += taps[k]
                dwg_ref[k:k + 1, :] += taps[3 + k]

    seq_a = pl.BlockSpec((1, s, FFN_LB), lambda l, bb: (bb, 0, l))
    seq_g = pl.BlockSpec((1, s, FFN_LB), lambda l, bb: (bb, 0, nlb + l))
    wsp = pl.BlockSpec((3, FFN_LB), lambda l, bb: (0, l))
    return pl.pallas_call(
        body, name=name, grid=(nlb, b),
        in_specs=[seq_a, seq_g, wsp, pl.BlockSpec((3, FFN_LB), lambda l, bb: (0, nlb + l)), seq_a],
        out_specs=[seq_a, seq_a, wsp, wsp],
        out_shape=[jax.ShapeDtypeStruct((b, s, f), BF16), jax.ShapeDtypeStruct((b, s, f), BF16),
                   jax.ShapeDtypeStruct((3, f), F32), jax.ShapeDtypeStruct((3, f), F32)],
        compiler_params=_cp("parallel", "arbitrary"),
    )(u, u, wconv, wconv, dh)


def _adam_math(w, g, m, v):
    m2 = ADAM_B1 * m + (1.0 - ADAM_B1) * g
    v2 = ADAM_B2 * v + (1.0 - ADAM_B2) * (g * g)
    m_hat = m2 / (1.0 - ADAM_B1 ** ADAM_STEP)
    v_hat = v2 / (1.0 - ADAM_B2 ** ADAM_STEP)
    delta = -ADAM_LR * (m_hat / (jnp.sqrt(v_hat) + ADAM_EPS) + ADAM_WD * w)
    return delta, m2, v2


def _adam(w, g, m, v, *, name):
    r, c = w.shape
    tr = _pick(r, (256, 128, 88, 64, 32, 16, 8))

    def body(w_ref, g_ref, m_ref, v_ref, d_ref, m2_ref, v2_ref):
        d, m2, v2 = _adam_math(w_ref[...], g_ref[...], m_ref[...], v_ref[...])
        d_ref[...] = d
        m2_ref[...] = m2
        v2_ref[...] = v2

    sp = pl.BlockSpec((tr, c), lambda i: (i, 0))
    return pl.pallas_call(
        body, name=name, grid=(r // tr,), in_specs=[sp] * 4, out_specs=[sp] * 3,
        out_shape=[jax.ShapeDtypeStruct((r, c), F32)] * 3,
        compiler_params=_cp("parallel"),
    )(w, g, m, v)


def _adam_small(quads, *, name):
    nq = len(quads)

    def body(*refs):
        ins, outs = refs[:4 * nq], refs[4 * nq:]
        for k in range(nq):
            w_ref, g_ref, m_ref, v_ref = ins[4 * k:4 * k + 4]
            d, m2, v2 = _adam_math(w_ref[...], g_ref[...], m_ref[...], v_ref[...])
            outs[3 * k][...] = d
            outs[3 * k + 1][...] = m2
            outs[3 * k + 2][...] = v2

    flat = [a for q in quads for a in q]
    out_shape = [jax.ShapeDtypeStruct(q[0].shape, F32) for q in quads for _ in range(3)]
    vm = pl.BlockSpec(memory_space=pltpu.VMEM)
    outs = pl.pallas_call(
        body, name=name, in_specs=[vm] * len(flat), out_specs=[vm] * len(out_shape), out_shape=out_shape,
        compiler_params=pltpu.CompilerParams(vmem_limit_bytes=VMEM_LIMIT_BYTES),
    )(*flat)
    return [tuple(outs[3 * k:3 * k + 3]) for k in range(nq)]


def _mesh_pos():
    return lax.axis_index("x"), lax.axis_index("y"), lax.axis_index("c")


def _flip(v, bit):
    return 1 - v if bit else v


def _all_gather_hbm(xl, *, name):
    r, c = xl.shape

    def body(x_ref, out_ref, send_sems, recv_sems, local_sem):
        x, y, cc = _mesh_pos()
        me, sibling = (x, y, cc), (x, y, 1 - cc)
        chips = [(1 - x, y), (x, 1 - y), (1 - x, 1 - y)]

        def rows(px, py, pc):
            return out_ref.at[pl.ds((4 * px + 2 * py + pc) * r, r), :]

        def copy(k, block, to, src=None):
            return pltpu.make_async_remote_copy(
                src_ref=rows(*block) if src is None else src, dst_ref=rows(*block),
                send_sem=send_sems.at[k], recv_sem=recv_sems.at[k], device_id=to, device_id_type=MESH_ID)

        mine = pltpu.make_async_copy(x_ref, rows(*me), local_sem)
        mine.start()
        first = [copy(0, me, sibling, src=x_ref)]
        first += [copy(1 + j, me, (*chip, cc), src=x_ref) for j, chip in enumerate(chips)]
        for cp in first:
            cp.start()
        passed = [copy(4 + j, (*chip, cc), sibling) for j, chip in enumerate(chips)]
        for j, chip in enumerate(chips):
            copy(1 + j, (*chip, cc), me).wait_recv()
            passed[j].start()
        copy(0, sibling, me).wait_recv()
        for j, chip in enumerate(chips):
            copy(4 + j, (*chip, 1 - cc), me).wait_recv()
        for cp in first + passed:
            cp.wait_send()
        mine.wait()

    hbm = pl.BlockSpec(memory_space=pltpu.HBM)
    return pl.pallas_call(
        body, name=name, in_specs=[hbm], out_specs=hbm,
        out_shape=jax.ShapeDtypeStruct((N_DEV * r, c), xl.dtype),
        scratch_shapes=[pltpu.SemaphoreType.DMA((7,)), pltpu.SemaphoreType.DMA((7,)), pltpu.SemaphoreType.DMA],
    )(xl)


def _all_reduce_small(xl, *, name):
    r, c = xl.shape

    def body(x_ref, sum_ref, all_ref, send_sems, recv_sems, local_sem):
        x, y, cc = _mesh_pos()
        me, sibling = (x, y, cc), (x, y, 1 - cc)
        chips = [(1 - x, y), (x, 1 - y), (1 - x, 1 - y)]

        def rows(px, py, pc):
            return all_ref.at[pl.ds((4 * px + 2 * py + pc) * r, r), :]

        def copy(k, block, to, src=None):
            return pltpu.make_async_remote_copy(
                src_ref=rows(*block) if src is None else src, dst_ref=rows(*block),
                send_sem=send_sems.at[k], recv_sem=recv_sems.at[k], device_id=to, device_id_type=MESH_ID)

        mine = pltpu.make_async_copy(x_ref, rows(*me), local_sem)
        mine.start()
        first = [copy(0, me, sibling, src=x_ref)]
        first += [copy(1 + j, me, (*chip, cc), src=x_ref) for j, chip in enumerate(chips)]
        for cp in first:
            cp.start()
        passed = [copy(4 + j, (*chip, cc), sibling) for j, chip in enumerate(chips)]
        for j, chip in enumerate(chips):
            copy(1 + j, (*chip, cc), me).wait_recv()
            passed[j].start()
        copy(0, sibling, me).wait_recv()
        for j, chip in enumerate(chips):
            copy(4 + j, (*chip, 1 - cc), me).wait_recv()
        for cp in first + passed:
            cp.wait_send()
        mine.wait()
        tot = all_ref[pl.ds(0, r), :]
        for dd in range(1, N_DEV):
            tot = tot + all_ref[pl.ds(dd * r, r), :]
        sum_ref[...] = tot

    vm = pl.BlockSpec(memory_space=pltpu.VMEM)
    return pl.pallas_call(
        body, name=name, in_specs=[vm], out_specs=[vm, vm],
        out_shape=[jax.ShapeDtypeStruct((r, c), F32), jax.ShapeDtypeStruct((N_DEV * r, c), F32)],
        scratch_shapes=[pltpu.SemaphoreType.DMA((7,)), pltpu.SemaphoreType.DMA((7,)), pltpu.SemaphoreType.DMA],
    )(xl)[0]


def _grad_exchange(gp, *, name):
    _, r, c = gp.shape

    def body(g_ref, recv_ref, send_sems, recv_sems, local_sem):
        x, y, cc = _mesh_pos()
        me = 4 * x + 2 * y + cc
        mine = pltpu.make_async_copy(g_ref.at[me], recv_ref.at[me], local_sem)
        mine.start()
        sends, lands = [], []
        for rel in range(1, N_DEV):
            px, py, pc = _flip(x, rel & 4), _flip(y, rel & 2), _flip(cc, rel & 1)
            peer = 4 * px + 2 * py + pc
            sends.append(pltpu.make_async_remote_copy(
                src_ref=g_ref.at[peer], dst_ref=recv_ref.at[me], send_sem=send_sems.at[rel - 1],
                recv_sem=recv_sems.at[rel - 1], device_id=(px, py, pc), device_id_type=MESH_ID))
            lands.append(pltpu.make_async_remote_copy(
                src_ref=g_ref.at[me], dst_ref=recv_ref.at[peer], send_sem=send_sems.at[rel - 1],
                recv_sem=recv_sems.at[rel - 1], device_id=(px, py, pc), device_id_type=MESH_ID))
        for cp in sends:
            cp.start()
        for cp in lands:
            cp.wait_recv()
        for cp in sends:
            cp.wait_send()
        mine.wait()

    hbm = pl.BlockSpec(memory_space=pltpu.HBM)
    return pl.pallas_call(
        body, name=name, in_specs=[hbm], out_specs=hbm,
        out_shape=jax.ShapeDtypeStruct(gp.shape, gp.dtype),
        scratch_shapes=[pltpu.SemaphoreType.DMA((7,)), pltpu.SemaphoreType.DMA((7,)), pltpu.SemaphoreType.DMA],
    )(gp)


N_PEERS = N_DEV - 1
_HBM = pl.BlockSpec(memory_space=pltpu.HBM)
_SEM = pl.BlockSpec(memory_space=pltpu.SEMAPHORE)


def _peer_list(x, y, cc):
    return [(_flip(x, rel & 4), _flip(y, rel & 2), _flip(cc, rel & 1)) for rel in range(1, N_DEV)]


def _dev_index(p):
    return 4 * p[0] + 2 * p[1] + p[2]


def _split_copy(src_ref, land_ref, sems, k, peer, me, gather, landing_of):
    if gather:
        r = src_ref.shape[0]
        src = src_ref
        dst = land_ref.at[pl.ds(_dev_index(landing_of) * r, r), :]
    else:
        src = src_ref.at[_dev_index(peer)]
        dst = land_ref.at[_dev_index(landing_of)]
    return pltpu.make_async_remote_copy(src_ref=src, dst_ref=dst, send_sem=sems[k], recv_sem=sems[N_PEERS + k],
                                        device_id=peer, device_id_type=MESH_ID)


def _exchange_start(src, land_shape, *, name, gather):
    def body(src_ref, land_ref, *rest):
        sems = rest[:2 * N_PEERS]
        token = rest[2 * N_PEERS + 2]
        x, y, cc = _mesh_pos()
        me = (x, y, cc)
        for k, peer in enumerate(_peer_list(x, y, cc)):
            _split_copy(src_ref, land_ref, sems, k, peer, me, gather, landing_of=me).start()
        token[...] = jnp.zeros_like(token)

    outs = pl.pallas_call(
        body, name=name,
        out_shape=tuple([pltpu.SemaphoreType.DMA(())] * (2 * N_PEERS)) + (
            pltpu.HBM(src.shape, src.dtype), pltpu.HBM(land_shape, src.dtype),
            jax.ShapeDtypeStruct((8, LANE), F32)),
        in_specs=(_HBM, _HBM),
        out_specs=tuple([_SEM] * (2 * N_PEERS)) + (_HBM, _HBM, pl.BlockSpec(memory_space=pltpu.VMEM)),
        input_output_aliases={0: 2 * N_PEERS, 1: 2 * N_PEERS + 1},
        compiler_params=pltpu.CompilerParams(has_side_effects=pltpu.SideEffectType.DATAFLOW_SIDE_EFFECTING),
    )(pltpu.with_memory_space_constraint(src, pltpu.HBM),
      pltpu.with_memory_space_constraint(lax.empty(land_shape, src.dtype), pltpu.HBM))
    return outs[:2 * N_PEERS], outs[2 * N_PEERS], outs[2 * N_PEERS + 1], outs[2 * N_PEERS + 2]


def _exchange_wait(sems, src_thru, land_thru, after, *, name, gather):
    def body(src_ref, land_ref, *rest):
        sem_refs = rest[:2 * N_PEERS]
        x, y, cc = _mesh_pos()
        me = (x, y, cc)
        for k, peer in enumerate(_peer_list(x, y, cc)):
            cp = _split_copy(src_ref, land_ref, sem_refs, k, peer, me, gather, landing_of=peer)
            cp.wait_send()
            cp.wait_recv()

    outs = pl.pallas_call(
        body, name=name,
        out_shape=(pltpu.HBM(src_thru.shape, src_thru.dtype), pltpu.HBM(land_thru.shape, land_thru.dtype)),
        in_specs=(_HBM, _HBM) + tuple([_SEM] * (2 * N_PEERS)) + (pl.BlockSpec(memory_space=pl.ANY),),
        out_specs=(_HBM, _HBM), input_output_aliases={0: 0, 1: 1},
        compiler_params=pltpu.CompilerParams(has_side_effects=pltpu.SideEffectType.DATAFLOW_SIDE_EFFECTING),
    )(src_thru, land_thru, *sems, after)
    return outs[1]


def _sum_blocks(recv, *, name):
    nd, r, c = recv.shape
    tr = _pick(r, (448, 256, 128, 64, 32, 16))

    def body(x_ref, o_ref):
        tot = x_ref[0].astype(F32)
        for dd in range(1, nd):
            tot = tot + x_ref[dd].astype(F32)
        o_ref[...] = tot

    return pl.pallas_call(
        body, name=name, grid=(r // tr,),
        in_specs=[pl.BlockSpec((nd, tr, c), lambda i: (0, i, 0))],
        out_specs=pl.BlockSpec((tr, c), lambda i: (i, 0)),
        out_shape=jax.ShapeDtypeStruct((r, c), F32),
        compiler_params=_cp("parallel"),
    )(recv)


SHARD_KIND = {"a_w_in": "col", "a_w_out": "row", "a_w_mem_kv": "row", "a_ffn_up": "col", "a_ffn_down": "row",
              "w_kv_shared": "col", "b_w_in": "row", "b_w_out": "row", "b_w_mem_kv": "row", "b_ffn_up": "col",
              "b_ffn_down": "row"}
EARLY_WEIGHTS = ("a_w_in", "a_w_mem_kv")
LATE_WEIGHTS = tuple(nm for nm in SHARD_KIND if nm not in EARLY_WEIGHTS)


def _as2d(a):
    return a.reshape(a.shape[-2], a.shape[-1]) if a.ndim >= 2 else a.reshape(1, a.shape[0])


def _pack_local(shards):
    return jnp.concatenate([_as2d(s).astype(BF16).reshape(-1, PACK_COLS) for s in shards], axis=0)


def _unpack_full(gathered, names, shapes):
    out = {}
    r0 = 0
    for name in names:
        rows, cols = shapes[name]
        nr = rows * cols // PACK_COLS
        blk = gathered[:, r0:r0 + nr, :].reshape(N_DEV, rows, cols)
        if SHARD_KIND[name] == "row":
            out[name] = blk.reshape(N_DEV * rows, cols)
        else:
            out[name] = blk.transpose(1, 0, 2).reshape(rows, N_DEV * cols)
        r0 += nr
    return out


def _pack_grads(grads, names, shapes):
    parts = []
    for name in names:
        rows, cols = shapes[name]
        g = grads[name]
        if SHARD_KIND[name] == "row":
            blk = g.reshape(N_DEV, rows, cols)
        else:
            blk = g.reshape(rows, N_DEV, cols).transpose(1, 0, 2)
        parts.append(blk.astype(BF16).reshape(N_DEV, rows * cols // PACK_COLS, PACK_COLS))
    return jnp.concatenate(parts, axis=1)


def _unpack_local(gsum, names, shapes):
    out = {}
    r0 = 0
    for name in names:
        rows, cols = shapes[name]
        nr = rows * cols // PACK_COLS
        out[name] = gsum[r0:r0 + nr].reshape(rows, cols)
        r0 += nr
    return out


def _by_residue(t, d):
    if d == 1:
        return t
    b, s, c = t.shape
    return t.reshape(b, s // d, d, c).transpose(0, 2, 1, 3).reshape(b * d, s // d, c)


def _from_residue(t, d, b):
    if d == 1:
        return t
    n, l, c = t.shape
    return t.reshape(b, d, l, c).transpose(0, 2, 1, 3).reshape(b, l * d, c)


def _alibi_slopes():
    return [2.0 ** (-ALIBI_MAX_BIAS * (i + 1) / N_DIL_HEADS) for i in range(N_DIL_HEADS)]


def _conv_ffn_fwd(xin, gain, w_up, wconv, w_down, tag, b, s):
    (n,), r = _rms_fwd(xin, [gain], name=f"{tag}_rms_ffn")
    u = _mm(n, w_up, name=f"{tag}_up", out_dtype=BF16).reshape(b, s, -1)
    hmid = _ffn_mid_fwd(u, wconv, name=f"{tag}_ffn_mid").reshape(b * s, -1)
    xout = _mm(hmid, w_down, name=f"{tag}_down", out_dtype=F32, res=xin)
    return xout, (n, r, u, hmid)


def _conv_ffn_bwd(dxout, xin, gain, saved, w_up, wconv, w_down, tag, b, s):
    n, r, u, hmid = saved
    f = hmid.shape[1]
    dhmid = _mm(dxout, w_down, name=f"{tag}_d_hmid", out_dtype=BF16, trans_b=True)
    g_down = _mm(hmid, dxout, name=f"{tag}_g_down", out_dtype=BF16, trans_a=True)
    du_a, du_g, gc_a, gc_g = _ffn_mid_bwd(u, wconv, dhmid.reshape(b, s, f), name=f"{tag}_ffn_mid_bwd")
    du_a = du_a.reshape(b * s, f)
    du_g = du_g.reshape(b * s, f)
    dn = _mm(du_a, w_up[:, :f], name=f"{tag}_d_n_a", out_dtype=F32, trans_b=True)
    dn = _mm(du_g, w_up[:, f:], name=f"{tag}_d_n_g", out_dtype=F32, res=dn, trans_b=True)
    g_up = jnp.concatenate([_mm(n, du_a, name=f"{tag}_g_up_a", out_dtype=BF16, trans_a=True),
                            _mm(n, du_g, name=f"{tag}_g_up_g", out_dtype=BF16, trans_a=True)], axis=1)
    dxin, (g_gain,) = _rms_bwd(xin, r, [(dn, gain)], dxout, name=f"{tag}_rms_ffn_bwd")
    return dxin, g_up, g_down, jnp.concatenate([gc_a, gc_g], axis=1), g_gain


def _mem_kv_fwd(mem2d, gain, w_mem_kv, tag, b):
    (nm,), rm = _rms_fwd(mem2d, [gain], name=f"{tag}_rms_mem")
    kvm = _mm(nm, w_mem_kv, name=f"{tag}_mem_kv", out_dtype=BF16)
    return kvm.reshape(b, -1, 2 * MEM_WIDTH), (nm, rm)


def _mem_kv_bwd(dk, dv, mem2d, gain, saved, w_mem_kv, tag):
    nm, rm = saved
    dkvm = jnp.concatenate([dk, dv], axis=-1).reshape(-1, 2 * MEM_WIDTH)
    dnm = _mm(dkvm, w_mem_kv, name=f"{tag}_d_nm", out_dtype=F32, trans_b=True)
    g_w = _mm(nm, dkvm, name=f"{tag}_g_mem_kv", out_dtype=BF16, trans_a=True)
    _, (g_gain,) = _rms_bwd(mem2d, rm, [(dnm, gain)], None, name=f"{tag}_rms_mem_bwd", need_dx=False)
    return g_w, g_gain


def kernel(x, mem, a_norm_attn, a_w_in, a_w_out, a_norm_mem, a_w_mem_kv, a_norm_ffn, a_ffn_up, a_ffn_conv, a_ffn_down, kv_norm, w_kv_shared, b_norm_attn, b_w_in, b_w_out, b_norm_mem, b_w_mem_kv, b_norm_ffn, b_ffn_up, b_ffn_conv, b_ffn_down, final_norm, loss_target, m_a_norm_attn, m_a_w_in, m_a_w_out, m_a_norm_mem, m_a_w_mem_kv, m_a_norm_ffn, m_a_ffn_up, m_a_ffn_conv, m_a_ffn_down, m_kv_norm, m_w_kv_shared, m_b_norm_attn, m_b_w_in, m_b_w_out, m_b_norm_mem, m_b_w_mem_kv, m_b_norm_ffn, m_b_ffn_up, m_b_ffn_conv, m_b_ffn_down, m_final_norm, v_a_norm_attn, v_a_w_in, v_a_w_out, v_a_norm_mem, v_a_w_mem_kv, v_a_norm_ffn, v_a_ffn_up, v_a_ffn_conv, v_a_ffn_down, v_kv_norm, v_w_kv_shared, v_b_norm_attn, v_b_w_in, v_b_w_out, v_b_norm_mem, v_b_w_mem_kv, v_b_norm_ffn, v_b_ffn_up, v_b_ffn_conv, v_b_ffn_down, v_final_norm):
    names = ["a_norm_attn", "a_w_in", "a_w_out", "a_norm_mem", "a_w_mem_kv", "a_norm_ffn", "a_ffn_up",
             "a_ffn_conv", "a_ffn_down", "kv_norm", "w_kv_shared", "b_norm_attn", "b_w_in", "b_w_out",
             "b_norm_mem", "b_w_mem_kv", "b_norm_ffn", "b_ffn_up", "b_ffn_conv", "b_ffn_down", "final_norm"]
    wl = dict(zip(names, [a_norm_attn, a_w_in, a_w_out, a_norm_mem, a_w_mem_kv, a_norm_ffn, a_ffn_up,
                          a_ffn_conv, a_ffn_down, kv_norm, w_kv_shared, b_norm_attn, b_w_in, b_w_out,
                          b_norm_mem, b_w_mem_kv, b_norm_ffn, b_ffn_up, b_ffn_conv, b_ffn_down, final_norm]))
    ml = dict(zip(names, [m_a_norm_attn, m_a_w_in, m_a_w_out, m_a_norm_mem, m_a_w_mem_kv, m_a_norm_ffn,
                          m_a_ffn_up, m_a_ffn_conv, m_a_ffn_down, m_kv_norm, m_w_kv_shared, m_b_norm_attn,
                          m_b_w_in, m_b_w_out, m_b_norm_mem, m_b_w_mem_kv, m_b_norm_ffn, m_b_ffn_up,
                          m_b_ffn_conv, m_b_ffn_down, m_final_norm]))
    vl = dict(zip(names, [v_a_norm_attn, v_a_w_in, v_a_w_out, v_a_norm_mem, v_a_w_mem_kv, v_a_norm_ffn,
                          v_a_ffn_up, v_a_ffn_conv, v_a_ffn_down, v_kv_norm, v_w_kv_shared, v_b_norm_attn,
                          v_b_w_in, v_b_w_out, v_b_norm_mem, v_b_w_mem_kv, v_b_norm_ffn, v_b_ffn_up,
                          v_b_ffn_conv, v_b_ffn_down, v_final_norm]))
    b, s, d = x.shape
    t = b * s
    my_x, my_y, my_c = _mesh_pos()
    me = 4 * my_x + 2 * my_y + my_c

    shapes = {nm: _as2d(wl[nm]).shape for nm in SHARD_KIND}
    early_local = _pack_local([wl[nm] for nm in EARLY_WEIGHTS])
    early_all = _all_gather_hbm(early_local, name="gather_early").reshape(N_DEV, early_local.shape[0], PACK_COLS)
    wf = _unpack_full(early_all, EARLY_WEIGHTS, shapes)
    late_local = _pack_local([wl[nm] for nm in LATE_WEIGHTS])
    late_rows = late_local.shape[0]
    gat_sems, gat_src, gat_land, gat_token = _exchange_start(
        late_local, (N_DEV * late_rows, PACK_COLS), name="gather_late_start", gather=True)

    sharded_small = ["a_norm_attn", "a_norm_mem", "a_norm_ffn", "a_ffn_conv", "b_ffn_conv"]
    small_flat = jnp.concatenate([wl[nm].reshape(-1) for nm in sharded_small])
    n_small = small_flat.shape[0]
    small_rows = -(-n_small // (8 * LANE)) * 8
    small_local = jnp.pad(small_flat, (0, small_rows * LANE - n_small)).reshape(small_rows, LANE)
    small_all = _all_gather_hbm(small_local, name="gather_small").reshape(N_DEV, small_rows * LANE)
    sfull = {}
    r0 = 0
    for nm in sharded_small:
        rows, cols = _as2d(wl[nm]).shape
        blk = small_all[:, r0:r0 + rows * cols].reshape(N_DEV, rows, cols)
        sfull[nm] = blk.transpose(1, 0, 2).reshape(rows, N_DEV * cols)
        r0 += rows * cols
    gain = {nm: sfull[nm] for nm in ("a_norm_attn", "a_norm_mem", "a_norm_ffn")}
    for nm in ("kv_norm", "b_norm_attn", "b_norm_mem", "b_norm_ffn", "final_norm"):
        gain[nm] = _as2d(wl[nm])
    conv_a, conv_b = sfull["a_ffn_conv"], sfull["b_ffn_conv"]

    x2d = x.reshape(t, d)
    mem2d = mem.reshape(-1, d)
    tgt2d = loss_target.reshape(t, d)
    qmem_blk_a = 3 * SB_WIDTH // GRP
    qmem_blk_b = DIL_WIDTH // GRP

    (n1,), r1 = _rms_fwd(x2d, [gain["a_norm_attn"]], name="a_rms_attn")
    proj_a = _mm(n1, wf["a_w_in"], name="a_in", out_dtype=BF16).reshape(b, s, -1)
    kvm_a, mem_saved_a = _mem_kv_fwd(mem2d, gain["a_norm_mem"], wf["a_w_mem_kv"], "a", b)
    o_sb, rsum = _sb_fwd(proj_a, gat_token, name="a_sb_fwd")
    o_mem_a, lse_mem_a = _attn_fwd(proj_a, kvm_a, kvm_a, name="a_mem_fwd", banded=False,
                                   q_lane_blk=qmem_blk_a, k_lane_blk=0, v_lane_blk=1, out_dtype=BF16)
    late_land = _exchange_wait(gat_sems, gat_src, gat_land, rsum, name="gather_late_wait", gather=True)
    late_all = lax.dynamic_update_slice(late_land, late_local, (me * late_rows, 0))
    wf.update(_unpack_full(late_all.reshape(N_DEV, late_rows, PACK_COLS), LATE_WEIGHTS, shapes))
    cat_a = jnp.concatenate([o_sb, o_mem_a], axis=-1).reshape(t, d)
    x1 = _mm(cat_a, wf["a_w_out"], name="a_out", out_dtype=F32, res=x2d)
    xa, ffn_saved_a = _conv_ffn_fwd(x1, gain["a_norm_ffn"], wf["a_ffn_up"], conv_a, wf["a_ffn_down"], "a", b, s)

    (nk, n3), r3 = _rms_fwd(xa, [gain["kv_norm"], gain["b_norm_attn"]], name="b_rms_attn")
    kvsh = _mm(nk, wf["w_kv_shared"], name="kv_shared", out_dtype=BF16).reshape(b, s, -1)
    proj_b = _mm(n3, wf["b_w_in"], name="b_in", out_dtype=BF16).reshape(b, s, -1)
    kvm_b, mem_saved_b = _mem_kv_fwd(mem2d, gain["b_norm_mem"], wf["b_w_mem_kv"], "b", b)
    slopes = _alibi_slopes()
    dil_q, dil_k, dil_v, dil_o, dil_lse, dil_slopes = [], [], [], [], [], []
    for g, (_, dil) in enumerate(DIL_GROUPS):
        qg = _by_residue(proj_b[:, :, GRP * g:GRP * (g + 1)], dil)
        kg = _by_residue(kvsh[:, :, GRP * g:GRP * (g + 1)], dil)
        vg = _by_residue(kvsh[:, :, DIL_WIDTH + GRP * g:DIL_WIDTH + GRP * (g + 1)], dil)
        sl = [slopes[4 * g + h] * dil for h in range(4)]
        og, lg = _attn_fwd(qg, kg, vg, name=f"b_dil{g}_fwd", banded=True, slopes_scaled=sl)
        dil_q.append(qg)
        dil_k.append(kg)
        dil_v.append(vg)
        dil_slopes.append(sl)
        dil_o.append(_from_residue(og, dil, b).reshape(t, GRP))
        dil_lse.append(_from_residue(lg, dil, b).reshape(t, GRP))
    o_dil, lse_joint = _dil_combine(dil_o, dil_lse, name="b_dil_combine")
    o_mem_b, lse_mem_b = _attn_fwd(proj_b, kvm_b, kvm_b, name="b_mem_fwd", banded=False,
                                   q_lane_blk=qmem_blk_b, k_lane_blk=0, v_lane_blk=1, out_dtype=BF16)
    cat_b = jnp.concatenate([o_dil, o_mem_b.reshape(t, MEM_WIDTH)], axis=-1)
    x3 = _mm(cat_b, wf["b_w_out"], name="b_out", out_dtype=F32, res=xa)
    xb, ffn_saved_b = _conv_ffn_fwd(x3, gain["b_norm_ffn"], wf["b_ffn_up"], conv_b, wf["b_ffn_down"], "b", b, s)

    dxb, g_final, loss_vec = _loss_head(xb, gain["final_norm"], tgt2d, name="loss_head")

    grads = {}
    sgrads = {"final_norm": g_final}
    dx3, grads["b_ffn_up"], grads["b_ffn_down"], sgrads["b_ffn_conv"], sgrads["b_norm_ffn"] = _conv_ffn_bwd(
        dxb, x3, gain["b_norm_ffn"], ffn_saved_b, wf["b_ffn_up"], conv_b, wf["b_ffn_down"], "b", b, s)
    dcat_b = _mm(dx3, wf["b_w_out"], name="b_d_cat", out_dtype=BF16, trans_b=True)
    grads["b_w_out"] = _mm(cat_b, dx3, name="b_g_out", out_dtype=BF16, trans_a=True)
    dcat_b3 = dcat_b.reshape(b, s, d)
    delta_mem_b = _attn_delta(dcat_b, cat_b, name="b_mem_delta", lane_blks=[qmem_blk_b]).reshape(b, s, GRP)
    dq_mem_b, dkm_b, dvm_b = _attn_bwd(proj_b, kvm_b, kvm_b, dcat_b3, lse_mem_b, delta_mem_b, name="b_mem_bwd",
                                       banded=False, q_lane_blk=qmem_blk_b, k_lane_blk=0, v_lane_blk=1,
                                       do_lane_blk=qmem_blk_b)
    delta_dil = _attn_delta(dcat_b, cat_b, name="b_dil_delta", lane_blks=[0, 1, 2]).reshape(b, s, GRP)
    lse_joint3 = lse_joint.reshape(b, s, GRP)
    dq_parts, dk_parts, dv_parts = [], [], []
    for g, (_, dil) in enumerate(DIL_GROUPS):
        dog = _by_residue(dcat_b3[:, :, GRP * g:GRP * (g + 1)], dil)
        lg = _by_residue(lse_joint3, dil)
        dg = _by_residue(delta_dil, dil)
        dqg, dkg, dvg = _attn_bwd(dil_q[g], dil_k[g], dil_v[g], dog, lg, dg, name=f"b_dil{g}_bwd", banded=True,
                                  slopes_scaled=dil_slopes[g])
        dq_parts.append(_from_residue(dqg, dil, b))
        dk_parts.append(_from_residue(dkg, dil, b))
        dv_parts.append(_from_residue(dvg, dil, b))
    dproj_b = jnp.concatenate(dq_parts + [dq_mem_b], axis=-1).reshape(t, d)
    dn3 = _mm(dproj_b, wf["b_w_in"], name="b_d_n", out_dtype=F32, trans_b=True)
    grads["b_w_in"] = _mm(n3, dproj_b, name="b_g_in", out_dtype=BF16, trans_a=True)
    grads["b_w_mem_kv"], sgrads["b_norm_mem"] = _mem_kv_bwd(dkm_b, dvm_b, mem2d, gain["b_norm_mem"], mem_saved_b,
                                                           wf["b_w_mem_kv"], "b")
    dkvsh = jnp.concatenate(dk_parts + dv_parts, axis=-1).reshape(t, 2 * DIL_WIDTH).astype(BF16)
    dnk = _mm(dkvsh, wf["w_kv_shared"], name="kv_d_n", out_dtype=F32, trans_b=True)
    grads["w_kv_shared"] = _mm(nk, dkvsh, name="kv_g", out_dtype=BF16, trans_a=True)
    dxa, (sgrads["kv_norm"], sgrads["b_norm_attn"]) = _rms_bwd(
        xa, r3, [(dnk, gain["kv_norm"]), (dn3, gain["b_norm_attn"])], dx3, name="b_rms_attn_bwd")

    dx1, grads["a_ffn_up"], grads["a_ffn_down"], sgrads["a_ffn_conv"], sgrads["a_norm_ffn"] = _conv_ffn_bwd(
        dxa, x1, gain["a_norm_ffn"], ffn_saved_a, wf["a_ffn_up"], conv_a, wf["a_ffn_down"], "a", b, s)
    dcat_a = _mm(dx1, wf["a_w_out"], name="a_d_cat", out_dtype=BF16, trans_b=True)
    grads["a_w_out"] = _mm(cat_a, dx1, name="a_g_out", out_dtype=BF16, trans_a=True)
    dcat_a3 = dcat_a.reshape(b, s, d)
    delta_mem_a = _attn_delta(dcat_a, cat_a, name="a_mem_delta", lane_blks=[qmem_blk_b]).reshape(b, s, GRP)
    dq_mem_a, dkm_a, dvm_a = _attn_bwd(proj_a, kvm_a, kvm_a, dcat_a3, lse_mem_a, delta_mem_a, name="a_mem_bwd",
                                       banded=False, q_lane_blk=qmem_blk_a, k_lane_blk=0, v_lane_blk=1,
                                       do_lane_blk=qmem_blk_b)
    late_grads = _pack_grads(grads, LATE_WEIGHTS, shapes)
    ex_sems, ex_src, ex_land, ex_token = _exchange_start(late_grads, late_grads.shape, name="grads_late_start",
                                                         gather=False)
    dq_sb, dk_sb, dv_sb = _sb_bwd(proj_a, dcat_a3, rsum, ex_token, name="a_sb_bwd")
    dproj_a = jnp.concatenate([dq_sb, dk_sb.astype(BF16), dv_sb.astype(BF16), dq_mem_a], axis=-1).reshape(t, -1)
    dn1 = _mm(dproj_a, wf["a_w_in"], name="a_d_n", out_dtype=F32, trans_b=True)
    grads["a_w_in"] = _mm(n1, dproj_a, name="a_g_in", out_dtype=BF16, trans_a=True)
    grads["a_w_mem_kv"], sgrads["a_norm_mem"] = _mem_kv_bwd(dkm_a, dvm_a, mem2d, gain["a_norm_mem"], mem_saved_a,
                                                           wf["a_w_mem_kv"], "a")
    dx0, (sgrads["a_norm_attn"],) = _rms_bwd(x2d, r1, [(dn1, gain["a_norm_attn"])], dx1, name="a_rms_attn_bwd")
    grad_x = dx0.reshape(b, s, d)

    late_recv = _exchange_wait(ex_sems, ex_src, ex_land, dx0, name="grads_late_wait", gather=False)
    own = lax.dynamic_slice(late_grads, (me, 0, 0), (1,) + late_grads.shape[1:])
    late_recv = lax.dynamic_update_slice(late_recv, own, (me, 0, 0))
    gl = _unpack_local(_sum_blocks(late_recv, name="sum_grads_late"), LATE_WEIGHTS, shapes)
    early_recv = _grad_exchange(_pack_grads(grads, EARLY_WEIGHTS, shapes), name="exchange_grads_early")
    gl.update(_unpack_local(_sum_blocks(early_recv, name="sum_grads_early"), EARLY_WEIGHTS, shapes))

    small_names = ["a_norm_attn", "a_norm_mem", "a_norm_ffn", "kv_norm", "b_norm_attn", "b_norm_mem",
                   "b_norm_ffn", "final_norm", "a_ffn_conv", "b_ffn_conv"]
    small_flat = jnp.concatenate([sgrads[nm].reshape(-1) for nm in small_names] + [loss_vec.reshape(-1)])
    n_flat = small_flat.shape[0]
    red_rows = -(-n_flat // (8 * PACK_COLS)) * 8
    small_pack = jnp.pad(small_flat, (0, red_rows * PACK_COLS - n_flat)).reshape(red_rows, PACK_COLS)
    small_sum = _all_reduce_small(small_pack, name="reduce_small").reshape(-1)
    r0 = 0
    for nm in small_names:
        rows, cols = sgrads[nm].shape
        full = small_sum[r0:r0 + rows * cols].reshape(rows, cols)
        r0 += rows * cols
        if nm in sharded_small:
            lc = cols // N_DEV
            gl[nm] = lax.dynamic_slice(full, (0, me * lc), (rows, lc))
        else:
            gl[nm] = full
    loss = (0.5 / d) * jnp.sum(small_sum[r0:r0 + d])

    upd = {}
    for nm in SHARD_KIND:
        upd[nm] = _adam(_as2d(wl[nm]), gl[nm], _as2d(ml[nm]), _as2d(vl[nm]), name=f"adam_{nm}")
    res_small = _adam_small([(_as2d(wl[nm]), gl[nm], _as2d(ml[nm]), _as2d(vl[nm])) for nm in small_names],
                            name="adam_small")
    for nm, r in zip(small_names, res_small):
        upd[nm] = r

    g_out = [gl[nm].reshape(wl[nm].shape) for nm in names]
    d_out = [upd[nm][0].reshape(wl[nm].shape) for nm in names]
    m_out = [upd[nm][1].reshape(wl[nm].shape) for nm in names]
    v_out = [upd[nm][2].reshape(wl[nm].shape) for nm in names]
    return (loss, grad_x, *g_out, *d_out, *m_out, *v_out)
```

```python
import functools
import math

import jax
import jax.numpy as jnp
from jax import lax
from jax.experimental import pallas as pl
from jax.experimental.pallas import tpu as pltpu

F32 = jnp.float32
BF16 = jnp.bfloat16

N_DEV = 8
HEAD_DIM = 64
N_SB_HEADS = 12
N_DIL_HEADS = 12
DIL_GROUPS = ((128, 1), (512, 4), (2048, 16))
SB_WIDTH = N_SB_HEADS * HEAD_DIM
MEM_WIDTH = 256
DIL_WIDTH = N_DIL_HEADS * HEAD_DIM
ATT_SCALE = HEAD_DIM ** -0.5
EPS = 1e-6
ALIBI_MAX_BIAS = 8.0
NEG_BIG = -1e30

ADAM_LR = 0.001
ADAM_B1 = 0.9
ADAM_B2 = 0.999
ADAM_EPS = 1e-08
ADAM_WD = 0.01
ADAM_STEP = 10

LANE = 128
QBLK = 128
VMEM_LIMIT_BYTES = 48 * 1024 * 1024
PACK_COLS = 1024
MESH_ID = pl.DeviceIdType.MESH


def _cp(*sem):
    return pltpu.CompilerParams(dimension_semantics=sem, vmem_limit_bytes=VMEM_LIMIT_BYTES)


def _pick(n, cands):
    for c in cands:
        if n % c == 0:
            return c
    raise ValueError(f"no tile for {n} in {cands}")


def _dot(a, b):
    return jnp.dot(a, b, preferred_element_type=F32)


def _dot_nt(a, b):
    return lax.dot_general(a, b, (((1,), (1,)), ((), ())), preferred_element_type=F32)


def _dot_tn(a, b):
    return lax.dot_general(a, b, (((0,), (0,)), ((), ())), preferred_element_type=F32)


def _dot_split(x, u):
    hi = x.astype(BF16)
    lo = (x - hi.astype(F32)).astype(BF16)
    return _dot(hi, u) + _dot(lo, u)


def _mm(a, b, *, name, out_dtype, res=None, trans_a=False, trans_b=False):
    assert not (trans_a and trans_b)
    if trans_a:
        kdim, m = a.shape
    else:
        m, kdim = a.shape
    if trans_b:
        n, kb = b.shape
    else:
        kb, n = b.shape
    assert kb == kdim, (a.shape, b.shape)
    if trans_a:
        tm = _pick(m, (1408, 1024, 512, 256, 128))
        tn = _pick(n, (1024, 1280, 1408, 768, 512, 256, 128))
        tk = _pick(kdim, (512, 256))
    else:
        tm = _pick(m, (1024, 512, 256, 128))
        tn = _pick(n, (512, 1408, 256, 128))
        tk = kdim if kdim <= 2048 else _pick(kdim, (2048, 1536, 1408, 1280, 1024, 512))
    nk = kdim // tk
    has_res = res is not None

    def body(*refs):
        if has_res:
            a_ref, b_ref, r_ref, o_ref = refs[:4]
            scr = refs[4:]
        else:
            a_ref, b_ref, o_ref = refs[:3]
            r_ref = None
            scr = refs[3:]
        av = a_ref[...].astype(BF16)
        bv = b_ref[...].astype(BF16)
        if trans_a:
            p = _dot_tn(av, bv)
        elif trans_b:
            p = _dot_nt(av, bv)
        else:
            p = _dot(av, bv)

        def finish(acc):
            if has_res:
                acc = acc + r_ref[...]
            o_ref[...] = acc.astype(o_ref.dtype)

        if nk == 1:
            finish(p)
        else:
            acc_ref = scr[0]
            k = pl.program_id(2)

            @pl.when(k == 0)
            def _():
                acc_ref[...] = p

            @pl.when(k > 0)
            def _():
                acc_ref[...] += p

            @pl.when(k == nk - 1)
            def _():
                finish(acc_ref[...])

    if trans_a:
        a_spec = pl.BlockSpec((tk, tm), lambda i, j, k: (k, i))
    else:
        a_spec = pl.BlockSpec((tm, tk), lambda i, j, k: (i, k))
    if trans_b:
        b_spec = pl.BlockSpec((tn, tk), lambda i, j, k: (j, k))
    else:
        b_spec = pl.BlockSpec((tk, tn), lambda i, j, k: (k, j))
    in_specs = [a_spec, b_spec]
    args = [a, b]
    if has_res:
        in_specs.append(pl.BlockSpec((tm, tn), lambda i, j, k: (i, j)))
        args.append(res)
    return pl.pallas_call(
        body, name=name,
        grid=(m // tm, n // tn, nk),
        in_specs=in_specs,
        out_specs=pl.BlockSpec((tm, tn), lambda i, j, k: (i, j)),
        out_shape=jax.ShapeDtypeStruct((m, n), out_dtype),
        scratch_shapes=[pltpu.VMEM((tm, tn), F32)] if nk > 1 else [],
        compiler_params=_cp("parallel", "parallel", "arbitrary"),
    )(*args)


def _rms_fwd(x, gains, *, name):
    t, d = x.shape
    tr = _pick(t, (512, 256, 128, 8))
    ng = len(gains)

    def body(x_ref, *rest):
        g_refs, n_refs, r_ref = rest[:ng], rest[ng:2 * ng], rest[2 * ng]
        xv = x_ref[...]
        r = lax.rsqrt(jnp.mean(xv * xv, axis=-1, keepdims=True) + EPS)
        xh = xv * r
        for g_ref, n_ref in zip(g_refs, n_refs):
            n_ref[...] = (xh * g_ref[...]).astype(BF16)
        r_ref[...] = r

    row = pl.BlockSpec((tr, d), lambda i: (i, 0))
    gsp = pl.BlockSpec((1, d), lambda i: (0, 0))
    outs = pl.pallas_call(
        body, name=name, grid=(t // tr,),
        in_specs=[row] + [gsp] * ng,
        out_specs=[row] * ng + [pl.BlockSpec((tr, 1), lambda i: (i, 0))],
        out_shape=[jax.ShapeDtypeStruct((t, d), BF16)] * ng + [jax.ShapeDtypeStruct((t, 1), F32)],
        compiler_params=_cp("parallel"),
    )(x, *gains)
    return list(outs[:ng]), outs[ng]


def _rms_bwd(x, r, pairs, dres, *, name, need_dx=True):
    t, d = x.shape
    tr = _pick(t, (512, 256, 128, 8))
    npair = len(pairs)
    has_res = dres is not None

    def body(*refs):
        x_ref, r_ref = refs[:2]
        pr = refs[2:2 + 2 * npair]
        pos = 2 + 2 * npair
        res_ref = None
        if has_res:
            res_ref = refs[pos]
            pos += 1
        dx_ref = None
        if need_dx:
            dx_ref = refs[pos]
            pos += 1
        dg_refs = refs[pos:pos + npair]
        i = pl.program_id(0)
        rv = r_ref[...]
        xh = x_ref[...] * rv
        dx = res_ref[...] if has_res else None
        for k in range(npair):
            dn = pr[2 * k][...].astype(F32)
            g = pr[2 * k + 1][...]
            part = jnp.sum(dn * xh, axis=0, keepdims=True)

            @pl.when(i == 0)
            def _():
                dg_refs[k][...] = part

            @pl.when(i > 0)
            def _():
                dg_refs[k][...] += part

            if need_dx:
                dxh = dn * g
                c = jnp.mean(dxh * xh, axis=-1, keepdims=True)
                term = rv * (dxh - xh * c)
                dx = term if dx is None else dx + term
        if need_dx:
            dx_ref[...] = dx

    row = pl.BlockSpec((tr, d), lambda i: (i, 0))
    gsp = pl.BlockSpec((1, d), lambda i: (0, 0))
    in_specs = [row, pl.BlockSpec((tr, 1), lambda i: (i, 0))]
    args = [x, r]
    for dn, g in pairs:
        in_specs += [row, gsp]
        args += [dn, g]
    if has_res:
        in_specs.append(row)
        args.append(dres)
    out_specs, out_shape = [], []
    if need_dx:
        out_specs.append(row)
        out_shape.append(jax.ShapeDtypeStruct((t, d), F32))
    out_specs += [gsp] * npair
    out_shape += [jax.ShapeDtypeStruct((1, d), F32)] * npair
    outs = pl.pallas_call(
        body, name=name, grid=(t // tr,), in_specs=in_specs, out_specs=out_specs, out_shape=out_shape,
        compiler_params=_cp("arbitrary"),
    )(*args)
    if need_dx:
        return outs[0], list(outs[1:])
    return None, list(outs)


def _loss_head(h, g, tgt, *, name):
    t, d = h.shape
    tr = _pick(t, (512, 256, 128, 8))

    def body(h_ref, g_ref, t_ref, dh_ref, dg_ref, l_ref):
        i = pl.program_id(0)
        xv = h_ref[...]
        gv = g_ref[...]
        r = lax.rsqrt(jnp.mean(xv * xv, axis=-1, keepdims=True) + EPS)
        xh = xv * r
        e = xh * gv - t_ref[...]
        dy = e * (1.0 / d)
        lpart = jnp.sum(e * e, axis=0, keepdims=True)
        gpart = jnp.sum(dy * xh, axis=0, keepdims=True)

        @pl.when(i == 0)
        def _():
            l_ref[...] = lpart
            dg_ref[...] = gpart

        @pl.when(i > 0)
        def _():
            l_ref[...] += lpart
            dg_ref[...] += gpart

        dxh = dy * gv
        c = jnp.mean(dxh * xh, axis=-1, keepdims=True)
        dh_ref[...] = r * (dxh - xh * c)

    row = pl.BlockSpec((tr, d), lambda i: (i, 0))
    gsp = pl.BlockSpec((1, d), lambda i: (0, 0))
    return pl.pallas_call(
        body, name=name, grid=(t // tr,), in_specs=[row, gsp, row], out_specs=[row, gsp, gsp],
        out_shape=[jax.ShapeDtypeStruct((t, d), F32), jax.ShapeDtypeStruct((1, d), F32),
                   jax.ShapeDtypeStruct((1, d), F32)],
        compiler_params=_cp("arbitrary"),
    )(h, g, tgt)


GRP = 4 * HEAD_DIM
SB_KB = 2 * QBLK
SB_QB = 2 * QBLK


def _head_masks4(shape):
    lane = lax.broadcasted_iota(jnp.int32, shape, 1)
    return [(lane >= HEAD_DIM * h) & (lane < HEAD_DIM * (h + 1)) for h in range(4)]


def _neg_softplus(z):
    return jnp.minimum(-z, 0.0) - jnp.log(1.0 + jnp.exp(-jnp.abs(z)))


def _stacked_col_minus_row():
    rowi = lax.broadcasted_iota(jnp.int32, (4 * SB_QB, SB_KB), 0)
    coli = lax.broadcasted_iota(jnp.int32, (4 * SB_QB, SB_KB), 1)
    return coli - (rowi & (SB_QB - 1))


def _sb_fwd(proj, after, *, name):
    b, s, _ = proj.shape
    nq = s // SB_QB
    ngrp = SB_WIDTH // GRP

    def body(q_ref, k_ref, v_ref, after_ref, o_ref, r_ref, acc_ref, car_ref):
        i = pl.program_id(2)
        masks = _head_masks4((SB_QB, GRP))
        row = lax.broadcasted_iota(jnp.int32, (SB_KB, SB_KB), 0)
        col = lax.broadcasted_iota(jnp.int32, (SB_KB, SB_KB), 1)
        later_mat = (row > col).astype(BF16)
        col_minus_row = _stacked_col_minus_row()
        qs = q_ref[0] * jnp.asarray(ATT_SCALE, BF16)
        q_stack = jnp.concatenate([jnp.where(mk, qs, jnp.zeros_like(qs)) for mk in masks], axis=0)
        acc_ref[...] = jnp.zeros_like(acc_ref)
        car_ref[...] = jnp.zeros_like(car_ref)

        def process(jb, masked):
            off = pl.multiple_of(jb * SB_KB, SB_KB)
            k2 = k_ref[0, pl.ds(off, SB_KB), :]
            v2 = v_ref[0, pl.ds(off, SB_KB), :]
            z = _dot_nt(q_stack, k2)
            ls = _neg_softplus(z)
            if masked:
                causal = col_minus_row < (i * SB_QB - jb * SB_KB)
                ls = jnp.where(causal, ls, 0.0)
            later = _dot(ls.astype(BF16), later_mat)
            car = car_ref[...]
            w = jnp.exp((z + ls) + later + car)
            if masked:
                w = jnp.where(causal, w, 0.0)
            car_ref[...] = car + jnp.sum(ls, axis=1, keepdims=True)
            acc_ref[...] += _dot(w.astype(BF16), v2)

        top = (i * SB_QB) // SB_KB
        process(top, True)

        def step(jj, carry):
            process(top - 1 - jj, False)
            return carry

        lax.fori_loop(0, top, step, 0)
        o = acc_ref[pl.ds(0, SB_QB), :]
        r = car_ref[pl.ds(0, SB_QB), :]
        for h in range(1, 4):
            o = jnp.where(masks[h], acc_ref[pl.ds(h * SB_QB, SB_QB), :], o)
            r = jnp.where(masks[h], car_ref[pl.ds(h * SB_QB, SB_QB), :], r)
        o_ref[0] = o.astype(o_ref.dtype)
        r_ref[0] = r

    blk = pl.BlockSpec((1, SB_QB, GRP), lambda bb, p, i: (bb, i, p))
    return pl.pallas_call(
        body, name=name, grid=(b, ngrp, nq),
        in_specs=[blk,
                  pl.BlockSpec((1, s, GRP), lambda bb, p, i: (bb, 0, ngrp + p)),
                  pl.BlockSpec((1, s, GRP), lambda bb, p, i: (bb, 0, 2 * ngrp + p)),
                  pl.BlockSpec(memory_space=pl.ANY)],
        out_specs=[blk, blk],
        out_shape=[jax.ShapeDtypeStruct((b, s, SB_WIDTH), BF16), jax.ShapeDtypeStruct((b, s, SB_WIDTH), F32)],
        scratch_shapes=[pltpu.VMEM((4 * SB_QB, GRP), F32), pltpu.VMEM((4 * SB_QB, SB_KB), F32)],
        compiler_params=_cp("parallel", "parallel", "arbitrary"),
    )(proj, proj, proj, after)


def _sb_bwd(proj, dcat, rsum, after, *, name):
    b, s, _ = proj.shape
    nq = s // SB_QB
    ngrp = SB_WIDTH // GRP

    def body(q_ref, k_ref, v_ref, do_ref, r_ref, after_ref, dq_ref, dk_ref, dv_ref, dq_acc, cp_ref, cg_ref):
        i = pl.program_id(2)

        @pl.when(i == 0)
        def _():
            dk_ref[...] = jnp.zeros_like(dk_ref)
            dv_ref[...] = jnp.zeros_like(dv_ref)

        masks = _head_masks4((SB_QB, GRP))
        row = lax.broadcasted_iota(jnp.int32, (SB_KB, SB_KB), 0)
        col = lax.broadcasted_iota(jnp.int32, (SB_KB, SB_KB), 1)
        later_mat = (row > col).astype(BF16)
        excl_mat = (row < col).astype(BF16)
        col_minus_row = _stacked_col_minus_row()
        qs = q_ref[0] * jnp.asarray(ATT_SCALE, BF16)
        do = do_ref[0]
        q_stack = jnp.concatenate([jnp.where(mk, qs, jnp.zeros_like(qs)) for mk in masks], axis=0)
        do_stack = jnp.concatenate([jnp.where(mk, do, jnp.zeros_like(do)) for mk in masks], axis=0)
        rv = r_ref[0]
        r_stack = jnp.concatenate([rv[:, HEAD_DIM * h:HEAD_DIM * h + 1] for h in range(4)], axis=0)
        dq_acc[...] = jnp.zeros_like(dq_acc)
        cp_ref[...] = jnp.zeros_like(cp_ref)
        cg_ref[...] = jnp.zeros_like(cg_ref)

        def process(jb, masked):
            off = pl.multiple_of(jb * SB_KB, SB_KB)
            k2 = k_ref[0, pl.ds(off, SB_KB), :]
            v2 = v_ref[0, pl.ds(off, SB_KB), :]
            z = _dot_nt(q_stack, k2)
            dw = _dot_nt(do_stack, v2)
            ls = _neg_softplus(z)
            lsig = z + ls
            if masked:
                causal = col_minus_row < (i * SB_QB - jb * SB_KB)
                ls = jnp.where(causal, ls, 0.0)
            later = _dot(ls.astype(BF16), later_mat)
            cpv = cp_ref[...] + jnp.sum(ls, axis=1, keepdims=True)
            cp_ref[...] = cpv
            w = jnp.exp(lsig + ((r_stack - cpv) + later))
            if masked:
                w = jnp.where(causal, w, 0.0)
            g = dw * w
            gpre = _dot(g.astype(BF16), excl_mat)
            cgv = cg_ref[...]
            cg_ref[...] = cgv + jnp.sum(g, axis=1, keepdims=True)
            dz = g - jnp.exp(lsig) * (g + (gpre + cgv))
            if masked:
                dz = jnp.where(causal, dz, 0.0)
            dzb = dz.astype(BF16)
            dq_acc[...] += _dot(dzb, k2)
            dk_ref[0, pl.ds(off, SB_KB), :] += _dot_tn(dzb, q_stack)
            dv_ref[0, pl.ds(off, SB_KB), :] += _dot_tn(w.astype(BF16), do_stack)

        top = (i * SB_QB) // SB_KB

        def step(jb, carry):
            process(jb, False)
            return carry

        lax.fori_loop(0, top, step, 0)
        process(top, True)
        dq = dq_acc[pl.ds(0, SB_QB), :]
        for h in range(1, 4):
            dq = jnp.where(masks[h], dq_acc[pl.ds(h * SB_QB, SB_QB), :], dq)
        dq_ref[0] = (dq * ATT_SCALE).astype(dq_ref.dtype)

    blk = pl.BlockSpec((1, SB_QB, GRP), lambda bb, p, i: (bb, i, p))
    seq = pl.BlockSpec((1, s, GRP), lambda bb, p, i: (bb, 0, p))
    return pl.pallas_call(
        body, name=name, grid=(b, ngrp, nq),
        in_specs=[blk,
                  pl.BlockSpec((1, s, GRP), lambda bb, p, i: (bb, 0, ngrp + p)),
                  pl.BlockSpec((1, s, GRP), lambda bb, p, i: (bb, 0, 2 * ngrp + p)),
                  blk, blk, pl.BlockSpec(memory_space=pl.ANY)],
        out_specs=[blk, seq, seq],
        out_shape=[jax.ShapeDtypeStruct((b, s, SB_WIDTH), BF16), jax.ShapeDtypeStruct((b, s, SB_WIDTH), F32),
                   jax.ShapeDtypeStruct((b, s, SB_WIDTH), F32)],
        scratch_shapes=[pltpu.VMEM((4 * SB_QB, GRP), F32), pltpu.VMEM((4 * SB_QB, SB_KB), F32),
                        pltpu.VMEM((4 * SB_QB, SB_KB), F32)],
        compiler_params=_cp("parallel", "parallel", "arbitrary"),
    )(proj, proj, proj, dcat, rsum, after)


def _band_bias(slopes_scaled):
    a = lax.broadcasted_iota(jnp.int32, (QBLK, 2 * QBLK), 0)
    bcol = lax.broadcasted_iota(jnp.int32, (QBLK, 2 * QBLK), 1)
    delta = a + QBLK - bcol
    in_band = (delta >= 0) & (delta <= QBLK)
    dist = delta.astype(F32)
    bias = jnp.concatenate([(-sl) * dist for sl in slopes_scaled], axis=0)
    return jnp.concatenate([in_band] * 4, axis=0), jnp.concatenate([bcol >= QBLK] * 4, axis=0), bias


def _stack_heads(x, masks):
    return jnp.concatenate([jnp.where(mk, x, jnp.zeros_like(x)) for mk in masks], axis=0)


def _unstack_heads(x, masks):
    out = jnp.broadcast_to(x[0:QBLK], (QBLK, GRP))
    for h in range(1, 4):
        out = jnp.where(masks[h], x[h * QBLK:(h + 1) * QBLK], out)
    return out


def _head_column(x):
    return jnp.concatenate([x[:, HEAD_DIM * h:HEAD_DIM * h + 1] for h in range(4)], axis=0)


def _attn_specs(banded, nsub, q_lane_blk, k_lane_blk, v_lane_blk):
    tq = nsub * QBLK
    qs = pl.BlockSpec((1, tq, GRP), lambda n, i: (n, i, q_lane_blk))
    if banded:
        ks = [pl.BlockSpec((1, QBLK, GRP), lambda n, i: (n, jnp.maximum(nsub * i - 1, 0), k_lane_blk)),
              pl.BlockSpec((1, tq, GRP), lambda n, i: (n, i, k_lane_blk))]
        vs = [pl.BlockSpec((1, QBLK, GRP), lambda n, i: (n, jnp.maximum(nsub * i - 1, 0), v_lane_blk)),
              pl.BlockSpec((1, tq, GRP), lambda n, i: (n, i, v_lane_blk))]
    else:
        ks = [pl.BlockSpec((1, 2 * QBLK, GRP), lambda n, i: (n, 0, k_lane_blk))]
        vs = [pl.BlockSpec((1, 2 * QBLK, GRP), lambda n, i: (n, 0, v_lane_blk))]
    return qs, ks, vs


def _attn_fwd(q, k, v, *, name, banded, slopes_scaled=None, q_lane_blk=0, k_lane_blk=0, v_lane_blk=0,
              out_dtype=F32):
    n, l, _ = q.shape
    nsub = 2 if l % (2 * QBLK) == 0 else 1
    tq = nsub * QBLK
    nkv = 2 if banded else 1

    def body(*refs):
        q_ref = refs[0]
        k_refs = refs[1:1 + nkv]
        v_refs = refs[1 + nkv:1 + 2 * nkv]
        o_ref, lse_ref = refs[1 + 2 * nkv:]
        step = pl.program_id(1)
        masks = _head_masks4((QBLK, GRP))
        qs = q_ref[0] * jnp.asarray(ATT_SCALE, BF16)
        if banded:
            kall = jnp.concatenate([k_refs[0][0], k_refs[1][0]], axis=0)
            vall = jnp.concatenate([v_refs[0][0], v_refs[1][0]], axis=0)
            in_band, is_cur, bias = _band_bias(slopes_scaled)
        scs, v2s = [], []
        for u in range(nsub):
            k2 = kall[u * QBLK:(u + 2) * QBLK] if banded else k_refs[0][0]
            v2s.append(vall[u * QBLK:(u + 2) * QBLK] if banded else v_refs[0][0])
            scs.append(_dot_nt(_stack_heads(qs[u * QBLK:(u + 1) * QBLK], masks), k2))
        ps, dens, lses = [], [], []
        for u in range(nsub):
            sc = scs[u]
            if banded:
                valid = in_band & (is_cur | (step * nsub + u > 0))
                sc = jnp.where(valid, sc + bias, NEG_BIG)
            m = jnp.max(sc, axis=-1, keepdims=True)
            p = jnp.exp(sc - m)
            den = jnp.sum(p, axis=-1, keepdims=True)
            ps.append(p.astype(BF16))
            dens.append(den)
            lses.append(m + jnp.log(den))
        ohs = [_dot(ps[u], v2s[u]) for u in range(nsub)]
        for u in range(nsub):
            o_ref[0, u * QBLK:(u + 1) * QBLK, :] = _unstack_heads(ohs[u] / dens[u], masks).astype(o_ref.dtype)
            lse_ref[0, u * QBLK:(u + 1) * QBLK, :] = _unstack_heads(lses[u], masks)

    qs, ks, vs = _attn_specs(banded, nsub, q_lane_blk, k_lane_blk, v_lane_blk)
    ob = pl.BlockSpec((1, tq, GRP), lambda nn, i: (nn, i, 0))
    return pl.pallas_call(
        body, name=name, grid=(n, l // tq),
        in_specs=[qs] + ks + vs, out_specs=[ob, ob],
        out_shape=[jax.ShapeDtypeStruct((n, l, GRP), out_dtype), jax.ShapeDtypeStruct((n, l, GRP), F32)],
        compiler_params=_cp("parallel", "arbitrary"),
    )(q, *([k] * nkv), *([v] * nkv))


def _attn_bwd(q, k, v, do, lse, delta, *, name, banded, slopes_scaled=None, q_lane_blk=0, k_lane_blk=0,
              v_lane_blk=0, do_lane_blk=0):
    n, l, _ = q.shape
    nsub = 2 if l % (2 * QBLK) == 0 else 1
    tq = nsub * QBLK
    nkv = 2 if banded else 1
    lk = l if banded else 2 * QBLK

    def body(*refs):
        q_ref = refs[0]
        k_refs = refs[1:1 + nkv]
        v_refs = refs[1 + nkv:1 + 2 * nkv]
        do_ref, lse_ref, dl_ref, dq_ref, dk_ref, dv_ref = refs[1 + 2 * nkv:]
        step = pl.program_id(1)

        @pl.when(step == 0)
        def _():
            dk_ref[...] = jnp.zeros_like(dk_ref)
            dv_ref[...] = jnp.zeros_like(dv_ref)

        masks = _head_masks4((QBLK, GRP))
        qs = q_ref[0] * jnp.asarray(ATT_SCALE, BF16)
        dov = do_ref[0]
        lsev = lse_ref[0]
        dlv = dl_ref[0]
        if banded:
            kall = jnp.concatenate([k_refs[0][0], k_refs[1][0]], axis=0)
            vall = jnp.concatenate([v_refs[0][0], v_refs[1][0]], axis=0)
            in_band, is_cur, bias = _band_bias(slopes_scaled)
        q_st, do_st, k2s, scs, dps = [], [], [], [], []
        for u in range(nsub):
            rows = slice(u * QBLK, (u + 1) * QBLK)
            k2s.append(kall[u * QBLK:(u + 2) * QBLK] if banded else k_refs[0][0])
            v2 = vall[u * QBLK:(u + 2) * QBLK] if banded else v_refs[0][0]
            q_st.append(_stack_heads(qs[rows], masks))
            do_st.append(_stack_heads(dov[rows], masks))
            scs.append(_dot_nt(q_st[u], k2s[u]))
            dps.append(_dot_nt(do_st[u], v2))
        pbs, dss = [], []
        for u in range(nsub):
            rows = slice(u * QBLK, (u + 1) * QBLK)
            sc = scs[u]
            if banded:
                valid = in_band & (is_cur | (step * nsub + u > 0))
                sc = jnp.where(valid, sc + bias, NEG_BIG)
            p = jnp.exp(sc - _head_column(lsev[rows]))
            pbs.append(p.astype(BF16))
            dss.append((p * (dps[u] - _head_column(dlv[rows]))).astype(BF16))
        dqs = [_dot(dss[u], k2s[u]) for u in range(nsub)]
        dk2s = [_dot_tn(dss[u], q_st[u]) for u in range(nsub)]
        dv2s = [_dot_tn(pbs[u], do_st[u]) for u in range(nsub)]
        for u in range(nsub):
            dq_ref[0, u * QBLK:(u + 1) * QBLK, :] = (_unstack_heads(dqs[u], masks) * ATT_SCALE).astype(dq_ref.dtype)
        if banded:
            for u in range(nsub):
                i = step * nsub + u
                cur = pl.multiple_of(i * QBLK, QBLK)
                dk_ref[0, pl.ds(cur, QBLK), :] += dk2s[u][QBLK:]
                dv_ref[0, pl.ds(cur, QBLK), :] += dv2s[u][QBLK:]

                @pl.when(i > 0)
                def _():
                    prev = pl.multiple_of((i - 1) * QBLK, QBLK)
                    dk_ref[0, pl.ds(prev, QBLK), :] += dk2s[u][:QBLK]
                    dv_ref[0, pl.ds(prev, QBLK), :] += dv2s[u][:QBLK]
        else:
            dk_ref[0] += functools.reduce(jnp.add, dk2s)
            dv_ref[0] += functools.reduce(jnp.add, dv2s)

    qs, ks, vs = _attn_specs(banded, nsub, q_lane_blk, k_lane_blk, v_lane_blk)
    ob = pl.BlockSpec((1, tq, GRP), lambda nn, i: (nn, i, 0))
    dos = pl.BlockSpec((1, tq, GRP), lambda nn, i: (nn, i, do_lane_blk))
    kvb = pl.BlockSpec((1, lk, GRP), lambda nn, i: (nn, 0, 0))
    return pl.pallas_call(
        body, name=name, grid=(n, l // tq),
        in_specs=[qs] + ks + vs + [dos, ob, ob], out_specs=[ob, kvb, kvb],
        out_shape=[jax.ShapeDtypeStruct((n, l, GRP), BF16), jax.ShapeDtypeStruct((n, lk, GRP), F32),
                   jax.ShapeDtypeStruct((n, lk, GRP), F32)],
        compiler_params=_cp("parallel", "arbitrary"),
    )(q, *([k] * nkv), *([v] * nkv), do, lse, delta)


def _attn_delta(do, o, *, name, lane_blks):
    t, _ = do.shape
    tr = _pick(t, (512, 256, 128, 8))
    ng = len(lane_blks)

    def body(*refs):
        do_refs, o_refs, d_ref = refs[:ng], refs[ng:2 * ng], refs[2 * ng]
        ra = lax.broadcasted_iota(jnp.int32, (GRP, GRP), 0) // HEAD_DIM
        rb = lax.broadcasted_iota(jnp.int32, (GRP, GRP), 1) // HEAD_DIM
        same_head = (ra == rb).astype(BF16)
        prod = None
        for a_ref, b_ref in zip(do_refs, o_refs):
            term = a_ref[...].astype(F32) * b_ref[...].astype(F32)
            prod = term if prod is None else prod + term
        d_ref[...] = _dot_split(prod, same_head)

    specs = [pl.BlockSpec((tr, GRP), functools.partial(lambda i, lb: (i, lb), lb=lb)) for lb in lane_blks]
    return pl.pallas_call(
        body, name=name, grid=(t // tr,), in_specs=specs + specs,
        out_specs=pl.BlockSpec((tr, GRP), lambda i: (i, 0)),
        out_shape=jax.ShapeDtypeStruct((t, GRP), F32),
        compiler_params=_cp("parallel"),
    )(*([do] * ng), *([o] * ng))


def _dil_combine(os, lses, *, name):
    t, _ = os[0].shape
    tr = _pick(t, (512, 256, 128, 8))
    ng = len(os)

    def body(*refs):
        o_refs, l_refs = refs[:ng], refs[ng:2 * ng]
        out_ref, lse_ref = refs[2 * ng:]
        ls = [r[...] for r in l_refs]
        m = functools.reduce(jnp.maximum, ls)
        tot = None
        for lv in ls:
            e = jnp.exp(lv - m)
            tot = e if tot is None else tot + e
        lse = m + jnp.log(tot)
        for g in range(ng):
            out_ref[:, GRP * g:GRP * (g + 1)] = (o_refs[g][...] * jnp.exp(ls[g] - lse)).astype(out_ref.dtype)
        lse_ref[...] = lse

    sp = pl.BlockSpec((tr, GRP), lambda i: (i, 0))
    return pl.pallas_call(
        body, name=name, grid=(t // tr,), in_specs=[sp] * (2 * ng),
        out_specs=[pl.BlockSpec((tr, GRP * ng), lambda i: (i, 0)), sp],
        out_shape=[jax.ShapeDtypeStruct((t, GRP * ng), BF16), jax.ShapeDtypeStruct((t, GRP), F32)],
        compiler_params=_cp("parallel"),
    )(*os, *lses)


FFN_LB = 256
FFN_ROWS = 64
HALO = 16


def _conv_chunk(u_ref, w, ci):
    r0 = pl.multiple_of(ci * FFN_ROWS, FFN_ROWS)
    cur = u_ref[0, pl.ds(r0, FFN_ROWS), :].astype(F32)
    p0 = pl.multiple_of(jnp.maximum(r0 - HALO, 0), HALO)
    prev = u_ref[0, pl.ds(p0, HALO), :].astype(F32)
    prev = jnp.where(ci > 0, prev, 0.0)
    rowi = lax.broadcasted_iota(jnp.int32, (8, cur.shape[1]), 0)
    r1 = pltpu.roll(cur, 1, 0)
    r2 = pltpu.roll(cur, 2, 0)
    s1 = jnp.concatenate([jnp.where(rowi == 0, prev[HALO - 1:HALO], r1[0:8]), r1[8:]], axis=0)
    s2 = jnp.concatenate([jnp.where(rowi == 0, prev[HALO - 2:HALO - 1],
                                    jnp.where(rowi == 1, prev[HALO - 1:HALO], r2[0:8])), r2[8:]], axis=0)
    c = w[0:1] * s2
    c = c + w[1:2] * s1
    c = c + w[2:3] * cur
    return c, cur, s1, s2


def _ffn_mid_fwd(u, wconv, *, name):
    b, s, f2 = u.shape
    f = f2 // 2
    nlb = f // FFN_LB

    def body(ua_ref, ug_ref, wa_ref, wg_ref, h_ref):
        wa = wa_ref[...]
        wg = wg_ref[...]

        def step(ci, carry):
            ca = _conv_chunk(ua_ref, wa, ci)[0]
            cg = _conv_chunk(ug_ref, wg, ci)[0]
            r0 = pl.multiple_of(ci * FFN_ROWS, FFN_ROWS)
            h_ref[0, pl.ds(r0, FFN_ROWS), :] = (cg * jax.nn.sigmoid(cg) * ca).astype(h_ref.dtype)
            return carry

        lax.fori_loop(0, s // FFN_ROWS, step, 0)

    return pl.pallas_call(
        body, name=name, grid=(nlb, b),
        in_specs=[pl.BlockSpec((1, s, FFN_LB), lambda l, bb: (bb, 0, l)),
                  pl.BlockSpec((1, s, FFN_LB), lambda l, bb: (bb, 0, nlb + l)),
                  pl.BlockSpec((3, FFN_LB), lambda l, bb: (0, l)),
                  pl.BlockSpec((3, FFN_LB), lambda l, bb: (0, nlb + l))],
        out_specs=pl.BlockSpec((1, s, FFN_LB), lambda l, bb: (bb, 0, l)),
        out_shape=jax.ShapeDtypeStruct((b, s, f), BF16),
        compiler_params=_cp("parallel", "parallel"),
    )(u, u, wconv, wconv)


def _ffn_mid_bwd(u, wconv, dh, *, name):
    b, s, f2 = u.shape
    f = f2 // 2
    nlb = f // FFN_LB
    nchunk = s // FFN_ROWS

    def body(ua_ref, ug_ref, wa_ref, wg_ref, dh_ref, dua_ref, dug_ref, dwa_ref, dwg_ref):
        bb = pl.program_id(1)
        wa = wa_ref[...]
        wg = wg_ref[...]
        rowi = lax.broadcasted_iota(jnp.int32, (8, FFN_LB), 0)
        last = FFN_ROWS - 8

        def conv_transpose(dc, nxt, w):
            r1 = pltpu.roll(dc, FFN_ROWS - 1, 0)
            r2 = pltpu.roll(dc, FFN_ROWS - 2, 0)
            n1 = jnp.concatenate([r1[:last], jnp.where(rowi == 7, nxt[0:1], r1[last:])], axis=0)
            n2 = jnp.concatenate([r2[:last], jnp.where(rowi == 6, nxt[0:1],
                                                       jnp.where(rowi == 7, nxt[1:2], r2[last:]))], axis=0)
            return w[2:3] * dc + w[1:2] * n1 + w[0:1] * n2

        def step(t, carry):
            ci = nchunk - 1 - t
            nxt_a, nxt_g = carry[0], carry[1]
            r0 = pl.multiple_of(ci * FFN_ROWS, FFN_ROWS)
            ca, cura, s1a, s2a = _conv_chunk(ua_ref, wa, ci)
            cg, curg, s1g, s2g = _conv_chunk(ug_ref, wg, ci)
            dhv = dh_ref[0, pl.ds(r0, FFN_ROWS), :].astype(F32)
            sg = jax.nn.sigmoid(cg)
            da = dhv * (cg * sg)
            dg = dhv * ca * (sg * (1.0 + cg * (1.0 - sg)))
            dua_ref[0, pl.ds(r0, FFN_ROWS), :] = conv_transpose(da, nxt_a, wa).astype(dua_ref.dtype)
            dug_ref[0, pl.ds(r0, FFN_ROWS), :] = conv_transpose(dg, nxt_g, wg).astype(dug_ref.dtype)
            red = lambda x: jnp.sum(x, axis=0, keepdims=True)
            parts = (red(da * s2a), red(da * s1a), red(da * cura), red(dg * s2g), red(dg * s1g), red(dg * curg))
            return (da[0:8], dg[0:8]) + tuple(c + p for c, p in zip(carry[2:], parts))

        zero = jnp.zeros((1, FFN_LB), F32)
        zero8 = jnp.zeros((8, FFN_LB), F32)
        taps = lax.fori_loop(0, nchunk, step, (zero8, zero8) + (zero,) * 6)[2:]

        @pl.when(bb == 0)
        def _():
            for k in range(3):
                dwa_ref[k:k + 1, :] = taps[k]
                dwg_ref[k:k + 1, :] = taps[3 + k]

        @pl.when(bb > 0)
        def _():
            for k in range(3):
                dwa_ref[k:k + 1, :] += taps[k]
                dwg_ref[k:k + 1, :] += taps[3 + k]

    seq_a = pl.BlockSpec((1, s, FFN_LB), lambda l, bb: (bb, 0, l))
    seq_g = pl.BlockSpec((1, s, FFN_LB), lambda l, bb: (bb, 0, nlb + l))
    wsp = pl.BlockSpec((3, FFN_LB), lambda l, bb: (0, l))
    return pl.pallas_call(
        body, name=name, grid=(nlb, b),
        in_specs=[seq_a, seq_g, wsp, pl.BlockSpec((3, FFN_LB), lambda l, bb: (0, nlb + l)), seq_a],
        out_specs=[seq_a, seq_a, wsp, wsp],
        out_shape=[jax.ShapeDtypeStruct((b, s, f), BF16), jax.ShapeDtypeStruct((b, s, f), BF16),
                   jax.ShapeDtypeStruct((3, f), F32), jax.ShapeDtypeStruct((3, f), F32)],
        compiler_params=_cp("parallel", "arbitrary"),
    )(u, u, wconv, wconv, dh)


def _adam_math(w, g, m, v):
    m2 = ADAM_B1 * m + (1.0 - ADAM_B1) * g
    v2 = ADAM_B2 * v + (1.0 - ADAM_B2) * (g * g)
    m_hat = m2 / (1.0 - ADAM_B1 ** ADAM_STEP)
    v_hat = v2 / (1.0 - ADAM_B2 ** ADAM_STEP)
    delta = -ADAM_LR * (m_hat / (jnp.sqrt(v_hat) + ADAM_EPS) + ADAM_WD * w)
    return delta, m2, v2


def _adam(w, g, m, v, *, name):
    r, c = w.shape
    tr = _pick(r, (256, 128, 88, 64, 32, 16, 8))

    def body(w_ref, g_ref, m_ref, v_ref, d_ref, m2_ref, v2_ref):
        d, m2, v2 = _adam_math(w_ref[...], g_ref[...], m_ref[...], v_ref[...])
        d_ref[...] = d
        m2_ref[...] = m2
        v2_ref[...] = v2

    sp = pl.BlockSpec((tr, c), lambda i: (i, 0))
    return pl.pallas_call(
        body, name=name, grid=(r // tr,), in_specs=[sp] * 4, out_specs=[sp] * 3,
        out_shape=[jax.ShapeDtypeStruct((r, c), F32)] * 3,
        compiler_params=_cp("parallel"),
    )(w, g, m, v)


def _adam_small(quads, *, name):
    nq = len(quads)

    def body(*refs):
        ins, outs = refs[:4 * nq], refs[4 * nq:]
        for k in range(nq):
            w_ref, g_ref, m_ref, v_ref = ins[4 * k:4 * k + 4]
            d, m2, v2 = _adam_math(w_ref[...], g_ref[...], m_ref[...], v_ref[...])
            outs[3 * k][...] = d
            outs[3 * k + 1][...] = m2
            outs[3 * k + 2][...] = v2

    flat = [a for q in quads for a in q]
    out_shape = [jax.ShapeDtypeStruct(q[0].shape, F32) for q in quads for _ in range(3)]
    vm = pl.BlockSpec(memory_space=pltpu.VMEM)
    outs = pl.pallas_call(
        body, name=name, in_specs=[vm] * len(flat), out_specs=[vm] * len(out_shape), out_shape=out_shape,
        compiler_params=pltpu.CompilerParams(vmem_limit_bytes=VMEM_LIMIT_BYTES),
    )(*flat)
    return [tuple(outs[3 * k:3 * k + 3]) for k in range(nq)]


def _mesh_pos():
    return lax.axis_index("x"), lax.axis_index("y"), lax.axis_index("c")


def _flip(v, bit):
    return 1 - v if bit else v


def _all_gather_hbm(xl, *, name):
    r, c = xl.shape

    def body(x_ref, out_ref, send_sems, recv_sems, local_sem):
        x, y, cc = _mesh_pos()
        me, sibling = (x, y, cc), (x, y, 1 - cc)
        chips = [(1 - x, y), (x, 1 - y), (1 - x, 1 - y)]

        def rows(px, py, pc):
            return out_ref.at[pl.ds((4 * px + 2 * py + pc) * r, r), :]

        def copy(k, block, to, src=None):
            return pltpu.make_async_remote_copy(
                src_ref=rows(*block) if src is None else src, dst_ref=rows(*block),
                send_sem=send_sems.at[k], recv_sem=recv_sems.at[k], device_id=to, device_id_type=MESH_ID)

        mine = pltpu.make_async_copy(x_ref, rows(*me), local_sem)
        mine.start()
        first = [copy(0, me, sibling, src=x_ref)]
        first += [copy(1 + j, me, (*chip, cc), src=x_ref) for j, chip in enumerate(chips)]
        for cp in first:
            cp.start()
        passed = [copy(4 + j, (*chip, cc), sibling) for j, chip in enumerate(chips)]
        for j, chip in enumerate(chips):
            copy(1 + j, (*chip, cc), me).wait_recv()
            passed[j].start()
        copy(0, sibling, me).wait_recv()
        for j, chip in enumerate(chips):
            copy(4 + j, (*chip, 1 - cc), me).wait_recv()
        for cp in first + passed:
            cp.wait_send()
        mine.wait()

    hbm = pl.BlockSpec(memory_space=pltpu.HBM)
    return pl.pallas_call(
        body, name=name, in_specs=[hbm], out_specs=hbm,
        out_shape=jax.ShapeDtypeStruct((N_DEV * r, c), xl.dtype),
        scratch_shapes=[pltpu.SemaphoreType.DMA((7,)), pltpu.SemaphoreType.DMA((7,)), pltpu.SemaphoreType.DMA],
    )(xl)


def _all_reduce_small(xl, *, name):
    r, c = xl.shape

    def body(x_ref, sum_ref, all_ref, send_sems, recv_sems, local_sem):
        x, y, cc = _mesh_pos()
        me, sibling = (x, y, cc), (x, y, 1 - cc)
        chips = [(1 - x, y), (x, 1 - y), (1 - x, 1 - y)]

        def rows(px, py, pc):
            return all_ref.at[pl.ds((4 * px + 2 * py + pc) * r, r), :]

        def copy(k, block, to, src=None):
            return pltpu.make_async_remote_copy(
                src_ref=rows(*block) if src is None else src, dst_ref=rows(*block),
                send_sem=send_sems.at[k], recv_sem=recv_sems.at[k], device_id=to, device_id_type=MESH_ID)

        mine = pltpu.make_async_copy(x_ref, rows(*me), local_sem)
        mine.start()
        first = [copy(0, me, sibling, src=x_ref)]
        first += [copy(1 + j, me, (*chip, cc), src=x_ref) for j, chip in enumerate(chips)]
        for cp in first:
            cp.start()
        passed = [copy(4 + j, (*chip, cc), sibling) for j, chip in enumerate(chips)]
        for j, chip in enumerate(chips):
            copy(1 + j, (*chip, cc), me).wait_recv()
            passed[j].start()
        copy(0, sibling, me).wait_recv()
        for j, chip in enumerate(chips):
            copy(4 + j, (*chip, 1 - cc), me).wait_recv()
        for cp in first + passed:
            cp.wait_send()
        mine.wait()
        tot = all_ref[pl.ds(0, r), :]
        for dd in range(1, N_DEV):
            tot = tot + all_ref[pl.ds(dd * r, r), :]
        sum_ref[...] = tot

    vm = pl.BlockSpec(memory_space=pltpu.VMEM)
    return pl.pallas_call(
        body, name=name, in_specs=[vm], out_specs=[vm, vm],
        out_shape=[jax.ShapeDtypeStruct((r, c), F32), jax.ShapeDtypeStruct((N_DEV * r, c), F32)],
        scratch_shapes=[pltpu.SemaphoreType.DMA((7,)), pltpu.SemaphoreType.DMA((7,)), pltpu.SemaphoreType.DMA],
    )(xl)[0]


N_PEERS = N_DEV - 1
_HBM = pl.BlockSpec(memory_space=pltpu.HBM)
_SEM = pl.BlockSpec(memory_space=pltpu.SEMAPHORE)


def _peer_list(x, y, cc):
    return [(_flip(x, rel & 4), _flip(y, rel & 2), _flip(cc, rel & 1)) for rel in range(1, N_DEV)]


def _dev_index(p):
    return 4 * p[0] + 2 * p[1] + p[2]


def _split_copy(src_ref, land_ref, sems, k, peer, me, gather, landing_of):
    if gather:
        r = src_ref.shape[0]
        src = src_ref
        dst = land_ref.at[pl.ds(_dev_index(landing_of) * r, r), :]
    else:
        src = src_ref.at[_dev_index(peer)]
        dst = land_ref.at[_dev_index(landing_of)]
    return pltpu.make_async_remote_copy(src_ref=src, dst_ref=dst, send_sem=sems[k], recv_sem=sems[N_PEERS + k],
                                        device_id=peer, device_id_type=MESH_ID)


def _exchange_start(src, land_shape, *, name, gather):
    def body(src_ref, land_ref, *rest):
        sems = rest[:2 * N_PEERS]
        token = rest[2 * N_PEERS + 2]
        x, y, cc = _mesh_pos()
        me = (x, y, cc)
        for k, peer in enumerate(_peer_list(x, y, cc)):
            _split_copy(src_ref, land_ref, sems, k, peer, me, gather, landing_of=me).start()
        token[...] = jnp.zeros_like(token)

    outs = pl.pallas_call(
        body, name=name,
        out_shape=tuple([pltpu.SemaphoreType.DMA(())] * (2 * N_PEERS)) + (
            pltpu.HBM(src.shape, src.dtype), pltpu.HBM(land_shape, src.dtype),
            jax.ShapeDtypeStruct((8, LANE), F32)),
        in_specs=(_HBM, _HBM),
        out_specs=tuple([_SEM] * (2 * N_PEERS)) + (_HBM, _HBM, pl.BlockSpec(memory_space=pltpu.VMEM)),
        input_output_aliases={0: 2 * N_PEERS, 1: 2 * N_PEERS + 1},
        compiler_params=pltpu.CompilerParams(has_side_effects=pltpu.SideEffectType.DATAFLOW_SIDE_EFFECTING),
    )(pltpu.with_memory_space_constraint(src, pltpu.HBM),
      pltpu.with_memory_space_constraint(lax.empty(land_shape, src.dtype), pltpu.HBM))
    return outs[:2 * N_PEERS], outs[2 * N_PEERS], outs[2 * N_PEERS + 1], outs[2 * N_PEERS + 2]


def _exchange_wait(sems, src_thru, land_thru, after, *, name, gather):
    def body(src_ref, land_ref, *rest):
        sem_refs = rest[:2 * N_PEERS]
        x, y, cc = _mesh_pos()
        me = (x, y, cc)
        for k, peer in enumerate(_peer_list(x, y, cc)):
            cp = _split_copy(src_ref, land_ref, sem_refs, k, peer, me, gather, landing_of=peer)
            cp.wait_send()
            cp.wait_recv()

    outs = pl.pallas_call(
        body, name=name,
        out_shape=(pltpu.HBM(src_thru.shape, src_thru.dtype), pltpu.HBM(land_thru.shape, land_thru.dtype)),
        in_specs=(_HBM, _HBM) + tuple([_SEM] * (2 * N_PEERS)) + (pl.BlockSpec(memory_space=pl.ANY),),
        out_specs=(_HBM, _HBM), input_output_aliases={0: 0, 1: 1},
        compiler_params=pltpu.CompilerParams(has_side_effects=pltpu.SideEffectType.DATAFLOW_SIDE_EFFECTING),
    )(src_thru, land_thru, *sems, after)
    return outs[1]


def _sum_blocks(recv, own, *, name):
    nd, r, c = recv.shape
    tr = _pick(r, (448, 256, 128, 64, 32, 16))

    def body(x_ref, own_ref, o_ref):
        x, y, cc = _mesh_pos()
        me = 4 * x + 2 * y + cc
        tot = None
        for dd in range(nd):
            term = jnp.where(me == dd, own_ref[0], x_ref[dd]).astype(F32)
            tot = term if tot is None else tot + term
        o_ref[...] = tot

    return pl.pallas_call(
        body, name=name, grid=(r // tr,),
        in_specs=[pl.BlockSpec((nd, tr, c), lambda i: (0, i, 0)), pl.BlockSpec((1, tr, c), lambda i: (0, i, 0))],
        out_specs=pl.BlockSpec((tr, c), lambda i: (i, 0)),
        out_shape=jax.ShapeDtypeStruct((r, c), F32),
        compiler_params=_cp("parallel"),
    )(recv, own)


SHARD_KIND = {"a_w_in": "col", "a_w_out": "row", "a_w_mem_kv": "row", "a_ffn_up": "col", "a_ffn_down": "row",
              "w_kv_shared": "col", "b_w_in": "row", "b_w_out": "row", "b_w_mem_kv": "row", "b_ffn_up": "col",
              "b_ffn_down": "row"}
EARLY_WEIGHTS = ("a_w_in", "a_w_mem_kv")
LATE_WEIGHTS = tuple(nm for nm in SHARD_KIND if nm not in EARLY_WEIGHTS)


def _as2d(a):
    return a.reshape(a.shape[-2], a.shape[-1]) if a.ndim >= 2 else a.reshape(1, a.shape[0])


def _pack_local(shards):
    return jnp.concatenate([_as2d(s).astype(BF16).reshape(-1, PACK_COLS) for s in shards], axis=0)


def _unpack_full(gathered, names, shapes):
    out = {}
    r0 = 0
    for name in names:
        rows, cols = shapes[name]
        nr = rows * cols // PACK_COLS
        blk = gathered[:, r0:r0 + nr, :].reshape(N_DEV, rows, cols)
        if SHARD_KIND[name] == "row":
            out[name] = blk.reshape(N_DEV * rows, cols)
        else:
            out[name] = blk.transpose(1, 0, 2).reshape(rows, N_DEV * cols)
        r0 += nr
    return out


def _pack_grads(grads, names, shapes):
    parts = []
    for name in names:
        rows, cols = shapes[name]
        g = grads[name]
        if SHARD_KIND[name] == "row":
            blk = g.reshape(N_DEV, rows, cols)
        else:
            blk = g.reshape(rows, N_DEV, cols).transpose(1, 0, 2)
        parts.append(blk.astype(BF16).reshape(N_DEV, rows * cols // PACK_COLS, PACK_COLS))
    return jnp.concatenate(parts, axis=1)


def _unpack_local(gsum, names, shapes):
    out = {}
    r0 = 0
    for name in names:
        rows, cols = shapes[name]
        nr = rows * cols // PACK_COLS
        out[name] = gsum[r0:r0 + nr].reshape(rows, cols)
        r0 += nr
    return out


def _by_residue(t, d):
    if d == 1:
        return t
    b, s, c = t.shape
    return t.reshape(b, s // d, d, c).transpose(0, 2, 1, 3).reshape(b * d, s // d, c)


def _from_residue(t, d, b):
    if d == 1:
        return t
    n, l, c = t.shape
    return t.reshape(b, d, l, c).transpose(0, 2, 1, 3).reshape(b, l * d, c)


def _alibi_slopes():
    return [2.0 ** (-ALIBI_MAX_BIAS * (i + 1) / N_DIL_HEADS) for i in range(N_DIL_HEADS)]


def _conv_ffn_fwd(xin, gain, w_up, wconv, w_down, tag, b, s):
    (n,), r = _rms_fwd(xin, [gain], name=f"{tag}_rms_ffn")
    u = _mm(n, w_up, name=f"{tag}_up", out_dtype=BF16).reshape(b, s, -1)
    hmid = _ffn_mid_fwd(u, wconv, name=f"{tag}_ffn_mid").reshape(b * s, -1)
    xout = _mm(hmid, w_down, name=f"{tag}_down", out_dtype=F32, res=xin)
    return xout, (n, r, u, hmid)


def _conv_ffn_bwd(dxout, xin, gain, saved, w_up, wconv, w_down, tag, b, s):
    n, r, u, hmid = saved
    f = hmid.shape[1]
    dhmid = _mm(dxout, w_down, name=f"{tag}_d_hmid", out_dtype=BF16, trans_b=True)
    g_down = _mm(hmid, dxout, name=f"{tag}_g_down", out_dtype=BF16, trans_a=True)
    du_a, du_g, gc_a, gc_g = _ffn_mid_bwd(u, wconv, dhmid.reshape(b, s, f), name=f"{tag}_ffn_mid_bwd")
    du_a = du_a.reshape(b * s, f)
    du_g = du_g.reshape(b * s, f)
    dn = _mm(du_a, w_up[:, :f], name=f"{tag}_d_n_a", out_dtype=F32, trans_b=True)
    dn = _mm(du_g, w_up[:, f:], name=f"{tag}_d_n_g", out_dtype=F32, res=dn, trans_b=True)
    g_up = jnp.concatenate([_mm(n, du_a, name=f"{tag}_g_up_a", out_dtype=BF16, trans_a=True),
                            _mm(n, du_g, name=f"{tag}_g_up_g", out_dtype=BF16, trans_a=True)], axis=1)
    dxin, (g_gain,) = _rms_bwd(xin, r, [(dn, gain)], dxout, name=f"{tag}_rms_ffn_bwd")
    return dxin, g_up, g_down, jnp.concatenate([gc_a, gc_g], axis=1), g_gain


def _mem_kv_fwd(mem2d, gain, w_mem_kv, tag, b):
    (nm,), rm = _rms_fwd(mem2d, [gain], name=f"{tag}_rms_mem")
    kvm = _mm(nm, w_mem_kv, name=f"{tag}_mem_kv", out_dtype=BF16)
    return kvm.reshape(b, -1, 2 * MEM_WIDTH), (nm, rm)


def _mem_kv_bwd(dk, dv, mem2d, gain, saved, w_mem_kv, tag):
    nm, rm = saved
    dkvm = jnp.concatenate([dk, dv], axis=-1).reshape(-1, 2 * MEM_WIDTH)
    dnm = _mm(dkvm, w_mem_kv, name=f"{tag}_d_nm", out_dtype=F32, trans_b=True)
    g_w = _mm(nm, dkvm, name=f"{tag}_g_mem_kv", out_dtype=BF16, trans_a=True)
    _, (g_gain,) = _rms_bwd(mem2d, rm, [(dnm, gain)], None, name=f"{tag}_rms_mem_bwd", need_dx=False)
    return g_w, g_gain


def kernel(x, mem, a_norm_attn, a_w_in, a_w_out, a_norm_mem, a_w_mem_kv, a_norm_ffn, a_ffn_up, a_ffn_conv, a_ffn_down, kv_norm, w_kv_shared, b_norm_attn, b_w_in, b_w_out, b_norm_mem, b_w_mem_kv, b_norm_ffn, b_ffn_up, b_ffn_conv, b_ffn_down, final_norm, loss_target, m_a_norm_attn, m_a_w_in, m_a_w_out, m_a_norm_mem, m_a_w_mem_kv, m_a_norm_ffn, m_a_ffn_up, m_a_ffn_conv, m_a_ffn_down, m_kv_norm, m_w_kv_shared, m_b_norm_attn, m_b_w_in, m_b_w_out, m_b_norm_mem, m_b_w_mem_kv, m_b_norm_ffn, m_b_ffn_up, m_b_ffn_conv, m_b_ffn_down, m_final_norm, v_a_norm_attn, v_a_w_in, v_a_w_out, v_a_norm_mem, v_a_w_mem_kv, v_a_norm_ffn, v_a_ffn_up, v_a_ffn_conv, v_a_ffn_down, v_kv_norm, v_w_kv_shared, v_b_norm_attn, v_b_w_in, v_b_w_out, v_b_norm_mem, v_b_w_mem_kv, v_b_norm_ffn, v_b_ffn_up, v_b_ffn_conv, v_b_ffn_down, v_final_norm):
    names = ["a_norm_attn", "a_w_in", "a_w_out", "a_norm_mem", "a_w_mem_kv", "a_norm_ffn", "a_ffn_up",
             "a_ffn_conv", "a_ffn_down", "kv_norm", "w_kv_shared", "b_norm_attn", "b_w_in", "b_w_out",
             "b_norm_mem", "b_w_mem_kv", "b_norm_ffn", "b_ffn_up", "b_ffn_conv", "b_ffn_down", "final_norm"]
    wl = dict(zip(names, [a_norm_attn, a_w_in, a_w_out, a_norm_mem, a_w_mem_kv, a_norm_ffn, a_ffn_up,
                          a_ffn_conv, a_ffn_down, kv_norm, w_kv_shared, b_norm_attn, b_w_in, b_w_out,
                          b_norm_mem, b_w_mem_kv, b_norm_ffn, b_ffn_up, b_ffn_conv, b_ffn_down, final_norm]))
    ml = dict(zip(names, [m_a_norm_attn, m_a_w_in, m_a_w_out, m_a_norm_mem, m_a_w_mem_kv, m_a_norm_ffn,
                          m_a_ffn_up, m_a_ffn_conv, m_a_ffn_down, m_kv_norm, m_w_kv_shared, m_b_norm_attn,
                          m_b_w_in, m_b_w_out, m_b_norm_mem, m_b_w_mem_kv, m_b_norm_ffn, m_b_ffn_up,
                          m_b_ffn_conv, m_b_ffn_down, m_final_norm]))
    vl = dict(zip(names, [v_a_norm_attn, v_a_w_in, v_a_w_out, v_a_norm_mem, v_a_w_mem_kv, v_a_norm_ffn,
                          v_a_ffn_up, v_a_ffn_conv, v_a_ffn_down, v_kv_norm, v_w_kv_shared, v_b_norm_attn,
                          v_b_w_in, v_b_w_out, v_b_norm_mem, v_b_w_mem_kv, v_b_norm_ffn, v_b_ffn_up,
                          v_b_ffn_conv, v_b_ffn_down, v_final_norm]))
    b, s, d = x.shape
    t = b * s
    my_x, my_y, my_c = _mesh_pos()
    me = 4 * my_x + 2 * my_y + my_c

    shapes = {nm: _as2d(wl[nm]).shape for nm in SHARD_KIND}
    early_local = _pack_local([wl[nm] for nm in EARLY_WEIGHTS])
    early_all = _all_gather_hbm(early_local, name="gather_early").reshape(N_DEV, early_local.shape[0], PACK_COLS)
    wf = _unpack_full(early_all, EARLY_WEIGHTS, shapes)
    late_local = _pack_local([wl[nm] for nm in LATE_WEIGHTS])
    late_rows = late_local.shape[0]
    gat_sems, gat_src, gat_land, gat_token = _exchange_start(
        late_local, (N_DEV * late_rows, PACK_COLS), name="gather_late_start", gather=True)

    sharded_small = ["a_norm_attn", "a_norm_mem", "a_norm_ffn", "a_ffn_conv", "b_ffn_conv"]
    small_flat = jnp.concatenate([wl[nm].reshape(-1) for nm in sharded_small])
    n_small = small_flat.shape[0]
    small_rows = -(-n_small // (8 * LANE)) * 8
    small_local = jnp.pad(small_flat, (0, small_rows * LANE - n_small)).reshape(small_rows, LANE)
    small_all = _all_gather_hbm(small_local, name="gather_small").reshape(N_DEV, small_rows * LANE)
    sfull = {}
    r0 = 0
    for nm in sharded_small:
        rows, cols = _as2d(wl[nm]).shape
        blk = small_all[:, r0:r0 + rows * cols].reshape(N_DEV, rows, cols)
        sfull[nm] = blk.transpose(1, 0, 2).reshape(rows, N_DEV * cols)
        r0 += rows * cols
    gain = {nm: sfull[nm] for nm in ("a_norm_attn", "a_norm_mem", "a_norm_ffn")}
    for nm in ("kv_norm", "b_norm_attn", "b_norm_mem", "b_norm_ffn", "final_norm"):
        gain[nm] = _as2d(wl[nm])
    conv_a, conv_b = sfull["a_ffn_conv"], sfull["b_ffn_conv"]

    x2d = x.reshape(t, d)
    mem2d = mem.reshape(-1, d)
    tgt2d = loss_target.reshape(t, d)
    qmem_blk_a = 3 * SB_WIDTH // GRP
    qmem_blk_b = DIL_WIDTH // GRP

    (n1,), r1 = _rms_fwd(x2d, [gain["a_norm_attn"]], name="a_rms_attn")
    proj_a = _mm(n1, wf["a_w_in"], name="a_in", out_dtype=BF16).reshape(b, s, -1)
    kvm_a, mem_saved_a = _mem_kv_fwd(mem2d, gain["a_norm_mem"], wf["a_w_mem_kv"], "a", b)
    o_sb, rsum = _sb_fwd(proj_a, gat_token, name="a_sb_fwd")
    o_mem_a, lse_mem_a = _attn_fwd(proj_a, kvm_a, kvm_a, name="a_mem_fwd", banded=False,
                                   q_lane_blk=qmem_blk_a, k_lane_blk=0, v_lane_blk=1, out_dtype=BF16)
    late_land = _exchange_wait(gat_sems, gat_src, gat_land, rsum, name="gather_late_wait", gather=True)
    late_all = lax.dynamic_update_slice(late_land, late_local, (me * late_rows, 0))
    wf.update(_unpack_full(late_all.reshape(N_DEV, late_rows, PACK_COLS), LATE_WEIGHTS, shapes))
    cat_a = jnp.concatenate([o_sb, o_mem_a], axis=-1).reshape(t, d)
    x1 = _mm(cat_a, wf["a_w_out"], name="a_out", out_dtype=F32, res=x2d)
    xa, ffn_saved_a = _conv_ffn_fwd(x1, gain["a_norm_ffn"], wf["a_ffn_up"], conv_a, wf["a_ffn_down"], "a", b, s)

    (nk, n3), r3 = _rms_fwd(xa, [gain["kv_norm"], gain["b_norm_attn"]], name="b_rms_attn")
    kvsh = _mm(nk, wf["w_kv_shared"], name="kv_shared", out_dtype=BF16).reshape(b, s, -1)
    proj_b = _mm(n3, wf["b_w_in"], name="b_in", out_dtype=BF16).reshape(b, s, -1)
    kvm_b, mem_saved_b = _mem_kv_fwd(mem2d, gain["b_norm_mem"], wf["b_w_mem_kv"], "b", b)
    slopes = _alibi_slopes()
    dil_q, dil_k, dil_v, dil_o, dil_lse, dil_slopes = [], [], [], [], [], []
    for g, (_, dil) in enumerate(DIL_GROUPS):
        qg = _by_residue(proj_b[:, :, GRP * g:GRP * (g + 1)], dil)
        kg = _by_residue(kvsh[:, :, GRP * g:GRP * (g + 1)], dil)
        vg = _by_residue(kvsh[:, :, DIL_WIDTH + GRP * g:DIL_WIDTH + GRP * (g + 1)], dil)
        sl = [slopes[4 * g + h] * dil for h in range(4)]
        og, lg = _attn_fwd(qg, kg, vg, name=f"b_dil{g}_fwd", banded=True, slopes_scaled=sl)
        dil_q.append(qg)
        dil_k.append(kg)
        dil_v.append(vg)
        dil_slopes.append(sl)
        dil_o.append(_from_residue(og, dil, b).reshape(t, GRP))
        dil_lse.append(_from_residue(lg, dil, b).reshape(t, GRP))
    o_dil, lse_joint = _dil_combine(dil_o, dil_lse, name="b_dil_combine")
    o_mem_b, lse_mem_b = _attn_fwd(proj_b, kvm_b, kvm_b, name="b_mem_fwd", banded=False,
                                   q_lane_blk=qmem_blk_b, k_lane_blk=0, v_lane_blk=1, out_dtype=BF16)
    cat_b = jnp.concatenate([o_dil, o_mem_b.reshape(t, MEM_WIDTH)], axis=-1)
    x3 = _mm(cat_b, wf["b_w_out"], name="b_out", out_dtype=F32, res=xa)
    xb, ffn_saved_b = _conv_ffn_fwd(x3, gain["b_norm_ffn"], wf["b_ffn_up"], conv_b, wf["b_ffn_down"], "b", b, s)

    dxb, g_final, loss_vec = _loss_head(xb, gain["final_norm"], tgt2d, name="loss_head")

    grads = {}
    sgrads = {"final_norm": g_final}
    dx3, grads["b_ffn_up"], grads["b_ffn_down"], sgrads["b_ffn_conv"], sgrads["b_norm_ffn"] = _conv_ffn_bwd(
        dxb, x3, gain["b_norm_ffn"], ffn_saved_b, wf["b_ffn_up"], conv_b, wf["b_ffn_down"], "b", b, s)
    dcat_b = _mm(dx3, wf["b_w_out"], name="b_d_cat", out_dtype=BF16, trans_b=True)
    grads["b_w_out"] = _mm(cat_b, dx3, name="b_g_out", out_dtype=BF16, trans_a=True)
    dcat_b3 = dcat_b.reshape(b, s, d)
    delta_mem_b = _attn_delta(dcat_b, cat_b, name="b_mem_delta", lane_blks=[qmem_blk_b]).reshape(b, s, GRP)
    dq_mem_b, dkm_b, dvm_b = _attn_bwd(proj_b, kvm_b, kvm_b, dcat_b3, lse_mem_b, delta_mem_b, name="b_mem_bwd",
                                       banded=False, q_lane_blk=qmem_blk_b, k_lane_blk=0, v_lane_blk=1,
                                       do_lane_blk=qmem_blk_b)
    delta_dil = _attn_delta(dcat_b, cat_b, name="b_dil_delta", lane_blks=[0, 1, 2]).reshape(b, s, GRP)
    lse_joint3 = lse_joint.reshape(b, s, GRP)
    dq_parts, dk_parts, dv_parts = [], [], []
    for g, (_, dil) in enumerate(DIL_GROUPS):
        dog = _by_residue(dcat_b3[:, :, GRP * g:GRP * (g + 1)], dil)
        lg = _by_residue(lse_joint3, dil)
        dg = _by_residue(delta_dil, dil)
        dqg, dkg, dvg = _attn_bwd(dil_q[g], dil_k[g], dil_v[g], dog, lg, dg, name=f"b_dil{g}_bwd", banded=True,
                                  slopes_scaled=dil_slopes[g])
        dq_parts.append(_from_residue(dqg, dil, b))
        dk_parts.append(_from_residue(dkg, dil, b))
        dv_parts.append(_from_residue(dvg, dil, b))
    dproj_b = jnp.concatenate(dq_parts + [dq_mem_b], axis=-1).reshape(t, d)
    dn3 = _mm(dproj_b, wf["b_w_in"], name="b_d_n", out_dtype=F32, trans_b=True)
    grads["b_w_in"] = _mm(n3, dproj_b, name="b_g_in", out_dtype=BF16, trans_a=True)
    grads["b_w_mem_kv"], sgrads["b_norm_mem"] = _mem_kv_bwd(dkm_b, dvm_b, mem2d, gain["b_norm_mem"], mem_saved_b,
                                                           wf["b_w_mem_kv"], "b")
    dkvsh = jnp.concatenate(dk_parts + dv_parts, axis=-1).reshape(t, 2 * DIL_WIDTH).astype(BF16)
    dnk = _mm(dkvsh, wf["w_kv_shared"], name="kv_d_n", out_dtype=F32, trans_b=True)
    grads["w_kv_shared"] = _mm(nk, dkvsh, name="kv_g", out_dtype=BF16, trans_a=True)
    dxa, (sgrads["kv_norm"], sgrads["b_norm_attn"]) = _rms_bwd(
        xa, r3, [(dnk, gain["kv_norm"]), (dn3, gain["b_norm_attn"])], dx3, name="b_rms_attn_bwd")

    dx1, grads["a_ffn_up"], grads["a_ffn_down"], sgrads["a_ffn_conv"], sgrads["a_norm_ffn"] = _conv_ffn_bwd(
        dxa, x1, gain["a_norm_ffn"], ffn_saved_a, wf["a_ffn_up"], conv_a, wf["a_ffn_down"], "a", b, s)
    dcat_a = _mm(dx1, wf["a_w_out"], name="a_d_cat", out_dtype=BF16, trans_b=True)
    grads["a_w_out"] = _mm(cat_a, dx1, name="a_g_out", out_dtype=BF16, trans_a=True)
    dcat_a3 = dcat_a.reshape(b, s, d)
    delta_mem_a = _attn_delta(dcat_a, cat_a, name="a_mem_delta", lane_blks=[qmem_blk_b]).reshape(b, s, GRP)
    dq_mem_a, dkm_a, dvm_a = _attn_bwd(proj_a, kvm_a, kvm_a, dcat_a3, lse_mem_a, delta_mem_a, name="a_mem_bwd",
                                       banded=False, q_lane_blk=qmem_blk_a, k_lane_blk=0, v_lane_blk=1,
                                       do_lane_blk=qmem_blk_b)
    late_grads = _pack_grads(grads, LATE_WEIGHTS, shapes)
    ex_sems, ex_src, ex_land, ex_token = _exchange_start(late_grads, late_grads.shape, name="grads_late_start",
                                                         gather=False)
    dq_sb, dk_sb, dv_sb = _sb_bwd(proj_a, dcat_a3, rsum, ex_token, name="a_sb_bwd")
    dproj_a = jnp.concatenate([dq_sb, dk_sb.astype(BF16), dv_sb.astype(BF16), dq_mem_a], axis=-1).reshape(t, -1)
    dn1 = _mm(dproj_a, wf["a_w_in"], name="a_d_n", out_dtype=F32, trans_b=True)
    grads["a_w_in"] = _mm(n1, dproj_a, name="a_g_in", out_dtype=BF16, trans_a=True)
    grads["a_w_mem_kv"], sgrads["a_norm_mem"] = _mem_kv_bwd(dkm_a, dvm_a, mem2d, gain["a_norm_mem"], mem_saved_a,
                                                           wf["a_w_mem_kv"], "a")
    early_grads = _pack_grads(grads, EARLY_WEIGHTS, shapes)
    ee_sems, ee_src, ee_land, _ = _exchange_start(early_grads, early_grads.shape, name="grads_early_start",
                                                  gather=False)
    dx0, (sgrads["a_norm_attn"],) = _rms_bwd(x2d, r1, [(dn1, gain["a_norm_attn"])], dx1, name="a_rms_attn_bwd")
    grad_x = dx0.reshape(b, s, d)

    late_recv = _exchange_wait(ex_sems, ex_src, ex_land, dx0, name="grads_late_wait", gather=False)
    own_late = lax.dynamic_slice(late_grads, (me, 0, 0), (1,) + late_grads.shape[1:])
    gl = _unpack_local(_sum_blocks(late_recv, own_late, name="sum_grads_late"), LATE_WEIGHTS, shapes)

    small_names = ["a_norm_attn", "a_norm_mem", "a_norm_ffn", "kv_norm", "b_norm_attn", "b_norm_mem",
                   "b_norm_ffn", "final_norm", "a_ffn_conv", "b_ffn_conv"]
    small_flat = jnp.concatenate([sgrads[nm].reshape(-1) for nm in small_names] + [loss_vec.reshape(-1)])
    n_flat = small_flat.shape[0]
    red_rows = -(-n_flat // (8 * PACK_COLS)) * 8
    small_pack = jnp.pad(small_flat, (0, red_rows * PACK_COLS - n_flat)).reshape(red_rows, PACK_COLS)
    small_sum = _all_reduce_small(small_pack, name="reduce_small").reshape(-1)
    r0 = 0
    for nm in small_names:
        rows, cols = sgrads[nm].shape
        full = small_sum[r0:r0 + rows * cols].reshape(rows, cols)
        r0 += rows * cols
        if nm in sharded_small:
            lc = cols // N_DEV
            gl[nm] = lax.dynamic_slice(full, (0, me * lc), (rows, lc))
        else:
            gl[nm] = full
    loss = (0.5 / d) * jnp.sum(small_sum[r0:r0 + d])

    upd = {}
    for nm in LATE_WEIGHTS:
        upd[nm] = _adam(_as2d(wl[nm]), gl[nm], _as2d(ml[nm]), _as2d(vl[nm]), name=f"adam_{nm}")
    res_small = _adam_small([(_as2d(wl[nm]), gl[nm], _as2d(ml[nm]), _as2d(vl[nm])) for nm in small_names],
                            name="adam_small")
    for nm, r in zip(small_names, res_small):
        upd[nm] = r
    early_recv = _exchange_wait(ee_sems, ee_src, ee_land, upd[LATE_WEIGHTS[-1]][0], name="grads_early_wait",
                                gather=False)
    own_early = lax.dynamic_slice(early_grads, (me, 0, 0), (1,) + early_grads.shape[1:])
    gl.update(_unpack_local(_sum_blocks(early_recv, own_early, name="sum_grads_early"), EARLY_WEIGHTS, shapes))
    for nm in EARLY_WEIGHTS:
        upd[nm] = _adam(_as2d(wl[nm]), gl[nm], _as2d(ml[nm]), _as2d(vl[nm]), name=f"adam_{nm}")

    g_out = [gl[nm].reshape(wl[nm].shape) for nm in names]
    d_out = [upd[nm][0].reshape(wl[nm].shape) for nm in names]
    m_out = [upd[nm][1].reshape(wl[nm].shape) for nm in names]
    v_out = [upd[nm][2].reshape(wl[nm].shape) for nm in names]
    return (loss, grad_x, *g_out, *d_out, *m_out, *v_out)
```

```python
import functools
import math

import jax
import jax.numpy as jnp
from jax import lax
from jax.experimental import pallas as pl
from jax.experimental.pallas import tpu as pltpu

F32 = jnp.float32
BF16 = jnp.bfloat16

N_DEV = 8
HEAD_DIM = 64
N_SB_HEADS = 12
N_DIL_HEADS = 12
DIL_GROUPS = ((128, 1), (512, 4), (2048, 16))
SB_WIDTH = N_SB_HEADS * HEAD_DIM
MEM_WIDTH = 256
DIL_WIDTH = N_DIL_HEADS * HEAD_DIM
ATT_SCALE = HEAD_DIM ** -0.5
EPS = 1e-6
ALIBI_MAX_BIAS = 8.0
NEG_BIG = -1e30

ADAM_LR = 0.001
ADAM_B1 = 0.9
ADAM_B2 = 0.999
ADAM_EPS = 1e-08
ADAM_WD = 0.01
ADAM_STEP = 10

LANE = 128
QBLK = 128
VMEM_LIMIT_BYTES = 48 * 1024 * 1024
PACK_COLS = 1024
MESH_ID = pl.DeviceIdType.MESH


def _cp(*sem):
    return pltpu.CompilerParams(dimension_semantics=sem, vmem_limit_bytes=VMEM_LIMIT_BYTES)


def _pick(n, cands):
    for c in cands:
        if n % c == 0:
            return c
    raise ValueError(f"no tile for {n} in {cands}")


def _dot(a, b):
    return jnp.dot(a, b, preferred_element_type=F32)


def _dot_nt(a, b):
    return lax.dot_general(a, b, (((1,), (1,)), ((), ())), preferred_element_type=F32)


def _dot_tn(a, b):
    return lax.dot_general(a, b, (((0,), (0,)), ((), ())), preferred_element_type=F32)


def _dot_split(x, u):
    hi = x.astype(BF16)
    lo = (x - hi.astype(F32)).astype(BF16)
    return _dot(hi, u) + _dot(lo, u)


def _mm(a, b, *, name, out_dtype, res=None, trans_a=False, trans_b=False):
    assert not (trans_a and trans_b)
    if trans_a:
        kdim, m = a.shape
    else:
        m, kdim = a.shape
    if trans_b:
        n, kb = b.shape
    else:
        kb, n = b.shape
    assert kb == kdim, (a.shape, b.shape)
    if trans_a:
        tm = _pick(m, (1408, 1024, 512, 256, 128))
        tn = _pick(n, (1024, 1280, 1408, 768, 512, 256, 128))
        tk = _pick(kdim, (1024, 512, 256))
    else:
        tm = _pick(m, (1024, 512, 256, 128))
        tk = kdim if kdim <= 2816 else _pick(kdim, (2048, 1536, 1408, 1280, 1024, 512))
        tn = _pick(n, (512, 256, 128) if tk > 2048 else (1408, 1280, 1024, 768, 512, 256, 128))
    nk = kdim // tk
    has_res = res is not None

    def body(*refs):
        if has_res:
            a_ref, b_ref, r_ref, o_ref = refs[:4]
            scr = refs[4:]
        else:
            a_ref, b_ref, o_ref = refs[:3]
            r_ref = None
            scr = refs[3:]
        av = a_ref[...].astype(BF16)
        bv = b_ref[...].astype(BF16)
        if trans_a:
            p = _dot_tn(av, bv)
        elif trans_b:
            p = _dot_nt(av, bv)
        else:
            p = _dot(av, bv)

        def finish(acc):
            if has_res:
                acc = acc + r_ref[...]
            o_ref[...] = acc.astype(o_ref.dtype)

        if nk == 1:
            finish(p)
        else:
            acc_ref = scr[0]
            k = pl.program_id(2)

            @pl.when(k == 0)
            def _():
                acc_ref[...] = p

            @pl.when(k > 0)
            def _():
                acc_ref[...] += p

            @pl.when(k == nk - 1)
            def _():
                finish(acc_ref[...])

    if trans_a:
        a_spec = pl.BlockSpec((tk, tm), lambda i, j, k: (k, i))
    else:
        a_spec = pl.BlockSpec((tm, tk), lambda i, j, k: (i, k))
    if trans_b:
        b_spec = pl.BlockSpec((tn, tk), lambda i, j, k: (j, k))
    else:
        b_spec = pl.BlockSpec((tk, tn), lambda i, j, k: (k, j))
    in_specs = [a_spec, b_spec]
    args = [a, b]
    if has_res:
        in_specs.append(pl.BlockSpec((tm, tn), lambda i, j, k: (i, j)))
        args.append(res)
    return pl.pallas_call(
        body, name=name,
        grid=(m // tm, n // tn, nk),
        in_specs=in_specs,
        out_specs=pl.BlockSpec((tm, tn), lambda i, j, k: (i, j)),
        out_shape=jax.ShapeDtypeStruct((m, n), out_dtype),
        scratch_shapes=[pltpu.VMEM((tm, tn), F32)] if nk > 1 else [],
        compiler_params=_cp("parallel", "parallel", "arbitrary"),
    )(*args)


def _rms_fwd(x, gains, *, name):
    t, d = x.shape
    tr = _pick(t, (512, 256, 128, 8))
    ng = len(gains)

    def body(x_ref, *rest):
        g_refs, n_refs, r_ref = rest[:ng], rest[ng:2 * ng], rest[2 * ng]
        xv = x_ref[...]
        r = lax.rsqrt(jnp.mean(xv * xv, axis=-1, keepdims=True) + EPS)
        xh = xv * r
        for g_ref, n_ref in zip(g_refs, n_refs):
            n_ref[...] = (xh * g_ref[...]).astype(BF16)
        r_ref[...] = r

    row = pl.BlockSpec((tr, d), lambda i: (i, 0))
    gsp = pl.BlockSpec((1, d), lambda i: (0, 0))
    outs = pl.pallas_call(
        body, name=name, grid=(t // tr,),
        in_specs=[row] + [gsp] * ng,
        out_specs=[row] * ng + [pl.BlockSpec((tr, 1), lambda i: (i, 0))],
        out_shape=[jax.ShapeDtypeStruct((t, d), BF16)] * ng + [jax.ShapeDtypeStruct((t, 1), F32)],
        compiler_params=_cp("parallel"),
    )(x, *gains)
    return list(outs[:ng]), outs[ng]


def _rms_bwd(x, r, pairs, dres, *, name, need_dx=True):
    t, d = x.shape
    tr = _pick(t, (512, 256, 128, 8))
    npair = len(pairs)
    has_res = dres is not None

    def body(*refs):
        x_ref, r_ref = refs[:2]
        pr = refs[2:2 + 2 * npair]
        pos = 2 + 2 * npair
        res_ref = None
        if has_res:
            res_ref = refs[pos]
            pos += 1
        dx_ref = None
        if need_dx:
            dx_ref = refs[pos]
            pos += 1
        dg_refs = refs[pos:pos + npair]
        i = pl.program_id(0)
        rv = r_ref[...]
        xh = x_ref[...] * rv
        dx = res_ref[...] if has_res else None
        for k in range(npair):
            dn = pr[2 * k][...].astype(F32)
            g = pr[2 * k + 1][...]
            part = jnp.sum(dn * xh, axis=0, keepdims=True)

            @pl.when(i == 0)
            def _():
                dg_refs[k][...] = part

            @pl.when(i > 0)
            def _():
                dg_refs[k][...] += part

            if need_dx:
                dxh = dn * g
                c = jnp.mean(dxh * xh, axis=-1, keepdims=True)
                term = rv * (dxh - xh * c)
                dx = term if dx is None else dx + term
        if need_dx:
            dx_ref[...] = dx

    row = pl.BlockSpec((tr, d), lambda i: (i, 0))
    gsp = pl.BlockSpec((1, d), lambda i: (0, 0))
    in_specs = [row, pl.BlockSpec((tr, 1), lambda i: (i, 0))]
    args = [x, r]
    for dn, g in pairs:
        in_specs += [row, gsp]
        args += [dn, g]
    if has_res:
        in_specs.append(row)
        args.append(dres)
    out_specs, out_shape = [], []
    if need_dx:
        out_specs.append(row)
        out_shape.append(jax.ShapeDtypeStruct((t, d), F32))
    out_specs += [gsp] * npair
    out_shape += [jax.ShapeDtypeStruct((1, d), F32)] * npair
    outs = pl.pallas_call(
        body, name=name, grid=(t // tr,), in_specs=in_specs, out_specs=out_specs, out_shape=out_shape,
        compiler_params=_cp("arbitrary"),
    )(*args)
    if need_dx:
        return outs[0], list(outs[1:])
    return None, list(outs)


def _loss_head(h, g, tgt, *, name):
    t, d = h.shape
    tr = _pick(t, (512, 256, 128, 8))

    def body(h_ref, g_ref, t_ref, dh_ref, dg_ref, l_ref):
        i = pl.program_id(0)
        xv = h_ref[...]
        gv = g_ref[...]
        r = lax.rsqrt(jnp.mean(xv * xv, axis=-1, keepdims=True) + EPS)
        xh = xv * r
        e = xh * gv - t_ref[...]
        dy = e * (1.0 / d)
        lpart = jnp.sum(e * e, axis=0, keepdims=True)
        gpart = jnp.sum(dy * xh, axis=0, keepdims=True)

        @pl.when(i == 0)
        def _():
            l_ref[...] = lpart
            dg_ref[...] = gpart

        @pl.when(i > 0)
        def _():
            l_ref[...] += lpart
            dg_ref[...] += gpart

        dxh = dy * gv
        c = jnp.mean(dxh * xh, axis=-1, keepdims=True)
        dh_ref[...] = r * (dxh - xh * c)

    row = pl.BlockSpec((tr, d), lambda i: (i, 0))
    gsp = pl.BlockSpec((1, d), lambda i: (0, 0))
    return pl.pallas_call(
        body, name=name, grid=(t // tr,), in_specs=[row, gsp, row], out_specs=[row, gsp, gsp],
        out_shape=[jax.ShapeDtypeStruct((t, d), F32), jax.ShapeDtypeStruct((1, d), F32),
                   jax.ShapeDtypeStruct((1, d), F32)],
        compiler_params=_cp("arbitrary"),
    )(h, g, tgt)


GRP = 4 * HEAD_DIM
SB_KB = 2 * QBLK
SB_QB = 2 * QBLK


def _head_masks4(shape):
    lane = lax.broadcasted_iota(jnp.int32, shape, 1)
    return [(lane >= HEAD_DIM * h) & (lane < HEAD_DIM * (h + 1)) for h in range(4)]


def _neg_softplus(z):
    return jnp.minimum(-z, 0.0) - jnp.log(1.0 + jnp.exp(-jnp.abs(z)))


def _stacked_col_minus_row():
    rowi = lax.broadcasted_iota(jnp.int32, (4 * SB_QB, SB_KB), 0)
    coli = lax.broadcasted_iota(jnp.int32, (4 * SB_QB, SB_KB), 1)
    return coli - (rowi & (SB_QB - 1))


def _sb_fwd(proj, after, *, name):
    b, s, _ = proj.shape
    nq = s // SB_QB
    ngrp = SB_WIDTH // GRP

    def body(q_ref, k_ref, v_ref, after_ref, o_ref, r_ref, acc_ref, car_ref):
        i = pl.program_id(2)
        masks = _head_masks4((SB_QB, GRP))
        row = lax.broadcasted_iota(jnp.int32, (SB_KB, SB_KB), 0)
        col = lax.broadcasted_iota(jnp.int32, (SB_KB, SB_KB), 1)
        later_mat = (row > col).astype(BF16)
        col_minus_row = _stacked_col_minus_row()
        qs = q_ref[0] * jnp.asarray(ATT_SCALE, BF16)
        q_stack = jnp.concatenate([jnp.where(mk, qs, jnp.zeros_like(qs)) for mk in masks], axis=0)
        acc_ref[...] = jnp.zeros_like(acc_ref)
        car_ref[...] = jnp.zeros_like(car_ref)

        def process(jb, masked):
            off = pl.multiple_of(jb * SB_KB, SB_KB)
            k2 = k_ref[0, pl.ds(off, SB_KB), :]
            v2 = v_ref[0, pl.ds(off, SB_KB), :]
            z = _dot_nt(q_stack, k2)
            ls = _neg_softplus(z)
            if masked:
                causal = col_minus_row < (i * SB_QB - jb * SB_KB)
                ls = jnp.where(causal, ls, 0.0)
            later = _dot(ls.astype(BF16), later_mat)
            car = car_ref[...]
            w = jnp.exp((z + ls) + later + car)
            if masked:
                w = jnp.where(causal, w, 0.0)
            car_ref[...] = car + jnp.sum(ls, axis=1, keepdims=True)
            acc_ref[...] += _dot(w.astype(BF16), v2)

        top = (i * SB_QB) // SB_KB
        process(top, True)

        def step(jj, carry):
            process(top - 1 - jj, False)
            return carry

        lax.fori_loop(0, top, step, 0)
        o = acc_ref[pl.ds(0, SB_QB), :]
        r = car_ref[pl.ds(0, SB_QB), :]
        for h in range(1, 4):
            o = jnp.where(masks[h], acc_ref[pl.ds(h * SB_QB, SB_QB), :], o)
            r = jnp.where(masks[h], car_ref[pl.ds(h * SB_QB, SB_QB), :], r)
        o_ref[0] = o.astype(o_ref.dtype)
        r_ref[0] = r

    blk = pl.BlockSpec((1, SB_QB, GRP), lambda bb, p, i: (bb, i, p))
    return pl.pallas_call(
        body, name=name, grid=(b, ngrp, nq),
        in_specs=[blk,
                  pl.BlockSpec((1, s, GRP), lambda bb, p, i: (bb, 0, ngrp + p)),
                  pl.BlockSpec((1, s, GRP), lambda bb, p, i: (bb, 0, 2 * ngrp + p)),
                  pl.BlockSpec(memory_space=pl.ANY)],
        out_specs=[blk, blk],
        out_shape=[jax.ShapeDtypeStruct((b, s, SB_WIDTH), BF16), jax.ShapeDtypeStruct((b, s, SB_WIDTH), F32)],
        scratch_shapes=[pltpu.VMEM((4 * SB_QB, GRP), F32), pltpu.VMEM((4 * SB_QB, SB_KB), F32)],
        compiler_params=_cp("parallel", "parallel", "arbitrary"),
    )(proj, proj, proj, after)


def _sb_bwd(proj, dcat, rsum, after, *, name):
    b, s, _ = proj.shape
    nq = s // SB_QB
    ngrp = SB_WIDTH // GRP

    def body(q_ref, k_ref, v_ref, do_ref, r_ref, after_ref, dq_ref, dk_ref, dv_ref, dq_acc, cp_ref, cg_ref):
        i = pl.program_id(2)

        @pl.when(i == 0)
        def _():
            dk_ref[...] = jnp.zeros_like(dk_ref)
            dv_ref[...] = jnp.zeros_like(dv_ref)

        masks = _head_masks4((SB_QB, GRP))
        row = lax.broadcasted_iota(jnp.int32, (SB_KB, SB_KB), 0)
        col = lax.broadcasted_iota(jnp.int32, (SB_KB, SB_KB), 1)
        later_mat = (row > col).astype(BF16)
        excl_mat = (row < col).astype(BF16)
        col_minus_row = _stacked_col_minus_row()
        qs = q_ref[0] * jnp.asarray(ATT_SCALE, BF16)
        do = do_ref[0]
        q_stack = jnp.concatenate([jnp.where(mk, qs, jnp.zeros_like(qs)) for mk in masks], axis=0)
        do_stack = jnp.concatenate([jnp.where(mk, do, jnp.zeros_like(do)) for mk in masks], axis=0)
        rv = r_ref[0]
        r_stack = jnp.concatenate([rv[:, HEAD_DIM * h:HEAD_DIM * h + 1] for h in range(4)], axis=0)
        dq_acc[...] = jnp.zeros_like(dq_acc)
        cp_ref[...] = jnp.zeros_like(cp_ref)
        cg_ref[...] = jnp.zeros_like(cg_ref)

        def process(jb, masked):
            off = pl.multiple_of(jb * SB_KB, SB_KB)
            k2 = k_ref[0, pl.ds(off, SB_KB), :]
            v2 = v_ref[0, pl.ds(off, SB_KB), :]
            z = _dot_nt(q_stack, k2)
            dw = _dot_nt(do_stack, v2)
            ls = _neg_softplus(z)
            lsig = z + ls
            if masked:
                causal = col_minus_row < (i * SB_QB - jb * SB_KB)
                ls = jnp.where(causal, ls, 0.0)
            later = _dot(ls.astype(BF16), later_mat)
            cpv = cp_ref[...] + jnp.sum(ls, axis=1, keepdims=True)
            cp_ref[...] = cpv
            w = jnp.exp(lsig + ((r_stack - cpv) + later))
            if masked:
                w = jnp.where(causal, w, 0.0)
            g = dw * w
            gpre = _dot(g.astype(BF16), excl_mat)
            cgv = cg_ref[...]
            cg_ref[...] = cgv + jnp.sum(g, axis=1, keepdims=True)
            dz = g - jnp.exp(lsig) * (g + (gpre + cgv))
            if masked:
                dz = jnp.where(causal, dz, 0.0)
            dzb = dz.astype(BF16)
            dq_acc[...] += _dot(dzb, k2)
            dk_ref[0, pl.ds(off, SB_KB), :] += _dot_tn(dzb, q_stack)
            dv_ref[0, pl.ds(off, SB_KB), :] += _dot_tn(w.astype(BF16), do_stack)

        top = (i * SB_QB) // SB_KB

        def step(jb, carry):
            process(jb, False)
            return carry

        lax.fori_loop(0, top, step, 0)
        process(top, True)
        dq = dq_acc[pl.ds(0, SB_QB), :]
        for h in range(1, 4):
            dq = jnp.where(masks[h], dq_acc[pl.ds(h * SB_QB, SB_QB), :], dq)
        dq_ref[0] = (dq * ATT_SCALE).astype(dq_ref.dtype)

    blk = pl.BlockSpec((1, SB_QB, GRP), lambda bb, p, i: (bb, i, p))
    seq = pl.BlockSpec((1, s, GRP), lambda bb, p, i: (bb, 0, p))
    return pl.pallas_call(
        body, name=name, grid=(b, ngrp, nq),
        in_specs=[blk,
                  pl.BlockSpec((1, s, GRP), lambda bb, p, i: (bb, 0, ngrp + p)),
                  pl.BlockSpec((1, s, GRP), lambda bb, p, i: (bb, 0, 2 * ngrp + p)),
                  blk, blk, pl.BlockSpec(memory_space=pl.ANY)],
        out_specs=[blk, seq, seq],
        out_shape=[jax.ShapeDtypeStruct((b, s, SB_WIDTH), BF16), jax.ShapeDtypeStruct((b, s, SB_WIDTH), F32),
                   jax.ShapeDtypeStruct((b, s, SB_WIDTH), F32)],
        scratch_shapes=[pltpu.VMEM((4 * SB_QB, GRP), F32), pltpu.VMEM((4 * SB_QB, SB_KB), F32),
                        pltpu.VMEM((4 * SB_QB, SB_KB), F32)],
        compiler_params=_cp("parallel", "parallel", "arbitrary"),
    )(proj, proj, proj, dcat, rsum, after)


def _band_bias(slopes_scaled):
    a = lax.broadcasted_iota(jnp.int32, (QBLK, 2 * QBLK), 0)
    bcol = lax.broadcasted_iota(jnp.int32, (QBLK, 2 * QBLK), 1)
    delta = a + QBLK - bcol
    in_band = (delta >= 0) & (delta <= QBLK)
    dist = delta.astype(F32)
    bias = jnp.concatenate([(-sl) * dist for sl in slopes_scaled], axis=0)
    return jnp.concatenate([in_band] * 4, axis=0), jnp.concatenate([bcol >= QBLK] * 4, axis=0), bias


def _stack_heads(x, masks):
    return jnp.concatenate([jnp.where(mk, x, jnp.zeros_like(x)) for mk in masks], axis=0)


def _unstack_heads(x, masks):
    out = jnp.broadcast_to(x[0:QBLK], (QBLK, GRP))
    for h in range(1, 4):
        out = jnp.where(masks[h], x[h * QBLK:(h + 1) * QBLK], out)
    return out


def _head_column(x):
    return jnp.concatenate([x[:, HEAD_DIM * h:HEAD_DIM * h + 1] for h in range(4)], axis=0)


def _attn_specs(banded, nsub, q_lane_blk, k_lane_blk, v_lane_blk):
    tq = nsub * QBLK
    qs = pl.BlockSpec((1, tq, GRP), lambda n, i: (n, i, q_lane_blk))
    if banded:
        ks = [pl.BlockSpec((1, QBLK, GRP), lambda n, i: (n, jnp.maximum(nsub * i - 1, 0), k_lane_blk)),
              pl.BlockSpec((1, tq, GRP), lambda n, i: (n, i, k_lane_blk))]
        vs = [pl.BlockSpec((1, QBLK, GRP), lambda n, i: (n, jnp.maximum(nsub * i - 1, 0), v_lane_blk)),
              pl.BlockSpec((1, tq, GRP), lambda n, i: (n, i, v_lane_blk))]
    else:
        ks = [pl.BlockSpec((1, 2 * QBLK, GRP), lambda n, i: (n, 0, k_lane_blk))]
        vs = [pl.BlockSpec((1, 2 * QBLK, GRP), lambda n, i: (n, 0, v_lane_blk))]
    return qs, ks, vs


def _attn_fwd(q, k, v, *, name, banded, slopes_scaled=None, q_lane_blk=0, k_lane_blk=0, v_lane_blk=0,
              out_dtype=F32):
    n, l, _ = q.shape
    nsub = 2 if l % (2 * QBLK) == 0 else 1
    tq = nsub * QBLK
    nkv = 2 if banded else 1

    def body(*refs):
        q_ref = refs[0]
        k_refs = refs[1:1 + nkv]
        v_refs = refs[1 + nkv:1 + 2 * nkv]
        o_ref, lse_ref = refs[1 + 2 * nkv:]
        step = pl.program_id(1)
        masks = _head_masks4((QBLK, GRP))
        qs = q_ref[0] * jnp.asarray(ATT_SCALE, BF16)
        if banded:
            kall = jnp.concatenate([k_refs[0][0], k_refs[1][0]], axis=0)
            vall = jnp.concatenate([v_refs[0][0], v_refs[1][0]], axis=0)
            in_band, is_cur, bias = _band_bias(slopes_scaled)
        scs, v2s = [], []
        for u in range(nsub):
            k2 = kall[u * QBLK:(u + 2) * QBLK] if banded else k_refs[0][0]
            v2s.append(vall[u * QBLK:(u + 2) * QBLK] if banded else v_refs[0][0])
            scs.append(_dot_nt(_stack_heads(qs[u * QBLK:(u + 1) * QBLK], masks), k2))
        ps, dens, lses = [], [], []
        for u in range(nsub):
            sc = scs[u]
            if banded:
                valid = in_band & (is_cur | (step * nsub + u > 0))
                sc = jnp.where(valid, sc + bias, NEG_BIG)
            m = jnp.max(sc, axis=-1, keepdims=True)
            p = jnp.exp(sc - m)
            den = jnp.sum(p, axis=-1, keepdims=True)
            ps.append(p.astype(BF16))
            dens.append(den)
            lses.append(m + jnp.log(den))
        ohs = [_dot(ps[u], v2s[u]) for u in range(nsub)]
        for u in range(nsub):
            o_ref[0, u * QBLK:(u + 1) * QBLK, :] = _unstack_heads(ohs[u] / dens[u], masks).astype(o_ref.dtype)
            lse_ref[0, u * QBLK:(u + 1) * QBLK, :] = _unstack_heads(lses[u], masks)

    qs, ks, vs = _attn_specs(banded, nsub, q_lane_blk, k_lane_blk, v_lane_blk)
    ob = pl.BlockSpec((1, tq, GRP), lambda nn, i: (nn, i, 0))
    return pl.pallas_call(
        body, name=name, grid=(n, l // tq),
        in_specs=[qs] + ks + vs, out_specs=[ob, ob],
        out_shape=[jax.ShapeDtypeStruct((n, l, GRP), out_dtype), jax.ShapeDtypeStruct((n, l, GRP), F32)],
        compiler_params=_cp("parallel", "arbitrary"),
    )(q, *([k] * nkv), *([v] * nkv))


def _attn_bwd(q, k, v, do, lse, delta, *, name, banded, slopes_scaled=None, q_lane_blk=0, k_lane_blk=0,
              v_lane_blk=0, do_lane_blk=0):
    n, l, _ = q.shape
    nsub = 2 if l % (2 * QBLK) == 0 else 1
    tq = nsub * QBLK
    nkv = 2 if banded else 1
    lk = l if banded else 2 * QBLK

    def body(*refs):
        q_ref = refs[0]
        k_refs = refs[1:1 + nkv]
        v_refs = refs[1 + nkv:1 + 2 * nkv]
        do_ref, lse_ref, dl_ref, dq_ref, dk_ref, dv_ref = refs[1 + 2 * nkv:]
        step = pl.program_id(1)

        @pl.when(step == 0)
        def _():
            dk_ref[...] = jnp.zeros_like(dk_ref)
            dv_ref[...] = jnp.zeros_like(dv_ref)

        masks = _head_masks4((QBLK, GRP))
        qs = q_ref[0] * jnp.asarray(ATT_SCALE, BF16)
        dov = do_ref[0]
        lsev = lse_ref[0]
        dlv = dl_ref[0]
        if banded:
            kall = jnp.concatenate([k_refs[0][0], k_refs[1][0]], axis=0)
            vall = jnp.concatenate([v_refs[0][0], v_refs[1][0]], axis=0)
            in_band, is_cur, bias = _band_bias(slopes_scaled)
        q_st, do_st, k2s, scs, dps = [], [], [], [], []
        for u in range(nsub):
            rows = slice(u * QBLK, (u + 1) * QBLK)
            k2s.append(kall[u * QBLK:(u + 2) * QBLK] if banded else k_refs[0][0])
            v2 = vall[u * QBLK:(u + 2) * QBLK] if banded else v_refs[0][0]
            q_st.append(_stack_heads(qs[rows], masks))
            do_st.append(_stack_heads(dov[rows], masks))
            scs.append(_dot_nt(q_st[u], k2s[u]))
            dps.append(_dot_nt(do_st[u], v2))
        pbs, dss = [], []
        for u in range(nsub):
            rows = slice(u * QBLK, (u + 1) * QBLK)
            sc = scs[u]
            if banded:
                valid = in_band & (is_cur | (step * nsub + u > 0))
                sc = jnp.where(valid, sc + bias, NEG_BIG)
            p = jnp.exp(sc - _head_column(lsev[rows]))
            pbs.append(p.astype(BF16))
            dss.append((p * (dps[u] - _head_column(dlv[rows]))).astype(BF16))
        dqs = [_dot(dss[u], k2s[u]) for u in range(nsub)]
        dk2s = [_dot_tn(dss[u], q_st[u]) for u in range(nsub)]
        dv2s = [_dot_tn(pbs[u], do_st[u]) for u in range(nsub)]
        for u in range(nsub):
            dq_ref[0, u * QBLK:(u + 1) * QBLK, :] = (_unstack_heads(dqs[u], masks) * ATT_SCALE).astype(dq_ref.dtype)
        if banded:
            for u in range(nsub):
                i = step * nsub + u
                cur = pl.multiple_of(i * QBLK, QBLK)
                dk_ref[0, pl.ds(cur, QBLK), :] += dk2s[u][QBLK:]
                dv_ref[0, pl.ds(cur, QBLK), :] += dv2s[u][QBLK:]

                @pl.when(i > 0)
                def _():
                    prev = pl.multiple_of((i - 1) * QBLK, QBLK)
                    dk_ref[0, pl.ds(prev, QBLK), :] += dk2s[u][:QBLK]
                    dv_ref[0, pl.ds(prev, QBLK), :] += dv2s[u][:QBLK]
        else:
            dk_ref[0] += functools.reduce(jnp.add, dk2s)
            dv_ref[0] += functools.reduce(jnp.add, dv2s)

    qs, ks, vs = _attn_specs(banded, nsub, q_lane_blk, k_lane_blk, v_lane_blk)
    ob = pl.BlockSpec((1, tq, GRP), lambda nn, i: (nn, i, 0))
    dos = pl.BlockSpec((1, tq, GRP), lambda nn, i: (nn, i, do_lane_blk))
    kvb = pl.BlockSpec((1, lk, GRP), lambda nn, i: (nn, 0, 0))
    return pl.pallas_call(
        body, name=name, grid=(n, l // tq),
        in_specs=[qs] + ks + vs + [dos, ob, ob], out_specs=[ob, kvb, kvb],
        out_shape=[jax.ShapeDtypeStruct((n, l, GRP), BF16), jax.ShapeDtypeStruct((n, lk, GRP), F32),
                   jax.ShapeDtypeStruct((n, lk, GRP), F32)],
        compiler_params=_cp("parallel", "arbitrary"),
    )(q, *([k] * nkv), *([v] * nkv), do, lse, delta)


def _attn_delta(do, o, *, name, lane_blks):
    t, _ = do.shape
    tr = _pick(t, (512, 256, 128, 8))
    ng = len(lane_blks)

    def body(*refs):
        do_refs, o_refs, d_ref = refs[:ng], refs[ng:2 * ng], refs[2 * ng]
        ra = lax.broadcasted_iota(jnp.int32, (GRP, GRP), 0) // HEAD_DIM
        rb = lax.broadcasted_iota(jnp.int32, (GRP, GRP), 1) // HEAD_DIM
        same_head = (ra == rb).astype(BF16)
        prod = None
        for a_ref, b_ref in zip(do_refs, o_refs):
            term = a_ref[...].astype(F32) * b_ref[...].astype(F32)
            prod = term if prod is None else prod + term
        d_ref[...] = _dot_split(prod, same_head)

    specs = [pl.BlockSpec((tr, GRP), functools.partial(lambda i, lb: (i, lb), lb=lb)) for lb in lane_blks]
    return pl.pallas_call(
        body, name=name, grid=(t // tr,), in_specs=specs + specs,
        out_specs=pl.BlockSpec((tr, GRP), lambda i: (i, 0)),
        out_shape=jax.ShapeDtypeStruct((t, GRP), F32),
        compiler_params=_cp("parallel"),
    )(*([do] * ng), *([o] * ng))


def _dil_combine(os, lses, *, name):
    t, _ = os[0].shape
    tr = _pick(t, (512, 256, 128, 8))
    ng = len(os)

    def body(*refs):
        o_refs, l_refs = refs[:ng], refs[ng:2 * ng]
        out_ref, lse_ref = refs[2 * ng:]
        ls = [r[...] for r in l_refs]
        m = functools.reduce(jnp.maximum, ls)
        tot = None
        for lv in ls:
            e = jnp.exp(lv - m)
            tot = e if tot is None else tot + e
        lse = m + jnp.log(tot)
        for g in range(ng):
            out_ref[:, GRP * g:GRP * (g + 1)] = (o_refs[g][...] * jnp.exp(ls[g] - lse)).astype(out_ref.dtype)
        lse_ref[...] = lse

    sp = pl.BlockSpec((tr, GRP), lambda i: (i, 0))
    return pl.pallas_call(
        body, name=name, grid=(t // tr,), in_specs=[sp] * (2 * ng),
        out_specs=[pl.BlockSpec((tr, GRP * ng), lambda i: (i, 0)), sp],
        out_shape=[jax.ShapeDtypeStruct((t, GRP * ng), BF16), jax.ShapeDtypeStruct((t, GRP), F32)],
        compiler_params=_cp("parallel"),
    )(*os, *lses)


FFN_LB = 256
FFN_ROWS = 64
HALO = 16


def _conv_chunk(u_ref, w, ci):
    r0 = pl.multiple_of(ci * FFN_ROWS, FFN_ROWS)
    cur = u_ref[0, pl.ds(r0, FFN_ROWS), :].astype(F32)
    p0 = pl.multiple_of(jnp.maximum(r0 - HALO, 0), HALO)
    prev = u_ref[0, pl.ds(p0, HALO), :].astype(F32)
    prev = jnp.where(ci > 0, prev, 0.0)
    rowi = lax.broadcasted_iota(jnp.int32, (8, cur.shape[1]), 0)
    r1 = pltpu.roll(cur, 1, 0)
    r2 = pltpu.roll(cur, 2, 0)
    s1 = jnp.concatenate([jnp.where(rowi == 0, prev[HALO - 1:HALO], r1[0:8]), r1[8:]], axis=0)
    s2 = jnp.concatenate([jnp.where(rowi == 0, prev[HALO - 2:HALO - 1],
                                    jnp.where(rowi == 1, prev[HALO - 1:HALO], r2[0:8])), r2[8:]], axis=0)
    c = w[0:1] * s2
    c = c + w[1:2] * s1
    c = c + w[2:3] * cur
    return c, cur, s1, s2


def _ffn_mid_fwd(u, wconv, *, name):
    b, s, f2 = u.shape
    f = f2 // 2
    nlb = f // FFN_LB

    def body(ua_ref, ug_ref, wa_ref, wg_ref, h_ref):
        wa = wa_ref[...]
        wg = wg_ref[...]

        def step(ci, carry):
            ca = _conv_chunk(ua_ref, wa, ci)[0]
            cg = _conv_chunk(ug_ref, wg, ci)[0]
            r0 = pl.multiple_of(ci * FFN_ROWS, FFN_ROWS)
            h_ref[0, pl.ds(r0, FFN_ROWS), :] = (cg * jax.nn.sigmoid(cg) * ca).astype(h_ref.dtype)
            return carry

        lax.fori_loop(0, s // FFN_ROWS, step, 0)

    return pl.pallas_call(
        body, name=name, grid=(nlb, b),
        in_specs=[pl.BlockSpec((1, s, FFN_LB), lambda l, bb: (bb, 0, l)),
                  pl.BlockSpec((1, s, FFN_LB), lambda l, bb: (bb, 0, nlb + l)),
                  pl.BlockSpec((3, FFN_LB), lambda l, bb: (0, l)),
                  pl.BlockSpec((3, FFN_LB), lambda l, bb: (0, nlb + l))],
        out_specs=pl.BlockSpec((1, s, FFN_LB), lambda l, bb: (bb, 0, l)),
        out_shape=jax.ShapeDtypeStruct((b, s, f), BF16),
        compiler_params=_cp("parallel", "parallel"),
    )(u, u, wconv, wconv)


def _ffn_mid_bwd(u, wconv, dh, *, name):
    b, s, f2 = u.shape
    f = f2 // 2
    nlb = f // FFN_LB
    nchunk = s // FFN_ROWS

    def body(ua_ref, ug_ref, wa_ref, wg_ref, dh_ref, dua_ref, dug_ref, dwa_ref, dwg_ref):
        bb = pl.program_id(1)
        wa = wa_ref[...]
        wg = wg_ref[...]
        rowi = lax.broadcasted_iota(jnp.int32, (8, FFN_LB), 0)
        last = FFN_ROWS - 8

        def conv_transpose(dc, nxt, w):
            r1 = pltpu.roll(dc, FFN_ROWS - 1, 0)
            r2 = pltpu.roll(dc, FFN_ROWS - 2, 0)
            n1 = jnp.concatenate([r1[:last], jnp.where(rowi == 7, nxt[0:1], r1[last:])], axis=0)
            n2 = jnp.concatenate([r2[:last], jnp.where(rowi == 6, nxt[0:1],
                                                       jnp.where(rowi == 7, nxt[1:2], r2[last:]))], axis=0)
            return w[2:3] * dc + w[1:2] * n1 + w[0:1] * n2

        def step(t, carry):
            ci = nchunk - 1 - t
            nxt_a, nxt_g = carry[0], carry[1]
            r0 = pl.multiple_of(ci * FFN_ROWS, FFN_ROWS)
            ca, cura, s1a, s2a = _conv_chunk(ua_ref, wa, ci)
            cg, curg, s1g, s2g = _conv_chunk(ug_ref, wg, ci)
            dhv = dh_ref[0, pl.ds(r0, FFN_ROWS), :].astype(F32)
            sg = jax.nn.sigmoid(cg)
            da = dhv * (cg * sg)
            dg = dhv * ca * (sg * (1.0 + cg * (1.0 - sg)))
            dua_ref[0, pl.ds(r0, FFN_ROWS), :] = conv_transpose(da, nxt_a, wa).astype(dua_ref.dtype)
            dug_ref[0, pl.ds(r0, FFN_ROWS), :] = conv_transpose(dg, nxt_g, wg).astype(dug_ref.dtype)
            red = lambda x: jnp.sum(x, axis=0, keepdims=True)
            parts = (red(da * s2a), red(da * s1a), red(da * cura), red(dg * s2g), red(dg * s1g), red(dg * curg))
            return (da[0:8], dg[0:8]) + tuple(c + p for c, p in zip(carry[2:], parts))

        zero = jnp.zeros((1, FFN_LB), F32)
        zero8 = jnp.zeros((8, FFN_LB), F32)
        taps = lax.fori_loop(0, nchunk, step, (zero8, zero8) + (zero,) * 6)[2:]

        @pl.when(bb == 0)
        def _():
            for k in range(3):
                dwa_ref[k:k + 1, :] = taps[k]
                dwg_ref[k:k + 1, :] = taps[3 + k]

        @pl.when(bb > 0)
        def _():
            for k in range(3):
                dwa_ref[k:k + 1, :] += taps[k]
                dwg_ref[k:k + 1, :] += taps[3 + k]

    seq_a = pl.BlockSpec((1, s, FFN_LB), lambda l, bb: (bb, 0, l))
    seq_g = pl.BlockSpec((1, s, FFN_LB), lambda l, bb: (bb, 0, nlb + l))
    wsp = pl.BlockSpec((3, FFN_LB), lambda l, bb: (0, l))
    return pl.pallas_call(
        body, name=name, grid=(nlb, b),
        in_specs=[seq_a, seq_g, wsp, pl.BlockSpec((3, FFN_LB), lambda l, bb: (0, nlb + l)), seq_a],
        out_specs=[seq_a, seq_a, wsp, wsp],
        out_shape=[jax.ShapeDtypeStruct((b, s, f), BF16), jax.ShapeDtypeStruct((b, s, f), BF16),
                   jax.ShapeDtypeStruct((3, f), F32), jax.ShapeDtypeStruct((3, f), F32)],
        compiler_params=_cp("parallel", "arbitrary"),
    )(u, u, wconv, wconv, dh)


def _adam_math(w, g, m, v):
    m2 = ADAM_B1 * m + (1.0 - ADAM_B1) * g
    v2 = ADAM_B2 * v + (1.0 - ADAM_B2) * (g * g)
    m_hat = m2 / (1.0 - ADAM_B1 ** ADAM_STEP)
    v_hat = v2 / (1.0 - ADAM_B2 ** ADAM_STEP)
    delta = -ADAM_LR * (m_hat / (jnp.sqrt(v_hat) + ADAM_EPS) + ADAM_WD * w)
    return delta, m2, v2


def _adam(w, g, m, v, *, name):
    r, c = w.shape
    tr = _pick(r, (256, 128, 88, 64, 32, 16, 8))

    def body(w_ref, g_ref, m_ref, v_ref, d_ref, m2_ref, v2_ref):
        d, m2, v2 = _adam_math(w_ref[...], g_ref[...], m_ref[...], v_ref[...])
        d_ref[...] = d
        m2_ref[...] = m2
        v2_ref[...] = v2

    sp = pl.BlockSpec((tr, c), lambda i: (i, 0))
    return pl.pallas_call(
        body, name=name, grid=(r // tr,), in_specs=[sp] * 4, out_specs=[sp] * 3,
        out_shape=[jax.ShapeDtypeStruct((r, c), F32)] * 3,
        compiler_params=_cp("parallel"),
    )(w, g, m, v)


def _adam_small(quads, *, name):
    nq = len(quads)

    def body(*refs):
        ins, outs = refs[:4 * nq], refs[4 * nq:]
        for k in range(nq):
            w_ref, g_ref, m_ref, v_ref = ins[4 * k:4 * k + 4]
            d, m2, v2 = _adam_math(w_ref[...], g_ref[...], m_ref[...], v_ref[...])
            outs[3 * k][...] = d
            outs[3 * k + 1][...] = m2
            outs[3 * k + 2][...] = v2

    flat = [a for q in quads for a in q]
    out_shape = [jax.ShapeDtypeStruct(q[0].shape, F32) for q in quads for _ in range(3)]
    vm = pl.BlockSpec(memory_space=pltpu.VMEM)
    outs = pl.pallas_call(
        body, name=name, in_specs=[vm] * len(flat), out_specs=[vm] * len(out_shape), out_shape=out_shape,
        compiler_params=pltpu.CompilerParams(vmem_limit_bytes=VMEM_LIMIT_BYTES),
    )(*flat)
    return [tuple(outs[3 * k:3 * k + 3]) for k in range(nq)]


def _mesh_pos():
    return lax.axis_index("x"), lax.axis_index("y"), lax.axis_index("c")


def _flip(v, bit):
    return 1 - v if bit else v


def _all_gather_hbm(xl, *, name):
    r, c = xl.shape

    def body(x_ref, out_ref, send_sems, recv_sems, local_sem):
        x, y, cc = _mesh_pos()
        me, sibling = (x, y, cc), (x, y, 1 - cc)
        chips = [(1 - x, y), (x, 1 - y), (1 - x, 1 - y)]

        def rows(px, py, pc):
            return out_ref.at[pl.ds((4 * px + 2 * py + pc) * r, r), :]

        def copy(k, block, to, src=None):
            return pltpu.make_async_remote_copy(
                src_ref=rows(*block) if src is None else src, dst_ref=rows(*block),
                send_sem=send_sems.at[k], recv_sem=recv_sems.at[k], device_id=to, device_id_type=MESH_ID)

        mine = pltpu.make_async_copy(x_ref, rows(*me), local_sem)
        mine.start()
        first = [copy(0, me, sibling, src=x_ref)]
        first += [copy(1 + j, me, (*chip, cc), src=x_ref) for j, chip in enumerate(chips)]
        for cp in first:
            cp.start()
        passed = [copy(4 + j, (*chip, cc), sibling) for j, chip in enumerate(chips)]
        for j, chip in enumerate(chips):
            copy(1 + j, (*chip, cc), me).wait_recv()
            passed[j].start()
        copy(0, sibling, me).wait_recv()
        for j, chip in enumerate(chips):
            copy(4 + j, (*chip, 1 - cc), me).wait_recv()
        for cp in first + passed:
            cp.wait_send()
        mine.wait()

    hbm = pl.BlockSpec(memory_space=pltpu.HBM)
    return pl.pallas_call(
        body, name=name, in_specs=[hbm], out_specs=hbm,
        out_shape=jax.ShapeDtypeStruct((N_DEV * r, c), xl.dtype),
        scratch_shapes=[pltpu.SemaphoreType.DMA((7,)), pltpu.SemaphoreType.DMA((7,)), pltpu.SemaphoreType.DMA],
    )(xl)


def _all_reduce_small(xl, *, name):
    r, c = xl.shape

    def body(x_ref, sum_ref, all_ref, send_sems, recv_sems, local_sem):
        x, y, cc = _mesh_pos()
        me, sibling = (x, y, cc), (x, y, 1 - cc)
        chips = [(1 - x, y), (x, 1 - y), (1 - x, 1 - y)]

        def rows(px, py, pc):
            return all_ref.at[pl.ds((4 * px + 2 * py + pc) * r, r), :]

        def copy(k, block, to, src=None):
            return pltpu.make_async_remote_copy(
                src_ref=rows(*block) if src is None else src, dst_ref=rows(*block),
                send_sem=send_sems.at[k], recv_sem=recv_sems.at[k], device_id=to, device_id_type=MESH_ID)

        mine = pltpu.make_async_copy(x_ref, rows(*me), local_sem)
        mine.start()
        first = [copy(0, me, sibling, src=x_ref)]
        first += [copy(1 + j, me, (*chip, cc), src=x_ref) for j, chip in enumerate(chips)]
        for cp in first:
            cp.start()
        passed = [copy(4 + j, (*chip, cc), sibling) for j, chip in enumerate(chips)]
        for j, chip in enumerate(chips):
            copy(1 + j, (*chip, cc), me).wait_recv()
            passed[j].start()
        copy(0, sibling, me).wait_recv()
        for j, chip in enumerate(chips):
            copy(4 + j, (*chip, 1 - cc), me).wait_recv()
        for cp in first + passed:
            cp.wait_send()
        mine.wait()
        tot = all_ref[pl.ds(0, r), :]
        for dd in range(1, N_DEV):
            tot = tot + all_ref[pl.ds(dd * r, r), :]
        sum_ref[...] = tot

    vm = pl.BlockSpec(memory_space=pltpu.VMEM)
    return pl.pallas_call(
        body, name=name, in_specs=[vm], out_specs=[vm, vm],
        out_shape=[jax.ShapeDtypeStruct((r, c), F32), jax.ShapeDtypeStruct((N_DEV * r, c), F32)],
        scratch_shapes=[pltpu.SemaphoreType.DMA((7,)), pltpu.SemaphoreType.DMA((7,)), pltpu.SemaphoreType.DMA],
    )(xl)[0]


N_PEERS = N_DEV - 1
_HBM = pl.BlockSpec(memory_space=pltpu.HBM)
_SEM = pl.BlockSpec(memory_space=pltpu.SEMAPHORE)


def _peer_list(x, y, cc):
    return [(_flip(x, rel & 4), _flip(y, rel & 2), _flip(cc, rel & 1)) for rel in range(1, N_DEV)]


def _dev_index(p):
    return 4 * p[0] + 2 * p[1] + p[2]


def _split_copy(src_ref, land_ref, sems, k, peer, me, gather, landing_of):
    if gather:
        r = src_ref.shape[0]
        src = src_ref
        dst = land_ref.at[pl.ds(_dev_index(landing_of) * r, r), :]
    else:
        src = src_ref.at[_dev_index(peer)]
        dst = land_ref.at[_dev_index(landing_of)]
    return pltpu.make_async_remote_copy(src_ref=src, dst_ref=dst, send_sem=sems[k], recv_sem=sems[N_PEERS + k],
                                        device_id=peer, device_id_type=MESH_ID)


def _exchange_start(src, land_shape, *, name, gather):
    def body(src_ref, land_ref, *rest):
        sems = rest[:2 * N_PEERS]
        token = rest[2 * N_PEERS + 2]
        x, y, cc = _mesh_pos()
        me = (x, y, cc)
        for k, peer in enumerate(_peer_list(x, y, cc)):
            _split_copy(src_ref, land_ref, sems, k, peer, me, gather, landing_of=me).start()
        token[...] = jnp.zeros_like(token)

    outs = pl.pallas_call(
        body, name=name,
        out_shape=tuple([pltpu.SemaphoreType.DMA(())] * (2 * N_PEERS)) + (
            pltpu.HBM(src.shape, src.dtype), pltpu.HBM(land_shape, src.dtype),
            jax.ShapeDtypeStruct((8, LANE), F32)),
        in_specs=(_HBM, _HBM),
        out_specs=tuple([_SEM] * (2 * N_PEERS)) + (_HBM, _HBM, pl.BlockSpec(memory_space=pltpu.VMEM)),
        input_output_aliases={0: 2 * N_PEERS, 1: 2 * N_PEERS + 1},
        compiler_params=pltpu.CompilerParams(has_side_effects=pltpu.SideEffectType.DATAFLOW_SIDE_EFFECTING),
    )(pltpu.with_memory_space_constraint(src, pltpu.HBM),
      pltpu.with_memory_space_constraint(lax.empty(land_shape, src.dtype), pltpu.HBM))
    return outs[:2 * N_PEERS], outs[2 * N_PEERS], outs[2 * N_PEERS + 1], outs[2 * N_PEERS + 2]


def _exchange_wait(sems, src_thru, land_thru, after, *, name, gather):
    def body(src_ref, land_ref, *rest):
        sem_refs = rest[:2 * N_PEERS]
        x, y, cc = _mesh_pos()
        me = (x, y, cc)
        for k, peer in enumerate(_peer_list(x, y, cc)):
            cp = _split_copy(src_ref, land_ref, sem_refs, k, peer, me, gather, landing_of=peer)
            cp.wait_send()
            cp.wait_recv()

    outs = pl.pallas_call(
        body, name=name,
        out_shape=(pltpu.HBM(src_thru.shape, src_thru.dtype), pltpu.HBM(land_thru.shape, land_thru.dtype)),
        in_specs=(_HBM, _HBM) + tuple([_SEM] * (2 * N_PEERS)) + (pl.BlockSpec(memory_space=pl.ANY),),
        out_specs=(_HBM, _HBM), input_output_aliases={0: 0, 1: 1},
        compiler_params=pltpu.CompilerParams(has_side_effects=pltpu.SideEffectType.DATAFLOW_SIDE_EFFECTING),
    )(src_thru, land_thru, *sems, after)
    return outs[1]


def _sum_blocks(recv, own, *, name):
    nd, r, c = recv.shape
    tr = _pick(r, (448, 256, 128, 64, 32, 16))

    def body(x_ref, own_ref, o_ref):
        x, y, cc = _mesh_pos()
        me = 4 * x + 2 * y + cc
        tot = None
        for dd in range(nd):
            term = jnp.where(me == dd, own_ref[0], x_ref[dd]).astype(F32)
            tot = term if tot is None else tot + term
        o_ref[...] = tot

    return pl.pallas_call(
        body, name=name, grid=(r // tr,),
        in_specs=[pl.BlockSpec((nd, tr, c), lambda i: (0, i, 0)), pl.BlockSpec((1, tr, c), lambda i: (0, i, 0))],
        out_specs=pl.BlockSpec((tr, c), lambda i: (i, 0)),
        out_shape=jax.ShapeDtypeStruct((r, c), F32),
        compiler_params=_cp("parallel"),
    )(recv, own)


SHARD_KIND = {"a_w_in": "col", "a_w_out": "row", "a_w_mem_kv": "row", "a_ffn_up": "col", "a_ffn_down": "row",
              "w_kv_shared": "col", "b_w_in": "row", "b_w_out": "row", "b_w_mem_kv": "row", "b_ffn_up": "col",
              "b_ffn_down": "row"}
EARLY_WEIGHTS = ("a_w_in", "a_w_mem_kv")
LATE_WEIGHTS = tuple(nm for nm in SHARD_KIND if nm not in EARLY_WEIGHTS)


def _as2d(a):
    return a.reshape(a.shape[-2], a.shape[-1]) if a.ndim >= 2 else a.reshape(1, a.shape[0])


def _pack_local(shards):
    return jnp.concatenate([_as2d(s).astype(BF16).reshape(-1, PACK_COLS) for s in shards], axis=0)


def _unpack_full(gathered, names, shapes):
    out = {}
    r0 = 0
    for name in names:
        rows, cols = shapes[name]
        nr = rows * cols // PACK_COLS
        blk = gathered[:, r0:r0 + nr, :].reshape(N_DEV, rows, cols)
        if SHARD_KIND[name] == "row":
            out[name] = blk.reshape(N_DEV * rows, cols)
        else:
            out[name] = blk.transpose(1, 0, 2).reshape(rows, N_DEV * cols)
        r0 += nr
    return out


def _pack_grads(grads, names, shapes):
    parts = []
    for name in names:
        rows, cols = shapes[name]
        g = grads[name]
        if SHARD_KIND[name] == "row":
            blk = g.reshape(N_DEV, rows, cols)
        else:
            blk = g.reshape(rows, N_DEV, cols).transpose(1, 0, 2)
        parts.append(blk.astype(BF16).reshape(N_DEV, rows * cols // PACK_COLS, PACK_COLS))
    return jnp.concatenate(parts, axis=1)


def _unpack_local(gsum, names, shapes):
    out = {}
    r0 = 0
    for name in names:
        rows, cols = shapes[name]
        nr = rows * cols // PACK_COLS
        out[name] = gsum[r0:r0 + nr].reshape(rows, cols)
        r0 += nr
    return out


def _by_residue(t, d):
    if d == 1:
        return t
    b, s, c = t.shape
    return t.reshape(b, s // d, d, c).transpose(0, 2, 1, 3).reshape(b * d, s // d, c)


def _from_residue(t, d, b):
    if d == 1:
        return t
    n, l, c = t.shape
    return t.reshape(b, d, l, c).transpose(0, 2, 1, 3).reshape(b, l * d, c)


def _alibi_slopes():
    return [2.0 ** (-ALIBI_MAX_BIAS * (i + 1) / N_DIL_HEADS) for i in range(N_DIL_HEADS)]


def _conv_ffn_fwd(xin, gain, w_up, wconv, w_down, tag, b, s):
    (n,), r = _rms_fwd(xin, [gain], name=f"{tag}_rms_ffn")
    u = _mm(n, w_up, name=f"{tag}_up", out_dtype=BF16).reshape(b, s, -1)
    hmid = _ffn_mid_fwd(u, wconv, name=f"{tag}_ffn_mid").reshape(b * s, -1)
    xout = _mm(hmid, w_down, name=f"{tag}_down", out_dtype=F32, res=xin)
    return xout, (n, r, u, hmid)


def _conv_ffn_bwd(dxout, xin, gain, saved, w_up, wconv, w_down, tag, b, s):
    n, r, u, hmid = saved
    f = hmid.shape[1]
    dhmid = _mm(dxout, w_down, name=f"{tag}_d_hmid", out_dtype=BF16, trans_b=True)
    g_down = _mm(hmid, dxout, name=f"{tag}_g_down", out_dtype=BF16, trans_a=True)
    du_a, du_g, gc_a, gc_g = _ffn_mid_bwd(u, wconv, dhmid.reshape(b, s, f), name=f"{tag}_ffn_mid_bwd")
    du_a = du_a.reshape(b * s, f)
    du_g = du_g.reshape(b * s, f)
    dn = _mm(du_a, w_up[:, :f], name=f"{tag}_d_n_a", out_dtype=F32, trans_b=True)
    dn = _mm(du_g, w_up[:, f:], name=f"{tag}_d_n_g", out_dtype=F32, res=dn, trans_b=True)
    g_up = jnp.concatenate([_mm(n, du_a, name=f"{tag}_g_up_a", out_dtype=BF16, trans_a=True),
                            _mm(n, du_g, name=f"{tag}_g_up_g", out_dtype=BF16, trans_a=True)], axis=1)
    dxin, (g_gain,) = _rms_bwd(xin, r, [(dn, gain)], dxout, name=f"{tag}_rms_ffn_bwd")
    return dxin, g_up, g_down, jnp.concatenate([gc_a, gc_g], axis=1), g_gain


def _mem_kv_fwd(mem2d, gain, w_mem_kv, tag, b):
    (nm,), rm = _rms_fwd(mem2d, [gain], name=f"{tag}_rms_mem")
    kvm = _mm(nm, w_mem_kv, name=f"{tag}_mem_kv", out_dtype=BF16)
    return kvm.reshape(b, -1, 2 * MEM_WIDTH), (nm, rm)


def _mem_kv_bwd(dk, dv, mem2d, gain, saved, w_mem_kv, tag):
    nm, rm = saved
    dkvm = jnp.concatenate([dk, dv], axis=-1).reshape(-1, 2 * MEM_WIDTH)
    dnm = _mm(dkvm, w_mem_kv, name=f"{tag}_d_nm", out_dtype=F32, trans_b=True)
    g_w = _mm(nm, dkvm, name=f"{tag}_g_mem_kv", out_dtype=BF16, trans_a=True)
    _, (g_gain,) = _rms_bwd(mem2d, rm, [(dnm, gain)], None, name=f"{tag}_rms_mem_bwd", need_dx=False)
    return g_w, g_gain


def kernel(x, mem, a_norm_attn, a_w_in, a_w_out, a_norm_mem, a_w_mem_kv, a_norm_ffn, a_ffn_up, a_ffn_conv, a_ffn_down, kv_norm, w_kv_shared, b_norm_attn, b_w_in, b_w_out, b_norm_mem, b_w_mem_kv, b_norm_ffn, b_ffn_up, b_ffn_conv, b_ffn_down, final_norm, loss_target, m_a_norm_attn, m_a_w_in, m_a_w_out, m_a_norm_mem, m_a_w_mem_kv, m_a_norm_ffn, m_a_ffn_up, m_a_ffn_conv, m_a_ffn_down, m_kv_norm, m_w_kv_shared, m_b_norm_attn, m_b_w_in, m_b_w_out, m_b_norm_mem, m_b_w_mem_kv, m_b_norm_ffn, m_b_ffn_up, m_b_ffn_conv, m_b_ffn_down, m_final_norm, v_a_norm_attn, v_a_w_in, v_a_w_out, v_a_norm_mem, v_a_w_mem_kv, v_a_norm_ffn, v_a_ffn_up, v_a_ffn_conv, v_a_ffn_down, v_kv_norm, v_w_kv_shared, v_b_norm_attn, v_b_w_in, v_b_w_out, v_b_norm_mem, v_b_w_mem_kv, v_b_norm_ffn, v_b_ffn_up, v_b_ffn_conv, v_b_ffn_down, v_final_norm):
    names = ["a_norm_attn", "a_w_in", "a_w_out", "a_norm_mem", "a_w_mem_kv", "a_norm_ffn", "a_ffn_up",
             "a_ffn_conv", "a_ffn_down", "kv_norm", "w_kv_shared", "b_norm_attn", "b_w_in", "b_w_out",
             "b_norm_mem", "b_w_mem_kv", "b_norm_ffn", "b_ffn_up", "b_ffn_conv", "b_ffn_down", "final_norm"]
    wl = dict(zip(names, [a_norm_attn, a_w_in, a_w_out, a_norm_mem, a_w_mem_kv, a_norm_ffn, a_ffn_up,
                          a_ffn_conv, a_ffn_down, kv_norm, w_kv_shared, b_norm_attn, b_w_in, b_w_out,
                          b_norm_mem, b_w_mem_kv, b_norm_ffn, b_ffn_up, b_ffn_conv, b_ffn_down, final_norm]))
    ml = dict(zip(names, [m_a_norm_attn, m_a_w_in, m_a_w_out, m_a_norm_mem, m_a_w_mem_kv, m_a_norm_ffn,
                          m_a_ffn_up, m_a_ffn_conv, m_a_ffn_down, m_kv_norm, m_w_kv_shared, m_b_norm_attn,
                          m_b_w_in, m_b_w_out, m_b_norm_mem, m_b_w_mem_kv, m_b_norm_ffn, m_b_ffn_up,
                          m_b_ffn_conv, m_b_ffn_down, m_final_norm]))
    vl = dict(zip(names, [v_a_norm_attn, v_a_w_in, v_a_w_out, v_a_norm_mem, v_a_w_mem_kv, v_a_norm_ffn,
                          v_a_ffn_up, v_a_ffn_conv, v_a_ffn_down, v_kv_norm, v_w_kv_shared, v_b_norm_attn,
                          v_b_w_in, v_b_w_out, v_b_norm_mem, v_b_w_mem_kv, v_b_norm_ffn, v_b_ffn_up,
                          v_b_ffn_conv, v_b_ffn_down, v_final_norm]))
    b, s, d = x.shape
    t = b * s
    my_x, my_y, my_c = _mesh_pos()
    me = 4 * my_x + 2 * my_y + my_c

    shapes = {nm: _as2d(wl[nm]).shape for nm in SHARD_KIND}
    early_local = _pack_local([wl[nm] for nm in EARLY_WEIGHTS])
    early_all = _all_gather_hbm(early_local, name="gather_early").reshape(N_DEV, early_local.shape[0], PACK_COLS)
    wf = _unpack_full(early_all, EARLY_WEIGHTS, shapes)
    late_local = _pack_local([wl[nm] for nm in LATE_WEIGHTS])
    late_rows = late_local.shape[0]
    gat_sems, gat_src, gat_land, gat_token = _exchange_start(
        late_local, (N_DEV * late_rows, PACK_COLS), name="gather_late_start", gather=True)

    sharded_small = ["a_norm_attn", "a_norm_mem", "a_norm_ffn", "a_ffn_conv", "b_ffn_conv"]
    small_flat = jnp.concatenate([wl[nm].reshape(-1) for nm in sharded_small])
    n_small = small_flat.shape[0]
    small_rows = -(-n_small // (8 * LANE)) * 8
    small_local = jnp.pad(small_flat, (0, small_rows * LANE - n_small)).reshape(small_rows, LANE)
    small_all = _all_gather_hbm(small_local, name="gather_small").reshape(N_DEV, small_rows * LANE)
    sfull = {}
    r0 = 0
    for nm in sharded_small:
        rows, cols = _as2d(wl[nm]).shape
        blk = small_all[:, r0:r0 + rows * cols].reshape(N_DEV, rows, cols)
        sfull[nm] = blk.transpose(1, 0, 2).reshape(rows, N_DEV * cols)
        r0 += rows * cols
    gain = {nm: sfull[nm] for nm in ("a_norm_attn", "a_norm_mem", "a_norm_ffn")}
    for nm in ("kv_norm", "b_norm_attn", "b_norm_mem", "b_norm_ffn", "final_norm"):
        gain[nm] = _as2d(wl[nm])
    conv_a, conv_b = sfull["a_ffn_conv"], sfull["b_ffn_conv"]

    x2d = x.reshape(t, d)
    mem2d = mem.reshape(-1, d)
    tgt2d = loss_target.reshape(t, d)
    qmem_blk_a = 3 * SB_WIDTH // GRP
    qmem_blk_b = DIL_WIDTH // GRP

    (n1,), r1 = _rms_fwd(x2d, [gain["a_norm_attn"]], name="a_rms_attn")
    proj_a = _mm(n1, wf["a_w_in"], name="a_in", out_dtype=BF16).reshape(b, s, -1)
    kvm_a, mem_saved_a = _mem_kv_fwd(mem2d, gain["a_norm_mem"], wf["a_w_mem_kv"], "a", b)
    o_sb, rsum = _sb_fwd(proj_a, gat_token, name="a_sb_fwd")
    o_mem_a, lse_mem_a = _attn_fwd(proj_a, kvm_a, kvm_a, name="a_mem_fwd", banded=False,
                                   q_lane_blk=qmem_blk_a, k_lane_blk=0, v_lane_blk=1, out_dtype=BF16)
    late_land = _exchange_wait(gat_sems, gat_src, gat_land, rsum, name="gather_late_wait", gather=True)
    late_all = lax.dynamic_update_slice(late_land, late_local, (me * late_rows, 0))
    wf.update(_unpack_full(late_all.reshape(N_DEV, late_rows, PACK_COLS), LATE_WEIGHTS, shapes))
    cat_a = jnp.concatenate([o_sb, o_mem_a], axis=-1).reshape(t, d)
    x1 = _mm(cat_a, wf["a_w_out"], name="a_out", out_dtype=F32, res=x2d)
    xa, ffn_saved_a = _conv_ffn_fwd(x1, gain["a_norm_ffn"], wf["a_ffn_up"], conv_a, wf["a_ffn_down"], "a", b, s)

    (nk, n3), r3 = _rms_fwd(xa, [gain["kv_norm"], gain["b_norm_attn"]], name="b_rms_attn")
    kvsh = _mm(nk, wf["w_kv_shared"], name="kv_shared", out_dtype=BF16).reshape(b, s, -1)
    proj_b = _mm(n3, wf["b_w_in"], name="b_in", out_dtype=BF16).reshape(b, s, -1)
    kvm_b, mem_saved_b = _mem_kv_fwd(mem2d, gain["b_norm_mem"], wf["b_w_mem_kv"], "b", b)
    slopes = _alibi_slopes()
    dil_q, dil_k, dil_v, dil_o, dil_lse, dil_slopes = [], [], [], [], [], []
    for g, (_, dil) in enumerate(DIL_GROUPS):
        qg = _by_residue(proj_b[:, :, GRP * g:GRP * (g + 1)], dil)
        kg = _by_residue(kvsh[:, :, GRP * g:GRP * (g + 1)], dil)
        vg = _by_residue(kvsh[:, :, DIL_WIDTH + GRP * g:DIL_WIDTH + GRP * (g + 1)], dil)
        sl = [slopes[4 * g + h] * dil for h in range(4)]
        og, lg = _attn_fwd(qg, kg, vg, name=f"b_dil{g}_fwd", banded=True, slopes_scaled=sl)
        dil_q.append(qg)
        dil_k.append(kg)
        dil_v.append(vg)
        dil_slopes.append(sl)
        dil_o.append(_from_residue(og, dil, b).reshape(t, GRP))
        dil_lse.append(_from_residue(lg, dil, b).reshape(t, GRP))
    o_dil, lse_joint = _dil_combine(dil_o, dil_lse, name="b_dil_combine")
    o_mem_b, lse_mem_b = _attn_fwd(proj_b, kvm_b, kvm_b, name="b_mem_fwd", banded=False,
                                   q_lane_blk=qmem_blk_b, k_lane_blk=0, v_lane_blk=1, out_dtype=BF16)
    cat_b = jnp.concatenate([o_dil, o_mem_b.reshape(t, MEM_WIDTH)], axis=-1)
    x3 = _mm(cat_b, wf["b_w_out"], name="b_out", out_dtype=F32, res=xa)
    xb, ffn_saved_b = _conv_ffn_fwd(x3, gain["b_norm_ffn"], wf["b_ffn_up"], conv_b, wf["b_ffn_down"], "b", b, s)

    dxb, g_final, loss_vec = _loss_head(xb, gain["final_norm"], tgt2d, name="loss_head")

    grads = {}
    sgrads = {"final_norm": g_final}
    dx3, grads["b_ffn_up"], grads["b_ffn_down"], sgrads["b_ffn_conv"], sgrads["b_norm_ffn"] = _conv_ffn_bwd(
        dxb, x3, gain["b_norm_ffn"], ffn_saved_b, wf["b_ffn_up"], conv_b, wf["b_ffn_down"], "b", b, s)
    dcat_b = _mm(dx3, wf["b_w_out"], name="b_d_cat", out_dtype=BF16, trans_b=True)
    grads["b_w_out"] = _mm(cat_b, dx3, name="b_g_out", out_dtype=BF16, trans_a=True)
    dcat_b3 = dcat_b.reshape(b, s, d)
    delta_mem_b = _attn_delta(dcat_b, cat_b, name="b_mem_delta", lane_blks=[qmem_blk_b]).reshape(b, s, GRP)
    dq_mem_b, dkm_b, dvm_b = _attn_bwd(proj_b, kvm_b, kvm_b, dcat_b3, lse_mem_b, delta_mem_b, name="b_mem_bwd",
                                       banded=False, q_lane_blk=qmem_blk_b, k_lane_blk=0, v_lane_blk=1,
                                       do_lane_blk=qmem_blk_b)
    delta_dil = _attn_delta(dcat_b, cat_b, name="b_dil_delta", lane_blks=[0, 1, 2]).reshape(b, s, GRP)
    lse_joint3 = lse_joint.reshape(b, s, GRP)
    dq_parts, dk_parts, dv_parts = [], [], []
    for g, (_, dil) in enumerate(DIL_GROUPS):
        dog = _by_residue(dcat_b3[:, :, GRP * g:GRP * (g + 1)], dil)
        lg = _by_residue(lse_joint3, dil)
        dg = _by_residue(delta_dil, dil)
        dqg, dkg, dvg = _attn_bwd(dil_q[g], dil_k[g], dil_v[g], dog, lg, dg, name=f"b_dil{g}_bwd", banded=True,
                                  slopes_scaled=dil_slopes[g])
        dq_parts.append(_from_residue(dqg, dil, b))
        dk_parts.append(_from_residue(dkg, dil, b))
        dv_parts.append(_from_residue(dvg, dil, b))
    dproj_b = jnp.concatenate(dq_parts + [dq_mem_b], axis=-1).reshape(t, d)
    dn3 = _mm(dproj_b, wf["b_w_in"], name="b_d_n", out_dtype=F32, trans_b=True)
    grads["b_w_in"] = _mm(n3, dproj_b, name="b_g_in", out_dtype=BF16, trans_a=True)
    grads["b_w_mem_kv"], sgrads["b_norm_mem"] = _mem_kv_bwd(dkm_b, dvm_b, mem2d, gain["b_norm_mem"], mem_saved_b,
                                                           wf["b_w_mem_kv"], "b")
    dkvsh = jnp.concatenate(dk_parts + dv_parts, axis=-1).reshape(t, 2 * DIL_WIDTH).astype(BF16)
    dnk = _mm(dkvsh, wf["w_kv_shared"], name="kv_d_n", out_dtype=F32, trans_b=True)
    grads["w_kv_shared"] = _mm(nk, dkvsh, name="kv_g", out_dtype=BF16, trans_a=True)
    dxa, (sgrads["kv_norm"], sgrads["b_norm_attn"]) = _rms_bwd(
        xa, r3, [(dnk, gain["kv_norm"]), (dn3, gain["b_norm_attn"])], dx3, name="b_rms_attn_bwd")

    dx1, grads["a_ffn_up"], grads["a_ffn_down"], sgrads["a_ffn_conv"], sgrads["a_norm_ffn"] = _conv_ffn_bwd(
        dxa, x1, gain["a_norm_ffn"], ffn_saved_a, wf["a_ffn_up"], conv_a, wf["a_ffn_down"], "a", b, s)
    dcat_a = _mm(dx1, wf["a_w_out"], name="a_d_cat", out_dtype=BF16, trans_b=True)
    grads["a_w_out"] = _mm(cat_a, dx1, name="a_g_out", out_dtype=BF16, trans_a=True)
    dcat_a3 = dcat_a.reshape(b, s, d)
    delta_mem_a = _attn_delta(dcat_a, cat_a, name="a_mem_delta", lane_blks=[qmem_blk_b]).reshape(b, s, GRP)
    dq_mem_a, dkm_a, dvm_a = _attn_bwd(proj_a, kvm_a, kvm_a, dcat_a3, lse_mem_a, delta_mem_a, name="a_mem_bwd",
                                       banded=False, q_lane_blk=qmem_blk_a, k_lane_blk=0, v_lane_blk=1,
                                       do_lane_blk=qmem_blk_b)
    late_grads = _pack_grads(grads, LATE_WEIGHTS, shapes)
    ex_sems, ex_src, ex_land, ex_token = _exchange_start(late_grads, late_grads.shape, name="grads_late_start",
                                                         gather=False)
    dq_sb, dk_sb, dv_sb = _sb_bwd(proj_a, dcat_a3, rsum, ex_token, name="a_sb_bwd")
    dproj_a = jnp.concatenate([dq_sb, dk_sb.astype(BF16), dv_sb.astype(BF16), dq_mem_a], axis=-1).reshape(t, -1)
    dn1 = _mm(dproj_a, wf["a_w_in"], name="a_d_n", out_dtype=F32, trans_b=True)
    grads["a_w_in"] = _mm(n1, dproj_a, name="a_g_in", out_dtype=BF16, trans_a=True)
    grads["a_w_mem_kv"], sgrads["a_norm_mem"] = _mem_kv_bwd(dkm_a, dvm_a, mem2d, gain["a_norm_mem"], mem_saved_a,
                                                           wf["a_w_mem_kv"], "a")
    early_grads = _pack_grads(grads, EARLY_WEIGHTS, shapes)
    ee_sems, ee_src, ee_land, _ = _exchange_start(early_grads, early_grads.shape, name="grads_early_start",
                                                  gather=False)
    dx0, (sgrads["a_norm_attn"],) = _rms_bwd(x2d, r1, [(dn1, gain["a_norm_attn"])], dx1, name="a_rms_attn_bwd")
    grad_x = dx0.reshape(b, s, d)

    late_recv = _exchange_wait(ex_sems, ex_src, ex_land, dx0, name="grads_late_wait", gather=False)
    own_late = lax.dynamic_slice(late_grads, (me, 0, 0), (1,) + late_grads.shape[1:])
    gl = _unpack_local(_sum_blocks(late_recv, own_late, name="sum_grads_late"), LATE_WEIGHTS, shapes)

    small_names = ["a_norm_attn", "a_norm_mem", "a_norm_ffn", "kv_norm", "b_norm_attn", "b_norm_mem",
                   "b_norm_ffn", "final_norm", "a_ffn_conv", "b_ffn_conv"]
    small_flat = jnp.concatenate([sgrads[nm].reshape(-1) for nm in small_names] + [loss_vec.reshape(-1)])
    n_flat = small_flat.shape[0]
    red_rows = -(-n_flat // (8 * PACK_COLS)) * 8
    small_pack = jnp.pad(small_flat, (0, red_rows * PACK_COLS - n_flat)).reshape(red_rows, PACK_COLS)
    small_sum = _all_reduce_small(small_pack, name="reduce_small").reshape(-1)
    r0 = 0
    for nm in small_names:
        rows, cols = sgrads[nm].shape
        full = small_sum[r0:r0 + rows * cols].reshape(rows, cols)
        r0 += rows * cols
        if nm in sharded_small:
            lc = cols // N_DEV
            gl[nm] = lax.dynamic_slice(full, (0, me * lc), (rows, lc))
        else:
            gl[nm] = full
    loss = (0.5 / d) * jnp.sum(small_sum[r0:r0 + d])

    upd = {}
    for nm in LATE_WEIGHTS:
        upd[nm] = _adam(_as2d(wl[nm]), gl[nm], _as2d(ml[nm]), _as2d(vl[nm]), name=f"adam_{nm}")
    res_small = _adam_small([(_as2d(wl[nm]), gl[nm], _as2d(ml[nm]), _as2d(vl[nm])) for nm in small_names],
                            name="adam_small")
    for nm, r in zip(small_names, res_small):
        upd[nm] = r
    early_recv = _exchange_wait(ee_sems, ee_src, ee_land, upd[LATE_WEIGHTS[-1]][0], name="grads_early_wait",
                                gather=False)
    own_early = lax.dynamic_slice(early_grads, (me, 0, 0), (1,) + early_grads.shape[1:])
    gl.update(_unpack_local(_sum_blocks(early_recv, own_early, name="sum_grads_early"), EARLY_WEIGHTS, shapes))
    for nm in EARLY_WEIGHTS:
        upd[nm] = _adam(_as2d(wl[nm]), gl[nm], _as2d(ml[nm]), _as2d(vl[nm]), name=f"adam_{nm}")

    g_out = [gl[nm].reshape(wl[nm].shape) for nm in names]
    d_out = [upd[nm][0].reshape(wl[nm].shape) for nm in names]
    m_out = [upd[nm][1].reshape(wl[nm].shape) for nm in names]
    v_out = [upd[nm][2].reshape(wl[nm].shape) for nm in names]
    return (loss, grad_x, *g_out, *d_out, *m_out, *v_out)
```

```python
import functools
import math

import jax
import jax.numpy as jnp
from jax import lax
from jax.experimental import pallas as pl
from jax.experimental.pallas import tpu as pltpu

F32 = jnp.float32
BF16 = jnp.bfloat16

N_DEV = 8
HEAD_DIM = 64
N_SB_HEADS = 12
N_DIL_HEADS = 12
DIL_GROUPS = ((128, 1), (512, 4), (2048, 16))
SB_WIDTH = N_SB_HEADS * HEAD_DIM
MEM_WIDTH = 256
DIL_WIDTH = N_DIL_HEADS * HEAD_DIM
ATT_SCALE = HEAD_DIM ** -0.5
EPS = 1e-6
ALIBI_MAX_BIAS = 8.0
NEG_BIG = -1e30

ADAM_LR = 0.001
ADAM_B1 = 0.9
ADAM_B2 = 0.999
ADAM_EPS = 1e-08
ADAM_WD = 0.01
ADAM_STEP = 10

LANE = 128
QBLK = 128
VMEM_LIMIT_BYTES = 48 * 1024 * 1024
PACK_COLS = 1024
MESH_ID = pl.DeviceIdType.MESH


def _cp(*sem):
    return pltpu.CompilerParams(dimension_semantics=sem, vmem_limit_bytes=VMEM_LIMIT_BYTES)


def _pick(n, cands):
    for c in cands:
        if n % c == 0:
            return c
    raise ValueError(f"no tile for {n} in {cands}")


def _dot(a, b):
    return jnp.dot(a, b, preferred_element_type=F32)


def _dot_nt(a, b):
    return lax.dot_general(a, b, (((1,), (1,)), ((), ())), preferred_element_type=F32)


def _dot_tn(a, b):
    return lax.dot_general(a, b, (((0,), (0,)), ((), ())), preferred_element_type=F32)


def _dot_split(x, u):
    hi = x.astype(BF16)
    lo = (x - hi.astype(F32)).astype(BF16)
    return _dot(hi, u) + _dot(lo, u)


def _mm(a, b, *, name, out_dtype, res=None, trans_a=False, trans_b=False):
    assert not (trans_a and trans_b)
    if trans_a:
        kdim, m = a.shape
    else:
        m, kdim = a.shape
    if trans_b:
        n, kb = b.shape
    else:
        kb, n = b.shape
    assert kb == kdim, (a.shape, b.shape)
    if trans_a:
        tm = _pick(m, (1408, 1024, 512, 256, 128))
        tn = _pick(n, (1024, 1280, 1408, 768, 512, 256, 128))
        tk = _pick(kdim, (1024, 512, 256))
    else:
        tm = _pick(m, (1024, 512, 256, 128))
        tk = kdim if kdim <= 2816 else _pick(kdim, (2048, 1536, 1408, 1280, 1024, 512))
        tn = _pick(n, (512, 256, 128) if tk > 2048 else (1408, 1280, 1024, 768, 512, 256, 128))
    nk = kdim // tk
    has_res = res is not None

    def body(*refs):
        if has_res:
            a_ref, b_ref, r_ref, o_ref = refs[:4]
            scr = refs[4:]
        else:
            a_ref, b_ref, o_ref = refs[:3]
            r_ref = None
            scr = refs[3:]
        av = a_ref[...].astype(BF16)
        bv = b_ref[...].astype(BF16)
        if trans_a:
            p = _dot_tn(av, bv)
        elif trans_b:
            p = _dot_nt(av, bv)
        else:
            p = _dot(av, bv)

        def finish(acc):
            if has_res:
                acc = acc + r_ref[...]
            o_ref[...] = acc.astype(o_ref.dtype)

        if nk == 1:
            finish(p)
        else:
            acc_ref = scr[0]
            k = pl.program_id(2)

            @pl.when(k == 0)
            def _():
                acc_ref[...] = p

            @pl.when(k > 0)
            def _():
                acc_ref[...] += p

            @pl.when(k == nk - 1)
            def _():
                finish(acc_ref[...])

    if trans_a:
        a_spec = pl.BlockSpec((tk, tm), lambda i, j, k: (k, i))
    else:
        a_spec = pl.BlockSpec((tm, tk), lambda i, j, k: (i, k))
    if trans_b:
        b_spec = pl.BlockSpec((tn, tk), lambda i, j, k: (j, k))
    else:
        b_spec = pl.BlockSpec((tk, tn), lambda i, j, k: (k, j))
    in_specs = [a_spec, b_spec]
    args = [a, b]
    if has_res:
        in_specs.append(pl.BlockSpec((tm, tn), lambda i, j, k: (i, j)))
        args.append(res)
    return pl.pallas_call(
        body, name=name,
        grid=(m // tm, n // tn, nk),
        in_specs=in_specs,
        out_specs=pl.BlockSpec((tm, tn), lambda i, j, k: (i, j)),
        out_shape=jax.ShapeDtypeStruct((m, n), out_dtype),
        scratch_shapes=[pltpu.VMEM((tm, tn), F32)] if nk > 1 else [],
        compiler_params=_cp("parallel", "parallel", "arbitrary"),
    )(*args)


def _rms_fwd(x, gains, *, name):
    t, d = x.shape
    tr = _pick(t, (512, 256, 128, 8))
    ng = len(gains)

    def body(x_ref, *rest):
        g_refs, n_refs, r_ref = rest[:ng], rest[ng:2 * ng], rest[2 * ng]
        xv = x_ref[...]
        r = lax.rsqrt(jnp.mean(xv * xv, axis=-1, keepdims=True) + EPS)
        xh = xv * r
        for g_ref, n_ref in zip(g_refs, n_refs):
            n_ref[...] = (xh * g_ref[...]).astype(BF16)
        r_ref[...] = r

    row = pl.BlockSpec((tr, d), lambda i: (i, 0))
    gsp = pl.BlockSpec((1, d), lambda i: (0, 0))
    outs = pl.pallas_call(
        body, name=name, grid=(t // tr,),
        in_specs=[row] + [gsp] * ng,
        out_specs=[row] * ng + [pl.BlockSpec((tr, 1), lambda i: (i, 0))],
        out_shape=[jax.ShapeDtypeStruct((t, d), BF16)] * ng + [jax.ShapeDtypeStruct((t, 1), F32)],
        compiler_params=_cp("parallel"),
    )(x, *gains)
    return list(outs[:ng]), outs[ng]


def _rms_bwd(x, r, pairs, dres, *, name, need_dx=True):
    t, d = x.shape
    tr = _pick(t, (512, 256, 128, 8))
    npair = len(pairs)
    has_res = dres is not None

    def body(*refs):
        x_ref, r_ref = refs[:2]
        pr = refs[2:2 + 2 * npair]
        pos = 2 + 2 * npair
        res_ref = None
        if has_res:
            res_ref = refs[pos]
            pos += 1
        dx_ref = None
        if need_dx:
            dx_ref = refs[pos]
            pos += 1
        dg_refs = refs[pos:pos + npair]
        i = pl.program_id(0)
        rv = r_ref[...]
        xh = x_ref[...] * rv
        dx = res_ref[...] if has_res else None
        for k in range(npair):
            dn = pr[2 * k][...].astype(F32)
            g = pr[2 * k + 1][...]
            part = jnp.sum(dn * xh, axis=0, keepdims=True)

            @pl.when(i == 0)
            def _():
                dg_refs[k][...] = part

            @pl.when(i > 0)
            def _():
                dg_refs[k][...] += part

            if need_dx:
                dxh = dn * g
                c = jnp.mean(dxh * xh, axis=-1, keepdims=True)
                term = rv * (dxh - xh * c)
                dx = term if dx is None else dx + term
        if need_dx:
            dx_ref[...] = dx

    row = pl.BlockSpec((tr, d), lambda i: (i, 0))
    gsp = pl.BlockSpec((1, d), lambda i: (0, 0))
    in_specs = [row, pl.BlockSpec((tr, 1), lambda i: (i, 0))]
    args = [x, r]
    for dn, g in pairs:
        in_specs += [row, gsp]
        args += [dn, g]
    if has_res:
        in_specs.append(row)
        args.append(dres)
    out_specs, out_shape = [], []
    if need_dx:
        out_specs.append(row)
        out_shape.append(jax.ShapeDtypeStruct((t, d), F32))
    out_specs += [gsp] * npair
    out_shape += [jax.ShapeDtypeStruct((1, d), F32)] * npair
    outs = pl.pallas_call(
        body, name=name, grid=(t // tr,), in_specs=in_specs, out_specs=out_specs, out_shape=out_shape,
        compiler_params=_cp("arbitrary"),
    )(*args)
    if need_dx:
        return outs[0], list(outs[1:])
    return None, list(outs)


def _loss_head(h, g, tgt, *, name):
    t, d = h.shape
    tr = _pick(t, (512, 256, 128, 8))

    def body(h_ref, g_ref, t_ref, dh_ref, dg_ref, l_ref):
        i = pl.program_id(0)
        xv = h_ref[...]
        gv = g_ref[...]
        r = lax.rsqrt(jnp.mean(xv * xv, axis=-1, keepdims=True) + EPS)
        xh = xv * r
        e = xh * gv - t_ref[...]
        dy = e * (1.0 / d)
        lpart = jnp.sum(e * e, axis=0, keepdims=True)
        gpart = jnp.sum(dy * xh, axis=0, keepdims=True)

        @pl.when(i == 0)
        def _():
            l_ref[...] = lpart
            dg_ref[...] = gpart

        @pl.when(i > 0)
        def _():
            l_ref[...] += lpart
            dg_ref[...] += gpart

        dxh = dy * gv
        c = jnp.mean(dxh * xh, axis=-1, keepdims=True)
        dh_ref[...] = r * (dxh - xh * c)

    row = pl.BlockSpec((tr, d), lambda i: (i, 0))
    gsp = pl.BlockSpec((1, d), lambda i: (0, 0))
    return pl.pallas_call(
        body, name=name, grid=(t // tr,), in_specs=[row, gsp, row], out_specs=[row, gsp, gsp],
        out_shape=[jax.ShapeDtypeStruct((t, d), F32), jax.ShapeDtypeStruct((1, d), F32),
                   jax.ShapeDtypeStruct((1, d), F32)],
        compiler_params=_cp("arbitrary"),
    )(h, g, tgt)


GRP = 4 * HEAD_DIM
SB_KB = 2 * QBLK
SB_QB = 2 * QBLK


def _head_masks4(shape):
    lane = lax.broadcasted_iota(jnp.int32, shape, 1)
    return [(lane >= HEAD_DIM * h) & (lane < HEAD_DIM * (h + 1)) for h in range(4)]


def _neg_softplus(z):
    return jnp.minimum(-z, 0.0) - jnp.log(1.0 + jnp.exp(-jnp.abs(z)))


def _stacked_col_minus_row():
    rowi = lax.broadcasted_iota(jnp.int32, (4 * SB_QB, SB_KB), 0)
    coli = lax.broadcasted_iota(jnp.int32, (4 * SB_QB, SB_KB), 1)
    return coli - (rowi & (SB_QB - 1))


def _sb_fwd(proj, after, *, name):
    b, s, _ = proj.shape
    nq = s // SB_QB
    ngrp = SB_WIDTH // GRP

    def body(q_ref, k_ref, v_ref, after_ref, o_ref, r_ref, acc_ref, car_ref):
        i = pl.program_id(2)
        masks = _head_masks4((SB_QB, GRP))
        row = lax.broadcasted_iota(jnp.int32, (SB_KB, SB_KB), 0)
        col = lax.broadcasted_iota(jnp.int32, (SB_KB, SB_KB), 1)
        later_mat = (row > col).astype(BF16)
        col_minus_row = _stacked_col_minus_row()
        qs = q_ref[0] * jnp.asarray(ATT_SCALE, BF16)
        q_stack = jnp.concatenate([jnp.where(mk, qs, jnp.zeros_like(qs)) for mk in masks], axis=0)
        acc_ref[...] = jnp.zeros_like(acc_ref)
        car_ref[...] = jnp.zeros_like(car_ref)

        def process(jb, masked):
            off = pl.multiple_of(jb * SB_KB, SB_KB)
            k2 = k_ref[0, pl.ds(off, SB_KB), :]
            v2 = v_ref[0, pl.ds(off, SB_KB), :]
            z = _dot_nt(q_stack, k2)
            ls = _neg_softplus(z)
            if masked:
                causal = col_minus_row < (i * SB_QB - jb * SB_KB)
                ls = jnp.where(causal, ls, 0.0)
            later = _dot(ls.astype(BF16), later_mat)
            car = car_ref[...]
            w = jnp.exp((z + ls) + later + car)
            if masked:
                w = jnp.where(causal, w, 0.0)
            car_ref[...] = car + jnp.sum(ls, axis=1, keepdims=True)
            acc_ref[...] += _dot(w.astype(BF16), v2)

        top = (i * SB_QB) // SB_KB
        process(top, True)

        def step(jj, carry):
            process(top - 1 - jj, False)
            return carry

        lax.fori_loop(0, top, step, 0)
        o = acc_ref[pl.ds(0, SB_QB), :]
        r = car_ref[pl.ds(0, SB_QB), :]
        for h in range(1, 4):
            o = jnp.where(masks[h], acc_ref[pl.ds(h * SB_QB, SB_QB), :], o)
            r = jnp.where(masks[h], car_ref[pl.ds(h * SB_QB, SB_QB), :], r)
        o_ref[0] = o.astype(o_ref.dtype)
        r_ref[0] = r

    blk = pl.BlockSpec((1, SB_QB, GRP), lambda bb, p, i: (bb, i, p))
    return pl.pallas_call(
        body, name=name, grid=(b, ngrp, nq),
        in_specs=[blk,
                  pl.BlockSpec((1, s, GRP), lambda bb, p, i: (bb, 0, ngrp + p)),
                  pl.BlockSpec((1, s, GRP), lambda bb, p, i: (bb, 0, 2 * ngrp + p)),
                  pl.BlockSpec(memory_space=pl.ANY)],
        out_specs=[blk, blk],
        out_shape=[jax.ShapeDtypeStruct((b, s, SB_WIDTH), BF16), jax.ShapeDtypeStruct((b, s, SB_WIDTH), F32)],
        scratch_shapes=[pltpu.VMEM((4 * SB_QB, GRP), F32), pltpu.VMEM((4 * SB_QB, SB_KB), F32)],
        compiler_params=_cp("parallel", "parallel", "arbitrary"),
    )(proj, proj, proj, after)


def _sb_bwd(proj, dcat, rsum, after, *, name):
    b, s, _ = proj.shape
    nq = s // SB_QB
    ngrp = SB_WIDTH // GRP

    def body(q_ref, k_ref, v_ref, do_ref, r_ref, after_ref, dq_ref, dk_ref, dv_ref, dq_acc, cp_ref, cg_ref):
        i = pl.program_id(2)

        @pl.when(i == 0)
        def _():
            dk_ref[...] = jnp.zeros_like(dk_ref)
            dv_ref[...] = jnp.zeros_like(dv_ref)

        masks = _head_masks4((SB_QB, GRP))
        row = lax.broadcasted_iota(jnp.int32, (SB_KB, SB_KB), 0)
        col = lax.broadcasted_iota(jnp.int32, (SB_KB, SB_KB), 1)
        later_mat = (row > col).astype(BF16)
        excl_mat = (row < col).astype(BF16)
        col_minus_row = _stacked_col_minus_row()
        qs = q_ref[0] * jnp.asarray(ATT_SCALE, BF16)
        do = do_ref[0]
        q_stack = jnp.concatenate([jnp.where(mk, qs, jnp.zeros_like(qs)) for mk in masks], axis=0)
        do_stack = jnp.concatenate([jnp.where(mk, do, jnp.zeros_like(do)) for mk in masks], axis=0)
        rv = r_ref[0]
        r_stack = jnp.concatenate([rv[:, HEAD_DIM * h:HEAD_DIM * h + 1] for h in range(4)], axis=0)
        dq_acc[...] = jnp.zeros_like(dq_acc)
        cp_ref[...] = jnp.zeros_like(cp_ref)
        cg_ref[...] = jnp.zeros_like(cg_ref)

        def process(jb, masked):
            off = pl.multiple_of(jb * SB_KB, SB_KB)
            k2 = k_ref[0, pl.ds(off, SB_KB), :]
            v2 = v_ref[0, pl.ds(off, SB_KB), :]
            z = _dot_nt(q_stack, k2)
            dw = _dot_nt(do_stack, v2)
            ls = _neg_softplus(z)
            lsig = z + ls
            if masked:
                causal = col_minus_row < (i * SB_QB - jb * SB_KB)
                ls = jnp.where(causal, ls, 0.0)
            later = _dot(ls.astype(BF16), later_mat)
            cpv = cp_ref[...] + jnp.sum(ls, axis=1, keepdims=True)
            cp_ref[...] = cpv
            w = jnp.exp(lsig + ((r_stack - cpv) + later))
            if masked:
                w = jnp.where(causal, w, 0.0)
            g = dw * w
            gpre = _dot(g.astype(BF16), excl_mat)
            cgv = cg_ref[...]
            cg_ref[...] = cgv + jnp.sum(g, axis=1, keepdims=True)
            dz = g - jnp.exp(lsig) * (g + (gpre + cgv))
            if masked:
                dz = jnp.where(causal, dz, 0.0)
            dzb = dz.astype(BF16)
            dq_acc[...] += _dot(dzb, k2)
            dk_ref[0, pl.ds(off, SB_KB), :] += _dot_tn(dzb, q_stack)
            dv_ref[0, pl.ds(off, SB_KB), :] += _dot_tn(w.astype(BF16), do_stack)

        top = (i * SB_QB) // SB_KB

        def step(jb, carry):
            process(jb, False)
            return carry

        lax.fori_loop(0, top, step, 0)
        process(top, True)
        dq = dq_acc[pl.ds(0, SB_QB), :]
        for h in range(1, 4):
            dq = jnp.where(masks[h], dq_acc[pl.ds(h * SB_QB, SB_QB), :], dq)
        dq_ref[0] = (dq * ATT_SCALE).astype(dq_ref.dtype)

    blk = pl.BlockSpec((1, SB_QB, GRP), lambda bb, p, i: (bb, i, p))
    seq = pl.BlockSpec((1, s, GRP), lambda bb, p, i: (bb, 0, p))
    return pl.pallas_call(
        body, name=name, grid=(b, ngrp, nq),
        in_specs=[blk,
                  pl.BlockSpec((1, s, GRP), lambda bb, p, i: (bb, 0, ngrp + p)),
                  pl.BlockSpec((1, s, GRP), lambda bb, p, i: (bb, 0, 2 * ngrp + p)),
                  blk, blk, pl.BlockSpec(memory_space=pl.ANY)],
        out_specs=[blk, seq, seq],
        out_shape=[jax.ShapeDtypeStruct((b, s, SB_WIDTH), BF16), jax.ShapeDtypeStruct((b, s, SB_WIDTH), F32),
                   jax.ShapeDtypeStruct((b, s, SB_WIDTH), F32)],
        scratch_shapes=[pltpu.VMEM((4 * SB_QB, GRP), F32), pltpu.VMEM((4 * SB_QB, SB_KB), F32),
                        pltpu.VMEM((4 * SB_QB, SB_KB), F32)],
        compiler_params=_cp("parallel", "parallel", "arbitrary"),
    )(proj, proj, proj, dcat, rsum, after)


def _band_bias(slopes_scaled):
    a = lax.broadcasted_iota(jnp.int32, (QBLK, 2 * QBLK), 0)
    bcol = lax.broadcasted_iota(jnp.int32, (QBLK, 2 * QBLK), 1)
    delta = a + QBLK - bcol
    in_band = (delta >= 0) & (delta <= QBLK)
    dist = delta.astype(F32)
    bias = jnp.concatenate([(-sl) * dist for sl in slopes_scaled], axis=0)
    return jnp.concatenate([in_band] * 4, axis=0), jnp.concatenate([bcol >= QBLK] * 4, axis=0), bias


def _stack_heads(x, masks):
    return jnp.concatenate([jnp.where(mk, x, jnp.zeros_like(x)) for mk in masks], axis=0)


def _unstack_heads(x, masks):
    out = jnp.broadcast_to(x[0:QBLK], (QBLK, GRP))
    for h in range(1, 4):
        out = jnp.where(masks[h], x[h * QBLK:(h + 1) * QBLK], out)
    return out


PER_HEAD = 8


def _head_column(x):
    return jnp.concatenate([x[:, h:h + 1] for h in range(4)], axis=0)


def _head_lanes(col):
    lane = lax.broadcasted_iota(jnp.int32, (QBLK, PER_HEAD), 1)
    out = jnp.zeros((QBLK, PER_HEAD), F32)
    for h in range(4):
        out = jnp.where(lane == h, col[h * QBLK:(h + 1) * QBLK], out)
    return out


def _spread_heads(x8, rows):
    masks = _head_masks4((rows, GRP))
    out = jnp.broadcast_to(x8[:, 0:1], (rows, GRP))
    for h in range(1, 4):
        out = jnp.where(masks[h], x8[:, h:h + 1], out)
    return out


def _attn_specs(banded, nsub, q_lane_blk, k_lane_blk, v_lane_blk):
    tq = nsub * QBLK
    qs = pl.BlockSpec((1, tq, GRP), lambda n, i: (n, i, q_lane_blk))
    if banded:
        ks = [pl.BlockSpec((1, QBLK, GRP), lambda n, i: (n, jnp.maximum(nsub * i - 1, 0), k_lane_blk)),
              pl.BlockSpec((1, tq, GRP), lambda n, i: (n, i, k_lane_blk))]
        vs = [pl.BlockSpec((1, QBLK, GRP), lambda n, i: (n, jnp.maximum(nsub * i - 1, 0), v_lane_blk)),
              pl.BlockSpec((1, tq, GRP), lambda n, i: (n, i, v_lane_blk))]
    else:
        ks = [pl.BlockSpec((1, 2 * QBLK, GRP), lambda n, i: (n, 0, k_lane_blk))]
        vs = [pl.BlockSpec((1, 2 * QBLK, GRP), lambda n, i: (n, 0, v_lane_blk))]
    return qs, ks, vs


def _attn_fwd(q, k, v, *, name, banded, slopes_scaled=None, q_lane_blk=0, k_lane_blk=0, v_lane_blk=0):
    n, l, _ = q.shape
    nsub = 2 if l % (2 * QBLK) == 0 else 1
    tq = nsub * QBLK
    nkv = 2 if banded else 1

    def body(*refs):
        q_ref = refs[0]
        k_refs = refs[1:1 + nkv]
        v_refs = refs[1 + nkv:1 + 2 * nkv]
        o_ref, lse_ref = refs[1 + 2 * nkv:]
        step = pl.program_id(1)
        masks = _head_masks4((QBLK, GRP))
        qs = q_ref[0] * jnp.asarray(ATT_SCALE, BF16)
        if banded:
            kall = jnp.concatenate([k_refs[0][0], k_refs[1][0]], axis=0)
            vall = jnp.concatenate([v_refs[0][0], v_refs[1][0]], axis=0)
            in_band, is_cur, bias = _band_bias(slopes_scaled)
        scs, v2s = [], []
        for u in range(nsub):
            k2 = kall[u * QBLK:(u + 2) * QBLK] if banded else k_refs[0][0]
            v2s.append(vall[u * QBLK:(u + 2) * QBLK] if banded else v_refs[0][0])
            scs.append(_dot_nt(_stack_heads(qs[u * QBLK:(u + 1) * QBLK], masks), k2))
        ps, dens, lses = [], [], []
        for u in range(nsub):
            sc = scs[u]
            if banded:
                valid = in_band & (is_cur | (step * nsub + u > 0))
                sc = jnp.where(valid, sc + bias, NEG_BIG)
            m = jnp.max(sc, axis=-1, keepdims=True)
            p = jnp.exp(sc - m)
            den = jnp.sum(p, axis=-1, keepdims=True)
            ps.append(p.astype(BF16))
            dens.append(den)
            lses.append(m + jnp.log(den))
        ohs = [_dot(ps[u], v2s[u]) for u in range(nsub)]
        for u in range(nsub):
            o_ref[0, u * QBLK:(u + 1) * QBLK, :] = _unstack_heads(ohs[u] / dens[u], masks).astype(o_ref.dtype)
            lse_ref[0, u * QBLK:(u + 1) * QBLK, :] = _head_lanes(lses[u])

    qs, ks, vs = _attn_specs(banded, nsub, q_lane_blk, k_lane_blk, v_lane_blk)
    ob = pl.BlockSpec((1, tq, GRP), lambda nn, i: (nn, i, 0))
    return pl.pallas_call(
        body, name=name, grid=(n, l // tq),
        in_specs=[qs] + ks + vs, out_specs=[ob, pl.BlockSpec((1, tq, PER_HEAD), lambda nn, i: (nn, i, 0))],
        out_shape=[jax.ShapeDtypeStruct((n, l, GRP), BF16), jax.ShapeDtypeStruct((n, l, PER_HEAD), F32)],
        compiler_params=_cp("parallel", "arbitrary"),
    )(q, *([k] * nkv), *([v] * nkv))


def _attn_bwd(q, k, v, do, lse, delta, *, name, banded, slopes_scaled=None, q_lane_blk=0, k_lane_blk=0,
              v_lane_blk=0, do_lane_blk=0):
    n, l, _ = q.shape
    nsub = 2 if l % (2 * QBLK) == 0 else 1
    tq = nsub * QBLK
    nsteps = l // tq
    nkv = 2 if banded else 1
    lk = l if banded else 2 * QBLK

    def body(*refs):
        q_ref = refs[0]
        k_refs = refs[1:1 + nkv]
        v_refs = refs[1 + nkv:1 + 2 * nkv]
        do_ref, lse_ref, dl_ref, dq_ref, dk_out, dv_out, dk_ref, dv_ref = refs[1 + 2 * nkv:]
        step = pl.program_id(1)

        @pl.when(step == 0)
        def _():
            dk_ref[...] = jnp.zeros_like(dk_ref)
            dv_ref[...] = jnp.zeros_like(dv_ref)

        masks = _head_masks4((QBLK, GRP))
        qs = q_ref[0] * jnp.asarray(ATT_SCALE, BF16)
        dov = do_ref[0]
        lsev = lse_ref[0]
        dlv = dl_ref[0]
        if banded:
            kall = jnp.concatenate([k_refs[0][0], k_refs[1][0]], axis=0)
            vall = jnp.concatenate([v_refs[0][0], v_refs[1][0]], axis=0)
            in_band, is_cur, bias = _band_bias(slopes_scaled)
        q_st, do_st, k2s, scs, dps = [], [], [], [], []
        for u in range(nsub):
            rows = slice(u * QBLK, (u + 1) * QBLK)
            k2s.append(kall[u * QBLK:(u + 2) * QBLK] if banded else k_refs[0][0])
            v2 = vall[u * QBLK:(u + 2) * QBLK] if banded else v_refs[0][0]
            q_st.append(_stack_heads(qs[rows], masks))
            do_st.append(_stack_heads(dov[rows], masks))
            scs.append(_dot_nt(q_st[u], k2s[u]))
            dps.append(_dot_nt(do_st[u], v2))
        pbs, dss = [], []
        for u in range(nsub):
            rows = slice(u * QBLK, (u + 1) * QBLK)
            sc = scs[u]
            if banded:
                valid = in_band & (is_cur | (step * nsub + u > 0))
                sc = jnp.where(valid, sc + bias, NEG_BIG)
            p = jnp.exp(sc - _head_column(lsev[rows]))
            pbs.append(p.astype(BF16))
            dss.append((p * (dps[u] - _head_column(dlv[rows]))).astype(BF16))
        dqs = [_dot(dss[u], k2s[u]) for u in range(nsub)]
        dk2s = [_dot_tn(dss[u], q_st[u]) for u in range(nsub)]
        dv2s = [_dot_tn(pbs[u], do_st[u]) for u in range(nsub)]
        for u in range(nsub):
            dq_ref[0, u * QBLK:(u + 1) * QBLK, :] = (_unstack_heads(dqs[u], masks) * ATT_SCALE).astype(dq_ref.dtype)
        if banded:
            for u in range(nsub):
                i = step * nsub + u
                cur = pl.multiple_of(i * QBLK, QBLK)
                dk_ref[0, pl.ds(cur, QBLK), :] += dk2s[u][QBLK:]
                dv_ref[0, pl.ds(cur, QBLK), :] += dv2s[u][QBLK:]

                @pl.when(i > 0)
                def _():
                    prev = pl.multiple_of((i - 1) * QBLK, QBLK)
                    dk_ref[0, pl.ds(prev, QBLK), :] += dk2s[u][:QBLK]
                    dv_ref[0, pl.ds(prev, QBLK), :] += dv2s[u][:QBLK]
        else:
            dk_ref[0] += functools.reduce(jnp.add, dk2s)
            dv_ref[0] += functools.reduce(jnp.add, dv2s)

        @pl.when(step == nsteps - 1)
        def _():
            dk_out[...] = dk_ref[...].astype(dk_out.dtype)
            dv_out[...] = dv_ref[...].astype(dv_out.dtype)

    qs, ks, vs = _attn_specs(banded, nsub, q_lane_blk, k_lane_blk, v_lane_blk)
    ob = pl.BlockSpec((1, tq, GRP), lambda nn, i: (nn, i, 0))
    stat = pl.BlockSpec((1, tq, PER_HEAD), lambda nn, i: (nn, i, 0))
    dos = pl.BlockSpec((1, tq, GRP), lambda nn, i: (nn, i, do_lane_blk))
    kvb = pl.BlockSpec((1, lk, GRP), lambda nn, i: (nn, 0, 0))
    return pl.pallas_call(
        body, name=name, grid=(n, nsteps),
        in_specs=[qs] + ks + vs + [dos, stat, stat], out_specs=[ob, kvb, kvb],
        out_shape=[jax.ShapeDtypeStruct((n, l, GRP), BF16), jax.ShapeDtypeStruct((n, lk, GRP), BF16),
                   jax.ShapeDtypeStruct((n, lk, GRP), BF16)],
        scratch_shapes=[pltpu.VMEM((1, lk, GRP), F32), pltpu.VMEM((1, lk, GRP), F32)],
        compiler_params=_cp("parallel", "arbitrary"),
    )(q, *([k] * nkv), *([v] * nkv), do, lse, delta)


def _attn_delta(do, o, *, name, lane_blks):
    t, _ = do.shape
    tr = _pick(t, (512, 256, 128, 8))
    ng = len(lane_blks)

    def body(*refs):
        do_refs, o_refs, d_ref = refs[:ng], refs[ng:2 * ng], refs[2 * ng]
        ra = lax.broadcasted_iota(jnp.int32, (GRP, LANE), 0) // HEAD_DIM
        rb = lax.broadcasted_iota(jnp.int32, (GRP, LANE), 1)
        head_sum = (ra == rb).astype(BF16)
        prod = None
        for a_ref, b_ref in zip(do_refs, o_refs):
            term = a_ref[...].astype(F32) * b_ref[...].astype(F32)
            prod = term if prod is None else prod + term
        d_ref[...] = _dot_split(prod, head_sum)[:, :PER_HEAD]

    specs = [pl.BlockSpec((tr, GRP), functools.partial(lambda i, lb: (i, lb), lb=lb)) for lb in lane_blks]
    return pl.pallas_call(
        body, name=name, grid=(t // tr,), in_specs=specs + specs,
        out_specs=pl.BlockSpec((tr, PER_HEAD), lambda i: (i, 0)),
        out_shape=jax.ShapeDtypeStruct((t, PER_HEAD), F32),
        compiler_params=_cp("parallel"),
    )(*([do] * ng), *([o] * ng))


def _dil_combine(os, lses, *, name):
    t, _ = os[0].shape
    tr = _pick(t, (512, 256, 128, 8))
    ng = len(os)

    def body(*refs):
        o_refs, l_refs = refs[:ng], refs[ng:2 * ng]
        out_ref, lse_ref = refs[2 * ng:]
        ls = [r[...] for r in l_refs]
        m = functools.reduce(jnp.maximum, ls)
        tot = None
        for lv in ls:
            e = jnp.exp(lv - m)
            tot = e if tot is None else tot + e
        lse = m + jnp.log(tot)
        for g in range(ng):
            alpha = _spread_heads(jnp.exp(ls[g] - lse), tr)
            out_ref[:, GRP * g:GRP * (g + 1)] = (o_refs[g][...].astype(F32) * alpha).astype(out_ref.dtype)
        lse_ref[...] = lse

    sp = pl.BlockSpec((tr, GRP), lambda i: (i, 0))
    st = pl.BlockSpec((tr, PER_HEAD), lambda i: (i, 0))
    return pl.pallas_call(
        body, name=name, grid=(t // tr,), in_specs=[sp] * ng + [st] * ng,
        out_specs=[pl.BlockSpec((tr, GRP * ng), lambda i: (i, 0)), st],
        out_shape=[jax.ShapeDtypeStruct((t, GRP * ng), BF16), jax.ShapeDtypeStruct((t, PER_HEAD), F32)],
        compiler_params=_cp("parallel"),
    )(*os, *lses)


FFN_LB = 256
FFN_ROWS = 64
HALO = 16


def _conv_chunk(u_ref, w, ci):
    r0 = pl.multiple_of(ci * FFN_ROWS, FFN_ROWS)
    cur = u_ref[0, pl.ds(r0, FFN_ROWS), :].astype(F32)
    p0 = pl.multiple_of(jnp.maximum(r0 - HALO, 0), HALO)
    prev = u_ref[0, pl.ds(p0, HALO), :].astype(F32)
    prev = jnp.where(ci > 0, prev, 0.0)
    rowi = lax.broadcasted_iota(jnp.int32, (8, cur.shape[1]), 0)
    r1 = pltpu.roll(cur, 1, 0)
    r2 = pltpu.roll(cur, 2, 0)
    s1 = jnp.concatenate([jnp.where(rowi == 0, prev[HALO - 1:HALO], r1[0:8]), r1[8:]], axis=0)
    s2 = jnp.concatenate([jnp.where(rowi == 0, prev[HALO - 2:HALO - 1],
                                    jnp.where(rowi == 1, prev[HALO - 1:HALO], r2[0:8])), r2[8:]], axis=0)
    c = w[0:1] * s2
    c = c + w[1:2] * s1
    c = c + w[2:3] * cur
    return c, cur, s1, s2


def _ffn_mid_fwd(u, wconv, *, name):
    b, s, f2 = u.shape
    f = f2 // 2
    nlb = f // FFN_LB

    def body(ua_ref, ug_ref, wa_ref, wg_ref, h_ref):
        wa = wa_ref[...]
        wg = wg_ref[...]

        def step(ci, carry):
            ca = _conv_chunk(ua_ref, wa, ci)[0]
            cg = _conv_chunk(ug_ref, wg, ci)[0]
            r0 = pl.multiple_of(ci * FFN_ROWS, FFN_ROWS)
            h_ref[0, pl.ds(r0, FFN_ROWS), :] = (cg * jax.nn.sigmoid(cg) * ca).astype(h_ref.dtype)
            return carry

        lax.fori_loop(0, s // FFN_ROWS, step, 0)

    return pl.pallas_call(
        body, name=name, grid=(nlb, b),
        in_specs=[pl.BlockSpec((1, s, FFN_LB), lambda l, bb: (bb, 0, l)),
                  pl.BlockSpec((1, s, FFN_LB), lambda l, bb: (bb, 0, nlb + l)),
                  pl.BlockSpec((3, FFN_LB), lambda l, bb: (0, l)),
                  pl.BlockSpec((3, FFN_LB), lambda l, bb: (0, nlb + l))],
        out_specs=pl.BlockSpec((1, s, FFN_LB), lambda l, bb: (bb, 0, l)),
        out_shape=jax.ShapeDtypeStruct((b, s, f), BF16),
        compiler_params=_cp("parallel", "parallel"),
    )(u, u, wconv, wconv)


def _ffn_mid_bwd(u, wconv, dh, *, name):
    b, s, f2 = u.shape
    f = f2 // 2
    nlb = f // FFN_LB
    nchunk = s // FFN_ROWS

    def body(ua_ref, ug_ref, wa_ref, wg_ref, dh_ref, dua_ref, dug_ref, dwa_ref, dwg_ref):
        bb = pl.program_id(1)
        wa = wa_ref[...]
        wg = wg_ref[...]
        rowi = lax.broadcasted_iota(jnp.int32, (8, FFN_LB), 0)
        last = FFN_ROWS - 8

        def conv_transpose(dc, nxt, w):
            r1 = pltpu.roll(dc, FFN_ROWS - 1, 0)
            r2 = pltpu.roll(dc, FFN_ROWS - 2, 0)
            n1 = jnp.concatenate([r1[:last], jnp.where(rowi == 7, nxt[0:1], r1[last:])], axis=0)
            n2 = jnp.concatenate([r2[:last], jnp.where(rowi == 6, nxt[0:1],
                                                       jnp.where(rowi == 7, nxt[1:2], r2[last:]))], axis=0)
            return w[2:3] * dc + w[1:2] * n1 + w[0:1] * n2

        def step(t, carry):
            ci = nchunk - 1 - t
            nxt_a, nxt_g = carry[0], carry[1]
            r0 = pl.multiple_of(ci * FFN_ROWS, FFN_ROWS)
            ca, cura, s1a, s2a = _conv_chunk(ua_ref, wa, ci)
            cg, curg, s1g, s2g = _conv_chunk(ug_ref, wg, ci)
            dhv = dh_ref[0, pl.ds(r0, FFN_ROWS), :].astype(F32)
            sg = jax.nn.sigmoid(cg)
            da = dhv * (cg * sg)
            dg = dhv * ca * (sg * (1.0 + cg * (1.0 - sg)))
            dua_ref[0, pl.ds(r0, FFN_ROWS), :] = conv_transpose(da, nxt_a, wa).astype(dua_ref.dtype)
            dug_ref[0, pl.ds(r0, FFN_ROWS), :] = conv_transpose(dg, nxt_g, wg).astype(dug_ref.dtype)
            red = lambda x: jnp.sum(x, axis=0, keepdims=True)
            parts = (red(da * s2a), red(da * s1a), red(da * cura), red(dg * s2g), red(dg * s1g), red(dg * curg))
            return (da[0:8], dg[0:8]) + tuple(c + p for c, p in zip(carry[2:], parts))

        zero = jnp.zeros((1, FFN_LB), F32)
        zero8 = jnp.zeros((8, FFN_LB), F32)
        taps = lax.fori_loop(0, nchunk, step, (zero8, zero8) + (zero,) * 6)[2:]

        @pl.when(bb == 0)
        def _():
            for k in range(3):
                dwa_ref[k:k + 1, :] = taps[k]
                dwg_ref[k:k + 1, :] = taps[3 + k]

        @pl.when(bb > 0)
        def _():
            for k in range(3):
                dwa_ref[k:k + 1, :] += taps[k]
                dwg_ref[k:k + 1, :] += taps[3 + k]

    seq_a = pl.BlockSpec((1, s, FFN_LB), lambda l, bb: (bb, 0, l))
    seq_g = pl.BlockSpec((1, s, FFN_LB), lambda l, bb: (bb, 0, nlb + l))
    wsp = pl.BlockSpec((3, FFN_LB), lambda l, bb: (0, l))
    return pl.pallas_call(
        body, name=name, grid=(nlb, b),
        in_specs=[seq_a, seq_g, wsp, pl.BlockSpec((3, FFN_LB), lambda l, bb: (0, nlb + l)), seq_a],
        out_specs=[seq_a, seq_a, wsp, wsp],
        out_shape=[jax.ShapeDtypeStruct((b, s, f), BF16), jax.ShapeDtypeStruct((b, s, f), BF16),
                   jax.ShapeDtypeStruct((3, f), F32), jax.ShapeDtypeStruct((3, f), F32)],
        compiler_params=_cp("parallel", "arbitrary"),
    )(u, u, wconv, wconv, dh)


def _adam_math(w, g, m, v):
    m2 = ADAM_B1 * m + (1.0 - ADAM_B1) * g
    v2 = ADAM_B2 * v + (1.0 - ADAM_B2) * (g * g)
    m_hat = m2 / (1.0 - ADAM_B1 ** ADAM_STEP)
    v_hat = v2 / (1.0 - ADAM_B2 ** ADAM_STEP)
    delta = -ADAM_LR * (m_hat / (jnp.sqrt(v_hat) + ADAM_EPS) + ADAM_WD * w)
    return delta, m2, v2


def _adam(w, g, m, v, *, name):
    r, c = w.shape
    tr = _pick(r, (256, 128, 88, 64, 32, 16, 8))

    def body(w_ref, g_ref, m_ref, v_ref, d_ref, m2_ref, v2_ref):
        d, m2, v2 = _adam_math(w_ref[...], g_ref[...], m_ref[...], v_ref[...])
        d_ref[...] = d
        m2_ref[...] = m2
        v2_ref[...] = v2

    sp = pl.BlockSpec((tr, c), lambda i: (i, 0))
    return pl.pallas_call(
        body, name=name, grid=(r // tr,), in_specs=[sp] * 4, out_specs=[sp] * 3,
        out_shape=[jax.ShapeDtypeStruct((r, c), F32)] * 3,
        compiler_params=_cp("parallel"),
    )(w, g, m, v)


def _adam_small(quads, *, name):
    nq = len(quads)

    def body(*refs):
        ins, outs = refs[:4 * nq], refs[4 * nq:]
        for k in range(nq):
            w_ref, g_ref, m_ref, v_ref = ins[4 * k:4 * k + 4]
            d, m2, v2 = _adam_math(w_ref[...], g_ref[...], m_ref[...], v_ref[...])
            outs[3 * k][...] = d
            outs[3 * k + 1][...] = m2
            outs[3 * k + 2][...] = v2

    flat = [a for q in quads for a in q]
    out_shape = [jax.ShapeDtypeStruct(q[0].shape, F32) for q in quads for _ in range(3)]
    vm = pl.BlockSpec(memory_space=pltpu.VMEM)
    outs = pl.pallas_call(
        body, name=name, in_specs=[vm] * len(flat), out_specs=[vm] * len(out_shape), out_shape=out_shape,
        compiler_params=pltpu.CompilerParams(vmem_limit_bytes=VMEM_LIMIT_BYTES),
    )(*flat)
    return [tuple(outs[3 * k:3 * k + 3]) for k in range(nq)]


def _mesh_pos():
    return lax.axis_index("x"), lax.axis_index("y"), lax.axis_index("c")


def _flip(v, bit):
    return 1 - v if bit else v


def _all_gather_hbm(xl, *, name):
    r, c = xl.shape

    def body(x_ref, out_ref, send_sems, recv_sems, local_sem):
        x, y, cc = _mesh_pos()
        me, sibling = (x, y, cc), (x, y, 1 - cc)
        chips = [(1 - x, y), (x, 1 - y), (1 - x, 1 - y)]

        def rows(px, py, pc):
            return out_ref.at[pl.ds((4 * px + 2 * py + pc) * r, r), :]

        def copy(k, block, to, src=None):
            return pltpu.make_async_remote_copy(
                src_ref=rows(*block) if src is None else src, dst_ref=rows(*block),
                send_sem=send_sems.at[k], recv_sem=recv_sems.at[k], device_id=to, device_id_type=MESH_ID)

        mine = pltpu.make_async_copy(x_ref, rows(*me), local_sem)
        mine.start()
        first = [copy(0, me, sibling, src=x_ref)]
        first += [copy(1 + j, me, (*chip, cc), src=x_ref) for j, chip in enumerate(chips)]
        for cp in first:
            cp.start()
        passed = [copy(4 + j, (*chip, cc), sibling) for j, chip in enumerate(chips)]
        for j, chip in enumerate(chips):
            copy(1 + j, (*chip, cc), me).wait_recv()
            passed[j].start()
        copy(0, sibling, me).wait_recv()
        for j, chip in enumerate(chips):
            copy(4 + j, (*chip, 1 - cc), me).wait_recv()
        for cp in first + passed:
            cp.wait_send()
        mine.wait()

    hbm = pl.BlockSpec(memory_space=pltpu.HBM)
    return pl.pallas_call(
        body, name=name, in_specs=[hbm], out_specs=hbm,
        out_shape=jax.ShapeDtypeStruct((N_DEV * r, c), xl.dtype),
        scratch_shapes=[pltpu.SemaphoreType.DMA((7,)), pltpu.SemaphoreType.DMA((7,)), pltpu.SemaphoreType.DMA],
    )(xl)


def _all_reduce_small(xl, *, name):
    r, c = xl.shape

    def body(x_ref, sum_ref, all_ref, send_sems, recv_sems, local_sem):
        x, y, cc = _mesh_pos()
        me, sibling = (x, y, cc), (x, y, 1 - cc)
        chips = [(1 - x, y), (x, 1 - y), (1 - x, 1 - y)]

        def rows(px, py, pc):
            return all_ref.at[pl.ds((4 * px + 2 * py + pc) * r, r), :]

        def copy(k, block, to, src=None):
            return pltpu.make_async_remote_copy(
                src_ref=rows(*block) if src is None else src, dst_ref=rows(*block),
                send_sem=send_sems.at[k], recv_sem=recv_sems.at[k], device_id=to, device_id_type=MESH_ID)

        mine = pltpu.make_async_copy(x_ref, rows(*me), local_sem)
        mine.start()
        first = [copy(0, me, sibling, src=x_ref)]
        first += [copy(1 + j, me, (*chip, cc), src=x_ref) for j, chip in enumerate(chips)]
        for cp in first:
            cp.start()
        passed = [copy(4 + j, (*chip, cc), sibling) for j, chip in enumerate(chips)]
        for j, chip in enumerate(chips):
            copy(1 + j, (*chip, cc), me).wait_recv()
            passed[j].start()
        copy(0, sibling, me).wait_recv()
        for j, chip in enumerate(chips):
            copy(4 + j, (*chip, 1 - cc), me).wait_recv()
        for cp in first + passed:
            cp.wait_send()
        mine.wait()
        tot = all_ref[pl.ds(0, r), :]
        for dd in range(1, N_DEV):
            tot = tot + all_ref[pl.ds(dd * r, r), :]
        sum_ref[...] = tot

    vm = pl.BlockSpec(memory_space=pltpu.VMEM)
    return pl.pallas_call(
        body, name=name, in_specs=[vm], out_specs=[vm, vm],
        out_shape=[jax.ShapeDtypeStruct((r, c), F32), jax.ShapeDtypeStruct((N_DEV * r, c), F32)],
        scratch_shapes=[pltpu.SemaphoreType.DMA((7,)), pltpu.SemaphoreType.DMA((7,)), pltpu.SemaphoreType.DMA],
    )(xl)[0]


N_PEERS = N_DEV - 1
_HBM = pl.BlockSpec(memory_space=pltpu.HBM)
_SEM = pl.BlockSpec(memory_space=pltpu.SEMAPHORE)


def _peer_list(x, y, cc):
    return [(_flip(x, rel & 4), _flip(y, rel & 2), _flip(cc, rel & 1)) for rel in range(1, N_DEV)]


def _dev_index(p):
    return 4 * p[0] + 2 * p[1] + p[2]


def _split_copy(src_ref, land_ref, sems, k, peer, me, gather, landing_of):
    if gather:
        r = src_ref.shape[0]
        src = src_ref
        dst = land_ref.at[pl.ds(_dev_index(landing_of) * r, r), :]
    else:
        src = src_ref.at[_dev_index(peer)]
        dst = land_ref.at[_dev_index(landing_of)]
    return pltpu.make_async_remote_copy(src_ref=src, dst_ref=dst, send_sem=sems[k], recv_sem=sems[N_PEERS + k],
                                        device_id=peer, device_id_type=MESH_ID)


def _exchange_start(src, land_shape, *, name, gather):
    def body(src_ref, land_ref, *rest):
        sems = rest[:2 * N_PEERS]
        token = rest[2 * N_PEERS + 2]
        x, y, cc = _mesh_pos()
        me = (x, y, cc)
        for k, peer in enumerate(_peer_list(x, y, cc)):
            _split_copy(src_ref, land_ref, sems, k, peer, me, gather, landing_of=me).start()
        token[...] = jnp.zeros_like(token)

    outs = pl.pallas_call(
        body, name=name,
        out_shape=tuple([pltpu.SemaphoreType.DMA(())] * (2 * N_PEERS)) + (
            pltpu.HBM(src.shape, src.dtype), pltpu.HBM(land_shape, src.dtype),
            jax.ShapeDtypeStruct((8, LANE), F32)),
        in_specs=(_HBM, _HBM),
        out_specs=tuple([_SEM] * (2 * N_PEERS)) + (_HBM, _HBM, pl.BlockSpec(memory_space=pltpu.VMEM)),
        input_output_aliases={0: 2 * N_PEERS, 1: 2 * N_PEERS + 1},
        compiler_params=pltpu.CompilerParams(has_side_effects=pltpu.SideEffectType.DATAFLOW_SIDE_EFFECTING),
    )(pltpu.with_memory_space_constraint(src, pltpu.HBM),
      pltpu.with_memory_space_constraint(lax.empty(land_shape, src.dtype), pltpu.HBM))
    return outs[:2 * N_PEERS], outs[2 * N_PEERS], outs[2 * N_PEERS + 1], outs[2 * N_PEERS + 2]


def _exchange_wait(sems, src_thru, land_thru, after, *, name, gather):
    def body(src_ref, land_ref, *rest):
        sem_refs = rest[:2 * N_PEERS]
        x, y, cc = _mesh_pos()
        me = (x, y, cc)
        for k, peer in enumerate(_peer_list(x, y, cc)):
            cp = _split_copy(src_ref, land_ref, sem_refs, k, peer, me, gather, landing_of=peer)
            cp.wait_send()
            cp.wait_recv()

    outs = pl.pallas_call(
        body, name=name,
        out_shape=(pltpu.HBM(src_thru.shape, src_thru.dtype), pltpu.HBM(land_thru.shape, land_thru.dtype)),
        in_specs=(_HBM, _HBM) + tuple([_SEM] * (2 * N_PEERS)) + (pl.BlockSpec(memory_space=pl.ANY),),
        out_specs=(_HBM, _HBM), input_output_aliases={0: 0, 1: 1},
        compiler_params=pltpu.CompilerParams(has_side_effects=pltpu.SideEffectType.DATAFLOW_SIDE_EFFECTING),
    )(src_thru, land_thru, *sems, after)
    return outs[1]


def _sum_blocks(recv, own, after, *, name):
    nd, r, c = recv.shape
    tr = _pick(r, (448, 256, 128, 64, 32, 16))

    def body(x_ref, own_ref, after_ref, o_ref):
        x, y, cc = _mesh_pos()
        me = 4 * x + 2 * y + cc
        tot = None
        for dd in range(nd):
            term = jnp.where(me == dd, own_ref[0], x_ref[dd]).astype(F32)
            tot = term if tot is None else tot + term
        o_ref[...] = tot

    return pl.pallas_call(
        body, name=name, grid=(r // tr,),
        in_specs=[pl.BlockSpec((nd, tr, c), lambda i: (0, i, 0)), pl.BlockSpec((1, tr, c), lambda i: (0, i, 0)),
                  pl.BlockSpec(memory_space=pl.ANY)],
        out_specs=pl.BlockSpec((tr, c), lambda i: (i, 0)),
        out_shape=jax.ShapeDtypeStruct((r, c), F32),
        compiler_params=_cp("parallel"),
    )(recv, own, after)


SHARD_KIND = {"a_w_in": "col", "a_w_out": "row", "a_w_mem_kv": "row", "a_ffn_up": "col", "a_ffn_down": "row",
              "w_kv_shared": "col", "b_w_in": "row", "b_w_out": "row", "b_w_mem_kv": "row", "b_ffn_up": "col",
              "b_ffn_down": "row"}
EARLY_WEIGHTS = ("a_w_in", "a_w_mem_kv")
LATE_WEIGHTS = tuple(nm for nm in SHARD_KIND if nm not in EARLY_WEIGHTS)


def _as2d(a):
    return a.reshape(a.shape[-2], a.shape[-1]) if a.ndim >= 2 else a.reshape(1, a.shape[0])


def _pack_local(shards):
    return jnp.concatenate([_as2d(s).astype(BF16).reshape(-1, PACK_COLS) for s in shards], axis=0)


def _unpack_full(gathered, names, shapes):
    out = {}
    r0 = 0
    for name in names:
        rows, cols = shapes[name]
        nr = rows * cols // PACK_COLS
        blk = gathered[:, r0:r0 + nr, :].reshape(N_DEV, rows, cols)
        if SHARD_KIND[name] == "row":
            out[name] = blk.reshape(N_DEV * rows, cols)
        else:
            out[name] = blk.transpose(1, 0, 2).reshape(rows, N_DEV * cols)
        r0 += nr
    return out


def _pack_grads(grads, names, shapes):
    parts = []
    for name in names:
        rows, cols = shapes[name]
        g = grads[name]
        if SHARD_KIND[name] == "row":
            blk = g.reshape(N_DEV, rows, cols)
        else:
            blk = g.reshape(rows, N_DEV, cols).transpose(1, 0, 2)
        parts.append(blk.astype(BF16).reshape(N_DEV, rows * cols // PACK_COLS, PACK_COLS))
    return jnp.concatenate(parts, axis=1)


def _unpack_local(gsum, names, shapes):
    out = {}
    r0 = 0
    for name in names:
        rows, cols = shapes[name]
        nr = rows * cols // PACK_COLS
        out[name] = gsum[r0:r0 + nr].reshape(rows, cols)
        r0 += nr
    return out


def _by_residue(t, d):
    if d == 1:
        return t
    b, s, c = t.shape
    return t.reshape(b, s // d, d, c).transpose(0, 2, 1, 3).reshape(b * d, s // d, c)


def _from_residue(t, d, b):
    if d == 1:
        return t
    n, l, c = t.shape
    return t.reshape(b, d, l, c).transpose(0, 2, 1, 3).reshape(b, l * d, c)


def _alibi_slopes():
    return [2.0 ** (-ALIBI_MAX_BIAS * (i + 1) / N_DIL_HEADS) for i in range(N_DIL_HEADS)]


def _conv_ffn_fwd(xin, gain, w_up, wconv, w_down, tag, b, s):
    (n,), r = _rms_fwd(xin, [gain], name=f"{tag}_rms_ffn")
    u = _mm(n, w_up, name=f"{tag}_up", out_dtype=BF16).reshape(b, s, -1)
    hmid = _ffn_mid_fwd(u, wconv, name=f"{tag}_ffn_mid").reshape(b * s, -1)
    xout = _mm(hmid, w_down, name=f"{tag}_down", out_dtype=F32, res=xin)
    return xout, (n, r, u, hmid)


def _conv_ffn_bwd(dxout, xin, gain, saved, w_up, wconv, w_down, tag, b, s):
    n, r, u, hmid = saved
    f = hmid.shape[1]
    dhmid = _mm(dxout, w_down, name=f"{tag}_d_hmid", out_dtype=BF16, trans_b=True)
    g_down = _mm(hmid, dxout, name=f"{tag}_g_down", out_dtype=BF16, trans_a=True)
    du_a, du_g, gc_a, gc_g = _ffn_mid_bwd(u, wconv, dhmid.reshape(b, s, f), name=f"{tag}_ffn_mid_bwd")
    du_a = du_a.reshape(b * s, f)
    du_g = du_g.reshape(b * s, f)
    dn = _mm(du_a, w_up[:, :f], name=f"{tag}_d_n_a", out_dtype=F32, trans_b=True)
    dn = _mm(du_g, w_up[:, f:], name=f"{tag}_d_n_g", out_dtype=F32, res=dn, trans_b=True)
    g_up = jnp.concatenate([_mm(n, du_a, name=f"{tag}_g_up_a", out_dtype=BF16, trans_a=True),
                            _mm(n, du_g, name=f"{tag}_g_up_g", out_dtype=BF16, trans_a=True)], axis=1)
    dxin, (g_gain,) = _rms_bwd(xin, r, [(dn, gain)], dxout, name=f"{tag}_rms_ffn_bwd")
    return dxin, g_up, g_down, jnp.concatenate([gc_a, gc_g], axis=1), g_gain


def _mem_kv_fwd(mem2d, gain, w_mem_kv, tag, b):
    (nm,), rm = _rms_fwd(mem2d, [gain], name=f"{tag}_rms_mem")
    kvm = _mm(nm, w_mem_kv, name=f"{tag}_mem_kv", out_dtype=BF16)
    return kvm.reshape(b, -1, 2 * MEM_WIDTH), (nm, rm)


def _mem_kv_bwd(dk, dv, mem2d, gain, saved, w_mem_kv, tag):
    nm, rm = saved
    dkvm = jnp.concatenate([dk, dv], axis=-1).reshape(-1, 2 * MEM_WIDTH)
    dnm = _mm(dkvm, w_mem_kv, name=f"{tag}_d_nm", out_dtype=F32, trans_b=True)
    g_w = _mm(nm, dkvm, name=f"{tag}_g_mem_kv", out_dtype=BF16, trans_a=True)
    _, (g_gain,) = _rms_bwd(mem2d, rm, [(dnm, gain)], None, name=f"{tag}_rms_mem_bwd", need_dx=False)
    return g_w, g_gain


def kernel(x, mem, a_norm_attn, a_w_in, a_w_out, a_norm_mem, a_w_mem_kv, a_norm_ffn, a_ffn_up, a_ffn_conv, a_ffn_down, kv_norm, w_kv_shared, b_norm_attn, b_w_in, b_w_out, b_norm_mem, b_w_mem_kv, b_norm_ffn, b_ffn_up, b_ffn_conv, b_ffn_down, final_norm, loss_target, m_a_norm_attn, m_a_w_in, m_a_w_out, m_a_norm_mem, m_a_w_mem_kv, m_a_norm_ffn, m_a_ffn_up, m_a_ffn_conv, m_a_ffn_down, m_kv_norm, m_w_kv_shared, m_b_norm_attn, m_b_w_in, m_b_w_out, m_b_norm_mem, m_b_w_mem_kv, m_b_norm_ffn, m_b_ffn_up, m_b_ffn_conv, m_b_ffn_down, m_final_norm, v_a_norm_attn, v_a_w_in, v_a_w_out, v_a_norm_mem, v_a_w_mem_kv, v_a_norm_ffn, v_a_ffn_up, v_a_ffn_conv, v_a_ffn_down, v_kv_norm, v_w_kv_shared, v_b_norm_attn, v_b_w_in, v_b_w_out, v_b_norm_mem, v_b_w_mem_kv, v_b_norm_ffn, v_b_ffn_up, v_b_ffn_conv, v_b_ffn_down, v_final_norm):
    names = ["a_norm_attn", "a_w_in", "a_w_out", "a_norm_mem", "a_w_mem_kv", "a_norm_ffn", "a_ffn_up",
             "a_ffn_conv", "a_ffn_down", "kv_norm", "w_kv_shared", "b_norm_attn", "b_w_in", "b_w_out",
             "b_norm_mem", "b_w_mem_kv", "b_norm_ffn", "b_ffn_up", "b_ffn_conv", "b_ffn_down", "final_norm"]
    wl = dict(zip(names, [a_norm_attn, a_w_in, a_w_out, a_norm_mem, a_w_mem_kv, a_norm_ffn, a_ffn_up,
                          a_ffn_conv, a_ffn_down, kv_norm, w_kv_shared, b_norm_attn, b_w_in, b_w_out,
                          b_norm_mem, b_w_mem_kv, b_norm_ffn, b_ffn_up, b_ffn_conv, b_ffn_down, final_norm]))
    ml = dict(zip(names, [m_a_norm_attn, m_a_w_in, m_a_w_out, m_a_norm_mem, m_a_w_mem_kv, m_a_norm_ffn,
                          m_a_ffn_up, m_a_ffn_conv, m_a_ffn_down, m_kv_norm, m_w_kv_shared, m_b_norm_attn,
                          m_b_w_in, m_b_w_out, m_b_norm_mem, m_b_w_mem_kv, m_b_norm_ffn, m_b_ffn_up,
                          m_b_ffn_conv, m_b_ffn_down, m_final_norm]))
    vl = dict(zip(names, [v_a_norm_attn, v_a_w_in, v_a_w_out, v_a_norm_mem, v_a_w_mem_kv, v_a_norm_ffn,
                          v_a_ffn_up, v_a_ffn_conv, v_a_ffn_down, v_kv_norm, v_w_kv_shared, v_b_norm_attn,
                          v_b_w_in, v_b_w_out, v_b_norm_mem, v_b_w_mem_kv, v_b_norm_ffn, v_b_ffn_up,
                          v_b_ffn_conv, v_b_ffn_down, v_final_norm]))
    b, s, d = x.shape
    t = b * s
    my_x, my_y, my_c = _mesh_pos()
    me = 4 * my_x + 2 * my_y + my_c

    shapes = {nm: _as2d(wl[nm]).shape for nm in SHARD_KIND}
    early_local = _pack_local([wl[nm] for nm in EARLY_WEIGHTS])
    early_all = _all_gather_hbm(early_local, name="gather_early").reshape(N_DEV, early_local.shape[0], PACK_COLS)
    wf = _unpack_full(early_all, EARLY_WEIGHTS, shapes)
    late_local = _pack_local([wl[nm] for nm in LATE_WEIGHTS])
    late_rows = late_local.shape[0]
    gat_sems, gat_src, gat_land, gat_token = _exchange_start(
        late_local, (N_DEV * late_rows, PACK_COLS), name="gather_late_start", gather=True)

    sharded_small = ["a_norm_attn", "a_norm_mem", "a_norm_ffn", "a_ffn_conv", "b_ffn_conv"]
    small_flat = jnp.concatenate([wl[nm].reshape(-1) for nm in sharded_small])
    n_small = small_flat.shape[0]
    small_rows = -(-n_small // (8 * LANE)) * 8
    small_local = jnp.pad(small_flat, (0, small_rows * LANE - n_small)).reshape(small_rows, LANE)
    small_all = _all_gather_hbm(small_local, name="gather_small").reshape(N_DEV, small_rows * LANE)
    sfull = {}
    r0 = 0
    for nm in sharded_small:
        rows, cols = _as2d(wl[nm]).shape
        blk = small_all[:, r0:r0 + rows * cols].reshape(N_DEV, rows, cols)
        sfull[nm] = blk.transpose(1, 0, 2).reshape(rows, N_DEV * cols)
        r0 += rows * cols
    gain = {nm: sfull[nm] for nm in ("a_norm_attn", "a_norm_mem", "a_norm_ffn")}
    for nm in ("kv_norm", "b_norm_attn", "b_norm_mem", "b_norm_ffn", "final_norm"):
        gain[nm] = _as2d(wl[nm])
    conv_a, conv_b = sfull["a_ffn_conv"], sfull["b_ffn_conv"]

    x2d = x.reshape(t, d)
    mem2d = mem.reshape(-1, d)
    tgt2d = loss_target.reshape(t, d)
    qmem_blk_a = 3 * SB_WIDTH // GRP
    qmem_blk_b = DIL_WIDTH // GRP

    (n1,), r1 = _rms_fwd(x2d, [gain["a_norm_attn"]], name="a_rms_attn")
    proj_a = _mm(n1, wf["a_w_in"], name="a_in", out_dtype=BF16).reshape(b, s, -1)
    kvm_a, mem_saved_a = _mem_kv_fwd(mem2d, gain["a_norm_mem"], wf["a_w_mem_kv"], "a", b)
    o_sb, rsum = _sb_fwd(proj_a, gat_token, name="a_sb_fwd")
    o_mem_a, lse_mem_a = _attn_fwd(proj_a, kvm_a, kvm_a, name="a_mem_fwd", banded=False,
                                   q_lane_blk=qmem_blk_a, k_lane_blk=0, v_lane_blk=1)
    late_land = _exchange_wait(gat_sems, gat_src, gat_land, rsum, name="gather_late_wait", gather=True)
    late_all = lax.dynamic_update_slice(late_land, late_local, (me * late_rows, 0))
    wf.update(_unpack_full(late_all.reshape(N_DEV, late_rows, PACK_COLS), LATE_WEIGHTS, shapes))
    cat_a = jnp.concatenate([o_sb, o_mem_a], axis=-1).reshape(t, d)
    x1 = _mm(cat_a, wf["a_w_out"], name="a_out", out_dtype=F32, res=x2d)
    xa, ffn_saved_a = _conv_ffn_fwd(x1, gain["a_norm_ffn"], wf["a_ffn_up"], conv_a, wf["a_ffn_down"], "a", b, s)

    (nk, n3), r3 = _rms_fwd(xa, [gain["kv_norm"], gain["b_norm_attn"]], name="b_rms_attn")
    kvsh = _mm(nk, wf["w_kv_shared"], name="kv_shared", out_dtype=BF16).reshape(b, s, -1)
    proj_b = _mm(n3, wf["b_w_in"], name="b_in", out_dtype=BF16).reshape(b, s, -1)
    kvm_b, mem_saved_b = _mem_kv_fwd(mem2d, gain["b_norm_mem"], wf["b_w_mem_kv"], "b", b)
    slopes = _alibi_slopes()
    dil_q, dil_k, dil_v, dil_o, dil_lse, dil_slopes = [], [], [], [], [], []
    for g, (_, dil) in enumerate(DIL_GROUPS):
        qg = _by_residue(proj_b[:, :, GRP * g:GRP * (g + 1)], dil)
        kg = _by_residue(kvsh[:, :, GRP * g:GRP * (g + 1)], dil)
        vg = _by_residue(kvsh[:, :, DIL_WIDTH + GRP * g:DIL_WIDTH + GRP * (g + 1)], dil)
        sl = [slopes[4 * g + h] * dil for h in range(4)]
        og, lg = _attn_fwd(qg, kg, vg, name=f"b_dil{g}_fwd", banded=True, slopes_scaled=sl)
        dil_q.append(qg)
        dil_k.append(kg)
        dil_v.append(vg)
        dil_slopes.append(sl)
        dil_o.append(_from_residue(og, dil, b).reshape(t, GRP))
        dil_lse.append(_from_residue(lg, dil, b).reshape(t, PER_HEAD))
    o_dil, lse_joint = _dil_combine(dil_o, dil_lse, name="b_dil_combine")
    o_mem_b, lse_mem_b = _attn_fwd(proj_b, kvm_b, kvm_b, name="b_mem_fwd", banded=False,
                                   q_lane_blk=qmem_blk_b, k_lane_blk=0, v_lane_blk=1)
    cat_b = jnp.concatenate([o_dil, o_mem_b.reshape(t, MEM_WIDTH)], axis=-1)
    x3 = _mm(cat_b, wf["b_w_out"], name="b_out", out_dtype=F32, res=xa)
    xb, ffn_saved_b = _conv_ffn_fwd(x3, gain["b_norm_ffn"], wf["b_ffn_up"], conv_b, wf["b_ffn_down"], "b", b, s)

    dxb, g_final, loss_vec = _loss_head(xb, gain["final_norm"], tgt2d, name="loss_head")

    grads = {}
    sgrads = {"final_norm": g_final}
    dx3, grads["b_ffn_up"], grads["b_ffn_down"], sgrads["b_ffn_conv"], sgrads["b_norm_ffn"] = _conv_ffn_bwd(
        dxb, x3, gain["b_norm_ffn"], ffn_saved_b, wf["b_ffn_up"], conv_b, wf["b_ffn_down"], "b", b, s)
    dcat_b = _mm(dx3, wf["b_w_out"], name="b_d_cat", out_dtype=BF16, trans_b=True)
    grads["b_w_out"] = _mm(cat_b, dx3, name="b_g_out", out_dtype=BF16, trans_a=True)
    dcat_b3 = dcat_b.reshape(b, s, d)
    delta_mem_b = _attn_delta(dcat_b, cat_b, name="b_mem_delta", lane_blks=[qmem_blk_b]).reshape(b, s, PER_HEAD)
    dq_mem_b, dkm_b, dvm_b = _attn_bwd(proj_b, kvm_b, kvm_b, dcat_b3, lse_mem_b, delta_mem_b, name="b_mem_bwd",
                                       banded=False, q_lane_blk=qmem_blk_b, k_lane_blk=0, v_lane_blk=1,
                                       do_lane_blk=qmem_blk_b)
    delta_dil = _attn_delta(dcat_b, cat_b, name="b_dil_delta", lane_blks=[0, 1, 2]).reshape(b, s, PER_HEAD)
    lse_joint3 = lse_joint.reshape(b, s, PER_HEAD)
    dq_parts, dk_parts, dv_parts = [], [], []
    for g, (_, dil) in enumerate(DIL_GROUPS):
        dog = _by_residue(dcat_b3[:, :, GRP * g:GRP * (g + 1)], dil)
        lg = _by_residue(lse_joint3, dil)
        dg = _by_residue(delta_dil, dil)
        dqg, dkg, dvg = _attn_bwd(dil_q[g], dil_k[g], dil_v[g], dog, lg, dg, name=f"b_dil{g}_bwd", banded=True,
                                  slopes_scaled=dil_slopes[g])
        dq_parts.append(_from_residue(dqg, dil, b))
        dk_parts.append(_from_residue(dkg, dil, b))
        dv_parts.append(_from_residue(dvg, dil, b))
    dproj_b = jnp.concatenate(dq_parts + [dq_mem_b], axis=-1).reshape(t, d)
    dn3 = _mm(dproj_b, wf["b_w_in"], name="b_d_n", out_dtype=F32, trans_b=True)
    grads["b_w_in"] = _mm(n3, dproj_b, name="b_g_in", out_dtype=BF16, trans_a=True)
    grads["b_w_mem_kv"], sgrads["b_norm_mem"] = _mem_kv_bwd(dkm_b, dvm_b, mem2d, gain["b_norm_mem"], mem_saved_b,
                                                           wf["b_w_mem_kv"], "b")
    dkvsh = jnp.concatenate(dk_parts + dv_parts, axis=-1).reshape(t, 2 * DIL_WIDTH).astype(BF16)
    dnk = _mm(dkvsh, wf["w_kv_shared"], name="kv_d_n", out_dtype=F32, trans_b=True)
    grads["w_kv_shared"] = _mm(nk, dkvsh, name="kv_g", out_dtype=BF16, trans_a=True)
    dxa, (sgrads["kv_norm"], sgrads["b_norm_attn"]) = _rms_bwd(
        xa, r3, [(dnk, gain["kv_norm"]), (dn3, gain["b_norm_attn"])], dx3, name="b_rms_attn_bwd")

    dx1, grads["a_ffn_up"], grads["a_ffn_down"], sgrads["a_ffn_conv"], sgrads["a_norm_ffn"] = _conv_ffn_bwd(
        dxa, x1, gain["a_norm_ffn"], ffn_saved_a, wf["a_ffn_up"], conv_a, wf["a_ffn_down"], "a", b, s)
    dcat_a = _mm(dx1, wf["a_w_out"], name="a_d_cat", out_dtype=BF16, trans_b=True)
    grads["a_w_out"] = _mm(cat_a, dx1, name="a_g_out", out_dtype=BF16, trans_a=True)
    dcat_a3 = dcat_a.reshape(b, s, d)
    delta_mem_a = _attn_delta(dcat_a, cat_a, name="a_mem_delta", lane_blks=[qmem_blk_b]).reshape(b, s, PER_HEAD)
    dq_mem_a, dkm_a, dvm_a = _attn_bwd(proj_a, kvm_a, kvm_a, dcat_a3, lse_mem_a, delta_mem_a, name="a_mem_bwd",
                                       banded=False, q_lane_blk=qmem_blk_a, k_lane_blk=0, v_lane_blk=1,
                                       do_lane_blk=qmem_blk_b)
    late_grads = _pack_grads(grads, LATE_WEIGHTS, shapes)
    ex_sems, ex_src, ex_land, ex_token = _exchange_start(late_grads, late_grads.shape, name="grads_late_start",
                                                         gather=False)
    dq_sb, dk_sb, dv_sb = _sb_bwd(proj_a, dcat_a3, rsum, ex_token, name="a_sb_bwd")
    dproj_a = jnp.concatenate([dq_sb, dk_sb.astype(BF16), dv_sb.astype(BF16), dq_mem_a], axis=-1).reshape(t, -1)
    dn1 = _mm(dproj_a, wf["a_w_in"], name="a_d_n", out_dtype=F32, trans_b=True)
    grads["a_w_in"] = _mm(n1, dproj_a, name="a_g_in", out_dtype=BF16, trans_a=True)
    grads["a_w_mem_kv"], sgrads["a_norm_mem"] = _mem_kv_bwd(dkm_a, dvm_a, mem2d, gain["a_norm_mem"], mem_saved_a,
                                                           wf["a_w_mem_kv"], "a")
    early_grads = _pack_grads(grads, EARLY_WEIGHTS, shapes)
    ee_sems, ee_src, ee_land, ee_token = _exchange_start(early_grads, early_grads.shape, name="grads_early_start",
                                                  gather=False)
    dx0, (sgrads["a_norm_attn"],) = _rms_bwd(x2d, r1, [(dn1, gain["a_norm_attn"])], dx1, name="a_rms_attn_bwd")
    grad_x = dx0.reshape(b, s, d)

    late_recv = _exchange_wait(ex_sems, ex_src, ex_land, dx0, name="grads_late_wait", gather=False)
    own_late = lax.dynamic_slice(late_grads, (me, 0, 0), (1,) + late_grads.shape[1:])
    gl = _unpack_local(_sum_blocks(late_recv, own_late, ee_token, name="sum_grads_late"), LATE_WEIGHTS, shapes)

    small_names = ["a_norm_attn", "a_norm_mem", "a_norm_ffn", "kv_norm", "b_norm_attn", "b_norm_mem",
                   "b_norm_ffn", "final_norm", "a_ffn_conv", "b_ffn_conv"]
    small_flat = jnp.concatenate([sgrads[nm].reshape(-1) for nm in small_names] + [loss_vec.reshape(-1)])
    n_flat = small_flat.shape[0]
    red_rows = -(-n_flat // (8 * PACK_COLS)) * 8
    small_pack = jnp.pad(small_flat, (0, red_rows * PACK_COLS - n_flat)).reshape(red_rows, PACK_COLS)
    small_sum = _all_reduce_small(small_pack, name="reduce_small").reshape(-1)
    r0 = 0
    for nm in small_names:
        rows, cols = sgrads[nm].shape
        full = small_sum[r0:r0 + rows * cols].reshape(rows, cols)
        r0 += rows * cols
        if nm in sharded_small:
            lc = cols // N_DEV
            gl[nm] = lax.dynamic_slice(full, (0, me * lc), (rows, lc))
        else:
            gl[nm] = full
    loss = (0.5 / d) * jnp.sum(small_sum[r0:r0 + d])

    upd = {}
    for nm in LATE_WEIGHTS:
        upd[nm] = _adam(_as2d(wl[nm]), gl[nm], _as2d(ml[nm]), _as2d(vl[nm]), name=f"adam_{nm}")
    res_small = _adam_small([(_as2d(wl[nm]), gl[nm], _as2d(ml[nm]), _as2d(vl[nm])) for nm in small_names],
                            name="adam_small")
    for nm, r in zip(small_names, res_small):
        upd[nm] = r
    early_recv = _exchange_wait(ee_sems, ee_src, ee_land, upd[LATE_WEIGHTS[-1]][0], name="grads_early_wait",
                                gather=False)
    own_early = lax.dynamic_slice(early_grads, (me, 0, 0), (1,) + early_grads.shape[1:])
    gl.update(_unpack_local(_sum_blocks(early_recv, own_early, ee_token, name="sum_grads_early"), EARLY_WEIGHTS,
                            shapes))
    for nm in EARLY_WEIGHTS:
        upd[nm] = _adam(_as2d(wl[nm]), gl[nm], _as2d(ml[nm]), _as2d(vl[nm]), name=f"adam_{nm}")

    g_out = [gl[nm].reshape(wl[nm].shape) for nm in names]
    d_out = [upd[nm][0].reshape(wl[nm].shape) for nm in names]
    m_out = [upd[nm][1].reshape(wl[nm].shape) for nm in names]
    v_out = [upd[nm][2].reshape(wl[nm].shape) for nm in names]
    return (loss, grad_x, *g_out, *d_out, *m_out, *v_out)
```

```python
import functools
import math

import jax
import jax.numpy as jnp
from jax import lax
from jax.experimental import pallas as pl
from jax.experimental.pallas import tpu as pltpu

F32 = jnp.float32
BF16 = jnp.bfloat16

N_DEV = 8
HEAD_DIM = 64
N_SB_HEADS = 12
N_DIL_HEADS = 12
DIL_GROUPS = ((128, 1), (512, 4), (2048, 16))
SB_WIDTH = N_SB_HEADS * HEAD_DIM
MEM_WIDTH = 256
DIL_WIDTH = N_DIL_HEADS * HEAD_DIM
ATT_SCALE = HEAD_DIM ** -0.5
EPS = 1e-6
ALIBI_MAX_BIAS = 8.0
NEG_BIG = -1e30

ADAM_LR = 0.001
ADAM_B1 = 0.9
ADAM_B2 = 0.999
ADAM_EPS = 1e-08
ADAM_WD = 0.01
ADAM_STEP = 10

LANE = 128
QBLK = 128
VMEM_LIMIT_BYTES = 48 * 1024 * 1024
PACK_COLS = 1024
MESH_ID = pl.DeviceIdType.MESH


def _cp(*sem):
    return pltpu.CompilerParams(dimension_semantics=sem, vmem_limit_bytes=VMEM_LIMIT_BYTES)


def _pick(n, cands):
    for c in cands:
        if n % c == 0:
            return c
    raise ValueError(f"no tile for {n} in {cands}")


def _dot(a, b):
    return jnp.dot(a, b, preferred_element_type=F32)


def _dot_nt(a, b):
    return lax.dot_general(a, b, (((1,), (1,)), ((), ())), preferred_element_type=F32)


def _dot_tn(a, b):
    return lax.dot_general(a, b, (((0,), (0,)), ((), ())), preferred_element_type=F32)


def _dot_split(x, u):
    hi = x.astype(BF16)
    lo = (x - hi.astype(F32)).astype(BF16)
    return _dot(hi, u) + _dot(lo, u)


def _mm(a, b, *, name, out_dtype, res=None, trans_a=False, trans_b=False):
    assert not (trans_a and trans_b)
    if trans_a:
        kdim, m = a.shape
    else:
        m, kdim = a.shape
    if trans_b:
        n, kb = b.shape
    else:
        kb, n = b.shape
    assert kb == kdim, (a.shape, b.shape)
    if trans_a:
        tm = _pick(m, (1408, 1024, 512, 256, 128))
        tn = _pick(n, (1024, 1280, 1408, 768, 512, 256, 128))
        tk = _pick(kdim, (1024, 512, 256))
    else:
        tm = _pick(m, (1024, 512, 256, 128))
        tk = kdim if kdim <= 2816 else _pick(kdim, (2048, 1536, 1408, 1280, 1024, 512))
        tn = _pick(n, (512, 256, 128) if tk > 2048 else (1408, 1280, 1024, 768, 512, 256, 128))
    nk = kdim // tk
    has_res = res is not None

    def body(*refs):
        if has_res:
            a_ref, b_ref, r_ref, o_ref = refs[:4]
            scr = refs[4:]
        else:
            a_ref, b_ref, o_ref = refs[:3]
            r_ref = None
            scr = refs[3:]
        av = a_ref[...].astype(BF16)
        bv = b_ref[...].astype(BF16)
        if trans_a:
            p = _dot_tn(av, bv)
        elif trans_b:
            p = _dot_nt(av, bv)
        else:
            p = _dot(av, bv)

        def finish(acc):
            if has_res:
                acc = acc + r_ref[...]
            o_ref[...] = acc.astype(o_ref.dtype)

        if nk == 1:
            finish(p)
        else:
            acc_ref = scr[0]
            k = pl.program_id(2)

            @pl.when(k == 0)
            def _():
                acc_ref[...] = p

            @pl.when(k > 0)
            def _():
                acc_ref[...] += p

            @pl.when(k == nk - 1)
            def _():
                finish(acc_ref[...])

    if trans_a:
        a_spec = pl.BlockSpec((tk, tm), lambda i, j, k: (k, i))
    else:
        a_spec = pl.BlockSpec((tm, tk), lambda i, j, k: (i, k))
    if trans_b:
        b_spec = pl.BlockSpec((tn, tk), lambda i, j, k: (j, k))
    else:
        b_spec = pl.BlockSpec((tk, tn), lambda i, j, k: (k, j))
    in_specs = [a_spec, b_spec]
    args = [a, b]
    if has_res:
        in_specs.append(pl.BlockSpec((tm, tn), lambda i, j, k: (i, j)))
        args.append(res)
    return pl.pallas_call(
        body, name=name,
        grid=(m // tm, n // tn, nk),
        in_specs=in_specs,
        out_specs=pl.BlockSpec((tm, tn), lambda i, j, k: (i, j)),
        out_shape=jax.ShapeDtypeStruct((m, n), out_dtype),
        scratch_shapes=[pltpu.VMEM((tm, tn), F32)] if nk > 1 else [],
        compiler_params=_cp("parallel", "parallel", "arbitrary"),
    )(*args)


def _rms_fwd(x, gains, *, name):
    t, d = x.shape
    tr = _pick(t, (512, 256, 128, 8))
    ng = len(gains)

    def body(x_ref, *rest):
        g_refs, n_refs, r_ref = rest[:ng], rest[ng:2 * ng], rest[2 * ng]
        xv = x_ref[...]
        r = lax.rsqrt(jnp.mean(xv * xv, axis=-1, keepdims=True) + EPS)
        xh = xv * r
        for g_ref, n_ref in zip(g_refs, n_refs):
            n_ref[...] = (xh * g_ref[...]).astype(BF16)
        r_ref[...] = r

    row = pl.BlockSpec((tr, d), lambda i: (i, 0))
    gsp = pl.BlockSpec((1, d), lambda i: (0, 0))
    outs = pl.pallas_call(
        body, name=name, grid=(t // tr,),
        in_specs=[row] + [gsp] * ng,
        out_specs=[row] * ng + [pl.BlockSpec((tr, 1), lambda i: (i, 0))],
        out_shape=[jax.ShapeDtypeStruct((t, d), BF16)] * ng + [jax.ShapeDtypeStruct((t, 1), F32)],
        compiler_params=_cp("parallel"),
    )(x, *gains)
    return list(outs[:ng]), outs[ng]


def _rms_bwd(x, r, pairs, dres, *, name, need_dx=True):
    t, d = x.shape
    tr = _pick(t, (512, 256, 128, 8))
    npair = len(pairs)
    has_res = dres is not None

    def body(*refs):
        x_ref, r_ref = refs[:2]
        pr = refs[2:2 + 2 * npair]
        pos = 2 + 2 * npair
        res_ref = None
        if has_res:
            res_ref = refs[pos]
            pos += 1
        dx_ref = None
        if need_dx:
            dx_ref = refs[pos]
            pos += 1
        dg_refs = refs[pos:pos + npair]
        i = pl.program_id(0)
        rv = r_ref[...]
        xh = x_ref[...] * rv
        dx = res_ref[...] if has_res else None
        for k in range(npair):
            dn = pr[2 * k][...].astype(F32)
            g = pr[2 * k + 1][...]
            part = jnp.sum(dn * xh, axis=0, keepdims=True)

            @pl.when(i == 0)
            def _():
                dg_refs[k][...] = part

            @pl.when(i > 0)
            def _():
                dg_refs[k][...] += part

            if need_dx:
                dxh = dn * g
                c = jnp.mean(dxh * xh, axis=-1, keepdims=True)
                term = rv * (dxh - xh * c)
                dx = term if dx is None else dx + term
        if need_dx:
            dx_ref[...] = dx

    row = pl.BlockSpec((tr, d), lambda i: (i, 0))
    gsp = pl.BlockSpec((1, d), lambda i: (0, 0))
    in_specs = [row, pl.BlockSpec((tr, 1), lambda i: (i, 0))]
    args = [x, r]
    for dn, g in pairs:
        in_specs += [row, gsp]
        args += [dn, g]
    if has_res:
        in_specs.append(row)
        args.append(dres)
    out_specs, out_shape = [], []
    if need_dx:
        out_specs.append(row)
        out_shape.append(jax.ShapeDtypeStruct((t, d), F32))
    out_specs += [gsp] * npair
    out_shape += [jax.ShapeDtypeStruct((1, d), F32)] * npair
    outs = pl.pallas_call(
        body, name=name, grid=(t // tr,), in_specs=in_specs, out_specs=out_specs, out_shape=out_shape,
        compiler_params=_cp("arbitrary"),
    )(*args)
    if need_dx:
        return outs[0], list(outs[1:])
    return None, list(outs)


def _loss_head(h, g, tgt, *, name):
    t, d = h.shape
    tr = _pick(t, (512, 256, 128, 8))

    def body(h_ref, g_ref, t_ref, dh_ref, dg_ref, l_ref):
        i = pl.program_id(0)
        xv = h_ref[...]
        gv = g_ref[...]
        r = lax.rsqrt(jnp.mean(xv * xv, axis=-1, keepdims=True) + EPS)
        xh = xv * r
        e = xh * gv - t_ref[...]
        dy = e * (1.0 / d)
        lpart = jnp.sum(e * e, axis=0, keepdims=True)
        gpart = jnp.sum(dy * xh, axis=0, keepdims=True)

        @pl.when(i == 0)
        def _():
            l_ref[...] = lpart
            dg_ref[...] = gpart

        @pl.when(i > 0)
        def _():
            l_ref[...] += lpart
            dg_ref[...] += gpart

        dxh = dy * gv
        c = jnp.mean(dxh * xh, axis=-1, keepdims=True)
        dh_ref[...] = r * (dxh - xh * c)

    row = pl.BlockSpec((tr, d), lambda i: (i, 0))
    gsp = pl.BlockSpec((1, d), lambda i: (0, 0))
    return pl.pallas_call(
        body, name=name, grid=(t // tr,), in_specs=[row, gsp, row], out_specs=[row, gsp, gsp],
        out_shape=[jax.ShapeDtypeStruct((t, d), F32), jax.ShapeDtypeStruct((1, d), F32),
                   jax.ShapeDtypeStruct((1, d), F32)],
        compiler_params=_cp("arbitrary"),
    )(h, g, tgt)


GRP = 4 * HEAD_DIM
SB_KB = 2 * QBLK
SB_QB = 2 * QBLK


def _head_masks4(shape):
    lane = lax.broadcasted_iota(jnp.int32, shape, 1)
    return [(lane >= HEAD_DIM * h) & (lane < HEAD_DIM * (h + 1)) for h in range(4)]


def _neg_softplus(z):
    return jnp.minimum(-z, 0.0) - jnp.log(1.0 + jnp.exp(-jnp.abs(z)))


def _stacked_col_minus_row():
    rowi = lax.broadcasted_iota(jnp.int32, (4 * SB_QB, SB_KB), 0)
    coli = lax.broadcasted_iota(jnp.int32, (4 * SB_QB, SB_KB), 1)
    return coli - (rowi & (SB_QB - 1))


def _sb_fwd(proj, after, *, name):
    b, s, _ = proj.shape
    nq = s // SB_QB
    ngrp = SB_WIDTH // GRP

    def body(q_ref, k_ref, v_ref, after_ref, o_ref, r_ref, acc_ref, car_ref):
        i = pl.program_id(2)
        masks = _head_masks4((SB_QB, GRP))
        row = lax.broadcasted_iota(jnp.int32, (SB_KB, SB_KB), 0)
        col = lax.broadcasted_iota(jnp.int32, (SB_KB, SB_KB), 1)
        later_mat = (row > col).astype(BF16)
        col_minus_row = _stacked_col_minus_row()
        qs = q_ref[0] * jnp.asarray(ATT_SCALE, BF16)
        q_stack = jnp.concatenate([jnp.where(mk, qs, jnp.zeros_like(qs)) for mk in masks], axis=0)
        acc_ref[...] = jnp.zeros_like(acc_ref)
        car_ref[...] = jnp.zeros_like(car_ref)

        def process(jb, masked):
            off = pl.multiple_of(jb * SB_KB, SB_KB)
            k2 = k_ref[0, pl.ds(off, SB_KB), :]
            v2 = v_ref[0, pl.ds(off, SB_KB), :]
            z = _dot_nt(q_stack, k2)
            ls = _neg_softplus(z)
            if masked:
                causal = col_minus_row < (i * SB_QB - jb * SB_KB)
                ls = jnp.where(causal, ls, 0.0)
            later = _dot(ls.astype(BF16), later_mat)
            car = car_ref[...]
            w = jnp.exp((z + ls) + later + car)
            if masked:
                w = jnp.where(causal, w, 0.0)
            car_ref[...] = car + jnp.sum(ls, axis=1, keepdims=True)
            acc_ref[...] += _dot(w.astype(BF16), v2)

        top = (i * SB_QB) // SB_KB
        process(top, True)

        def step(jj, carry):
            process(top - 1 - jj, False)
            return carry

        lax.fori_loop(0, top, step, 0)
        o = acc_ref[pl.ds(0, SB_QB), :]
        r = car_ref[pl.ds(0, SB_QB), :]
        for h in range(1, 4):
            o = jnp.where(masks[h], acc_ref[pl.ds(h * SB_QB, SB_QB), :], o)
            r = jnp.where(masks[h], car_ref[pl.ds(h * SB_QB, SB_QB), :], r)
        o_ref[0] = o.astype(o_ref.dtype)
        r_ref[0] = r

    blk = pl.BlockSpec((1, SB_QB, GRP), lambda bb, p, i: (bb, i, p))
    return pl.pallas_call(
        body, name=name, grid=(b, ngrp, nq),
        in_specs=[blk,
                  pl.BlockSpec((1, s, GRP), lambda bb, p, i: (bb, 0, ngrp + p)),
                  pl.BlockSpec((1, s, GRP), lambda bb, p, i: (bb, 0, 2 * ngrp + p)),
                  pl.BlockSpec(memory_space=pl.ANY)],
        out_specs=[blk, blk],
        out_shape=[jax.ShapeDtypeStruct((b, s, SB_WIDTH), BF16), jax.ShapeDtypeStruct((b, s, SB_WIDTH), F32)],
        scratch_shapes=[pltpu.VMEM((4 * SB_QB, GRP), F32), pltpu.VMEM((4 * SB_QB, SB_KB), F32)],
        compiler_params=_cp("parallel", "parallel", "arbitrary"),
    )(proj, proj, proj, after)


def _sb_bwd(proj, dcat, rsum, after, *, name):
    b, s, _ = proj.shape
    nq = s // SB_QB
    ngrp = SB_WIDTH // GRP

    def body(q_ref, k_ref, v_ref, do_ref, r_ref, after_ref, dq_ref, dk_out, dv_out, dq_acc, cp_ref, cg_ref,
             dk_ref, dv_ref):
        i = pl.program_id(2)

        @pl.when(i == 0)
        def _():
            dk_ref[...] = jnp.zeros_like(dk_ref)
            dv_ref[...] = jnp.zeros_like(dv_ref)

        masks = _head_masks4((SB_QB, GRP))
        row = lax.broadcasted_iota(jnp.int32, (SB_KB, SB_KB), 0)
        col = lax.broadcasted_iota(jnp.int32, (SB_KB, SB_KB), 1)
        later_mat = (row > col).astype(BF16)
        excl_mat = (row < col).astype(BF16)
        col_minus_row = _stacked_col_minus_row()
        qs = q_ref[0] * jnp.asarray(ATT_SCALE, BF16)
        do = do_ref[0]
        q_stack = jnp.concatenate([jnp.where(mk, qs, jnp.zeros_like(qs)) for mk in masks], axis=0)
        do_stack = jnp.concatenate([jnp.where(mk, do, jnp.zeros_like(do)) for mk in masks], axis=0)
        rv = r_ref[0]
        r_stack = jnp.concatenate([rv[:, HEAD_DIM * h:HEAD_DIM * h + 1] for h in range(4)], axis=0)
        dq_acc[...] = jnp.zeros_like(dq_acc)
        cp_ref[...] = jnp.zeros_like(cp_ref)
        cg_ref[...] = jnp.zeros_like(cg_ref)

        def process(jb, masked):
            off = pl.multiple_of(jb * SB_KB, SB_KB)
            k2 = k_ref[0, pl.ds(off, SB_KB), :]
            v2 = v_ref[0, pl.ds(off, SB_KB), :]
            z = _dot_nt(q_stack, k2)
            dw = _dot_nt(do_stack, v2)
            ls = _neg_softplus(z)
            lsig = z + ls
            if masked:
                causal = col_minus_row < (i * SB_QB - jb * SB_KB)
                ls = jnp.where(causal, ls, 0.0)
            later = _dot(ls.astype(BF16), later_mat)
            cpv = cp_ref[...] + jnp.sum(ls, axis=1, keepdims=True)
            cp_ref[...] = cpv
            w = jnp.exp(lsig + ((r_stack - cpv) + later))
            if masked:
                w = jnp.where(causal, w, 0.0)
            g = dw * w
            gpre = _dot(g.astype(BF16), excl_mat)
            cgv = cg_ref[...]
            cg_ref[...] = cgv + jnp.sum(g, axis=1, keepdims=True)
            dz = g - jnp.exp(lsig) * (g + (gpre + cgv))
            if masked:
                dz = jnp.where(causal, dz, 0.0)
            dzb = dz.astype(BF16)
            dq_acc[...] += _dot(dzb, k2)
            dk_ref[0, pl.ds(off, SB_KB), :] += _dot_tn(dzb, q_stack)
            dv_ref[0, pl.ds(off, SB_KB), :] += _dot_tn(w.astype(BF16), do_stack)

        top = (i * SB_QB) // SB_KB

        def step(jb, carry):
            process(jb, False)
            return carry

        lax.fori_loop(0, top, step, 0)
        process(top, True)
        dq = dq_acc[pl.ds(0, SB_QB), :]
        for h in range(1, 4):
            dq = jnp.where(masks[h], dq_acc[pl.ds(h * SB_QB, SB_QB), :], dq)
        dq_ref[0] = (dq * ATT_SCALE).astype(dq_ref.dtype)

        @pl.when(i == nq - 1)
        def _():
            dk_out[...] = dk_ref[...].astype(dk_out.dtype)
            dv_out[...] = dv_ref[...].astype(dv_out.dtype)

    blk = pl.BlockSpec((1, SB_QB, GRP), lambda bb, p, i: (bb, i, p))
    seq = pl.BlockSpec((1, s, GRP), lambda bb, p, i: (bb, 0, p))
    return pl.pallas_call(
        body, name=name, grid=(b, ngrp, nq),
        in_specs=[blk,
                  pl.BlockSpec((1, s, GRP), lambda bb, p, i: (bb, 0, ngrp + p)),
                  pl.BlockSpec((1, s, GRP), lambda bb, p, i: (bb, 0, 2 * ngrp + p)),
                  blk, blk, pl.BlockSpec(memory_space=pl.ANY)],
        out_specs=[blk, seq, seq],
        out_shape=[jax.ShapeDtypeStruct((b, s, SB_WIDTH), BF16)] * 3,
        scratch_shapes=[pltpu.VMEM((4 * SB_QB, GRP), F32), pltpu.VMEM((4 * SB_QB, SB_KB), F32),
                        pltpu.VMEM((4 * SB_QB, SB_KB), F32), pltpu.VMEM((1, s, GRP), F32),
                        pltpu.VMEM((1, s, GRP), F32)],
        compiler_params=_cp("parallel", "parallel", "arbitrary"),
    )(proj, proj, proj, dcat, rsum, after)


def _band_bias(slopes_scaled):
    a = lax.broadcasted_iota(jnp.int32, (QBLK, 2 * QBLK), 0)
    bcol = lax.broadcasted_iota(jnp.int32, (QBLK, 2 * QBLK), 1)
    delta = a + QBLK - bcol
    in_band = (delta >= 0) & (delta <= QBLK)
    dist = delta.astype(F32)
    bias = jnp.concatenate([(-sl) * dist for sl in slopes_scaled], axis=0)
    return jnp.concatenate([in_band] * 4, axis=0), jnp.concatenate([bcol >= QBLK] * 4, axis=0), bias


def _stack_heads(x, masks):
    return jnp.concatenate([jnp.where(mk, x, jnp.zeros_like(x)) for mk in masks], axis=0)


def _unstack_heads(x, masks):
    out = jnp.broadcast_to(x[0:QBLK], (QBLK, GRP))
    for h in range(1, 4):
        out = jnp.where(masks[h], x[h * QBLK:(h + 1) * QBLK], out)
    return out


PER_HEAD = 8


def _head_column(x):
    return jnp.concatenate([x[:, h:h + 1] for h in range(4)], axis=0)


def _head_lanes(col):
    lane = lax.broadcasted_iota(jnp.int32, (QBLK, PER_HEAD), 1)
    out = jnp.zeros((QBLK, PER_HEAD), F32)
    for h in range(4):
        out = jnp.where(lane == h, col[h * QBLK:(h + 1) * QBLK], out)
    return out


def _spread_heads(x8, rows):
    masks = _head_masks4((rows, GRP))
    out = jnp.broadcast_to(x8[:, 0:1], (rows, GRP))
    for h in range(1, 4):
        out = jnp.where(masks[h], x8[:, h:h + 1], out)
    return out


def _attn_specs(banded, nsub, q_lane_blk, k_lane_blk, v_lane_blk):
    tq = nsub * QBLK
    qs = pl.BlockSpec((1, tq, GRP), lambda n, i: (n, i, q_lane_blk))
    if banded:
        ks = [pl.BlockSpec((1, QBLK, GRP), lambda n, i: (n, jnp.maximum(nsub * i - 1, 0), k_lane_blk)),
              pl.BlockSpec((1, tq, GRP), lambda n, i: (n, i, k_lane_blk))]
        vs = [pl.BlockSpec((1, QBLK, GRP), lambda n, i: (n, jnp.maximum(nsub * i - 1, 0), v_lane_blk)),
              pl.BlockSpec((1, tq, GRP), lambda n, i: (n, i, v_lane_blk))]
    else:
        ks = [pl.BlockSpec((1, 2 * QBLK, GRP), lambda n, i: (n, 0, k_lane_blk))]
        vs = [pl.BlockSpec((1, 2 * QBLK, GRP), lambda n, i: (n, 0, v_lane_blk))]
    return qs, ks, vs


def _attn_fwd(q, k, v, *, name, banded, slopes_scaled=None, q_lane_blk=0, k_lane_blk=0, v_lane_blk=0):
    n, l, _ = q.shape
    nsub = 2 if l % (2 * QBLK) == 0 else 1
    tq = nsub * QBLK
    nkv = 2 if banded else 1

    def body(*refs):
        q_ref = refs[0]
        k_refs = refs[1:1 + nkv]
        v_refs = refs[1 + nkv:1 + 2 * nkv]
        o_ref, lse_ref = refs[1 + 2 * nkv:]
        step = pl.program_id(1)
        masks = _head_masks4((QBLK, GRP))
        qs = q_ref[0] * jnp.asarray(ATT_SCALE, BF16)
        if banded:
            kall = jnp.concatenate([k_refs[0][0], k_refs[1][0]], axis=0)
            vall = jnp.concatenate([v_refs[0][0], v_refs[1][0]], axis=0)
            in_band, is_cur, bias = _band_bias(slopes_scaled)
        scs, v2s = [], []
        for u in range(nsub):
            k2 = kall[u * QBLK:(u + 2) * QBLK] if banded else k_refs[0][0]
            v2s.append(vall[u * QBLK:(u + 2) * QBLK] if banded else v_refs[0][0])
            scs.append(_dot_nt(_stack_heads(qs[u * QBLK:(u + 1) * QBLK], masks), k2))
        ps, dens, lses = [], [], []
        for u in range(nsub):
            sc = scs[u]
            if banded:
                valid = in_band & (is_cur | (step * nsub + u > 0))
                sc = jnp.where(valid, sc + bias, NEG_BIG)
            m = jnp.max(sc, axis=-1, keepdims=True)
            p = jnp.exp(sc - m)
            den = jnp.sum(p, axis=-1, keepdims=True)
            ps.append(p.astype(BF16))
            dens.append(den)
            lses.append(m + jnp.log(den))
        ohs = [_dot(ps[u], v2s[u]) for u in range(nsub)]
        for u in range(nsub):
            o_ref[0, u * QBLK:(u + 1) * QBLK, :] = _unstack_heads(ohs[u] / dens[u], masks).astype(o_ref.dtype)
            lse_ref[0, u * QBLK:(u + 1) * QBLK, :] = _head_lanes(lses[u])

    qs, ks, vs = _attn_specs(banded, nsub, q_lane_blk, k_lane_blk, v_lane_blk)
    ob = pl.BlockSpec((1, tq, GRP), lambda nn, i: (nn, i, 0))
    return pl.pallas_call(
        body, name=name, grid=(n, l // tq),
        in_specs=[qs] + ks + vs, out_specs=[ob, pl.BlockSpec((1, tq, PER_HEAD), lambda nn, i: (nn, i, 0))],
        out_shape=[jax.ShapeDtypeStruct((n, l, GRP), BF16), jax.ShapeDtypeStruct((n, l, PER_HEAD), F32)],
        compiler_params=_cp("parallel", "arbitrary"),
    )(q, *([k] * nkv), *([v] * nkv))


def _attn_bwd(q, k, v, do, lse, delta, *, name, banded, slopes_scaled=None, q_lane_blk=0, k_lane_blk=0,
              v_lane_blk=0, do_lane_blk=0):
    n, l, _ = q.shape
    nsub = 2 if l % (2 * QBLK) == 0 else 1
    tq = nsub * QBLK
    nsteps = l // tq
    nkv = 2 if banded else 1
    lk = l if banded else 2 * QBLK

    def body(*refs):
        q_ref = refs[0]
        k_refs = refs[1:1 + nkv]
        v_refs = refs[1 + nkv:1 + 2 * nkv]
        do_ref, lse_ref, dl_ref, dq_ref, dk_out, dv_out, dk_ref, dv_ref = refs[1 + 2 * nkv:]
        step = pl.program_id(1)

        @pl.when(step == 0)
        def _():
            dk_ref[...] = jnp.zeros_like(dk_ref)
            dv_ref[...] = jnp.zeros_like(dv_ref)

        masks = _head_masks4((QBLK, GRP))
        qs = q_ref[0] * jnp.asarray(ATT_SCALE, BF16)
        dov = do_ref[0]
        lsev = lse_ref[0]
        dlv = dl_ref[0]
        if banded:
            kall = jnp.concatenate([k_refs[0][0], k_refs[1][0]], axis=0)
            vall = jnp.concatenate([v_refs[0][0], v_refs[1][0]], axis=0)
            in_band, is_cur, bias = _band_bias(slopes_scaled)
        q_st, do_st, k2s, scs, dps = [], [], [], [], []
        for u in range(nsub):
            rows = slice(u * QBLK, (u + 1) * QBLK)
            k2s.append(kall[u * QBLK:(u + 2) * QBLK] if banded else k_refs[0][0])
            v2 = vall[u * QBLK:(u + 2) * QBLK] if banded else v_refs[0][0]
            q_st.append(_stack_heads(qs[rows], masks))
            do_st.append(_stack_heads(dov[rows], masks))
            scs.append(_dot_nt(q_st[u], k2s[u]))
            dps.append(_dot_nt(do_st[u], v2))
        pbs, dss = [], []
        for u in range(nsub):
            rows = slice(u * QBLK, (u + 1) * QBLK)
            sc = scs[u]
            if banded:
                valid = in_band & (is_cur | (step * nsub + u > 0))
                sc = jnp.where(valid, sc + bias, NEG_BIG)
            p = jnp.exp(sc - _head_column(lsev[rows]))
            pbs.append(p.astype(BF16))
            dss.append((p * (dps[u] - _head_column(dlv[rows]))).astype(BF16))
        dqs = [_dot(dss[u], k2s[u]) for u in range(nsub)]
        dk2s = [_dot_tn(dss[u], q_st[u]) for u in range(nsub)]
        dv2s = [_dot_tn(pbs[u], do_st[u]) for u in range(nsub)]
        for u in range(nsub):
            dq_ref[0, u * QBLK:(u + 1) * QBLK, :] = (_unstack_heads(dqs[u], masks) * ATT_SCALE).astype(dq_ref.dtype)
        if banded:
            for u in range(nsub):
                i = step * nsub + u
                cur = pl.multiple_of(i * QBLK, QBLK)
                dk_ref[0, pl.ds(cur, QBLK), :] += dk2s[u][QBLK:]
                dv_ref[0, pl.ds(cur, QBLK), :] += dv2s[u][QBLK:]

                @pl.when(i > 0)
                def _():
                    prev = pl.multiple_of((i - 1) * QBLK, QBLK)
                    dk_ref[0, pl.ds(prev, QBLK), :] += dk2s[u][:QBLK]
                    dv_ref[0, pl.ds(prev, QBLK), :] += dv2s[u][:QBLK]
        else:
            dk_ref[0] += functools.reduce(jnp.add, dk2s)
            dv_ref[0] += functools.reduce(jnp.add, dv2s)

        @pl.when(step == nsteps - 1)
        def _():
            dk_out[...] = dk_ref[...].astype(dk_out.dtype)
            dv_out[...] = dv_ref[...].astype(dv_out.dtype)

    qs, ks, vs = _attn_specs(banded, nsub, q_lane_blk, k_lane_blk, v_lane_blk)
    ob = pl.BlockSpec((1, tq, GRP), lambda nn, i: (nn, i, 0))
    stat = pl.BlockSpec((1, tq, PER_HEAD), lambda nn, i: (nn, i, 0))
    dos = pl.BlockSpec((1, tq, GRP), lambda nn, i: (nn, i, do_lane_blk))
    kvb = pl.BlockSpec((1, lk, GRP), lambda nn, i: (nn, 0, 0))
    return pl.pallas_call(
        body, name=name, grid=(n, nsteps),
        in_specs=[qs] + ks + vs + [dos, stat, stat], out_specs=[ob, kvb, kvb],
        out_shape=[jax.ShapeDtypeStruct((n, l, GRP), BF16), jax.ShapeDtypeStruct((n, lk, GRP), BF16),
                   jax.ShapeDtypeStruct((n, lk, GRP), BF16)],
        scratch_shapes=[pltpu.VMEM((1, lk, GRP), F32), pltpu.VMEM((1, lk, GRP), F32)],
        compiler_params=_cp("parallel", "arbitrary"),
    )(q, *([k] * nkv), *([v] * nkv), do, lse, delta)


def _attn_delta(do, o, *, name, lane_blks):
    t, _ = do.shape
    tr = _pick(t, (512, 256, 128, 8))
    ng = len(lane_blks)

    def body(*refs):
        do_refs, o_refs, d_ref = refs[:ng], refs[ng:2 * ng], refs[2 * ng]
        ra = lax.broadcasted_iota(jnp.int32, (GRP, LANE), 0) // HEAD_DIM
        rb = lax.broadcasted_iota(jnp.int32, (GRP, LANE), 1)
        head_sum = (ra == rb).astype(BF16)
        prod = None
        for a_ref, b_ref in zip(do_refs, o_refs):
            term = a_ref[...].astype(F32) * b_ref[...].astype(F32)
            prod = term if prod is None else prod + term
        d_ref[...] = _dot_split(prod, head_sum)[:, :PER_HEAD]

    specs = [pl.BlockSpec((tr, GRP), functools.partial(lambda i, lb: (i, lb), lb=lb)) for lb in lane_blks]
    return pl.pallas_call(
        body, name=name, grid=(t // tr,), in_specs=specs + specs,
        out_specs=pl.BlockSpec((tr, PER_HEAD), lambda i: (i, 0)),
        out_shape=jax.ShapeDtypeStruct((t, PER_HEAD), F32),
        compiler_params=_cp("parallel"),
    )(*([do] * ng), *([o] * ng))


def _dil_combine(os, lses, *, name):
    t, _ = os[0].shape
    tr = _pick(t, (512, 256, 128, 8))
    ng = len(os)

    def body(*refs):
        o_refs, l_refs = refs[:ng], refs[ng:2 * ng]
        out_ref, lse_ref = refs[2 * ng:]
        ls = [r[...] for r in l_refs]
        m = functools.reduce(jnp.maximum, ls)
        tot = None
        for lv in ls:
            e = jnp.exp(lv - m)
            tot = e if tot is None else tot + e
        lse = m + jnp.log(tot)
        for g in range(ng):
            alpha = _spread_heads(jnp.exp(ls[g] - lse), tr)
            out_ref[:, GRP * g:GRP * (g + 1)] = (o_refs[g][...].astype(F32) * alpha).astype(out_ref.dtype)
        lse_ref[...] = lse

    sp = pl.BlockSpec((tr, GRP), lambda i: (i, 0))
    st = pl.BlockSpec((tr, PER_HEAD), lambda i: (i, 0))
    return pl.pallas_call(
        body, name=name, grid=(t // tr,), in_specs=[sp] * ng + [st] * ng,
        out_specs=[pl.BlockSpec((tr, GRP * ng), lambda i: (i, 0)), st],
        out_shape=[jax.ShapeDtypeStruct((t, GRP * ng), BF16), jax.ShapeDtypeStruct((t, PER_HEAD), F32)],
        compiler_params=_cp("parallel"),
    )(*os, *lses)


FFN_LB = 256
FFN_ROWS = 64
HALO = 16


def _conv_chunk(u_ref, w, ci):
    r0 = pl.multiple_of(ci * FFN_ROWS, FFN_ROWS)
    cur = u_ref[0, pl.ds(r0, FFN_ROWS), :].astype(F32)
    p0 = pl.multiple_of(jnp.maximum(r0 - HALO, 0), HALO)
    prev = u_ref[0, pl.ds(p0, HALO), :].astype(F32)
    prev = jnp.where(ci > 0, prev, 0.0)
    rowi = lax.broadcasted_iota(jnp.int32, (8, cur.shape[1]), 0)
    r1 = pltpu.roll(cur, 1, 0)
    r2 = pltpu.roll(cur, 2, 0)
    s1 = jnp.concatenate([jnp.where(rowi == 0, prev[HALO - 1:HALO], r1[0:8]), r1[8:]], axis=0)
    s2 = jnp.concatenate([jnp.where(rowi == 0, prev[HALO - 2:HALO - 1],
                                    jnp.where(rowi == 1, prev[HALO - 1:HALO], r2[0:8])), r2[8:]], axis=0)
    c = w[0:1] * s2
    c = c + w[1:2] * s1
    c = c + w[2:3] * cur
    return c, cur, s1, s2


def _ffn_mid_fwd(u, wconv, *, name):
    b, s, f2 = u.shape
    f = f2 // 2
    nlb = f // FFN_LB

    def body(ua_ref, ug_ref, wa_ref, wg_ref, h_ref):
        wa = wa_ref[...]
        wg = wg_ref[...]

        def step(ci, carry):
            ca = _conv_chunk(ua_ref, wa, ci)[0]
            cg = _conv_chunk(ug_ref, wg, ci)[0]
            r0 = pl.multiple_of(ci * FFN_ROWS, FFN_ROWS)
            h_ref[0, pl.ds(r0, FFN_ROWS), :] = (cg * jax.nn.sigmoid(cg) * ca).astype(h_ref.dtype)
            return carry

        lax.fori_loop(0, s // FFN_ROWS, step, 0)

    return pl.pallas_call(
        body, name=name, grid=(nlb, b),
        in_specs=[pl.BlockSpec((1, s, FFN_LB), lambda l, bb: (bb, 0, l)),
                  pl.BlockSpec((1, s, FFN_LB), lambda l, bb: (bb, 0, nlb + l)),
                  pl.BlockSpec((3, FFN_LB), lambda l, bb: (0, l)),
                  pl.BlockSpec((3, FFN_LB), lambda l, bb: (0, nlb + l))],
        out_specs=pl.BlockSpec((1, s, FFN_LB), lambda l, bb: (bb, 0, l)),
        out_shape=jax.ShapeDtypeStruct((b, s, f), BF16),
        compiler_params=_cp("parallel", "parallel"),
    )(u, u, wconv, wconv)


def _ffn_mid_bwd(u, wconv, dh, *, name):
    b, s, f2 = u.shape
    f = f2 // 2
    nlb = f // FFN_LB
    nchunk = s // FFN_ROWS

    def body(ua_ref, ug_ref, wa_ref, wg_ref, dh_ref, dua_ref, dug_ref, dwa_ref, dwg_ref):
        bb = pl.program_id(1)
        wa = wa_ref[...]
        wg = wg_ref[...]
        rowi = lax.broadcasted_iota(jnp.int32, (8, FFN_LB), 0)
        last = FFN_ROWS - 8

        def conv_transpose(dc, nxt, w):
            r1 = pltpu.roll(dc, FFN_ROWS - 1, 0)
            r2 = pltpu.roll(dc, FFN_ROWS - 2, 0)
            n1 = jnp.concatenate([r1[:last], jnp.where(rowi == 7, nxt[0:1], r1[last:])], axis=0)
            n2 = jnp.concatenate([r2[:last], jnp.where(rowi == 6, nxt[0:1],
                                                       jnp.where(rowi == 7, nxt[1:2], r2[last:]))], axis=0)
            return w[2:3] * dc + w[1:2] * n1 + w[0:1] * n2

        def step(t, carry):
            ci = nchunk - 1 - t
            nxt_a, nxt_g = carry[0], carry[1]
            r0 = pl.multiple_of(ci * FFN_ROWS, FFN_ROWS)
            ca, cura, s1a, s2a = _conv_chunk(ua_ref, wa, ci)
            cg, curg, s1g, s2g = _conv_chunk(ug_ref, wg, ci)
            dhv = dh_ref[0, pl.ds(r0, FFN_ROWS), :].astype(F32)
            sg = jax.nn.sigmoid(cg)
            da = dhv * (cg * sg)
            dg = dhv * ca * (sg * (1.0 + cg * (1.0 - sg)))
            dua_ref[0, pl.ds(r0, FFN_ROWS), :] = conv_transpose(da, nxt_a, wa).astype(dua_ref.dtype)
            dug_ref[0, pl.ds(r0, FFN_ROWS), :] = conv_transpose(dg, nxt_g, wg).astype(dug_ref.dtype)
            red = lambda x: jnp.sum(x, axis=0, keepdims=True)
            parts = (red(da * s2a), red(da * s1a), red(da * cura), red(dg * s2g), red(dg * s1g), red(dg * curg))
            return (da[0:8], dg[0:8]) + tuple(c + p for c, p in zip(carry[2:], parts))

        zero = jnp.zeros((1, FFN_LB), F32)
        zero8 = jnp.zeros((8, FFN_LB), F32)
        taps = lax.fori_loop(0, nchunk, step, (zero8, zero8) + (zero,) * 6)[2:]

        @pl.when(bb == 0)
        def _():
            for k in range(3):
                dwa_ref[k:k + 1, :] = taps[k]
                dwg_ref[k:k + 1, :] = taps[3 + k]

        @pl.when(bb > 0)
        def _():
            for k in range(3):
                dwa_ref[k:k + 1, :] += taps[k]
                dwg_ref[k:k + 1, :] += taps[3 + k]

    seq_a = pl.BlockSpec((1, s, FFN_LB), lambda l, bb: (bb, 0, l))
    seq_g = pl.BlockSpec((1, s, FFN_LB), lambda l, bb: (bb, 0, nlb + l))
    wsp = pl.BlockSpec((3, FFN_LB), lambda l, bb: (0, l))
    return pl.pallas_call(
        body, name=name, grid=(nlb, b),
        in_specs=[seq_a, seq_g, wsp, pl.BlockSpec((3, FFN_LB), lambda l, bb: (0, nlb + l)), seq_a],
        out_specs=[seq_a, seq_a, wsp, wsp],
        out_shape=[jax.ShapeDtypeStruct((b, s, f), BF16), jax.ShapeDtypeStruct((b, s, f), BF16),
                   jax.ShapeDtypeStruct((3, f), F32), jax.ShapeDtypeStruct((3, f), F32)],
        compiler_params=_cp("parallel", "arbitrary"),
    )(u, u, wconv, wconv, dh)


def _adam_math(w, g, m, v):
    m2 = ADAM_B1 * m + (1.0 - ADAM_B1) * g
    v2 = ADAM_B2 * v + (1.0 - ADAM_B2) * (g * g)
    m_hat = m2 / (1.0 - ADAM_B1 ** ADAM_STEP)
    v_hat = v2 / (1.0 - ADAM_B2 ** ADAM_STEP)
    delta = -ADAM_LR * (m_hat / (jnp.sqrt(v_hat) + ADAM_EPS) + ADAM_WD * w)
    return delta, m2, v2


def _adam(w, g, m, v, *, name):
    r, c = w.shape
    tr = _pick(r, (256, 128, 88, 64, 32, 16, 8))

    def body(w_ref, g_ref, m_ref, v_ref, d_ref, m2_ref, v2_ref):
        d, m2, v2 = _adam_math(w_ref[...], g_ref[...], m_ref[...], v_ref[...])
        d_ref[...] = d
        m2_ref[...] = m2
        v2_ref[...] = v2

    sp = pl.BlockSpec((tr, c), lambda i: (i, 0))
    return pl.pallas_call(
        body, name=name, grid=(r // tr,), in_specs=[sp] * 4, out_specs=[sp] * 3,
        out_shape=[jax.ShapeDtypeStruct((r, c), F32)] * 3,
        compiler_params=_cp("parallel"),
    )(w, g, m, v)


def _adam_small(quads, *, name):
    nq = len(quads)

    def body(*refs):
        ins, outs = refs[:4 * nq], refs[4 * nq:]
        for k in range(nq):
            w_ref, g_ref, m_ref, v_ref = ins[4 * k:4 * k + 4]
            d, m2, v2 = _adam_math(w_ref[...], g_ref[...], m_ref[...], v_ref[...])
            outs[3 * k][...] = d
            outs[3 * k + 1][...] = m2
            outs[3 * k + 2][...] = v2

    flat = [a for q in quads for a in q]
    out_shape = [jax.ShapeDtypeStruct(q[0].shape, F32) for q in quads for _ in range(3)]
    vm = pl.BlockSpec(memory_space=pltpu.VMEM)
    outs = pl.pallas_call(
        body, name=name, in_specs=[vm] * len(flat), out_specs=[vm] * len(out_shape), out_shape=out_shape,
        compiler_params=pltpu.CompilerParams(vmem_limit_bytes=VMEM_LIMIT_BYTES),
    )(*flat)
    return [tuple(outs[3 * k:3 * k + 3]) for k in range(nq)]


def _mesh_pos():
    return lax.axis_index("x"), lax.axis_index("y"), lax.axis_index("c")


def _flip(v, bit):
    return 1 - v if bit else v


def _all_gather_hbm(xl, *, name):
    r, c = xl.shape

    def body(x_ref, out_ref, send_sems, recv_sems, local_sem):
        x, y, cc = _mesh_pos()
        me, sibling = (x, y, cc), (x, y, 1 - cc)
        chips = [(1 - x, y), (x, 1 - y), (1 - x, 1 - y)]

        def rows(px, py, pc):
            return out_ref.at[pl.ds((4 * px + 2 * py + pc) * r, r), :]

        def copy(k, block, to, src=None):
            return pltpu.make_async_remote_copy(
                src_ref=rows(*block) if src is None else src, dst_ref=rows(*block),
                send_sem=send_sems.at[k], recv_sem=recv_sems.at[k], device_id=to, device_id_type=MESH_ID)

        mine = pltpu.make_async_copy(x_ref, rows(*me), local_sem)
        mine.start()
        first = [copy(0, me, sibling, src=x_ref)]
        first += [copy(1 + j, me, (*chip, cc), src=x_ref) for j, chip in enumerate(chips)]
        for cp in first:
            cp.start()
        passed = [copy(4 + j, (*chip, cc), sibling) for j, chip in enumerate(chips)]
        for j, chip in enumerate(chips):
            copy(1 + j, (*chip, cc), me).wait_recv()
            passed[j].start()
        copy(0, sibling, me).wait_recv()
        for j, chip in enumerate(chips):
            copy(4 + j, (*chip, 1 - cc), me).wait_recv()
        for cp in first + passed:
            cp.wait_send()
        mine.wait()

    hbm = pl.BlockSpec(memory_space=pltpu.HBM)
    return pl.pallas_call(
        body, name=name, in_specs=[hbm], out_specs=hbm,
        out_shape=jax.ShapeDtypeStruct((N_DEV * r, c), xl.dtype),
        scratch_shapes=[pltpu.SemaphoreType.DMA((7,)), pltpu.SemaphoreType.DMA((7,)), pltpu.SemaphoreType.DMA],
    )(xl)


def _all_reduce_small(xl, *, name):
    r, c = xl.shape

    def body(x_ref, sum_ref, all_ref, send_sems, recv_sems, local_sem):
        x, y, cc = _mesh_pos()
        me, sibling = (x, y, cc), (x, y, 1 - cc)
        chips = [(1 - x, y), (x, 1 - y), (1 - x, 1 - y)]

        def rows(px, py, pc):
            return all_ref.at[pl.ds((4 * px + 2 * py + pc) * r, r), :]

        def copy(k, block, to, src=None):
            return pltpu.make_async_remote_copy(
                src_ref=rows(*block) if src is None else src, dst_ref=rows(*block),
                send_sem=send_sems.at[k], recv_sem=recv_sems.at[k], device_id=to, device_id_type=MESH_ID)

        mine = pltpu.make_async_copy(x_ref, rows(*me), local_sem)
        mine.start()
        first = [copy(0, me, sibling, src=x_ref)]
        first += [copy(1 + j, me, (*chip, cc), src=x_ref) for j, chip in enumerate(chips)]
        for cp in first:
            cp.start()
        passed = [copy(4 + j, (*chip, cc), sibling) for j, chip in enumerate(chips)]
        for j, chip in enumerate(chips):
            copy(1 + j, (*chip, cc), me).wait_recv()
            passed[j].start()
        copy(0, sibling, me).wait_recv()
        for j, chip in enumerate(chips):
            copy(4 + j, (*chip, 1 - cc), me).wait_recv()
        for cp in first + passed:
            cp.wait_send()
        mine.wait()
        tot = all_ref[pl.ds(0, r), :]
        for dd in range(1, N_DEV):
            tot = tot + all_ref[pl.ds(dd * r, r), :]
        sum_ref[...] = tot

    vm = pl.BlockSpec(memory_space=pltpu.VMEM)
    return pl.pallas_call(
        body, name=name, in_specs=[vm], out_specs=[vm, vm],
        out_shape=[jax.ShapeDtypeStruct((r, c), F32), jax.ShapeDtypeStruct((N_DEV * r, c), F32)],
        scratch_shapes=[pltpu.SemaphoreType.DMA((7,)), pltpu.SemaphoreType.DMA((7,)), pltpu.SemaphoreType.DMA],
    )(xl)[0]


N_PEERS = N_DEV - 1
_HBM = pl.BlockSpec(memory_space=pltpu.HBM)
_SEM = pl.BlockSpec(memory_space=pltpu.SEMAPHORE)


def _peer_list(x, y, cc):
    return [(_flip(x, rel & 4), _flip(y, rel & 2), _flip(cc, rel & 1)) for rel in range(1, N_DEV)]


def _dev_index(p):
    return 4 * p[0] + 2 * p[1] + p[2]


def _split_copy(src_ref, land_ref, sems, k, peer, me, gather, landing_of):
    if gather:
        r = src_ref.shape[0]
        src = src_ref
        dst = land_ref.at[pl.ds(_dev_index(landing_of) * r, r), :]
    else:
        src = src_ref.at[_dev_index(peer)]
        dst = land_ref.at[_dev_index(landing_of)]
    return pltpu.make_async_remote_copy(src_ref=src, dst_ref=dst, send_sem=sems[k], recv_sem=sems[N_PEERS + k],
                                        device_id=peer, device_id_type=MESH_ID)


def _exchange_start(src, land, *, name, gather):
    def body(src_ref, land_ref, *rest):
        sems = rest[:2 * N_PEERS]
        token = rest[2 * N_PEERS + 2]
        x, y, cc = _mesh_pos()
        me = (x, y, cc)
        for k, peer in enumerate(_peer_list(x, y, cc)):
            _split_copy(src_ref, land_ref, sems, k, peer, me, gather, landing_of=me).start()
        token[...] = jnp.zeros_like(token)

    outs = pl.pallas_call(
        body, name=name,
        out_shape=tuple([pltpu.SemaphoreType.DMA(())] * (2 * N_PEERS)) + (
            pltpu.HBM(src.shape, src.dtype), pltpu.HBM(land.shape, land.dtype),
            jax.ShapeDtypeStruct((8, LANE), F32)),
        in_specs=(_HBM, _HBM),
        out_specs=tuple([_SEM] * (2 * N_PEERS)) + (_HBM, _HBM, pl.BlockSpec(memory_space=pltpu.VMEM)),
        input_output_aliases={0: 2 * N_PEERS, 1: 2 * N_PEERS + 1},
        compiler_params=pltpu.CompilerParams(has_side_effects=pltpu.SideEffectType.DATAFLOW_SIDE_EFFECTING),
    )(pltpu.with_memory_space_constraint(src, pltpu.HBM), pltpu.with_memory_space_constraint(land, pltpu.HBM))
    return outs[:2 * N_PEERS], outs[2 * N_PEERS], outs[2 * N_PEERS + 1], outs[2 * N_PEERS + 2]


def _gather_start(local, me, *, name):
    rows, cols = local.shape
    land = lax.dynamic_update_slice(lax.empty((N_DEV * rows, cols), local.dtype), local, (me * rows, 0))
    return _exchange_start(local, land, name=name, gather=True)


def _exchange_wait(sems, src_thru, land_thru, after, *, name, gather):
    def body(src_ref, land_ref, *rest):
        sem_refs = rest[:2 * N_PEERS]
        x, y, cc = _mesh_pos()
        me = (x, y, cc)
        for k, peer in enumerate(_peer_list(x, y, cc)):
            cp = _split_copy(src_ref, land_ref, sem_refs, k, peer, me, gather, landing_of=peer)
            cp.wait_send()
            cp.wait_recv()

    outs = pl.pallas_call(
        body, name=name,
        out_shape=(pltpu.HBM(src_thru.shape, src_thru.dtype), pltpu.HBM(land_thru.shape, land_thru.dtype)),
        in_specs=(_HBM, _HBM) + tuple([_SEM] * (2 * N_PEERS)) + (pl.BlockSpec(memory_space=pl.ANY),),
        out_specs=(_HBM, _HBM), input_output_aliases={0: 0, 1: 1},
        compiler_params=pltpu.CompilerParams(has_side_effects=pltpu.SideEffectType.DATAFLOW_SIDE_EFFECTING),
    )(src_thru, land_thru, *sems, after)
    return outs[1]


def _sum_blocks(recv, own, after, *, name):
    nd, r, c = recv.shape
    tr = _pick(r, (448, 256, 128, 64, 32, 16))

    def body(x_ref, own_ref, after_ref, o_ref):
        x, y, cc = _mesh_pos()
        me = 4 * x + 2 * y + cc
        tot = None
        for dd in range(nd):
            term = jnp.where(me == dd, own_ref[0], x_ref[dd]).astype(F32)
            tot = term if tot is None else tot + term
        o_ref[...] = tot

    return pl.pallas_call(
        body, name=name, grid=(r // tr,),
        in_specs=[pl.BlockSpec((nd, tr, c), lambda i: (0, i, 0)), pl.BlockSpec((1, tr, c), lambda i: (0, i, 0)),
                  pl.BlockSpec(memory_space=pl.ANY)],
        out_specs=pl.BlockSpec((tr, c), lambda i: (i, 0)),
        out_shape=jax.ShapeDtypeStruct((r, c), F32),
        compiler_params=_cp("parallel"),
    )(recv, own, after)


SHARD_KIND = {"a_w_in": "col", "a_w_out": "row", "a_w_mem_kv": "row", "a_ffn_up": "col", "a_ffn_down": "row",
              "w_kv_shared": "col", "b_w_in": "row", "b_w_out": "row", "b_w_mem_kv": "row", "b_ffn_up": "col",
              "b_ffn_down": "row"}
EARLY_WEIGHTS = ("a_w_in", "a_w_mem_kv")
FFN_UP_WEIGHTS = ("a_ffn_up", "b_ffn_up")
WIDE_WEIGHTS = tuple(nm for nm in SHARD_KIND if nm not in EARLY_WEIGHTS + FFN_UP_WEIGHTS)
LATE_WEIGHTS = WIDE_WEIGHTS + FFN_UP_WEIGHTS


def _as2d(a):
    return a.reshape(a.shape[-2], a.shape[-1]) if a.ndim >= 2 else a.reshape(1, a.shape[0])


def _pack_local(shards):
    return jnp.concatenate([_as2d(s).astype(BF16).reshape(-1, PACK_COLS) for s in shards], axis=0)


def _unpack_full(gathered, names, shapes):
    out = {}
    r0 = 0
    for name in names:
        rows, cols = shapes[name]
        nr = rows * cols // PACK_COLS
        blk = gathered[:, r0:r0 + nr, :].reshape(N_DEV, rows, cols)
        if SHARD_KIND[name] == "row":
            out[name] = blk.reshape(N_DEV * rows, cols)
        else:
            out[name] = blk.transpose(1, 0, 2).reshape(rows, N_DEV * cols)
        r0 += nr
    return out


def _pack_grads(grads, names, shapes):
    parts = []
    for name in names:
        rows, cols = shapes[name]
        g = grads[name]
        if SHARD_KIND[name] == "row":
            blk = g.reshape(N_DEV, rows, cols)
        else:
            blk = g.reshape(rows, N_DEV, cols).transpose(1, 0, 2)
        parts.append(blk.astype(BF16).reshape(N_DEV, rows * cols // PACK_COLS, PACK_COLS))
    return jnp.concatenate(parts, axis=1)


def _unpack_local(gsum, names, shapes):
    out = {}
    r0 = 0
    for name in names:
        rows, cols = shapes[name]
        nr = rows * cols // PACK_COLS
        out[name] = gsum[r0:r0 + nr].reshape(rows, cols)
        r0 += nr
    return out


def _by_residue(t, d):
    if d == 1:
        return t
    b, s, c = t.shape
    return t.reshape(b, s // d, d, c).transpose(0, 2, 1, 3).reshape(b * d, s // d, c)


def _from_residue(t, d, b):
    if d == 1:
        return t
    n, l, c = t.shape
    return t.reshape(b, d, l, c).transpose(0, 2, 1, 3).reshape(b, l * d, c)


def _alibi_slopes():
    return [2.0 ** (-ALIBI_MAX_BIAS * (i + 1) / N_DIL_HEADS) for i in range(N_DIL_HEADS)]


def _conv_ffn_fwd(xin, gain, w_up, wconv, w_down, tag, b, s):
    (n,), r = _rms_fwd(xin, [gain], name=f"{tag}_rms_ffn")
    u = _mm(n, w_up, name=f"{tag}_up", out_dtype=BF16).reshape(b, s, -1)
    hmid = _ffn_mid_fwd(u, wconv, name=f"{tag}_ffn_mid").reshape(b * s, -1)
    xout = _mm(hmid, w_down, name=f"{tag}_down", out_dtype=F32, res=xin)
    return xout, (n, r, u, hmid)


def _conv_ffn_bwd(dxout, xin, gain, saved, w_up, wconv, w_down, tag, b, s):
    n, r, u, hmid = saved
    f = hmid.shape[1]
    dhmid = _mm(dxout, w_down, name=f"{tag}_d_hmid", out_dtype=BF16, trans_b=True)
    g_down = _mm(hmid, dxout, name=f"{tag}_g_down", out_dtype=BF16, trans_a=True)
    du_a, du_g, gc_a, gc_g = _ffn_mid_bwd(u, wconv, dhmid.reshape(b, s, f), name=f"{tag}_ffn_mid_bwd")
    du_a = du_a.reshape(b * s, f)
    du_g = du_g.reshape(b * s, f)
    dn = _mm(du_a, w_up[:, :f], name=f"{tag}_d_n_a", out_dtype=F32, trans_b=True)
    dn = _mm(du_g, w_up[:, f:], name=f"{tag}_d_n_g", out_dtype=F32, res=dn, trans_b=True)
    g_up = jnp.concatenate([_mm(n, du_a, name=f"{tag}_g_up_a", out_dtype=BF16, trans_a=True),
                            _mm(n, du_g, name=f"{tag}_g_up_g", out_dtype=BF16, trans_a=True)], axis=1)
    dxin, (g_gain,) = _rms_bwd(xin, r, [(dn, gain)], dxout, name=f"{tag}_rms_ffn_bwd")
    return dxin, g_up, g_down, jnp.concatenate([gc_a, gc_g], axis=1), g_gain


def _mem_kv_fwd(mem2d, gain, w_mem_kv, tag, b):
    (nm,), rm = _rms_fwd(mem2d, [gain], name=f"{tag}_rms_mem")
    kvm = _mm(nm, w_mem_kv, name=f"{tag}_mem_kv", out_dtype=BF16)
    return kvm.reshape(b, -1, 2 * MEM_WIDTH), (nm, rm)


def _mem_kv_bwd(dk, dv, mem2d, gain, saved, w_mem_kv, tag):
    nm, rm = saved
    dkvm = jnp.concatenate([dk, dv], axis=-1).reshape(-1, 2 * MEM_WIDTH)
    dnm = _mm(dkvm, w_mem_kv, name=f"{tag}_d_nm", out_dtype=F32, trans_b=True)
    g_w = _mm(nm, dkvm, name=f"{tag}_g_mem_kv", out_dtype=BF16, trans_a=True)
    _, (g_gain,) = _rms_bwd(mem2d, rm, [(dnm, gain)], None, name=f"{tag}_rms_mem_bwd", need_dx=False)
    return g_w, g_gain


def kernel(x, mem, a_norm_attn, a_w_in, a_w_out, a_norm_mem, a_w_mem_kv, a_norm_ffn, a_ffn_up, a_ffn_conv, a_ffn_down, kv_norm, w_kv_shared, b_norm_attn, b_w_in, b_w_out, b_norm_mem, b_w_mem_kv, b_norm_ffn, b_ffn_up, b_ffn_conv, b_ffn_down, final_norm, loss_target, m_a_norm_attn, m_a_w_in, m_a_w_out, m_a_norm_mem, m_a_w_mem_kv, m_a_norm_ffn, m_a_ffn_up, m_a_ffn_conv, m_a_ffn_down, m_kv_norm, m_w_kv_shared, m_b_norm_attn, m_b_w_in, m_b_w_out, m_b_norm_mem, m_b_w_mem_kv, m_b_norm_ffn, m_b_ffn_up, m_b_ffn_conv, m_b_ffn_down, m_final_norm, v_a_norm_attn, v_a_w_in, v_a_w_out, v_a_norm_mem, v_a_w_mem_kv, v_a_norm_ffn, v_a_ffn_up, v_a_ffn_conv, v_a_ffn_down, v_kv_norm, v_w_kv_shared, v_b_norm_attn, v_b_w_in, v_b_w_out, v_b_norm_mem, v_b_w_mem_kv, v_b_norm_ffn, v_b_ffn_up, v_b_ffn_conv, v_b_ffn_down, v_final_norm):
    names = ["a_norm_attn", "a_w_in", "a_w_out", "a_norm_mem", "a_w_mem_kv", "a_norm_ffn", "a_ffn_up",
             "a_ffn_conv", "a_ffn_down", "kv_norm", "w_kv_shared", "b_norm_attn", "b_w_in", "b_w_out",
             "b_norm_mem", "b_w_mem_kv", "b_norm_ffn", "b_ffn_up", "b_ffn_conv", "b_ffn_down", "final_norm"]
    wl = dict(zip(names, [a_norm_attn, a_w_in, a_w_out, a_norm_mem, a_w_mem_kv, a_norm_ffn, a_ffn_up,
                          a_ffn_conv, a_ffn_down, kv_norm, w_kv_shared, b_norm_attn, b_w_in, b_w_out,
                          b_norm_mem, b_w_mem_kv, b_norm_ffn, b_ffn_up, b_ffn_conv, b_ffn_down, final_norm]))
    ml = dict(zip(names, [m_a_norm_attn, m_a_w_in, m_a_w_out, m_a_norm_mem, m_a_w_mem_kv, m_a_norm_ffn,
                          m_a_ffn_up, m_a_ffn_conv, m_a_ffn_down, m_kv_norm, m_w_kv_shared, m_b_norm_attn,
                          m_b_w_in, m_b_w_out, m_b_norm_mem, m_b_w_mem_kv, m_b_norm_ffn, m_b_ffn_up,
                          m_b_ffn_conv, m_b_ffn_down, m_final_norm]))
    vl = dict(zip(names, [v_a_norm_attn, v_a_w_in, v_a_w_out, v_a_norm_mem, v_a_w_mem_kv, v_a_norm_ffn,
                          v_a_ffn_up, v_a_ffn_conv, v_a_ffn_down, v_kv_norm, v_w_kv_shared, v_b_norm_attn,
                          v_b_w_in, v_b_w_out, v_b_norm_mem, v_b_w_mem_kv, v_b_norm_ffn, v_b_ffn_up,
                          v_b_ffn_conv, v_b_ffn_down, v_final_norm]))
    b, s, d = x.shape
    t = b * s
    my_x, my_y, my_c = _mesh_pos()
    me = 4 * my_x + 2 * my_y + my_c

    shapes = {nm: _as2d(wl[nm]).shape for nm in SHARD_KIND}
    early_local = _pack_local([wl[nm] for nm in EARLY_WEIGHTS])
    early_all = _all_gather_hbm(early_local, name="gather_early").reshape(N_DEV, early_local.shape[0], PACK_COLS)
    wf = _unpack_full(early_all, EARLY_WEIGHTS, shapes)
    wide_local = _pack_local([wl[nm] for nm in WIDE_WEIGHTS])
    up_local = jnp.concatenate([_as2d(wl[nm]).astype(BF16) for nm in FFN_UP_WEIGHTS], axis=0)
    gw_sems, gw_src, gw_land, gw_token = _gather_start(wide_local, me, name="gather_wide_start")
    gu_sems, gu_src, gu_land, gu_token = _gather_start(up_local, me, name="gather_up_start")

    sharded_small = ["a_norm_attn", "a_norm_mem", "a_norm_ffn", "a_ffn_conv", "b_ffn_conv"]
    small_flat = jnp.concatenate([wl[nm].reshape(-1) for nm in sharded_small])
    n_small = small_flat.shape[0]
    small_rows = -(-n_small // (8 * LANE)) * 8
    small_local = jnp.pad(small_flat, (0, small_rows * LANE - n_small)).reshape(small_rows, LANE)
    small_all = _all_gather_hbm(small_local, name="gather_small").reshape(N_DEV, small_rows * LANE)
    sfull = {}
    r0 = 0
    for nm in sharded_small:
        rows, cols = _as2d(wl[nm]).shape
        blk = small_all[:, r0:r0 + rows * cols].reshape(N_DEV, rows, cols)
        sfull[nm] = blk.transpose(1, 0, 2).reshape(rows, N_DEV * cols)
        r0 += rows * cols
    gain = {nm: sfull[nm] for nm in ("a_norm_attn", "a_norm_mem", "a_norm_ffn")}
    for nm in ("kv_norm", "b_norm_attn", "b_norm_mem", "b_norm_ffn", "final_norm"):
        gain[nm] = _as2d(wl[nm])
    conv_a, conv_b = sfull["a_ffn_conv"], sfull["b_ffn_conv"]

    x2d = x.reshape(t, d)
    mem2d = mem.reshape(-1, d)
    tgt2d = loss_target.reshape(t, d)
    qmem_blk_a = 3 * SB_WIDTH // GRP
    qmem_blk_b = DIL_WIDTH // GRP

    (n1,), r1 = _rms_fwd(x2d, [gain["a_norm_attn"]], name="a_rms_attn")
    proj_a = _mm(n1, wf["a_w_in"], name="a_in", out_dtype=BF16).reshape(b, s, -1)
    kvm_a, mem_saved_a = _mem_kv_fwd(mem2d, gain["a_norm_mem"], wf["a_w_mem_kv"], "a", b)
    o_sb, rsum = _sb_fwd(proj_a, gw_token + gu_token, name="a_sb_fwd")
    o_mem_a, lse_mem_a = _attn_fwd(proj_a, kvm_a, kvm_a, name="a_mem_fwd", banded=False,
                                   q_lane_blk=qmem_blk_a, k_lane_blk=0, v_lane_blk=1)
    wide_all = _exchange_wait(gw_sems, gw_src, gw_land, rsum, name="gather_wide_wait", gather=True)
    up_all = _exchange_wait(gu_sems, gu_src, gu_land, rsum, name="gather_up_wait", gather=True)
    wf.update(_unpack_full(wide_all.reshape(N_DEV, wide_local.shape[0], PACK_COLS), WIDE_WEIGHTS, shapes))
    up_rows, up_cols = shapes[FFN_UP_WEIGHTS[0]]
    up_all = up_all.reshape(N_DEV, len(FFN_UP_WEIGHTS), up_rows, up_cols)
    for k, nm in enumerate(FFN_UP_WEIGHTS):
        wf[nm] = up_all[:, k].transpose(1, 0, 2).reshape(up_rows, N_DEV * up_cols)
    cat_a = jnp.concatenate([o_sb, o_mem_a], axis=-1).reshape(t, d)
    x1 = _mm(cat_a, wf["a_w_out"], name="a_out", out_dtype=F32, res=x2d)
    xa, ffn_saved_a = _conv_ffn_fwd(x1, gain["a_norm_ffn"], wf["a_ffn_up"], conv_a, wf["a_ffn_down"], "a", b, s)

    (nk, n3), r3 = _rms_fwd(xa, [gain["kv_norm"], gain["b_norm_attn"]], name="b_rms_attn")
    kvsh = _mm(nk, wf["w_kv_shared"], name="kv_shared", out_dtype=BF16).reshape(b, s, -1)
    proj_b = _mm(n3, wf["b_w_in"], name="b_in", out_dtype=BF16).reshape(b, s, -1)
    kvm_b, mem_saved_b = _mem_kv_fwd(mem2d, gain["b_norm_mem"], wf["b_w_mem_kv"], "b", b)
    slopes = _alibi_slopes()
    dil_q, dil_k, dil_v, dil_o, dil_lse, dil_slopes = [], [], [], [], [], []
    for g, (_, dil) in enumerate(DIL_GROUPS):
        qg = _by_residue(proj_b[:, :, GRP * g:GRP * (g + 1)], dil)
        kg = _by_residue(kvsh[:, :, GRP * g:GRP * (g + 1)], dil)
        vg = _by_residue(kvsh[:, :, DIL_WIDTH + GRP * g:DIL_WIDTH + GRP * (g + 1)], dil)
        sl = [slopes[4 * g + h] * dil for h in range(4)]
        og, lg = _attn_fwd(qg, kg, vg, name=f"b_dil{g}_fwd", banded=True, slopes_scaled=sl)
        dil_q.append(qg)
        dil_k.append(kg)
        dil_v.append(vg)
        dil_slopes.append(sl)
        dil_o.append(_from_residue(og, dil, b).reshape(t, GRP))
        dil_lse.append(_from_residue(lg, dil, b).reshape(t, PER_HEAD))
    o_dil, lse_joint = _dil_combine(dil_o, dil_lse, name="b_dil_combine")
    o_mem_b, lse_mem_b = _attn_fwd(proj_b, kvm_b, kvm_b, name="b_mem_fwd", banded=False,
                                   q_lane_blk=qmem_blk_b, k_lane_blk=0, v_lane_blk=1)
    cat_b = jnp.concatenate([o_dil, o_mem_b.reshape(t, MEM_WIDTH)], axis=-1)
    x3 = _mm(cat_b, wf["b_w_out"], name="b_out", out_dtype=F32, res=xa)
    xb, ffn_saved_b = _conv_ffn_fwd(x3, gain["b_norm_ffn"], wf["b_ffn_up"], conv_b, wf["b_ffn_down"], "b", b, s)

    dxb, g_final, loss_vec = _loss_head(xb, gain["final_norm"], tgt2d, name="loss_head")

    grads = {}
    sgrads = {"final_norm": g_final}
    dx3, grads["b_ffn_up"], grads["b_ffn_down"], sgrads["b_ffn_conv"], sgrads["b_norm_ffn"] = _conv_ffn_bwd(
        dxb, x3, gain["b_norm_ffn"], ffn_saved_b, wf["b_ffn_up"], conv_b, wf["b_ffn_down"], "b", b, s)
    dcat_b = _mm(dx3, wf["b_w_out"], name="b_d_cat", out_dtype=BF16, trans_b=True)
    grads["b_w_out"] = _mm(cat_b, dx3, name="b_g_out", out_dtype=BF16, trans_a=True)
    dcat_b3 = dcat_b.reshape(b, s, d)
    delta_mem_b = _attn_delta(dcat_b, cat_b, name="b_mem_delta", lane_blks=[qmem_blk_b]).reshape(b, s, PER_HEAD)
    dq_mem_b, dkm_b, dvm_b = _attn_bwd(proj_b, kvm_b, kvm_b, dcat_b3, lse_mem_b, delta_mem_b, name="b_mem_bwd",
                                       banded=False, q_lane_blk=qmem_blk_b, k_lane_blk=0, v_lane_blk=1,
                                       do_lane_blk=qmem_blk_b)
    delta_dil = _attn_delta(dcat_b, cat_b, name="b_dil_delta", lane_blks=[0, 1, 2]).reshape(b, s, PER_HEAD)
    lse_joint3 = lse_joint.reshape(b, s, PER_HEAD)
    dq_parts, dk_parts, dv_parts = [], [], []
    for g, (_, dil) in enumerate(DIL_GROUPS):
        dog = _by_residue(dcat_b3[:, :, GRP * g:GRP * (g + 1)], dil)
        lg = _by_residue(lse_joint3, dil)
        dg = _by_residue(delta_dil, dil)
        dqg, dkg, dvg = _attn_bwd(dil_q[g], dil_k[g], dil_v[g], dog, lg, dg, name=f"b_dil{g}_bwd", banded=True,
                                  slopes_scaled=dil_slopes[g])
        dq_parts.append(_from_residue(dqg, dil, b))
        dk_parts.append(_from_residue(dkg, dil, b))
        dv_parts.append(_from_residue(dvg, dil, b))
    dproj_b = jnp.concatenate(dq_parts + [dq_mem_b], axis=-1).reshape(t, d)
    dn3 = _mm(dproj_b, wf["b_w_in"], name="b_d_n", out_dtype=F32, trans_b=True)
    grads["b_w_in"] = _mm(n3, dproj_b, name="b_g_in", out_dtype=BF16, trans_a=True)
    grads["b_w_mem_kv"], sgrads["b_norm_mem"] = _mem_kv_bwd(dkm_b, dvm_b, mem2d, gain["b_norm_mem"], mem_saved_b,
                                                           wf["b_w_mem_kv"], "b")
    dkvsh = jnp.concatenate(dk_parts + dv_parts, axis=-1).reshape(t, 2 * DIL_WIDTH).astype(BF16)
    dnk = _mm(dkvsh, wf["w_kv_shared"], name="kv_d_n", out_dtype=F32, trans_b=True)
    grads["w_kv_shared"] = _mm(nk, dkvsh, name="kv_g", out_dtype=BF16, trans_a=True)
    dxa, (sgrads["kv_norm"], sgrads["b_norm_attn"]) = _rms_bwd(
        xa, r3, [(dnk, gain["kv_norm"]), (dn3, gain["b_norm_attn"])], dx3, name="b_rms_attn_bwd")

    dx1, grads["a_ffn_up"], grads["a_ffn_down"], sgrads["a_ffn_conv"], sgrads["a_norm_ffn"] = _conv_ffn_bwd(
        dxa, x1, gain["a_norm_ffn"], ffn_saved_a, wf["a_ffn_up"], conv_a, wf["a_ffn_down"], "a", b, s)
    dcat_a = _mm(dx1, wf["a_w_out"], name="a_d_cat", out_dtype=BF16, trans_b=True)
    grads["a_w_out"] = _mm(cat_a, dx1, name="a_g_out", out_dtype=BF16, trans_a=True)
    dcat_a3 = dcat_a.reshape(b, s, d)
    delta_mem_a = _attn_delta(dcat_a, cat_a, name="a_mem_delta", lane_blks=[qmem_blk_b]).reshape(b, s, PER_HEAD)
    dq_mem_a, dkm_a, dvm_a = _attn_bwd(proj_a, kvm_a, kvm_a, dcat_a3, lse_mem_a, delta_mem_a, name="a_mem_bwd",
                                       banded=False, q_lane_blk=qmem_blk_a, k_lane_blk=0, v_lane_blk=1,
                                       do_lane_blk=qmem_blk_b)
    wide_grads = _pack_grads(grads, WIDE_WEIGHTS, shapes)
    up_grads = jnp.concatenate([grads[nm].reshape(up_rows, N_DEV, up_cols).transpose(1, 0, 2)
                                for nm in FFN_UP_WEIGHTS], axis=1)
    own_wide = lax.dynamic_slice(wide_grads, (me, 0, 0), (1,) + wide_grads.shape[1:])
    own_up = lax.dynamic_slice(up_grads, (me, 0, 0), (1,) + up_grads.shape[1:])
    xw_sems, xw_src, xw_land, xw_token = _exchange_start(wide_grads, lax.empty(wide_grads.shape, BF16),
                                                         name="grads_wide_start", gather=False)
    xu_sems, xu_src, xu_land, xu_token = _exchange_start(up_grads, lax.empty(up_grads.shape, BF16),
                                                         name="grads_up_start", gather=False)
    dq_sb, dk_sb, dv_sb = _sb_bwd(proj_a, dcat_a3, rsum, xw_token + xu_token, name="a_sb_bwd")
    dproj_a = jnp.concatenate([dq_sb, dk_sb, dv_sb, dq_mem_a], axis=-1).reshape(t, -1)
    dn1 = _mm(dproj_a, wf["a_w_in"], name="a_d_n", out_dtype=F32, trans_b=True)
    grads["a_w_in"] = _mm(n1, dproj_a, name="a_g_in", out_dtype=BF16, trans_a=True)
    grads["a_w_mem_kv"], sgrads["a_norm_mem"] = _mem_kv_bwd(dkm_a, dvm_a, mem2d, gain["a_norm_mem"], mem_saved_a,
                                                           wf["a_w_mem_kv"], "a")
    early_grads = _pack_grads(grads, EARLY_WEIGHTS, shapes)
    own_early = lax.dynamic_slice(early_grads, (me, 0, 0), (1,) + early_grads.shape[1:])
    ee_sems, ee_src, ee_land, ee_token = _exchange_start(early_grads, lax.empty(early_grads.shape, BF16),
                                                         name="grads_early_start", gather=False)
    dx0, (sgrads["a_norm_attn"],) = _rms_bwd(x2d, r1, [(dn1, gain["a_norm_attn"])], dx1, name="a_rms_attn_bwd")
    grad_x = dx0.reshape(b, s, d)

    wide_recv = _exchange_wait(xw_sems, xw_src, xw_land, dx0, name="grads_wide_wait", gather=False)
    up_recv = _exchange_wait(xu_sems, xu_src, xu_land, dx0, name="grads_up_wait", gather=False)
    gl = _unpack_local(_sum_blocks(wide_recv, own_wide, ee_token, name="sum_grads_wide"), WIDE_WEIGHTS, shapes)
    up_sum = _sum_blocks(up_recv, own_up, ee_token, name="sum_grads_up")
    for k, nm in enumerate(FFN_UP_WEIGHTS):
        gl[nm] = up_sum[k * up_rows:(k + 1) * up_rows]

    small_names = ["a_norm_attn", "a_norm_mem", "a_norm_ffn", "kv_norm", "b_norm_attn", "b_norm_mem",
                   "b_norm_ffn", "final_norm", "a_ffn_conv", "b_ffn_conv"]
    small_flat = jnp.concatenate([sgrads[nm].reshape(-1) for nm in small_names] + [loss_vec.reshape(-1)])
    n_flat = small_flat.shape[0]
    red_rows = -(-n_flat // (8 * PACK_COLS)) * 8
    small_pack = jnp.pad(small_flat, (0, red_rows * PACK_COLS - n_flat)).reshape(red_rows, PACK_COLS)
    small_sum = _all_reduce_small(small_pack, name="reduce_small").reshape(-1)
    r0 = 0
    for nm in small_names:
        rows, cols = sgrads[nm].shape
        full = small_sum[r0:r0 + rows * cols].reshape(rows, cols)
        r0 += rows * cols
        if nm in sharded_small:
            lc = cols // N_DEV
            gl[nm] = lax.dynamic_slice(full, (0, me * lc), (rows, lc))
        else:
            gl[nm] = full
    loss = (0.5 / d) * jnp.sum(small_sum[r0:r0 + d])

    upd = {}
    for nm in LATE_WEIGHTS:
        upd[nm] = _adam(_as2d(wl[nm]), gl[nm], _as2d(ml[nm]), _as2d(vl[nm]), name=f"adam_{nm}")
    res_small = _adam_small([(_as2d(wl[nm]), gl[nm], _as2d(ml[nm]), _as2d(vl[nm])) for nm in small_names],
                            name="adam_small")
    for nm, r in zip(small_names, res_small):
        upd[nm] = r
    early_recv = _exchange_wait(ee_sems, ee_src, ee_land, upd[LATE_WEIGHTS[-1]][0], name="grads_early_wait",
                                gather=False)
    gl.update(_unpack_local(_sum_blocks(early_recv, own_early, ee_token, name="sum_grads_early"), EARLY_WEIGHTS,
                            shapes))
    for nm in EARLY_WEIGHTS:
        upd[nm] = _adam(_as2d(wl[nm]), gl[nm], _as2d(ml[nm]), _as2d(vl[nm]), name=f"adam_{nm}")

    g_out = [gl[nm].reshape(wl[nm].shape) for nm in names]
    d_out = [upd[nm][0].reshape(wl[nm].shape) for nm in names]
    m_out = [upd[nm][1].reshape(wl[nm].shape) for nm in names]
    v_out = [upd[nm][2].reshape(wl[nm].shape) for nm in names]
    return (loss, grad_x, *g_out, *d_out, *m_out, *v_out)
```

```python
import functools

import jax
import jax.numpy as jnp
from jax import lax
from jax.experimental import pallas as pl
from jax.experimental.pallas import tpu as pltpu

F32 = jnp.float32
BF16 = jnp.bfloat16

N_DEV = 8
HEAD_DIM = 64
N_SB_HEADS = 12
N_DIL_HEADS = 12
DIL_GROUPS = ((128, 1), (512, 4), (2048, 16))
SB_WIDTH = N_SB_HEADS * HEAD_DIM
MEM_WIDTH = 256
DIL_WIDTH = N_DIL_HEADS * HEAD_DIM
ATT_SCALE = HEAD_DIM ** -0.5
EPS = 1e-6
ALIBI_MAX_BIAS = 8.0
NEG_BIG = -1e30

ADAM_LR = 0.001
ADAM_B1 = 0.9
ADAM_B2 = 0.999
ADAM_EPS = 1e-08
ADAM_WD = 0.01
ADAM_STEP = 10

LANE = 128
QBLK = 128
VMEM_LIMIT_BYTES = 48 * 1024 * 1024
PACK_COLS = 1024
MESH_ID = pl.DeviceIdType.MESH


def _cp(*sem):
    return pltpu.CompilerParams(dimension_semantics=sem, vmem_limit_bytes=VMEM_LIMIT_BYTES)


def _pick(n, cands):
    for c in cands:
        if n % c == 0:
            return c
    raise ValueError(f"no tile for {n} in {cands}")


def _dot(a, b):
    return jnp.dot(a, b, preferred_element_type=F32)


def _dot_nt(a, b):
    return lax.dot_general(a, b, (((1,), (1,)), ((), ())), preferred_element_type=F32)


def _dot_tn(a, b):
    return lax.dot_general(a, b, (((0,), (0,)), ((), ())), preferred_element_type=F32)


def _dot_split(x, u):
    hi = x.astype(BF16)
    lo = (x - hi.astype(F32)).astype(BF16)
    return _dot(hi, u) + _dot(lo, u)


def _mm(a, b, *, name, out_dtype, res=None, trans_a=False, trans_b=False):
    assert not (trans_a and trans_b)
    if trans_a:
        kdim, m = a.shape
    else:
        m, kdim = a.shape
    if trans_b:
        n, kb = b.shape
    else:
        kb, n = b.shape
    assert kb == kdim, (a.shape, b.shape)
    if trans_a:
        tm = _pick(m, (1408, 1024, 512, 256, 128))
        tn = _pick(n, (1024, 1280, 1408, 768, 512, 256, 128))
        tk = _pick(kdim, (1024, 512, 256))
    else:
        tm = _pick(m, (1024, 512, 256, 128))
        tk = kdim if kdim <= 2816 else _pick(kdim, (2048, 1536, 1408, 1280, 1024, 512))
        tn = _pick(n, (512, 256, 128) if tk > 2048 else (1408, 1280, 1024, 768, 512, 256, 128))
    nk = kdim // tk
    has_res = res is not None

    def body(*refs):
        if has_res:
            a_ref, b_ref, r_ref, o_ref = refs[:4]
            scr = refs[4:]
        else:
            a_ref, b_ref, o_ref = refs[:3]
            r_ref = None
            scr = refs[3:]
        av = a_ref[...].astype(BF16)
        bv = b_ref[...].astype(BF16)
        if trans_a:
            p = _dot_tn(av, bv)
        elif trans_b:
            p = _dot_nt(av, bv)
        else:
            p = _dot(av, bv)

        def finish(acc):
            if has_res:
                acc = acc + r_ref[...]
            o_ref[...] = acc.astype(o_ref.dtype)

        if nk == 1:
            finish(p)
        else:
            acc_ref = scr[0]
            k = pl.program_id(2)

            @pl.when(k == 0)
            def _():
                acc_ref[...] = p

            @pl.when(k > 0)
            def _():
                acc_ref[...] += p

            @pl.when(k == nk - 1)
            def _():
                finish(acc_ref[...])

    if trans_a:
        a_spec = pl.BlockSpec((tk, tm), lambda i, j, k: (k, i))
    else:
        a_spec = pl.BlockSpec((tm, tk), lambda i, j, k: (i, k))
    if trans_b:
        b_spec = pl.BlockSpec((tn, tk), lambda i, j, k: (j, k))
    else:
        b_spec = pl.BlockSpec((tk, tn), lambda i, j, k: (k, j))
    in_specs = [a_spec, b_spec]
    args = [a, b]
    if has_res:
        in_specs.append(pl.BlockSpec((tm, tn), lambda i, j, k: (i, j)))
        args.append(res)
    return pl.pallas_call(
        body, name=name,
        grid=(m // tm, n // tn, nk),
        in_specs=in_specs,
        out_specs=pl.BlockSpec((tm, tn), lambda i, j, k: (i, j)),
        out_shape=jax.ShapeDtypeStruct((m, n), out_dtype),
        scratch_shapes=[pltpu.VMEM((tm, tn), F32)] if nk > 1 else [],
        compiler_params=_cp("parallel", "parallel", "arbitrary"),
    )(*args)


def _rms_fwd(x, gains, *, name):
    t, d = x.shape
    tr = _pick(t, (512, 256, 128, 8))
    ng = len(gains)

    def body(x_ref, *rest):
        g_refs, n_refs, r_ref = rest[:ng], rest[ng:2 * ng], rest[2 * ng]
        xv = x_ref[...]
        r = lax.rsqrt(jnp.mean(xv * xv, axis=-1, keepdims=True) + EPS)
        xh = xv * r
        for g_ref, n_ref in zip(g_refs, n_refs):
            n_ref[...] = (xh * g_ref[...]).astype(BF16)
        r_ref[...] = r

    row = pl.BlockSpec((tr, d), lambda i: (i, 0))
    gsp = pl.BlockSpec((1, d), lambda i: (0, 0))
    outs = pl.pallas_call(
        body, name=name, grid=(t // tr,),
        in_specs=[row] + [gsp] * ng,
        out_specs=[row] * ng + [pl.BlockSpec((tr, 1), lambda i: (i, 0))],
        out_shape=[jax.ShapeDtypeStruct((t, d), BF16)] * ng + [jax.ShapeDtypeStruct((t, 1), F32)],
        compiler_params=_cp("parallel"),
    )(x, *gains)
    return list(outs[:ng]), outs[ng]


def _rms_bwd(x, r, pairs, dres, *, name, need_dx=True):
    t, d = x.shape
    tr = _pick(t, (512, 256, 128, 8))
    npair = len(pairs)
    has_res = dres is not None

    def body(*refs):
        x_ref, r_ref = refs[:2]
        pr = refs[2:2 + 2 * npair]
        pos = 2 + 2 * npair
        res_ref = None
        if has_res:
            res_ref = refs[pos]
            pos += 1
        dx_ref = None
        if need_dx:
            dx_ref = refs[pos]
            pos += 1
        dg_refs = refs[pos:pos + npair]
        i = pl.program_id(0)
        rv = r_ref[...]
        xh = x_ref[...] * rv
        dx = res_ref[...] if has_res else None
        for k in range(npair):
            dn = pr[2 * k][...].astype(F32)
            g = pr[2 * k + 1][...]
            part = jnp.sum(dn * xh, axis=0, keepdims=True)

            @pl.when(i == 0)
            def _():
                dg_refs[k][...] = part

            @pl.when(i > 0)
            def _():
                dg_refs[k][...] += part

            if need_dx:
                dxh = dn * g
                c = jnp.mean(dxh * xh, axis=-1, keepdims=True)
                term = rv * (dxh - xh * c)
                dx = term if dx is None else dx + term
        if need_dx:
            dx_ref[...] = dx

    row = pl.BlockSpec((tr, d), lambda i: (i, 0))
    gsp = pl.BlockSpec((1, d), lambda i: (0, 0))
    in_specs = [row, pl.BlockSpec((tr, 1), lambda i: (i, 0))]
    args = [x, r]
    for dn, g in pairs:
        in_specs += [row, gsp]
        args += [dn, g]
    if has_res:
        in_specs.append(row)
        args.append(dres)
    out_specs, out_shape = [], []
    if need_dx:
        out_specs.append(row)
        out_shape.append(jax.ShapeDtypeStruct((t, d), F32))
    out_specs += [gsp] * npair
    out_shape += [jax.ShapeDtypeStruct((1, d), F32)] * npair
    outs = pl.pallas_call(
        body, name=name, grid=(t // tr,), in_specs=in_specs, out_specs=out_specs, out_shape=out_shape,
        compiler_params=_cp("arbitrary"),
    )(*args)
    if need_dx:
        return outs[0], list(outs[1:])
    return None, list(outs)


def _loss_head(h, g, tgt, *, name):
    t, d = h.shape
    tr = _pick(t, (512, 256, 128, 8))

    def body(h_ref, g_ref, t_ref, dh_ref, dg_ref, l_ref):
        i = pl.program_id(0)
        xv = h_ref[...]
        gv = g_ref[...]
        r = lax.rsqrt(jnp.mean(xv * xv, axis=-1, keepdims=True) + EPS)
        xh = xv * r
        e = xh * gv - t_ref[...]
        dy = e * (1.0 / d)
        lpart = jnp.sum(e * e, axis=0, keepdims=True)
        gpart = jnp.sum(dy * xh, axis=0, keepdims=True)

        @pl.when(i == 0)
        def _():
            l_ref[...] = lpart
            dg_ref[...] = gpart

        @pl.when(i > 0)
        def _():
            l_ref[...] += lpart
            dg_ref[...] += gpart

        dxh = dy * gv
        c = jnp.mean(dxh * xh, axis=-1, keepdims=True)
        dh_ref[...] = r * (dxh - xh * c)

    row = pl.BlockSpec((tr, d), lambda i: (i, 0))
    gsp = pl.BlockSpec((1, d), lambda i: (0, 0))
    return pl.pallas_call(
        body, name=name, grid=(t // tr,), in_specs=[row, gsp, row], out_specs=[row, gsp, gsp],
        out_shape=[jax.ShapeDtypeStruct((t, d), F32), jax.ShapeDtypeStruct((1, d), F32),
                   jax.ShapeDtypeStruct((1, d), F32)],
        compiler_params=_cp("arbitrary"),
    )(h, g, tgt)


GRP = 4 * HEAD_DIM
SB_KB = 2 * QBLK
SB_QB = 2 * QBLK


def _head_masks4(shape):
    lane = lax.broadcasted_iota(jnp.int32, shape, 1)
    return [(lane >= HEAD_DIM * h) & (lane < HEAD_DIM * (h + 1)) for h in range(4)]


def _neg_softplus(z):
    nz = -z
    return jnp.minimum(nz, 0.0) - jnp.log(1.0 + jnp.exp(jnp.minimum(z, nz)))


def _stacked_col_minus_row():
    rowi = lax.broadcasted_iota(jnp.int32, (4 * SB_QB, SB_KB), 0)
    coli = lax.broadcasted_iota(jnp.int32, (4 * SB_QB, SB_KB), 1)
    return coli - (rowi & (SB_QB - 1))


def _sb_fwd(proj, after, *, name):
    b, s, _ = proj.shape
    nq = s // SB_QB
    ngrp = SB_WIDTH // GRP

    def body(q_ref, k_ref, v_ref, after_ref, o_ref, r_ref, acc_ref, car_ref):
        i = pl.program_id(2)
        masks = _head_masks4((SB_QB, GRP))
        row = lax.broadcasted_iota(jnp.int32, (SB_KB, SB_KB), 0)
        col = lax.broadcasted_iota(jnp.int32, (SB_KB, SB_KB), 1)
        later_mat = (row > col).astype(BF16)
        col_minus_row = _stacked_col_minus_row()
        qs = q_ref[0] * jnp.asarray(ATT_SCALE, BF16)
        q_stack = jnp.concatenate([jnp.where(mk, qs, jnp.zeros_like(qs)) for mk in masks], axis=0)
        acc_ref[...] = jnp.zeros_like(acc_ref)
        car_ref[...] = jnp.zeros_like(car_ref)

        def process(jb, masked):
            off = pl.multiple_of(jb * SB_KB, SB_KB)
            k2 = k_ref[0, pl.ds(off, SB_KB), :]
            v2 = v_ref[0, pl.ds(off, SB_KB), :]
            z = _dot_nt(q_stack, k2)
            ls = _neg_softplus(z)
            if masked:
                causal = col_minus_row < (i * SB_QB - jb * SB_KB)
                ls = jnp.where(causal, ls, 0.0)
            later = _dot(ls.astype(BF16), later_mat)
            car = car_ref[...]
            w = jnp.exp((z + ls) + later + car)
            if masked:
                w = jnp.where(causal, w, 0.0)
            car_ref[...] = car + jnp.sum(ls, axis=1, keepdims=True)
            acc_ref[...] += _dot(w.astype(BF16), v2)

        top = (i * SB_QB) // SB_KB
        process(top, True)

        def step(jj, carry):
            process(top - 1 - jj, False)
            return carry

        lax.fori_loop(0, top, step, 0)
        o = acc_ref[pl.ds(0, SB_QB), :]
        r = car_ref[pl.ds(0, SB_QB), :]
        for h in range(1, 4):
            o = jnp.where(masks[h], acc_ref[pl.ds(h * SB_QB, SB_QB), :], o)
            r = jnp.where(masks[h], car_ref[pl.ds(h * SB_QB, SB_QB), :], r)
        o_ref[0] = o.astype(o_ref.dtype)
        r_ref[0] = r

    blk = pl.BlockSpec((1, SB_QB, GRP), lambda bb, p, i: (bb, i, p))
    return pl.pallas_call(
        body, name=name, grid=(b, ngrp, nq),
        in_specs=[blk,
                  pl.BlockSpec((1, s, GRP), lambda bb, p, i: (bb, 0, ngrp + p)),
                  pl.BlockSpec((1, s, GRP), lambda bb, p, i: (bb, 0, 2 * ngrp + p)),
                  pl.BlockSpec(memory_space=pl.ANY)],
        out_specs=[blk, blk],
        out_shape=[jax.ShapeDtypeStruct((b, s, SB_WIDTH), BF16), jax.ShapeDtypeStruct((b, s, SB_WIDTH), F32)],
        scratch_shapes=[pltpu.VMEM((4 * SB_QB, GRP), F32), pltpu.VMEM((4 * SB_QB, SB_KB), F32)],
        compiler_params=_cp("parallel", "parallel", "arbitrary"),
    )(proj, proj, proj, after)


def _sb_bwd(proj, dcat, rsum, after, *, name):
    b, s, _ = proj.shape
    nq = s // SB_QB
    ngrp = SB_WIDTH // GRP

    def body(q_ref, k_ref, v_ref, do_ref, r_ref, after_ref, dq_ref, dk_out, dv_out, dq_acc, cp_ref, cg_ref,
             dk_ref, dv_ref):
        i = pl.program_id(2)

        @pl.when(i == 0)
        def _():
            dk_ref[...] = jnp.zeros_like(dk_ref)
            dv_ref[...] = jnp.zeros_like(dv_ref)

        masks = _head_masks4((SB_QB, GRP))
        row = lax.broadcasted_iota(jnp.int32, (SB_KB, SB_KB), 0)
        col = lax.broadcasted_iota(jnp.int32, (SB_KB, SB_KB), 1)
        later_mat = (row > col).astype(BF16)
        excl_mat = (row < col).astype(BF16)
        col_minus_row = _stacked_col_minus_row()
        qs = q_ref[0] * jnp.asarray(ATT_SCALE, BF16)
        do = do_ref[0]
        q_stack = jnp.concatenate([jnp.where(mk, qs, jnp.zeros_like(qs)) for mk in masks], axis=0)
        do_stack = jnp.concatenate([jnp.where(mk, do, jnp.zeros_like(do)) for mk in masks], axis=0)
        rv = r_ref[0]
        r_stack = jnp.concatenate([rv[:, HEAD_DIM * h:HEAD_DIM * h + 1] for h in range(4)], axis=0)
        dq_acc[...] = jnp.zeros_like(dq_acc)
        cp_ref[...] = jnp.zeros_like(cp_ref)
        cg_ref[...] = jnp.zeros_like(cg_ref)

        def process(jb, masked):
            off = pl.multiple_of(jb * SB_KB, SB_KB)
            k2 = k_ref[0, pl.ds(off, SB_KB), :]
            v2 = v_ref[0, pl.ds(off, SB_KB), :]
            z = _dot_nt(q_stack, k2)
            dw = _dot_nt(do_stack, v2)
            ls = _neg_softplus(z)
            lsig = z + ls
            if masked:
                causal = col_minus_row < (i * SB_QB - jb * SB_KB)
                ls = jnp.where(causal, ls, 0.0)
            later = _dot(ls.astype(BF16), later_mat)
            cpv = cp_ref[...] + jnp.sum(ls, axis=1, keepdims=True)
            cp_ref[...] = cpv
            w = jnp.exp(lsig + ((r_stack - cpv) + later))
            if masked:
                w = jnp.where(causal, w, 0.0)
            g = dw * w
            gpre = _dot(g.astype(BF16), excl_mat)
            cgv = cg_ref[...]
            cg_ref[...] = cgv + jnp.sum(g, axis=1, keepdims=True)
            dz = g - jnp.exp(lsig) * (g + (gpre + cgv))
            if masked:
                dz = jnp.where(causal, dz, 0.0)
            dzb = dz.astype(BF16)
            dq_acc[...] += _dot(dzb, k2)
            dk_ref[0, pl.ds(off, SB_KB), :] += _dot_tn(dzb, q_stack)
            dv_ref[0, pl.ds(off, SB_KB), :] += _dot_tn(w.astype(BF16), do_stack)

        top = (i * SB_QB) // SB_KB

        def step(jb, carry):
            process(jb, False)
            return carry

        lax.fori_loop(0, top, step, 0)
        process(top, True)
        dq = dq_acc[pl.ds(0, SB_QB), :]
        for h in range(1, 4):
            dq = jnp.where(masks[h], dq_acc[pl.ds(h * SB_QB, SB_QB), :], dq)
        dq_ref[0] = (dq * ATT_SCALE).astype(dq_ref.dtype)

        @pl.when(i == nq - 1)
        def _():
            dk_out[...] = dk_ref[...].astype(dk_out.dtype)
            dv_out[...] = dv_ref[...].astype(dv_out.dtype)

    blk = pl.BlockSpec((1, SB_QB, GRP), lambda bb, p, i: (bb, i, p))
    seq = pl.BlockSpec((1, s, GRP), lambda bb, p, i: (bb, 0, p))
    return pl.pallas_call(
        body, name=name, grid=(b, ngrp, nq),
        in_specs=[blk,
                  pl.BlockSpec((1, s, GRP), lambda bb, p, i: (bb, 0, ngrp + p)),
                  pl.BlockSpec((1, s, GRP), lambda bb, p, i: (bb, 0, 2 * ngrp + p)),
                  blk, blk, pl.BlockSpec(memory_space=pl.ANY)],
        out_specs=[blk, seq, seq],
        out_shape=[jax.ShapeDtypeStruct((b, s, SB_WIDTH), BF16)] * 3,
        scratch_shapes=[pltpu.VMEM((4 * SB_QB, GRP), F32), pltpu.VMEM((4 * SB_QB, SB_KB), F32),
                        pltpu.VMEM((4 * SB_QB, SB_KB), F32), pltpu.VMEM((1, s, GRP), F32),
                        pltpu.VMEM((1, s, GRP), F32)],
        compiler_params=_cp("parallel", "parallel", "arbitrary"),
    )(proj, proj, proj, dcat, rsum, after)


def _band_bias(slopes_scaled):
    a = lax.broadcasted_iota(jnp.int32, (QBLK, 2 * QBLK), 0)
    bcol = lax.broadcasted_iota(jnp.int32, (QBLK, 2 * QBLK), 1)
    delta = a + QBLK - bcol
    in_band = (delta >= 0) & (delta <= QBLK)
    dist = delta.astype(F32)
    bias = jnp.concatenate([(-sl) * dist for sl in slopes_scaled], axis=0)
    return jnp.concatenate([in_band] * 4, axis=0), jnp.concatenate([bcol >= QBLK] * 4, axis=0), bias


def _stack_heads(x, masks):
    return jnp.concatenate([jnp.where(mk, x, jnp.zeros_like(x)) for mk in masks], axis=0)


def _unstack_heads(x, masks):
    out = jnp.broadcast_to(x[0:QBLK], (QBLK, GRP))
    for h in range(1, 4):
        out = jnp.where(masks[h], x[h * QBLK:(h + 1) * QBLK], out)
    return out


PER_HEAD = 8


def _head_column(x):
    return jnp.concatenate([x[:, h:h + 1] for h in range(4)], axis=0)


def _head_lanes(col):
    lane = lax.broadcasted_iota(jnp.int32, (QBLK, PER_HEAD), 1)
    out = jnp.zeros((QBLK, PER_HEAD), F32)
    for h in range(4):
        out = jnp.where(lane == h, col[h * QBLK:(h + 1) * QBLK], out)
    return out


def _spread_heads(x8, rows):
    masks = _head_masks4((rows, GRP))
    out = jnp.broadcast_to(x8[:, 0:1], (rows, GRP))
    for h in range(1, 4):
        out = jnp.where(masks[h], x8[:, h:h + 1], out)
    return out


def _attn_units(n, l, banded):
    nsub = 2 if l % (2 * QBLK) == 0 else 1
    nseq = 4 if (banded and nsub == 1 and n % 4 == 0) else 1
    return nseq, nsub


def _attn_specs(banded, nseq, nsub, q_lane_blk, k_lane_blk, v_lane_blk):
    tq = nsub * QBLK
    qs = pl.BlockSpec((nseq, tq, GRP), lambda n, i: (n, i, q_lane_blk))
    if banded:
        ks = [pl.BlockSpec((nseq, QBLK, GRP), lambda n, i: (n, jnp.maximum(nsub * i - 1, 0), k_lane_blk)),
              pl.BlockSpec((nseq, tq, GRP), lambda n, i: (n, i, k_lane_blk))]
        vs = [pl.BlockSpec((nseq, QBLK, GRP), lambda n, i: (n, jnp.maximum(nsub * i - 1, 0), v_lane_blk)),
              pl.BlockSpec((nseq, tq, GRP), lambda n, i: (n, i, v_lane_blk))]
    else:
        ks = [pl.BlockSpec((nseq, 2 * QBLK, GRP), lambda n, i: (n, 0, k_lane_blk))]
        vs = [pl.BlockSpec((nseq, 2 * QBLK, GRP), lambda n, i: (n, 0, v_lane_blk))]
    return qs, ks, vs


def _attn_fwd(q, k, v, *, name, banded, slopes_scaled=None, q_lane_blk=0, k_lane_blk=0, v_lane_blk=0):
    n, l, _ = q.shape
    nseq, nsub = _attn_units(n, l, banded)
    units = [(sq, u) for sq in range(nseq) for u in range(nsub)]
    tq = nsub * QBLK
    nkv = 2 if banded else 1

    def body(*refs):
        q_ref = refs[0]
        k_refs = refs[1:1 + nkv]
        v_refs = refs[1 + nkv:1 + 2 * nkv]
        o_ref, lse_ref = refs[1 + 2 * nkv:]
        step = pl.program_id(1)
        masks = _head_masks4((QBLK, GRP))
        if banded:
            in_band, is_cur, bias = _band_bias(slopes_scaled)
        scs, v2s = [], []
        for sq, u in units:
            qs = q_ref[sq, u * QBLK:(u + 1) * QBLK, :] * jnp.asarray(ATT_SCALE, BF16)
            if banded:
                kall = jnp.concatenate([k_refs[0][sq], k_refs[1][sq]], axis=0)
                vall = jnp.concatenate([v_refs[0][sq], v_refs[1][sq]], axis=0)
                k2 = kall[u * QBLK:(u + 2) * QBLK]
                v2s.append(vall[u * QBLK:(u + 2) * QBLK])
            else:
                k2 = k_refs[0][sq]
                v2s.append(v_refs[0][sq])
            scs.append(_dot_nt(_stack_heads(qs, masks), k2))
        ps, dens, lses = [], [], []
        for j, (sq, u) in enumerate(units):
            sc = scs[j]
            if banded:
                valid = in_band & (is_cur | (step * nsub + u > 0))
                sc = jnp.where(valid, sc + bias, NEG_BIG)
            m = jnp.max(sc, axis=-1, keepdims=True)
            p = jnp.exp(sc - m)
            den = jnp.sum(p, axis=-1, keepdims=True)
            ps.append(p.astype(BF16))
            dens.append(den)
            lses.append(m + jnp.log(den))
        ohs = [_dot(ps[j], v2s[j]) for j in range(len(units))]
        for j, (sq, u) in enumerate(units):
            o_ref[sq, u * QBLK:(u + 1) * QBLK, :] = _unstack_heads(ohs[j] / dens[j], masks).astype(o_ref.dtype)
            lse_ref[sq, u * QBLK:(u + 1) * QBLK, :] = _head_lanes(lses[j])

    qs, ks, vs = _attn_specs(banded, nseq, nsub, q_lane_blk, k_lane_blk, v_lane_blk)
    ob = pl.BlockSpec((nseq, tq, GRP), lambda nn, i: (nn, i, 0))
    return pl.pallas_call(
        body, name=name, grid=(n // nseq, l // tq),
        in_specs=[qs] + ks + vs, out_specs=[ob, pl.BlockSpec((nseq, tq, PER_HEAD), lambda nn, i: (nn, i, 0))],
        out_shape=[jax.ShapeDtypeStruct((n, l, GRP), BF16), jax.ShapeDtypeStruct((n, l, PER_HEAD), F32)],
        compiler_params=_cp("parallel", "arbitrary"),
    )(q, *([k] * nkv), *([v] * nkv))


def _attn_bwd(q, k, v, do, lse, delta, *, name, banded, slopes_scaled=None, q_lane_blk=0, k_lane_blk=0,
              v_lane_blk=0, do_lane_blk=0):
    n, l, _ = q.shape
    nseq, nsub = _attn_units(n, l, banded)
    units = [(sq, u) for sq in range(nseq) for u in range(nsub)]
    tq = nsub * QBLK
    nsteps = l // tq
    nkv = 2 if banded else 1
    lk = l if banded else 2 * QBLK

    def body(*refs):
        q_ref = refs[0]
        k_refs = refs[1:1 + nkv]
        v_refs = refs[1 + nkv:1 + 2 * nkv]
        do_ref, lse_ref, dl_ref, dq_ref, dk_out, dv_out, dk_ref, dv_ref = refs[1 + 2 * nkv:]
        step = pl.program_id(1)

        @pl.when(step == 0)
        def _():
            dk_ref[...] = jnp.zeros_like(dk_ref)
            dv_ref[...] = jnp.zeros_like(dv_ref)

        masks = _head_masks4((QBLK, GRP))
        if banded:
            in_band, is_cur, bias = _band_bias(slopes_scaled)
        q_st, do_st, k2s, scs, dps = [], [], [], [], []
        for j, (sq, u) in enumerate(units):
            rows = slice(u * QBLK, (u + 1) * QBLK)
            if banded:
                kall = jnp.concatenate([k_refs[0][sq], k_refs[1][sq]], axis=0)
                vall = jnp.concatenate([v_refs[0][sq], v_refs[1][sq]], axis=0)
                k2s.append(kall[u * QBLK:(u + 2) * QBLK])
                v2 = vall[u * QBLK:(u + 2) * QBLK]
            else:
                k2s.append(k_refs[0][sq])
                v2 = v_refs[0][sq]
            qs = q_ref[sq, rows, :] * jnp.asarray(ATT_SCALE, BF16)
            q_st.append(_stack_heads(qs, masks))
            do_st.append(_stack_heads(do_ref[sq, rows, :], masks))
            scs.append(_dot_nt(q_st[j], k2s[j]))
            dps.append(_dot_nt(do_st[j], v2))
        pbs, dss = [], []
        for j, (sq, u) in enumerate(units):
            rows = slice(u * QBLK, (u + 1) * QBLK)
            sc = scs[j]
            if banded:
                valid = in_band & (is_cur | (step * nsub + u > 0))
                sc = jnp.where(valid, sc + bias, NEG_BIG)
            p = jnp.exp(sc - _head_column(lse_ref[sq, rows, :]))
            pbs.append(p.astype(BF16))
            dss.append((p * (dps[j] - _head_column(dl_ref[sq, rows, :]))).astype(BF16))
        dqs = [_dot(dss[j], k2s[j]) for j in range(len(units))]
        dk2s = [_dot_tn(dss[j], q_st[j]) for j in range(len(units))]
        dv2s = [_dot_tn(pbs[j], do_st[j]) for j in range(len(units))]
        for j, (sq, u) in enumerate(units):
            dq_ref[sq, u * QBLK:(u + 1) * QBLK, :] = (_unstack_heads(dqs[j], masks) * ATT_SCALE).astype(dq_ref.dtype)
        if banded:
            for j, (sq, u) in enumerate(units):
                i = step * nsub + u
                cur = pl.multiple_of(i * QBLK, QBLK)
                dk_ref[sq, pl.ds(cur, QBLK), :] += dk2s[j][QBLK:]
                dv_ref[sq, pl.ds(cur, QBLK), :] += dv2s[j][QBLK:]

                @pl.when(i > 0)
                def _():
                    prev = pl.multiple_of((i - 1) * QBLK, QBLK)
                    dk_ref[sq, pl.ds(prev, QBLK), :] += dk2s[j][:QBLK]
                    dv_ref[sq, pl.ds(prev, QBLK), :] += dv2s[j][:QBLK]
        else:
            dk_ref[0] += functools.reduce(jnp.add, dk2s)
            dv_ref[0] += functools.reduce(jnp.add, dv2s)

        @pl.when(step == nsteps - 1)
        def _():
            dk_out[...] = dk_ref[...].astype(dk_out.dtype)
            dv_out[...] = dv_ref[...].astype(dv_out.dtype)

    qs, ks, vs = _attn_specs(banded, nseq, nsub, q_lane_blk, k_lane_blk, v_lane_blk)
    ob = pl.BlockSpec((nseq, tq, GRP), lambda nn, i: (nn, i, 0))
    stat = pl.BlockSpec((nseq, tq, PER_HEAD), lambda nn, i: (nn, i, 0))
    dos = pl.BlockSpec((nseq, tq, GRP), lambda nn, i: (nn, i, do_lane_blk))
    kvb = pl.BlockSpec((nseq, lk, GRP), lambda nn, i: (nn, 0, 0))
    return pl.pallas_call(
        body, name=name, grid=(n // nseq, nsteps),
        in_specs=[qs] + ks + vs + [dos, stat, stat], out_specs=[ob, kvb, kvb],
        out_shape=[jax.ShapeDtypeStruct((n, l, GRP), BF16), jax.ShapeDtypeStruct((n, lk, GRP), BF16),
                   jax.ShapeDtypeStruct((n, lk, GRP), BF16)],
        scratch_shapes=[pltpu.VMEM((nseq, lk, GRP), F32), pltpu.VMEM((nseq, lk, GRP), F32)],
        compiler_params=_cp("parallel", "arbitrary"),
    )(q, *([k] * nkv), *([v] * nkv), do, lse, delta)


def _attn_delta(do, o, *, name, lane_blks):
    t, _ = do.shape
    tr = _pick(t, (512, 256, 128, 8))
    ng = len(lane_blks)

    def body(*refs):
        do_refs, o_refs, d_ref = refs[:ng], refs[ng:2 * ng], refs[2 * ng]
        ra = lax.broadcasted_iota(jnp.int32, (GRP, LANE), 0) // HEAD_DIM
        rb = lax.broadcasted_iota(jnp.int32, (GRP, LANE), 1)
        head_sum = (ra == rb).astype(BF16)
        prod = None
        for a_ref, b_ref in zip(do_refs, o_refs):
            term = a_ref[...].astype(F32) * b_ref[...].astype(F32)
            prod = term if prod is None else prod + term
        d_ref[...] = _dot_split(prod, head_sum)[:, :PER_HEAD]

    specs = [pl.BlockSpec((tr, GRP), functools.partial(lambda i, lb: (i, lb), lb=lb)) for lb in lane_blks]
    return pl.pallas_call(
        body, name=name, grid=(t // tr,), in_specs=specs + specs,
        out_specs=pl.BlockSpec((tr, PER_HEAD), lambda i: (i, 0)),
        out_shape=jax.ShapeDtypeStruct((t, PER_HEAD), F32),
        compiler_params=_cp("parallel"),
    )(*([do] * ng), *([o] * ng))


def _dil_combine(os, lses, *, name):
    t, _ = os[0].shape
    tr = _pick(t, (512, 256, 128, 8))
    ng = len(os)

    def body(*refs):
        o_refs, l_refs = refs[:ng], refs[ng:2 * ng]
        out_ref, lse_ref = refs[2 * ng:]
        ls = [r[...] for r in l_refs]
        m = functools.reduce(jnp.maximum, ls)
        tot = None
        for lv in ls:
            e = jnp.exp(lv - m)
            tot = e if tot is None else tot + e
        lse = m + jnp.log(tot)
        for g in range(ng):
            alpha = _spread_heads(jnp.exp(ls[g] - lse), tr)
            out_ref[:, GRP * g:GRP * (g + 1)] = (o_refs[g][...].astype(F32) * alpha).astype(out_ref.dtype)
        lse_ref[...] = lse

    sp = pl.BlockSpec((tr, GRP), lambda i: (i, 0))
    st = pl.BlockSpec((tr, PER_HEAD), lambda i: (i, 0))
    return pl.pallas_call(
        body, name=name, grid=(t // tr,), in_specs=[sp] * ng + [st] * ng,
        out_specs=[pl.BlockSpec((tr, GRP * ng), lambda i: (i, 0)), st],
        out_shape=[jax.ShapeDtypeStruct((t, GRP * ng), BF16), jax.ShapeDtypeStruct((t, PER_HEAD), F32)],
        compiler_params=_cp("parallel"),
    )(*os, *lses)


FFN_LB = 256
FFN_ROWS = 64
HALO = 16


def _conv_chunk(u_ref, w, ci):
    r0 = pl.multiple_of(ci * FFN_ROWS, FFN_ROWS)
    cur = u_ref[0, pl.ds(r0, FFN_ROWS), :].astype(F32)
    p0 = pl.multiple_of(jnp.maximum(r0 - HALO, 0), HALO)
    prev = u_ref[0, pl.ds(p0, HALO), :].astype(F32)
    prev = jnp.where(ci > 0, prev, 0.0)
    rowi = lax.broadcasted_iota(jnp.int32, (8, cur.shape[1]), 0)
    r1 = pltpu.roll(cur, 1, 0)
    r2 = pltpu.roll(cur, 2, 0)
    s1 = jnp.concatenate([jnp.where(rowi == 0, prev[HALO - 1:HALO], r1[0:8]), r1[8:]], axis=0)
    s2 = jnp.concatenate([jnp.where(rowi == 0, prev[HALO - 2:HALO - 1],
                                    jnp.where(rowi == 1, prev[HALO - 1:HALO], r2[0:8])), r2[8:]], axis=0)
    c = w[0:1] * s2
    c = c + w[1:2] * s1
    c = c + w[2:3] * cur
    return c, cur, s1, s2


def _ffn_mid_fwd(u, wconv, *, name):
    b, s, f2 = u.shape
    f = f2 // 2
    nlb = f // FFN_LB

    def body(ua_ref, ug_ref, wa_ref, wg_ref, h_ref):
        wa = wa_ref[...]
        wg = wg_ref[...]

        def step(ci, carry):
            ca = _conv_chunk(ua_ref, wa, ci)[0]
            cg = _conv_chunk(ug_ref, wg, ci)[0]
            r0 = pl.multiple_of(ci * FFN_ROWS, FFN_ROWS)
            h_ref[0, pl.ds(r0, FFN_ROWS), :] = (cg * jax.nn.sigmoid(cg) * ca).astype(h_ref.dtype)
            return carry

        lax.fori_loop(0, s // FFN_ROWS, step, 0)

    return pl.pallas_call(
        body, name=name, grid=(nlb, b),
        in_specs=[pl.BlockSpec((1, s, FFN_LB), lambda l, bb: (bb, 0, l)),
                  pl.BlockSpec((1, s, FFN_LB), lambda l, bb: (bb, 0, nlb + l)),
                  pl.BlockSpec((3, FFN_LB), lambda l, bb: (0, l)),
                  pl.BlockSpec((3, FFN_LB), lambda l, bb: (0, nlb + l))],
        out_specs=pl.BlockSpec((1, s, FFN_LB), lambda l, bb: (bb, 0, l)),
        out_shape=jax.ShapeDtypeStruct((b, s, f), BF16),
        compiler_params=_cp("parallel", "parallel"),
    )(u, u, wconv, wconv)


def _ffn_mid_bwd(u, wconv, dh, *, name):
    b, s, f2 = u.shape
    f = f2 // 2
    nlb = f // FFN_LB
    nchunk = s // FFN_ROWS

    def body(ua_ref, ug_ref, wa_ref, wg_ref, dh_ref, dua_ref, dug_ref, dwa_ref, dwg_ref):
        bb = pl.program_id(1)
        wa = wa_ref[...]
        wg = wg_ref[...]
        rowi = lax.broadcasted_iota(jnp.int32, (8, FFN_LB), 0)
        last = FFN_ROWS - 8

        def conv_transpose(dc, nxt, w):
            r1 = pltpu.roll(dc, FFN_ROWS - 1, 0)
            r2 = pltpu.roll(dc, FFN_ROWS - 2, 0)
            n1 = jnp.concatenate([r1[:last], jnp.where(rowi == 7, nxt[0:1], r1[last:])], axis=0)
            n2 = jnp.concatenate([r2[:last], jnp.where(rowi == 6, nxt[0:1],
                                                       jnp.where(rowi == 7, nxt[1:2], r2[last:]))], axis=0)
            return w[2:3] * dc + w[1:2] * n1 + w[0:1] * n2

        def step(t, carry):
            ci = nchunk - 1 - t
            nxt_a, nxt_g = carry[0], carry[1]
            r0 = pl.multiple_of(ci * FFN_ROWS, FFN_ROWS)
            ca, cura, s1a, s2a = _conv_chunk(ua_ref, wa, ci)
            cg, curg, s1g, s2g = _conv_chunk(ug_ref, wg, ci)
            dhv = dh_ref[0, pl.ds(r0, FFN_ROWS), :].astype(F32)
            sg = jax.nn.sigmoid(cg)
            da = dhv * (cg * sg)
            dg = dhv * ca * (sg * (1.0 + cg * (1.0 - sg)))
            dua_ref[0, pl.ds(r0, FFN_ROWS), :] = conv_transpose(da, nxt_a, wa).astype(dua_ref.dtype)
            dug_ref[0, pl.ds(r0, FFN_ROWS), :] = conv_transpose(dg, nxt_g, wg).astype(dug_ref.dtype)
            red = lambda x: jnp.sum(x, axis=0, keepdims=True)
            parts = (red(da * s2a), red(da * s1a), red(da * cura), red(dg * s2g), red(dg * s1g), red(dg * curg))
            return (da[0:8], dg[0:8]) + tuple(c + p for c, p in zip(carry[2:], parts))

        zero = jnp.zeros((1, FFN_LB), F32)
        zero8 = jnp.zeros((8, FFN_LB), F32)
        taps = lax.fori_loop(0, nchunk, step, (zero8, zero8) + (zero,) * 6)[2:]

        @pl.when(bb == 0)
        def _():
            for k in range(3):
                dwa_ref[k:k + 1, :] = taps[k]
                dwg_ref[k:k + 1, :] = taps[3 + k]

        @pl.when(bb > 0)
        def _():
            for k in range(3):
                dwa_ref[k:k + 1, :] += taps[k]
                dwg_ref[k:k + 1, :] += taps[3 + k]

    seq_a = pl.BlockSpec((1, s, FFN_LB), lambda l, bb: (bb, 0, l))
    seq_g = pl.BlockSpec((1, s, FFN_LB), lambda l, bb: (bb, 0, nlb + l))
    wsp = pl.BlockSpec((3, FFN_LB), lambda l, bb: (0, l))
    return pl.pallas_call(
        body, name=name, grid=(nlb, b),
        in_specs=[seq_a, seq_g, wsp, pl.BlockSpec((3, FFN_LB), lambda l, bb: (0, nlb + l)), seq_a],
        out_specs=[seq_a, seq_a, wsp, wsp],
        out_shape=[jax.ShapeDtypeStruct((b, s, f), BF16), jax.ShapeDtypeStruct((b, s, f), BF16),
                   jax.ShapeDtypeStruct((3, f), F32), jax.ShapeDtypeStruct((3, f), F32)],
        compiler_params=_cp("parallel", "arbitrary"),
    )(u, u, wconv, wconv, dh)


def _adam_math(w, g, m, v):
    m2 = ADAM_B1 * m + (1.0 - ADAM_B1) * g
    v2 = ADAM_B2 * v + (1.0 - ADAM_B2) * (g * g)
    m_hat = m2 / (1.0 - ADAM_B1 ** ADAM_STEP)
    v_hat = v2 / (1.0 - ADAM_B2 ** ADAM_STEP)
    delta = -ADAM_LR * (m_hat / (jnp.sqrt(v_hat) + ADAM_EPS) + ADAM_WD * w)
    return delta, m2, v2


def _adam(w, g, m, v, *, name):
    r, c = w.shape
    tr = _pick(r, (256, 128, 88, 64, 32, 16, 8))

    def body(w_ref, g_ref, m_ref, v_ref, d_ref, m2_ref, v2_ref):
        d, m2, v2 = _adam_math(w_ref[...], g_ref[...], m_ref[...], v_ref[...])
        d_ref[...] = d
        m2_ref[...] = m2
        v2_ref[...] = v2

    sp = pl.BlockSpec((tr, c), lambda i: (i, 0))
    return pl.pallas_call(
        body, name=name, grid=(r // tr,), in_specs=[sp] * 4, out_specs=[sp] * 3,
        out_shape=[jax.ShapeDtypeStruct((r, c), F32)] * 3,
        compiler_params=_cp("parallel"),
    )(w, g, m, v)


def _adam_small(quads, *, name):
    nq = len(quads)

    def body(*refs):
        ins, outs = refs[:4 * nq], refs[4 * nq:]
        for k in range(nq):
            w_ref, g_ref, m_ref, v_ref = ins[4 * k:4 * k + 4]
            d, m2, v2 = _adam_math(w_ref[...], g_ref[...], m_ref[...], v_ref[...])
            outs[3 * k][...] = d
            outs[3 * k + 1][...] = m2
            outs[3 * k + 2][...] = v2

    flat = [a for q in quads for a in q]
    out_shape = [jax.ShapeDtypeStruct(q[0].shape, F32) for q in quads for _ in range(3)]
    vm = pl.BlockSpec(memory_space=pltpu.VMEM)
    outs = pl.pallas_call(
        body, name=name, in_specs=[vm] * len(flat), out_specs=[vm] * len(out_shape), out_shape=out_shape,
        compiler_params=pltpu.CompilerParams(vmem_limit_bytes=VMEM_LIMIT_BYTES),
    )(*flat)
    return [tuple(outs[3 * k:3 * k + 3]) for k in range(nq)]


def _mesh_pos():
    return lax.axis_index("x"), lax.axis_index("y"), lax.axis_index("c")


def _flip(v, bit):
    return 1 - v if bit else v


def _all_gather_hbm(xl, *, name):
    r, c = xl.shape

    def body(x_ref, out_ref, send_sems, recv_sems, local_sem):
        x, y, cc = _mesh_pos()
        me, sibling = (x, y, cc), (x, y, 1 - cc)
        chips = [(1 - x, y), (x, 1 - y), (1 - x, 1 - y)]

        def rows(px, py, pc):
            return out_ref.at[pl.ds((4 * px + 2 * py + pc) * r, r), :]

        def copy(k, block, to, src=None):
            return pltpu.make_async_remote_copy(
                src_ref=rows(*block) if src is None else src, dst_ref=rows(*block),
                send_sem=send_sems.at[k], recv_sem=recv_sems.at[k], device_id=to, device_id_type=MESH_ID)

        mine = pltpu.make_async_copy(x_ref, rows(*me), local_sem)
        mine.start()
        first = [copy(0, me, sibling, src=x_ref)]
        first += [copy(1 + j, me, (*chip, cc), src=x_ref) for j, chip in enumerate(chips)]
        for cp in first:
            cp.start()
        passed = [copy(4 + j, (*chip, cc), sibling) for j, chip in enumerate(chips)]
        for j, chip in enumerate(chips):
            copy(1 + j, (*chip, cc), me).wait_recv()
            passed[j].start()
        copy(0, sibling, me).wait_recv()
        for j, chip in enumerate(chips):
            copy(4 + j, (*chip, 1 - cc), me).wait_recv()
        for cp in first + passed:
            cp.wait_send()
        mine.wait()

    hbm = pl.BlockSpec(memory_space=pltpu.HBM)
    return pl.pallas_call(
        body, name=name, in_specs=[hbm], out_specs=hbm,
        out_shape=jax.ShapeDtypeStruct((N_DEV * r, c), xl.dtype),
        scratch_shapes=[pltpu.SemaphoreType.DMA((7,)), pltpu.SemaphoreType.DMA((7,)), pltpu.SemaphoreType.DMA],
    )(xl)


def _all_reduce_small(xl, *, name):
    r, c = xl.shape

    def body(x_ref, sum_ref, all_ref, send_sems, recv_sems, local_sem):
        x, y, cc = _mesh_pos()
        me, sibling = (x, y, cc), (x, y, 1 - cc)
        chips = [(1 - x, y), (x, 1 - y), (1 - x, 1 - y)]

        def rows(px, py, pc):
            return all_ref.at[pl.ds((4 * px + 2 * py + pc) * r, r), :]

        def copy(k, block, to, src=None):
            return pltpu.make_async_remote_copy(
                src_ref=rows(*block) if src is None else src, dst_ref=rows(*block),
                send_sem=send_sems.at[k], recv_sem=recv_sems.at[k], device_id=to, device_id_type=MESH_ID)

        mine = pltpu.make_async_copy(x_ref, rows(*me), local_sem)
        mine.start()
        first = [copy(0, me, sibling, src=x_ref)]
        first += [copy(1 + j, me, (*chip, cc), src=x_ref) for j, chip in enumerate(chips)]
        for cp in first:
            cp.start()
        passed = [copy(4 + j, (*chip, cc), sibling) for j, chip in enumerate(chips)]
        for j, chip in enumerate(chips):
            copy(1 + j, (*chip, cc), me).wait_recv()
            passed[j].start()
        copy(0, sibling, me).wait_recv()
        for j, chip in enumerate(chips):
            copy(4 + j, (*chip, 1 - cc), me).wait_recv()
        for cp in first + passed:
            cp.wait_send()
        mine.wait()
        tot = all_ref[pl.ds(0, r), :]
        for dd in range(1, N_DEV):
            tot = tot + all_ref[pl.ds(dd * r, r), :]
        sum_ref[...] = tot

    vm = pl.BlockSpec(memory_space=pltpu.VMEM)
    return pl.pallas_call(
        body, name=name, in_specs=[vm], out_specs=[vm, vm],
        out_shape=[jax.ShapeDtypeStruct((r, c), F32), jax.ShapeDtypeStruct((N_DEV * r, c), F32)],
        scratch_shapes=[pltpu.SemaphoreType.DMA((7,)), pltpu.SemaphoreType.DMA((7,)), pltpu.SemaphoreType.DMA],
    )(xl)[0]


N_PEERS = N_DEV - 1
_HBM = pl.BlockSpec(memory_space=pltpu.HBM)
_SEM = pl.BlockSpec(memory_space=pltpu.SEMAPHORE)


def _peer_list(x, y, cc):
    return [(_flip(x, rel & 4), _flip(y, rel & 2), _flip(cc, rel & 1)) for rel in range(1, N_DEV)]


def _dev_index(p):
    return 4 * p[0] + 2 * p[1] + p[2]


def _split_copy(src_ref, land_ref, sems, k, peer, me, gather, landing_of):
    if gather:
        r = src_ref.shape[0]
        src = src_ref
        dst = land_ref.at[pl.ds(_dev_index(landing_of) * r, r), :]
    else:
        src = src_ref.at[_dev_index(peer)]
        dst = land_ref.at[_dev_index(landing_of)]
    return pltpu.make_async_remote_copy(src_ref=src, dst_ref=dst, send_sem=sems[k], recv_sem=sems[N_PEERS + k],
                                        device_id=peer, device_id_type=MESH_ID)


def _exchange_start(src, land, *, name, gather):
    def body(src_ref, land_ref, *rest):
        sems = rest[:2 * N_PEERS]
        token = rest[2 * N_PEERS + 2]
        x, y, cc = _mesh_pos()
        me = (x, y, cc)
        for k, peer in enumerate(_peer_list(x, y, cc)):
            _split_copy(src_ref, land_ref, sems, k, peer, me, gather, landing_of=me).start()
        token[...] = jnp.zeros_like(token)

    outs = pl.pallas_call(
        body, name=name,
        out_shape=tuple([pltpu.SemaphoreType.DMA(())] * (2 * N_PEERS)) + (
            pltpu.HBM(src.shape, src.dtype), pltpu.HBM(land.shape, land.dtype),
            jax.ShapeDtypeStruct((8, LANE), F32)),
        in_specs=(_HBM, _HBM),
        out_specs=tuple([_SEM] * (2 * N_PEERS)) + (_HBM, _HBM, pl.BlockSpec(memory_space=pltpu.VMEM)),
        input_output_aliases={0: 2 * N_PEERS, 1: 2 * N_PEERS + 1},
        compiler_params=pltpu.CompilerParams(has_side_effects=pltpu.SideEffectType.DATAFLOW_SIDE_EFFECTING),
    )(pltpu.with_memory_space_constraint(src, pltpu.HBM), pltpu.with_memory_space_constraint(land, pltpu.HBM))
    return outs[:2 * N_PEERS], outs[2 * N_PEERS], outs[2 * N_PEERS + 1], outs[2 * N_PEERS + 2]


def _gather_start(local, me, *, name):
    rows, cols = local.shape
    land = lax.dynamic_update_slice(lax.empty((N_DEV * rows, cols), local.dtype), local, (me * rows, 0))
    return _exchange_start(local, land, name=name, gather=True)


def _exchange_wait(sems, src_thru, land_thru, after, *, name, gather):
    def body(src_ref, land_ref, *rest):
        sem_refs = rest[:2 * N_PEERS]
        x, y, cc = _mesh_pos()
        me = (x, y, cc)
        for k, peer in enumerate(_peer_list(x, y, cc)):
            cp = _split_copy(src_ref, land_ref, sem_refs, k, peer, me, gather, landing_of=peer)
            cp.wait_send()
            cp.wait_recv()

    outs = pl.pallas_call(
        body, name=name,
        out_shape=(pltpu.HBM(src_thru.shape, src_thru.dtype), pltpu.HBM(land_thru.shape, land_thru.dtype)),
        in_specs=(_HBM, _HBM) + tuple([_SEM] * (2 * N_PEERS)) + (pl.BlockSpec(memory_space=pl.ANY),),
        out_specs=(_HBM, _HBM), input_output_aliases={0: 0, 1: 1},
        compiler_params=pltpu.CompilerParams(has_side_effects=pltpu.SideEffectType.DATAFLOW_SIDE_EFFECTING),
    )(src_thru, land_thru, *sems, after)
    return outs[1]


def _sum_blocks(recv, own, after, *, name):
    nd, r, c = recv.shape
    tr = _pick(r, (448, 256, 128, 64, 32, 16))

    def body(x_ref, own_ref, after_ref, o_ref):
        x, y, cc = _mesh_pos()
        me = 4 * x + 2 * y + cc
        tot = None
        for dd in range(nd):
            term = jnp.where(me == dd, own_ref[0], x_ref[dd]).astype(F32)
            tot = term if tot is None else tot + term
        o_ref[...] = tot

    return pl.pallas_call(
        body, name=name, grid=(r // tr,),
        in_specs=[pl.BlockSpec((nd, tr, c), lambda i: (0, i, 0)), pl.BlockSpec((1, tr, c), lambda i: (0, i, 0)),
                  pl.BlockSpec(memory_space=pl.ANY)],
        out_specs=pl.BlockSpec((tr, c), lambda i: (i, 0)),
        out_shape=jax.ShapeDtypeStruct((r, c), F32),
        compiler_params=_cp("parallel"),
    )(recv, own, after)


SHARD_KIND = {"a_w_in": "col", "a_w_out": "row", "a_w_mem_kv": "row", "a_ffn_up": "col", "a_ffn_down": "row",
              "w_kv_shared": "col", "b_w_in": "row", "b_w_out": "row", "b_w_mem_kv": "row", "b_ffn_up": "col",
              "b_ffn_down": "row"}
EARLY_WEIGHTS = ("a_w_in", "a_w_mem_kv")
FFN_UP_WEIGHTS = ("a_ffn_up", "b_ffn_up")
WIDE_WEIGHTS = tuple(nm for nm in SHARD_KIND if nm not in EARLY_WEIGHTS + FFN_UP_WEIGHTS)
LATE_WEIGHTS = WIDE_WEIGHTS + FFN_UP_WEIGHTS


def _as2d(a):
    return a.reshape(a.shape[-2], a.shape[-1]) if a.ndim >= 2 else a.reshape(1, a.shape[0])


def _pack_local(shards):
    return jnp.concatenate([_as2d(s).astype(BF16).reshape(-1, PACK_COLS) for s in shards], axis=0)


def _unpack_full(gathered, names, shapes):
    out = {}
    r0 = 0
    for name in names:
        rows, cols = shapes[name]
        nr = rows * cols // PACK_COLS
        blk = gathered[:, r0:r0 + nr, :].reshape(N_DEV, rows, cols)
        if SHARD_KIND[name] == "row":
            out[name] = blk.reshape(N_DEV * rows, cols)
        else:
            out[name] = blk.transpose(1, 0, 2).reshape(rows, N_DEV * cols)
        r0 += nr
    return out


def _pack_grads(grads, names, shapes):
    parts = []
    for name in names:
        rows, cols = shapes[name]
        g = grads[name]
        if SHARD_KIND[name] == "row":
            blk = g.reshape(N_DEV, rows, cols)
        else:
            blk = g.reshape(rows, N_DEV, cols).transpose(1, 0, 2)
        parts.append(blk.astype(BF16).reshape(N_DEV, rows * cols // PACK_COLS, PACK_COLS))
    return jnp.concatenate(parts, axis=1)


def _unpack_local(gsum, names, shapes):
    out = {}
    r0 = 0
    for name in names:
        rows, cols = shapes[name]
        nr = rows * cols // PACK_COLS
        out[name] = gsum[r0:r0 + nr].reshape(rows, cols)
        r0 += nr
    return out


def _by_residue(t, d):
    if d == 1:
        return t
    b, s, c = t.shape
    return t.reshape(b, s // d, d, c).transpose(0, 2, 1, 3).reshape(b * d, s // d, c)


def _from_residue(t, d, b):
    if d == 1:
        return t
    n, l, c = t.shape
    return t.reshape(b, d, l, c).transpose(0, 2, 1, 3).reshape(b, l * d, c)


def _alibi_slopes():
    return [2.0 ** (-ALIBI_MAX_BIAS * (i + 1) / N_DIL_HEADS) for i in range(N_DIL_HEADS)]


def _conv_ffn_fwd(xin, gain, w_up, wconv, w_down, tag, b, s):
    (n,), r = _rms_fwd(xin, [gain], name=f"{tag}_rms_ffn")
    u = _mm(n, w_up, name=f"{tag}_up", out_dtype=BF16).reshape(b, s, -1)
    hmid = _ffn_mid_fwd(u, wconv, name=f"{tag}_ffn_mid").reshape(b * s, -1)
    xout = _mm(hmid, w_down, name=f"{tag}_down", out_dtype=F32, res=xin)
    return xout, (n, r, u, hmid)


def _conv_ffn_bwd(dxout, xin, gain, saved, w_up, wconv, w_down, tag, b, s):
    n, r, u, hmid = saved
    f = hmid.shape[1]
    dhmid = _mm(dxout, w_down, name=f"{tag}_d_hmid", out_dtype=BF16, trans_b=True)
    g_down = _mm(hmid, dxout, name=f"{tag}_g_down", out_dtype=BF16, trans_a=True)
    du_a, du_g, gc_a, gc_g = _ffn_mid_bwd(u, wconv, dhmid.reshape(b, s, f), name=f"{tag}_ffn_mid_bwd")
    du_a = du_a.reshape(b * s, f)
    du_g = du_g.reshape(b * s, f)
    dn = _mm(du_a, w_up[:, :f], name=f"{tag}_d_n_a", out_dtype=F32, trans_b=True)
    dn = _mm(du_g, w_up[:, f:], name=f"{tag}_d_n_g", out_dtype=F32, res=dn, trans_b=True)
    g_up = jnp.concatenate([_mm(n, du_a, name=f"{tag}_g_up_a", out_dtype=BF16, trans_a=True),
                            _mm(n, du_g, name=f"{tag}_g_up_g", out_dtype=BF16, trans_a=True)], axis=1)
    dxin, (g_gain,) = _rms_bwd(xin, r, [(dn, gain)], dxout, name=f"{tag}_rms_ffn_bwd")
    return dxin, g_up, g_down, jnp.concatenate([gc_a, gc_g], axis=1), g_gain


def _mem_kv_fwd(mem2d, gain, w_mem_kv, tag, b):
    (nm,), rm = _rms_fwd(mem2d, [gain], name=f"{tag}_rms_mem")
    kvm = _mm(nm, w_mem_kv, name=f"{tag}_mem_kv", out_dtype=BF16)
    return kvm.reshape(b, -1, 2 * MEM_WIDTH), (nm, rm)


def _mem_kv_bwd(dk, dv, mem2d, gain, saved, w_mem_kv, tag):
    nm, rm = saved
    dkvm = jnp.concatenate([dk, dv], axis=-1).reshape(-1, 2 * MEM_WIDTH)
    dnm = _mm(dkvm, w_mem_kv, name=f"{tag}_d_nm", out_dtype=F32, trans_b=True)
    g_w = _mm(nm, dkvm, name=f"{tag}_g_mem_kv", out_dtype=BF16, trans_a=True)
    _, (g_gain,) = _rms_bwd(mem2d, rm, [(dnm, gain)], None, name=f"{tag}_rms_mem_bwd", need_dx=False)
    return g_w, g_gain


def kernel(x, mem, a_norm_attn, a_w_in, a_w_out, a_norm_mem, a_w_mem_kv, a_norm_ffn, a_ffn_up, a_ffn_conv, a_ffn_down, kv_norm, w_kv_shared, b_norm_attn, b_w_in, b_w_out, b_norm_mem, b_w_mem_kv, b_norm_ffn, b_ffn_up, b_ffn_conv, b_ffn_down, final_norm, loss_target, m_a_norm_attn, m_a_w_in, m_a_w_out, m_a_norm_mem, m_a_w_mem_kv, m_a_norm_ffn, m_a_ffn_up, m_a_ffn_conv, m_a_ffn_down, m_kv_norm, m_w_kv_shared, m_b_norm_attn, m_b_w_in, m_b_w_out, m_b_norm_mem, m_b_w_mem_kv, m_b_norm_ffn, m_b_ffn_up, m_b_ffn_conv, m_b_ffn_down, m_final_norm, v_a_norm_attn, v_a_w_in, v_a_w_out, v_a_norm_mem, v_a_w_mem_kv, v_a_norm_ffn, v_a_ffn_up, v_a_ffn_conv, v_a_ffn_down, v_kv_norm, v_w_kv_shared, v_b_norm_attn, v_b_w_in, v_b_w_out, v_b_norm_mem, v_b_w_mem_kv, v_b_norm_ffn, v_b_ffn_up, v_b_ffn_conv, v_b_ffn_down, v_final_norm):
    names = ["a_norm_attn", "a_w_in", "a_w_out", "a_norm_mem", "a_w_mem_kv", "a_norm_ffn", "a_ffn_up",
             "a_ffn_conv", "a_ffn_down", "kv_norm", "w_kv_shared", "b_norm_attn", "b_w_in", "b_w_out",
             "b_norm_mem", "b_w_mem_kv", "b_norm_ffn", "b_ffn_up", "b_ffn_conv", "b_ffn_down", "final_norm"]
    wl = dict(zip(names, [a_norm_attn, a_w_in, a_w_out, a_norm_mem, a_w_mem_kv, a_norm_ffn, a_ffn_up,
                          a_ffn_conv, a_ffn_down, kv_norm, w_kv_shared, b_norm_attn, b_w_in, b_w_out,
                          b_norm_mem, b_w_mem_kv, b_norm_ffn, b_ffn_up, b_ffn_conv, b_ffn_down, final_norm]))
    ml = dict(zip(names, [m_a_norm_attn, m_a_w_in, m_a_w_out, m_a_norm_mem, m_a_w_mem_kv, m_a_norm_ffn,
                          m_a_ffn_up, m_a_ffn_conv, m_a_ffn_down, m_kv_norm, m_w_kv_shared, m_b_norm_attn,
                          m_b_w_in, m_b_w_out, m_b_norm_mem, m_b_w_mem_kv, m_b_norm_ffn, m_b_ffn_up,
                          m_b_ffn_conv, m_b_ffn_down, m_final_norm]))
    vl = dict(zip(names, [v_a_norm_attn, v_a_w_in, v_a_w_out, v_a_norm_mem, v_a_w_mem_kv, v_a_norm_ffn,
                          v_a_ffn_up, v_a_ffn_conv, v_a_ffn_down, v_kv_norm, v_w_kv_shared, v_b_norm_attn,
                          v_b_w_in, v_b_w_out, v_b_norm_mem, v_b_w_mem_kv, v_b_norm_ffn, v_b_ffn_up,
                          v_b_ffn_conv, v_b_ffn_down, v_final_norm]))
    b, s, d = x.shape
    t = b * s
    my_x, my_y, my_c = _mesh_pos()
    me = 4 * my_x + 2 * my_y + my_c

    shapes = {nm: _as2d(wl[nm]).shape for nm in SHARD_KIND}
    early_local = _pack_local([wl[nm] for nm in EARLY_WEIGHTS])
    early_all = _all_gather_hbm(early_local, name="gather_early").reshape(N_DEV, early_local.shape[0], PACK_COLS)
    wf = _unpack_full(early_all, EARLY_WEIGHTS, shapes)
    wide_local = _pack_local([wl[nm] for nm in WIDE_WEIGHTS])
    up_local = jnp.concatenate([_as2d(wl[nm]).astype(BF16) for nm in FFN_UP_WEIGHTS], axis=0)
    gw_sems, gw_src, gw_land, gw_token = _gather_start(wide_local, me, name="gather_wide_start")
    gu_sems, gu_src, gu_land, gu_token = _gather_start(up_local, me, name="gather_up_start")

    sharded_small = ["a_norm_attn", "a_norm_mem", "a_norm_ffn", "a_ffn_conv", "b_ffn_conv"]
    small_flat = jnp.concatenate([wl[nm].reshape(-1) for nm in sharded_small])
    n_small = small_flat.shape[0]
    small_rows = -(-n_small // (8 * LANE)) * 8
    small_local = jnp.pad(small_flat, (0, small_rows * LANE - n_small)).reshape(small_rows, LANE)
    small_all = _all_gather_hbm(small_local, name="gather_small").reshape(N_DEV, small_rows * LANE)
    sfull = {}
    r0 = 0
    for nm in sharded_small:
        rows, cols = _as2d(wl[nm]).shape
        blk = small_all[:, r0:r0 + rows * cols].reshape(N_DEV, rows, cols)
        sfull[nm] = blk.transpose(1, 0, 2).reshape(rows, N_DEV * cols)
        r0 += rows * cols
    gain = {nm: sfull[nm] for nm in ("a_norm_attn", "a_norm_mem", "a_norm_ffn")}
    for nm in ("kv_norm", "b_norm_attn", "b_norm_mem", "b_norm_ffn", "final_norm"):
        gain[nm] = _as2d(wl[nm])
    conv_a, conv_b = sfull["a_ffn_conv"], sfull["b_ffn_conv"]

    x2d = x.reshape(t, d)
    mem2d = mem.reshape(-1, d)
    tgt2d = loss_target.reshape(t, d)
    qmem_blk_a = 3 * SB_WIDTH // GRP
    qmem_blk_b = DIL_WIDTH // GRP

    (n1,), r1 = _rms_fwd(x2d, [gain["a_norm_attn"]], name="a_rms_attn")
    proj_a = _mm(n1, wf["a_w_in"], name="a_in", out_dtype=BF16).reshape(b, s, -1)
    kvm_a, mem_saved_a = _mem_kv_fwd(mem2d, gain["a_norm_mem"], wf["a_w_mem_kv"], "a", b)
    o_sb, rsum = _sb_fwd(proj_a, gw_token + gu_token, name="a_sb_fwd")
    o_mem_a, lse_mem_a = _attn_fwd(proj_a, kvm_a, kvm_a, name="a_mem_fwd", banded=False,
                                   q_lane_blk=qmem_blk_a, k_lane_blk=0, v_lane_blk=1)
    wide_all = _exchange_wait(gw_sems, gw_src, gw_land, rsum, name="gather_wide_wait", gather=True)
    up_all = _exchange_wait(gu_sems, gu_src, gu_land, rsum, name="gather_up_wait", gather=True)
    wf.update(_unpack_full(wide_all.reshape(N_DEV, wide_local.shape[0], PACK_COLS), WIDE_WEIGHTS, shapes))
    up_rows, up_cols = shapes[FFN_UP_WEIGHTS[0]]
    up_all = up_all.reshape(N_DEV, len(FFN_UP_WEIGHTS), up_rows, up_cols)
    for k, nm in enumerate(FFN_UP_WEIGHTS):
        wf[nm] = up_all[:, k].transpose(1, 0, 2).reshape(up_rows, N_DEV * up_cols)
    cat_a = jnp.concatenate([o_sb, o_mem_a], axis=-1).reshape(t, d)
    x1 = _mm(cat_a, wf["a_w_out"], name="a_out", out_dtype=F32, res=x2d)
    xa, ffn_saved_a = _conv_ffn_fwd(x1, gain["a_norm_ffn"], wf["a_ffn_up"], conv_a, wf["a_ffn_down"], "a", b, s)

    (nk, n3), r3 = _rms_fwd(xa, [gain["kv_norm"], gain["b_norm_attn"]], name="b_rms_attn")
    kvsh = _mm(nk, wf["w_kv_shared"], name="kv_shared", out_dtype=BF16).reshape(b, s, -1)
    proj_b = _mm(n3, wf["b_w_in"], name="b_in", out_dtype=BF16).reshape(b, s, -1)
    kvm_b, mem_saved_b = _mem_kv_fwd(mem2d, gain["b_norm_mem"], wf["b_w_mem_kv"], "b", b)
    slopes = _alibi_slopes()
    dil_q, dil_k, dil_v, dil_o, dil_lse, dil_slopes = [], [], [], [], [], []
    for g, (_, dil) in enumerate(DIL_GROUPS):
        qg = _by_residue(proj_b[:, :, GRP * g:GRP * (g + 1)], dil)
        kg = _by_residue(kvsh[:, :, GRP * g:GRP * (g + 1)], dil)
        vg = _by_residue(kvsh[:, :, DIL_WIDTH + GRP * g:DIL_WIDTH + GRP * (g + 1)], dil)
        sl = [slopes[4 * g + h] * dil for h in range(4)]
        og, lg = _attn_fwd(qg, kg, vg, name=f"b_dil{g}_fwd", banded=True, slopes_scaled=sl)
        dil_q.append(qg)
        dil_k.append(kg)
        dil_v.append(vg)
        dil_slopes.append(sl)
        dil_o.append(_from_residue(og, dil, b).reshape(t, GRP))
        dil_lse.append(_from_residue(lg, dil, b).reshape(t, PER_HEAD))
    o_dil, lse_joint = _dil_combine(dil_o, dil_lse, name="b_dil_combine")
    o_mem_b, lse_mem_b = _attn_fwd(proj_b, kvm_b, kvm_b, name="b_mem_fwd", banded=False,
                                   q_lane_blk=qmem_blk_b, k_lane_blk=0, v_lane_blk=1)
    cat_b = jnp.concatenate([o_dil, o_mem_b.reshape(t, MEM_WIDTH)], axis=-1)
    x3 = _mm(cat_b, wf["b_w_out"], name="b_out", out_dtype=F32, res=xa)
    xb, ffn_saved_b = _conv_ffn_fwd(x3, gain["b_norm_ffn"], wf["b_ffn_up"], conv_b, wf["b_ffn_down"], "b", b, s)

    dxb, g_final, loss_vec = _loss_head(xb, gain["final_norm"], tgt2d, name="loss_head")

    grads = {}
    sgrads = {"final_norm": g_final}
    dx3, grads["b_ffn_up"], grads["b_ffn_down"], sgrads["b_ffn_conv"], sgrads["b_norm_ffn"] = _conv_ffn_bwd(
        dxb, x3, gain["b_norm_ffn"], ffn_saved_b, wf["b_ffn_up"], conv_b, wf["b_ffn_down"], "b", b, s)
    dcat_b = _mm(dx3, wf["b_w_out"], name="b_d_cat", out_dtype=BF16, trans_b=True)
    grads["b_w_out"] = _mm(cat_b, dx3, name="b_g_out", out_dtype=BF16, trans_a=True)
    dcat_b3 = dcat_b.reshape(b, s, d)
    delta_mem_b = _attn_delta(dcat_b, cat_b, name="b_mem_delta", lane_blks=[qmem_blk_b]).reshape(b, s, PER_HEAD)
    dq_mem_b, dkm_b, dvm_b = _attn_bwd(proj_b, kvm_b, kvm_b, dcat_b3, lse_mem_b, delta_mem_b, name="b_mem_bwd",
                                       banded=False, q_lane_blk=qmem_blk_b, k_lane_blk=0, v_lane_blk=1,
                                       do_lane_blk=qmem_blk_b)
    delta_dil = _attn_delta(dcat_b, cat_b, name="b_dil_delta", lane_blks=[0, 1, 2]).reshape(b, s, PER_HEAD)
    lse_joint3 = lse_joint.reshape(b, s, PER_HEAD)
    dq_parts, dk_parts, dv_parts = [], [], []
    for g, (_, dil) in enumerate(DIL_GROUPS):
        dog = _by_residue(dcat_b3[:, :, GRP * g:GRP * (g + 1)], dil)
        lg = _by_residue(lse_joint3, dil)
        dg = _by_residue(delta_dil, dil)
        dqg, dkg, dvg = _attn_bwd(dil_q[g], dil_k[g], dil_v[g], dog, lg, dg, name=f"b_dil{g}_bwd", banded=True,
                                  slopes_scaled=dil_slopes[g])
        dq_parts.append(_from_residue(dqg, dil, b))
        dk_parts.append(_from_residue(dkg, dil, b))
        dv_parts.append(_from_residue(dvg, dil, b))
    dproj_b = jnp.concatenate(dq_parts + [dq_mem_b], axis=-1).reshape(t, d)
    dn3 = _mm(dproj_b, wf["b_w_in"], name="b_d_n", out_dtype=F32, trans_b=True)
    grads["b_w_in"] = _mm(n3, dproj_b, name="b_g_in", out_dtype=BF16, trans_a=True)
    grads["b_w_mem_kv"], sgrads["b_norm_mem"] = _mem_kv_bwd(dkm_b, dvm_b, mem2d, gain["b_norm_mem"], mem_saved_b,
                                                           wf["b_w_mem_kv"], "b")
    dkvsh = jnp.concatenate(dk_parts + dv_parts, axis=-1).reshape(t, 2 * DIL_WIDTH).astype(BF16)
    dnk = _mm(dkvsh, wf["w_kv_shared"], name="kv_d_n", out_dtype=F32, trans_b=True)
    grads["w_kv_shared"] = _mm(nk, dkvsh, name="kv_g", out_dtype=BF16, trans_a=True)
    dxa, (sgrads["kv_norm"], sgrads["b_norm_attn"]) = _rms_bwd(
        xa, r3, [(dnk, gain["kv_norm"]), (dn3, gain["b_norm_attn"])], dx3, name="b_rms_attn_bwd")

    dx1, grads["a_ffn_up"], grads["a_ffn_down"], sgrads["a_ffn_conv"], sgrads["a_norm_ffn"] = _conv_ffn_bwd(
        dxa, x1, gain["a_norm_ffn"], ffn_saved_a, wf["a_ffn_up"], conv_a, wf["a_ffn_down"], "a", b, s)
    dcat_a = _mm(dx1, wf["a_w_out"], name="a_d_cat", out_dtype=BF16, trans_b=True)
    grads["a_w_out"] = _mm(cat_a, dx1, name="a_g_out", out_dtype=BF16, trans_a=True)
    dcat_a3 = dcat_a.reshape(b, s, d)
    delta_mem_a = _attn_delta(dcat_a, cat_a, name="a_mem_delta", lane_blks=[qmem_blk_b]).reshape(b, s, PER_HEAD)
    dq_mem_a, dkm_a, dvm_a = _attn_bwd(proj_a, kvm_a, kvm_a, dcat_a3, lse_mem_a, delta_mem_a, name="a_mem_bwd",
                                       banded=False, q_lane_blk=qmem_blk_a, k_lane_blk=0, v_lane_blk=1,
                                       do_lane_blk=qmem_blk_b)
    wide_grads = _pack_grads(grads, WIDE_WEIGHTS, shapes)
    up_grads = jnp.concatenate([grads[nm].reshape(up_rows, N_DEV, up_cols).transpose(1, 0, 2)
                                for nm in FFN_UP_WEIGHTS], axis=1)
    own_wide = lax.dynamic_slice(wide_grads, (me, 0, 0), (1,) + wide_grads.shape[1:])
    own_up = lax.dynamic_slice(up_grads, (me, 0, 0), (1,) + up_grads.shape[1:])
    xw_sems, xw_src, xw_land, xw_token = _exchange_start(wide_grads, lax.empty(wide_grads.shape, BF16),
                                                         name="grads_wide_start", gather=False)
    xu_sems, xu_src, xu_land, xu_token = _exchange_start(up_grads, lax.empty(up_grads.shape, BF16),
                                                         name="grads_up_start", gather=False)
    dq_sb, dk_sb, dv_sb = _sb_bwd(proj_a, dcat_a3, rsum, xw_token + xu_token, name="a_sb_bwd")
    dproj_a = jnp.concatenate([dq_sb, dk_sb, dv_sb, dq_mem_a], axis=-1).reshape(t, -1)
    dn1 = _mm(dproj_a, wf["a_w_in"], name="a_d_n", out_dtype=F32, trans_b=True)
    grads["a_w_in"] = _mm(n1, dproj_a, name="a_g_in", out_dtype=BF16, trans_a=True)
    grads["a_w_mem_kv"], sgrads["a_norm_mem"] = _mem_kv_bwd(dkm_a, dvm_a, mem2d, gain["a_norm_mem"], mem_saved_a,
                                                           wf["a_w_mem_kv"], "a")
    early_grads = _pack_grads(grads, EARLY_WEIGHTS, shapes)
    own_early = lax.dynamic_slice(early_grads, (me, 0, 0), (1,) + early_grads.shape[1:])
    ee_sems, ee_src, ee_land, ee_token = _exchange_start(early_grads, lax.empty(early_grads.shape, BF16),
                                                         name="grads_early_start", gather=False)
    dx0, (sgrads["a_norm_attn"],) = _rms_bwd(x2d, r1, [(dn1, gain["a_norm_attn"])], dx1, name="a_rms_attn_bwd")
    grad_x = dx0.reshape(b, s, d)

    wide_recv = _exchange_wait(xw_sems, xw_src, xw_land, dx0, name="grads_wide_wait", gather=False)
    up_recv = _exchange_wait(xu_sems, xu_src, xu_land, dx0, name="grads_up_wait", gather=False)
    gl = _unpack_local(_sum_blocks(wide_recv, own_wide, ee_token, name="sum_grads_wide"), WIDE_WEIGHTS, shapes)
    up_sum = _sum_blocks(up_recv, own_up, ee_token, name="sum_grads_up")
    for k, nm in enumerate(FFN_UP_WEIGHTS):
        gl[nm] = up_sum[k * up_rows:(k + 1) * up_rows]

    small_names = ["a_norm_attn", "a_norm_mem", "a_norm_ffn", "kv_norm", "b_norm_attn", "b_norm_mem",
                   "b_norm_ffn", "final_norm", "a_ffn_conv", "b_ffn_conv"]
    small_flat = jnp.concatenate([sgrads[nm].reshape(-1) for nm in small_names] + [loss_vec.reshape(-1)])
    n_flat = small_flat.shape[0]
    red_rows = -(-n_flat // (8 * PACK_COLS)) * 8
    small_pack = jnp.pad(small_flat, (0, red_rows * PACK_COLS - n_flat)).reshape(red_rows, PACK_COLS)
    small_sum = _all_reduce_small(small_pack, name="reduce_small").reshape(-1)
    r0 = 0
    for nm in small_names:
        rows, cols = sgrads[nm].shape
        full = small_sum[r0:r0 + rows * cols].reshape(rows, cols)
        r0 += rows * cols
        if nm in sharded_small:
            lc = cols // N_DEV
            gl[nm] = lax.dynamic_slice(full, (0, me * lc), (rows, lc))
        else:
            gl[nm] = full
    loss = (0.5 / d) * jnp.sum(small_sum[r0:r0 + d])

    upd = {}
    for nm in LATE_WEIGHTS:
        upd[nm] = _adam(_as2d(wl[nm]), gl[nm], _as2d(ml[nm]), _as2d(vl[nm]), name=f"adam_{nm}")
    res_small = _adam_small([(_as2d(wl[nm]), gl[nm], _as2d(ml[nm]), _as2d(vl[nm])) for nm in small_names],
                            name="adam_small")
    for nm, r in zip(small_names, res_small):
        upd[nm] = r
    early_recv = _exchange_wait(ee_sems, ee_src, ee_land, upd[LATE_WEIGHTS[-1]][0], name="grads_early_wait",
                                gather=False)
    gl.update(_unpack_local(_sum_blocks(early_recv, own_early, ee_token, name="sum_grads_early"), EARLY_WEIGHTS,
                            shapes))
    for nm in EARLY_WEIGHTS:
        upd[nm] = _adam(_as2d(wl[nm]), gl[nm], _as2d(ml[nm]), _as2d(vl[nm]), name=f"adam_{nm}")

    g_out = [gl[nm].reshape(wl[nm].shape) for nm in names]
    d_out = [upd[nm][0].reshape(wl[nm].shape) for nm in names]
    m_out = [upd[nm][1].reshape(wl[nm].shape) for nm in names]
    v_out = [upd[nm][2].reshape(wl[nm].shape) for nm in names]
    return (loss, grad_x, *g_out, *d_out, *m_out, *v_out)
```

```python
import functools

import jax
import jax.numpy as jnp
from jax import lax
from jax.experimental import pallas as pl
from jax.experimental.pallas import tpu as pltpu

F32 = jnp.float32
BF16 = jnp.bfloat16

N_DEV = 8
HEAD_DIM = 64
N_SB_HEADS = 12
N_DIL_HEADS = 12
DIL_GROUPS = ((128, 1), (512, 4), (2048, 16))
SB_WIDTH = N_SB_HEADS * HEAD_DIM
MEM_WIDTH = 256
DIL_WIDTH = N_DIL_HEADS * HEAD_DIM
ATT_SCALE = HEAD_DIM ** -0.5
EPS = 1e-6
ALIBI_MAX_BIAS = 8.0
NEG_BIG = -1e30

ADAM_LR = 0.001
ADAM_B1 = 0.9
ADAM_B2 = 0.999
ADAM_EPS = 1e-08
ADAM_WD = 0.01
ADAM_STEP = 10

LANE = 128
QBLK = 128
VMEM_LIMIT_BYTES = 48 * 1024 * 1024
PACK_COLS = 1024
MESH_ID = pl.DeviceIdType.MESH


def _cp(*sem):
    return pltpu.CompilerParams(dimension_semantics=sem, vmem_limit_bytes=VMEM_LIMIT_BYTES)


def _pick(n, cands):
    for c in cands:
        if n % c == 0:
            return c
    raise ValueError(f"no tile for {n} in {cands}")


def _dot(a, b):
    return jnp.dot(a, b, preferred_element_type=F32)


def _dot_nt(a, b):
    return lax.dot_general(a, b, (((1,), (1,)), ((), ())), preferred_element_type=F32)


def _dot_tn(a, b):
    return lax.dot_general(a, b, (((0,), (0,)), ((), ())), preferred_element_type=F32)


def _dot_split(x, u):
    hi = x.astype(BF16)
    lo = (x - hi.astype(F32)).astype(BF16)
    return _dot(hi, u) + _dot(lo, u)


def _mm(a, b, *, name, out_dtype, res=None, trans_a=False, trans_b=False):
    assert not (trans_a and trans_b)
    if trans_a:
        kdim, m = a.shape
    else:
        m, kdim = a.shape
    if trans_b:
        n, kb = b.shape
    else:
        kb, n = b.shape
    assert kb == kdim, (a.shape, b.shape)
    if trans_a:
        tm = _pick(m, (1408, 1024, 512, 256, 128))
        tn = _pick(n, (1024, 1280, 1408, 768, 512, 256, 128))
        tk = _pick(kdim, (1024, 512, 256))
    else:
        tm = _pick(m, (1024, 512, 256, 128))
        tk = kdim if kdim <= 2816 else _pick(kdim, (2048, 1536, 1408, 1280, 1024, 512))
        tn = _pick(n, (512, 256, 128) if tk > 2048 else (1408, 1280, 1024, 768, 512, 256, 128))
    nk = kdim // tk
    has_res = res is not None

    def body(*refs):
        if has_res:
            a_ref, b_ref, r_ref, o_ref = refs[:4]
            scr = refs[4:]
        else:
            a_ref, b_ref, o_ref = refs[:3]
            r_ref = None
            scr = refs[3:]
        av = a_ref[...].astype(BF16)
        bv = b_ref[...].astype(BF16)
        if trans_a:
            p = _dot_tn(av, bv)
        elif trans_b:
            p = _dot_nt(av, bv)
        else:
            p = _dot(av, bv)

        def finish(acc):
            if has_res:
                acc = acc + r_ref[...]
            o_ref[...] = acc.astype(o_ref.dtype)

        if nk == 1:
            finish(p)
        else:
            acc_ref = scr[0]
            k = pl.program_id(2)

            @pl.when(k == 0)
            def _():
                acc_ref[...] = p

            @pl.when(k > 0)
            def _():
                acc_ref[...] += p

            @pl.when(k == nk - 1)
            def _():
                finish(acc_ref[...])

    if trans_a:
        a_spec = pl.BlockSpec((tk, tm), lambda i, j, k: (k, i))
    else:
        a_spec = pl.BlockSpec((tm, tk), lambda i, j, k: (i, k))
    if trans_b:
        b_spec = pl.BlockSpec((tn, tk), lambda i, j, k: (j, k))
    else:
        b_spec = pl.BlockSpec((tk, tn), lambda i, j, k: (k, j))
    in_specs = [a_spec, b_spec]
    args = [a, b]
    if has_res:
        in_specs.append(pl.BlockSpec((tm, tn), lambda i, j, k: (i, j)))
        args.append(res)
    return pl.pallas_call(
        body, name=name,
        grid=(m // tm, n // tn, nk),
        in_specs=in_specs,
        out_specs=pl.BlockSpec((tm, tn), lambda i, j, k: (i, j)),
        out_shape=jax.ShapeDtypeStruct((m, n), out_dtype),
        scratch_shapes=[pltpu.VMEM((tm, tn), F32)] if nk > 1 else [],
        compiler_params=_cp("parallel", "parallel", "arbitrary"),
    )(*args)


def _rms_fwd(x, gains, *, name):
    t, d = x.shape
    tr = _pick(t, (512, 256, 128, 8))
    ng = len(gains)

    def body(x_ref, *rest):
        g_refs, n_refs, r_ref = rest[:ng], rest[ng:2 * ng], rest[2 * ng]
        xv = x_ref[...]
        r = lax.rsqrt(jnp.mean(xv * xv, axis=-1, keepdims=True) + EPS)
        xh = xv * r
        for g_ref, n_ref in zip(g_refs, n_refs):
            n_ref[...] = (xh * g_ref[...]).astype(BF16)
        r_ref[...] = r

    row = pl.BlockSpec((tr, d), lambda i: (i, 0))
    gsp = pl.BlockSpec((1, d), lambda i: (0, 0))
    outs = pl.pallas_call(
        body, name=name, grid=(t // tr,),
        in_specs=[row] + [gsp] * ng,
        out_specs=[row] * ng + [pl.BlockSpec((tr, 1), lambda i: (i, 0))],
        out_shape=[jax.ShapeDtypeStruct((t, d), BF16)] * ng + [jax.ShapeDtypeStruct((t, 1), F32)],
        compiler_params=_cp("parallel"),
    )(x, *gains)
    return list(outs[:ng]), outs[ng]


def _rms_bwd(x, r, pairs, dres, *, name, need_dx=True):
    t, d = x.shape
    tr = _pick(t, (512, 256, 128, 8))
    npair = len(pairs)
    has_res = dres is not None

    def body(*refs):
        x_ref, r_ref = refs[:2]
        pr = refs[2:2 + 2 * npair]
        pos = 2 + 2 * npair
        res_ref = None
        if has_res:
            res_ref = refs[pos]
            pos += 1
        dx_ref = None
        if need_dx:
            dx_ref = refs[pos]
            pos += 1
        dg_refs = refs[pos:pos + npair]
        i = pl.program_id(0)
        rv = r_ref[...]
        xh = x_ref[...] * rv
        dx = res_ref[...] if has_res else None
        for k in range(npair):
            dn = pr[2 * k][...].astype(F32)
            g = pr[2 * k + 1][...]
            part = jnp.sum(dn * xh, axis=0, keepdims=True)

            @pl.when(i == 0)
            def _():
                dg_refs[k][...] = part

            @pl.when(i > 0)
            def _():
                dg_refs[k][...] += part

            if need_dx:
                dxh = dn * g
                c = jnp.mean(dxh * xh, axis=-1, keepdims=True)
                term = rv * (dxh - xh * c)
                dx = term if dx is None else dx + term
        if need_dx:
            dx_ref[...] = dx

    row = pl.BlockSpec((tr, d), lambda i: (i, 0))
    gsp = pl.BlockSpec((1, d), lambda i: (0, 0))
    in_specs = [row, pl.BlockSpec((tr, 1), lambda i: (i, 0))]
    args = [x, r]
    for dn, g in pairs:
        in_specs += [row, gsp]
        args += [dn, g]
    if has_res:
        in_specs.append(row)
        args.append(dres)
    out_specs, out_shape = [], []
    if need_dx:
        out_specs.append(row)
        out_shape.append(jax.ShapeDtypeStruct((t, d), F32))
    out_specs += [gsp] * npair
    out_shape += [jax.ShapeDtypeStruct((1, d), F32)] * npair
    outs = pl.pallas_call(
        body, name=name, grid=(t // tr,), in_specs=in_specs, out_specs=out_specs, out_shape=out_shape,
        compiler_params=_cp("arbitrary"),
    )(*args)
    if need_dx:
        return outs[0], list(outs[1:])
    return None, list(outs)


def _loss_head(h, g, tgt, *, name):
    t, d = h.shape
    tr = _pick(t, (512, 256, 128, 8))

    def body(h_ref, g_ref, t_ref, dh_ref, dg_ref, l_ref):
        i = pl.program_id(0)
        xv = h_ref[...]
        gv = g_ref[...]
        r = lax.rsqrt(jnp.mean(xv * xv, axis=-1, keepdims=True) + EPS)
        xh = xv * r
        e = xh * gv - t_ref[...]
        dy = e * (1.0 / d)
        lpart = jnp.sum(e * e, axis=0, keepdims=True)
        gpart = jnp.sum(dy * xh, axis=0, keepdims=True)

        @pl.when(i == 0)
        def _():
            l_ref[...] = lpart
            dg_ref[...] = gpart

        @pl.when(i > 0)
        def _():
            l_ref[...] += lpart
            dg_ref[...] += gpart

        dxh = dy * gv
        c = jnp.mean(dxh * xh, axis=-1, keepdims=True)
        dh_ref[...] = r * (dxh - xh * c)

    row = pl.BlockSpec((tr, d), lambda i: (i, 0))
    gsp = pl.BlockSpec((1, d), lambda i: (0, 0))
    return pl.pallas_call(
        body, name=name, grid=(t // tr,), in_specs=[row, gsp, row], out_specs=[row, gsp, gsp],
        out_shape=[jax.ShapeDtypeStruct((t, d), F32), jax.ShapeDtypeStruct((1, d), F32),
                   jax.ShapeDtypeStruct((1, d), F32)],
        compiler_params=_cp("arbitrary"),
    )(h, g, tgt)


GRP = 4 * HEAD_DIM
SB_KB = 2 * QBLK
SB_QB = 2 * QBLK


def _head_masks4(shape):
    lane = lax.broadcasted_iota(jnp.int32, shape, 1)
    return [(lane >= HEAD_DIM * h) & (lane < HEAD_DIM * (h + 1)) for h in range(4)]


def _neg_softplus(z):
    nz = -z
    return jnp.minimum(nz, 0.0) - jnp.log(1.0 + jnp.exp(jnp.minimum(z, nz)))


def _stacked_col_minus_row():
    rowi = lax.broadcasted_iota(jnp.int32, (4 * SB_QB, SB_KB), 0)
    coli = lax.broadcasted_iota(jnp.int32, (4 * SB_QB, SB_KB), 1)
    return coli - (rowi & (SB_QB - 1))


def _sb_fwd(proj, after, *, name):
    b, s, _ = proj.shape
    nq = s // SB_QB
    ngrp = SB_WIDTH // GRP

    def body(q_ref, k_ref, v_ref, after_ref, o_ref, r_ref, acc_ref, car_ref):
        i = pl.program_id(2)
        masks = _head_masks4((SB_QB, GRP))
        row = lax.broadcasted_iota(jnp.int32, (SB_KB, SB_KB), 0)
        col = lax.broadcasted_iota(jnp.int32, (SB_KB, SB_KB), 1)
        later_mat = (row > col).astype(BF16)
        col_minus_row = _stacked_col_minus_row()
        qs = q_ref[0] * jnp.asarray(ATT_SCALE, BF16)
        q_stack = jnp.concatenate([jnp.where(mk, qs, jnp.zeros_like(qs)) for mk in masks], axis=0)
        acc_ref[...] = jnp.zeros_like(acc_ref)
        car_ref[...] = jnp.zeros_like(car_ref)

        def process(jb, masked):
            off = pl.multiple_of(jb * SB_KB, SB_KB)
            k2 = k_ref[0, pl.ds(off, SB_KB), :]
            v2 = v_ref[0, pl.ds(off, SB_KB), :]
            z = _dot_nt(q_stack, k2)
            ls = _neg_softplus(z)
            if masked:
                causal = col_minus_row < (i * SB_QB - jb * SB_KB)
                ls = jnp.where(causal, ls, 0.0)
            later = _dot(ls.astype(BF16), later_mat)
            car = car_ref[...]
            w = jnp.exp((z + ls) + later + car)
            if masked:
                w = jnp.where(causal, w, 0.0)
            car_ref[...] = car + jnp.sum(ls, axis=1, keepdims=True)
            acc_ref[...] += _dot(w.astype(BF16), v2)

        top = (i * SB_QB) // SB_KB
        process(top, True)

        def step(jj, carry):
            process(top - 1 - jj, False)
            return carry

        lax.fori_loop(0, top, step, 0)
        o = acc_ref[pl.ds(0, SB_QB), :]
        r = car_ref[pl.ds(0, SB_QB), :]
        for h in range(1, 4):
            o = jnp.where(masks[h], acc_ref[pl.ds(h * SB_QB, SB_QB), :], o)
            r = jnp.where(masks[h], car_ref[pl.ds(h * SB_QB, SB_QB), :], r)
        o_ref[0] = o.astype(o_ref.dtype)
        r_ref[0] = r

    blk = pl.BlockSpec((1, SB_QB, GRP), lambda bb, p, i: (bb, i, p))
    return pl.pallas_call(
        body, name=name, grid=(b, ngrp, nq),
        in_specs=[blk,
                  pl.BlockSpec((1, s, GRP), lambda bb, p, i: (bb, 0, ngrp + p)),
                  pl.BlockSpec((1, s, GRP), lambda bb, p, i: (bb, 0, 2 * ngrp + p)),
                  pl.BlockSpec(memory_space=pl.ANY)],
        out_specs=[blk, blk],
        out_shape=[jax.ShapeDtypeStruct((b, s, SB_WIDTH), BF16), jax.ShapeDtypeStruct((b, s, SB_WIDTH), F32)],
        scratch_shapes=[pltpu.VMEM((4 * SB_QB, GRP), F32), pltpu.VMEM((4 * SB_QB, SB_KB), F32)],
        compiler_params=_cp("parallel", "parallel", "arbitrary"),
    )(proj, proj, proj, after)


def _sb_bwd(proj, dcat, rsum, after, *, name):
    b, s, _ = proj.shape
    nq = s // SB_QB
    ngrp = SB_WIDTH // GRP

    def body(q_ref, k_ref, v_ref, do_ref, r_ref, after_ref, dq_ref, dk_out, dv_out, dq_acc, cp_ref, cg_ref,
             dk_ref, dv_ref):
        i = pl.program_id(2)

        @pl.when(i == 0)
        def _():
            dk_ref[...] = jnp.zeros_like(dk_ref)
            dv_ref[...] = jnp.zeros_like(dv_ref)

        masks = _head_masks4((SB_QB, GRP))
        row = lax.broadcasted_iota(jnp.int32, (SB_KB, SB_KB), 0)
        col = lax.broadcasted_iota(jnp.int32, (SB_KB, SB_KB), 1)
        later_mat = (row > col).astype(BF16)
        excl_mat = (row < col).astype(BF16)
        col_minus_row = _stacked_col_minus_row()
        qs = q_ref[0] * jnp.asarray(ATT_SCALE, BF16)
        do = do_ref[0]
        q_stack = jnp.concatenate([jnp.where(mk, qs, jnp.zeros_like(qs)) for mk in masks], axis=0)
        do_stack = jnp.concatenate([jnp.where(mk, do, jnp.zeros_like(do)) for mk in masks], axis=0)
        rv = r_ref[0]
        r_stack = jnp.concatenate([rv[:, HEAD_DIM * h:HEAD_DIM * h + 1] for h in range(4)], axis=0)
        dq_acc[...] = jnp.zeros_like(dq_acc)
        cp_ref[...] = jnp.zeros_like(cp_ref)
        cg_ref[...] = jnp.zeros_like(cg_ref)

        def process(jb, masked):
            off = pl.multiple_of(jb * SB_KB, SB_KB)
            k2 = k_ref[0, pl.ds(off, SB_KB), :]
            v2 = v_ref[0, pl.ds(off, SB_KB), :]
            z = _dot_nt(q_stack, k2)
            dw = _dot_nt(do_stack, v2)
            ls = _neg_softplus(z)
            lsig = z + ls
            if masked:
                causal = col_minus_row < (i * SB_QB - jb * SB_KB)
                ls = jnp.where(causal, ls, 0.0)
            later = _dot(ls.astype(BF16), later_mat)
            cpv = cp_ref[...] + jnp.sum(ls, axis=1, keepdims=True)
            cp_ref[...] = cpv
            w = jnp.exp(lsig + ((r_stack - cpv) + later))
            if masked:
                w = jnp.where(causal, w, 0.0)
            g = dw * w
            gpre = _dot(g.astype(BF16), excl_mat)
            cgv = cg_ref[...]
            cg_ref[...] = cgv + jnp.sum(g, axis=1, keepdims=True)
            dz = g - jnp.exp(lsig) * (g + (gpre + cgv))
            if masked:
                dz = jnp.where(causal, dz, 0.0)
            dzb = dz.astype(BF16)
            dq_acc[...] += _dot(dzb, k2)
            dk_ref[0, pl.ds(off, SB_KB), :] += _dot_tn(dzb, q_stack)
            dv_ref[0, pl.ds(off, SB_KB), :] += _dot_tn(w.astype(BF16), do_stack)

        top = (i * SB_QB) // SB_KB

        def step(jb, carry):
            process(jb, False)
            return carry

        lax.fori_loop(0, top, step, 0)
        process(top, True)
        dq = dq_acc[pl.ds(0, SB_QB), :]
        for h in range(1, 4):
            dq = jnp.where(masks[h], dq_acc[pl.ds(h * SB_QB, SB_QB), :], dq)
        dq_ref[0] = (dq * ATT_SCALE).astype(dq_ref.dtype)

        @pl.when(i == nq - 1)
        def _():
            dk_out[...] = dk_ref[...].astype(dk_out.dtype)
            dv_out[...] = dv_ref[...].astype(dv_out.dtype)

    blk = pl.BlockSpec((1, SB_QB, GRP), lambda bb, p, i: (bb, i, p))
    seq = pl.BlockSpec((1, s, GRP), lambda bb, p, i: (bb, 0, p))
    return pl.pallas_call(
        body, name=name, grid=(b, ngrp, nq),
        in_specs=[blk,
                  pl.BlockSpec((1, s, GRP), lambda bb, p, i: (bb, 0, ngrp + p)),
                  pl.BlockSpec((1, s, GRP), lambda bb, p, i: (bb, 0, 2 * ngrp + p)),
                  blk, blk, pl.BlockSpec(memory_space=pl.ANY)],
        out_specs=[blk, seq, seq],
        out_shape=[jax.ShapeDtypeStruct((b, s, SB_WIDTH), BF16)] * 3,
        scratch_shapes=[pltpu.VMEM((4 * SB_QB, GRP), F32), pltpu.VMEM((4 * SB_QB, SB_KB), F32),
                        pltpu.VMEM((4 * SB_QB, SB_KB), F32), pltpu.VMEM((1, s, GRP), F32),
                        pltpu.VMEM((1, s, GRP), F32)],
        compiler_params=_cp("parallel", "parallel", "arbitrary"),
    )(proj, proj, proj, dcat, rsum, after)


def _band_bias(slopes_scaled):
    a = lax.broadcasted_iota(jnp.int32, (QBLK, 2 * QBLK), 0)
    bcol = lax.broadcasted_iota(jnp.int32, (QBLK, 2 * QBLK), 1)
    delta = a + QBLK - bcol
    in_band = (delta >= 0) & (delta <= QBLK)
    dist = delta.astype(F32)
    bias = jnp.concatenate([(-sl) * dist for sl in slopes_scaled], axis=0)
    return jnp.concatenate([in_band] * 4, axis=0), jnp.concatenate([bcol >= QBLK] * 4, axis=0), bias


def _stack_heads(x, masks):
    return jnp.concatenate([jnp.where(mk, x, jnp.zeros_like(x)) for mk in masks], axis=0)


def _unstack_heads(x, masks):
    out = jnp.broadcast_to(x[0:QBLK], (QBLK, GRP))
    for h in range(1, 4):
        out = jnp.where(masks[h], x[h * QBLK:(h + 1) * QBLK], out)
    return out


PER_HEAD = 8


def _head_column(x):
    return jnp.concatenate([x[:, h:h + 1] for h in range(4)], axis=0)


def _head_lanes(col):
    lane = lax.broadcasted_iota(jnp.int32, (QBLK, PER_HEAD), 1)
    out = jnp.zeros((QBLK, PER_HEAD), F32)
    for h in range(4):
        out = jnp.where(lane == h, col[h * QBLK:(h + 1) * QBLK], out)
    return out


def _spread_heads(x8, rows):
    masks = _head_masks4((rows, GRP))
    out = jnp.broadcast_to(x8[:, 0:1], (rows, GRP))
    for h in range(1, 4):
        out = jnp.where(masks[h], x8[:, h:h + 1], out)
    return out


def _attn_units(n, l, banded):
    nsub = 4 if l % (4 * QBLK) == 0 else (2 if l % (2 * QBLK) == 0 else 1)
    nseq = 4 if (banded and nsub == 1 and n % 4 == 0) else 1
    return nseq, nsub


def _attn_specs(banded, nseq, nsub, q_lane_blk, k_lane_blk, v_lane_blk):
    tq = nsub * QBLK
    qs = pl.BlockSpec((nseq, tq, GRP), lambda n, i: (n, i, q_lane_blk))
    if banded:
        ks = [pl.BlockSpec((nseq, QBLK, GRP), lambda n, i: (n, jnp.maximum(nsub * i - 1, 0), k_lane_blk)),
              pl.BlockSpec((nseq, tq, GRP), lambda n, i: (n, i, k_lane_blk))]
        vs = [pl.BlockSpec((nseq, QBLK, GRP), lambda n, i: (n, jnp.maximum(nsub * i - 1, 0), v_lane_blk)),
              pl.BlockSpec((nseq, tq, GRP), lambda n, i: (n, i, v_lane_blk))]
    else:
        ks = [pl.BlockSpec((nseq, 2 * QBLK, GRP), lambda n, i: (n, 0, k_lane_blk))]
        vs = [pl.BlockSpec((nseq, 2 * QBLK, GRP), lambda n, i: (n, 0, v_lane_blk))]
    return qs, ks, vs


def _attn_fwd(q, k, v, *, name, banded, slopes_scaled=None, q_lane_blk=0, k_lane_blk=0, v_lane_blk=0):
    n, l, _ = q.shape
    nseq, nsub = _attn_units(n, l, banded)
    units = [(sq, u) for sq in range(nseq) for u in range(nsub)]
    tq = nsub * QBLK
    nkv = 2 if banded else 1

    def body(*refs):
        q_ref = refs[0]
        k_refs = refs[1:1 + nkv]
        v_refs = refs[1 + nkv:1 + 2 * nkv]
        o_ref, lse_ref = refs[1 + 2 * nkv:]
        step = pl.program_id(1)
        masks = _head_masks4((QBLK, GRP))
        if banded:
            in_band, is_cur, bias = _band_bias(slopes_scaled)
        scs, v2s = [], []
        for sq, u in units:
            qs = q_ref[sq, u * QBLK:(u + 1) * QBLK, :] * jnp.asarray(ATT_SCALE, BF16)
            if banded:
                kall = jnp.concatenate([k_refs[0][sq], k_refs[1][sq]], axis=0)
                vall = jnp.concatenate([v_refs[0][sq], v_refs[1][sq]], axis=0)
                k2 = kall[u * QBLK:(u + 2) * QBLK]
                v2s.append(vall[u * QBLK:(u + 2) * QBLK])
            else:
                k2 = k_refs[0][sq]
                v2s.append(v_refs[0][sq])
            scs.append(_dot_nt(_stack_heads(qs, masks), k2))
        ps, dens, lses = [], [], []
        for j, (sq, u) in enumerate(units):
            sc = scs[j]
            if banded:
                valid = in_band & (is_cur | (step * nsub + u > 0))
                sc = jnp.where(valid, sc + bias, NEG_BIG)
            m = jnp.max(sc, axis=-1, keepdims=True)
            p = jnp.exp(sc - m)
            den = jnp.sum(p, axis=-1, keepdims=True)
            ps.append(p.astype(BF16))
            dens.append(den)
            lses.append(m + jnp.log(den))
        ohs = [_dot(ps[j], v2s[j]) for j in range(len(units))]
        for j, (sq, u) in enumerate(units):
            o_ref[sq, u * QBLK:(u + 1) * QBLK, :] = _unstack_heads(ohs[j] / dens[j], masks).astype(o_ref.dtype)
            lse_ref[sq, u * QBLK:(u + 1) * QBLK, :] = _head_lanes(lses[j])

    qs, ks, vs = _attn_specs(banded, nseq, nsub, q_lane_blk, k_lane_blk, v_lane_blk)
    ob = pl.BlockSpec((nseq, tq, GRP), lambda nn, i: (nn, i, 0))
    return pl.pallas_call(
        body, name=name, grid=(n // nseq, l // tq),
        in_specs=[qs] + ks + vs, out_specs=[ob, pl.BlockSpec((nseq, tq, PER_HEAD), lambda nn, i: (nn, i, 0))],
        out_shape=[jax.ShapeDtypeStruct((n, l, GRP), BF16), jax.ShapeDtypeStruct((n, l, PER_HEAD), F32)],
        compiler_params=_cp("parallel", "arbitrary"),
    )(q, *([k] * nkv), *([v] * nkv))


def _attn_bwd(q, k, v, do, lse, delta, *, name, banded, slopes_scaled=None, q_lane_blk=0, k_lane_blk=0,
              v_lane_blk=0, do_lane_blk=0):
    n, l, _ = q.shape
    nseq, nsub = _attn_units(n, l, banded)
    units = [(sq, u) for sq in range(nseq) for u in range(nsub)]
    tq = nsub * QBLK
    nsteps = l // tq
    nkv = 2 if banded else 1
    lk = l if banded else 2 * QBLK

    def body(*refs):
        q_ref = refs[0]
        k_refs = refs[1:1 + nkv]
        v_refs = refs[1 + nkv:1 + 2 * nkv]
        do_ref, lse_ref, dl_ref, dq_ref, dk_out, dv_out, dk_ref, dv_ref = refs[1 + 2 * nkv:]
        step = pl.program_id(1)

        @pl.when(step == 0)
        def _():
            dk_ref[...] = jnp.zeros_like(dk_ref)
            dv_ref[...] = jnp.zeros_like(dv_ref)

        masks = _head_masks4((QBLK, GRP))
        if banded:
            in_band, is_cur, bias = _band_bias(slopes_scaled)
        q_st, do_st, k2s, scs, dps = [], [], [], [], []
        for j, (sq, u) in enumerate(units):
            rows = slice(u * QBLK, (u + 1) * QBLK)
            if banded:
                kall = jnp.concatenate([k_refs[0][sq], k_refs[1][sq]], axis=0)
                vall = jnp.concatenate([v_refs[0][sq], v_refs[1][sq]], axis=0)
                k2s.append(kall[u * QBLK:(u + 2) * QBLK])
                v2 = vall[u * QBLK:(u + 2) * QBLK]
            else:
                k2s.append(k_refs[0][sq])
                v2 = v_refs[0][sq]
            qs = q_ref[sq, rows, :] * jnp.asarray(ATT_SCALE, BF16)
            q_st.append(_stack_heads(qs, masks))
            do_st.append(_stack_heads(do_ref[sq, rows, :], masks))
            scs.append(_dot_nt(q_st[j], k2s[j]))
            dps.append(_dot_nt(do_st[j], v2))
        pbs, dss = [], []
        for j, (sq, u) in enumerate(units):
            rows = slice(u * QBLK, (u + 1) * QBLK)
            sc = scs[j]
            if banded:
                valid = in_band & (is_cur | (step * nsub + u > 0))
                sc = jnp.where(valid, sc + bias, NEG_BIG)
            p = jnp.exp(sc - _head_column(lse_ref[sq, rows, :]))
            pbs.append(p.astype(BF16))
            dss.append((p * (dps[j] - _head_column(dl_ref[sq, rows, :]))).astype(BF16))
        dqs = [_dot(dss[j], k2s[j]) for j in range(len(units))]
        dk2s = [_dot_tn(dss[j], q_st[j]) for j in range(len(units))]
        dv2s = [_dot_tn(pbs[j], do_st[j]) for j in range(len(units))]
        for j, (sq, u) in enumerate(units):
            dq_ref[sq, u * QBLK:(u + 1) * QBLK, :] = (_unstack_heads(dqs[j], masks) * ATT_SCALE).astype(dq_ref.dtype)
        if banded:
            for j, (sq, u) in enumerate(units):
                i = step * nsub + u
                cur = pl.multiple_of(i * QBLK, QBLK)
                dk_ref[sq, pl.ds(cur, QBLK), :] += dk2s[j][QBLK:]
                dv_ref[sq, pl.ds(cur, QBLK), :] += dv2s[j][QBLK:]

                @pl.when(i > 0)
                def _():
                    prev = pl.multiple_of((i - 1) * QBLK, QBLK)
                    dk_ref[sq, pl.ds(prev, QBLK), :] += dk2s[j][:QBLK]
                    dv_ref[sq, pl.ds(prev, QBLK), :] += dv2s[j][:QBLK]
        else:
            dk_ref[0] += functools.reduce(jnp.add, dk2s)
            dv_ref[0] += functools.reduce(jnp.add, dv2s)

        @pl.when(step == nsteps - 1)
        def _():
            dk_out[...] = dk_ref[...].astype(dk_out.dtype)
            dv_out[...] = dv_ref[...].astype(dv_out.dtype)

    qs, ks, vs = _attn_specs(banded, nseq, nsub, q_lane_blk, k_lane_blk, v_lane_blk)
    ob = pl.BlockSpec((nseq, tq, GRP), lambda nn, i: (nn, i, 0))
    stat = pl.BlockSpec((nseq, tq, PER_HEAD), lambda nn, i: (nn, i, 0))
    dos = pl.BlockSpec((nseq, tq, GRP), lambda nn, i: (nn, i, do_lane_blk))
    kvb = pl.BlockSpec((nseq, lk, GRP), lambda nn, i: (nn, 0, 0))
    return pl.pallas_call(
        body, name=name, grid=(n // nseq, nsteps),
        in_specs=[qs] + ks + vs + [dos, stat, stat], out_specs=[ob, kvb, kvb],
        out_shape=[jax.ShapeDtypeStruct((n, l, GRP), BF16), jax.ShapeDtypeStruct((n, lk, GRP), BF16),
                   jax.ShapeDtypeStruct((n, lk, GRP), BF16)],
        scratch_shapes=[pltpu.VMEM((nseq, lk, GRP), F32), pltpu.VMEM((nseq, lk, GRP), F32)],
        compiler_params=_cp("parallel", "arbitrary"),
    )(q, *([k] * nkv), *([v] * nkv), do, lse, delta)


def _attn_delta(do, o, *, name, lane_blks):
    t, _ = do.shape
    tr = _pick(t, (512, 256, 128, 8))
    ng = len(lane_blks)

    def body(*refs):
        do_refs, o_refs, d_ref = refs[:ng], refs[ng:2 * ng], refs[2 * ng]
        ra = lax.broadcasted_iota(jnp.int32, (GRP, LANE), 0) // HEAD_DIM
        rb = lax.broadcasted_iota(jnp.int32, (GRP, LANE), 1)
        head_sum = (ra == rb).astype(BF16)
        prod = None
        for a_ref, b_ref in zip(do_refs, o_refs):
            term = a_ref[...].astype(F32) * b_ref[...].astype(F32)
            prod = term if prod is None else prod + term
        d_ref[...] = _dot_split(prod, head_sum)[:, :PER_HEAD]

    specs = [pl.BlockSpec((tr, GRP), functools.partial(lambda i, lb: (i, lb), lb=lb)) for lb in lane_blks]
    return pl.pallas_call(
        body, name=name, grid=(t // tr,), in_specs=specs + specs,
        out_specs=pl.BlockSpec((tr, PER_HEAD), lambda i: (i, 0)),
        out_shape=jax.ShapeDtypeStruct((t, PER_HEAD), F32),
        compiler_params=_cp("parallel"),
    )(*([do] * ng), *([o] * ng))


def _dil_combine(os, lses, *, name):
    t, _ = os[0].shape
    tr = _pick(t, (512, 256, 128, 8))
    ng = len(os)

    def body(*refs):
        o_refs, l_refs = refs[:ng], refs[ng:2 * ng]
        out_ref, lse_ref = refs[2 * ng:]
        ls = [r[...] for r in l_refs]
        m = functools.reduce(jnp.maximum, ls)
        tot = None
        for lv in ls:
            e = jnp.exp(lv - m)
            tot = e if tot is None else tot + e
        lse = m + jnp.log(tot)
        for g in range(ng):
            alpha = _spread_heads(jnp.exp(ls[g] - lse), tr)
            out_ref[:, GRP * g:GRP * (g + 1)] = (o_refs[g][...].astype(F32) * alpha).astype(out_ref.dtype)
        lse_ref[...] = lse

    sp = pl.BlockSpec((tr, GRP), lambda i: (i, 0))
    st = pl.BlockSpec((tr, PER_HEAD), lambda i: (i, 0))
    return pl.pallas_call(
        body, name=name, grid=(t // tr,), in_specs=[sp] * ng + [st] * ng,
        out_specs=[pl.BlockSpec((tr, GRP * ng), lambda i: (i, 0)), st],
        out_shape=[jax.ShapeDtypeStruct((t, GRP * ng), BF16), jax.ShapeDtypeStruct((t, PER_HEAD), F32)],
        compiler_params=_cp("parallel"),
    )(*os, *lses)


FFN_LB = 256
FFN_ROWS = 64
HALO = 16


def _conv_chunk(u_ref, w, ci):
    r0 = pl.multiple_of(ci * FFN_ROWS, FFN_ROWS)
    cur = u_ref[0, pl.ds(r0, FFN_ROWS), :].astype(F32)
    p0 = pl.multiple_of(jnp.maximum(r0 - HALO, 0), HALO)
    prev = u_ref[0, pl.ds(p0, HALO), :].astype(F32)
    prev = jnp.where(ci > 0, prev, 0.0)
    rowi = lax.broadcasted_iota(jnp.int32, (8, cur.shape[1]), 0)
    r1 = pltpu.roll(cur, 1, 0)
    r2 = pltpu.roll(cur, 2, 0)
    s1 = jnp.concatenate([jnp.where(rowi == 0, prev[HALO - 1:HALO], r1[0:8]), r1[8:]], axis=0)
    s2 = jnp.concatenate([jnp.where(rowi == 0, prev[HALO - 2:HALO - 1],
                                    jnp.where(rowi == 1, prev[HALO - 1:HALO], r2[0:8])), r2[8:]], axis=0)
    c = w[0:1] * s2
    c = c + w[1:2] * s1
    c = c + w[2:3] * cur
    return c, cur, s1, s2


def _ffn_mid_fwd(u, wconv, *, name):
    b, s, f2 = u.shape
    f = f2 // 2
    nlb = f // FFN_LB

    def body(ua_ref, ug_ref, wa_ref, wg_ref, h_ref):
        wa = wa_ref[...]
        wg = wg_ref[...]

        def step(ci, carry):
            ca = _conv_chunk(ua_ref, wa, ci)[0]
            cg = _conv_chunk(ug_ref, wg, ci)[0]
            r0 = pl.multiple_of(ci * FFN_ROWS, FFN_ROWS)
            h_ref[0, pl.ds(r0, FFN_ROWS), :] = (cg * jax.nn.sigmoid(cg) * ca).astype(h_ref.dtype)
            return carry

        lax.fori_loop(0, s // FFN_ROWS, step, 0)

    return pl.pallas_call(
        body, name=name, grid=(nlb, b),
        in_specs=[pl.BlockSpec((1, s, FFN_LB), lambda l, bb: (bb, 0, l)),
                  pl.BlockSpec((1, s, FFN_LB), lambda l, bb: (bb, 0, nlb + l)),
                  pl.BlockSpec((3, FFN_LB), lambda l, bb: (0, l)),
                  pl.BlockSpec((3, FFN_LB), lambda l, bb: (0, nlb + l))],
        out_specs=pl.BlockSpec((1, s, FFN_LB), lambda l, bb: (bb, 0, l)),
        out_shape=jax.ShapeDtypeStruct((b, s, f), BF16),
        compiler_params=_cp("parallel", "parallel"),
    )(u, u, wconv, wconv)


def _ffn_mid_bwd(u, wconv, dh, *, name):
    b, s, f2 = u.shape
    f = f2 // 2
    nlb = f // FFN_LB
    nchunk = s // FFN_ROWS

    def body(ua_ref, ug_ref, wa_ref, wg_ref, dh_ref, dua_ref, dug_ref, dwa_ref, dwg_ref):
        bb = pl.program_id(1)
        wa = wa_ref[...]
        wg = wg_ref[...]
        rowi = lax.broadcasted_iota(jnp.int32, (8, FFN_LB), 0)
        last = FFN_ROWS - 8

        def conv_transpose(dc, nxt, w):
            r1 = pltpu.roll(dc, FFN_ROWS - 1, 0)
            r2 = pltpu.roll(dc, FFN_ROWS - 2, 0)
            n1 = jnp.concatenate([r1[:last], jnp.where(rowi == 7, nxt[0:1], r1[last:])], axis=0)
            n2 = jnp.concatenate([r2[:last], jnp.where(rowi == 6, nxt[0:1],
                                                       jnp.where(rowi == 7, nxt[1:2], r2[last:]))], axis=0)
            return w[2:3] * dc + w[1:2] * n1 + w[0:1] * n2

        def step(t, carry):
            ci = nchunk - 1 - t
            nxt_a, nxt_g = carry[0], carry[1]
            r0 = pl.multiple_of(ci * FFN_ROWS, FFN_ROWS)
            ca, cura, s1a, s2a = _conv_chunk(ua_ref, wa, ci)
            cg, curg, s1g, s2g = _conv_chunk(ug_ref, wg, ci)
            dhv = dh_ref[0, pl.ds(r0, FFN_ROWS), :].astype(F32)
            sg = jax.nn.sigmoid(cg)
            da = dhv * (cg * sg)
            dg = dhv * ca * (sg * (1.0 + cg * (1.0 - sg)))
            dua_ref[0, pl.ds(r0, FFN_ROWS), :] = conv_transpose(da, nxt_a, wa).astype(dua_ref.dtype)
            dug_ref[0, pl.ds(r0, FFN_ROWS), :] = conv_transpose(dg, nxt_g, wg).astype(dug_ref.dtype)
            red = lambda x: jnp.sum(x, axis=0, keepdims=True)
            parts = (red(da * s2a), red(da * s1a), red(da * cura), red(dg * s2g), red(dg * s1g), red(dg * curg))
            return (da[0:8], dg[0:8]) + tuple(c + p for c, p in zip(carry[2:], parts))

        zero = jnp.zeros((1, FFN_LB), F32)
        zero8 = jnp.zeros((8, FFN_LB), F32)
        taps = lax.fori_loop(0, nchunk, step, (zero8, zero8) + (zero,) * 6)[2:]

        @pl.when(bb == 0)
        def _():
            for k in range(3):
                dwa_ref[k:k + 1, :] = taps[k]
                dwg_ref[k:k + 1, :] = taps[3 + k]

        @pl.when(bb > 0)
        def _():
            for k in range(3):
                dwa_ref[k:k + 1, :] += taps[k]
                dwg_ref[k:k + 1, :] += taps[3 + k]

    seq_a = pl.BlockSpec((1, s, FFN_LB), lambda l, bb: (bb, 0, l))
    seq_g = pl.BlockSpec((1, s, FFN_LB), lambda l, bb: (bb, 0, nlb + l))
    wsp = pl.BlockSpec((3, FFN_LB), lambda l, bb: (0, l))
    return pl.pallas_call(
        body, name=name, grid=(nlb, b),
        in_specs=[seq_a, seq_g, wsp, pl.BlockSpec((3, FFN_LB), lambda l, bb: (0, nlb + l)), seq_a],
        out_specs=[seq_a, seq_a, wsp, wsp],
        out_shape=[jax.ShapeDtypeStruct((b, s, f), BF16), jax.ShapeDtypeStruct((b, s, f), BF16),
                   jax.ShapeDtypeStruct((3, f), F32), jax.ShapeDtypeStruct((3, f), F32)],
        compiler_params=_cp("parallel", "arbitrary"),
    )(u, u, wconv, wconv, dh)


def _adam_math(w, g, m, v):
    m2 = ADAM_B1 * m + (1.0 - ADAM_B1) * g
    v2 = ADAM_B2 * v + (1.0 - ADAM_B2) * (g * g)
    m_hat = m2 / (1.0 - ADAM_B1 ** ADAM_STEP)
    v_hat = v2 / (1.0 - ADAM_B2 ** ADAM_STEP)
    delta = -ADAM_LR * (m_hat / (jnp.sqrt(v_hat) + ADAM_EPS) + ADAM_WD * w)
    return delta, m2, v2


def _adam(w, g, m, v, *, name):
    r, c = w.shape
    tr = _pick(r, (256, 128, 88, 64, 32, 16, 8))

    def body(w_ref, g_ref, m_ref, v_ref, d_ref, m2_ref, v2_ref):
        d, m2, v2 = _adam_math(w_ref[...], g_ref[...], m_ref[...], v_ref[...])
        d_ref[...] = d
        m2_ref[...] = m2
        v2_ref[...] = v2

    sp = pl.BlockSpec((tr, c), lambda i: (i, 0))
    return pl.pallas_call(
        body, name=name, grid=(r // tr,), in_specs=[sp] * 4, out_specs=[sp] * 3,
        out_shape=[jax.ShapeDtypeStruct((r, c), F32)] * 3,
        compiler_params=_cp("parallel"),
    )(w, g, m, v)


def _adam_small(quads, *, name):
    nq = len(quads)

    def body(*refs):
        ins, outs = refs[:4 * nq], refs[4 * nq:]
        for k in range(nq):
            w_ref, g_ref, m_ref, v_ref = ins[4 * k:4 * k + 4]
            d, m2, v2 = _adam_math(w_ref[...], g_ref[...], m_ref[...], v_ref[...])
            outs[3 * k][...] = d
            outs[3 * k + 1][...] = m2
            outs[3 * k + 2][...] = v2

    flat = [a for q in quads for a in q]
    out_shape = [jax.ShapeDtypeStruct(q[0].shape, F32) for q in quads for _ in range(3)]
    vm = pl.BlockSpec(memory_space=pltpu.VMEM)
    outs = pl.pallas_call(
        body, name=name, in_specs=[vm] * len(flat), out_specs=[vm] * len(out_shape), out_shape=out_shape,
        compiler_params=pltpu.CompilerParams(vmem_limit_bytes=VMEM_LIMIT_BYTES),
    )(*flat)
    return [tuple(outs[3 * k:3 * k + 3]) for k in range(nq)]


def _mesh_pos():
    return lax.axis_index("x"), lax.axis_index("y"), lax.axis_index("c")


def _flip(v, bit):
    return 1 - v if bit else v


def _all_gather_hbm(xl, *, name):
    r, c = xl.shape

    def body(x_ref, out_ref, send_sems, recv_sems, local_sem):
        x, y, cc = _mesh_pos()
        me, sibling = (x, y, cc), (x, y, 1 - cc)
        chips = [(1 - x, y), (x, 1 - y), (1 - x, 1 - y)]

        def rows(px, py, pc):
            return out_ref.at[pl.ds((4 * px + 2 * py + pc) * r, r), :]

        def copy(k, block, to, src=None):
            return pltpu.make_async_remote_copy(
                src_ref=rows(*block) if src is None else src, dst_ref=rows(*block),
                send_sem=send_sems.at[k], recv_sem=recv_sems.at[k], device_id=to, device_id_type=MESH_ID)

        mine = pltpu.make_async_copy(x_ref, rows(*me), local_sem)
        mine.start()
        first = [copy(0, me, sibling, src=x_ref)]
        first += [copy(1 + j, me, (*chip, cc), src=x_ref) for j, chip in enumerate(chips)]
        for cp in first:
            cp.start()
        passed = [copy(4 + j, (*chip, cc), sibling) for j, chip in enumerate(chips)]
        for j, chip in enumerate(chips):
            copy(1 + j, (*chip, cc), me).wait_recv()
            passed[j].start()
        copy(0, sibling, me).wait_recv()
        for j, chip in enumerate(chips):
            copy(4 + j, (*chip, 1 - cc), me).wait_recv()
        for cp in first + passed:
            cp.wait_send()
        mine.wait()

    hbm = pl.BlockSpec(memory_space=pltpu.HBM)
    return pl.pallas_call(
        body, name=name, in_specs=[hbm], out_specs=hbm,
        out_shape=jax.ShapeDtypeStruct((N_DEV * r, c), xl.dtype),
        scratch_shapes=[pltpu.SemaphoreType.DMA((7,)), pltpu.SemaphoreType.DMA((7,)), pltpu.SemaphoreType.DMA],
    )(xl)


def _all_reduce_small(xl, *, name):
    r, c = xl.shape

    def body(x_ref, sum_ref, all_ref, send_sems, recv_sems, local_sem):
        x, y, cc = _mesh_pos()
        me, sibling = (x, y, cc), (x, y, 1 - cc)
        chips = [(1 - x, y), (x, 1 - y), (1 - x, 1 - y)]

        def rows(px, py, pc):
            return all_ref.at[pl.ds((4 * px + 2 * py + pc) * r, r), :]

        def copy(k, block, to, src=None):
            return pltpu.make_async_remote_copy(
                src_ref=rows(*block) if src is None else src, dst_ref=rows(*block),
                send_sem=send_sems.at[k], recv_sem=recv_sems.at[k], device_id=to, device_id_type=MESH_ID)

        mine = pltpu.make_async_copy(x_ref, rows(*me), local_sem)
        mine.start()
        first = [copy(0, me, sibling, src=x_ref)]
        first += [copy(1 + j, me, (*chip, cc), src=x_ref) for j, chip in enumerate(chips)]
        for cp in first:
            cp.start()
        passed = [copy(4 + j, (*chip, cc), sibling) for j, chip in enumerate(chips)]
        for j, chip in enumerate(chips):
            copy(1 + j, (*chip, cc), me).wait_recv()
            passed[j].start()
        copy(0, sibling, me).wait_recv()
        for j, chip in enumerate(chips):
            copy(4 + j, (*chip, 1 - cc), me).wait_recv()
        for cp in first + passed:
            cp.wait_send()
        mine.wait()
        tot = all_ref[pl.ds(0, r), :]
        for dd in range(1, N_DEV):
            tot = tot + all_ref[pl.ds(dd * r, r), :]
        sum_ref[...] = tot

    vm = pl.BlockSpec(memory_space=pltpu.VMEM)
    return pl.pallas_call(
        body, name=name, in_specs=[vm], out_specs=[vm, vm],
        out_shape=[jax.ShapeDtypeStruct((r, c), F32), jax.ShapeDtypeStruct((N_DEV * r, c), F32)],
        scratch_shapes=[pltpu.SemaphoreType.DMA((7,)), pltpu.SemaphoreType.DMA((7,)), pltpu.SemaphoreType.DMA],
    )(xl)[0]


N_PEERS = N_DEV - 1
_HBM = pl.BlockSpec(memory_space=pltpu.HBM)
_SEM = pl.BlockSpec(memory_space=pltpu.SEMAPHORE)


def _peer_list(x, y, cc):
    return [(_flip(x, rel & 4), _flip(y, rel & 2), _flip(cc, rel & 1)) for rel in range(1, N_DEV)]


def _dev_index(p):
    return 4 * p[0] + 2 * p[1] + p[2]


def _split_copy(src_ref, land_ref, sems, k, peer, me, gather, landing_of):
    if gather:
        r = src_ref.shape[0]
        src = src_ref
        dst = land_ref.at[pl.ds(_dev_index(landing_of) * r, r), :]
    else:
        src = src_ref.at[_dev_index(peer)]
        dst = land_ref.at[_dev_index(landing_of)]
    return pltpu.make_async_remote_copy(src_ref=src, dst_ref=dst, send_sem=sems[k], recv_sem=sems[N_PEERS + k],
                                        device_id=peer, device_id_type=MESH_ID)


def _exchange_start(src, land, *, name, gather):
    def body(src_ref, land_ref, *rest):
        sems = rest[:2 * N_PEERS]
        token = rest[2 * N_PEERS + 2]
        x, y, cc = _mesh_pos()
        me = (x, y, cc)
        for k, peer in enumerate(_peer_list(x, y, cc)):
            _split_copy(src_ref, land_ref, sems, k, peer, me, gather, landing_of=me).start()
        token[...] = jnp.zeros_like(token)

    outs = pl.pallas_call(
        body, name=name,
        out_shape=tuple([pltpu.SemaphoreType.DMA(())] * (2 * N_PEERS)) + (
            pltpu.HBM(src.shape, src.dtype), pltpu.HBM(land.shape, land.dtype),
            jax.ShapeDtypeStruct((8, LANE), F32)),
        in_specs=(_HBM, _HBM),
        out_specs=tuple([_SEM] * (2 * N_PEERS)) + (_HBM, _HBM, pl.BlockSpec(memory_space=pltpu.VMEM)),
        input_output_aliases={0: 2 * N_PEERS, 1: 2 * N_PEERS + 1},
        compiler_params=pltpu.CompilerParams(has_side_effects=pltpu.SideEffectType.DATAFLOW_SIDE_EFFECTING),
    )(pltpu.with_memory_space_constraint(src, pltpu.HBM), pltpu.with_memory_space_constraint(land, pltpu.HBM))
    return outs[:2 * N_PEERS], outs[2 * N_PEERS], outs[2 * N_PEERS + 1], outs[2 * N_PEERS + 2]


def _gather_start(local, me, *, name):
    rows, cols = local.shape
    land = lax.dynamic_update_slice(lax.empty((N_DEV * rows, cols), local.dtype), local, (me * rows, 0))
    return _exchange_start(local, land, name=name, gather=True)


def _exchange_wait(sems, src_thru, land_thru, after, *, name, gather):
    def body(src_ref, land_ref, *rest):
        sem_refs = rest[:2 * N_PEERS]
        x, y, cc = _mesh_pos()
        me = (x, y, cc)
        for k, peer in enumerate(_peer_list(x, y, cc)):
            cp = _split_copy(src_ref, land_ref, sem_refs, k, peer, me, gather, landing_of=peer)
            cp.wait_send()
            cp.wait_recv()

    outs = pl.pallas_call(
        body, name=name,
        out_shape=(pltpu.HBM(src_thru.shape, src_thru.dtype), pltpu.HBM(land_thru.shape, land_thru.dtype)),
        in_specs=(_HBM, _HBM) + tuple([_SEM] * (2 * N_PEERS)) + (pl.BlockSpec(memory_space=pl.ANY),),
        out_specs=(_HBM, _HBM), input_output_aliases={0: 0, 1: 1},
        compiler_params=pltpu.CompilerParams(has_side_effects=pltpu.SideEffectType.DATAFLOW_SIDE_EFFECTING),
    )(src_thru, land_thru, *sems, after)
    return outs[1]


def _sum_blocks(recv, own, after, *, name):
    nd, r, c = recv.shape
    tr = _pick(r, (448, 256, 128, 64, 32, 16))

    def body(x_ref, own_ref, after_ref, o_ref):
        x, y, cc = _mesh_pos()
        me = 4 * x + 2 * y + cc
        tot = None
        for dd in range(nd):
            term = jnp.where(me == dd, own_ref[0], x_ref[dd]).astype(F32)
            tot = term if tot is None else tot + term
        o_ref[...] = tot

    return pl.pallas_call(
        body, name=name, grid=(r // tr,),
        in_specs=[pl.BlockSpec((nd, tr, c), lambda i: (0, i, 0)), pl.BlockSpec((1, tr, c), lambda i: (0, i, 0)),
                  pl.BlockSpec(memory_space=pl.ANY)],
        out_specs=pl.BlockSpec((tr, c), lambda i: (i, 0)),
        out_shape=jax.ShapeDtypeStruct((r, c), F32),
        compiler_params=_cp("parallel"),
    )(recv, own, after)


SHARD_KIND = {"a_w_in": "col", "a_w_out": "row", "a_w_mem_kv": "row", "a_ffn_up": "col", "a_ffn_down": "row",
              "w_kv_shared": "col", "b_w_in": "row", "b_w_out": "row", "b_w_mem_kv": "row", "b_ffn_up": "col",
              "b_ffn_down": "row"}
EARLY_WEIGHTS = ("a_w_in", "a_w_mem_kv")
FFN_UP_WEIGHTS = ("a_ffn_up", "b_ffn_up")
WIDE_WEIGHTS = tuple(nm for nm in SHARD_KIND if nm not in EARLY_WEIGHTS + FFN_UP_WEIGHTS)
LATE_WEIGHTS = WIDE_WEIGHTS + FFN_UP_WEIGHTS


def _as2d(a):
    return a.reshape(a.shape[-2], a.shape[-1]) if a.ndim >= 2 else a.reshape(1, a.shape[0])


def _pack_local(shards):
    return jnp.concatenate([_as2d(s).astype(BF16).reshape(-1, PACK_COLS) for s in shards], axis=0)


def _unpack_full(gathered, names, shapes):
    out = {}
    r0 = 0
    for name in names:
        rows, cols = shapes[name]
        nr = rows * cols // PACK_COLS
        blk = gathered[:, r0:r0 + nr, :].reshape(N_DEV, rows, cols)
        if SHARD_KIND[name] == "row":
            out[name] = blk.reshape(N_DEV * rows, cols)
        else:
            out[name] = blk.transpose(1, 0, 2).reshape(rows, N_DEV * cols)
        r0 += nr
    return out


def _pack_grads(grads, names, shapes):
    parts = []
    for name in names:
        rows, cols = shapes[name]
        g = grads[name]
        if SHARD_KIND[name] == "row":
            blk = g.reshape(N_DEV, rows, cols)
        else:
            blk = g.reshape(rows, N_DEV, cols).transpose(1, 0, 2)
        parts.append(blk.astype(BF16).reshape(N_DEV, rows * cols // PACK_COLS, PACK_COLS))
    return jnp.concatenate(parts, axis=1)


def _unpack_local(gsum, names, shapes):
    out = {}
    r0 = 0
    for name in names:
        rows, cols = shapes[name]
        nr = rows * cols // PACK_COLS
        out[name] = gsum[r0:r0 + nr].reshape(rows, cols)
        r0 += nr
    return out


def _by_residue(t, d):
    if d == 1:
        return t
    b, s, c = t.shape
    return t.reshape(b, s // d, d, c).transpose(0, 2, 1, 3).reshape(b * d, s // d, c)


def _from_residue(t, d, b):
    if d == 1:
        return t
    n, l, c = t.shape
    return t.reshape(b, d, l, c).transpose(0, 2, 1, 3).reshape(b, l * d, c)


def _alibi_slopes():
    return [2.0 ** (-ALIBI_MAX_BIAS * (i + 1) / N_DIL_HEADS) for i in range(N_DIL_HEADS)]


def _conv_ffn_fwd(xin, gain, w_up, wconv, w_down, tag, b, s):
    (n,), r = _rms_fwd(xin, [gain], name=f"{tag}_rms_ffn")
    u = _mm(n, w_up, name=f"{tag}_up", out_dtype=BF16).reshape(b, s, -1)
    hmid = _ffn_mid_fwd(u, wconv, name=f"{tag}_ffn_mid").reshape(b * s, -1)
    xout = _mm(hmid, w_down, name=f"{tag}_down", out_dtype=F32, res=xin)
    return xout, (n, r, u, hmid)


def _conv_ffn_bwd(dxout, xin, gain, saved, w_up, wconv, w_down, tag, b, s):
    n, r, u, hmid = saved
    f = hmid.shape[1]
    dhmid = _mm(dxout, w_down, name=f"{tag}_d_hmid", out_dtype=BF16, trans_b=True)
    g_down = _mm(hmid, dxout, name=f"{tag}_g_down", out_dtype=BF16, trans_a=True)
    du_a, du_g, gc_a, gc_g = _ffn_mid_bwd(u, wconv, dhmid.reshape(b, s, f), name=f"{tag}_ffn_mid_bwd")
    du_a = du_a.reshape(b * s, f)
    du_g = du_g.reshape(b * s, f)
    dn = _mm(du_a, w_up[:, :f], name=f"{tag}_d_n_a", out_dtype=F32, trans_b=True)
    dn = _mm(du_g, w_up[:, f:], name=f"{tag}_d_n_g", out_dtype=F32, res=dn, trans_b=True)
    g_up = jnp.concatenate([_mm(n, du_a, name=f"{tag}_g_up_a", out_dtype=BF16, trans_a=True),
                            _mm(n, du_g, name=f"{tag}_g_up_g", out_dtype=BF16, trans_a=True)], axis=1)
    dxin, (g_gain,) = _rms_bwd(xin, r, [(dn, gain)], dxout, name=f"{tag}_rms_ffn_bwd")
    return dxin, g_up, g_down, jnp.concatenate([gc_a, gc_g], axis=1), g_gain


def _mem_kv_fwd(mem2d, gain, w_mem_kv, tag, b):
    (nm,), rm = _rms_fwd(mem2d, [gain], name=f"{tag}_rms_mem")
    kvm = _mm(nm, w_mem_kv, name=f"{tag}_mem_kv", out_dtype=BF16)
    return kvm.reshape(b, -1, 2 * MEM_WIDTH), (nm, rm)


def _mem_kv_bwd(dk, dv, mem2d, gain, saved, w_mem_kv, tag):
    nm, rm = saved
    dkvm = jnp.concatenate([dk, dv], axis=-1).reshape(-1, 2 * MEM_WIDTH)
    dnm = _mm(dkvm, w_mem_kv, name=f"{tag}_d_nm", out_dtype=F32, trans_b=True)
    g_w = _mm(nm, dkvm, name=f"{tag}_g_mem_kv", out_dtype=BF16, trans_a=True)
    _, (g_gain,) = _rms_bwd(mem2d, rm, [(dnm, gain)], None, name=f"{tag}_rms_mem_bwd", need_dx=False)
    return g_w, g_gain


def kernel(x, mem, a_norm_attn, a_w_in, a_w_out, a_norm_mem, a_w_mem_kv, a_norm_ffn, a_ffn_up, a_ffn_conv, a_ffn_down, kv_norm, w_kv_shared, b_norm_attn, b_w_in, b_w_out, b_norm_mem, b_w_mem_kv, b_norm_ffn, b_ffn_up, b_ffn_conv, b_ffn_down, final_norm, loss_target, m_a_norm_attn, m_a_w_in, m_a_w_out, m_a_norm_mem, m_a_w_mem_kv, m_a_norm_ffn, m_a_ffn_up, m_a_ffn_conv, m_a_ffn_down, m_kv_norm, m_w_kv_shared, m_b_norm_attn, m_b_w_in, m_b_w_out, m_b_norm_mem, m_b_w_mem_kv, m_b_norm_ffn, m_b_ffn_up, m_b_ffn_conv, m_b_ffn_down, m_final_norm, v_a_norm_attn, v_a_w_in, v_a_w_out, v_a_norm_mem, v_a_w_mem_kv, v_a_norm_ffn, v_a_ffn_up, v_a_ffn_conv, v_a_ffn_down, v_kv_norm, v_w_kv_shared, v_b_norm_attn, v_b_w_in, v_b_w_out, v_b_norm_mem, v_b_w_mem_kv, v_b_norm_ffn, v_b_ffn_up, v_b_ffn_conv, v_b_ffn_down, v_final_norm):
    names = ["a_norm_attn", "a_w_in", "a_w_out", "a_norm_mem", "a_w_mem_kv", "a_norm_ffn", "a_ffn_up",
             "a_ffn_conv", "a_ffn_down", "kv_norm", "w_kv_shared", "b_norm_attn", "b_w_in", "b_w_out",
             "b_norm_mem", "b_w_mem_kv", "b_norm_ffn", "b_ffn_up", "b_ffn_conv", "b_ffn_down", "final_norm"]
    wl = dict(zip(names, [a_norm_attn, a_w_in, a_w_out, a_norm_mem, a_w_mem_kv, a_norm_ffn, a_ffn_up,
                          a_ffn_conv, a_ffn_down, kv_norm, w_kv_shared, b_norm_attn, b_w_in, b_w_out,
                          b_norm_mem, b_w_mem_kv, b_norm_ffn, b_ffn_up, b_ffn_conv, b_ffn_down, final_norm]))
    ml = dict(zip(names, [m_a_norm_attn, m_a_w_in, m_a_w_out, m_a_norm_mem, m_a_w_mem_kv, m_a_norm_ffn,
                          m_a_ffn_up, m_a_ffn_conv, m_a_ffn_down, m_kv_norm, m_w_kv_shared, m_b_norm_attn,
                          m_b_w_in, m_b_w_out, m_b_norm_mem, m_b_w_mem_kv, m_b_norm_ffn, m_b_ffn_up,
                          m_b_ffn_conv, m_b_ffn_down, m_final_norm]))
    vl = dict(zip(names, [v_a_norm_attn, v_a_w_in, v_a_w_out, v_a_norm_mem, v_a_w_mem_kv, v_a_norm_ffn,
                          v_a_ffn_up, v_a_ffn_conv, v_a_ffn_down, v_kv_norm, v_w_kv_shared, v_b_norm_attn,
                          v_b_w_in, v_b_w_out, v_b_norm_mem, v_b_w_mem_kv, v_b_norm_ffn, v_b_ffn_up,
                          v_b_ffn_conv, v_b_ffn_down, v_final_norm]))
    b, s, d = x.shape
    t = b * s
    my_x, my_y, my_c = _mesh_pos()
    me = 4 * my_x + 2 * my_y + my_c

    shapes = {nm: _as2d(wl[nm]).shape for nm in SHARD_KIND}
    early_local = _pack_local([wl[nm] for nm in EARLY_WEIGHTS])
    early_all = _all_gather_hbm(early_local, name="gather_early").reshape(N_DEV, early_local.shape[0], PACK_COLS)
    wf = _unpack_full(early_all, EARLY_WEIGHTS, shapes)
    wide_local = _pack_local([wl[nm] for nm in WIDE_WEIGHTS])
    up_local = jnp.concatenate([_as2d(wl[nm]).astype(BF16) for nm in FFN_UP_WEIGHTS], axis=0)
    gw_sems, gw_src, gw_land, gw_token = _gather_start(wide_local, me, name="gather_wide_start")
    gu_sems, gu_src, gu_land, gu_token = _gather_start(up_local, me, name="gather_up_start")

    sharded_small = ["a_norm_attn", "a_norm_mem", "a_norm_ffn", "a_ffn_conv", "b_ffn_conv"]
    small_flat = jnp.concatenate([wl[nm].reshape(-1) for nm in sharded_small])
    n_small = small_flat.shape[0]
    small_rows = -(-n_small // (8 * LANE)) * 8
    small_local = jnp.pad(small_flat, (0, small_rows * LANE - n_small)).reshape(small_rows, LANE)
    small_all = _all_gather_hbm(small_local, name="gather_small").reshape(N_DEV, small_rows * LANE)
    sfull = {}
    r0 = 0
    for nm in sharded_small:
        rows, cols = _as2d(wl[nm]).shape
        blk = small_all[:, r0:r0 + rows * cols].reshape(N_DEV, rows, cols)
        sfull[nm] = blk.transpose(1, 0, 2).reshape(rows, N_DEV * cols)
        r0 += rows * cols
    gain = {nm: sfull[nm] for nm in ("a_norm_attn", "a_norm_mem", "a_norm_ffn")}
    for nm in ("kv_norm", "b_norm_attn", "b_norm_mem", "b_norm_ffn", "final_norm"):
        gain[nm] = _as2d(wl[nm])
    conv_a, conv_b = sfull["a_ffn_conv"], sfull["b_ffn_conv"]

    x2d = x.reshape(t, d)
    mem2d = mem.reshape(-1, d)
    tgt2d = loss_target.reshape(t, d)
    qmem_blk_a = 3 * SB_WIDTH // GRP
    qmem_blk_b = DIL_WIDTH // GRP

    (n1,), r1 = _rms_fwd(x2d, [gain["a_norm_attn"]], name="a_rms_attn")
    proj_a = _mm(n1, wf["a_w_in"], name="a_in", out_dtype=BF16).reshape(b, s, -1)
    kvm_a, mem_saved_a = _mem_kv_fwd(mem2d, gain["a_norm_mem"], wf["a_w_mem_kv"], "a", b)
    o_sb, rsum = _sb_fwd(proj_a, gw_token + gu_token, name="a_sb_fwd")
    o_mem_a, lse_mem_a = _attn_fwd(proj_a, kvm_a, kvm_a, name="a_mem_fwd", banded=False,
                                   q_lane_blk=qmem_blk_a, k_lane_blk=0, v_lane_blk=1)
    wide_all = _exchange_wait(gw_sems, gw_src, gw_land, rsum, name="gather_wide_wait", gather=True)
    up_all = _exchange_wait(gu_sems, gu_src, gu_land, rsum, name="gather_up_wait", gather=True)
    wf.update(_unpack_full(wide_all.reshape(N_DEV, wide_local.shape[0], PACK_COLS), WIDE_WEIGHTS, shapes))
    up_rows, up_cols = shapes[FFN_UP_WEIGHTS[0]]
    up_all = up_all.reshape(N_DEV, len(FFN_UP_WEIGHTS), up_rows, up_cols)
    for k, nm in enumerate(FFN_UP_WEIGHTS):
        wf[nm] = up_all[:, k].transpose(1, 0, 2).reshape(up_rows, N_DEV * up_cols)
    cat_a = jnp.concatenate([o_sb, o_mem_a], axis=-1).reshape(t, d)
    x1 = _mm(cat_a, wf["a_w_out"], name="a_out", out_dtype=F32, res=x2d)
    xa, ffn_saved_a = _conv_ffn_fwd(x1, gain["a_norm_ffn"], wf["a_ffn_up"], conv_a, wf["a_ffn_down"], "a", b, s)

    (nk, n3), r3 = _rms_fwd(xa, [gain["kv_norm"], gain["b_norm_attn"]], name="b_rms_attn")
    kvsh = _mm(nk, wf["w_kv_shared"], name="kv_shared", out_dtype=BF16).reshape(b, s, -1)
    proj_b = _mm(n3, wf["b_w_in"], name="b_in", out_dtype=BF16).reshape(b, s, -1)
    kvm_b, mem_saved_b = _mem_kv_fwd(mem2d, gain["b_norm_mem"], wf["b_w_mem_kv"], "b", b)
    slopes = _alibi_slopes()
    dil_q, dil_k, dil_v, dil_o, dil_lse, dil_slopes = [], [], [], [], [], []
    for g, (_, dil) in enumerate(DIL_GROUPS):
        qg = _by_residue(proj_b[:, :, GRP * g:GRP * (g + 1)], dil)
        kg = _by_residue(kvsh[:, :, GRP * g:GRP * (g + 1)], dil)
        vg = _by_residue(kvsh[:, :, DIL_WIDTH + GRP * g:DIL_WIDTH + GRP * (g + 1)], dil)
        sl = [slopes[4 * g + h] * dil for h in range(4)]
        og, lg = _attn_fwd(qg, kg, vg, name=f"b_dil{g}_fwd", banded=True, slopes_scaled=sl)
        dil_q.append(qg)
        dil_k.append(kg)
        dil_v.append(vg)
        dil_slopes.append(sl)
        dil_o.append(_from_residue(og, dil, b).reshape(t, GRP))
        dil_lse.append(_from_residue(lg, dil, b).reshape(t, PER_HEAD))
    o_dil, lse_joint = _dil_combine(dil_o, dil_lse, name="b_dil_combine")
    o_mem_b, lse_mem_b = _attn_fwd(proj_b, kvm_b, kvm_b, name="b_mem_fwd", banded=False,
                                   q_lane_blk=qmem_blk_b, k_lane_blk=0, v_lane_blk=1)
    cat_b = jnp.concatenate([o_dil, o_mem_b.reshape(t, MEM_WIDTH)], axis=-1)
    x3 = _mm(cat_b, wf["b_w_out"], name="b_out", out_dtype=F32, res=xa)
    xb, ffn_saved_b = _conv_ffn_fwd(x3, gain["b_norm_ffn"], wf["b_ffn_up"], conv_b, wf["b_ffn_down"], "b", b, s)

    dxb, g_final, loss_vec = _loss_head(xb, gain["final_norm"], tgt2d, name="loss_head")

    grads = {}
    sgrads = {"final_norm": g_final}
    dx3, grads["b_ffn_up"], grads["b_ffn_down"], sgrads["b_ffn_conv"], sgrads["b_norm_ffn"] = _conv_ffn_bwd(
        dxb, x3, gain["b_norm_ffn"], ffn_saved_b, wf["b_ffn_up"], conv_b, wf["b_ffn_down"], "b", b, s)
    dcat_b = _mm(dx3, wf["b_w_out"], name="b_d_cat", out_dtype=BF16, trans_b=True)
    grads["b_w_out"] = _mm(cat_b, dx3, name="b_g_out", out_dtype=BF16, trans_a=True)
    dcat_b3 = dcat_b.reshape(b, s, d)
    delta_mem_b = _attn_delta(dcat_b, cat_b, name="b_mem_delta", lane_blks=[qmem_blk_b]).reshape(b, s, PER_HEAD)
    dq_mem_b, dkm_b, dvm_b = _attn_bwd(proj_b, kvm_b, kvm_b, dcat_b3, lse_mem_b, delta_mem_b, name="b_mem_bwd",
                                       banded=False, q_lane_blk=qmem_blk_b, k_lane_blk=0, v_lane_blk=1,
                                       do_lane_blk=qmem_blk_b)
    delta_dil = _attn_delta(dcat_b, cat_b, name="b_dil_delta", lane_blks=[0, 1, 2]).reshape(b, s, PER_HEAD)
    lse_joint3 = lse_joint.reshape(b, s, PER_HEAD)
    dq_parts, dk_parts, dv_parts = [], [], []
    for g, (_, dil) in enumerate(DIL_GROUPS):
        dog = _by_residue(dcat_b3[:, :, GRP * g:GRP * (g + 1)], dil)
        lg = _by_residue(lse_joint3, dil)
        dg = _by_residue(delta_dil, dil)
        dqg, dkg, dvg = _attn_bwd(dil_q[g], dil_k[g], dil_v[g], dog, lg, dg, name=f"b_dil{g}_bwd", banded=True,
                                  slopes_scaled=dil_slopes[g])
        dq_parts.append(_from_residue(dqg, dil, b))
        dk_parts.append(_from_residue(dkg, dil, b))
        dv_parts.append(_from_residue(dvg, dil, b))
    dproj_b = jnp.concatenate(dq_parts + [dq_mem_b], axis=-1).reshape(t, d)
    dn3 = _mm(dproj_b, wf["b_w_in"], name="b_d_n", out_dtype=F32, trans_b=True)
    grads["b_w_in"] = _mm(n3, dproj_b, name="b_g_in", out_dtype=BF16, trans_a=True)
    grads["b_w_mem_kv"], sgrads["b_norm_mem"] = _mem_kv_bwd(dkm_b, dvm_b, mem2d, gain["b_norm_mem"], mem_saved_b,
                                                           wf["b_w_mem_kv"], "b")
    dkvsh = jnp.concatenate(dk_parts + dv_parts, axis=-1).reshape(t, 2 * DIL_WIDTH).astype(BF16)
    dnk = _mm(dkvsh, wf["w_kv_shared"], name="kv_d_n", out_dtype=F32, trans_b=True)
    grads["w_kv_shared"] = _mm(nk, dkvsh, name="kv_g", out_dtype=BF16, trans_a=True)
    dxa, (sgrads["kv_norm"], sgrads["b_norm_attn"]) = _rms_bwd(
        xa, r3, [(dnk, gain["kv_norm"]), (dn3, gain["b_norm_attn"])], dx3, name="b_rms_attn_bwd")

    dx1, grads["a_ffn_up"], grads["a_ffn_down"], sgrads["a_ffn_conv"], sgrads["a_norm_ffn"] = _conv_ffn_bwd(
        dxa, x1, gain["a_norm_ffn"], ffn_saved_a, wf["a_ffn_up"], conv_a, wf["a_ffn_down"], "a", b, s)
    dcat_a = _mm(dx1, wf["a_w_out"], name="a_d_cat", out_dtype=BF16, trans_b=True)
    grads["a_w_out"] = _mm(cat_a, dx1, name="a_g_out", out_dtype=BF16, trans_a=True)
    dcat_a3 = dcat_a.reshape(b, s, d)
    delta_mem_a = _attn_delta(dcat_a, cat_a, name="a_mem_delta", lane_blks=[qmem_blk_b]).reshape(b, s, PER_HEAD)
    dq_mem_a, dkm_a, dvm_a = _attn_bwd(proj_a, kvm_a, kvm_a, dcat_a3, lse_mem_a, delta_mem_a, name="a_mem_bwd",
                                       banded=False, q_lane_blk=qmem_blk_a, k_lane_blk=0, v_lane_blk=1,
                                       do_lane_blk=qmem_blk_b)
    wide_grads = _pack_grads(grads, WIDE_WEIGHTS, shapes)
    up_grads = jnp.concatenate([grads[nm].reshape(up_rows, N_DEV, up_cols).transpose(1, 0, 2)
                                for nm in FFN_UP_WEIGHTS], axis=1)
    own_wide = lax.dynamic_slice(wide_grads, (me, 0, 0), (1,) + wide_grads.shape[1:])
    own_up = lax.dynamic_slice(up_grads, (me, 0, 0), (1,) + up_grads.shape[1:])
    xw_sems, xw_src, xw_land, xw_token = _exchange_start(wide_grads, lax.empty(wide_grads.shape, BF16),
                                                         name="grads_wide_start", gather=False)
    xu_sems, xu_src, xu_land, xu_token = _exchange_start(up_grads, lax.empty(up_grads.shape, BF16),
                                                         name="grads_up_start", gather=False)
    dq_sb, dk_sb, dv_sb = _sb_bwd(proj_a, dcat_a3, rsum, xw_token + xu_token, name="a_sb_bwd")
    dproj_a = jnp.concatenate([dq_sb, dk_sb, dv_sb, dq_mem_a], axis=-1).reshape(t, -1)
    dn1 = _mm(dproj_a, wf["a_w_in"], name="a_d_n", out_dtype=F32, trans_b=True)
    grads["a_w_in"] = _mm(n1, dproj_a, name="a_g_in", out_dtype=BF16, trans_a=True)
    grads["a_w_mem_kv"], sgrads["a_norm_mem"] = _mem_kv_bwd(dkm_a, dvm_a, mem2d, gain["a_norm_mem"], mem_saved_a,
                                                           wf["a_w_mem_kv"], "a")
    early_grads = _pack_grads(grads, EARLY_WEIGHTS, shapes)
    own_early = lax.dynamic_slice(early_grads, (me, 0, 0), (1,) + early_grads.shape[1:])
    ee_sems, ee_src, ee_land, ee_token = _exchange_start(early_grads, lax.empty(early_grads.shape, BF16),
                                                         name="grads_early_start", gather=False)
    dx0, (sgrads["a_norm_attn"],) = _rms_bwd(x2d, r1, [(dn1, gain["a_norm_attn"])], dx1, name="a_rms_attn_bwd")
    grad_x = dx0.reshape(b, s, d)

    wide_recv = _exchange_wait(xw_sems, xw_src, xw_land, dx0, name="grads_wide_wait", gather=False)
    up_recv = _exchange_wait(xu_sems, xu_src, xu_land, dx0, name="grads_up_wait", gather=False)
    gl = _unpack_local(_sum_blocks(wide_recv, own_wide, ee_token, name="sum_grads_wide"), WIDE_WEIGHTS, shapes)
    up_sum = _sum_blocks(up_recv, own_up, ee_token, name="sum_grads_up")
    for k, nm in enumerate(FFN_UP_WEIGHTS):
        gl[nm] = up_sum[k * up_rows:(k + 1) * up_rows]

    small_names = ["a_norm_attn", "a_norm_mem", "a_norm_ffn", "kv_norm", "b_norm_attn", "b_norm_mem",
                   "b_norm_ffn", "final_norm", "a_ffn_conv", "b_ffn_conv"]
    small_flat = jnp.concatenate([sgrads[nm].reshape(-1) for nm in small_names] + [loss_vec.reshape(-1)])
    n_flat = small_flat.shape[0]
    red_rows = -(-n_flat // (8 * PACK_COLS)) * 8
    small_pack = jnp.pad(small_flat, (0, red_rows * PACK_COLS - n_flat)).reshape(red_rows, PACK_COLS)
    small_sum = _all_reduce_small(small_pack, name="reduce_small").reshape(-1)
    r0 = 0
    for nm in small_names:
        rows, cols = sgrads[nm].shape
        full = small_sum[r0:r0 + rows * cols].reshape(rows, cols)
        r0 += rows * cols
        if nm in sharded_small:
            lc = cols // N_DEV
            gl[nm] = lax.dynamic_slice(full, (0, me * lc), (rows, lc))
        else:
            gl[nm] = full
    loss = (0.5 / d) * jnp.sum(small_sum[r0:r0 + d])

    upd = {}
    for nm in LATE_WEIGHTS:
        upd[nm] = _adam(_as2d(wl[nm]), gl[nm], _as2d(ml[nm]), _as2d(vl[nm]), name=f"adam_{nm}")
    res_small = _adam_small([(_as2d(wl[nm]), gl[nm], _as2d(ml[nm]), _as2d(vl[nm])) for nm in small_names],
                            name="adam_small")
    for nm, r in zip(small_names, res_small):
        upd[nm] = r
    early_recv = _exchange_wait(ee_sems, ee_src, ee_land, upd[LATE_WEIGHTS[-1]][0], name="grads_early_wait",
                                gather=False)
    gl.update(_unpack_local(_sum_blocks(early_recv, own_early, ee_token, name="sum_grads_early"), EARLY_WEIGHTS,
                            shapes))
    for nm in EARLY_WEIGHTS:
        upd[nm] = _adam(_as2d(wl[nm]), gl[nm], _as2d(ml[nm]), _as2d(vl[nm]), name=f"adam_{nm}")

    g_out = [gl[nm].reshape(wl[nm].shape) for nm in names]
    d_out = [upd[nm][0].reshape(wl[nm].shape) for nm in names]
    m_out = [upd[nm][1].reshape(wl[nm].shape) for nm in names]
    v_out = [upd[nm][2].reshape(wl[nm].shape) for nm in names]
    return (loss, grad_x, *g_out, *d_out, *m_out, *v_out)
```

```python
import functools

import jax
import jax.numpy as jnp
from jax import lax
from jax.experimental import pallas as pl
from jax.experimental.pallas import tpu as pltpu

F32 = jnp.float32
BF16 = jnp.bfloat16

N_DEV = 8
HEAD_DIM = 64
N_SB_HEADS = 12
N_DIL_HEADS = 12
DIL_GROUPS = ((128, 1), (512, 4), (2048, 16))
SB_WIDTH = N_SB_HEADS * HEAD_DIM
MEM_WIDTH = 256
DIL_WIDTH = N_DIL_HEADS * HEAD_DIM
ATT_SCALE = HEAD_DIM ** -0.5
EPS = 1e-6
ALIBI_MAX_BIAS = 8.0
NEG_BIG = -1e30

ADAM_LR = 0.001
ADAM_B1 = 0.9
ADAM_B2 = 0.999
ADAM_EPS = 1e-08
ADAM_WD = 0.01
ADAM_STEP = 10

LANE = 128
QBLK = 128
VMEM_LIMIT_BYTES = 48 * 1024 * 1024
PACK_COLS = 1024
MESH_ID = pl.DeviceIdType.MESH


def _cp(*sem):
    return pltpu.CompilerParams(dimension_semantics=sem, vmem_limit_bytes=VMEM_LIMIT_BYTES)


def _pick(n, cands):
    for c in cands:
        if n % c == 0:
            return c
    raise ValueError(f"no tile for {n} in {cands}")


def _dot(a, b):
    return jnp.dot(a, b, preferred_element_type=F32)


def _dot_nt(a, b):
    return lax.dot_general(a, b, (((1,), (1,)), ((), ())), preferred_element_type=F32)


def _dot_tn(a, b):
    return lax.dot_general(a, b, (((0,), (0,)), ((), ())), preferred_element_type=F32)


def _dot_split(x, u):
    hi = x.astype(BF16)
    lo = (x - hi.astype(F32)).astype(BF16)
    return _dot(hi, u) + _dot(lo, u)


def _mm(a, b, *, name, out_dtype, res=None, trans_a=False, trans_b=False):
    assert not (trans_a and trans_b)
    if trans_a:
        kdim, m = a.shape
    else:
        m, kdim = a.shape
    if trans_b:
        n, kb = b.shape
    else:
        kb, n = b.shape
    assert kb == kdim, (a.shape, b.shape)
    if trans_a:
        tm = _pick(m, (1408, 1024, 512, 256, 128))
        tn = _pick(n, (1024, 1280, 1408, 768, 512, 256, 128))
        tk = _pick(kdim, (1024, 512, 256))
    else:
        tm = _pick(m, (1024, 512, 256, 128))
        tk = kdim if kdim <= 2816 else _pick(kdim, (2048, 1536, 1408, 1280, 1024, 512))
        tn = _pick(n, (512, 256, 128) if tk > 2048 else (1408, 1280, 1024, 768, 512, 256, 128))
    nk = kdim // tk
    has_res = res is not None

    def body(*refs):
        if has_res:
            a_ref, b_ref, r_ref, o_ref = refs[:4]
            scr = refs[4:]
        else:
            a_ref, b_ref, o_ref = refs[:3]
            r_ref = None
            scr = refs[3:]
        av = a_ref[...].astype(BF16)
        bv = b_ref[...].astype(BF16)
        if trans_a:
            p = _dot_tn(av, bv)
        elif trans_b:
            p = _dot_nt(av, bv)
        else:
            p = _dot(av, bv)

        def finish(acc):
            if has_res:
                acc = acc + r_ref[...]
            o_ref[...] = acc.astype(o_ref.dtype)

        if nk == 1:
            finish(p)
        else:
            acc_ref = scr[0]
            k = pl.program_id(2)

            @pl.when(k == 0)
            def _():
                acc_ref[...] = p

            @pl.when(k > 0)
            def _():
                acc_ref[...] += p

            @pl.when(k == nk - 1)
            def _():
                finish(acc_ref[...])

    if trans_a:
        a_spec = pl.BlockSpec((tk, tm), lambda i, j, k: (k, i))
    else:
        a_spec = pl.BlockSpec((tm, tk), lambda i, j, k: (i, k))
    if trans_b:
        b_spec = pl.BlockSpec((tn, tk), lambda i, j, k: (j, k))
    else:
        b_spec = pl.BlockSpec((tk, tn), lambda i, j, k: (k, j))
    in_specs = [a_spec, b_spec]
    args = [a, b]
    if has_res:
        in_specs.append(pl.BlockSpec((tm, tn), lambda i, j, k: (i, j)))
        args.append(res)
    return pl.pallas_call(
        body, name=name,
        grid=(m // tm, n // tn, nk),
        in_specs=in_specs,
        out_specs=pl.BlockSpec((tm, tn), lambda i, j, k: (i, j)),
        out_shape=jax.ShapeDtypeStruct((m, n), out_dtype),
        scratch_shapes=[pltpu.VMEM((tm, tn), F32)] if nk > 1 else [],
        compiler_params=_cp("parallel", "parallel", "arbitrary"),
    )(*args)


def _rms_fwd(x, gains, *, name):
    t, d = x.shape
    tr = _pick(t, (512, 256, 128, 8))
    ng = len(gains)

    def body(x_ref, *rest):
        g_refs, n_refs, r_ref = rest[:ng], rest[ng:2 * ng], rest[2 * ng]
        xv = x_ref[...]
        r = lax.rsqrt(jnp.mean(xv * xv, axis=-1, keepdims=True) + EPS)
        xh = xv * r
        for g_ref, n_ref in zip(g_refs, n_refs):
            n_ref[...] = (xh * g_ref[...]).astype(BF16)
        r_ref[...] = r

    row = pl.BlockSpec((tr, d), lambda i: (i, 0))
    gsp = pl.BlockSpec((1, d), lambda i: (0, 0))
    outs = pl.pallas_call(
        body, name=name, grid=(t // tr,),
        in_specs=[row] + [gsp] * ng,
        out_specs=[row] * ng + [pl.BlockSpec((tr, 1), lambda i: (i, 0))],
        out_shape=[jax.ShapeDtypeStruct((t, d), BF16)] * ng + [jax.ShapeDtypeStruct((t, 1), F32)],
        compiler_params=_cp("parallel"),
    )(x, *gains)
    return list(outs[:ng]), outs[ng]


def _rms_bwd(x, r, pairs, dres, *, name, need_dx=True):
    t, d = x.shape
    tr = _pick(t, (512, 256, 128, 8))
    npair = len(pairs)
    has_res = dres is not None

    def body(*refs):
        x_ref, r_ref = refs[:2]
        pr = refs[2:2 + 2 * npair]
        pos = 2 + 2 * npair
        res_ref = None
        if has_res:
            res_ref = refs[pos]
            pos += 1
        dx_ref = None
        if need_dx:
            dx_ref = refs[pos]
            pos += 1
        dg_refs = refs[pos:pos + npair]
        i = pl.program_id(0)
        rv = r_ref[...]
        xh = x_ref[...] * rv
        dx = res_ref[...] if has_res else None
        for k in range(npair):
            dn = pr[2 * k][...].astype(F32)
            g = pr[2 * k + 1][...]
            part = jnp.sum(dn * xh, axis=0, keepdims=True)

            @pl.when(i == 0)
            def _():
                dg_refs[k][...] = part

            @pl.when(i > 0)
            def _():
                dg_refs[k][...] += part

            if need_dx:
                dxh = dn * g
                c = jnp.mean(dxh * xh, axis=-1, keepdims=True)
                term = rv * (dxh - xh * c)
                dx = term if dx is None else dx + term
        if need_dx:
            dx_ref[...] = dx

    row = pl.BlockSpec((tr, d), lambda i: (i, 0))
    gsp = pl.BlockSpec((1, d), lambda i: (0, 0))
    in_specs = [row, pl.BlockSpec((tr, 1), lambda i: (i, 0))]
    args = [x, r]
    for dn, g in pairs:
        in_specs += [row, gsp]
        args += [dn, g]
    if has_res:
        in_specs.append(row)
        args.append(dres)
    out_specs, out_shape = [], []
    if need_dx:
        out_specs.append(row)
        out_shape.append(jax.ShapeDtypeStruct((t, d), F32))
    out_specs += [gsp] * npair
    out_shape += [jax.ShapeDtypeStruct((1, d), F32)] * npair
    outs = pl.pallas_call(
        body, name=name, grid=(t // tr,), in_specs=in_specs, out_specs=out_specs, out_shape=out_shape,
        compiler_params=_cp("arbitrary"),
    )(*args)
    if need_dx:
        return outs[0], list(outs[1:])
    return None, list(outs)


def _loss_head(h, g, tgt, *, name):
    t, d = h.shape
    tr = _pick(t, (512, 256, 128, 8))

    def body(h_ref, g_ref, t_ref, dh_ref, dg_ref, l_ref):
        i = pl.program_id(0)
        xv = h_ref[...]
        gv = g_ref[...]
        r = lax.rsqrt(jnp.mean(xv * xv, axis=-1, keepdims=True) + EPS)
        xh = xv * r
        e = xh * gv - t_ref[...]
        dy = e * (1.0 / d)
        lpart = jnp.sum(e * e, axis=0, keepdims=True)
        gpart = jnp.sum(dy * xh, axis=0, keepdims=True)

        @pl.when(i == 0)
        def _():
            l_ref[...] = lpart
            dg_ref[...] = gpart

        @pl.when(i > 0)
        def _():
            l_ref[...] += lpart
            dg_ref[...] += gpart

        dxh = dy * gv
        c = jnp.mean(dxh * xh, axis=-1, keepdims=True)
        dh_ref[...] = r * (dxh - xh * c)

    row = pl.BlockSpec((tr, d), lambda i: (i, 0))
    gsp = pl.BlockSpec((1, d), lambda i: (0, 0))
    return pl.pallas_call(
        body, name=name, grid=(t // tr,), in_specs=[row, gsp, row], out_specs=[row, gsp, gsp],
        out_shape=[jax.ShapeDtypeStruct((t, d), F32), jax.ShapeDtypeStruct((1, d), F32),
                   jax.ShapeDtypeStruct((1, d), F32)],
        compiler_params=_cp("arbitrary"),
    )(h, g, tgt)


GRP = 4 * HEAD_DIM
SB_KB = 2 * QBLK
SB_QB = 2 * QBLK


def _head_masks4(shape):
    lane = lax.broadcasted_iota(jnp.int32, shape, 1)
    return [(lane >= HEAD_DIM * h) & (lane < HEAD_DIM * (h + 1)) for h in range(4)]


def _neg_softplus(z):
    nz = -z
    return jnp.minimum(nz, 0.0) - jnp.log(1.0 + jnp.exp(jnp.minimum(z, nz)))


def _stacked_col_minus_row():
    rowi = lax.broadcasted_iota(jnp.int32, (4 * SB_QB, SB_KB), 0)
    coli = lax.broadcasted_iota(jnp.int32, (4 * SB_QB, SB_KB), 1)
    return coli - (rowi & (SB_QB - 1))


def _sb_fwd(proj, after, *, name):
    b, s, _ = proj.shape
    nq = s // SB_QB
    ngrp = SB_WIDTH // GRP

    def body(q_ref, k_ref, v_ref, after_ref, o_ref, r_ref, acc_ref, car_ref):
        i = pl.program_id(2)
        masks = _head_masks4((SB_QB, GRP))
        row = lax.broadcasted_iota(jnp.int32, (SB_KB, SB_KB), 0)
        col = lax.broadcasted_iota(jnp.int32, (SB_KB, SB_KB), 1)
        later_mat = (row > col).astype(BF16)
        col_minus_row = _stacked_col_minus_row()
        qs = q_ref[0] * jnp.asarray(ATT_SCALE, BF16)
        q_stack = jnp.concatenate([jnp.where(mk, qs, jnp.zeros_like(qs)) for mk in masks], axis=0)
        acc_ref[...] = jnp.zeros_like(acc_ref)
        car_ref[...] = jnp.zeros_like(car_ref)

        def process(jbs, masked):
            offs = [pl.multiple_of(jb * SB_KB, SB_KB) for jb in jbs]
            k2s = [k_ref[0, pl.ds(off, SB_KB), :] for off in offs]
            v2s = [v_ref[0, pl.ds(off, SB_KB), :] for off in offs]
            zs = [_dot_nt(q_stack, k2) for k2 in k2s]
            lss = []
            for jb, z in zip(jbs, zs):
                ls = _neg_softplus(z)
                if masked:
                    causal = col_minus_row < (i * SB_QB - jb * SB_KB)
                    ls = jnp.where(causal, ls, 0.0)
                lss.append(ls)
            laters = [_dot(ls.astype(BF16), later_mat) for ls in lss]
            car = car_ref[...]
            ws = []
            for jb, z, ls, later in zip(jbs, zs, lss, laters):
                w = jnp.exp((z + ls) + later + car)
                if masked:
                    w = jnp.where(col_minus_row < (i * SB_QB - jb * SB_KB), w, 0.0)
                ws.append(w.astype(BF16))
                car = car + jnp.sum(ls, axis=1, keepdims=True)
            car_ref[...] = car
            acc_ref[...] += functools.reduce(jnp.add, [_dot(w, v2) for w, v2 in zip(ws, v2s)])

        top = (i * SB_QB) // SB_KB
        process([top], True)

        def step(jj, carry):
            process([top - 1 - 2 * jj, top - 2 - 2 * jj], False)
            return carry

        lax.fori_loop(0, top // 2, step, 0)

        @pl.when(top % 2 == 1)
        def _():
            process([0], False)

        o = acc_ref[pl.ds(0, SB_QB), :]
        r = car_ref[pl.ds(0, SB_QB), :]
        for h in range(1, 4):
            o = jnp.where(masks[h], acc_ref[pl.ds(h * SB_QB, SB_QB), :], o)
            r = jnp.where(masks[h], car_ref[pl.ds(h * SB_QB, SB_QB), :], r)
        o_ref[0] = o.astype(o_ref.dtype)
        r_ref[0] = r

    blk = pl.BlockSpec((1, SB_QB, GRP), lambda bb, p, i: (bb, i, p))
    return pl.pallas_call(
        body, name=name, grid=(b, ngrp, nq),
        in_specs=[blk,
                  pl.BlockSpec((1, s, GRP), lambda bb, p, i: (bb, 0, ngrp + p)),
                  pl.BlockSpec((1, s, GRP), lambda bb, p, i: (bb, 0, 2 * ngrp + p)),
                  pl.BlockSpec(memory_space=pl.ANY)],
        out_specs=[blk, blk],
        out_shape=[jax.ShapeDtypeStruct((b, s, SB_WIDTH), BF16), jax.ShapeDtypeStruct((b, s, SB_WIDTH), F32)],
        scratch_shapes=[pltpu.VMEM((4 * SB_QB, GRP), F32), pltpu.VMEM((4 * SB_QB, SB_KB), F32)],
        compiler_params=_cp("parallel", "parallel", "arbitrary"),
    )(proj, proj, proj, after)


def _sb_bwd(proj, dcat, rsum, after, *, name):
    b, s, _ = proj.shape
    nq = s // SB_QB
    ngrp = SB_WIDTH // GRP

    def body(q_ref, k_ref, v_ref, do_ref, r_ref, after_ref, dq_ref, dk_out, dv_out, dq_acc, cp_ref, cg_ref,
             dk_ref, dv_ref):
        i = pl.program_id(2)

        @pl.when(i == 0)
        def _():
            dk_ref[...] = jnp.zeros_like(dk_ref)
            dv_ref[...] = jnp.zeros_like(dv_ref)

        masks = _head_masks4((SB_QB, GRP))
        row = lax.broadcasted_iota(jnp.int32, (SB_KB, SB_KB), 0)
        col = lax.broadcasted_iota(jnp.int32, (SB_KB, SB_KB), 1)
        later_mat = (row > col).astype(BF16)
        excl_mat = (row < col).astype(BF16)
        col_minus_row = _stacked_col_minus_row()
        qs = q_ref[0] * jnp.asarray(ATT_SCALE, BF16)
        do = do_ref[0]
        q_stack = jnp.concatenate([jnp.where(mk, qs, jnp.zeros_like(qs)) for mk in masks], axis=0)
        do_stack = jnp.concatenate([jnp.where(mk, do, jnp.zeros_like(do)) for mk in masks], axis=0)
        rv = r_ref[0]
        r_stack = jnp.concatenate([rv[:, HEAD_DIM * h:HEAD_DIM * h + 1] for h in range(4)], axis=0)
        dq_acc[...] = jnp.zeros_like(dq_acc)
        cp_ref[...] = jnp.zeros_like(cp_ref)
        cg_ref[...] = jnp.zeros_like(cg_ref)

        def process(jbs, masked):
            offs = [pl.multiple_of(jb * SB_KB, SB_KB) for jb in jbs]
            k2s = [k_ref[0, pl.ds(off, SB_KB), :] for off in offs]
            v2s = [v_ref[0, pl.ds(off, SB_KB), :] for off in offs]
            zs = [_dot_nt(q_stack, k2) for k2 in k2s]
            dws = [_dot_nt(do_stack, v2) for v2 in v2s]
            lss, lsigs = [], []
            for jb, z in zip(jbs, zs):
                ls = _neg_softplus(z)
                lsigs.append(z + ls)
                if masked:
                    ls = jnp.where(col_minus_row < (i * SB_QB - jb * SB_KB), ls, 0.0)
                lss.append(ls)
            laters = [_dot(ls.astype(BF16), later_mat) for ls in lss]
            cpv = cp_ref[...]
            ws, gs = [], []
            for jb, ls, lsig, later, dw in zip(jbs, lss, lsigs, laters, dws):
                cpv = cpv + jnp.sum(ls, axis=1, keepdims=True)
                w = jnp.exp(lsig + ((r_stack - cpv) + later))
                if masked:
                    w = jnp.where(col_minus_row < (i * SB_QB - jb * SB_KB), w, 0.0)
                ws.append(w.astype(BF16))
                gs.append(dw * w)
            cp_ref[...] = cpv
            gpres = [_dot(g.astype(BF16), excl_mat) for g in gs]
            cgv = cg_ref[...]
            dzs = []
            for jb, g, lsig, gpre in zip(jbs, gs, lsigs, gpres):
                dz = g - jnp.exp(lsig) * (g + (gpre + cgv))
                if masked:
                    dz = jnp.where(col_minus_row < (i * SB_QB - jb * SB_KB), dz, 0.0)
                dzs.append(dz.astype(BF16))
                cgv = cgv + jnp.sum(g, axis=1, keepdims=True)
            cg_ref[...] = cgv
            dq_acc[...] += functools.reduce(jnp.add, [_dot(dzb, k2) for dzb, k2 in zip(dzs, k2s)])
            for off, dzb, wb in zip(offs, dzs, ws):
                dk_ref[0, pl.ds(off, SB_KB), :] += _dot_tn(dzb, q_stack)
                dv_ref[0, pl.ds(off, SB_KB), :] += _dot_tn(wb, do_stack)

        top = (i * SB_QB) // SB_KB

        def step(jj, carry):
            process([2 * jj, 2 * jj + 1], False)
            return carry

        lax.fori_loop(0, top // 2, step, 0)

        @pl.when(top % 2 == 1)
        def _():
            process([top - 1], False)

        process([top], True)
        dq = dq_acc[pl.ds(0, SB_QB), :]
        for h in range(1, 4):
            dq = jnp.where(masks[h], dq_acc[pl.ds(h * SB_QB, SB_QB), :], dq)
        dq_ref[0] = (dq * ATT_SCALE).astype(dq_ref.dtype)

        @pl.when(i == nq - 1)
        def _():
            dk_out[...] = dk_ref[...].astype(dk_out.dtype)
            dv_out[...] = dv_ref[...].astype(dv_out.dtype)

    blk = pl.BlockSpec((1, SB_QB, GRP), lambda bb, p, i: (bb, i, p))
    seq = pl.BlockSpec((1, s, GRP), lambda bb, p, i: (bb, 0, p))
    return pl.pallas_call(
        body, name=name, grid=(b, ngrp, nq),
        in_specs=[blk,
                  pl.BlockSpec((1, s, GRP), lambda bb, p, i: (bb, 0, ngrp + p)),
                  pl.BlockSpec((1, s, GRP), lambda bb, p, i: (bb, 0, 2 * ngrp + p)),
                  blk, blk, pl.BlockSpec(memory_space=pl.ANY)],
        out_specs=[blk, seq, seq],
        out_shape=[jax.ShapeDtypeStruct((b, s, SB_WIDTH), BF16)] * 3,
        scratch_shapes=[pltpu.VMEM((4 * SB_QB, GRP), F32), pltpu.VMEM((4 * SB_QB, SB_KB), F32),
                        pltpu.VMEM((4 * SB_QB, SB_KB), F32), pltpu.VMEM((1, s, GRP), F32),
                        pltpu.VMEM((1, s, GRP), F32)],
        compiler_params=_cp("parallel", "parallel", "arbitrary"),
    )(proj, proj, proj, dcat, rsum, after)


def _band_bias(slopes_scaled):
    a = lax.broadcasted_iota(jnp.int32, (QBLK, 2 * QBLK), 0)
    bcol = lax.broadcasted_iota(jnp.int32, (QBLK, 2 * QBLK), 1)
    delta = a + QBLK - bcol
    in_band = (delta >= 0) & (delta <= QBLK)
    dist = delta.astype(F32)
    bias = jnp.concatenate([(-sl) * dist for sl in slopes_scaled], axis=0)
    return jnp.concatenate([in_band] * 4, axis=0), jnp.concatenate([bcol >= QBLK] * 4, axis=0), bias


def _stack_heads(x, masks):
    return jnp.concatenate([jnp.where(mk, x, jnp.zeros_like(x)) for mk in masks], axis=0)


def _unstack_heads(x, masks):
    out = jnp.broadcast_to(x[0:QBLK], (QBLK, GRP))
    for h in range(1, 4):
        out = jnp.where(masks[h], x[h * QBLK:(h + 1) * QBLK], out)
    return out


PER_HEAD = 8


def _head_column(x):
    return jnp.concatenate([x[:, h:h + 1] for h in range(4)], axis=0)


def _head_lanes(col):
    lane = lax.broadcasted_iota(jnp.int32, (QBLK, PER_HEAD), 1)
    out = jnp.zeros((QBLK, PER_HEAD), F32)
    for h in range(4):
        out = jnp.where(lane == h, col[h * QBLK:(h + 1) * QBLK], out)
    return out


def _spread_heads(x8, rows):
    masks = _head_masks4((rows, GRP))
    out = jnp.broadcast_to(x8[:, 0:1], (rows, GRP))
    for h in range(1, 4):
        out = jnp.where(masks[h], x8[:, h:h + 1], out)
    return out


def _attn_units(n, l, banded):
    nsub = 4 if l % (4 * QBLK) == 0 else (2 if l % (2 * QBLK) == 0 else 1)
    nseq = 4 if (banded and nsub == 1 and n % 4 == 0) else 1
    return nseq, nsub


def _attn_specs(banded, nseq, nsub, q_lane_blk, k_lane_blk, v_lane_blk):
    tq = nsub * QBLK
    qs = pl.BlockSpec((nseq, tq, GRP), lambda n, i: (n, i, q_lane_blk))
    if banded:
        ks = [pl.BlockSpec((nseq, QBLK, GRP), lambda n, i: (n, jnp.maximum(nsub * i - 1, 0), k_lane_blk)),
              pl.BlockSpec((nseq, tq, GRP), lambda n, i: (n, i, k_lane_blk))]
        vs = [pl.BlockSpec((nseq, QBLK, GRP), lambda n, i: (n, jnp.maximum(nsub * i - 1, 0), v_lane_blk)),
              pl.BlockSpec((nseq, tq, GRP), lambda n, i: (n, i, v_lane_blk))]
    else:
        ks = [pl.BlockSpec((nseq, 2 * QBLK, GRP), lambda n, i: (n, 0, k_lane_blk))]
        vs = [pl.BlockSpec((nseq, 2 * QBLK, GRP), lambda n, i: (n, 0, v_lane_blk))]
    return qs, ks, vs


def _attn_fwd(q, k, v, *, name, banded, slopes_scaled=None, q_lane_blk=0, k_lane_blk=0, v_lane_blk=0):
    n, l, _ = q.shape
    nseq, nsub = _attn_units(n, l, banded)
    units = [(sq, u) for sq in range(nseq) for u in range(nsub)]
    tq = nsub * QBLK
    nkv = 2 if banded else 1

    def body(*refs):
        q_ref = refs[0]
        k_refs = refs[1:1 + nkv]
        v_refs = refs[1 + nkv:1 + 2 * nkv]
        o_ref, lse_ref = refs[1 + 2 * nkv:]
        step = pl.program_id(1)
        masks = _head_masks4((QBLK, GRP))
        if banded:
            in_band, is_cur, bias = _band_bias(slopes_scaled)
        scs, v2s = [], []
        for sq, u in units:
            qs = q_ref[sq, u * QBLK:(u + 1) * QBLK, :] * jnp.asarray(ATT_SCALE, BF16)
            if banded:
                kall = jnp.concatenate([k_refs[0][sq], k_refs[1][sq]], axis=0)
                vall = jnp.concatenate([v_refs[0][sq], v_refs[1][sq]], axis=0)
                k2 = kall[u * QBLK:(u + 2) * QBLK]
                v2s.append(vall[u * QBLK:(u + 2) * QBLK])
            else:
                k2 = k_refs[0][sq]
                v2s.append(v_refs[0][sq])
            scs.append(_dot_nt(_stack_heads(qs, masks), k2))
        ps, dens, lses = [], [], []
        for j, (sq, u) in enumerate(units):
            sc = scs[j]
            if banded:
                valid = in_band & (is_cur | (step * nsub + u > 0))
                sc = jnp.where(valid, sc + bias, NEG_BIG)
            m = jnp.max(sc, axis=-1, keepdims=True)
            p = jnp.exp(sc - m)
            den = jnp.sum(p, axis=-1, keepdims=True)
            ps.append(p.astype(BF16))
            dens.append(den)
            lses.append(m + jnp.log(den))
        ohs = [_dot(ps[j], v2s[j]) for j in range(len(units))]
        for j, (sq, u) in enumerate(units):
            o_ref[sq, u * QBLK:(u + 1) * QBLK, :] = _unstack_heads(ohs[j] / dens[j], masks).astype(o_ref.dtype)
            lse_ref[sq, u * QBLK:(u + 1) * QBLK, :] = _head_lanes(lses[j])

    qs, ks, vs = _attn_specs(banded, nseq, nsub, q_lane_blk, k_lane_blk, v_lane_blk)
    ob = pl.BlockSpec((nseq, tq, GRP), lambda nn, i: (nn, i, 0))
    return pl.pallas_call(
        body, name=name, grid=(n // nseq, l // tq),
        in_specs=[qs] + ks + vs, out_specs=[ob, pl.BlockSpec((nseq, tq, PER_HEAD), lambda nn, i: (nn, i, 0))],
        out_shape=[jax.ShapeDtypeStruct((n, l, GRP), BF16), jax.ShapeDtypeStruct((n, l, PER_HEAD), F32)],
        compiler_params=_cp("parallel", "arbitrary"),
    )(q, *([k] * nkv), *([v] * nkv))


def _attn_bwd(q, k, v, do, lse, delta, *, name, banded, slopes_scaled=None, q_lane_blk=0, k_lane_blk=0,
              v_lane_blk=0, do_lane_blk=0):
    n, l, _ = q.shape
    nseq, nsub = _attn_units(n, l, banded)
    units = [(sq, u) for sq in range(nseq) for u in range(nsub)]
    tq = nsub * QBLK
    nsteps = l // tq
    nkv = 2 if banded else 1
    lk = l if banded else 2 * QBLK

    def body(*refs):
        q_ref = refs[0]
        k_refs = refs[1:1 + nkv]
        v_refs = refs[1 + nkv:1 + 2 * nkv]
        do_ref, lse_ref, dl_ref, dq_ref, dk_out, dv_out, dk_ref, dv_ref = refs[1 + 2 * nkv:]
        step = pl.program_id(1)

        @pl.when(step == 0)
        def _():
            dk_ref[...] = jnp.zeros_like(dk_ref)
            dv_ref[...] = jnp.zeros_like(dv_ref)

        masks = _head_masks4((QBLK, GRP))
        if banded:
            in_band, is_cur, bias = _band_bias(slopes_scaled)
        q_st, do_st, k2s, scs, dps = [], [], [], [], []
        for j, (sq, u) in enumerate(units):
            rows = slice(u * QBLK, (u + 1) * QBLK)
            if banded:
                kall = jnp.concatenate([k_refs[0][sq], k_refs[1][sq]], axis=0)
                vall = jnp.concatenate([v_refs[0][sq], v_refs[1][sq]], axis=0)
                k2s.append(kall[u * QBLK:(u + 2) * QBLK])
                v2 = vall[u * QBLK:(u + 2) * QBLK]
            else:
                k2s.append(k_refs[0][sq])
                v2 = v_refs[0][sq]
            qs = q_ref[sq, rows, :] * jnp.asarray(ATT_SCALE, BF16)
            q_st.append(_stack_heads(qs, masks))
            do_st.append(_stack_heads(do_ref[sq, rows, :], masks))
            scs.append(_dot_nt(q_st[j], k2s[j]))
            dps.append(_dot_nt(do_st[j], v2))
        pbs, dss = [], []
        for j, (sq, u) in enumerate(units):
            rows = slice(u * QBLK, (u + 1) * QBLK)
            sc = scs[j]
            if banded:
                valid = in_band & (is_cur | (step * nsub + u > 0))
                sc = jnp.where(valid, sc + bias, NEG_BIG)
            p = jnp.exp(sc - _head_column(lse_ref[sq, rows, :]))
            pbs.append(p.astype(BF16))
            dss.append((p * (dps[j] - _head_column(dl_ref[sq, rows, :]))).astype(BF16))
        dqs = [_dot(dss[j], k2s[j]) for j in range(len(units))]
        dk2s = [_dot_tn(dss[j], q_st[j]) for j in range(len(units))]
        dv2s = [_dot_tn(pbs[j], do_st[j]) for j in range(len(units))]
        for j, (sq, u) in enumerate(units):
            dq_ref[sq, u * QBLK:(u + 1) * QBLK, :] = (_unstack_heads(dqs[j], masks) * ATT_SCALE).astype(dq_ref.dtype)
        if banded:
            for j, (sq, u) in enumerate(units):
                i = step * nsub + u
                cur = pl.multiple_of(i * QBLK, QBLK)
                dk_ref[sq, pl.ds(cur, QBLK), :] += dk2s[j][QBLK:]
                dv_ref[sq, pl.ds(cur, QBLK), :] += dv2s[j][QBLK:]

                @pl.when(i > 0)
                def _():
                    prev = pl.multiple_of((i - 1) * QBLK, QBLK)
                    dk_ref[sq, pl.ds(prev, QBLK), :] += dk2s[j][:QBLK]
                    dv_ref[sq, pl.ds(prev, QBLK), :] += dv2s[j][:QBLK]
        else:
            dk_ref[0] += functools.reduce(jnp.add, dk2s)
            dv_ref[0] += functools.reduce(jnp.add, dv2s)

        @pl.when(step == nsteps - 1)
        def _():
            dk_out[...] = dk_ref[...].astype(dk_out.dtype)
            dv_out[...] = dv_ref[...].astype(dv_out.dtype)

    qs, ks, vs = _attn_specs(banded, nseq, nsub, q_lane_blk, k_lane_blk, v_lane_blk)
    ob = pl.BlockSpec((nseq, tq, GRP), lambda nn, i: (nn, i, 0))
    stat = pl.BlockSpec((nseq, tq, PER_HEAD), lambda nn, i: (nn, i, 0))
    dos = pl.BlockSpec((nseq, tq, GRP), lambda nn, i: (nn, i, do_lane_blk))
    kvb = pl.BlockSpec((nseq, lk, GRP), lambda nn, i: (nn, 0, 0))
    return pl.pallas_call(
        body, name=name, grid=(n // nseq, nsteps),
        in_specs=[qs] + ks + vs + [dos, stat, stat], out_specs=[ob, kvb, kvb],
        out_shape=[jax.ShapeDtypeStruct((n, l, GRP), BF16), jax.ShapeDtypeStruct((n, lk, GRP), BF16),
                   jax.ShapeDtypeStruct((n, lk, GRP), BF16)],
        scratch_shapes=[pltpu.VMEM((nseq, lk, GRP), F32), pltpu.VMEM((nseq, lk, GRP), F32)],
        compiler_params=_cp("parallel", "arbitrary"),
    )(q, *([k] * nkv), *([v] * nkv), do, lse, delta)


def _attn_delta(do, o, *, name, lane_blks):
    t, _ = do.shape
    tr = _pick(t, (512, 256, 128, 8))
    ng = len(lane_blks)

    def body(*refs):
        do_refs, o_refs, d_ref = refs[:ng], refs[ng:2 * ng], refs[2 * ng]
        ra = lax.broadcasted_iota(jnp.int32, (GRP, LANE), 0) // HEAD_DIM
        rb = lax.broadcasted_iota(jnp.int32, (GRP, LANE), 1)
        head_sum = (ra == rb).astype(BF16)
        prod = None
        for a_ref, b_ref in zip(do_refs, o_refs):
            term = a_ref[...].astype(F32) * b_ref[...].astype(F32)
            prod = term if prod is None else prod + term
        d_ref[...] = _dot_split(prod, head_sum)[:, :PER_HEAD]

    specs = [pl.BlockSpec((tr, GRP), functools.partial(lambda i, lb: (i, lb), lb=lb)) for lb in lane_blks]
    return pl.pallas_call(
        body, name=name, grid=(t // tr,), in_specs=specs + specs,
        out_specs=pl.BlockSpec((tr, PER_HEAD), lambda i: (i, 0)),
        out_shape=jax.ShapeDtypeStruct((t, PER_HEAD), F32),
        compiler_params=_cp("parallel"),
    )(*([do] * ng), *([o] * ng))


def _dil_combine(os, lses, *, name):
    t, _ = os[0].shape
    tr = _pick(t, (512, 256, 128, 8))
    ng = len(os)

    def body(*refs):
        o_refs, l_refs = refs[:ng], refs[ng:2 * ng]
        out_ref, lse_ref = refs[2 * ng:]
        ls = [r[...] for r in l_refs]
        m = functools.reduce(jnp.maximum, ls)
        tot = None
        for lv in ls:
            e = jnp.exp(lv - m)
            tot = e if tot is None else tot + e
        lse = m + jnp.log(tot)
        for g in range(ng):
            alpha = _spread_heads(jnp.exp(ls[g] - lse), tr)
            out_ref[:, GRP * g:GRP * (g + 1)] = (o_refs[g][...].astype(F32) * alpha).astype(out_ref.dtype)
        lse_ref[...] = lse

    sp = pl.BlockSpec((tr, GRP), lambda i: (i, 0))
    st = pl.BlockSpec((tr, PER_HEAD), lambda i: (i, 0))
    return pl.pallas_call(
        body, name=name, grid=(t // tr,), in_specs=[sp] * ng + [st] * ng,
        out_specs=[pl.BlockSpec((tr, GRP * ng), lambda i: (i, 0)), st],
        out_shape=[jax.ShapeDtypeStruct((t, GRP * ng), BF16), jax.ShapeDtypeStruct((t, PER_HEAD), F32)],
        compiler_params=_cp("parallel"),
    )(*os, *lses)


FFN_LB = 256
FFN_ROWS = 64
HALO = 16


def _conv_chunk(u_ref, w, ci):
    r0 = pl.multiple_of(ci * FFN_ROWS, FFN_ROWS)
    cur = u_ref[0, pl.ds(r0, FFN_ROWS), :].astype(F32)
    p0 = pl.multiple_of(jnp.maximum(r0 - HALO, 0), HALO)
    prev = u_ref[0, pl.ds(p0, HALO), :].astype(F32)
    prev = jnp.where(ci > 0, prev, 0.0)
    rowi = lax.broadcasted_iota(jnp.int32, (8, cur.shape[1]), 0)
    r1 = pltpu.roll(cur, 1, 0)
    r2 = pltpu.roll(cur, 2, 0)
    s1 = jnp.concatenate([jnp.where(rowi == 0, prev[HALO - 1:HALO], r1[0:8]), r1[8:]], axis=0)
    s2 = jnp.concatenate([jnp.where(rowi == 0, prev[HALO - 2:HALO - 1],
                                    jnp.where(rowi == 1, prev[HALO - 1:HALO], r2[0:8])), r2[8:]], axis=0)
    c = w[0:1] * s2
    c = c + w[1:2] * s1
    c = c + w[2:3] * cur
    return c, cur, s1, s2


def _ffn_mid_fwd(u, wconv, *, name):
    b, s, f2 = u.shape
    f = f2 // 2
    nlb = f // FFN_LB

    def body(ua_ref, ug_ref, wa_ref, wg_ref, h_ref):
        wa = wa_ref[...]
        wg = wg_ref[...]

        def step(ci, carry):
            ca = _conv_chunk(ua_ref, wa, ci)[0]
            cg = _conv_chunk(ug_ref, wg, ci)[0]
            r0 = pl.multiple_of(ci * FFN_ROWS, FFN_ROWS)
            h_ref[0, pl.ds(r0, FFN_ROWS), :] = (cg * jax.nn.sigmoid(cg) * ca).astype(h_ref.dtype)
            return carry

        lax.fori_loop(0, s // FFN_ROWS, step, 0)

    return pl.pallas_call(
        body, name=name, grid=(nlb, b),
        in_specs=[pl.BlockSpec((1, s, FFN_LB), lambda l, bb: (bb, 0, l)),
                  pl.BlockSpec((1, s, FFN_LB), lambda l, bb: (bb, 0, nlb + l)),
                  pl.BlockSpec((3, FFN_LB), lambda l, bb: (0, l)),
                  pl.BlockSpec((3, FFN_LB), lambda l, bb: (0, nlb + l))],
        out_specs=pl.BlockSpec((1, s, FFN_LB), lambda l, bb: (bb, 0, l)),
        out_shape=jax.ShapeDtypeStruct((b, s, f), BF16),
        compiler_params=_cp("parallel", "parallel"),
    )(u, u, wconv, wconv)


def _ffn_mid_bwd(u, wconv, dh, *, name):
    b, s, f2 = u.shape
    f = f2 // 2
    nlb = f // FFN_LB
    nchunk = s // FFN_ROWS

    def body(ua_ref, ug_ref, wa_ref, wg_ref, dh_ref, dua_ref, dug_ref, dwa_ref, dwg_ref):
        bb = pl.program_id(1)
        wa = wa_ref[...]
        wg = wg_ref[...]
        rowi = lax.broadcasted_iota(jnp.int32, (8, FFN_LB), 0)
        last = FFN_ROWS - 8

        def conv_transpose(dc, nxt, w):
            r1 = pltpu.roll(dc, FFN_ROWS - 1, 0)
            r2 = pltpu.roll(dc, FFN_ROWS - 2, 0)
            n1 = jnp.concatenate([r1[:last], jnp.where(rowi == 7, nxt[0:1], r1[last:])], axis=0)
            n2 = jnp.concatenate([r2[:last], jnp.where(rowi == 6, nxt[0:1],
                                                       jnp.where(rowi == 7, nxt[1:2], r2[last:]))], axis=0)
            return w[2:3] * dc + w[1:2] * n1 + w[0:1] * n2

        def step(t, carry):
            ci = nchunk - 1 - t
            nxt_a, nxt_g = carry[0], carry[1]
            r0 = pl.multiple_of(ci * FFN_ROWS, FFN_ROWS)
            ca, cura, s1a, s2a = _conv_chunk(ua_ref, wa, ci)
            cg, curg, s1g, s2g = _conv_chunk(ug_ref, wg, ci)
            dhv = dh_ref[0, pl.ds(r0, FFN_ROWS), :].astype(F32)
            sg = jax.nn.sigmoid(cg)
            da = dhv * (cg * sg)
            dg = dhv * ca * (sg * (1.0 + cg * (1.0 - sg)))
            dua_ref[0, pl.ds(r0, FFN_ROWS), :] = conv_transpose(da, nxt_a, wa).astype(dua_ref.dtype)
            dug_ref[0, pl.ds(r0, FFN_ROWS), :] = conv_transpose(dg, nxt_g, wg).astype(dug_ref.dtype)
            red = lambda x: jnp.sum(x, axis=0, keepdims=True)
            parts = (red(da * s2a), red(da * s1a), red(da * cura), red(dg * s2g), red(dg * s1g), red(dg * curg))
            return (da[0:8], dg[0:8]) + tuple(c + p for c, p in zip(carry[2:], parts))

        zero = jnp.zeros((1, FFN_LB), F32)
        zero8 = jnp.zeros((8, FFN_LB), F32)
        taps = lax.fori_loop(0, nchunk, step, (zero8, zero8) + (zero,) * 6)[2:]

        @pl.when(bb == 0)
        def _():
            for k in range(3):
                dwa_ref[k:k + 1, :] = taps[k]
                dwg_ref[k:k + 1, :] = taps[3 + k]

        @pl.when(bb > 0)
        def _():
            for k in range(3):
                dwa_ref[k:k + 1, :] += taps[k]
                dwg_ref[k:k + 1, :] += taps[3 + k]

    seq_a = pl.BlockSpec((1, s, FFN_LB), lambda l, bb: (bb, 0, l))
    seq_g = pl.BlockSpec((1, s, FFN_LB), lambda l, bb: (bb, 0, nlb + l))
    wsp = pl.BlockSpec((3, FFN_LB), lambda l, bb: (0, l))
    return pl.pallas_call(
        body, name=name, grid=(nlb, b),
        in_specs=[seq_a, seq_g, wsp, pl.BlockSpec((3, FFN_LB), lambda l, bb: (0, nlb + l)), seq_a],
        out_specs=[seq_a, seq_a, wsp, wsp],
        out_shape=[jax.ShapeDtypeStruct((b, s, f), BF16), jax.ShapeDtypeStruct((b, s, f), BF16),
                   jax.ShapeDtypeStruct((3, f), F32), jax.ShapeDtypeStruct((3, f), F32)],
        compiler_params=_cp("parallel", "arbitrary"),
    )(u, u, wconv, wconv, dh)


def _adam_math(w, g, m, v):
    m2 = ADAM_B1 * m + (1.0 - ADAM_B1) * g
    v2 = ADAM_B2 * v + (1.0 - ADAM_B2) * (g * g)
    m_hat = m2 / (1.0 - ADAM_B1 ** ADAM_STEP)
    v_hat = v2 / (1.0 - ADAM_B2 ** ADAM_STEP)
    delta = -ADAM_LR * (m_hat / (jnp.sqrt(v_hat) + ADAM_EPS) + ADAM_WD * w)
    return delta, m2, v2


def _adam(w, g, m, v, *, name):
    r, c = w.shape
    tr = _pick(r, (256, 128, 88, 64, 32, 16, 8))

    def body(w_ref, g_ref, m_ref, v_ref, d_ref, m2_ref, v2_ref):
        d, m2, v2 = _adam_math(w_ref[...], g_ref[...], m_ref[...], v_ref[...])
        d_ref[...] = d
        m2_ref[...] = m2
        v2_ref[...] = v2

    sp = pl.BlockSpec((tr, c), lambda i: (i, 0))
    return pl.pallas_call(
        body, name=name, grid=(r // tr,), in_specs=[sp] * 4, out_specs=[sp] * 3,
        out_shape=[jax.ShapeDtypeStruct((r, c), F32)] * 3,
        compiler_params=_cp("parallel"),
    )(w, g, m, v)


def _adam_small(quads, *, name):
    nq = len(quads)

    def body(*refs):
        ins, outs = refs[:4 * nq], refs[4 * nq:]
        for k in range(nq):
            w_ref, g_ref, m_ref, v_ref = ins[4 * k:4 * k + 4]
            d, m2, v2 = _adam_math(w_ref[...], g_ref[...], m_ref[...], v_ref[...])
            outs[3 * k][...] = d
            outs[3 * k + 1][...] = m2
            outs[3 * k + 2][...] = v2

    flat = [a for q in quads for a in q]
    out_shape = [jax.ShapeDtypeStruct(q[0].shape, F32) for q in quads for _ in range(3)]
    vm = pl.BlockSpec(memory_space=pltpu.VMEM)
    outs = pl.pallas_call(
        body, name=name, in_specs=[vm] * len(flat), out_specs=[vm] * len(out_shape), out_shape=out_shape,
        compiler_params=pltpu.CompilerParams(vmem_limit_bytes=VMEM_LIMIT_BYTES),
    )(*flat)
    return [tuple(outs[3 * k:3 * k + 3]) for k in range(nq)]


def _mesh_pos():
    return lax.axis_index("x"), lax.axis_index("y"), lax.axis_index("c")


def _flip(v, bit):
    return 1 - v if bit else v


def _all_gather_hbm(xl, *, name):
    r, c = xl.shape

    def body(x_ref, out_ref, send_sems, recv_sems, local_sem):
        x, y, cc = _mesh_pos()
        me, sibling = (x, y, cc), (x, y, 1 - cc)
        chips = [(1 - x, y), (x, 1 - y), (1 - x, 1 - y)]

        def rows(px, py, pc):
            return out_ref.at[pl.ds((4 * px + 2 * py + pc) * r, r), :]

        def copy(k, block, to, src=None):
            return pltpu.make_async_remote_copy(
                src_ref=rows(*block) if src is None else src, dst_ref=rows(*block),
                send_sem=send_sems.at[k], recv_sem=recv_sems.at[k], device_id=to, device_id_type=MESH_ID)

        mine = pltpu.make_async_copy(x_ref, rows(*me), local_sem)
        mine.start()
        first = [copy(0, me, sibling, src=x_ref)]
        first += [copy(1 + j, me, (*chip, cc), src=x_ref) for j, chip in enumerate(chips)]
        for cp in first:
            cp.start()
        passed = [copy(4 + j, (*chip, cc), sibling) for j, chip in enumerate(chips)]
        for j, chip in enumerate(chips):
            copy(1 + j, (*chip, cc), me).wait_recv()
            passed[j].start()
        copy(0, sibling, me).wait_recv()
        for j, chip in enumerate(chips):
            copy(4 + j, (*chip, 1 - cc), me).wait_recv()
        for cp in first + passed:
            cp.wait_send()
        mine.wait()

    hbm = pl.BlockSpec(memory_space=pltpu.HBM)
    return pl.pallas_call(
        body, name=name, in_specs=[hbm], out_specs=hbm,
        out_shape=jax.ShapeDtypeStruct((N_DEV * r, c), xl.dtype),
        scratch_shapes=[pltpu.SemaphoreType.DMA((7,)), pltpu.SemaphoreType.DMA((7,)), pltpu.SemaphoreType.DMA],
    )(xl)


def _all_reduce_small(xl, *, name):
    r, c = xl.shape

    def body(x_ref, sum_ref, all_ref, send_sems, recv_sems, local_sem):
        x, y, cc = _mesh_pos()
        me, sibling = (x, y, cc), (x, y, 1 - cc)
        chips = [(1 - x, y), (x, 1 - y), (1 - x, 1 - y)]

        def rows(px, py, pc):
            return all_ref.at[pl.ds((4 * px + 2 * py + pc) * r, r), :]

        def copy(k, block, to, src=None):
            return pltpu.make_async_remote_copy(
                src_ref=rows(*block) if src is None else src, dst_ref=rows(*block),
                send_sem=send_sems.at[k], recv_sem=recv_sems.at[k], device_id=to, device_id_type=MESH_ID)

        mine = pltpu.make_async_copy(x_ref, rows(*me), local_sem)
        mine.start()
        first = [copy(0, me, sibling, src=x_ref)]
        first += [copy(1 + j, me, (*chip, cc), src=x_ref) for j, chip in enumerate(chips)]
        for cp in first:
            cp.start()
        passed = [copy(4 + j, (*chip, cc), sibling) for j, chip in enumerate(chips)]
        for j, chip in enumerate(chips):
            copy(1 + j, (*chip, cc), me).wait_recv()
            passed[j].start()
        copy(0, sibling, me).wait_recv()
        for j, chip in enumerate(chips):
            copy(4 + j, (*chip, 1 - cc), me).wait_recv()
        for cp in first + passed:
            cp.wait_send()
        mine.wait()
        tot = all_ref[pl.ds(0, r), :]
        for dd in range(1, N_DEV):
            tot = tot + all_ref[pl.ds(dd * r, r), :]
        sum_ref[...] = tot

    vm = pl.BlockSpec(memory_space=pltpu.VMEM)
    return pl.pallas_call(
        body, name=name, in_specs=[vm], out_specs=[vm, vm],
        out_shape=[jax.ShapeDtypeStruct((r, c), F32), jax.ShapeDtypeStruct((N_DEV * r, c), F32)],
        scratch_shapes=[pltpu.SemaphoreType.DMA((7,)), pltpu.SemaphoreType.DMA((7,)), pltpu.SemaphoreType.DMA],
    )(xl)[0]


N_PEERS = N_DEV - 1
_HBM = pl.BlockSpec(memory_space=pltpu.HBM)
_SEM = pl.BlockSpec(memory_space=pltpu.SEMAPHORE)


def _peer_list(x, y, cc):
    return [(_flip(x, rel & 4), _flip(y, rel & 2), _flip(cc, rel & 1)) for rel in range(1, N_DEV)]


def _dev_index(p):
    return 4 * p[0] + 2 * p[1] + p[2]


def _split_copy(src_ref, land_ref, sems, k, peer, me, gather, landing_of):
    if gather:
        r = src_ref.shape[0]
        src = src_ref
        dst = land_ref.at[pl.ds(_dev_index(landing_of) * r, r), :]
    else:
        src = src_ref.at[_dev_index(peer)]
        dst = land_ref.at[_dev_index(landing_of)]
    return pltpu.make_async_remote_copy(src_ref=src, dst_ref=dst, send_sem=sems[k], recv_sem=sems[N_PEERS + k],
                                        device_id=peer, device_id_type=MESH_ID)


def _exchange_start(src, land, *, name, gather):
    def body(src_ref, land_ref, *rest):
        sems = rest[:2 * N_PEERS]
        token = rest[2 * N_PEERS + 2]
        x, y, cc = _mesh_pos()
        me = (x, y, cc)
        for k, peer in enumerate(_peer_list(x, y, cc)):
            _split_copy(src_ref, land_ref, sems, k, peer, me, gather, landing_of=me).start()
        token[...] = jnp.zeros_like(token)

    outs = pl.pallas_call(
        body, name=name,
        out_shape=tuple([pltpu.SemaphoreType.DMA(())] * (2 * N_PEERS)) + (
            pltpu.HBM(src.shape, src.dtype), pltpu.HBM(land.shape, land.dtype),
            jax.ShapeDtypeStruct((8, LANE), F32)),
        in_specs=(_HBM, _HBM),
        out_specs=tuple([_SEM] * (2 * N_PEERS)) + (_HBM, _HBM, pl.BlockSpec(memory_space=pltpu.VMEM)),
        input_output_aliases={0: 2 * N_PEERS, 1: 2 * N_PEERS + 1},
        compiler_params=pltpu.CompilerParams(has_side_effects=pltpu.SideEffectType.DATAFLOW_SIDE_EFFECTING),
    )(pltpu.with_memory_space_constraint(src, pltpu.HBM), pltpu.with_memory_space_constraint(land, pltpu.HBM))
    return outs[:2 * N_PEERS], outs[2 * N_PEERS], outs[2 * N_PEERS + 1], outs[2 * N_PEERS + 2]


def _gather_start(local, me, *, name):
    rows, cols = local.shape
    land = lax.dynamic_update_slice(lax.empty((N_DEV * rows, cols), local.dtype), local, (me * rows, 0))
    return _exchange_start(local, land, name=name, gather=True)


def _exchange_wait(sems, src_thru, land_thru, after, *, name, gather):
    def body(src_ref, land_ref, *rest):
        sem_refs = rest[:2 * N_PEERS]
        x, y, cc = _mesh_pos()
        me = (x, y, cc)
        for k, peer in enumerate(_peer_list(x, y, cc)):
            cp = _split_copy(src_ref, land_ref, sem_refs, k, peer, me, gather, landing_of=peer)
            cp.wait_send()
            cp.wait_recv()

    outs = pl.pallas_call(
        body, name=name,
        out_shape=(pltpu.HBM(src_thru.shape, src_thru.dtype), pltpu.HBM(land_thru.shape, land_thru.dtype)),
        in_specs=(_HBM, _HBM) + tuple([_SEM] * (2 * N_PEERS)) + (pl.BlockSpec(memory_space=pl.ANY),),
        out_specs=(_HBM, _HBM), input_output_aliases={0: 0, 1: 1},
        compiler_params=pltpu.CompilerParams(has_side_effects=pltpu.SideEffectType.DATAFLOW_SIDE_EFFECTING),
    )(src_thru, land_thru, *sems, after)
    return outs[1]


def _sum_blocks(recv, own, after, *, name):
    nd, r, c = recv.shape
    tr = _pick(r, (448, 256, 128, 64, 32, 16))

    def body(x_ref, own_ref, after_ref, o_ref):
        x, y, cc = _mesh_pos()
        me = 4 * x + 2 * y + cc
        tot = None
        for dd in range(nd):
            term = jnp.where(me == dd, own_ref[0], x_ref[dd]).astype(F32)
            tot = term if tot is None else tot + term
        o_ref[...] = tot

    return pl.pallas_call(
        body, name=name, grid=(r // tr,),
        in_specs=[pl.BlockSpec((nd, tr, c), lambda i: (0, i, 0)), pl.BlockSpec((1, tr, c), lambda i: (0, i, 0)),
                  pl.BlockSpec(memory_space=pl.ANY)],
        out_specs=pl.BlockSpec((tr, c), lambda i: (i, 0)),
        out_shape=jax.ShapeDtypeStruct((r, c), F32),
        compiler_params=_cp("parallel"),
    )(recv, own, after)


SHARD_KIND = {"a_w_in": "col", "a_w_out": "row", "a_w_mem_kv": "row", "a_ffn_up": "col", "a_ffn_down": "row",
              "w_kv_shared": "col", "b_w_in": "row", "b_w_out": "row", "b_w_mem_kv": "row", "b_ffn_up": "col",
              "b_ffn_down": "row"}
EARLY_WEIGHTS = ("a_w_in", "a_w_mem_kv")
FFN_UP_WEIGHTS = ("a_ffn_up", "b_ffn_up")
WIDE_WEIGHTS = tuple(nm for nm in SHARD_KIND if nm not in EARLY_WEIGHTS + FFN_UP_WEIGHTS)
LATE_WEIGHTS = WIDE_WEIGHTS + FFN_UP_WEIGHTS


def _as2d(a):
    return a.reshape(a.shape[-2], a.shape[-1]) if a.ndim >= 2 else a.reshape(1, a.shape[0])


def _pack_local(shards):
    return jnp.concatenate([_as2d(s).astype(BF16).reshape(-1, PACK_COLS) for s in shards], axis=0)


def _unpack_full(gathered, names, shapes):
    out = {}
    r0 = 0
    for name in names:
        rows, cols = shapes[name]
        nr = rows * cols // PACK_COLS
        blk = gathered[:, r0:r0 + nr, :].reshape(N_DEV, rows, cols)
        if SHARD_KIND[name] == "row":
            out[name] = blk.reshape(N_DEV * rows, cols)
        else:
            out[name] = blk.transpose(1, 0, 2).reshape(rows, N_DEV * cols)
        r0 += nr
    return out


def _pack_grads(grads, names, shapes):
    parts = []
    for name in names:
        rows, cols = shapes[name]
        g = grads[name]
        if SHARD_KIND[name] == "row":
            blk = g.reshape(N_DEV, rows, cols)
        else:
            blk = g.reshape(rows, N_DEV, cols).transpose(1, 0, 2)
        parts.append(blk.astype(BF16).reshape(N_DEV, rows * cols // PACK_COLS, PACK_COLS))
    return jnp.concatenate(parts, axis=1)


def _unpack_local(gsum, names, shapes):
    out = {}
    r0 = 0
    for name in names:
        rows, cols = shapes[name]
        nr = rows * cols // PACK_COLS
        out[name] = gsum[r0:r0 + nr].reshape(rows, cols)
        r0 += nr
    return out


def _by_residue(t, d):
    if d == 1:
        return t
    b, s, c = t.shape
    return t.reshape(b, s // d, d, c).transpose(0, 2, 1, 3).reshape(b * d, s // d, c)


def _from_residue(t, d, b):
    if d == 1:
        return t
    n, l, c = t.shape
    return t.reshape(b, d, l, c).transpose(0, 2, 1, 3).reshape(b, l * d, c)


def _alibi_slopes():
    return [2.0 ** (-ALIBI_MAX_BIAS * (i + 1) / N_DIL_HEADS) for i in range(N_DIL_HEADS)]


def _conv_ffn_fwd(xin, gain, w_up, wconv, w_down, tag, b, s):
    (n,), r = _rms_fwd(xin, [gain], name=f"{tag}_rms_ffn")
    u = _mm(n, w_up, name=f"{tag}_up", out_dtype=BF16).reshape(b, s, -1)
    hmid = _ffn_mid_fwd(u, wconv, name=f"{tag}_ffn_mid").reshape(b * s, -1)
    xout = _mm(hmid, w_down, name=f"{tag}_down", out_dtype=F32, res=xin)
    return xout, (n, r, u, hmid)


def _conv_ffn_bwd(dxout, xin, gain, saved, w_up, wconv, w_down, tag, b, s):
    n, r, u, hmid = saved
    f = hmid.shape[1]
    dhmid = _mm(dxout, w_down, name=f"{tag}_d_hmid", out_dtype=BF16, trans_b=True)
    g_down = _mm(hmid, dxout, name=f"{tag}_g_down", out_dtype=BF16, trans_a=True)
    du_a, du_g, gc_a, gc_g = _ffn_mid_bwd(u, wconv, dhmid.reshape(b, s, f), name=f"{tag}_ffn_mid_bwd")
    du_a = du_a.reshape(b * s, f)
    du_g = du_g.reshape(b * s, f)
    dn = _mm(du_a, w_up[:, :f], name=f"{tag}_d_n_a", out_dtype=F32, trans_b=True)
    dn = _mm(du_g, w_up[:, f:], name=f"{tag}_d_n_g", out_dtype=F32, res=dn, trans_b=True)
    g_up = jnp.concatenate([_mm(n, du_a, name=f"{tag}_g_up_a", out_dtype=BF16, trans_a=True),
                            _mm(n, du_g, name=f"{tag}_g_up_g", out_dtype=BF16, trans_a=True)], axis=1)
    dxin, (g_gain,) = _rms_bwd(xin, r, [(dn, gain)], dxout, name=f"{tag}_rms_ffn_bwd")
    return dxin, g_up, g_down, jnp.concatenate([gc_a, gc_g], axis=1), g_gain


def _mem_kv_fwd(mem2d, gain, w_mem_kv, tag, b):
    (nm,), rm = _rms_fwd(mem2d, [gain], name=f"{tag}_rms_mem")
    kvm = _mm(nm, w_mem_kv, name=f"{tag}_mem_kv", out_dtype=BF16)
    return kvm.reshape(b, -1, 2 * MEM_WIDTH), (nm, rm)


def _mem_kv_bwd(dk, dv, mem2d, gain, saved, w_mem_kv, tag):
    nm, rm = saved
    dkvm = jnp.concatenate([dk, dv], axis=-1).reshape(-1, 2 * MEM_WIDTH)
    dnm = _mm(dkvm, w_mem_kv, name=f"{tag}_d_nm", out_dtype=F32, trans_b=True)
    g_w = _mm(nm, dkvm, name=f"{tag}_g_mem_kv", out_dtype=BF16, trans_a=True)
    _, (g_gain,) = _rms_bwd(mem2d, rm, [(dnm, gain)], None, name=f"{tag}_rms_mem_bwd", need_dx=False)
    return g_w, g_gain


def kernel(x, mem, a_norm_attn, a_w_in, a_w_out, a_norm_mem, a_w_mem_kv, a_norm_ffn, a_ffn_up, a_ffn_conv, a_ffn_down, kv_norm, w_kv_shared, b_norm_attn, b_w_in, b_w_out, b_norm_mem, b_w_mem_kv, b_norm_ffn, b_ffn_up, b_ffn_conv, b_ffn_down, final_norm, loss_target, m_a_norm_attn, m_a_w_in, m_a_w_out, m_a_norm_mem, m_a_w_mem_kv, m_a_norm_ffn, m_a_ffn_up, m_a_ffn_conv, m_a_ffn_down, m_kv_norm, m_w_kv_shared, m_b_norm_attn, m_b_w_in, m_b_w_out, m_b_norm_mem, m_b_w_mem_kv, m_b_norm_ffn, m_b_ffn_up, m_b_ffn_conv, m_b_ffn_down, m_final_norm, v_a_norm_attn, v_a_w_in, v_a_w_out, v_a_norm_mem, v_a_w_mem_kv, v_a_norm_ffn, v_a_ffn_up, v_a_ffn_conv, v_a_ffn_down, v_kv_norm, v_w_kv_shared, v_b_norm_attn, v_b_w_in, v_b_w_out, v_b_norm_mem, v_b_w_mem_kv, v_b_norm_ffn, v_b_ffn_up, v_b_ffn_conv, v_b_ffn_down, v_final_norm):
    names = ["a_norm_attn", "a_w_in", "a_w_out", "a_norm_mem", "a_w_mem_kv", "a_norm_ffn", "a_ffn_up",
             "a_ffn_conv", "a_ffn_down", "kv_norm", "w_kv_shared", "b_norm_attn", "b_w_in", "b_w_out",
             "b_norm_mem", "b_w_mem_kv", "b_norm_ffn", "b_ffn_up", "b_ffn_conv", "b_ffn_down", "final_norm"]
    wl = dict(zip(names, [a_norm_attn, a_w_in, a_w_out, a_norm_mem, a_w_mem_kv, a_norm_ffn, a_ffn_up,
                          a_ffn_conv, a_ffn_down, kv_norm, w_kv_shared, b_norm_attn, b_w_in, b_w_out,
                          b_norm_mem, b_w_mem_kv, b_norm_ffn, b_ffn_up, b_ffn_conv, b_ffn_down, final_norm]))
    ml = dict(zip(names, [m_a_norm_attn, m_a_w_in, m_a_w_out, m_a_norm_mem, m_a_w_mem_kv, m_a_norm_ffn,
                          m_a_ffn_up, m_a_ffn_conv, m_a_ffn_down, m_kv_norm, m_w_kv_shared, m_b_norm_attn,
                          m_b_w_in, m_b_w_out, m_b_norm_mem, m_b_w_mem_kv, m_b_norm_ffn, m_b_ffn_up,
                          m_b_ffn_conv, m_b_ffn_down, m_final_norm]))
    vl = dict(zip(names, [v_a_norm_attn, v_a_w_in, v_a_w_out, v_a_norm_mem, v_a_w_mem_kv, v_a_norm_ffn,
                          v_a_ffn_up, v_a_ffn_conv, v_a_ffn_down, v_kv_norm, v_w_kv_shared, v_b_norm_attn,
                          v_b_w_in, v_b_w_out, v_b_norm_mem, v_b_w_mem_kv, v_b_norm_ffn, v_b_ffn_up,
                          v_b_ffn_conv, v_b_ffn_down, v_final_norm]))
    b, s, d = x.shape
    t = b * s
    my_x, my_y, my_c = _mesh_pos()
    me = 4 * my_x + 2 * my_y + my_c

    shapes = {nm: _as2d(wl[nm]).shape for nm in SHARD_KIND}
    early_local = _pack_local([wl[nm] for nm in EARLY_WEIGHTS])
    early_all = _all_gather_hbm(early_local, name="gather_early").reshape(N_DEV, early_local.shape[0], PACK_COLS)
    wf = _unpack_full(early_all, EARLY_WEIGHTS, shapes)
    wide_local = _pack_local([wl[nm] for nm in WIDE_WEIGHTS])
    up_local = jnp.concatenate([_as2d(wl[nm]).astype(BF16) for nm in FFN_UP_WEIGHTS], axis=0)
    gw_sems, gw_src, gw_land, gw_token = _gather_start(wide_local, me, name="gather_wide_start")
    gu_sems, gu_src, gu_land, gu_token = _gather_start(up_local, me, name="gather_up_start")

    sharded_small = ["a_norm_attn", "a_norm_mem", "a_norm_ffn", "a_ffn_conv", "b_ffn_conv"]
    small_flat = jnp.concatenate([wl[nm].reshape(-1) for nm in sharded_small])
    n_small = small_flat.shape[0]
    small_rows = -(-n_small // (8 * LANE)) * 8
    small_local = jnp.pad(small_flat, (0, small_rows * LANE - n_small)).reshape(small_rows, LANE)
    small_all = _all_gather_hbm(small_local, name="gather_small").reshape(N_DEV, small_rows * LANE)
    sfull = {}
    r0 = 0
    for nm in sharded_small:
        rows, cols = _as2d(wl[nm]).shape
        blk = small_all[:, r0:r0 + rows * cols].reshape(N_DEV, rows, cols)
        sfull[nm] = blk.transpose(1, 0, 2).reshape(rows, N_DEV * cols)
        r0 += rows * cols
    gain = {nm: sfull[nm] for nm in ("a_norm_attn", "a_norm_mem", "a_norm_ffn")}
    for nm in ("kv_norm", "b_norm_attn", "b_norm_mem", "b_norm_ffn", "final_norm"):
        gain[nm] = _as2d(wl[nm])
    conv_a, conv_b = sfull["a_ffn_conv"], sfull["b_ffn_conv"]

    x2d = x.reshape(t, d)
    mem2d = mem.reshape(-1, d)
    tgt2d = loss_target.reshape(t, d)
    qmem_blk_a = 3 * SB_WIDTH // GRP
    qmem_blk_b = DIL_WIDTH // GRP

    (n1,), r1 = _rms_fwd(x2d, [gain["a_norm_attn"]], name="a_rms_attn")
    proj_a = _mm(n1, wf["a_w_in"], name="a_in", out_dtype=BF16).reshape(b, s, -1)
    kvm_a, mem_saved_a = _mem_kv_fwd(mem2d, gain["a_norm_mem"], wf["a_w_mem_kv"], "a", b)
    o_sb, rsum = _sb_fwd(proj_a, gw_token + gu_token, name="a_sb_fwd")
    o_mem_a, lse_mem_a = _attn_fwd(proj_a, kvm_a, kvm_a, name="a_mem_fwd", banded=False,
                                   q_lane_blk=qmem_blk_a, k_lane_blk=0, v_lane_blk=1)
    wide_all = _exchange_wait(gw_sems, gw_src, gw_land, rsum, name="gather_wide_wait", gather=True)
    up_all = _exchange_wait(gu_sems, gu_src, gu_land, rsum, name="gather_up_wait", gather=True)
    wf.update(_unpack_full(wide_all.reshape(N_DEV, wide_local.shape[0], PACK_COLS), WIDE_WEIGHTS, shapes))
    up_rows, up_cols = shapes[FFN_UP_WEIGHTS[0]]
    up_all = up_all.reshape(N_DEV, len(FFN_UP_WEIGHTS), up_rows, up_cols)
    for k, nm in enumerate(FFN_UP_WEIGHTS):
        wf[nm] = up_all[:, k].transpose(1, 0, 2).reshape(up_rows, N_DEV * up_cols)
    cat_a = jnp.concatenate([o_sb, o_mem_a], axis=-1).reshape(t, d)
    x1 = _mm(cat_a, wf["a_w_out"], name="a_out", out_dtype=F32, res=x2d)
    xa, ffn_saved_a = _conv_ffn_fwd(x1, gain["a_norm_ffn"], wf["a_ffn_up"], conv_a, wf["a_ffn_down"], "a", b, s)

    (nk, n3), r3 = _rms_fwd(xa, [gain["kv_norm"], gain["b_norm_attn"]], name="b_rms_attn")
    kvsh = _mm(nk, wf["w_kv_shared"], name="kv_shared", out_dtype=BF16).reshape(b, s, -1)
    proj_b = _mm(n3, wf["b_w_in"], name="b_in", out_dtype=BF16).reshape(b, s, -1)
    kvm_b, mem_saved_b = _mem_kv_fwd(mem2d, gain["b_norm_mem"], wf["b_w_mem_kv"], "b", b)
    slopes = _alibi_slopes()
    dil_q, dil_k, dil_v, dil_o, dil_lse, dil_slopes = [], [], [], [], [], []
    for g, (_, dil) in enumerate(DIL_GROUPS):
        qg = _by_residue(proj_b[:, :, GRP * g:GRP * (g + 1)], dil)
        kg = _by_residue(kvsh[:, :, GRP * g:GRP * (g + 1)], dil)
        vg = _by_residue(kvsh[:, :, DIL_WIDTH + GRP * g:DIL_WIDTH + GRP * (g + 1)], dil)
        sl = [slopes[4 * g + h] * dil for h in range(4)]
        og, lg = _attn_fwd(qg, kg, vg, name=f"b_dil{g}_fwd", banded=True, slopes_scaled=sl)
        dil_q.append(qg)
        dil_k.append(kg)
        dil_v.append(vg)
        dil_slopes.append(sl)
        dil_o.append(_from_residue(og, dil, b).reshape(t, GRP))
        dil_lse.append(_from_residue(lg, dil, b).reshape(t, PER_HEAD))
    o_dil, lse_joint = _dil_combine(dil_o, dil_lse, name="b_dil_combine")
    o_mem_b, lse_mem_b = _attn_fwd(proj_b, kvm_b, kvm_b, name="b_mem_fwd", banded=False,
                                   q_lane_blk=qmem_blk_b, k_lane_blk=0, v_lane_blk=1)
    cat_b = jnp.concatenate([o_dil, o_mem_b.reshape(t, MEM_WIDTH)], axis=-1)
    x3 = _mm(cat_b, wf["b_w_out"], name="b_out", out_dtype=F32, res=xa)
    xb, ffn_saved_b = _conv_ffn_fwd(x3, gain["b_norm_ffn"], wf["b_ffn_up"], conv_b, wf["b_ffn_down"], "b", b, s)

    dxb, g_final, loss_vec = _loss_head(xb, gain["final_norm"], tgt2d, name="loss_head")

    grads = {}
    sgrads = {"final_norm": g_final}
    dx3, grads["b_ffn_up"], grads["b_ffn_down"], sgrads["b_ffn_conv"], sgrads["b_norm_ffn"] = _conv_ffn_bwd(
        dxb, x3, gain["b_norm_ffn"], ffn_saved_b, wf["b_ffn_up"], conv_b, wf["b_ffn_down"], "b", b, s)
    dcat_b = _mm(dx3, wf["b_w_out"], name="b_d_cat", out_dtype=BF16, trans_b=True)
    grads["b_w_out"] = _mm(cat_b, dx3, name="b_g_out", out_dtype=BF16, trans_a=True)
    dcat_b3 = dcat_b.reshape(b, s, d)
    delta_mem_b = _attn_delta(dcat_b, cat_b, name="b_mem_delta", lane_blks=[qmem_blk_b]).reshape(b, s, PER_HEAD)
    dq_mem_b, dkm_b, dvm_b = _attn_bwd(proj_b, kvm_b, kvm_b, dcat_b3, lse_mem_b, delta_mem_b, name="b_mem_bwd",
                                       banded=False, q_lane_blk=qmem_blk_b, k_lane_blk=0, v_lane_blk=1,
                                       do_lane_blk=qmem_blk_b)
    delta_dil = _attn_delta(dcat_b, cat_b, name="b_dil_delta", lane_blks=[0, 1, 2]).reshape(b, s, PER_HEAD)
    lse_joint3 = lse_joint.reshape(b, s, PER_HEAD)
    dq_parts, dk_parts, dv_parts = [], [], []
    for g, (_, dil) in enumerate(DIL_GROUPS):
        dog = _by_residue(dcat_b3[:, :, GRP * g:GRP * (g + 1)], dil)
        lg = _by_residue(lse_joint3, dil)
        dg = _by_residue(delta_dil, dil)
        dqg, dkg, dvg = _attn_bwd(dil_q[g], dil_k[g], dil_v[g], dog, lg, dg, name=f"b_dil{g}_bwd", banded=True,
                                  slopes_scaled=dil_slopes[g])
        dq_parts.append(_from_residue(dqg, dil, b))
        dk_parts.append(_from_residue(dkg, dil, b))
        dv_parts.append(_from_residue(dvg, dil, b))
    dproj_b = jnp.concatenate(dq_parts + [dq_mem_b], axis=-1).reshape(t, d)
    dn3 = _mm(dproj_b, wf["b_w_in"], name="b_d_n", out_dtype=F32, trans_b=True)
    grads["b_w_in"] = _mm(n3, dproj_b, name="b_g_in", out_dtype=BF16, trans_a=True)
    grads["b_w_mem_kv"], sgrads["b_norm_mem"] = _mem_kv_bwd(dkm_b, dvm_b, mem2d, gain["b_norm_mem"], mem_saved_b,
                                                           wf["b_w_mem_kv"], "b")
    dkvsh = jnp.concatenate(dk_parts + dv_parts, axis=-1).reshape(t, 2 * DIL_WIDTH).astype(BF16)
    dnk = _mm(dkvsh, wf["w_kv_shared"], name="kv_d_n", out_dtype=F32, trans_b=True)
    grads["w_kv_shared"] = _mm(nk, dkvsh, name="kv_g", out_dtype=BF16, trans_a=True)
    dxa, (sgrads["kv_norm"], sgrads["b_norm_attn"]) = _rms_bwd(
        xa, r3, [(dnk, gain["kv_norm"]), (dn3, gain["b_norm_attn"])], dx3, name="b_rms_attn_bwd")

    dx1, grads["a_ffn_up"], grads["a_ffn_down"], sgrads["a_ffn_conv"], sgrads["a_norm_ffn"] = _conv_ffn_bwd(
        dxa, x1, gain["a_norm_ffn"], ffn_saved_a, wf["a_ffn_up"], conv_a, wf["a_ffn_down"], "a", b, s)
    dcat_a = _mm(dx1, wf["a_w_out"], name="a_d_cat", out_dtype=BF16, trans_b=True)
    grads["a_w_out"] = _mm(cat_a, dx1, name="a_g_out", out_dtype=BF16, trans_a=True)
    dcat_a3 = dcat_a.reshape(b, s, d)
    delta_mem_a = _attn_delta(dcat_a, cat_a, name="a_mem_delta", lane_blks=[qmem_blk_b]).reshape(b, s, PER_HEAD)
    dq_mem_a, dkm_a, dvm_a = _attn_bwd(proj_a, kvm_a, kvm_a, dcat_a3, lse_mem_a, delta_mem_a, name="a_mem_bwd",
                                       banded=False, q_lane_blk=qmem_blk_a, k_lane_blk=0, v_lane_blk=1,
                                       do_lane_blk=qmem_blk_b)
    wide_grads = _pack_grads(grads, WIDE_WEIGHTS, shapes)
    up_grads = jnp.concatenate([grads[nm].reshape(up_rows, N_DEV, up_cols).transpose(1, 0, 2)
                                for nm in FFN_UP_WEIGHTS], axis=1)
    own_wide = lax.dynamic_slice(wide_grads, (me, 0, 0), (1,) + wide_grads.shape[1:])
    own_up = lax.dynamic_slice(up_grads, (me, 0, 0), (1,) + up_grads.shape[1:])
    xw_sems, xw_src, xw_land, xw_token = _exchange_start(wide_grads, lax.empty(wide_grads.shape, BF16),
                                                         name="grads_wide_start", gather=False)
    xu_sems, xu_src, xu_land, xu_token = _exchange_start(up_grads, lax.empty(up_grads.shape, BF16),
                                                         name="grads_up_start", gather=False)
    dq_sb, dk_sb, dv_sb = _sb_bwd(proj_a, dcat_a3, rsum, xw_token + xu_token, name="a_sb_bwd")
    dproj_a = jnp.concatenate([dq_sb, dk_sb, dv_sb, dq_mem_a], axis=-1).reshape(t, -1)
    dn1 = _mm(dproj_a, wf["a_w_in"], name="a_d_n", out_dtype=F32, trans_b=True)
    grads["a_w_in"] = _mm(n1, dproj_a, name="a_g_in", out_dtype=BF16, trans_a=True)
    grads["a_w_mem_kv"], sgrads["a_norm_mem"] = _mem_kv_bwd(dkm_a, dvm_a, mem2d, gain["a_norm_mem"], mem_saved_a,
                                                           wf["a_w_mem_kv"], "a")
    early_grads = _pack_grads(grads, EARLY_WEIGHTS, shapes)
    own_early = lax.dynamic_slice(early_grads, (me, 0, 0), (1,) + early_grads.shape[1:])
    ee_sems, ee_src, ee_land, ee_token = _exchange_start(early_grads, lax.empty(early_grads.shape, BF16),
                                                         name="grads_early_start", gather=False)
    dx0, (sgrads["a_norm_attn"],) = _rms_bwd(x2d, r1, [(dn1, gain["a_norm_attn"])], dx1, name="a_rms_attn_bwd")
    grad_x = dx0.reshape(b, s, d)

    wide_recv = _exchange_wait(xw_sems, xw_src, xw_land, dx0, name="grads_wide_wait", gather=False)
    up_recv = _exchange_wait(xu_sems, xu_src, xu_land, dx0, name="grads_up_wait", gather=False)
    gl = _unpack_local(_sum_blocks(wide_recv, own_wide, ee_token, name="sum_grads_wide"), WIDE_WEIGHTS, shapes)
    up_sum = _sum_blocks(up_recv, own_up, ee_token, name="sum_grads_up")
    for k, nm in enumerate(FFN_UP_WEIGHTS):
        gl[nm] = up_sum[k * up_rows:(k + 1) * up_rows]

    small_names = ["a_norm_attn", "a_norm_mem", "a_norm_ffn", "kv_norm", "b_norm_attn", "b_norm_mem",
                   "b_norm_ffn", "final_norm", "a_ffn_conv", "b_ffn_conv"]
    small_flat = jnp.concatenate([sgrads[nm].reshape(-1) for nm in small_names] + [loss_vec.reshape(-1)])
    n_flat = small_flat.shape[0]
    red_rows = -(-n_flat // (8 * PACK_COLS)) * 8
    small_pack = jnp.pad(small_flat, (0, red_rows * PACK_COLS - n_flat)).reshape(red_rows, PACK_COLS)
    small_sum = _all_reduce_small(small_pack, name="reduce_small").reshape(-1)
    r0 = 0
    for nm in small_names:
        rows, cols = sgrads[nm].shape
        full = small_sum[r0:r0 + rows * cols].reshape(rows, cols)
        r0 += rows * cols
        if nm in sharded_small:
            lc = cols // N_DEV
            gl[nm] = lax.dynamic_slice(full, (0, me * lc), (rows, lc))
        else:
            gl[nm] = full
    loss = (0.5 / d) * jnp.sum(small_sum[r0:r0 + d])

    upd = {}
    for nm in LATE_WEIGHTS:
        upd[nm] = _adam(_as2d(wl[nm]), gl[nm], _as2d(ml[nm]), _as2d(vl[nm]), name=f"adam_{nm}")
    res_small = _adam_small([(_as2d(wl[nm]), gl[nm], _as2d(ml[nm]), _as2d(vl[nm])) for nm in small_names],
                            name="adam_small")
    for nm, r in zip(small_names, res_small):
        upd[nm] = r
    early_recv = _exchange_wait(ee_sems, ee_src, ee_land, upd[LATE_WEIGHTS[-1]][0], name="grads_early_wait",
                                gather=False)
    gl.update(_unpack_local(_sum_blocks(early_recv, own_early, ee_token, name="sum_grads_early"), EARLY_WEIGHTS,
                            shapes))
    for nm in EARLY_WEIGHTS:
        upd[nm] = _adam(_as2d(wl[nm]), gl[nm], _as2d(ml[nm]), _as2d(vl[nm]), name=f"adam_{nm}")

    g_out = [gl[nm].reshape(wl[nm].shape) for nm in names]
    d_out = [upd[nm][0].reshape(wl[nm].shape) for nm in names]
    m_out = [upd[nm][1].reshape(wl[nm].shape) for nm in names]
    v_out = [upd[nm][2].reshape(wl[nm].shape) for nm in names]
    return (loss, grad_x, *g_out, *d_out, *m_out, *v_out)
```

```python
import functools

import jax
import jax.numpy as jnp
from jax import lax
from jax.experimental import pallas as pl
from jax.experimental.pallas import tpu as pltpu

F32 = jnp.float32
BF16 = jnp.bfloat16

N_DEV = 8
HEAD_DIM = 64
N_SB_HEADS = 12
N_DIL_HEADS = 12
DIL_GROUPS = ((128, 1), (512, 4), (2048, 16))
SB_WIDTH = N_SB_HEADS * HEAD_DIM
MEM_WIDTH = 256
DIL_WIDTH = N_DIL_HEADS * HEAD_DIM
ATT_SCALE = HEAD_DIM ** -0.5
EPS = 1e-6
ALIBI_MAX_BIAS = 8.0
NEG_BIG = -1e30

ADAM_LR = 0.001
ADAM_B1 = 0.9
ADAM_B2 = 0.999
ADAM_EPS = 1e-08
ADAM_WD = 0.01
ADAM_STEP = 10

LANE = 128
QBLK = 128
VMEM_LIMIT_BYTES = 48 * 1024 * 1024
PACK_COLS = 1024
MESH_ID = pl.DeviceIdType.MESH


def _cp(*sem):
    return pltpu.CompilerParams(dimension_semantics=sem, vmem_limit_bytes=VMEM_LIMIT_BYTES)


def _pick(n, cands):
    for c in cands:
        if n % c == 0:
            return c
    raise ValueError(f"no tile for {n} in {cands}")


def _dot(a, b):
    return jnp.dot(a, b, preferred_element_type=F32)


def _dot_nt(a, b):
    return lax.dot_general(a, b, (((1,), (1,)), ((), ())), preferred_element_type=F32)


def _dot_tn(a, b):
    return lax.dot_general(a, b, (((0,), (0,)), ((), ())), preferred_element_type=F32)


def _dot_split(x, u):
    hi = x.astype(BF16)
    lo = (x - hi.astype(F32)).astype(BF16)
    return _dot(hi, u) + _dot(lo, u)


def _mm(a, b, *, name, out_dtype, res=None, trans_a=False, trans_b=False):
    assert not (trans_a and trans_b)
    if trans_a:
        kdim, m = a.shape
    else:
        m, kdim = a.shape
    if trans_b:
        n, kb = b.shape
    else:
        kb, n = b.shape
    assert kb == kdim, (a.shape, b.shape)
    if trans_a:
        tm = _pick(m, (1408, 1024, 512, 256, 128))
        tn = _pick(n, (1024, 1280, 1408, 768, 512, 256, 128))
        tk = _pick(kdim, (1024, 512, 256))
    else:
        tm = _pick(m, (1024, 512, 256, 128))
        tk = kdim if kdim <= 2816 else _pick(kdim, (2048, 1536, 1408, 1280, 1024, 512))
        tn = _pick(n, (512, 256, 128) if tk > 2048 else (1408, 1280, 1024, 768, 512, 256, 128))
    nk = kdim // tk
    has_res = res is not None

    def body(*refs):
        if has_res:
            a_ref, b_ref, r_ref, o_ref = refs[:4]
            scr = refs[4:]
        else:
            a_ref, b_ref, o_ref = refs[:3]
            r_ref = None
            scr = refs[3:]
        av = a_ref[...].astype(BF16)
        bv = b_ref[...].astype(BF16)
        if trans_a:
            p = _dot_tn(av, bv)
        elif trans_b:
            p = _dot_nt(av, bv)
        else:
            p = _dot(av, bv)

        def finish(acc):
            if has_res:
                acc = acc + r_ref[...]
            o_ref[...] = acc.astype(o_ref.dtype)

        if nk == 1:
            finish(p)
        else:
            acc_ref = scr[0]
            k = pl.program_id(2)

            @pl.when(k == 0)
            def _():
                acc_ref[...] = p

            @pl.when(k > 0)
            def _():
                acc_ref[...] += p

            @pl.when(k == nk - 1)
            def _():
                finish(acc_ref[...])

    if trans_a:
        a_spec = pl.BlockSpec((tk, tm), lambda i, j, k: (k, i))
    else:
        a_spec = pl.BlockSpec((tm, tk), lambda i, j, k: (i, k))
    if trans_b:
        b_spec = pl.BlockSpec((tn, tk), lambda i, j, k: (j, k))
    else:
        b_spec = pl.BlockSpec((tk, tn), lambda i, j, k: (k, j))
    in_specs = [a_spec, b_spec]
    args = [a, b]
    if has_res:
        in_specs.append(pl.BlockSpec((tm, tn), lambda i, j, k: (i, j)))
        args.append(res)
    return pl.pallas_call(
        body, name=name,
        grid=(m // tm, n // tn, nk),
        in_specs=in_specs,
        out_specs=pl.BlockSpec((tm, tn), lambda i, j, k: (i, j)),
        out_shape=jax.ShapeDtypeStruct((m, n), out_dtype),
        scratch_shapes=[pltpu.VMEM((tm, tn), F32)] if nk > 1 else [],
        compiler_params=_cp("parallel", "parallel", "arbitrary"),
    )(*args)


def _rms_fwd(x, gains, *, name):
    t, d = x.shape
    tr = _pick(t, (512, 256, 128, 8))
    ng = len(gains)

    def body(x_ref, *rest):
        g_refs, n_refs, r_ref = rest[:ng], rest[ng:2 * ng], rest[2 * ng]
        xv = x_ref[...]
        r = lax.rsqrt(jnp.mean(xv * xv, axis=-1, keepdims=True) + EPS)
        xh = xv * r
        for g_ref, n_ref in zip(g_refs, n_refs):
            n_ref[...] = (xh * g_ref[...]).astype(BF16)
        r_ref[...] = r

    row = pl.BlockSpec((tr, d), lambda i: (i, 0))
    gsp = pl.BlockSpec((1, d), lambda i: (0, 0))
    outs = pl.pallas_call(
        body, name=name, grid=(t // tr,),
        in_specs=[row] + [gsp] * ng,
        out_specs=[row] * ng + [pl.BlockSpec((tr, 1), lambda i: (i, 0))],
        out_shape=[jax.ShapeDtypeStruct((t, d), BF16)] * ng + [jax.ShapeDtypeStruct((t, 1), F32)],
        compiler_params=_cp("parallel"),
    )(x, *gains)
    return list(outs[:ng]), outs[ng]


def _rms_bwd(x, r, pairs, dres, *, name, need_dx=True):
    t, d = x.shape
    tr = _pick(t, (512, 256, 128, 8))
    npair = len(pairs)
    has_res = dres is not None

    def body(*refs):
        x_ref, r_ref = refs[:2]
        pr = refs[2:2 + 2 * npair]
        pos = 2 + 2 * npair
        res_ref = None
        if has_res:
            res_ref = refs[pos]
            pos += 1
        dx_ref = None
        if need_dx:
            dx_ref = refs[pos]
            pos += 1
        dg_refs = refs[pos:pos + npair]
        i = pl.program_id(0)
        rv = r_ref[...]
        xh = x_ref[...] * rv
        dx = res_ref[...] if has_res else None
        for k in range(npair):
            dn = pr[2 * k][...].astype(F32)
            g = pr[2 * k + 1][...]
            part = jnp.sum(dn * xh, axis=0, keepdims=True)

            @pl.when(i == 0)
            def _():
                dg_refs[k][...] = part

            @pl.when(i > 0)
            def _():
                dg_refs[k][...] += part

            if need_dx:
                dxh = dn * g
                c = jnp.mean(dxh * xh, axis=-1, keepdims=True)
                term = rv * (dxh - xh * c)
                dx = term if dx is None else dx + term
        if need_dx:
            dx_ref[...] = dx

    row = pl.BlockSpec((tr, d), lambda i: (i, 0))
    gsp = pl.BlockSpec((1, d), lambda i: (0, 0))
    in_specs = [row, pl.BlockSpec((tr, 1), lambda i: (i, 0))]
    args = [x, r]
    for dn, g in pairs:
        in_specs += [row, gsp]
        args += [dn, g]
    if has_res:
        in_specs.append(row)
        args.append(dres)
    out_specs, out_shape = [], []
    if need_dx:
        out_specs.append(row)
        out_shape.append(jax.ShapeDtypeStruct((t, d), F32))
    out_specs += [gsp] * npair
    out_shape += [jax.ShapeDtypeStruct((1, d), F32)] * npair
    outs = pl.pallas_call(
        body, name=name, grid=(t // tr,), in_specs=in_specs, out_specs=out_specs, out_shape=out_shape,
        compiler_params=_cp("arbitrary"),
    )(*args)
    if need_dx:
        return outs[0], list(outs[1:])
    return None, list(outs)


def _loss_head(h, g, tgt, *, name):
    t, d = h.shape
    tr = _pick(t, (512, 256, 128, 8))

    def body(h_ref, g_ref, t_ref, dh_ref, dg_ref, l_ref):
        i = pl.program_id(0)
        xv = h_ref[...]
        gv = g_ref[...]
        r = lax.rsqrt(jnp.mean(xv * xv, axis=-1, keepdims=True) + EPS)
        xh = xv * r
        e = xh * gv - t_ref[...]
        dy = e * (1.0 / d)
        lpart = jnp.sum(e * e, axis=0, keepdims=True)
        gpart = jnp.sum(dy * xh, axis=0, keepdims=True)

        @pl.when(i == 0)
        def _():
            l_ref[...] = lpart
            dg_ref[...] = gpart

        @pl.when(i > 0)
        def _():
            l_ref[...] += lpart
            dg_ref[...] += gpart

        dxh = dy * gv
        c = jnp.mean(dxh * xh, axis=-1, keepdims=True)
        dh_ref[...] = r * (dxh - xh * c)

    row = pl.BlockSpec((tr, d), lambda i: (i, 0))
    gsp = pl.BlockSpec((1, d), lambda i: (0, 0))
    return pl.pallas_call(
        body, name=name, grid=(t // tr,), in_specs=[row, gsp, row], out_specs=[row, gsp, gsp],
        out_shape=[jax.ShapeDtypeStruct((t, d), F32), jax.ShapeDtypeStruct((1, d), F32),
                   jax.ShapeDtypeStruct((1, d), F32)],
        compiler_params=_cp("arbitrary"),
    )(h, g, tgt)


GRP = 4 * HEAD_DIM
SB_KB = 2 * QBLK
SB_QB = 2 * QBLK


def _head_masks4(shape):
    lane = lax.broadcasted_iota(jnp.int32, shape, 1)
    return [(lane >= HEAD_DIM * h) & (lane < HEAD_DIM * (h + 1)) for h in range(4)]


def _neg_softplus(z):
    nz = -z
    return jnp.minimum(nz, 0.0) - jnp.log(1.0 + jnp.exp(jnp.minimum(z, nz)))


def _stacked_col_minus_row():
    rowi = lax.broadcasted_iota(jnp.int32, (4 * SB_QB, SB_KB), 0)
    coli = lax.broadcasted_iota(jnp.int32, (4 * SB_QB, SB_KB), 1)
    return coli - (rowi & (SB_QB - 1))


def _sb_fwd(proj, after, *, name):
    b, s, _ = proj.shape
    nq = s // SB_QB
    ngrp = SB_WIDTH // GRP

    def body(q_ref, k_ref, v_ref, after_ref, o_ref, r_ref, acc_ref, car_ref):
        i = pl.program_id(2)
        masks = _head_masks4((SB_QB, GRP))
        row = lax.broadcasted_iota(jnp.int32, (SB_KB, SB_KB), 0)
        col = lax.broadcasted_iota(jnp.int32, (SB_KB, SB_KB), 1)
        later_mat = (row > col).astype(BF16)
        col_minus_row = _stacked_col_minus_row()
        qs = q_ref[0] * jnp.asarray(ATT_SCALE, BF16)
        q_stack = jnp.concatenate([jnp.where(mk, qs, jnp.zeros_like(qs)) for mk in masks], axis=0)
        acc_ref[...] = jnp.zeros_like(acc_ref)
        car_ref[...] = jnp.zeros_like(car_ref)

        def process(jbs, masked):
            offs = [pl.multiple_of(jb * SB_KB, SB_KB) for jb in jbs]
            k2s = [k_ref[0, pl.ds(off, SB_KB), :] for off in offs]
            v2s = [v_ref[0, pl.ds(off, SB_KB), :] for off in offs]
            zs = [_dot_nt(q_stack, k2) for k2 in k2s]
            lss = []
            for jb, z in zip(jbs, zs):
                ls = _neg_softplus(z)
                if masked:
                    causal = col_minus_row < (i * SB_QB - jb * SB_KB)
                    ls = jnp.where(causal, ls, 0.0)
                lss.append(ls)
            laters = [_dot(ls.astype(BF16), later_mat) for ls in lss]
            car = car_ref[...]
            ws = []
            for jb, z, ls, later in zip(jbs, zs, lss, laters):
                w = jnp.exp((z + ls) + later + car)
                if masked:
                    w = jnp.where(col_minus_row < (i * SB_QB - jb * SB_KB), w, 0.0)
                ws.append(w.astype(BF16))
                car = car + jnp.sum(ls, axis=1, keepdims=True)
            car_ref[...] = car
            acc_ref[...] += functools.reduce(jnp.add, [_dot(w, v2) for w, v2 in zip(ws, v2s)])

        top = (i * SB_QB) // SB_KB
        process([top], True)

        def step(jj, carry):
            process([top - 1 - 2 * jj, top - 2 - 2 * jj], False)
            return carry

        lax.fori_loop(0, top // 2, step, 0)

        @pl.when(top % 2 == 1)
        def _():
            process([0], False)

        o = acc_ref[pl.ds(0, SB_QB), :]
        r = car_ref[pl.ds(0, SB_QB), :]
        for h in range(1, 4):
            o = jnp.where(masks[h], acc_ref[pl.ds(h * SB_QB, SB_QB), :], o)
            r = jnp.where(masks[h], car_ref[pl.ds(h * SB_QB, SB_QB), :], r)
        o_ref[0] = o.astype(o_ref.dtype)
        r_ref[0] = r

    blk = pl.BlockSpec((1, SB_QB, GRP), lambda bb, p, i: (bb, i, p))
    return pl.pallas_call(
        body, name=name, grid=(b, ngrp, nq),
        in_specs=[blk,
                  pl.BlockSpec((1, s, GRP), lambda bb, p, i: (bb, 0, ngrp + p)),
                  pl.BlockSpec((1, s, GRP), lambda bb, p, i: (bb, 0, 2 * ngrp + p)),
                  pl.BlockSpec(memory_space=pl.ANY)],
        out_specs=[blk, blk],
        out_shape=[jax.ShapeDtypeStruct((b, s, SB_WIDTH), BF16), jax.ShapeDtypeStruct((b, s, SB_WIDTH), F32)],
        scratch_shapes=[pltpu.VMEM((4 * SB_QB, GRP), F32), pltpu.VMEM((4 * SB_QB, SB_KB), F32)],
        compiler_params=_cp("parallel", "parallel", "arbitrary"),
    )(proj, proj, proj, after)


def _sb_bwd(proj, dcat, rsum, after, *, name):
    b, s, _ = proj.shape
    nq = s // SB_QB
    ngrp = SB_WIDTH // GRP

    def body(q_ref, k_ref, v_ref, do_ref, r_ref, after_ref, dq_ref, dk_out, dv_out, dq_acc, cp_ref, cg_ref,
             dk_ref, dv_ref):
        i = pl.program_id(2)

        @pl.when(i == 0)
        def _():
            dk_ref[...] = jnp.zeros_like(dk_ref)
            dv_ref[...] = jnp.zeros_like(dv_ref)

        masks = _head_masks4((SB_QB, GRP))
        row = lax.broadcasted_iota(jnp.int32, (SB_KB, SB_KB), 0)
        col = lax.broadcasted_iota(jnp.int32, (SB_KB, SB_KB), 1)
        later_mat = (row > col).astype(BF16)
        excl_mat = (row < col).astype(BF16)
        col_minus_row = _stacked_col_minus_row()
        qs = q_ref[0] * jnp.asarray(ATT_SCALE, BF16)
        do = do_ref[0]
        q_stack = jnp.concatenate([jnp.where(mk, qs, jnp.zeros_like(qs)) for mk in masks], axis=0)
        do_stack = jnp.concatenate([jnp.where(mk, do, jnp.zeros_like(do)) for mk in masks], axis=0)
        rv = r_ref[0]
        r_stack = jnp.concatenate([rv[:, HEAD_DIM * h:HEAD_DIM * h + 1] for h in range(4)], axis=0)
        dq_acc[...] = jnp.zeros_like(dq_acc)
        cp_ref[...] = jnp.zeros_like(cp_ref)
        cg_ref[...] = jnp.zeros_like(cg_ref)

        def process(jbs, masked):
            offs = [pl.multiple_of(jb * SB_KB, SB_KB) for jb in jbs]
            k2s = [k_ref[0, pl.ds(off, SB_KB), :] for off in offs]
            v2s = [v_ref[0, pl.ds(off, SB_KB), :] for off in offs]
            zs = [_dot_nt(q_stack, k2) for k2 in k2s]
            dws = [_dot_nt(do_stack, v2) for v2 in v2s]
            lss, lsigs = [], []
            for jb, z in zip(jbs, zs):
                ls = _neg_softplus(z)
                lsigs.append(z + ls)
                if masked:
                    ls = jnp.where(col_minus_row < (i * SB_QB - jb * SB_KB), ls, 0.0)
                lss.append(ls)
            laters = [_dot(ls.astype(BF16), later_mat) for ls in lss]
            cpv = cp_ref[...]
            ws, gs = [], []
            for jb, ls, lsig, later, dw in zip(jbs, lss, lsigs, laters, dws):
                cpv = cpv + jnp.sum(ls, axis=1, keepdims=True)
                w = jnp.exp(lsig + ((r_stack - cpv) + later))
                if masked:
                    w = jnp.where(col_minus_row < (i * SB_QB - jb * SB_KB), w, 0.0)
                ws.append(w.astype(BF16))
                gs.append(dw * w)
            cp_ref[...] = cpv
            gpres = [_dot(g.astype(BF16), excl_mat) for g in gs]
            cgv = cg_ref[...]
            dzs = []
            for jb, g, lsig, gpre in zip(jbs, gs, lsigs, gpres):
                dz = g - jnp.exp(lsig) * (g + (gpre + cgv))
                if masked:
                    dz = jnp.where(col_minus_row < (i * SB_QB - jb * SB_KB), dz, 0.0)
                dzs.append(dz.astype(BF16))
                cgv = cgv + jnp.sum(g, axis=1, keepdims=True)
            cg_ref[...] = cgv
            dq_acc[...] += functools.reduce(jnp.add, [_dot(dzb, k2) for dzb, k2 in zip(dzs, k2s)])
            for off, dzb, wb in zip(offs, dzs, ws):
                dk_ref[0, pl.ds(off, SB_KB), :] += _dot_tn(dzb, q_stack)
                dv_ref[0, pl.ds(off, SB_KB), :] += _dot_tn(wb, do_stack)

        top = (i * SB_QB) // SB_KB

        def step(jj, carry):
            process([2 * jj, 2 * jj + 1], False)
            return carry

        lax.fori_loop(0, top // 2, step, 0)

        @pl.when(top % 2 == 1)
        def _():
            process([top - 1], False)

        process([top], True)
        dq = dq_acc[pl.ds(0, SB_QB), :]
        for h in range(1, 4):
            dq = jnp.where(masks[h], dq_acc[pl.ds(h * SB_QB, SB_QB), :], dq)
        dq_ref[0] = (dq * ATT_SCALE).astype(dq_ref.dtype)

        @pl.when(i == nq - 1)
        def _():
            dk_out[...] = dk_ref[...].astype(dk_out.dtype)
            dv_out[...] = dv_ref[...].astype(dv_out.dtype)

    blk = pl.BlockSpec((1, SB_QB, GRP), lambda bb, p, i: (bb, i, p))
    seq = pl.BlockSpec((1, s, GRP), lambda bb, p, i: (bb, 0, p))
    return pl.pallas_call(
        body, name=name, grid=(b, ngrp, nq),
        in_specs=[blk,
                  pl.BlockSpec((1, s, GRP), lambda bb, p, i: (bb, 0, ngrp + p)),
                  pl.BlockSpec((1, s, GRP), lambda bb, p, i: (bb, 0, 2 * ngrp + p)),
                  blk, blk, pl.BlockSpec(memory_space=pl.ANY)],
        out_specs=[blk, seq, seq],
        out_shape=[jax.ShapeDtypeStruct((b, s, SB_WIDTH), BF16)] * 3,
        scratch_shapes=[pltpu.VMEM((4 * SB_QB, GRP), F32), pltpu.VMEM((4 * SB_QB, SB_KB), F32),
                        pltpu.VMEM((4 * SB_QB, SB_KB), F32), pltpu.VMEM((1, s, GRP), F32),
                        pltpu.VMEM((1, s, GRP), F32)],
        compiler_params=_cp("parallel", "parallel", "arbitrary"),
    )(proj, proj, proj, dcat, rsum, after)


def _band_bias(slopes_scaled):
    a = lax.broadcasted_iota(jnp.int32, (QBLK, 2 * QBLK), 0)
    bcol = lax.broadcasted_iota(jnp.int32, (QBLK, 2 * QBLK), 1)
    delta = a + QBLK - bcol
    in_band = (delta >= 0) & (delta <= QBLK)
    dist = delta.astype(F32)
    bias = jnp.concatenate([(-sl) * dist for sl in slopes_scaled], axis=0)
    return jnp.concatenate([in_band] * 4, axis=0), jnp.concatenate([bcol >= QBLK] * 4, axis=0), bias


def _stack_heads(x, masks):
    return jnp.concatenate([jnp.where(mk, x, jnp.zeros_like(x)) for mk in masks], axis=0)


def _unstack_heads(x, masks):
    out = jnp.broadcast_to(x[0:QBLK], (QBLK, GRP))
    for h in range(1, 4):
        out = jnp.where(masks[h], x[h * QBLK:(h + 1) * QBLK], out)
    return out


PER_HEAD = 8


def _head_column(x):
    return jnp.concatenate([x[:, h:h + 1] for h in range(4)], axis=0)


def _head_lanes(col):
    lane = lax.broadcasted_iota(jnp.int32, (QBLK, PER_HEAD), 1)
    out = jnp.zeros((QBLK, PER_HEAD), F32)
    for h in range(4):
        out = jnp.where(lane == h, col[h * QBLK:(h + 1) * QBLK], out)
    return out


def _spread_heads(x8, rows):
    masks = _head_masks4((rows, GRP))
    out = jnp.broadcast_to(x8[:, 0:1], (rows, GRP))
    for h in range(1, 4):
        out = jnp.where(masks[h], x8[:, h:h + 1], out)
    return out


def _attn_units(n, l, banded):
    nsub = 4 if l % (4 * QBLK) == 0 else (2 if l % (2 * QBLK) == 0 else 1)
    nseq = 4 if (banded and nsub == 1 and n % 4 == 0) else 1
    return nseq, nsub


def _attn_specs(banded, nseq, nsub, q_lane_blk, k_lane_blk, v_lane_blk):
    tq = nsub * QBLK
    qs = pl.BlockSpec((nseq, tq, GRP), lambda n, i: (n, i, q_lane_blk))
    if banded:
        ks = [pl.BlockSpec((nseq, QBLK, GRP), lambda n, i: (n, jnp.maximum(nsub * i - 1, 0), k_lane_blk)),
              pl.BlockSpec((nseq, tq, GRP), lambda n, i: (n, i, k_lane_blk))]
        vs = [pl.BlockSpec((nseq, QBLK, GRP), lambda n, i: (n, jnp.maximum(nsub * i - 1, 0), v_lane_blk)),
              pl.BlockSpec((nseq, tq, GRP), lambda n, i: (n, i, v_lane_blk))]
    else:
        ks = [pl.BlockSpec((nseq, 2 * QBLK, GRP), lambda n, i: (n, 0, k_lane_blk))]
        vs = [pl.BlockSpec((nseq, 2 * QBLK, GRP), lambda n, i: (n, 0, v_lane_blk))]
    return qs, ks, vs


def _attn_fwd(q, k, v, *, name, banded, slopes_scaled=None, q_lane_blk=0, k_lane_blk=0, v_lane_blk=0):
    n, l, _ = q.shape
    nseq, nsub = _attn_units(n, l, banded)
    units = [(sq, u) for sq in range(nseq) for u in range(nsub)]
    tq = nsub * QBLK
    nkv = 2 if banded else 1

    def body(*refs):
        q_ref = refs[0]
        k_refs = refs[1:1 + nkv]
        v_refs = refs[1 + nkv:1 + 2 * nkv]
        o_ref, lse_ref = refs[1 + 2 * nkv:]
        step = pl.program_id(1)
        masks = _head_masks4((QBLK, GRP))
        if banded:
            in_band, is_cur, bias = _band_bias(slopes_scaled)
        scs, v2s = [], []
        for sq, u in units:
            qs = q_ref[sq, u * QBLK:(u + 1) * QBLK, :] * jnp.asarray(ATT_SCALE, BF16)
            if banded:
                kall = jnp.concatenate([k_refs[0][sq], k_refs[1][sq]], axis=0)
                vall = jnp.concatenate([v_refs[0][sq], v_refs[1][sq]], axis=0)
                k2 = kall[u * QBLK:(u + 2) * QBLK]
                v2s.append(vall[u * QBLK:(u + 2) * QBLK])
            else:
                k2 = k_refs[0][sq]
                v2s.append(v_refs[0][sq])
            scs.append(_dot_nt(_stack_heads(qs, masks), k2))
        ps, dens, lses = [], [], []
        for j, (sq, u) in enumerate(units):
            sc = scs[j]
            if banded:
                valid = in_band & (is_cur | (step * nsub + u > 0))
                sc = jnp.where(valid, sc + bias, NEG_BIG)
            m = jnp.max(sc, axis=-1, keepdims=True)
            p = jnp.exp(sc - m)
            den = jnp.sum(p, axis=-1, keepdims=True)
            ps.append(p.astype(BF16))
            dens.append(den)
            lses.append(m + jnp.log(den))
        ohs = [_dot(ps[j], v2s[j]) for j in range(len(units))]
        for j, (sq, u) in enumerate(units):
            o_ref[sq, u * QBLK:(u + 1) * QBLK, :] = _unstack_heads(ohs[j] / dens[j], masks).astype(o_ref.dtype)
            lse_ref[sq, u * QBLK:(u + 1) * QBLK, :] = _head_lanes(lses[j])

    qs, ks, vs = _attn_specs(banded, nseq, nsub, q_lane_blk, k_lane_blk, v_lane_blk)
    ob = pl.BlockSpec((nseq, tq, GRP), lambda nn, i: (nn, i, 0))
    return pl.pallas_call(
        body, name=name, grid=(n // nseq, l // tq),
        in_specs=[qs] + ks + vs, out_specs=[ob, pl.BlockSpec((nseq, tq, PER_HEAD), lambda nn, i: (nn, i, 0))],
        out_shape=[jax.ShapeDtypeStruct((n, l, GRP), BF16), jax.ShapeDtypeStruct((n, l, PER_HEAD), F32)],
        compiler_params=_cp("parallel", "arbitrary"),
    )(q, *([k] * nkv), *([v] * nkv))


def _attn_bwd(q, k, v, do, lse, delta, *, name, banded, slopes_scaled=None, q_lane_blk=0, k_lane_blk=0,
              v_lane_blk=0, do_lane_blk=0):
    n, l, _ = q.shape
    nseq, nsub = _attn_units(n, l, banded)
    units = [(sq, u) for sq in range(nseq) for u in range(nsub)]
    tq = nsub * QBLK
    nsteps = l // tq
    nkv = 2 if banded else 1
    lk = l if banded else 2 * QBLK

    def body(*refs):
        q_ref = refs[0]
        k_refs = refs[1:1 + nkv]
        v_refs = refs[1 + nkv:1 + 2 * nkv]
        do_ref, lse_ref, dl_ref, dq_ref, dk_out, dv_out, dk_ref, dv_ref = refs[1 + 2 * nkv:]
        step = pl.program_id(1)

        @pl.when(step == 0)
        def _():
            dk_ref[...] = jnp.zeros_like(dk_ref)
            dv_ref[...] = jnp.zeros_like(dv_ref)

        masks = _head_masks4((QBLK, GRP))
        if banded:
            in_band, is_cur, bias = _band_bias(slopes_scaled)
        q_st, do_st, k2s, scs, dps = [], [], [], [], []
        for j, (sq, u) in enumerate(units):
            rows = slice(u * QBLK, (u + 1) * QBLK)
            if banded:
                kall = jnp.concatenate([k_refs[0][sq], k_refs[1][sq]], axis=0)
                vall = jnp.concatenate([v_refs[0][sq], v_refs[1][sq]], axis=0)
                k2s.append(kall[u * QBLK:(u + 2) * QBLK])
                v2 = vall[u * QBLK:(u + 2) * QBLK]
            else:
                k2s.append(k_refs[0][sq])
                v2 = v_refs[0][sq]
            qs = q_ref[sq, rows, :] * jnp.asarray(ATT_SCALE, BF16)
            q_st.append(_stack_heads(qs, masks))
            do_st.append(_stack_heads(do_ref[sq, rows, :], masks))
            scs.append(_dot_nt(q_st[j], k2s[j]))
            dps.append(_dot_nt(do_st[j], v2))
        pbs, dss = [], []
        for j, (sq, u) in enumerate(units):
            rows = slice(u * QBLK, (u + 1) * QBLK)
            sc = scs[j]
            if banded:
                valid = in_band & (is_cur | (step * nsub + u > 0))
                sc = jnp.where(valid, sc + bias, NEG_BIG)
            p = jnp.exp(sc - _head_column(lse_ref[sq, rows, :]))
            pbs.append(p.astype(BF16))
            dss.append((p * (dps[j] - _head_column(dl_ref[sq, rows, :]))).astype(BF16))
        dqs = [_dot(dss[j], k2s[j]) for j in range(len(units))]
        dk2s = [_dot_tn(dss[j], q_st[j]) for j in range(len(units))]
        dv2s = [_dot_tn(pbs[j], do_st[j]) for j in range(len(units))]
        for j, (sq, u) in enumerate(units):
            dq_ref[sq, u * QBLK:(u + 1) * QBLK, :] = (_unstack_heads(dqs[j], masks) * ATT_SCALE).astype(dq_ref.dtype)
        if banded:
            for j, (sq, u) in enumerate(units):
                i = step * nsub + u
                cur = pl.multiple_of(i * QBLK, QBLK)
                dk_ref[sq, pl.ds(cur, QBLK), :] += dk2s[j][QBLK:]
                dv_ref[sq, pl.ds(cur, QBLK), :] += dv2s[j][QBLK:]

                @pl.when(i > 0)
                def _():
                    prev = pl.multiple_of((i - 1) * QBLK, QBLK)
                    dk_ref[sq, pl.ds(prev, QBLK), :] += dk2s[j][:QBLK]
                    dv_ref[sq, pl.ds(prev, QBLK), :] += dv2s[j][:QBLK]
        else:
            dk_ref[0] += functools.reduce(jnp.add, dk2s)
            dv_ref[0] += functools.reduce(jnp.add, dv2s)

        @pl.when(step == nsteps - 1)
        def _():
            dk_out[...] = dk_ref[...].astype(dk_out.dtype)
            dv_out[...] = dv_ref[...].astype(dv_out.dtype)

    qs, ks, vs = _attn_specs(banded, nseq, nsub, q_lane_blk, k_lane_blk, v_lane_blk)
    ob = pl.BlockSpec((nseq, tq, GRP), lambda nn, i: (nn, i, 0))
    stat = pl.BlockSpec((nseq, tq, PER_HEAD), lambda nn, i: (nn, i, 0))
    dos = pl.BlockSpec((nseq, tq, GRP), lambda nn, i: (nn, i, do_lane_blk))
    kvb = pl.BlockSpec((nseq, lk, GRP), lambda nn, i: (nn, 0, 0))
    return pl.pallas_call(
        body, name=name, grid=(n // nseq, nsteps),
        in_specs=[qs] + ks + vs + [dos, stat, stat], out_specs=[ob, kvb, kvb],
        out_shape=[jax.ShapeDtypeStruct((n, l, GRP), BF16), jax.ShapeDtypeStruct((n, lk, GRP), BF16),
                   jax.ShapeDtypeStruct((n, lk, GRP), BF16)],
        scratch_shapes=[pltpu.VMEM((nseq, lk, GRP), F32), pltpu.VMEM((nseq, lk, GRP), F32)],
        compiler_params=_cp("parallel", "arbitrary"),
    )(q, *([k] * nkv), *([v] * nkv), do, lse, delta)


def _attn_delta(do, o, *, name, lane_blks):
    t, _ = do.shape
    tr = _pick(t, (512, 256, 128, 8))
    ng = len(lane_blks)

    def body(*refs):
        do_refs, o_refs, d_ref = refs[:ng], refs[ng:2 * ng], refs[2 * ng]
        ra = lax.broadcasted_iota(jnp.int32, (GRP, LANE), 0) // HEAD_DIM
        rb = lax.broadcasted_iota(jnp.int32, (GRP, LANE), 1)
        head_sum = (ra == rb).astype(BF16)
        prod = None
        for a_ref, b_ref in zip(do_refs, o_refs):
            term = a_ref[...].astype(F32) * b_ref[...].astype(F32)
            prod = term if prod is None else prod + term
        d_ref[...] = _dot_split(prod, head_sum)[:, :PER_HEAD]

    specs = [pl.BlockSpec((tr, GRP), functools.partial(lambda i, lb: (i, lb), lb=lb)) for lb in lane_blks]
    return pl.pallas_call(
        body, name=name, grid=(t // tr,), in_specs=specs + specs,
        out_specs=pl.BlockSpec((tr, PER_HEAD), lambda i: (i, 0)),
        out_shape=jax.ShapeDtypeStruct((t, PER_HEAD), F32),
        compiler_params=_cp("parallel"),
    )(*([do] * ng), *([o] * ng))


def _dil_combine(os, lses, *, name):
    t, _ = os[0].shape
    tr = _pick(t, (512, 256, 128, 8))
    ng = len(os)

    def body(*refs):
        o_refs, l_refs = refs[:ng], refs[ng:2 * ng]
        out_ref, lse_ref = refs[2 * ng:]
        ls = [r[...] for r in l_refs]
        m = functools.reduce(jnp.maximum, ls)
        tot = None
        for lv in ls:
            e = jnp.exp(lv - m)
            tot = e if tot is None else tot + e
        lse = m + jnp.log(tot)
        for g in range(ng):
            alpha = _spread_heads(jnp.exp(ls[g] - lse), tr)
            out_ref[:, GRP * g:GRP * (g + 1)] = (o_refs[g][...].astype(F32) * alpha).astype(out_ref.dtype)
        lse_ref[...] = lse

    sp = pl.BlockSpec((tr, GRP), lambda i: (i, 0))
    st = pl.BlockSpec((tr, PER_HEAD), lambda i: (i, 0))
    return pl.pallas_call(
        body, name=name, grid=(t // tr,), in_specs=[sp] * ng + [st] * ng,
        out_specs=[pl.BlockSpec((tr, GRP * ng), lambda i: (i, 0)), st],
        out_shape=[jax.ShapeDtypeStruct((t, GRP * ng), BF16), jax.ShapeDtypeStruct((t, PER_HEAD), F32)],
        compiler_params=_cp("parallel"),
    )(*os, *lses)


FFN_LB = 256
FFN_ROWS = 128
HALO = 16


def _conv_chunk(u_ref, w, ci):
    r0 = pl.multiple_of(ci * FFN_ROWS, FFN_ROWS)
    cur = u_ref[0, pl.ds(r0, FFN_ROWS), :].astype(F32)
    p0 = pl.multiple_of(jnp.maximum(r0 - HALO, 0), HALO)
    prev = u_ref[0, pl.ds(p0, HALO), :].astype(F32)
    prev = jnp.where(ci > 0, prev, 0.0)
    rowi = lax.broadcasted_iota(jnp.int32, (8, cur.shape[1]), 0)
    r1 = pltpu.roll(cur, 1, 0)
    r2 = pltpu.roll(cur, 2, 0)
    s1 = jnp.concatenate([jnp.where(rowi == 0, prev[HALO - 1:HALO], r1[0:8]), r1[8:]], axis=0)
    s2 = jnp.concatenate([jnp.where(rowi == 0, prev[HALO - 2:HALO - 1],
                                    jnp.where(rowi == 1, prev[HALO - 1:HALO], r2[0:8])), r2[8:]], axis=0)
    c = w[0:1] * s2
    c = c + w[1:2] * s1
    c = c + w[2:3] * cur
    return c, cur, s1, s2


def _ffn_mid_fwd(u, wconv, *, name):
    b, s, f2 = u.shape
    f = f2 // 2
    nlb = f // FFN_LB

    def body(ua_ref, ug_ref, wa_ref, wg_ref, h_ref):
        wa = wa_ref[...]
        wg = wg_ref[...]

        def step(ci, carry):
            ca = _conv_chunk(ua_ref, wa, ci)[0]
            cg = _conv_chunk(ug_ref, wg, ci)[0]
            r0 = pl.multiple_of(ci * FFN_ROWS, FFN_ROWS)
            h_ref[0, pl.ds(r0, FFN_ROWS), :] = (cg * jax.nn.sigmoid(cg) * ca).astype(h_ref.dtype)
            return carry

        lax.fori_loop(0, s // FFN_ROWS, step, 0)

    return pl.pallas_call(
        body, name=name, grid=(nlb, b),
        in_specs=[pl.BlockSpec((1, s, FFN_LB), lambda l, bb: (bb, 0, l)),
                  pl.BlockSpec((1, s, FFN_LB), lambda l, bb: (bb, 0, nlb + l)),
                  pl.BlockSpec((3, FFN_LB), lambda l, bb: (0, l)),
                  pl.BlockSpec((3, FFN_LB), lambda l, bb: (0, nlb + l))],
        out_specs=pl.BlockSpec((1, s, FFN_LB), lambda l, bb: (bb, 0, l)),
        out_shape=jax.ShapeDtypeStruct((b, s, f), BF16),
        compiler_params=_cp("parallel", "parallel"),
    )(u, u, wconv, wconv)


def _ffn_mid_bwd(u, wconv, dh, *, name):
    b, s, f2 = u.shape
    f = f2 // 2
    nlb = f // FFN_LB
    nchunk = s // FFN_ROWS

    def body(ua_ref, ug_ref, wa_ref, wg_ref, dh_ref, dua_ref, dug_ref, dwa_ref, dwg_ref):
        bb = pl.program_id(1)
        wa = wa_ref[...]
        wg = wg_ref[...]
        rowi = lax.broadcasted_iota(jnp.int32, (8, FFN_LB), 0)
        last = FFN_ROWS - 8

        def conv_transpose(dc, nxt, w):
            r1 = pltpu.roll(dc, FFN_ROWS - 1, 0)
            r2 = pltpu.roll(dc, FFN_ROWS - 2, 0)
            n1 = jnp.concatenate([r1[:last], jnp.where(rowi == 7, nxt[0:1], r1[last:])], axis=0)
            n2 = jnp.concatenate([r2[:last], jnp.where(rowi == 6, nxt[0:1],
                                                       jnp.where(rowi == 7, nxt[1:2], r2[last:]))], axis=0)
            return w[2:3] * dc + w[1:2] * n1 + w[0:1] * n2

        def step(t, carry):
            ci = nchunk - 1 - t
            nxt_a, nxt_g = carry[0], carry[1]
            r0 = pl.multiple_of(ci * FFN_ROWS, FFN_ROWS)
            ca, cura, s1a, s2a = _conv_chunk(ua_ref, wa, ci)
            cg, curg, s1g, s2g = _conv_chunk(ug_ref, wg, ci)
            dhv = dh_ref[0, pl.ds(r0, FFN_ROWS), :].astype(F32)
            sg = jax.nn.sigmoid(cg)
            da = dhv * (cg * sg)
            dg = dhv * ca * (sg * (1.0 + cg * (1.0 - sg)))
            dua_ref[0, pl.ds(r0, FFN_ROWS), :] = conv_transpose(da, nxt_a, wa).astype(dua_ref.dtype)
            dug_ref[0, pl.ds(r0, FFN_ROWS), :] = conv_transpose(dg, nxt_g, wg).astype(dug_ref.dtype)
            red = lambda x: jnp.sum(x, axis=0, keepdims=True)
            parts = (red(da * s2a), red(da * s1a), red(da * cura), red(dg * s2g), red(dg * s1g), red(dg * curg))
            return (da[0:8], dg[0:8]) + tuple(c + p for c, p in zip(carry[2:], parts))

        zero = jnp.zeros((1, FFN_LB), F32)
        zero8 = jnp.zeros((8, FFN_LB), F32)
        taps = lax.fori_loop(0, nchunk, step, (zero8, zero8) + (zero,) * 6)[2:]

        @pl.when(bb == 0)
        def _():
            for k in range(3):
                dwa_ref[k:k + 1, :] = taps[k]
                dwg_ref[k:k + 1, :] = taps[3 + k]

        @pl.when(bb > 0)
        def _():
            for k in range(3):
                dwa_ref[k:k + 1, :] += taps[k]
                dwg_ref[k:k + 1, :] += taps[3 + k]

    seq_a = pl.BlockSpec((1, s, FFN_LB), lambda l, bb: (bb, 0, l))
    seq_g = pl.BlockSpec((1, s, FFN_LB), lambda l, bb: (bb, 0, nlb + l))
    wsp = pl.BlockSpec((3, FFN_LB), lambda l, bb: (0, l))
    return pl.pallas_call(
        body, name=name, grid=(nlb, b),
        in_specs=[seq_a, seq_g, wsp, pl.BlockSpec((3, FFN_LB), lambda l, bb: (0, nlb + l)), seq_a],
        out_specs=[seq_a, seq_a, wsp, wsp],
        out_shape=[jax.ShapeDtypeStruct((b, s, f), BF16), jax.ShapeDtypeStruct((b, s, f), BF16),
                   jax.ShapeDtypeStruct((3, f), F32), jax.ShapeDtypeStruct((3, f), F32)],
        compiler_params=_cp("parallel", "arbitrary"),
    )(u, u, wconv, wconv, dh)


def _adam_math(w, g, m, v):
    m2 = ADAM_B1 * m + (1.0 - ADAM_B1) * g
    v2 = ADAM_B2 * v + (1.0 - ADAM_B2) * (g * g)
    m_hat = m2 / (1.0 - ADAM_B1 ** ADAM_STEP)
    v_hat = v2 / (1.0 - ADAM_B2 ** ADAM_STEP)
    delta = -ADAM_LR * (m_hat / (jnp.sqrt(v_hat) + ADAM_EPS) + ADAM_WD * w)
    return delta, m2, v2


def _adam(w, g, m, v, *, name):
    r, c = w.shape
    tr = _pick(r, (256, 128, 88, 64, 32, 16, 8))

    def body(w_ref, g_ref, m_ref, v_ref, d_ref, m2_ref, v2_ref):
        d, m2, v2 = _adam_math(w_ref[...], g_ref[...], m_ref[...], v_ref[...])
        d_ref[...] = d
        m2_ref[...] = m2
        v2_ref[...] = v2

    sp = pl.BlockSpec((tr, c), lambda i: (i, 0))
    return pl.pallas_call(
        body, name=name, grid=(r // tr,), in_specs=[sp] * 4, out_specs=[sp] * 3,
        out_shape=[jax.ShapeDtypeStruct((r, c), F32)] * 3,
        compiler_params=_cp("parallel"),
    )(w, g, m, v)


def _adam_small(quads, *, name):
    nq = len(quads)

    def body(*refs):
        ins, outs = refs[:4 * nq], refs[4 * nq:]
        for k in range(nq):
            w_ref, g_ref, m_ref, v_ref = ins[4 * k:4 * k + 4]
            d, m2, v2 = _adam_math(w_ref[...], g_ref[...], m_ref[...], v_ref[...])
            outs[3 * k][...] = d
            outs[3 * k + 1][...] = m2
            outs[3 * k + 2][...] = v2

    flat = [a for q in quads for a in q]
    out_shape = [jax.ShapeDtypeStruct(q[0].shape, F32) for q in quads for _ in range(3)]
    vm = pl.BlockSpec(memory_space=pltpu.VMEM)
    outs = pl.pallas_call(
        body, name=name, in_specs=[vm] * len(flat), out_specs=[vm] * len(out_shape), out_shape=out_shape,
        compiler_params=pltpu.CompilerParams(vmem_limit_bytes=VMEM_LIMIT_BYTES),
    )(*flat)
    return [tuple(outs[3 * k:3 * k + 3]) for k in range(nq)]


def _mesh_pos():
    return lax.axis_index("x"), lax.axis_index("y"), lax.axis_index("c")


def _flip(v, bit):
    return 1 - v if bit else v


def _all_gather_hbm(xl, *, name):
    r, c = xl.shape

    def body(x_ref, out_ref, send_sems, recv_sems, local_sem):
        x, y, cc = _mesh_pos()
        me, sibling = (x, y, cc), (x, y, 1 - cc)
        chips = [(1 - x, y), (x, 1 - y), (1 - x, 1 - y)]

        def rows(px, py, pc):
            return out_ref.at[pl.ds((4 * px + 2 * py + pc) * r, r), :]

        def copy(k, block, to, src=None):
            return pltpu.make_async_remote_copy(
                src_ref=rows(*block) if src is None else src, dst_ref=rows(*block),
                send_sem=send_sems.at[k], recv_sem=recv_sems.at[k], device_id=to, device_id_type=MESH_ID)

        mine = pltpu.make_async_copy(x_ref, rows(*me), local_sem)
        mine.start()
        first = [copy(0, me, sibling, src=x_ref)]
        first += [copy(1 + j, me, (*chip, cc), src=x_ref) for j, chip in enumerate(chips)]
        for cp in first:
            cp.start()
        passed = [copy(4 + j, (*chip, cc), sibling) for j, chip in enumerate(chips)]
        for j, chip in enumerate(chips):
            copy(1 + j, (*chip, cc), me).wait_recv()
            passed[j].start()
        copy(0, sibling, me).wait_recv()
        for j, chip in enumerate(chips):
            copy(4 + j, (*chip, 1 - cc), me).wait_recv()
        for cp in first + passed:
            cp.wait_send()
        mine.wait()

    hbm = pl.BlockSpec(memory_space=pltpu.HBM)
    return pl.pallas_call(
        body, name=name, in_specs=[hbm], out_specs=hbm,
        out_shape=jax.ShapeDtypeStruct((N_DEV * r, c), xl.dtype),
        scratch_shapes=[pltpu.SemaphoreType.DMA((7,)), pltpu.SemaphoreType.DMA((7,)), pltpu.SemaphoreType.DMA],
    )(xl)


def _all_reduce_small(xl, *, name):
    r, c = xl.shape

    def body(x_ref, sum_ref, all_ref, send_sems, recv_sems, local_sem):
        x, y, cc = _mesh_pos()
        me, sibling = (x, y, cc), (x, y, 1 - cc)
        chips = [(1 - x, y), (x, 1 - y), (1 - x, 1 - y)]

        def rows(px, py, pc):
            return all_ref.at[pl.ds((4 * px + 2 * py + pc) * r, r), :]

        def copy(k, block, to, src=None):
            return pltpu.make_async_remote_copy(
                src_ref=rows(*block) if src is None else src, dst_ref=rows(*block),
                send_sem=send_sems.at[k], recv_sem=recv_sems.at[k], device_id=to, device_id_type=MESH_ID)

        mine = pltpu.make_async_copy(x_ref, rows(*me), local_sem)
        mine.start()
        first = [copy(0, me, sibling, src=x_ref)]
        first += [copy(1 + j, me, (*chip, cc), src=x_ref) for j, chip in enumerate(chips)]
        for cp in first:
            cp.start()
        passed = [copy(4 + j, (*chip, cc), sibling) for j, chip in enumerate(chips)]
        for j, chip in enumerate(chips):
            copy(1 + j, (*chip, cc), me).wait_recv()
            passed[j].start()
        copy(0, sibling, me).wait_recv()
        for j, chip in enumerate(chips):
            copy(4 + j, (*chip, 1 - cc), me).wait_recv()
        for cp in first + passed:
            cp.wait_send()
        mine.wait()
        tot = all_ref[pl.ds(0, r), :]
        for dd in range(1, N_DEV):
            tot = tot + all_ref[pl.ds(dd * r, r), :]
        sum_ref[...] = tot

    vm = pl.BlockSpec(memory_space=pltpu.VMEM)
    return pl.pallas_call(
        body, name=name, in_specs=[vm], out_specs=[vm, vm],
        out_shape=[jax.ShapeDtypeStruct((r, c), F32), jax.ShapeDtypeStruct((N_DEV * r, c), F32)],
        scratch_shapes=[pltpu.SemaphoreType.DMA((7,)), pltpu.SemaphoreType.DMA((7,)), pltpu.SemaphoreType.DMA],
    )(xl)[0]


N_PEERS = N_DEV - 1
_HBM = pl.BlockSpec(memory_space=pltpu.HBM)
_SEM = pl.BlockSpec(memory_space=pltpu.SEMAPHORE)


def _peer_list(x, y, cc):
    return [(_flip(x, rel & 4), _flip(y, rel & 2), _flip(cc, rel & 1)) for rel in range(1, N_DEV)]


def _dev_index(p):
    return 4 * p[0] + 2 * p[1] + p[2]


def _split_copy(src_ref, land_ref, sems, k, peer, me, gather, landing_of):
    if gather:
        r = src_ref.shape[0]
        src = src_ref
        dst = land_ref.at[pl.ds(_dev_index(landing_of) * r, r), :]
    else:
        src = src_ref.at[_dev_index(peer)]
        dst = land_ref.at[_dev_index(landing_of)]
    return pltpu.make_async_remote_copy(src_ref=src, dst_ref=dst, send_sem=sems[k], recv_sem=sems[N_PEERS + k],
                                        device_id=peer, device_id_type=MESH_ID)


def _exchange_start(src, land, *, name, gather):
    def body(src_ref, land_ref, *rest):
        sems = rest[:2 * N_PEERS]
        token = rest[2 * N_PEERS + 2]
        x, y, cc = _mesh_pos()
        me = (x, y, cc)
        for k, peer in enumerate(_peer_list(x, y, cc)):
            _split_copy(src_ref, land_ref, sems, k, peer, me, gather, landing_of=me).start()
        token[...] = jnp.zeros_like(token)

    outs = pl.pallas_call(
        body, name=name,
        out_shape=tuple([pltpu.SemaphoreType.DMA(())] * (2 * N_PEERS)) + (
            pltpu.HBM(src.shape, src.dtype), pltpu.HBM(land.shape, land.dtype),
            jax.ShapeDtypeStruct((8, LANE), F32)),
        in_specs=(_HBM, _HBM),
        out_specs=tuple([_SEM] * (2 * N_PEERS)) + (_HBM, _HBM, pl.BlockSpec(memory_space=pltpu.VMEM)),
        input_output_aliases={0: 2 * N_PEERS, 1: 2 * N_PEERS + 1},
        compiler_params=pltpu.CompilerParams(has_side_effects=pltpu.SideEffectType.DATAFLOW_SIDE_EFFECTING),
    )(pltpu.with_memory_space_constraint(src, pltpu.HBM), pltpu.with_memory_space_constraint(land, pltpu.HBM))
    return outs[:2 * N_PEERS], outs[2 * N_PEERS], outs[2 * N_PEERS + 1], outs[2 * N_PEERS + 2]


def _gather_start(local, me, *, name):
    rows, cols = local.shape
    land = lax.dynamic_update_slice(lax.empty((N_DEV * rows, cols), local.dtype), local, (me * rows, 0))
    return _exchange_start(local, land, name=name, gather=True)


def _exchange_wait(sems, src_thru, land_thru, after, *, name, gather):
    def body(src_ref, land_ref, *rest):
        sem_refs = rest[:2 * N_PEERS]
        x, y, cc = _mesh_pos()
        me = (x, y, cc)
        for k, peer in enumerate(_peer_list(x, y, cc)):
            cp = _split_copy(src_ref, land_ref, sem_refs, k, peer, me, gather, landing_of=peer)
            cp.wait_send()
            cp.wait_recv()

    outs = pl.pallas_call(
        body, name=name,
        out_shape=(pltpu.HBM(src_thru.shape, src_thru.dtype), pltpu.HBM(land_thru.shape, land_thru.dtype)),
        in_specs=(_HBM, _HBM) + tuple([_SEM] * (2 * N_PEERS)) + (pl.BlockSpec(memory_space=pl.ANY),),
        out_specs=(_HBM, _HBM), input_output_aliases={0: 0, 1: 1},
        compiler_params=pltpu.CompilerParams(has_side_effects=pltpu.SideEffectType.DATAFLOW_SIDE_EFFECTING),
    )(src_thru, land_thru, *sems, after)
    return outs[1]


def _sum_blocks(recv, own, after, *, name):
    nd, r, c = recv.shape
    tr = _pick(r, (448, 256, 128, 64, 32, 16))

    def body(x_ref, own_ref, after_ref, o_ref):
        x, y, cc = _mesh_pos()
        me = 4 * x + 2 * y + cc
        tot = None
        for dd in range(nd):
            term = jnp.where(me == dd, own_ref[0], x_ref[dd]).astype(F32)
            tot = term if tot is None else tot + term
        o_ref[...] = tot

    return pl.pallas_call(
        body, name=name, grid=(r // tr,),
        in_specs=[pl.BlockSpec((nd, tr, c), lambda i: (0, i, 0)), pl.BlockSpec((1, tr, c), lambda i: (0, i, 0)),
                  pl.BlockSpec(memory_space=pl.ANY)],
        out_specs=pl.BlockSpec((tr, c), lambda i: (i, 0)),
        out_shape=jax.ShapeDtypeStruct((r, c), F32),
        compiler_params=_cp("parallel"),
    )(recv, own, after)


SHARD_KIND = {"a_w_in": "col", "a_w_out": "row", "a_w_mem_kv": "row", "a_ffn_up": "col", "a_ffn_down": "row",
              "w_kv_shared": "col", "b_w_in": "row", "b_w_out": "row", "b_w_mem_kv": "row", "b_ffn_up": "col",
              "b_ffn_down": "row"}
EARLY_WEIGHTS = ("a_w_in", "a_w_mem_kv")
FFN_UP_WEIGHTS = ("a_ffn_up", "b_ffn_up")
WIDE_WEIGHTS = tuple(nm for nm in SHARD_KIND if nm not in EARLY_WEIGHTS + FFN_UP_WEIGHTS)
LATE_WEIGHTS = WIDE_WEIGHTS + FFN_UP_WEIGHTS


def _as2d(a):
    return a.reshape(a.shape[-2], a.shape[-1]) if a.ndim >= 2 else a.reshape(1, a.shape[0])


def _pack_local(shards):
    return jnp.concatenate([_as2d(s).astype(BF16).reshape(-1, PACK_COLS) for s in shards], axis=0)


def _unpack_full(gathered, names, shapes):
    out = {}
    r0 = 0
    for name in names:
        rows, cols = shapes[name]
        nr = rows * cols // PACK_COLS
        blk = gathered[:, r0:r0 + nr, :].reshape(N_DEV, rows, cols)
        if SHARD_KIND[name] == "row":
            out[name] = blk.reshape(N_DEV * rows, cols)
        else:
            out[name] = blk.transpose(1, 0, 2).reshape(rows, N_DEV * cols)
        r0 += nr
    return out


def _pack_grads(grads, names, shapes):
    parts = []
    for name in names:
        rows, cols = shapes[name]
        g = grads[name]
        if SHARD_KIND[name] == "row":
            blk = g.reshape(N_DEV, rows, cols)
        else:
            blk = g.reshape(rows, N_DEV, cols).transpose(1, 0, 2)
        parts.append(blk.astype(BF16).reshape(N_DEV, rows * cols // PACK_COLS, PACK_COLS))
    return jnp.concatenate(parts, axis=1)


def _unpack_local(gsum, names, shapes):
    out = {}
    r0 = 0
    for name in names:
        rows, cols = shapes[name]
        nr = rows * cols // PACK_COLS
        out[name] = gsum[r0:r0 + nr].reshape(rows, cols)
        r0 += nr
    return out


def _by_residue(t, d):
    if d == 1:
        return t
    b, s, c = t.shape
    return t.reshape(b, s // d, d, c).transpose(0, 2, 1, 3).reshape(b * d, s // d, c)


def _from_residue(t, d, b):
    if d == 1:
        return t
    n, l, c = t.shape
    return t.reshape(b, d, l, c).transpose(0, 2, 1, 3).reshape(b, l * d, c)


def _alibi_slopes():
    return [2.0 ** (-ALIBI_MAX_BIAS * (i + 1) / N_DIL_HEADS) for i in range(N_DIL_HEADS)]


def _conv_ffn_fwd(xin, gain, w_up, wconv, w_down, tag, b, s):
    (n,), r = _rms_fwd(xin, [gain], name=f"{tag}_rms_ffn")
    u = _mm(n, w_up, name=f"{tag}_up", out_dtype=BF16).reshape(b, s, -1)
    hmid = _ffn_mid_fwd(u, wconv, name=f"{tag}_ffn_mid").reshape(b * s, -1)
    xout = _mm(hmid, w_down, name=f"{tag}_down", out_dtype=F32, res=xin)
    return xout, (n, r, u, hmid)


def _conv_ffn_bwd(dxout, xin, gain, saved, w_up, wconv, w_down, tag, b, s):
    n, r, u, hmid = saved
    f = hmid.shape[1]
    dhmid = _mm(dxout, w_down, name=f"{tag}_d_hmid", out_dtype=BF16, trans_b=True)
    g_down = _mm(hmid, dxout, name=f"{tag}_g_down", out_dtype=BF16, trans_a=True)
    du_a, du_g, gc_a, gc_g = _ffn_mid_bwd(u, wconv, dhmid.reshape(b, s, f), name=f"{tag}_ffn_mid_bwd")
    du_a = du_a.reshape(b * s, f)
    du_g = du_g.reshape(b * s, f)
    dn = _mm(du_a, w_up[:, :f], name=f"{tag}_d_n_a", out_dtype=F32, trans_b=True)
    dn = _mm(du_g, w_up[:, f:], name=f"{tag}_d_n_g", out_dtype=F32, res=dn, trans_b=True)
    g_up = jnp.concatenate([_mm(n, du_a, name=f"{tag}_g_up_a", out_dtype=BF16, trans_a=True),
                            _mm(n, du_g, name=f"{tag}_g_up_g", out_dtype=BF16, trans_a=True)], axis=1)
    dxin, (g_gain,) = _rms_bwd(xin, r, [(dn, gain)], dxout, name=f"{tag}_rms_ffn_bwd")
    return dxin, g_up, g_down, jnp.concatenate([gc_a, gc_g], axis=1), g_gain


def _mem_kv_fwd(mem2d, gain, w_mem_kv, tag, b):
    (nm,), rm = _rms_fwd(mem2d, [gain], name=f"{tag}_rms_mem")
    kvm = _mm(nm, w_mem_kv, name=f"{tag}_mem_kv", out_dtype=BF16)
    return kvm.reshape(b, -1, 2 * MEM_WIDTH), (nm, rm)


def _mem_kv_bwd(dk, dv, mem2d, gain, saved, w_mem_kv, tag):
    nm, rm = saved
    dkvm = jnp.concatenate([dk, dv], axis=-1).reshape(-1, 2 * MEM_WIDTH)
    dnm = _mm(dkvm, w_mem_kv, name=f"{tag}_d_nm", out_dtype=F32, trans_b=True)
    g_w = _mm(nm, dkvm, name=f"{tag}_g_mem_kv", out_dtype=BF16, trans_a=True)
    _, (g_gain,) = _rms_bwd(mem2d, rm, [(dnm, gain)], None, name=f"{tag}_rms_mem_bwd", need_dx=False)
    return g_w, g_gain


def kernel(x, mem, a_norm_attn, a_w_in, a_w_out, a_norm_mem, a_w_mem_kv, a_norm_ffn, a_ffn_up, a_ffn_conv, a_ffn_down, kv_norm, w_kv_shared, b_norm_attn, b_w_in, b_w_out, b_norm_mem, b_w_mem_kv, b_norm_ffn, b_ffn_up, b_ffn_conv, b_ffn_down, final_norm, loss_target, m_a_norm_attn, m_a_w_in, m_a_w_out, m_a_norm_mem, m_a_w_mem_kv, m_a_norm_ffn, m_a_ffn_up, m_a_ffn_conv, m_a_ffn_down, m_kv_norm, m_w_kv_shared, m_b_norm_attn, m_b_w_in, m_b_w_out, m_b_norm_mem, m_b_w_mem_kv, m_b_norm_ffn, m_b_ffn_up, m_b_ffn_conv, m_b_ffn_down, m_final_norm, v_a_norm_attn, v_a_w_in, v_a_w_out, v_a_norm_mem, v_a_w_mem_kv, v_a_norm_ffn, v_a_ffn_up, v_a_ffn_conv, v_a_ffn_down, v_kv_norm, v_w_kv_shared, v_b_norm_attn, v_b_w_in, v_b_w_out, v_b_norm_mem, v_b_w_mem_kv, v_b_norm_ffn, v_b_ffn_up, v_b_ffn_conv, v_b_ffn_down, v_final_norm):
    names = ["a_norm_attn", "a_w_in", "a_w_out", "a_norm_mem", "a_w_mem_kv", "a_norm_ffn", "a_ffn_up",
             "a_ffn_conv", "a_ffn_down", "kv_norm", "w_kv_shared", "b_norm_attn", "b_w_in", "b_w_out",
             "b_norm_mem", "b_w_mem_kv", "b_norm_ffn", "b_ffn_up", "b_ffn_conv", "b_ffn_down", "final_norm"]
    wl = dict(zip(names, [a_norm_attn, a_w_in, a_w_out, a_norm_mem, a_w_mem_kv, a_norm_ffn, a_ffn_up,
                          a_ffn_conv, a_ffn_down, kv_norm, w_kv_shared, b_norm_attn, b_w_in, b_w_out,
                          b_norm_mem, b_w_mem_kv, b_norm_ffn, b_ffn_up, b_ffn_conv, b_ffn_down, final_norm]))
    ml = dict(zip(names, [m_a_norm_attn, m_a_w_in, m_a_w_out, m_a_norm_mem, m_a_w_mem_kv, m_a_norm_ffn,
                          m_a_ffn_up, m_a_ffn_conv, m_a_ffn_down, m_kv_norm, m_w_kv_shared, m_b_norm_attn,
                          m_b_w_in, m_b_w_out, m_b_norm_mem, m_b_w_mem_kv, m_b_norm_ffn, m_b_ffn_up,
                          m_b_ffn_conv, m_b_ffn_down, m_final_norm]))
    vl = dict(zip(names, [v_a_norm_attn, v_a_w_in, v_a_w_out, v_a_norm_mem, v_a_w_mem_kv, v_a_norm_ffn,
                          v_a_ffn_up, v_a_ffn_conv, v_a_ffn_down, v_kv_norm, v_w_kv_shared, v_b_norm_attn,
                          v_b_w_in, v_b_w_out, v_b_norm_mem, v_b_w_mem_kv, v_b_norm_ffn, v_b_ffn_up,
                          v_b_ffn_conv, v_b_ffn_down, v_final_norm]))
    b, s, d = x.shape
    t = b * s
    my_x, my_y, my_c = _mesh_pos()
    me = 4 * my_x + 2 * my_y + my_c

    shapes = {nm: _as2d(wl[nm]).shape for nm in SHARD_KIND}
    early_local = _pack_local([wl[nm] for nm in EARLY_WEIGHTS])
    early_all = _all_gather_hbm(early_local, name="gather_early").reshape(N_DEV, early_local.shape[0], PACK_COLS)
    wf = _unpack_full(early_all, EARLY_WEIGHTS, shapes)
    wide_local = _pack_local([wl[nm] for nm in WIDE_WEIGHTS])
    up_local = jnp.concatenate([_as2d(wl[nm]).astype(BF16) for nm in FFN_UP_WEIGHTS], axis=0)
    gw_sems, gw_src, gw_land, gw_token = _gather_start(wide_local, me, name="gather_wide_start")
    gu_sems, gu_src, gu_land, gu_token = _gather_start(up_local, me, name="gather_up_start")

    sharded_small = ["a_norm_attn", "a_norm_mem", "a_norm_ffn", "a_ffn_conv", "b_ffn_conv"]
    small_flat = jnp.concatenate([wl[nm].reshape(-1) for nm in sharded_small])
    n_small = small_flat.shape[0]
    small_rows = -(-n_small // (8 * LANE)) * 8
    small_local = jnp.pad(small_flat, (0, small_rows * LANE - n_small)).reshape(small_rows, LANE)
    small_all = _all_gather_hbm(small_local, name="gather_small").reshape(N_DEV, small_rows * LANE)
    sfull = {}
    r0 = 0
    for nm in sharded_small:
        rows, cols = _as2d(wl[nm]).shape
        blk = small_all[:, r0:r0 + rows * cols].reshape(N_DEV, rows, cols)
        sfull[nm] = blk.transpose(1, 0, 2).reshape(rows, N_DEV * cols)
        r0 += rows * cols
    gain = {nm: sfull[nm] for nm in ("a_norm_attn", "a_norm_mem", "a_norm_ffn")}
    for nm in ("kv_norm", "b_norm_attn", "b_norm_mem", "b_norm_ffn", "final_norm"):
        gain[nm] = _as2d(wl[nm])
    conv_a, conv_b = sfull["a_ffn_conv"], sfull["b_ffn_conv"]

    x2d = x.reshape(t, d)
    mem2d = mem.reshape(-1, d)
    tgt2d = loss_target.reshape(t, d)
    qmem_blk_a = 3 * SB_WIDTH // GRP
    qmem_blk_b = DIL_WIDTH // GRP

    (n1,), r1 = _rms_fwd(x2d, [gain["a_norm_attn"]], name="a_rms_attn")
    proj_a = _mm(n1, wf["a_w_in"], name="a_in", out_dtype=BF16).reshape(b, s, -1)
    kvm_a, mem_saved_a = _mem_kv_fwd(mem2d, gain["a_norm_mem"], wf["a_w_mem_kv"], "a", b)
    o_sb, rsum = _sb_fwd(proj_a, gw_token + gu_token, name="a_sb_fwd")
    o_mem_a, lse_mem_a = _attn_fwd(proj_a, kvm_a, kvm_a, name="a_mem_fwd", banded=False,
                                   q_lane_blk=qmem_blk_a, k_lane_blk=0, v_lane_blk=1)
    wide_all = _exchange_wait(gw_sems, gw_src, gw_land, rsum, name="gather_wide_wait", gather=True)
    up_all = _exchange_wait(gu_sems, gu_src, gu_land, rsum, name="gather_up_wait", gather=True)
    wf.update(_unpack_full(wide_all.reshape(N_DEV, wide_local.shape[0], PACK_COLS), WIDE_WEIGHTS, shapes))
    up_rows, up_cols = shapes[FFN_UP_WEIGHTS[0]]
    up_all = up_all.reshape(N_DEV, len(FFN_UP_WEIGHTS), up_rows, up_cols)
    for k, nm in enumerate(FFN_UP_WEIGHTS):
        wf[nm] = up_all[:, k].transpose(1, 0, 2).reshape(up_rows, N_DEV * up_cols)
    cat_a = jnp.concatenate([o_sb, o_mem_a], axis=-1).reshape(t, d)
    x1 = _mm(cat_a, wf["a_w_out"], name="a_out", out_dtype=F32, res=x2d)
    xa, ffn_saved_a = _conv_ffn_fwd(x1, gain["a_norm_ffn"], wf["a_ffn_up"], conv_a, wf["a_ffn_down"], "a", b, s)

    (nk, n3), r3 = _rms_fwd(xa, [gain["kv_norm"], gain["b_norm_attn"]], name="b_rms_attn")
    kvsh = _mm(nk, wf["w_kv_shared"], name="kv_shared", out_dtype=BF16).reshape(b, s, -1)
    proj_b = _mm(n3, wf["b_w_in"], name="b_in", out_dtype=BF16).reshape(b, s, -1)
    kvm_b, mem_saved_b = _mem_kv_fwd(mem2d, gain["b_norm_mem"], wf["b_w_mem_kv"], "b", b)
    slopes = _alibi_slopes()
    dil_q, dil_k, dil_v, dil_o, dil_lse, dil_slopes = [], [], [], [], [], []
    for g, (_, dil) in enumerate(DIL_GROUPS):
        qg = _by_residue(proj_b[:, :, GRP * g:GRP * (g + 1)], dil)
        kg = _by_residue(kvsh[:, :, GRP * g:GRP * (g + 1)], dil)
        vg = _by_residue(kvsh[:, :, DIL_WIDTH + GRP * g:DIL_WIDTH + GRP * (g + 1)], dil)
        sl = [slopes[4 * g + h] * dil for h in range(4)]
        og, lg = _attn_fwd(qg, kg, vg, name=f"b_dil{g}_fwd", banded=True, slopes_scaled=sl)
        dil_q.append(qg)
        dil_k.append(kg)
        dil_v.append(vg)
        dil_slopes.append(sl)
        dil_o.append(_from_residue(og, dil, b).reshape(t, GRP))
        dil_lse.append(_from_residue(lg, dil, b).reshape(t, PER_HEAD))
    o_dil, lse_joint = _dil_combine(dil_o, dil_lse, name="b_dil_combine")
    o_mem_b, lse_mem_b = _attn_fwd(proj_b, kvm_b, kvm_b, name="b_mem_fwd", banded=False,
                                   q_lane_blk=qmem_blk_b, k_lane_blk=0, v_lane_blk=1)
    cat_b = jnp.concatenate([o_dil, o_mem_b.reshape(t, MEM_WIDTH)], axis=-1)
    x3 = _mm(cat_b, wf["b_w_out"], name="b_out", out_dtype=F32, res=xa)
    xb, ffn_saved_b = _conv_ffn_fwd(x3, gain["b_norm_ffn"], wf["b_ffn_up"], conv_b, wf["b_ffn_down"], "b", b, s)

    dxb, g_final, loss_vec = _loss_head(xb, gain["final_norm"], tgt2d, name="loss_head")

    grads = {}
    sgrads = {"final_norm": g_final}
    dx3, grads["b_ffn_up"], grads["b_ffn_down"], sgrads["b_ffn_conv"], sgrads["b_norm_ffn"] = _conv_ffn_bwd(
        dxb, x3, gain["b_norm_ffn"], ffn_saved_b, wf["b_ffn_up"], conv_b, wf["b_ffn_down"], "b", b, s)
    dcat_b = _mm(dx3, wf["b_w_out"], name="b_d_cat", out_dtype=BF16, trans_b=True)
    grads["b_w_out"] = _mm(cat_b, dx3, name="b_g_out", out_dtype=BF16, trans_a=True)
    dcat_b3 = dcat_b.reshape(b, s, d)
    delta_mem_b = _attn_delta(dcat_b, cat_b, name="b_mem_delta", lane_blks=[qmem_blk_b]).reshape(b, s, PER_HEAD)
    dq_mem_b, dkm_b, dvm_b = _attn_bwd(proj_b, kvm_b, kvm_b, dcat_b3, lse_mem_b, delta_mem_b, name="b_mem_bwd",
                                       banded=False, q_lane_blk=qmem_blk_b, k_lane_blk=0, v_lane_blk=1,
                                       do_lane_blk=qmem_blk_b)
    delta_dil = _attn_delta(dcat_b, cat_b, name="b_dil_delta", lane_blks=[0, 1, 2]).reshape(b, s, PER_HEAD)
    lse_joint3 = lse_joint.reshape(b, s, PER_HEAD)
    dq_parts, dk_parts, dv_parts = [], [], []
    for g, (_, dil) in enumerate(DIL_GROUPS):
        dog = _by_residue(dcat_b3[:, :, GRP * g:GRP * (g + 1)], dil)
        lg = _by_residue(lse_joint3, dil)
        dg = _by_residue(delta_dil, dil)
        dqg, dkg, dvg = _attn_bwd(dil_q[g], dil_k[g], dil_v[g], dog, lg, dg, name=f"b_dil{g}_bwd", banded=True,
                                  slopes_scaled=dil_slopes[g])
        dq_parts.append(_from_residue(dqg, dil, b))
        dk_parts.append(_from_residue(dkg, dil, b))
        dv_parts.append(_from_residue(dvg, dil, b))
    dproj_b = jnp.concatenate(dq_parts + [dq_mem_b], axis=-1).reshape(t, d)
    dn3 = _mm(dproj_b, wf["b_w_in"], name="b_d_n", out_dtype=F32, trans_b=True)
    grads["b_w_in"] = _mm(n3, dproj_b, name="b_g_in", out_dtype=BF16, trans_a=True)
    grads["b_w_mem_kv"], sgrads["b_norm_mem"] = _mem_kv_bwd(dkm_b, dvm_b, mem2d, gain["b_norm_mem"], mem_saved_b,
                                                           wf["b_w_mem_kv"], "b")
    dkvsh = jnp.concatenate(dk_parts + dv_parts, axis=-1).reshape(t, 2 * DIL_WIDTH).astype(BF16)
    dnk = _mm(dkvsh, wf["w_kv_shared"], name="kv_d_n", out_dtype=F32, trans_b=True)
    grads["w_kv_shared"] = _mm(nk, dkvsh, name="kv_g", out_dtype=BF16, trans_a=True)
    dxa, (sgrads["kv_norm"], sgrads["b_norm_attn"]) = _rms_bwd(
        xa, r3, [(dnk, gain["kv_norm"]), (dn3, gain["b_norm_attn"])], dx3, name="b_rms_attn_bwd")

    dx1, grads["a_ffn_up"], grads["a_ffn_down"], sgrads["a_ffn_conv"], sgrads["a_norm_ffn"] = _conv_ffn_bwd(
        dxa, x1, gain["a_norm_ffn"], ffn_saved_a, wf["a_ffn_up"], conv_a, wf["a_ffn_down"], "a", b, s)
    dcat_a = _mm(dx1, wf["a_w_out"], name="a_d_cat", out_dtype=BF16, trans_b=True)
    grads["a_w_out"] = _mm(cat_a, dx1, name="a_g_out", out_dtype=BF16, trans_a=True)
    dcat_a3 = dcat_a.reshape(b, s, d)
    delta_mem_a = _attn_delta(dcat_a, cat_a, name="a_mem_delta", lane_blks=[qmem_blk_b]).reshape(b, s, PER_HEAD)
    dq_mem_a, dkm_a, dvm_a = _attn_bwd(proj_a, kvm_a, kvm_a, dcat_a3, lse_mem_a, delta_mem_a, name="a_mem_bwd",
                                       banded=False, q_lane_blk=qmem_blk_a, k_lane_blk=0, v_lane_blk=1,
                                       do_lane_blk=qmem_blk_b)
    wide_grads = _pack_grads(grads, WIDE_WEIGHTS, shapes)
    up_grads = jnp.concatenate([grads[nm].reshape(up_rows, N_DEV, up_cols).transpose(1, 0, 2)
                                for nm in FFN_UP_WEIGHTS], axis=1)
    own_wide = lax.dynamic_slice(wide_grads, (me, 0, 0), (1,) + wide_grads.shape[1:])
    own_up = lax.dynamic_slice(up_grads, (me, 0, 0), (1,) + up_grads.shape[1:])
    xw_sems, xw_src, xw_land, xw_token = _exchange_start(wide_grads, lax.empty(wide_grads.shape, BF16),
                                                         name="grads_wide_start", gather=False)
    xu_sems, xu_src, xu_land, xu_token = _exchange_start(up_grads, lax.empty(up_grads.shape, BF16),
                                                         name="grads_up_start", gather=False)
    dq_sb, dk_sb, dv_sb = _sb_bwd(proj_a, dcat_a3, rsum, xw_token + xu_token, name="a_sb_bwd")
    dproj_a = jnp.concatenate([dq_sb, dk_sb, dv_sb, dq_mem_a], axis=-1).reshape(t, -1)
    dn1 = _mm(dproj_a, wf["a_w_in"], name="a_d_n", out_dtype=F32, trans_b=True)
    grads["a_w_in"] = _mm(n1, dproj_a, name="a_g_in", out_dtype=BF16, trans_a=True)
    grads["a_w_mem_kv"], sgrads["a_norm_mem"] = _mem_kv_bwd(dkm_a, dvm_a, mem2d, gain["a_norm_mem"], mem_saved_a,
                                                           wf["a_w_mem_kv"], "a")
    early_grads = _pack_grads(grads, EARLY_WEIGHTS, shapes)
    own_early = lax.dynamic_slice(early_grads, (me, 0, 0), (1,) + early_grads.shape[1:])
    ee_sems, ee_src, ee_land, ee_token = _exchange_start(early_grads, lax.empty(early_grads.shape, BF16),
                                                         name="grads_early_start", gather=False)
    dx0, (sgrads["a_norm_attn"],) = _rms_bwd(x2d, r1, [(dn1, gain["a_norm_attn"])], dx1, name="a_rms_attn_bwd")
    grad_x = dx0.reshape(b, s, d)

    wide_recv = _exchange_wait(xw_sems, xw_src, xw_land, dx0, name="grads_wide_wait", gather=False)
    up_recv = _exchange_wait(xu_sems, xu_src, xu_land, dx0, name="grads_up_wait", gather=False)
    gl = _unpack_local(_sum_blocks(wide_recv, own_wide, ee_token, name="sum_grads_wide"), WIDE_WEIGHTS, shapes)
    up_sum = _sum_blocks(up_recv, own_up, ee_token, name="sum_grads_up")
    for k, nm in enumerate(FFN_UP_WEIGHTS):
        gl[nm] = up_sum[k * up_rows:(k + 1) * up_rows]

    small_names = ["a_norm_attn", "a_norm_mem", "a_norm_ffn", "kv_norm", "b_norm_attn", "b_norm_mem",
                   "b_norm_ffn", "final_norm", "a_ffn_conv", "b_ffn_conv"]
    small_flat = jnp.concatenate([sgrads[nm].reshape(-1) for nm in small_names] + [loss_vec.reshape(-1)])
    n_flat = small_flat.shape[0]
    red_rows = -(-n_flat // (8 * PACK_COLS)) * 8
    small_pack = jnp.pad(small_flat, (0, red_rows * PACK_COLS - n_flat)).reshape(red_rows, PACK_COLS)
    small_sum = _all_reduce_small(small_pack, name="reduce_small").reshape(-1)
    r0 = 0
    for nm in small_names:
        rows, cols = sgrads[nm].shape
        full = small_sum[r0:r0 + rows * cols].reshape(rows, cols)
        r0 += rows * cols
        if nm in sharded_small:
            lc = cols // N_DEV
            gl[nm] = lax.dynamic_slice(full, (0, me * lc), (rows, lc))
        else:
            gl[nm] = full
    loss = (0.5 / d) * jnp.sum(small_sum[r0:r0 + d])

    upd = {}
    for nm in LATE_WEIGHTS:
        upd[nm] = _adam(_as2d(wl[nm]), gl[nm], _as2d(ml[nm]), _as2d(vl[nm]), name=f"adam_{nm}")
    res_small = _adam_small([(_as2d(wl[nm]), gl[nm], _as2d(ml[nm]), _as2d(vl[nm])) for nm in small_names],
                            name="adam_small")
    for nm, r in zip(small_names, res_small):
        upd[nm] = r
    early_recv = _exchange_wait(ee_sems, ee_src, ee_land, upd[LATE_WEIGHTS[-1]][0], name="grads_early_wait",
                                gather=False)
    gl.update(_unpack_local(_sum_blocks(early_recv, own_early, ee_token, name="sum_grads_early"), EARLY_WEIGHTS,
                            shapes))
    for nm in EARLY_WEIGHTS:
        upd[nm] = _adam(_as2d(wl[nm]), gl[nm], _as2d(ml[nm]), _as2d(vl[nm]), name=f"adam_{nm}")

    g_out = [gl[nm].reshape(wl[nm].shape) for nm in names]
    d_out = [upd[nm][0].reshape(wl[nm].shape) for nm in names]
    m_out = [upd[nm][1].reshape(wl[nm].shape) for nm in names]
    v_out = [upd[nm][2].reshape(wl[nm].shape) for nm in names]
    return (loss, grad_x, *g_out, *d_out, *m_out, *v_out)
```

```python
import functools

import jax
import jax.numpy as jnp
from jax import lax
from jax.experimental import pallas as pl
from jax.experimental.pallas import tpu as pltpu

F32 = jnp.float32
BF16 = jnp.bfloat16

N_DEV = 8
HEAD_DIM = 64
N_SB_HEADS = 12
N_DIL_HEADS = 12
DIL_GROUPS = ((128, 1), (512, 4), (2048, 16))
SB_WIDTH = N_SB_HEADS * HEAD_DIM
MEM_WIDTH = 256
DIL_WIDTH = N_DIL_HEADS * HEAD_DIM
ATT_SCALE = HEAD_DIM ** -0.5
EPS = 1e-6
ALIBI_MAX_BIAS = 8.0
NEG_BIG = -1e30

ADAM_LR = 0.001
ADAM_B1 = 0.9
ADAM_B2 = 0.999
ADAM_EPS = 1e-08
ADAM_WD = 0.01
ADAM_STEP = 10

LANE = 128
QBLK = 128
VMEM_LIMIT_BYTES = 48 * 1024 * 1024
PACK_COLS = 1024
MESH_ID = pl.DeviceIdType.MESH


def _cp(*sem):
    return pltpu.CompilerParams(dimension_semantics=sem, vmem_limit_bytes=VMEM_LIMIT_BYTES)


def _pick(n, cands):
    for c in cands:
        if n % c == 0:
            return c
    raise ValueError(f"no tile for {n} in {cands}")


def _dot(a, b):
    return jnp.dot(a, b, preferred_element_type=F32)


def _dot_nt(a, b):
    return lax.dot_general(a, b, (((1,), (1,)), ((), ())), preferred_element_type=F32)


def _dot_tn(a, b):
    return lax.dot_general(a, b, (((0,), (0,)), ((), ())), preferred_element_type=F32)


def _dot_split(x, u):
    hi = x.astype(BF16)
    lo = (x - hi.astype(F32)).astype(BF16)
    return _dot(hi, u) + _dot(lo, u)


def _mm(a, b, *, name, out_dtype, res=None, trans_a=False, trans_b=False):
    assert not (trans_a and trans_b)
    if trans_a:
        kdim, m = a.shape
    else:
        m, kdim = a.shape
    if trans_b:
        n, kb = b.shape
    else:
        kb, n = b.shape
    assert kb == kdim, (a.shape, b.shape)
    if trans_a:
        tm = _pick(m, (1408, 1024, 512, 256, 128))
        tn = _pick(n, (1024, 1280, 1408, 768, 512, 256, 128))
        tk = _pick(kdim, (1024, 512, 256))
    else:
        tm = _pick(m, (1024, 512, 256, 128))
        tk = kdim if kdim <= 2816 else _pick(kdim, (2048, 1536, 1408, 1280, 1024, 512))
        tn = _pick(n, (512, 256, 128) if tk > 2048 else (1408, 1280, 1024, 768, 512, 256, 128))
    nk = kdim // tk
    has_res = res is not None

    def body(*refs):
        if has_res:
            a_ref, b_ref, r_ref, o_ref = refs[:4]
            scr = refs[4:]
        else:
            a_ref, b_ref, o_ref = refs[:3]
            r_ref = None
            scr = refs[3:]
        av = a_ref[...].astype(BF16)
        bv = b_ref[...].astype(BF16)
        if trans_a:
            p = _dot_tn(av, bv)
        elif trans_b:
            p = _dot_nt(av, bv)
        else:
            p = _dot(av, bv)

        def finish(acc):
            if has_res:
                acc = acc + r_ref[...]
            o_ref[...] = acc.astype(o_ref.dtype)

        if nk == 1:
            finish(p)
        else:
            acc_ref = scr[0]
            k = pl.program_id(2)

            @pl.when(k == 0)
            def _():
                acc_ref[...] = p

            @pl.when(k > 0)
            def _():
                acc_ref[...] += p

            @pl.when(k == nk - 1)
            def _():
                finish(acc_ref[...])

    if trans_a:
        a_spec = pl.BlockSpec((tk, tm), lambda i, j, k: (k, i))
    else:
        a_spec = pl.BlockSpec((tm, tk), lambda i, j, k: (i, k))
    if trans_b:
        b_spec = pl.BlockSpec((tn, tk), lambda i, j, k: (j, k))
    else:
        b_spec = pl.BlockSpec((tk, tn), lambda i, j, k: (k, j))
    in_specs = [a_spec, b_spec]
    args = [a, b]
    if has_res:
        in_specs.append(pl.BlockSpec((tm, tn), lambda i, j, k: (i, j)))
        args.append(res)
    return pl.pallas_call(
        body, name=name,
        grid=(m // tm, n // tn, nk),
        in_specs=in_specs,
        out_specs=pl.BlockSpec((tm, tn), lambda i, j, k: (i, j)),
        out_shape=jax.ShapeDtypeStruct((m, n), out_dtype),
        scratch_shapes=[pltpu.VMEM((tm, tn), F32)] if nk > 1 else [],
        compiler_params=_cp("parallel", "parallel", "arbitrary"),
    )(*args)


def _rms_fwd(x, gains, *, name):
    t, d = x.shape
    tr = _pick(t, (512, 256, 128, 8))
    ng = len(gains)

    def body(x_ref, *rest):
        g_refs, n_refs, r_ref = rest[:ng], rest[ng:2 * ng], rest[2 * ng]
        xv = x_ref[...]
        r = lax.rsqrt(jnp.mean(xv * xv, axis=-1, keepdims=True) + EPS)
        xh = xv * r
        for g_ref, n_ref in zip(g_refs, n_refs):
            n_ref[...] = (xh * g_ref[...]).astype(BF16)
        r_ref[...] = r

    row = pl.BlockSpec((tr, d), lambda i: (i, 0))
    gsp = pl.BlockSpec((1, d), lambda i: (0, 0))
    outs = pl.pallas_call(
        body, name=name, grid=(t // tr,),
        in_specs=[row] + [gsp] * ng,
        out_specs=[row] * ng + [pl.BlockSpec((tr, 1), lambda i: (i, 0))],
        out_shape=[jax.ShapeDtypeStruct((t, d), BF16)] * ng + [jax.ShapeDtypeStruct((t, 1), F32)],
        compiler_params=_cp("parallel"),
    )(x, *gains)
    return list(outs[:ng]), outs[ng]


def _rms_bwd(x, r, pairs, dres, *, name, need_dx=True):
    t, d = x.shape
    tr = _pick(t, (512, 256, 128, 8))
    npair = len(pairs)
    has_res = dres is not None

    def body(*refs):
        x_ref, r_ref = refs[:2]
        pr = refs[2:2 + 2 * npair]
        pos = 2 + 2 * npair
        res_ref = None
        if has_res:
            res_ref = refs[pos]
            pos += 1
        dx_ref = None
        if need_dx:
            dx_ref = refs[pos]
            pos += 1
        dg_refs = refs[pos:pos + npair]
        i = pl.program_id(0)
        rv = r_ref[...]
        xh = x_ref[...] * rv
        dx = res_ref[...] if has_res else None
        for k in range(npair):
            dn = pr[2 * k][...].astype(F32)
            g = pr[2 * k + 1][...]
            part = jnp.sum(dn * xh, axis=0, keepdims=True)

            @pl.when(i == 0)
            def _():
                dg_refs[k][...] = part

            @pl.when(i > 0)
            def _():
                dg_refs[k][...] += part

            if need_dx:
                dxh = dn * g
                c = jnp.mean(dxh * xh, axis=-1, keepdims=True)
                term = rv * (dxh - xh * c)
                dx = term if dx is None else dx + term
        if need_dx:
            dx_ref[...] = dx

    row = pl.BlockSpec((tr, d), lambda i: (i, 0))
    gsp = pl.BlockSpec((1, d), lambda i: (0, 0))
    in_specs = [row, pl.BlockSpec((tr, 1), lambda i: (i, 0))]
    args = [x, r]
    for dn, g in pairs:
        in_specs += [row, gsp]
        args += [dn, g]
    if has_res:
        in_specs.append(row)
        args.append(dres)
    out_specs, out_shape = [], []
    if need_dx:
        out_specs.append(row)
        out_shape.append(jax.ShapeDtypeStruct((t, d), F32))
    out_specs += [gsp] * npair
    out_shape += [jax.ShapeDtypeStruct((1, d), F32)] * npair
    outs = pl.pallas_call(
        body, name=name, grid=(t // tr,), in_specs=in_specs, out_specs=out_specs, out_shape=out_shape,
        compiler_params=_cp("arbitrary"),
    )(*args)
    if need_dx:
        return outs[0], list(outs[1:])
    return None, list(outs)


def _loss_head(h, g, tgt, *, name):
    t, d = h.shape
    tr = _pick(t, (512, 256, 128, 8))

    def body(h_ref, g_ref, t_ref, dh_ref, dg_ref, l_ref):
        i = pl.program_id(0)
        xv = h_ref[...]
        gv = g_ref[...]
        r = lax.rsqrt(jnp.mean(xv * xv, axis=-1, keepdims=True) + EPS)
        xh = xv * r
        e = xh * gv - t_ref[...]
        dy = e * (1.0 / d)
        lpart = jnp.sum(e * e, axis=0, keepdims=True)
        gpart = jnp.sum(dy * xh, axis=0, keepdims=True)

        @pl.when(i == 0)
        def _():
            l_ref[...] = lpart
            dg_ref[...] = gpart

        @pl.when(i > 0)
        def _():
            l_ref[...] += lpart
            dg_ref[...] += gpart

        dxh = dy * gv
        c = jnp.mean(dxh * xh, axis=-1, keepdims=True)
        dh_ref[...] = r * (dxh - xh * c)

    row = pl.BlockSpec((tr, d), lambda i: (i, 0))
    gsp = pl.BlockSpec((1, d), lambda i: (0, 0))
    return pl.pallas_call(
        body, name=name, grid=(t // tr,), in_specs=[row, gsp, row], out_specs=[row, gsp, gsp],
        out_shape=[jax.ShapeDtypeStruct((t, d), F32), jax.ShapeDtypeStruct((1, d), F32),
                   jax.ShapeDtypeStruct((1, d), F32)],
        compiler_params=_cp("arbitrary"),
    )(h, g, tgt)


GRP = 4 * HEAD_DIM
SB_KB = 2 * QBLK
SB_QB = 2 * QBLK


def _head_masks4(shape):
    lane = lax.broadcasted_iota(jnp.int32, shape, 1)
    return [(lane >= HEAD_DIM * h) & (lane < HEAD_DIM * (h + 1)) for h in range(4)]


def _neg_softplus(z):
    nz = -z
    return jnp.minimum(nz, 0.0) - jnp.log(1.0 + jnp.exp(jnp.minimum(z, nz)))


def _stacked_col_minus_row():
    rowi = lax.broadcasted_iota(jnp.int32, (4 * SB_QB, SB_KB), 0)
    coli = lax.broadcasted_iota(jnp.int32, (4 * SB_QB, SB_KB), 1)
    return coli - (rowi & (SB_QB - 1))


def _sb_fwd(proj, after, *, name):
    b, s, _ = proj.shape
    nq = s // SB_QB
    ngrp = SB_WIDTH // GRP

    def body(q_ref, k_ref, v_ref, after_ref, o_ref, r_ref, acc_ref, car_ref):
        i = pl.program_id(2)
        masks = _head_masks4((SB_QB, GRP))
        row = lax.broadcasted_iota(jnp.int32, (SB_KB, SB_KB), 0)
        col = lax.broadcasted_iota(jnp.int32, (SB_KB, SB_KB), 1)
        later_mat = (row > col).astype(BF16)
        col_minus_row = _stacked_col_minus_row()
        qs = q_ref[0] * jnp.asarray(ATT_SCALE, BF16)
        q_stack = jnp.concatenate([jnp.where(mk, qs, jnp.zeros_like(qs)) for mk in masks], axis=0)
        acc_ref[...] = jnp.zeros_like(acc_ref)
        car_ref[...] = jnp.zeros_like(car_ref)

        def process(jbs, masked):
            offs = [pl.multiple_of(jb * SB_KB, SB_KB) for jb in jbs]
            k2s = [k_ref[0, pl.ds(off, SB_KB), :] for off in offs]
            v2s = [v_ref[0, pl.ds(off, SB_KB), :] for off in offs]
            zs = [_dot_nt(q_stack, k2) for k2 in k2s]
            lss = []
            for jb, z in zip(jbs, zs):
                ls = _neg_softplus(z)
                if masked:
                    causal = col_minus_row < (i * SB_QB - jb * SB_KB)
                    ls = jnp.where(causal, ls, 0.0)
                lss.append(ls)
            laters = [_dot(ls.astype(BF16), later_mat) for ls in lss]
            car = car_ref[...]
            ws = []
            for jb, z, ls, later in zip(jbs, zs, lss, laters):
                w = jnp.exp((z + ls) + later + car)
                if masked:
                    w = jnp.where(col_minus_row < (i * SB_QB - jb * SB_KB), w, 0.0)
                ws.append(w.astype(BF16))
                car = car + jnp.sum(ls, axis=1, keepdims=True)
            car_ref[...] = car
            acc_ref[...] += functools.reduce(jnp.add, [_dot(w, v2) for w, v2 in zip(ws, v2s)])

        top = (i * SB_QB) // SB_KB
        process([top], True)

        def step(jj, carry):
            process([top - 1 - 2 * jj, top - 2 - 2 * jj], False)
            return carry

        lax.fori_loop(0, top // 2, step, 0)

        @pl.when(top % 2 == 1)
        def _():
            process([0], False)

        o = acc_ref[pl.ds(0, SB_QB), :]
        r = car_ref[pl.ds(0, SB_QB), :]
        for h in range(1, 4):
            o = jnp.where(masks[h], acc_ref[pl.ds(h * SB_QB, SB_QB), :], o)
            r = jnp.where(masks[h], car_ref[pl.ds(h * SB_QB, SB_QB), :], r)
        o_ref[0] = o.astype(o_ref.dtype)
        r_ref[0] = r

    blk = pl.BlockSpec((1, SB_QB, GRP), lambda bb, p, i: (bb, i, p))
    return pl.pallas_call(
        body, name=name, grid=(b, ngrp, nq),
        in_specs=[blk,
                  pl.BlockSpec((1, s, GRP), lambda bb, p, i: (bb, 0, ngrp + p)),
                  pl.BlockSpec((1, s, GRP), lambda bb, p, i: (bb, 0, 2 * ngrp + p)),
                  pl.BlockSpec(memory_space=pl.ANY)],
        out_specs=[blk, blk],
        out_shape=[jax.ShapeDtypeStruct((b, s, SB_WIDTH), BF16), jax.ShapeDtypeStruct((b, s, SB_WIDTH), F32)],
        scratch_shapes=[pltpu.VMEM((4 * SB_QB, GRP), F32), pltpu.VMEM((4 * SB_QB, SB_KB), F32)],
        compiler_params=_cp("parallel", "parallel", "arbitrary"),
    )(proj, proj, proj, after)


def _sb_bwd(proj, dcat, rsum, after, *, name):
    b, s, _ = proj.shape
    nq = s // SB_QB
    ngrp = SB_WIDTH // GRP

    def body(q_ref, k_ref, v_ref, do_ref, r_ref, after_ref, dq_ref, dk_out, dv_out, dq_acc, cp_ref, cg_ref,
             dk_ref, dv_ref):
        i = pl.program_id(2)

        @pl.when(i == 0)
        def _():
            dk_ref[...] = jnp.zeros_like(dk_ref)
            dv_ref[...] = jnp.zeros_like(dv_ref)

        masks = _head_masks4((SB_QB, GRP))
        row = lax.broadcasted_iota(jnp.int32, (SB_KB, SB_KB), 0)
        col = lax.broadcasted_iota(jnp.int32, (SB_KB, SB_KB), 1)
        later_mat = (row > col).astype(BF16)
        excl_mat = (row < col).astype(BF16)
        col_minus_row = _stacked_col_minus_row()
        qs = q_ref[0] * jnp.asarray(ATT_SCALE, BF16)
        do = do_ref[0]
        q_stack = jnp.concatenate([jnp.where(mk, qs, jnp.zeros_like(qs)) for mk in masks], axis=0)
        do_stack = jnp.concatenate([jnp.where(mk, do, jnp.zeros_like(do)) for mk in masks], axis=0)
        rv = r_ref[0]
        r_stack = jnp.concatenate([rv[:, HEAD_DIM * h:HEAD_DIM * h + 1] for h in range(4)], axis=0)
        dq_acc[...] = jnp.zeros_like(dq_acc)
        cp_ref[...] = jnp.zeros_like(cp_ref)
        cg_ref[...] = jnp.zeros_like(cg_ref)

        def process(jbs, masked):
            offs = [pl.multiple_of(jb * SB_KB, SB_KB) for jb in jbs]
            k2s = [k_ref[0, pl.ds(off, SB_KB), :] for off in offs]
            v2s = [v_ref[0, pl.ds(off, SB_KB), :] for off in offs]
            zs = [_dot_nt(q_stack, k2) for k2 in k2s]
            dws = [_dot_nt(do_stack, v2) for v2 in v2s]
            lss, lsigs = [], []
            for jb, z in zip(jbs, zs):
                ls = _neg_softplus(z)
                lsigs.append(z + ls)
                if masked:
                    ls = jnp.where(col_minus_row < (i * SB_QB - jb * SB_KB), ls, 0.0)
                lss.append(ls)
            laters = [_dot(ls.astype(BF16), later_mat) for ls in lss]
            cpv = cp_ref[...]
            ws, gs = [], []
            for jb, ls, lsig, later, dw in zip(jbs, lss, lsigs, laters, dws):
                cpv = cpv + jnp.sum(ls, axis=1, keepdims=True)
                w = jnp.exp(lsig + ((r_stack - cpv) + later))
                if masked:
                    w = jnp.where(col_minus_row < (i * SB_QB - jb * SB_KB), w, 0.0)
                ws.append(w.astype(BF16))
                gs.append(dw * w)
            cp_ref[...] = cpv
            gpres = [_dot(g.astype(BF16), excl_mat) for g in gs]
            cgv = cg_ref[...]
            dzs = []
            for jb, g, lsig, gpre in zip(jbs, gs, lsigs, gpres):
                dz = g - jnp.exp(lsig) * (g + (gpre + cgv))
                if masked:
                    dz = jnp.where(col_minus_row < (i * SB_QB - jb * SB_KB), dz, 0.0)
                dzs.append(dz.astype(BF16))
                cgv = cgv + jnp.sum(g, axis=1, keepdims=True)
            cg_ref[...] = cgv
            dq_acc[...] += functools.reduce(jnp.add, [_dot(dzb, k2) for dzb, k2 in zip(dzs, k2s)])
            for off, dzb, wb in zip(offs, dzs, ws):
                dk_ref[0, pl.ds(off, SB_KB), :] += _dot_tn(dzb, q_stack)
                dv_ref[0, pl.ds(off, SB_KB), :] += _dot_tn(wb, do_stack)

        top = (i * SB_QB) // SB_KB

        def step(jj, carry):
            process([2 * jj, 2 * jj + 1], False)
            return carry

        lax.fori_loop(0, top // 2, step, 0)

        @pl.when(top % 2 == 1)
        def _():
            process([top - 1], False)

        process([top], True)
        dq = dq_acc[pl.ds(0, SB_QB), :]
        for h in range(1, 4):
            dq = jnp.where(masks[h], dq_acc[pl.ds(h * SB_QB, SB_QB), :], dq)
        dq_ref[0] = (dq * ATT_SCALE).astype(dq_ref.dtype)

        @pl.when(i == nq - 1)
        def _():
            dk_out[...] = dk_ref[...].astype(dk_out.dtype)
            dv_out[...] = dv_ref[...].astype(dv_out.dtype)

    blk = pl.BlockSpec((1, SB_QB, GRP), lambda bb, p, i: (bb, i, p))
    seq = pl.BlockSpec((1, s, GRP), lambda bb, p, i: (bb, 0, p))
    return pl.pallas_call(
        body, name=name, grid=(b, ngrp, nq),
        in_specs=[blk,
                  pl.BlockSpec((1, s, GRP), lambda bb, p, i: (bb, 0, ngrp + p)),
                  pl.BlockSpec((1, s, GRP), lambda bb, p, i: (bb, 0, 2 * ngrp + p)),
                  blk, blk, pl.BlockSpec(memory_space=pl.ANY)],
        out_specs=[blk, seq, seq],
        out_shape=[jax.ShapeDtypeStruct((b, s, SB_WIDTH), BF16)] * 3,
        scratch_shapes=[pltpu.VMEM((4 * SB_QB, GRP), F32), pltpu.VMEM((4 * SB_QB, SB_KB), F32),
                        pltpu.VMEM((4 * SB_QB, SB_KB), F32), pltpu.VMEM((1, s, GRP), F32),
                        pltpu.VMEM((1, s, GRP), F32)],
        compiler_params=_cp("parallel", "parallel", "arbitrary"),
    )(proj, proj, proj, dcat, rsum, after)


def _band_bias(slopes_scaled):
    a = lax.broadcasted_iota(jnp.int32, (QBLK, 2 * QBLK), 0)
    bcol = lax.broadcasted_iota(jnp.int32, (QBLK, 2 * QBLK), 1)
    delta = a + QBLK - bcol
    in_band = (delta >= 0) & (delta <= QBLK)
    dist = delta.astype(F32)
    bias = jnp.concatenate([(-sl) * dist for sl in slopes_scaled], axis=0)
    return jnp.concatenate([in_band] * 4, axis=0), jnp.concatenate([bcol >= QBLK] * 4, axis=0), bias


def _stack_heads(x, masks):
    return jnp.concatenate([jnp.where(mk, x, jnp.zeros_like(x)) for mk in masks], axis=0)


def _unstack_heads(x, masks):
    out = jnp.broadcast_to(x[0:QBLK], (QBLK, GRP))
    for h in range(1, 4):
        out = jnp.where(masks[h], x[h * QBLK:(h + 1) * QBLK], out)
    return out


PER_HEAD = 8


def _head_column(x):
    return jnp.concatenate([x[:, h:h + 1] for h in range(4)], axis=0)


def _head_lanes(col):
    lane = lax.broadcasted_iota(jnp.int32, (QBLK, PER_HEAD), 1)
    out = jnp.zeros((QBLK, PER_HEAD), F32)
    for h in range(4):
        out = jnp.where(lane == h, col[h * QBLK:(h + 1) * QBLK], out)
    return out


def _spread_heads(x8, rows):
    masks = _head_masks4((rows, GRP))
    out = jnp.broadcast_to(x8[:, 0:1], (rows, GRP))
    for h in range(1, 4):
        out = jnp.where(masks[h], x8[:, h:h + 1], out)
    return out


def _attn_units(n, l, banded):
    nsub = 4 if l % (4 * QBLK) == 0 else (2 if l % (2 * QBLK) == 0 else 1)
    nseq = 4 if (banded and nsub == 1 and n % 4 == 0) else 1
    return nseq, nsub


def _attn_specs(banded, nseq, nsub, q_lane_blk, k_lane_blk, v_lane_blk):
    tq = nsub * QBLK
    qs = pl.BlockSpec((nseq, tq, GRP), lambda n, i: (n, i, q_lane_blk))
    if banded:
        ks = [pl.BlockSpec((nseq, QBLK, GRP), lambda n, i: (n, jnp.maximum(nsub * i - 1, 0), k_lane_blk)),
              pl.BlockSpec((nseq, tq, GRP), lambda n, i: (n, i, k_lane_blk))]
        vs = [pl.BlockSpec((nseq, QBLK, GRP), lambda n, i: (n, jnp.maximum(nsub * i - 1, 0), v_lane_blk)),
              pl.BlockSpec((nseq, tq, GRP), lambda n, i: (n, i, v_lane_blk))]
    else:
        ks = [pl.BlockSpec((nseq, 2 * QBLK, GRP), lambda n, i: (n, 0, k_lane_blk))]
        vs = [pl.BlockSpec((nseq, 2 * QBLK, GRP), lambda n, i: (n, 0, v_lane_blk))]
    return qs, ks, vs


def _attn_fwd(q, k, v, *, name, banded, slopes_scaled=None, q_lane_blk=0, k_lane_blk=0, v_lane_blk=0):
    n, l, _ = q.shape
    nseq, nsub = _attn_units(n, l, banded)
    units = [(sq, u) for sq in range(nseq) for u in range(nsub)]
    tq = nsub * QBLK
    nkv = 2 if banded else 1

    def body(*refs):
        q_ref = refs[0]
        k_refs = refs[1:1 + nkv]
        v_refs = refs[1 + nkv:1 + 2 * nkv]
        o_ref, lse_ref = refs[1 + 2 * nkv:]
        step = pl.program_id(1)
        masks = _head_masks4((QBLK, GRP))
        if banded:
            in_band, is_cur, bias = _band_bias(slopes_scaled)
        scs, v2s = [], []
        for sq, u in units:
            qs = q_ref[sq, u * QBLK:(u + 1) * QBLK, :] * jnp.asarray(ATT_SCALE, BF16)
            if banded:
                kall = jnp.concatenate([k_refs[0][sq], k_refs[1][sq]], axis=0)
                vall = jnp.concatenate([v_refs[0][sq], v_refs[1][sq]], axis=0)
                k2 = kall[u * QBLK:(u + 2) * QBLK]
                v2s.append(vall[u * QBLK:(u + 2) * QBLK])
            else:
                k2 = k_refs[0][sq]
                v2s.append(v_refs[0][sq])
            scs.append(_dot_nt(_stack_heads(qs, masks), k2))
        ps, dens, lses = [], [], []
        for j, (sq, u) in enumerate(units):
            sc = scs[j]
            if banded:
                valid = in_band & (is_cur | (step * nsub + u > 0))
                sc = jnp.where(valid, sc + bias, NEG_BIG)
            m = jnp.max(sc, axis=-1, keepdims=True)
            p = jnp.exp(sc - m)
            den = jnp.sum(p, axis=-1, keepdims=True)
            ps.append(p.astype(BF16))
            dens.append(den)
            lses.append(m + jnp.log(den))
        ohs = [_dot(ps[j], v2s[j]) for j in range(len(units))]
        for j, (sq, u) in enumerate(units):
            o_ref[sq, u * QBLK:(u + 1) * QBLK, :] = _unstack_heads(ohs[j] / dens[j], masks).astype(o_ref.dtype)
            lse_ref[sq, u * QBLK:(u + 1) * QBLK, :] = _head_lanes(lses[j])

    qs, ks, vs = _attn_specs(banded, nseq, nsub, q_lane_blk, k_lane_blk, v_lane_blk)
    ob = pl.BlockSpec((nseq, tq, GRP), lambda nn, i: (nn, i, 0))
    return pl.pallas_call(
        body, name=name, grid=(n // nseq, l // tq),
        in_specs=[qs] + ks + vs, out_specs=[ob, pl.BlockSpec((nseq, tq, PER_HEAD), lambda nn, i: (nn, i, 0))],
        out_shape=[jax.ShapeDtypeStruct((n, l, GRP), BF16), jax.ShapeDtypeStruct((n, l, PER_HEAD), F32)],
        compiler_params=_cp("parallel", "arbitrary"),
    )(q, *([k] * nkv), *([v] * nkv))


def _attn_bwd(q, k, v, do, lse, delta, *, name, banded, slopes_scaled=None, q_lane_blk=0, k_lane_blk=0,
              v_lane_blk=0, do_lane_blk=0):
    n, l, _ = q.shape
    nseq, nsub = _attn_units(n, l, banded)
    units = [(sq, u) for sq in range(nseq) for u in range(nsub)]
    tq = nsub * QBLK
    nsteps = l // tq
    nkv = 2 if banded else 1
    lk = l if banded else 2 * QBLK

    def body(*refs):
        q_ref = refs[0]
        k_refs = refs[1:1 + nkv]
        v_refs = refs[1 + nkv:1 + 2 * nkv]
        do_ref, lse_ref, dl_ref, dq_ref, dk_out, dv_out, dk_ref, dv_ref = refs[1 + 2 * nkv:]
        step = pl.program_id(1)

        @pl.when(step == 0)
        def _():
            dk_ref[...] = jnp.zeros_like(dk_ref)
            dv_ref[...] = jnp.zeros_like(dv_ref)

        masks = _head_masks4((QBLK, GRP))
        if banded:
            in_band, is_cur, bias = _band_bias(slopes_scaled)
        q_st, do_st, k2s, scs, dps = [], [], [], [], []
        for j, (sq, u) in enumerate(units):
            rows = slice(u * QBLK, (u + 1) * QBLK)
            if banded:
                kall = jnp.concatenate([k_refs[0][sq], k_refs[1][sq]], axis=0)
                vall = jnp.concatenate([v_refs[0][sq], v_refs[1][sq]], axis=0)
                k2s.append(kall[u * QBLK:(u + 2) * QBLK])
                v2 = vall[u * QBLK:(u + 2) * QBLK]
            else:
                k2s.append(k_refs[0][sq])
                v2 = v_refs[0][sq]
            qs = q_ref[sq, rows, :] * jnp.asarray(ATT_SCALE, BF16)
            q_st.append(_stack_heads(qs, masks))
            do_st.append(_stack_heads(do_ref[sq, rows, :], masks))
            scs.append(_dot_nt(q_st[j], k2s[j]))
            dps.append(_dot_nt(do_st[j], v2))
        pbs, dss = [], []
        for j, (sq, u) in enumerate(units):
            rows = slice(u * QBLK, (u + 1) * QBLK)
            sc = scs[j]
            if banded:
                valid = in_band & (is_cur | (step * nsub + u > 0))
                sc = jnp.where(valid, sc + bias, NEG_BIG)
            p = jnp.exp(sc - _head_column(lse_ref[sq, rows, :]))
            pbs.append(p.astype(BF16))
            dss.append((p * (dps[j] - _head_column(dl_ref[sq, rows, :]))).astype(BF16))
        dqs = [_dot(dss[j], k2s[j]) for j in range(len(units))]
        dk2s = [_dot_tn(dss[j], q_st[j]) for j in range(len(units))]
        dv2s = [_dot_tn(pbs[j], do_st[j]) for j in range(len(units))]
        for j, (sq, u) in enumerate(units):
            dq_ref[sq, u * QBLK:(u + 1) * QBLK, :] = (_unstack_heads(dqs[j], masks) * ATT_SCALE).astype(dq_ref.dtype)
        if banded:
            for j, (sq, u) in enumerate(units):
                i = step * nsub + u
                cur = pl.multiple_of(i * QBLK, QBLK)
                dk_ref[sq, pl.ds(cur, QBLK), :] += dk2s[j][QBLK:]
                dv_ref[sq, pl.ds(cur, QBLK), :] += dv2s[j][QBLK:]

                @pl.when(i > 0)
                def _():
                    prev = pl.multiple_of((i - 1) * QBLK, QBLK)
                    dk_ref[sq, pl.ds(prev, QBLK), :] += dk2s[j][:QBLK]
                    dv_ref[sq, pl.ds(prev, QBLK), :] += dv2s[j][:QBLK]
        else:
            dk_ref[0] += functools.reduce(jnp.add, dk2s)
            dv_ref[0] += functools.reduce(jnp.add, dv2s)

        @pl.when(step == nsteps - 1)
        def _():
            dk_out[...] = dk_ref[...].astype(dk_out.dtype)
            dv_out[...] = dv_ref[...].astype(dv_out.dtype)

    qs, ks, vs = _attn_specs(banded, nseq, nsub, q_lane_blk, k_lane_blk, v_lane_blk)
    ob = pl.BlockSpec((nseq, tq, GRP), lambda nn, i: (nn, i, 0))
    stat = pl.BlockSpec((nseq, tq, PER_HEAD), lambda nn, i: (nn, i, 0))
    dos = pl.BlockSpec((nseq, tq, GRP), lambda nn, i: (nn, i, do_lane_blk))
    kvb = pl.BlockSpec((nseq, lk, GRP), lambda nn, i: (nn, 0, 0))
    return pl.pallas_call(
        body, name=name, grid=(n // nseq, nsteps),
        in_specs=[qs] + ks + vs + [dos, stat, stat], out_specs=[ob, kvb, kvb],
        out_shape=[jax.ShapeDtypeStruct((n, l, GRP), BF16), jax.ShapeDtypeStruct((n, lk, GRP), BF16),
                   jax.ShapeDtypeStruct((n, lk, GRP), BF16)],
        scratch_shapes=[pltpu.VMEM((nseq, lk, GRP), F32), pltpu.VMEM((nseq, lk, GRP), F32)],
        compiler_params=_cp("parallel", "arbitrary"),
    )(q, *([k] * nkv), *([v] * nkv), do, lse, delta)


def _attn_delta(do, o, *, name, lane_blks):
    t, _ = do.shape
    tr = _pick(t, (512, 256, 128, 8))
    ng = len(lane_blks)

    def body(*refs):
        do_refs, o_refs, d_ref = refs[:ng], refs[ng:2 * ng], refs[2 * ng]
        ra = lax.broadcasted_iota(jnp.int32, (GRP, LANE), 0) // HEAD_DIM
        rb = lax.broadcasted_iota(jnp.int32, (GRP, LANE), 1)
        head_sum = (ra == rb).astype(BF16)
        prod = None
        for a_ref, b_ref in zip(do_refs, o_refs):
            term = a_ref[...].astype(F32) * b_ref[...].astype(F32)
            prod = term if prod is None else prod + term
        d_ref[...] = _dot_split(prod, head_sum)[:, :PER_HEAD]

    specs = [pl.BlockSpec((tr, GRP), functools.partial(lambda i, lb: (i, lb), lb=lb)) for lb in lane_blks]
    return pl.pallas_call(
        body, name=name, grid=(t // tr,), in_specs=specs + specs,
        out_specs=pl.BlockSpec((tr, PER_HEAD), lambda i: (i, 0)),
        out_shape=jax.ShapeDtypeStruct((t, PER_HEAD), F32),
        compiler_params=_cp("parallel"),
    )(*([do] * ng), *([o] * ng))


def _dil_combine(os, lses, *, name):
    t, _ = os[0].shape
    tr = _pick(t, (512, 256, 128, 8))
    ng = len(os)

    def body(*refs):
        o_refs, l_refs = refs[:ng], refs[ng:2 * ng]
        out_ref, lse_ref = refs[2 * ng:]
        ls = [r[...] for r in l_refs]
        m = functools.reduce(jnp.maximum, ls)
        tot = None
        for lv in ls:
            e = jnp.exp(lv - m)
            tot = e if tot is None else tot + e
        lse = m + jnp.log(tot)
        for g in range(ng):
            alpha = _spread_heads(jnp.exp(ls[g] - lse), tr)
            out_ref[:, GRP * g:GRP * (g + 1)] = (o_refs[g][...].astype(F32) * alpha).astype(out_ref.dtype)
        lse_ref[...] = lse

    sp = pl.BlockSpec((tr, GRP), lambda i: (i, 0))
    st = pl.BlockSpec((tr, PER_HEAD), lambda i: (i, 0))
    return pl.pallas_call(
        body, name=name, grid=(t // tr,), in_specs=[sp] * ng + [st] * ng,
        out_specs=[pl.BlockSpec((tr, GRP * ng), lambda i: (i, 0)), st],
        out_shape=[jax.ShapeDtypeStruct((t, GRP * ng), BF16), jax.ShapeDtypeStruct((t, PER_HEAD), F32)],
        compiler_params=_cp("parallel"),
    )(*os, *lses)


FFN_LB = 256
FFN_ROWS = 256
HALO = 16


def _conv_chunk(u_ref, w, ci):
    r0 = pl.multiple_of(ci * FFN_ROWS, FFN_ROWS)
    cur = u_ref[0, pl.ds(r0, FFN_ROWS), :].astype(F32)
    p0 = pl.multiple_of(jnp.maximum(r0 - HALO, 0), HALO)
    prev = u_ref[0, pl.ds(p0, HALO), :].astype(F32)
    prev = jnp.where(ci > 0, prev, 0.0)
    rowi = lax.broadcasted_iota(jnp.int32, (8, cur.shape[1]), 0)
    r1 = pltpu.roll(cur, 1, 0)
    r2 = pltpu.roll(cur, 2, 0)
    s1 = jnp.concatenate([jnp.where(rowi == 0, prev[HALO - 1:HALO], r1[0:8]), r1[8:]], axis=0)
    s2 = jnp.concatenate([jnp.where(rowi == 0, prev[HALO - 2:HALO - 1],
                                    jnp.where(rowi == 1, prev[HALO - 1:HALO], r2[0:8])), r2[8:]], axis=0)
    c = w[0:1] * s2
    c = c + w[1:2] * s1
    c = c + w[2:3] * cur
    return c, cur, s1, s2


def _ffn_mid_fwd(u, wconv, *, name):
    b, s, f2 = u.shape
    f = f2 // 2
    nlb = f // FFN_LB

    def body(ua_ref, ug_ref, wa_ref, wg_ref, h_ref):
        wa = wa_ref[...]
        wg = wg_ref[...]

        def step(ci, carry):
            ca = _conv_chunk(ua_ref, wa, ci)[0]
            cg = _conv_chunk(ug_ref, wg, ci)[0]
            r0 = pl.multiple_of(ci * FFN_ROWS, FFN_ROWS)
            h_ref[0, pl.ds(r0, FFN_ROWS), :] = (cg * jax.nn.sigmoid(cg) * ca).astype(h_ref.dtype)
            return carry

        lax.fori_loop(0, s // FFN_ROWS, step, 0)

    return pl.pallas_call(
        body, name=name, grid=(nlb, b),
        in_specs=[pl.BlockSpec((1, s, FFN_LB), lambda l, bb: (bb, 0, l)),
                  pl.BlockSpec((1, s, FFN_LB), lambda l, bb: (bb, 0, nlb + l)),
                  pl.BlockSpec((3, FFN_LB), lambda l, bb: (0, l)),
                  pl.BlockSpec((3, FFN_LB), lambda l, bb: (0, nlb + l))],
        out_specs=pl.BlockSpec((1, s, FFN_LB), lambda l, bb: (bb, 0, l)),
        out_shape=jax.ShapeDtypeStruct((b, s, f), BF16),
        compiler_params=_cp("parallel", "parallel"),
    )(u, u, wconv, wconv)


def _ffn_mid_bwd(u, wconv, dh, *, name):
    b, s, f2 = u.shape
    f = f2 // 2
    nlb = f // FFN_LB
    nchunk = s // FFN_ROWS

    def body(ua_ref, ug_ref, wa_ref, wg_ref, dh_ref, dua_ref, dug_ref, dwa_ref, dwg_ref):
        bb = pl.program_id(1)
        wa = wa_ref[...]
        wg = wg_ref[...]
        rowi = lax.broadcasted_iota(jnp.int32, (8, FFN_LB), 0)
        last = FFN_ROWS - 8

        def conv_transpose(dc, nxt, w):
            r1 = pltpu.roll(dc, FFN_ROWS - 1, 0)
            r2 = pltpu.roll(dc, FFN_ROWS - 2, 0)
            n1 = jnp.concatenate([r1[:last], jnp.where(rowi == 7, nxt[0:1], r1[last:])], axis=0)
            n2 = jnp.concatenate([r2[:last], jnp.where(rowi == 6, nxt[0:1],
                                                       jnp.where(rowi == 7, nxt[1:2], r2[last:]))], axis=0)
            return w[2:3] * dc + w[1:2] * n1 + w[0:1] * n2

        def step(t, carry):
            ci = nchunk - 1 - t
            nxt_a, nxt_g = carry[0], carry[1]
            r0 = pl.multiple_of(ci * FFN_ROWS, FFN_ROWS)
            ca, cura, s1a, s2a = _conv_chunk(ua_ref, wa, ci)
            cg, curg, s1g, s2g = _conv_chunk(ug_ref, wg, ci)
            dhv = dh_ref[0, pl.ds(r0, FFN_ROWS), :].astype(F32)
            sg = jax.nn.sigmoid(cg)
            da = dhv * (cg * sg)
            dg = dhv * ca * (sg * (1.0 + cg * (1.0 - sg)))
            dua_ref[0, pl.ds(r0, FFN_ROWS), :] = conv_transpose(da, nxt_a, wa).astype(dua_ref.dtype)
            dug_ref[0, pl.ds(r0, FFN_ROWS), :] = conv_transpose(dg, nxt_g, wg).astype(dug_ref.dtype)
            red = lambda x: jnp.sum(x, axis=0, keepdims=True)
            parts = (red(da * s2a), red(da * s1a), red(da * cura), red(dg * s2g), red(dg * s1g), red(dg * curg))
            return (da[0:8], dg[0:8]) + tuple(c + p for c, p in zip(carry[2:], parts))

        zero = jnp.zeros((1, FFN_LB), F32)
        zero8 = jnp.zeros((8, FFN_LB), F32)
        taps = lax.fori_loop(0, nchunk, step, (zero8, zero8) + (zero,) * 6)[2:]

        @pl.when(bb == 0)
        def _():
            for k in range(3):
                dwa_ref[k:k + 1, :] = taps[k]
                dwg_ref[k:k + 1, :] = taps[3 + k]

        @pl.when(bb > 0)
        def _():
            for k in range(3):
                dwa_ref[k:k + 1, :] += taps[k]
                dwg_ref[k:k + 1, :] += taps[3 + k]

    seq_a = pl.BlockSpec((1, s, FFN_LB), lambda l, bb: (bb, 0, l))
    seq_g = pl.BlockSpec((1, s, FFN_LB), lambda l, bb: (bb, 0, nlb + l))
    wsp = pl.BlockSpec((3, FFN_LB), lambda l, bb: (0, l))
    return pl.pallas_call(
        body, name=name, grid=(nlb, b),
        in_specs=[seq_a, seq_g, wsp, pl.BlockSpec((3, FFN_LB), lambda l, bb: (0, nlb + l)), seq_a],
        out_specs=[seq_a, seq_a, wsp, wsp],
        out_shape=[jax.ShapeDtypeStruct((b, s, f), BF16), jax.ShapeDtypeStruct((b, s, f), BF16),
                   jax.ShapeDtypeStruct((3, f), F32), jax.ShapeDtypeStruct((3, f), F32)],
        compiler_params=_cp("parallel", "arbitrary"),
    )(u, u, wconv, wconv, dh)


def _adam_math(w, g, m, v):
    m2 = ADAM_B1 * m + (1.0 - ADAM_B1) * g
    v2 = ADAM_B2 * v + (1.0 - ADAM_B2) * (g * g)
    m_hat = m2 / (1.0 - ADAM_B1 ** ADAM_STEP)
    v_hat = v2 / (1.0 - ADAM_B2 ** ADAM_STEP)
    delta = -ADAM_LR * (m_hat / (jnp.sqrt(v_hat) + ADAM_EPS) + ADAM_WD * w)
    return delta, m2, v2


def _adam(w, g, m, v, *, name):
    r, c = w.shape
    tr = _pick(r, (256, 128, 88, 64, 32, 16, 8))

    def body(w_ref, g_ref, m_ref, v_ref, d_ref, m2_ref, v2_ref):
        d, m2, v2 = _adam_math(w_ref[...], g_ref[...], m_ref[...], v_ref[...])
        d_ref[...] = d
        m2_ref[...] = m2
        v2_ref[...] = v2

    sp = pl.BlockSpec((tr, c), lambda i: (i, 0))
    return pl.pallas_call(
        body, name=name, grid=(r // tr,), in_specs=[sp] * 4, out_specs=[sp] * 3,
        out_shape=[jax.ShapeDtypeStruct((r, c), F32)] * 3,
        compiler_params=_cp("parallel"),
    )(w, g, m, v)


def _adam_small(quads, *, name):
    nq = len(quads)

    def body(*refs):
        ins, outs = refs[:4 * nq], refs[4 * nq:]
        for k in range(nq):
            w_ref, g_ref, m_ref, v_ref = ins[4 * k:4 * k + 4]
            d, m2, v2 = _adam_math(w_ref[...], g_ref[...], m_ref[...], v_ref[...])
            outs[3 * k][...] = d
            outs[3 * k + 1][...] = m2
            outs[3 * k + 2][...] = v2

    flat = [a for q in quads for a in q]
    out_shape = [jax.ShapeDtypeStruct(q[0].shape, F32) for q in quads for _ in range(3)]
    vm = pl.BlockSpec(memory_space=pltpu.VMEM)
    outs = pl.pallas_call(
        body, name=name, in_specs=[vm] * len(flat), out_specs=[vm] * len(out_shape), out_shape=out_shape,
        compiler_params=pltpu.CompilerParams(vmem_limit_bytes=VMEM_LIMIT_BYTES),
    )(*flat)
    return [tuple(outs[3 * k:3 * k + 3]) for k in range(nq)]


def _mesh_pos():
    return lax.axis_index("x"), lax.axis_index("y"), lax.axis_index("c")


def _flip(v, bit):
    return 1 - v if bit else v


def _all_gather_hbm(xl, *, name):
    r, c = xl.shape

    def body(x_ref, out_ref, send_sems, recv_sems, local_sem):
        x, y, cc = _mesh_pos()
        me, sibling = (x, y, cc), (x, y, 1 - cc)
        chips = [(1 - x, y), (x, 1 - y), (1 - x, 1 - y)]

        def rows(px, py, pc):
            return out_ref.at[pl.ds((4 * px + 2 * py + pc) * r, r), :]

        def copy(k, block, to, src=None):
            return pltpu.make_async_remote_copy(
                src_ref=rows(*block) if src is None else src, dst_ref=rows(*block),
                send_sem=send_sems.at[k], recv_sem=recv_sems.at[k], device_id=to, device_id_type=MESH_ID)

        mine = pltpu.make_async_copy(x_ref, rows(*me), local_sem)
        mine.start()
        first = [copy(0, me, sibling, src=x_ref)]
        first += [copy(1 + j, me, (*chip, cc), src=x_ref) for j, chip in enumerate(chips)]
        for cp in first:
            cp.start()
        passed = [copy(4 + j, (*chip, cc), sibling) for j, chip in enumerate(chips)]
        for j, chip in enumerate(chips):
            copy(1 + j, (*chip, cc), me).wait_recv()
            passed[j].start()
        copy(0, sibling, me).wait_recv()
        for j, chip in enumerate(chips):
            copy(4 + j, (*chip, 1 - cc), me).wait_recv()
        for cp in first + passed:
            cp.wait_send()
        mine.wait()

    hbm = pl.BlockSpec(memory_space=pltpu.HBM)
    return pl.pallas_call(
        body, name=name, in_specs=[hbm], out_specs=hbm,
        out_shape=jax.ShapeDtypeStruct((N_DEV * r, c), xl.dtype),
        scratch_shapes=[pltpu.SemaphoreType.DMA((7,)), pltpu.SemaphoreType.DMA((7,)), pltpu.SemaphoreType.DMA],
    )(xl)


def _all_reduce_small(xl, *, name):
    r, c = xl.shape

    def body(x_ref, sum_ref, all_ref, send_sems, recv_sems, local_sem):
        x, y, cc = _mesh_pos()
        me, sibling = (x, y, cc), (x, y, 1 - cc)
        chips = [(1 - x, y), (x, 1 - y), (1 - x, 1 - y)]

        def rows(px, py, pc):
            return all_ref.at[pl.ds((4 * px + 2 * py + pc) * r, r), :]

        def copy(k, block, to, src=None):
            return pltpu.make_async_remote_copy(
                src_ref=rows(*block) if src is None else src, dst_ref=rows(*block),
                send_sem=send_sems.at[k], recv_sem=recv_sems.at[k], device_id=to, device_id_type=MESH_ID)

        mine = pltpu.make_async_copy(x_ref, rows(*me), local_sem)
        mine.start()
        first = [copy(0, me, sibling, src=x_ref)]
        first += [copy(1 + j, me, (*chip, cc), src=x_ref) for j, chip in enumerate(chips)]
        for cp in first:
            cp.start()
        passed = [copy(4 + j, (*chip, cc), sibling) for j, chip in enumerate(chips)]
        for j, chip in enumerate(chips):
            copy(1 + j, (*chip, cc), me).wait_recv()
            passed[j].start()
        copy(0, sibling, me).wait_recv()
        for j, chip in enumerate(chips):
            copy(4 + j, (*chip, 1 - cc), me).wait_recv()
        for cp in first + passed:
            cp.wait_send()
        mine.wait()
        tot = all_ref[pl.ds(0, r), :]
        for dd in range(1, N_DEV):
            tot = tot + all_ref[pl.ds(dd * r, r), :]
        sum_ref[...] = tot

    vm = pl.BlockSpec(memory_space=pltpu.VMEM)
    return pl.pallas_call(
        body, name=name, in_specs=[vm], out_specs=[vm, vm],
        out_shape=[jax.ShapeDtypeStruct((r, c), F32), jax.ShapeDtypeStruct((N_DEV * r, c), F32)],
        scratch_shapes=[pltpu.SemaphoreType.DMA((7,)), pltpu.SemaphoreType.DMA((7,)), pltpu.SemaphoreType.DMA],
    )(xl)[0]


N_PEERS = N_DEV - 1
_HBM = pl.BlockSpec(memory_space=pltpu.HBM)
_SEM = pl.BlockSpec(memory_space=pltpu.SEMAPHORE)


def _peer_list(x, y, cc):
    return [(_flip(x, rel & 4), _flip(y, rel & 2), _flip(cc, rel & 1)) for rel in range(1, N_DEV)]


def _dev_index(p):
    return 4 * p[0] + 2 * p[1] + p[2]


def _split_copy(src_ref, land_ref, sems, k, peer, me, gather, landing_of):
    if gather:
        r = src_ref.shape[0]
        src = src_ref
        dst = land_ref.at[pl.ds(_dev_index(landing_of) * r, r), :]
    else:
        src = src_ref.at[_dev_index(peer)]
        dst = land_ref.at[_dev_index(landing_of)]
    return pltpu.make_async_remote_copy(src_ref=src, dst_ref=dst, send_sem=sems[k], recv_sem=sems[N_PEERS + k],
                                        device_id=peer, device_id_type=MESH_ID)


def _exchange_start(src, land, *, name, gather):
    def body(src_ref, land_ref, *rest):
        sems = rest[:2 * N_PEERS]
        token = rest[2 * N_PEERS + 2]
        x, y, cc = _mesh_pos()
        me = (x, y, cc)
        for k, peer in enumerate(_peer_list(x, y, cc)):
            _split_copy(src_ref, land_ref, sems, k, peer, me, gather, landing_of=me).start()
        token[...] = jnp.zeros_like(token)

    outs = pl.pallas_call(
        body, name=name,
        out_shape=tuple([pltpu.SemaphoreType.DMA(())] * (2 * N_PEERS)) + (
            pltpu.HBM(src.shape, src.dtype), pltpu.HBM(land.shape, land.dtype),
            jax.ShapeDtypeStruct((8, LANE), F32)),
        in_specs=(_HBM, _HBM),
        out_specs=tuple([_SEM] * (2 * N_PEERS)) + (_HBM, _HBM, pl.BlockSpec(memory_space=pltpu.VMEM)),
        input_output_aliases={0: 2 * N_PEERS, 1: 2 * N_PEERS + 1},
        compiler_params=pltpu.CompilerParams(has_side_effects=pltpu.SideEffectType.DATAFLOW_SIDE_EFFECTING),
    )(pltpu.with_memory_space_constraint(src, pltpu.HBM), pltpu.with_memory_space_constraint(land, pltpu.HBM))
    return outs[:2 * N_PEERS], outs[2 * N_PEERS], outs[2 * N_PEERS + 1], outs[2 * N_PEERS + 2]


def _gather_start(local, me, *, name):
    rows, cols = local.shape
    land = lax.dynamic_update_slice(lax.empty((N_DEV * rows, cols), local.dtype), local, (me * rows, 0))
    return _exchange_start(local, land, name=name, gather=True)


def _exchange_wait(sems, src_thru, land_thru, after, *, name, gather):
    def body(src_ref, land_ref, *rest):
        sem_refs = rest[:2 * N_PEERS]
        x, y, cc = _mesh_pos()
        me = (x, y, cc)
        for k, peer in enumerate(_peer_list(x, y, cc)):
            cp = _split_copy(src_ref, land_ref, sem_refs, k, peer, me, gather, landing_of=peer)
            cp.wait_send()
            cp.wait_recv()

    outs = pl.pallas_call(
        body, name=name,
        out_shape=(pltpu.HBM(src_thru.shape, src_thru.dtype), pltpu.HBM(land_thru.shape, land_thru.dtype)),
        in_specs=(_HBM, _HBM) + tuple([_SEM] * (2 * N_PEERS)) + (pl.BlockSpec(memory_space=pl.ANY),),
        out_specs=(_HBM, _HBM), input_output_aliases={0: 0, 1: 1},
        compiler_params=pltpu.CompilerParams(has_side_effects=pltpu.SideEffectType.DATAFLOW_SIDE_EFFECTING),
    )(src_thru, land_thru, *sems, after)
    return outs[1]


def _sum_blocks(recv, own, after, *, name):
    nd, r, c = recv.shape
    tr = _pick(r, (448, 256, 128, 64, 32, 16))

    def body(x_ref, own_ref, after_ref, o_ref):
        x, y, cc = _mesh_pos()
        me = 4 * x + 2 * y + cc
        tot = None
        for dd in range(nd):
            term = jnp.where(me == dd, own_ref[0], x_ref[dd]).astype(F32)
            tot = term if tot is None else tot + term
        o_ref[...] = tot

    return pl.pallas_call(
        body, name=name, grid=(r // tr,),
        in_specs=[pl.BlockSpec((nd, tr, c), lambda i: (0, i, 0)), pl.BlockSpec((1, tr, c), lambda i: (0, i, 0)),
                  pl.BlockSpec(memory_space=pl.ANY)],
        out_specs=pl.BlockSpec((tr, c), lambda i: (i, 0)),
        out_shape=jax.ShapeDtypeStruct((r, c), F32),
        compiler_params=_cp("parallel"),
    )(recv, own, after)


SHARD_KIND = {"a_w_in": "col", "a_w_out": "row", "a_w_mem_kv": "row", "a_ffn_up": "col", "a_ffn_down": "row",
              "w_kv_shared": "col", "b_w_in": "row", "b_w_out": "row", "b_w_mem_kv": "row", "b_ffn_up": "col",
              "b_ffn_down": "row"}
EARLY_WEIGHTS = ("a_w_in", "a_w_mem_kv")
FFN_UP_WEIGHTS = ("a_ffn_up", "b_ffn_up")
WIDE_WEIGHTS = tuple(nm for nm in SHARD_KIND if nm not in EARLY_WEIGHTS + FFN_UP_WEIGHTS)
LATE_WEIGHTS = WIDE_WEIGHTS + FFN_UP_WEIGHTS


def _as2d(a):
    return a.reshape(a.shape[-2], a.shape[-1]) if a.ndim >= 2 else a.reshape(1, a.shape[0])


def _pack_local(shards):
    return jnp.concatenate([_as2d(s).astype(BF16).reshape(-1, PACK_COLS) for s in shards], axis=0)


def _unpack_full(gathered, names, shapes):
    out = {}
    r0 = 0
    for name in names:
        rows, cols = shapes[name]
        nr = rows * cols // PACK_COLS
        blk = gathered[:, r0:r0 + nr, :].reshape(N_DEV, rows, cols)
        if SHARD_KIND[name] == "row":
            out[name] = blk.reshape(N_DEV * rows, cols)
        else:
            out[name] = blk.transpose(1, 0, 2).reshape(rows, N_DEV * cols)
        r0 += nr
    return out


def _pack_grads(grads, names, shapes):
    parts = []
    for name in names:
        rows, cols = shapes[name]
        g = grads[name]
        if SHARD_KIND[name] == "row":
            blk = g.reshape(N_DEV, rows, cols)
        else:
            blk = g.reshape(rows, N_DEV, cols).transpose(1, 0, 2)
        parts.append(blk.astype(BF16).reshape(N_DEV, rows * cols // PACK_COLS, PACK_COLS))
    return jnp.concatenate(parts, axis=1)


def _unpack_local(gsum, names, shapes):
    out = {}
    r0 = 0
    for name in names:
        rows, cols = shapes[name]
        nr = rows * cols // PACK_COLS
        out[name] = gsum[r0:r0 + nr].reshape(rows, cols)
        r0 += nr
    return out


def _by_residue(t, d):
    if d == 1:
        return t
    b, s, c = t.shape
    return t.reshape(b, s // d, d, c).transpose(0, 2, 1, 3).reshape(b * d, s // d, c)


def _from_residue(t, d, b):
    if d == 1:
        return t
    n, l, c = t.shape
    return t.reshape(b, d, l, c).transpose(0, 2, 1, 3).reshape(b, l * d, c)


def _alibi_slopes():
    return [2.0 ** (-ALIBI_MAX_BIAS * (i + 1) / N_DIL_HEADS) for i in range(N_DIL_HEADS)]


def _conv_ffn_fwd(xin, gain, w_up, wconv, w_down, tag, b, s):
    (n,), r = _rms_fwd(xin, [gain], name=f"{tag}_rms_ffn")
    u = _mm(n, w_up, name=f"{tag}_up", out_dtype=BF16).reshape(b, s, -1)
    hmid = _ffn_mid_fwd(u, wconv, name=f"{tag}_ffn_mid").reshape(b * s, -1)
    xout = _mm(hmid, w_down, name=f"{tag}_down", out_dtype=F32, res=xin)
    return xout, (n, r, u, hmid)


def _conv_ffn_bwd(dxout, xin, gain, saved, w_up, wconv, w_down, tag, b, s):
    n, r, u, hmid = saved
    f = hmid.shape[1]
    dhmid = _mm(dxout, w_down, name=f"{tag}_d_hmid", out_dtype=BF16, trans_b=True)
    g_down = _mm(hmid, dxout, name=f"{tag}_g_down", out_dtype=BF16, trans_a=True)
    du_a, du_g, gc_a, gc_g = _ffn_mid_bwd(u, wconv, dhmid.reshape(b, s, f), name=f"{tag}_ffn_mid_bwd")
    du_a = du_a.reshape(b * s, f)
    du_g = du_g.reshape(b * s, f)
    dn = _mm(du_a, w_up[:, :f], name=f"{tag}_d_n_a", out_dtype=F32, trans_b=True)
    dn = _mm(du_g, w_up[:, f:], name=f"{tag}_d_n_g", out_dtype=F32, res=dn, trans_b=True)
    g_up = jnp.concatenate([_mm(n, du_a, name=f"{tag}_g_up_a", out_dtype=BF16, trans_a=True),
                            _mm(n, du_g, name=f"{tag}_g_up_g", out_dtype=BF16, trans_a=True)], axis=1)
    dxin, (g_gain,) = _rms_bwd(xin, r, [(dn, gain)], dxout, name=f"{tag}_rms_ffn_bwd")
    return dxin, g_up, g_down, jnp.concatenate([gc_a, gc_g], axis=1), g_gain


def _mem_kv_fwd(mem2d, gain, w_mem_kv, tag, b):
    (nm,), rm = _rms_fwd(mem2d, [gain], name=f"{tag}_rms_mem")
    kvm = _mm(nm, w_mem_kv, name=f"{tag}_mem_kv", out_dtype=BF16)
    return kvm.reshape(b, -1, 2 * MEM_WIDTH), (nm, rm)


def _mem_kv_bwd(dk, dv, mem2d, gain, saved, w_mem_kv, tag):
    nm, rm = saved
    dkvm = jnp.concatenate([dk, dv], axis=-1).reshape(-1, 2 * MEM_WIDTH)
    dnm = _mm(dkvm, w_mem_kv, name=f"{tag}_d_nm", out_dtype=F32, trans_b=True)
    g_w = _mm(nm, dkvm, name=f"{tag}_g_mem_kv", out_dtype=BF16, trans_a=True)
    _, (g_gain,) = _rms_bwd(mem2d, rm, [(dnm, gain)], None, name=f"{tag}_rms_mem_bwd", need_dx=False)
    return g_w, g_gain


def kernel(x, mem, a_norm_attn, a_w_in, a_w_out, a_norm_mem, a_w_mem_kv, a_norm_ffn, a_ffn_up, a_ffn_conv, a_ffn_down, kv_norm, w_kv_shared, b_norm_attn, b_w_in, b_w_out, b_norm_mem, b_w_mem_kv, b_norm_ffn, b_ffn_up, b_ffn_conv, b_ffn_down, final_norm, loss_target, m_a_norm_attn, m_a_w_in, m_a_w_out, m_a_norm_mem, m_a_w_mem_kv, m_a_norm_ffn, m_a_ffn_up, m_a_ffn_conv, m_a_ffn_down, m_kv_norm, m_w_kv_shared, m_b_norm_attn, m_b_w_in, m_b_w_out, m_b_norm_mem, m_b_w_mem_kv, m_b_norm_ffn, m_b_ffn_up, m_b_ffn_conv, m_b_ffn_down, m_final_norm, v_a_norm_attn, v_a_w_in, v_a_w_out, v_a_norm_mem, v_a_w_mem_kv, v_a_norm_ffn, v_a_ffn_up, v_a_ffn_conv, v_a_ffn_down, v_kv_norm, v_w_kv_shared, v_b_norm_attn, v_b_w_in, v_b_w_out, v_b_norm_mem, v_b_w_mem_kv, v_b_norm_ffn, v_b_ffn_up, v_b_ffn_conv, v_b_ffn_down, v_final_norm):
    names = ["a_norm_attn", "a_w_in", "a_w_out", "a_norm_mem", "a_w_mem_kv", "a_norm_ffn", "a_ffn_up",
             "a_ffn_conv", "a_ffn_down", "kv_norm", "w_kv_shared", "b_norm_attn", "b_w_in", "b_w_out",
             "b_norm_mem", "b_w_mem_kv", "b_norm_ffn", "b_ffn_up", "b_ffn_conv", "b_ffn_down", "final_norm"]
    wl = dict(zip(names, [a_norm_attn, a_w_in, a_w_out, a_norm_mem, a_w_mem_kv, a_norm_ffn, a_ffn_up,
                          a_ffn_conv, a_ffn_down, kv_norm, w_kv_shared, b_norm_attn, b_w_in, b_w_out,
                          b_norm_mem, b_w_mem_kv, b_norm_ffn, b_ffn_up, b_ffn_conv, b_ffn_down, final_norm]))
    ml = dict(zip(names, [m_a_norm_attn, m_a_w_in, m_a_w_out, m_a_norm_mem, m_a_w_mem_kv, m_a_norm_ffn,
                          m_a_ffn_up, m_a_ffn_conv, m_a_ffn_down, m_kv_norm, m_w_kv_shared, m_b_norm_attn,
                          m_b_w_in, m_b_w_out, m_b_norm_mem, m_b_w_mem_kv, m_b_norm_ffn, m_b_ffn_up,
                          m_b_ffn_conv, m_b_ffn_down, m_final_norm]))
    vl = dict(zip(names, [v_a_norm_attn, v_a_w_in, v_a_w_out, v_a_norm_mem, v_a_w_mem_kv, v_a_norm_ffn,
                          v_a_ffn_up, v_a_ffn_conv, v_a_ffn_down, v_kv_norm, v_w_kv_shared, v_b_norm_attn,
                          v_b_w_in, v_b_w_out, v_b_norm_mem, v_b_w_mem_kv, v_b_norm_ffn, v_b_ffn_up,
                          v_b_ffn_conv, v_b_ffn_down, v_final_norm]))
    b, s, d = x.shape
    t = b * s
    my_x, my_y, my_c = _mesh_pos()
    me = 4 * my_x + 2 * my_y + my_c

    shapes = {nm: _as2d(wl[nm]).shape for nm in SHARD_KIND}
    early_local = _pack_local([wl[nm] for nm in EARLY_WEIGHTS])
    early_all = _all_gather_hbm(early_local, name="gather_early").reshape(N_DEV, early_local.shape[0], PACK_COLS)
    wf = _unpack_full(early_all, EARLY_WEIGHTS, shapes)
    wide_local = _pack_local([wl[nm] for nm in WIDE_WEIGHTS])
    up_local = jnp.concatenate([_as2d(wl[nm]).astype(BF16) for nm in FFN_UP_WEIGHTS], axis=0)
    gw_sems, gw_src, gw_land, gw_token = _gather_start(wide_local, me, name="gather_wide_start")
    gu_sems, gu_src, gu_land, gu_token = _gather_start(up_local, me, name="gather_up_start")

    sharded_small = ["a_norm_attn", "a_norm_mem", "a_norm_ffn", "a_ffn_conv", "b_ffn_conv"]
    small_flat = jnp.concatenate([wl[nm].reshape(-1) for nm in sharded_small])
    n_small = small_flat.shape[0]
    small_rows = -(-n_small // (8 * LANE)) * 8
    small_local = jnp.pad(small_flat, (0, small_rows * LANE - n_small)).reshape(small_rows, LANE)
    small_all = _all_gather_hbm(small_local, name="gather_small").reshape(N_DEV, small_rows * LANE)
    sfull = {}
    r0 = 0
    for nm in sharded_small:
        rows, cols = _as2d(wl[nm]).shape
        blk = small_all[:, r0:r0 + rows * cols].reshape(N_DEV, rows, cols)
        sfull[nm] = blk.transpose(1, 0, 2).reshape(rows, N_DEV * cols)
        r0 += rows * cols
    gain = {nm: sfull[nm] for nm in ("a_norm_attn", "a_norm_mem", "a_norm_ffn")}
    for nm in ("kv_norm", "b_norm_attn", "b_norm_mem", "b_norm_ffn", "final_norm"):
        gain[nm] = _as2d(wl[nm])
    conv_a, conv_b = sfull["a_ffn_conv"], sfull["b_ffn_conv"]

    x2d = x.reshape(t, d)
    mem2d = mem.reshape(-1, d)
    tgt2d = loss_target.reshape(t, d)
    qmem_blk_a = 3 * SB_WIDTH // GRP
    qmem_blk_b = DIL_WIDTH // GRP

    (n1,), r1 = _rms_fwd(x2d, [gain["a_norm_attn"]], name="a_rms_attn")
    proj_a = _mm(n1, wf["a_w_in"], name="a_in", out_dtype=BF16).reshape(b, s, -1)
    kvm_a, mem_saved_a = _mem_kv_fwd(mem2d, gain["a_norm_mem"], wf["a_w_mem_kv"], "a", b)
    o_sb, rsum = _sb_fwd(proj_a, gw_token + gu_token, name="a_sb_fwd")
    o_mem_a, lse_mem_a = _attn_fwd(proj_a, kvm_a, kvm_a, name="a_mem_fwd", banded=False,
                                   q_lane_blk=qmem_blk_a, k_lane_blk=0, v_lane_blk=1)
    wide_all = _exchange_wait(gw_sems, gw_src, gw_land, rsum, name="gather_wide_wait", gather=True)
    up_all = _exchange_wait(gu_sems, gu_src, gu_land, rsum, name="gather_up_wait", gather=True)
    wf.update(_unpack_full(wide_all.reshape(N_DEV, wide_local.shape[0], PACK_COLS), WIDE_WEIGHTS, shapes))
    up_rows, up_cols = shapes[FFN_UP_WEIGHTS[0]]
    up_all = up_all.reshape(N_DEV, len(FFN_UP_WEIGHTS), up_rows, up_cols)
    for k, nm in enumerate(FFN_UP_WEIGHTS):
        wf[nm] = up_all[:, k].transpose(1, 0, 2).reshape(up_rows, N_DEV * up_cols)
    cat_a = jnp.concatenate([o_sb, o_mem_a], axis=-1).reshape(t, d)
    x1 = _mm(cat_a, wf["a_w_out"], name="a_out", out_dtype=F32, res=x2d)
    xa, ffn_saved_a = _conv_ffn_fwd(x1, gain["a_norm_ffn"], wf["a_ffn_up"], conv_a, wf["a_ffn_down"], "a", b, s)

    (nk, n3), r3 = _rms_fwd(xa, [gain["kv_norm"], gain["b_norm_attn"]], name="b_rms_attn")
    kvsh = _mm(nk, wf["w_kv_shared"], name="kv_shared", out_dtype=BF16).reshape(b, s, -1)
    proj_b = _mm(n3, wf["b_w_in"], name="b_in", out_dtype=BF16).reshape(b, s, -1)
    kvm_b, mem_saved_b = _mem_kv_fwd(mem2d, gain["b_norm_mem"], wf["b_w_mem_kv"], "b", b)
    slopes = _alibi_slopes()
    dil_q, dil_k, dil_v, dil_o, dil_lse, dil_slopes = [], [], [], [], [], []
    for g, (_, dil) in enumerate(DIL_GROUPS):
        qg = _by_residue(proj_b[:, :, GRP * g:GRP * (g + 1)], dil)
        kg = _by_residue(kvsh[:, :, GRP * g:GRP * (g + 1)], dil)
        vg = _by_residue(kvsh[:, :, DIL_WIDTH + GRP * g:DIL_WIDTH + GRP * (g + 1)], dil)
        sl = [slopes[4 * g + h] * dil for h in range(4)]
        og, lg = _attn_fwd(qg, kg, vg, name=f"b_dil{g}_fwd", banded=True, slopes_scaled=sl)
        dil_q.append(qg)
        dil_k.append(kg)
        dil_v.append(vg)
        dil_slopes.append(sl)
        dil_o.append(_from_residue(og, dil, b).reshape(t, GRP))
        dil_lse.append(_from_residue(lg, dil, b).reshape(t, PER_HEAD))
    o_dil, lse_joint = _dil_combine(dil_o, dil_lse, name="b_dil_combine")
    o_mem_b, lse_mem_b = _attn_fwd(proj_b, kvm_b, kvm_b, name="b_mem_fwd", banded=False,
                                   q_lane_blk=qmem_blk_b, k_lane_blk=0, v_lane_blk=1)
    cat_b = jnp.concatenate([o_dil, o_mem_b.reshape(t, MEM_WIDTH)], axis=-1)
    x3 = _mm(cat_b, wf["b_w_out"], name="b_out", out_dtype=F32, res=xa)
    xb, ffn_saved_b = _conv_ffn_fwd(x3, gain["b_norm_ffn"], wf["b_ffn_up"], conv_b, wf["b_ffn_down"], "b", b, s)

    dxb, g_final, loss_vec = _loss_head(xb, gain["final_norm"], tgt2d, name="loss_head")

    grads = {}
    sgrads = {"final_norm": g_final}
    dx3, grads["b_ffn_up"], grads["b_ffn_down"], sgrads["b_ffn_conv"], sgrads["b_norm_ffn"] = _conv_ffn_bwd(
        dxb, x3, gain["b_norm_ffn"], ffn_saved_b, wf["b_ffn_up"], conv_b, wf["b_ffn_down"], "b", b, s)
    dcat_b = _mm(dx3, wf["b_w_out"], name="b_d_cat", out_dtype=BF16, trans_b=True)
    grads["b_w_out"] = _mm(cat_b, dx3, name="b_g_out", out_dtype=BF16, trans_a=True)
    dcat_b3 = dcat_b.reshape(b, s, d)
    delta_mem_b = _attn_delta(dcat_b, cat_b, name="b_mem_delta", lane_blks=[qmem_blk_b]).reshape(b, s, PER_HEAD)
    dq_mem_b, dkm_b, dvm_b = _attn_bwd(proj_b, kvm_b, kvm_b, dcat_b3, lse_mem_b, delta_mem_b, name="b_mem_bwd",
                                       banded=False, q_lane_blk=qmem_blk_b, k_lane_blk=0, v_lane_blk=1,
                                       do_lane_blk=qmem_blk_b)
    delta_dil = _attn_delta(dcat_b, cat_b, name="b_dil_delta", lane_blks=[0, 1, 2]).reshape(b, s, PER_HEAD)
    lse_joint3 = lse_joint.reshape(b, s, PER_HEAD)
    dq_parts, dk_parts, dv_parts = [], [], []
    for g, (_, dil) in enumerate(DIL_GROUPS):
        dog = _by_residue(dcat_b3[:, :, GRP * g:GRP * (g + 1)], dil)
        lg = _by_residue(lse_joint3, dil)
        dg = _by_residue(delta_dil, dil)
        dqg, dkg, dvg = _attn_bwd(dil_q[g], dil_k[g], dil_v[g], dog, lg, dg, name=f"b_dil{g}_bwd", banded=True,
                                  slopes_scaled=dil_slopes[g])
        dq_parts.append(_from_residue(dqg, dil, b))
        dk_parts.append(_from_residue(dkg, dil, b))
        dv_parts.append(_from_residue(dvg, dil, b))
    dproj_b = jnp.concatenate(dq_parts + [dq_mem_b], axis=-1).reshape(t, d)
    dn3 = _mm(dproj_b, wf["b_w_in"], name="b_d_n", out_dtype=F32, trans_b=True)
    grads["b_w_in"] = _mm(n3, dproj_b, name="b_g_in", out_dtype=BF16, trans_a=True)
    grads["b_w_mem_kv"], sgrads["b_norm_mem"] = _mem_kv_bwd(dkm_b, dvm_b, mem2d, gain["b_norm_mem"], mem_saved_b,
                                                           wf["b_w_mem_kv"], "b")
    dkvsh = jnp.concatenate(dk_parts + dv_parts, axis=-1).reshape(t, 2 * DIL_WIDTH).astype(BF16)
    dnk = _mm(dkvsh, wf["w_kv_shared"], name="kv_d_n", out_dtype=F32, trans_b=True)
    grads["w_kv_shared"] = _mm(nk, dkvsh, name="kv_g", out_dtype=BF16, trans_a=True)
    dxa, (sgrads["kv_norm"], sgrads["b_norm_attn"]) = _rms_bwd(
        xa, r3, [(dnk, gain["kv_norm"]), (dn3, gain["b_norm_attn"])], dx3, name="b_rms_attn_bwd")

    dx1, grads["a_ffn_up"], grads["a_ffn_down"], sgrads["a_ffn_conv"], sgrads["a_norm_ffn"] = _conv_ffn_bwd(
        dxa, x1, gain["a_norm_ffn"], ffn_saved_a, wf["a_ffn_up"], conv_a, wf["a_ffn_down"], "a", b, s)
    dcat_a = _mm(dx1, wf["a_w_out"], name="a_d_cat", out_dtype=BF16, trans_b=True)
    grads["a_w_out"] = _mm(cat_a, dx1, name="a_g_out", out_dtype=BF16, trans_a=True)
    dcat_a3 = dcat_a.reshape(b, s, d)
    delta_mem_a = _attn_delta(dcat_a, cat_a, name="a_mem_delta", lane_blks=[qmem_blk_b]).reshape(b, s, PER_HEAD)
    dq_mem_a, dkm_a, dvm_a = _attn_bwd(proj_a, kvm_a, kvm_a, dcat_a3, lse_mem_a, delta_mem_a, name="a_mem_bwd",
                                       banded=False, q_lane_blk=qmem_blk_a, k_lane_blk=0, v_lane_blk=1,
                                       do_lane_blk=qmem_blk_b)
    wide_grads = _pack_grads(grads, WIDE_WEIGHTS, shapes)
    up_grads = jnp.concatenate([grads[nm].reshape(up_rows, N_DEV, up_cols).transpose(1, 0, 2)
                                for nm in FFN_UP_WEIGHTS], axis=1)
    own_wide = lax.dynamic_slice(wide_grads, (me, 0, 0), (1,) + wide_grads.shape[1:])
    own_up = lax.dynamic_slice(up_grads, (me, 0, 0), (1,) + up_grads.shape[1:])
    xw_sems, xw_src, xw_land, xw_token = _exchange_start(wide_grads, lax.empty(wide_grads.shape, BF16),
                                                         name="grads_wide_start", gather=False)
    xu_sems, xu_src, xu_land, xu_token = _exchange_start(up_grads, lax.empty(up_grads.shape, BF16),
                                                         name="grads_up_start", gather=False)
    dq_sb, dk_sb, dv_sb = _sb_bwd(proj_a, dcat_a3, rsum, xw_token + xu_token, name="a_sb_bwd")
    dproj_a = jnp.concatenate([dq_sb, dk_sb, dv_sb, dq_mem_a], axis=-1).reshape(t, -1)
    dn1 = _mm(dproj_a, wf["a_w_in"], name="a_d_n", out_dtype=F32, trans_b=True)
    grads["a_w_in"] = _mm(n1, dproj_a, name="a_g_in", out_dtype=BF16, trans_a=True)
    grads["a_w_mem_kv"], sgrads["a_norm_mem"] = _mem_kv_bwd(dkm_a, dvm_a, mem2d, gain["a_norm_mem"], mem_saved_a,
                                                           wf["a_w_mem_kv"], "a")
    early_grads = _pack_grads(grads, EARLY_WEIGHTS, shapes)
    own_early = lax.dynamic_slice(early_grads, (me, 0, 0), (1,) + early_grads.shape[1:])
    ee_sems, ee_src, ee_land, ee_token = _exchange_start(early_grads, lax.empty(early_grads.shape, BF16),
                                                         name="grads_early_start", gather=False)
    dx0, (sgrads["a_norm_attn"],) = _rms_bwd(x2d, r1, [(dn1, gain["a_norm_attn"])], dx1, name="a_rms_attn_bwd")
    grad_x = dx0.reshape(b, s, d)

    wide_recv = _exchange_wait(xw_sems, xw_src, xw_land, dx0, name="grads_wide_wait", gather=False)
    up_recv = _exchange_wait(xu_sems, xu_src, xu_land, dx0, name="grads_up_wait", gather=False)
    gl = _unpack_local(_sum_blocks(wide_recv, own_wide, ee_token, name="sum_grads_wide"), WIDE_WEIGHTS, shapes)
    up_sum = _sum_blocks(up_recv, own_up, ee_token, name="sum_grads_up")
    for k, nm in enumerate(FFN_UP_WEIGHTS):
        gl[nm] = up_sum[k * up_rows:(k + 1) * up_rows]

    small_names = ["a_norm_attn", "a_norm_mem", "a_norm_ffn", "kv_norm", "b_norm_attn", "b_norm_mem",
                   "b_norm_ffn", "final_norm", "a_ffn_conv", "b_ffn_conv"]
    small_flat = jnp.concatenate([sgrads[nm].reshape(-1) for nm in small_names] + [loss_vec.reshape(-1)])
    n_flat = small_flat.shape[0]
    red_rows = -(-n_flat // (8 * PACK_COLS)) * 8
    small_pack = jnp.pad(small_flat, (0, red_rows * PACK_COLS - n_flat)).reshape(red_rows, PACK_COLS)
    small_sum = _all_reduce_small(small_pack, name="reduce_small").reshape(-1)
    r0 = 0
    for nm in small_names:
        rows, cols = sgrads[nm].shape
        full = small_sum[r0:r0 + rows * cols].reshape(rows, cols)
        r0 += rows * cols
        if nm in sharded_small:
            lc = cols // N_DEV
            gl[nm] = lax.dynamic_slice(full, (0, me * lc), (rows, lc))
        else:
            gl[nm] = full
    loss = (0.5 / d) * jnp.sum(small_sum[r0:r0 + d])

    upd = {}
    for nm in LATE_WEIGHTS:
        upd[nm] = _adam(_as2d(wl[nm]), gl[nm], _as2d(ml[nm]), _as2d(vl[nm]), name=f"adam_{nm}")
    res_small = _adam_small([(_as2d(wl[nm]), gl[nm], _as2d(ml[nm]), _as2d(vl[nm])) for nm in small_names],
                            name="adam_small")
    for nm, r in zip(small_names, res_small):
        upd[nm] = r
    early_recv = _exchange_wait(ee_sems, ee_src, ee_land, upd[LATE_WEIGHTS[-1]][0], name="grads_early_wait",
                                gather=False)
    gl.update(_unpack_local(_sum_blocks(early_recv, own_early, ee_token, name="sum_grads_early"), EARLY_WEIGHTS,
                            shapes))
    for nm in EARLY_WEIGHTS:
        upd[nm] = _adam(_as2d(wl[nm]), gl[nm], _as2d(ml[nm]), _as2d(vl[nm]), name=f"adam_{nm}")

    g_out = [gl[nm].reshape(wl[nm].shape) for nm in names]
    d_out = [upd[nm][0].reshape(wl[nm].shape) for nm in names]
    m_out = [upd[nm][1].reshape(wl[nm].shape) for nm in names]
    v_out = [upd[nm][2].reshape(wl[nm].shape) for nm in names]
    return (loss, grad_x, *g_out, *d_out, *m_out, *v_out)
```
